```python
import jax, jax.numpy as jnp
from jax import lax
import numpy as np

D_MODEL = 1024
BATCH = 8
SEQ = 8192
DEPTH = 4

GRID_W = 64
CTX_LEN = 256
N_MIXERS = 4
GROUP_W = D_MODEL // N_MIXERS
HEAD_DIM = 64
N_Q_HEADS = GROUP_W // HEAD_DIM
N_KV_HEADS = N_Q_HEADS // 2
GQA_GROUP = N_Q_HEADS // N_KV_HEADS
KV_W = N_KV_HEADS * HEAD_DIM
AXIS_DIM = HEAD_DIM // 2
ROPE_THETA = 10000.0
ATTN_SCALE = HEAD_DIM ** -0.5
Q_BLOCK = 128
SHORT_CONV_K = 3
CONFORMER_K = 31
CHUNK = 128
N_SPATIAL_GROUPS = 4
RMS_EPS = 1e-6
LN_EPS = 1e-5

PROJ_WIDTHS = (GROUP_W, GROUP_W, GROUP_W, GROUP_W,
               2 * GROUP_W, GROUP_W,
               GROUP_W, GROUP_W, GROUP_W,
               GROUP_W, KV_W, KV_W, GROUP_W)
PROJ_W = sum(PROJ_WIDTHS)
SPLIT_IDX = tuple(int(i) for i in np.cumsum(PROJ_WIDTHS)[:-1])
KV_START = SPLIT_IDX[9]
KV_END = SPLIT_IDX[11]

kernel_name = "hybrid_parallel_group_dit_trunk"


def rms_norm(x, g):
    xf = x.astype(jnp.float32)
    y = xf * lax.rsqrt(jnp.mean(xf * xf, axis=-1, keepdims=True) + RMS_EPS)
    return (y * g.astype(jnp.float32)).astype(x.dtype)


def layer_norm(x, g, b):
    xf = x.astype(jnp.float32)
    mu = jnp.mean(xf, axis=-1, keepdims=True)
    xc = xf - mu
    y = xc * lax.rsqrt(jnp.mean(xc * xc, axis=-1, keepdims=True) + LN_EPS)
    return (y * g.astype(jnp.float32) + b.astype(jnp.float32)).astype(x.dtype)


def dwconv(x, w):
    k = w.shape[0]
    return lax.conv_general_dilated(
        x, w[:, None, :].astype(x.dtype), window_strides=(1,), padding=[(k // 2, k // 2)],
        dimension_numbers=('NWC', 'WIO', 'NWC'), feature_group_count=x.shape[-1])


def short_conv_mixer(b_gate, c_gate, h, w):
    return b_gate * dwconv(c_gate * h, w)


def conformer_conv_mixer(glu_in, w, bias, ln_g, ln_b):
    a, g = jnp.split(glu_in, 2, axis=-1)
    z = a * jax.nn.sigmoid(g)
    z = dwconv(z, w) + bias
    return jax.nn.silu(layer_norm(z, ln_g, ln_b))


def chunk_mlp_mixer(u, v, ln_g, ln_b, w_s, b_s):
    bsz, s, ch = v.shape
    v = layer_norm(v, ln_g, ln_b)
    vc = v.reshape(bsz, s // CHUNK, CHUNK, N_SPATIAL_GROUPS, ch // N_SPATIAL_GROUPS)
    sg = jnp.einsum('gij,bnjgc->bnigc', w_s.astype(v.dtype), vc) + b_s.T[:, :, None].astype(v.dtype)
    return u * sg.reshape(bsz, s, ch)


def rope_axis(x, cos, sin):
    x1, x2 = jnp.split(x, 2, axis=-1)
    return jnp.concatenate([x1 * cos - x2 * sin, x2 * cos + x1 * sin], axis=-1)


def rope_2d(x, cos_r, sin_r, cos_c, sin_c):
    xr, xc = jnp.split(x, 2, axis=-1)
    return jnp.concatenate([rope_axis(xr, cos_r, sin_r), rope_axis(xc, cos_c, sin_c)], axis=-1)


def split_heads(t, n_heads):
    return t.reshape(t.shape[:-1] + (n_heads, HEAD_DIM))


def attend(q, k, v):
    bsz, nq = q.shape[:2]
    qg = q.reshape(bsz, nq, N_KV_HEADS, GQA_GROUP, HEAD_DIM)
    s = jnp.einsum('bqkgd,btkd->bkgqt', qg, k).astype(jnp.float32) * ATTN_SCALE
    p = jax.nn.softmax(s, axis=-1).astype(v.dtype)
    o = jnp.einsum('bkgqt,btkd->bqkgd', p, v)
    return o.reshape(bsz, nq, N_Q_HEADS * HEAD_DIM)


def attend_blocks(q, k, v):
    bsz, s = q.shape[:2]
    nb = s // Q_BLOCK
    qb = q.reshape(bsz, nb, Q_BLOCK, N_Q_HEADS, HEAD_DIM).swapaxes(0, 1)
    o = lax.map(lambda qi: attend(qi, k, v), qb)
    return o.swapaxes(0, 1).reshape(bsz, s, N_Q_HEADS * HEAD_DIM)


def mixer_output(parts, att, conv_a_l, conv_b_l, conv_b_bias_l, conf_ln_g_l, conf_ln_b_l,
                 sgu_ln_g_l, sgu_ln_b_l, w_s_l, b_s_l, w_out_l):
    a_b, a_c, a_h, a_g, b_glu, b_g, c_u, c_v, c_g = parts[:9]
    d_g = parts[12]
    ya = short_conv_mixer(a_b, a_c, a_h, conv_a_l)
    yb = conformer_conv_mixer(b_glu, conv_b_l, conv_b_bias_l, conf_ln_g_l, conf_ln_b_l)
    yc = chunk_mlp_mixer(c_u, c_v, sgu_ln_g_l, sgu_ln_b_l, w_s_l, b_s_l)
    y = jnp.concatenate([ya * jax.nn.silu(a_g), yb * jax.nn.silu(b_g),
                         yc * jax.nn.silu(c_g), att * jax.nn.silu(d_g)], axis=-1)
    return y @ w_out_l


def _fwd_setup_inputs(seed: int = 0) -> dict:
    key = jax.random.key(seed)
    ks = jax.random.split(key, 24)
    f32 = jnp.float32
    nrm = lambda k, shape, s: jax.random.normal(k, shape, f32) * s
    return {
        "x": nrm(ks[0], (BATCH, SEQ, D_MODEL), 1.0),
        "c": nrm(ks[1], (BATCH, D_MODEL), 1.0),
        "ctx": nrm(ks[2], (BATCH, CTX_LEN, D_MODEL), 1.0),
        "c_ctx": nrm(ks[3], (D_MODEL,), 1.0),
        "w_mod": nrm(ks[4], (DEPTH, D_MODEL, 3 * D_MODEL), D_MODEL ** -0.5),
        "b_mod": nrm(ks[5], (DEPTH, 3 * D_MODEL), 0.02),
        "g_pre": 1.0 + nrm(ks[6], (DEPTH, D_MODEL), 0.02),
        "g_post": 1.0 + nrm(ks[7], (DEPTH, D_MODEL), 0.02),
        "w_in": nrm(ks[8], (DEPTH, D_MODEL, PROJ_W), D_MODEL ** -0.5),
        "w_out": nrm(ks[9], (DEPTH, D_MODEL, D_MODEL), D_MODEL ** -0.5),
        "conv_a": nrm(ks[10], (DEPTH, SHORT_CONV_K, GROUP_W), SHORT_CONV_K ** -0.5),
        "conv_b": nrm(ks[11], (DEPTH, CONFORMER_K, GROUP_W), CONFORMER_K ** -0.5),
        "conv_b_bias": nrm(ks[12], (DEPTH, GROUP_W), 0.02),
        "conf_ln_g": 1.0 + nrm(ks[13], (DEPTH, GROUP_W), 0.02),
        "conf_ln_b": nrm(ks[14], (DEPTH, GROUP_W), 0.02),
        "sgu_ln_g": 1.0 + nrm(ks[15], (DEPTH, GROUP_W), 0.02),
        "sgu_ln_b": nrm(ks[16], (DEPTH, GROUP_W), 0.02),
        "w_s": nrm(ks[17], (DEPTH, N_SPATIAL_GROUPS, CHUNK, CHUNK), CHUNK ** -0.5),
        "b_s": 1.0 + nrm(ks[18], (DEPTH, N_SPATIAL_GROUPS, CHUNK), 0.1),
        "q_gain": 1.0 + nrm(ks[19], (DEPTH, HEAD_DIM), 0.02),
        "k_gain": 1.0 + nrm(ks[20], (DEPTH, HEAD_DIM), 0.02),
    }


def _fwd_reference(x, c, ctx, c_ctx, w_mod, b_mod, g_pre, g_post, w_in, w_out, conv_a, conv_b, conv_b_bias,
              conf_ln_g, conf_ln_b, sgu_ln_g, sgu_ln_b, w_s, b_s, q_gain, k_gain):
    bsz, s, _ = x.shape
    rows = s // GRID_W
    r_idx, c_idx = jnp.meshgrid(jnp.arange(rows), jnp.arange(GRID_W), indexing='ij')
    pos_row = r_idx.reshape(-1).astype(jnp.float32)
    pos_col = c_idx.reshape(-1).astype(jnp.float32)
    inv_freq = 1.0 / (ROPE_THETA ** (jnp.arange(0, AXIS_DIM, 2, dtype=jnp.float32) / AXIS_DIM))
    ang_r = pos_row[:, None] * inv_freq[None, :]
    ang_c = pos_col[:, None] * inv_freq[None, :]
    cos_r = jnp.cos(ang_r)[:, None, :].astype(x.dtype)
    sin_r = jnp.sin(ang_r)[:, None, :].astype(x.dtype)
    cos_c = jnp.cos(ang_c)[:, None, :].astype(x.dtype)
    sin_c = jnp.sin(ang_c)[:, None, :].astype(x.dtype)

    xc = ctx
    for l in range(DEPTH):
        last = l == DEPTH - 1
        mod_l = (jax.nn.silu(c) @ w_mod[l] + b_mod[l])[:, None, :]
        mod_c = (jax.nn.silu(c_ctx) @ w_mod[l] + b_mod[l])[None, None, :]
        sh, sc, gt = jnp.split(mod_l, 3, axis=-1)
        sh_c, sc_c, gt_c = jnp.split(mod_c, 3, axis=-1)
        h = rms_norm(x, g_pre[l]) * (1.0 + sc) + sh
        hc = rms_norm(xc, g_pre[l]) * (1.0 + sc_c) + sh_c

        parts = jnp.split(h @ w_in[l], SPLIT_IDX, axis=-1)
        if last:
            k_c, v_c = jnp.split(hc @ w_in[l][:, KV_START:KV_END], 2, axis=-1)
            parts_c = None
        else:
            parts_c = jnp.split(hc @ w_in[l], SPLIT_IDX, axis=-1)
            k_c, v_c = parts_c[10], parts_c[11]
        k_c = rms_norm(split_heads(k_c, N_KV_HEADS), k_gain[l])
        v_c = split_heads(v_c, N_KV_HEADS)

        q = rope_2d(rms_norm(split_heads(parts[9], N_Q_HEADS), q_gain[l]), cos_r, sin_r, cos_c, sin_c)
        k = rope_2d(rms_norm(split_heads(parts[10], N_KV_HEADS), k_gain[l]), cos_r, sin_r, cos_c, sin_c)
        v = split_heads(parts[11], N_KV_HEADS)
        k_all = jnp.concatenate([k_c, k], axis=1)
        v_all = jnp.concatenate([v_c, v], axis=1)
        att = attend_blocks(q, k_all, v_all)

        y = mixer_output(parts, att, conv_a[l], conv_b[l], conv_b_bias[l], conf_ln_g[l], conf_ln_b[l],
                         sgu_ln_g[l], sgu_ln_b[l], w_s[l], b_s[l], w_out[l])

        if not last:
            q_c = rms_norm(split_heads(parts_c[9], N_Q_HEADS), q_gain[l])
            att_c = attend(q_c, k_c, v_c)
            y_c = mixer_output(parts_c, att_c, conv_a[l], conv_b[l], conv_b_bias[l], conf_ln_g[l], conf_ln_b[l],
                               sgu_ln_g[l], sgu_ln_b[l], w_s[l], b_s[l], w_out[l])
            xc = xc + gt_c * rms_norm(y_c, g_post[l])

        x = x + gt * rms_norm(y, g_post[l])
    return x


import jax as _jax
import jax.numpy as _jnp

TWIN_FORMAT = 'train_step'
FWD_PARAMS = ['x', 'c', 'ctx', 'c_ctx', 'w_mod', 'b_mod', 'g_pre', 'g_post', 'w_in', 'w_out', 'conv_a', 'conv_b', 'conv_b_bias', 'conf_ln_g', 'conf_ln_b', 'sgu_ln_g', 'sgu_ln_b', 'w_s', 'b_s', 'q_gain', 'k_gain']
TWIN_WEIGHTS = ['c_ctx', 'w_mod', 'b_mod', 'g_pre', 'g_post', 'w_in', 'w_out', 'conv_a', 'conv_b', 'conv_b_bias', 'conf_ln_g', 'conf_ln_b', 'sgu_ln_g', 'sgu_ln_b', 'w_s', 'b_s', 'q_gain', 'k_gain']
TWIN_DIFF_INPUT = 'x'
TWIN_INPUTS = ['x', 'c', 'ctx', 'c_ctx', 'w_mod', 'b_mod', 'g_pre', 'g_post', 'w_in', 'w_out', 'conv_a', 'conv_b', 'conv_b_bias', 'conf_ln_g', 'conf_ln_b', 'sgu_ln_g', 'sgu_ln_b', 'w_s', 'b_s', 'q_gain', 'k_gain', 'loss_target', 'm_c_ctx', 'm_w_mod', 'm_b_mod', 'm_g_pre', 'm_g_post', 'm_w_in', 'm_w_out', 'm_conv_a', 'm_conv_b', 'm_conv_b_bias', 'm_conf_ln_g', 'm_conf_ln_b', 'm_sgu_ln_g', 'm_sgu_ln_b', 'm_w_s', 'm_b_s', 'm_q_gain', 'm_k_gain', 'v_c_ctx', 'v_w_mod', 'v_b_mod', 'v_g_pre', 'v_g_post', 'v_w_in', 'v_w_out', 'v_conv_a', 'v_conv_b', 'v_conv_b_bias', 'v_conf_ln_g', 'v_conf_ln_b', 'v_sgu_ln_g', 'v_sgu_ln_b', 'v_w_s', 'v_b_s', 'v_q_gain', 'v_k_gain']
TWIN_OUTPUTS = ['loss', 'grad_x', 'grad_c_ctx', 'grad_w_mod', 'grad_b_mod', 'grad_g_pre', 'grad_g_post', 'grad_w_in', 'grad_w_out', 'grad_conv_a', 'grad_conv_b', 'grad_conv_b_bias', 'grad_conf_ln_g', 'grad_conf_ln_b', 'grad_sgu_ln_g', 'grad_sgu_ln_b', 'grad_w_s', 'grad_b_s', 'grad_q_gain', 'grad_k_gain', 'delta_c_ctx', 'delta_w_mod', 'delta_b_mod', 'delta_g_pre', 'delta_g_post', 'delta_w_in', 'delta_w_out', 'delta_conv_a', 'delta_conv_b', 'delta_conv_b_bias', 'delta_conf_ln_g', 'delta_conf_ln_b', 'delta_sgu_ln_g', 'delta_sgu_ln_b', 'delta_w_s', 'delta_b_s', 'delta_q_gain', 'delta_k_gain', 'new_m_c_ctx', 'new_m_w_mod', 'new_m_b_mod', 'new_m_g_pre', 'new_m_g_post', 'new_m_w_in', 'new_m_w_out', 'new_m_conv_a', 'new_m_conv_b', 'new_m_conv_b_bias', 'new_m_conf_ln_g', 'new_m_conf_ln_b', 'new_m_sgu_ln_g', 'new_m_sgu_ln_b', 'new_m_w_s', 'new_m_b_s', 'new_m_q_gain', 'new_m_k_gain', 'new_v_c_ctx', 'new_v_w_mod', 'new_v_b_mod', 'new_v_g_pre', 'new_v_g_post', 'new_v_w_in', 'new_v_w_out', 'new_v_conv_a', 'new_v_conv_b', 'new_v_conv_b_bias', 'new_v_conf_ln_g', 'new_v_conf_ln_b', 'new_v_sgu_ln_g', 'new_v_sgu_ln_b', 'new_v_w_s', 'new_v_b_s', 'new_v_q_gain', 'new_v_k_gain']
TWIN_LEAF_KINDS = {'loss': 'loss', 'grad_x': 'grad_x', 'grad_c_ctx': 'grad_w', 'grad_w_mod': 'grad_w', 'grad_b_mod': 'grad_w', 'grad_g_pre': 'grad_w', 'grad_g_post': 'grad_w', 'grad_w_in': 'grad_w', 'grad_w_out': 'grad_w', 'grad_conv_a': 'grad_w', 'grad_conv_b': 'grad_w', 'grad_conv_b_bias': 'grad_w', 'grad_conf_ln_g': 'grad_w', 'grad_conf_ln_b': 'grad_w', 'grad_sgu_ln_g': 'grad_w', 'grad_sgu_ln_b': 'grad_w', 'grad_w_s': 'grad_w', 'grad_b_s': 'grad_w', 'grad_q_gain': 'grad_w', 'grad_k_gain': 'grad_w', 'delta_c_ctx': 'delta_w', 'delta_w_mod': 'delta_w', 'delta_b_mod': 'delta_w', 'delta_g_pre': 'delta_w', 'delta_g_post': 'delta_w', 'delta_w_in': 'delta_w', 'delta_w_out': 'delta_w', 'delta_conv_a': 'delta_w', 'delta_conv_b': 'delta_w', 'delta_conv_b_bias': 'delta_w', 'delta_conf_ln_g': 'delta_w', 'delta_conf_ln_b': 'delta_w', 'delta_sgu_ln_g': 'delta_w', 'delta_sgu_ln_b': 'delta_w', 'delta_w_s': 'delta_w', 'delta_b_s': 'delta_w', 'delta_q_gain': 'delta_w', 'delta_k_gain': 'delta_w', 'new_m_c_ctx': 'new_m', 'new_m_w_mod': 'new_m', 'new_m_b_mod': 'new_m', 'new_m_g_pre': 'new_m', 'new_m_g_post': 'new_m', 'new_m_w_in': 'new_m', 'new_m_w_out': 'new_m', 'new_m_conv_a': 'new_m', 'new_m_conv_b': 'new_m', 'new_m_conv_b_bias': 'new_m', 'new_m_conf_ln_g': 'new_m', 'new_m_conf_ln_b': 'new_m', 'new_m_sgu_ln_g': 'new_m', 'new_m_sgu_ln_b': 'new_m', 'new_m_w_s': 'new_m', 'new_m_b_s': 'new_m', 'new_m_q_gain': 'new_m', 'new_m_k_gain': 'new_m', 'new_v_c_ctx': 'new_v', 'new_v_w_mod': 'new_v', 'new_v_b_mod': 'new_v', 'new_v_g_pre': 'new_v', 'new_v_g_post': 'new_v', 'new_v_w_in': 'new_v', 'new_v_w_out': 'new_v', 'new_v_conv_a': 'new_v', 'new_v_conv_b': 'new_v', 'new_v_conv_b_bias': 'new_v', 'new_v_conf_ln_g': 'new_v', 'new_v_conf_ln_b': 'new_v', 'new_v_sgu_ln_g': 'new_v', 'new_v_sgu_ln_b': 'new_v', 'new_v_w_s': 'new_v', 'new_v_b_s': 'new_v', 'new_v_q_gain': 'new_v', 'new_v_k_gain': 'new_v'}


def _forward(args):
    return _fwd_reference(*[args[k] for k in FWD_PARAMS])


def _output_shape():
    def fwd():
        inp = _fwd_setup_inputs(0)
        return _fwd_reference(*[inp[k] for k in FWD_PARAMS])
    out = _jax.eval_shape(fwd)
    return out.shape, out.dtype

N_MICROBATCH = 1
ADAM_LR = 0.001
ADAM_B1 = 0.9
ADAM_B2 = 0.999
ADAM_EPS = 1e-08
ADAM_WD = 0.01
ADAM_STEP = 10
PER_EXAMPLE_BATCH_AXIS = {'x': 0, 'c': 0, 'ctx': 0, 'loss_target': 0}
SHARED_INPUTS = []
_WEIGHT_DTYPES = {'c_ctx': _jnp.float32, 'w_mod': _jnp.float32, 'b_mod': _jnp.float32, 'g_pre': _jnp.float32, 'g_post': _jnp.float32, 'w_in': _jnp.float32, 'w_out': _jnp.float32, 'conv_a': _jnp.float32, 'conv_b': _jnp.float32, 'conv_b_bias': _jnp.float32, 'conf_ln_g': _jnp.float32, 'conf_ln_b': _jnp.float32, 'sgu_ln_g': _jnp.float32, 'sgu_ln_b': _jnp.float32, 'w_s': _jnp.float32, 'b_s': _jnp.float32, 'q_gain': _jnp.float32, 'k_gain': _jnp.float32}
MOMENT_SCALE = {'c_ctx': 1.397839e-01, 'w_mod': 4.736659e+00, 'b_mod': 1.021755e+01, 'g_pre': 7.492172e-01, 'g_post': 2.658647e+01, 'w_in': 5.335071e-01, 'w_out': 7.367326e-01, 'conv_a': 7.869637e-01, 'conv_b': 2.412936e-01, 'conv_b_bias': 6.154854e-01, 'conf_ln_g': 3.979811e-01, 'conf_ln_b': 4.564817e-01, 'sgu_ln_g': 5.156575e-01, 'sgu_ln_b': 3.940449e-01, 'w_s': 2.891482e-01, 'b_s': 2.809734e-01, 'q_gain': 8.939947e-02, 'k_gain': 9.275691e-02}


def _to_microbatches(a, axis):
    t = _jnp.moveaxis(a, axis, 0)
    t = t.reshape((N_MICROBATCH, t.shape[0] // N_MICROBATCH) + t.shape[1:])
    return _jnp.moveaxis(t, 1, axis + 1)


def setup_inputs(seed: int = 0) -> dict:
    inp = _fwd_setup_inputs(seed)
    key = _jax.random.fold_in(_jax.random.key(seed), 7919)
    shape, _ = _output_shape()
    out = dict(inp)
    out["loss_target"] = _jax.random.normal(_jax.random.fold_in(key, 0), shape, _jnp.float32)
    for i, name in enumerate(TWIN_WEIGHTS):
        w = inp[name].astype(_jnp.float32)
        if MOMENT_SCALE is None:
            s = _jnp.sqrt(_jnp.mean(_jnp.square(w)) + 1e-30)
        else:
            s = MOMENT_SCALE[name]
        km, kv = _jax.random.split(_jax.random.fold_in(key, i + 1))
        out[name] = w
        out["m_" + name] = s * _jax.random.normal(km, w.shape, _jnp.float32)
        out["v_" + name] = (s * s) * _jax.random.uniform(kv, w.shape, _jnp.float32, 0.5, 1.5)
    if N_MICROBATCH > 1:
        for name, axis in PER_EXAMPLE_BATCH_AXIS.items():
            out[name] = _to_microbatches(out[name], axis)
    return {'x': out['x'], 'c': out['c'], 'ctx': out['ctx'], 'c_ctx': out['c_ctx'], 'w_mod': out['w_mod'], 'b_mod': out['b_mod'], 'g_pre': out['g_pre'], 'g_post': out['g_post'], 'w_in': out['w_in'], 'w_out': out['w_out'], 'conv_a': out['conv_a'], 'conv_b': out['conv_b'], 'conv_b_bias': out['conv_b_bias'], 'conf_ln_g': out['conf_ln_g'], 'conf_ln_b': out['conf_ln_b'], 'sgu_ln_g': out['sgu_ln_g'], 'sgu_ln_b': out['sgu_ln_b'], 'w_s': out['w_s'], 'b_s': out['b_s'], 'q_gain': out['q_gain'], 'k_gain': out['k_gain'], 'loss_target': out['loss_target'], 'm_c_ctx': out['m_c_ctx'], 'm_w_mod': out['m_w_mod'], 'm_b_mod': out['m_b_mod'], 'm_g_pre': out['m_g_pre'], 'm_g_post': out['m_g_post'], 'm_w_in': out['m_w_in'], 'm_w_out': out['m_w_out'], 'm_conv_a': out['m_conv_a'], 'm_conv_b': out['m_conv_b'], 'm_conv_b_bias': out['m_conv_b_bias'], 'm_conf_ln_g': out['m_conf_ln_g'], 'm_conf_ln_b': out['m_conf_ln_b'], 'm_sgu_ln_g': out['m_sgu_ln_g'], 'm_sgu_ln_b': out['m_sgu_ln_b'], 'm_w_s': out['m_w_s'], 'm_b_s': out['m_b_s'], 'm_q_gain': out['m_q_gain'], 'm_k_gain': out['m_k_gain'], 'v_c_ctx': out['v_c_ctx'], 'v_w_mod': out['v_w_mod'], 'v_b_mod': out['v_b_mod'], 'v_g_pre': out['v_g_pre'], 'v_g_post': out['v_g_post'], 'v_w_in': out['v_w_in'], 'v_w_out': out['v_w_out'], 'v_conv_a': out['v_conv_a'], 'v_conv_b': out['v_conv_b'], 'v_conv_b_bias': out['v_conv_b_bias'], 'v_conf_ln_g': out['v_conf_ln_g'], 'v_conf_ln_b': out['v_conf_ln_b'], 'v_sgu_ln_g': out['v_sgu_ln_g'], 'v_sgu_ln_b': out['v_sgu_ln_b'], 'v_w_s': out['v_w_s'], 'v_b_s': out['v_b_s'], 'v_q_gain': out['v_q_gain'], 'v_k_gain': out['v_k_gain']}


def _loss(weights, diff, rest, loss_target):
    with _jax.named_scope("forward"):
        args = {**rest, TWIN_DIFF_INPUT: diff, **{k: w.astype(_WEIGHT_DTYPES[k]) for k, w in weights.items()}}
        y = _forward(args)
    with _jax.named_scope("loss_head"):
        err = _jnp.square(y.astype(_jnp.float32) - loss_target)
        return 0.5 * _jnp.sum(_jnp.mean(err, axis=-1)) if err.ndim else 0.5 * err


def _adamw(w, g, m, v):
    m = ADAM_B1 * m + (1.0 - ADAM_B1) * g
    v = ADAM_B2 * v + (1.0 - ADAM_B2) * _jnp.square(g)
    m_hat = m / (1.0 - ADAM_B1 ** ADAM_STEP)
    v_hat = v / (1.0 - ADAM_B2 ** ADAM_STEP)
    delta = -ADAM_LR * (m_hat / (_jnp.sqrt(v_hat) + ADAM_EPS) + ADAM_WD * w)
    return delta, m, v


def reference(x, c, ctx, c_ctx, w_mod, b_mod, g_pre, g_post, w_in, w_out, conv_a, conv_b, conv_b_bias, conf_ln_g, conf_ln_b, sgu_ln_g, sgu_ln_b, w_s, b_s, q_gain, k_gain, loss_target, m_c_ctx, m_w_mod, m_b_mod, m_g_pre, m_g_post, m_w_in, m_w_out, m_conv_a, m_conv_b, m_conv_b_bias, m_conf_ln_g, m_conf_ln_b, m_sgu_ln_g, m_sgu_ln_b, m_w_s, m_b_s, m_q_gain, m_k_gain, v_c_ctx, v_w_mod, v_b_mod, v_g_pre, v_g_post, v_w_in, v_w_out, v_conv_a, v_conv_b, v_conv_b_bias, v_conf_ln_g, v_conf_ln_b, v_sgu_ln_g, v_sgu_ln_b, v_w_s, v_b_s, v_q_gain, v_k_gain):
    given = dict(x=x, c=c, ctx=ctx, c_ctx=c_ctx, w_mod=w_mod, b_mod=b_mod, g_pre=g_pre, g_post=g_post, w_in=w_in, w_out=w_out, conv_a=conv_a, conv_b=conv_b, conv_b_bias=conv_b_bias, conf_ln_g=conf_ln_g, conf_ln_b=conf_ln_b, sgu_ln_g=sgu_ln_g, sgu_ln_b=sgu_ln_b, w_s=w_s, b_s=b_s, q_gain=q_gain, k_gain=k_gain, loss_target=loss_target, m_c_ctx=m_c_ctx, m_w_mod=m_w_mod, m_b_mod=m_b_mod, m_g_pre=m_g_pre, m_g_post=m_g_post, m_w_in=m_w_in, m_w_out=m_w_out, m_conv_a=m_conv_a, m_conv_b=m_conv_b, m_conv_b_bias=m_conv_b_bias, m_conf_ln_g=m_conf_ln_g, m_conf_ln_b=m_conf_ln_b, m_sgu_ln_g=m_sgu_ln_g, m_sgu_ln_b=m_sgu_ln_b, m_w_s=m_w_s, m_b_s=m_b_s, m_q_gain=m_q_gain, m_k_gain=m_k_gain, v_c_ctx=v_c_ctx, v_w_mod=v_w_mod, v_b_mod=v_b_mod, v_g_pre=v_g_pre, v_g_post=v_g_post, v_w_in=v_w_in, v_w_out=v_w_out, v_conv_a=v_conv_a, v_conv_b=v_conv_b, v_conv_b_bias=v_conv_b_bias, v_conf_ln_g=v_conf_ln_g, v_conf_ln_b=v_conf_ln_b, v_sgu_ln_g=v_sgu_ln_g, v_sgu_ln_b=v_sgu_ln_b, v_w_s=v_w_s, v_b_s=v_b_s, v_q_gain=v_q_gain, v_k_gain=v_k_gain)
    weights = {n: given[n] for n in TWIN_WEIGHTS}
    shared = {n: given[n] for n in SHARED_INPUTS}
    per_example = {n: given[n] for n in ['x', 'c', 'ctx']}
    grad_fn = _jax.value_and_grad(_loss, argnums=(0, 1))

    def one_microbatch(ex, loss_target):
        ex = dict(ex)
        diff = ex.pop(TWIN_DIFF_INPUT)
        return grad_fn(weights, diff, {**shared, **ex}, loss_target)

    if N_MICROBATCH == 1:
        loss, (grad_w, grad_x) = one_microbatch(per_example, given["loss_target"])
    else:
        def body(carry, xs):
            loss_sum, grad_sum = carry
            l_k, (gw_k, gx_k) = one_microbatch(xs[0], xs[1])
            with _jax.named_scope("update"):
                return (loss_sum + l_k, _jax.tree.map(_jnp.add, grad_sum, gw_k)), gx_k

        init = (_jnp.zeros((), _jnp.float32), _jax.tree.map(_jnp.zeros_like, weights))
        (loss, grad_w), grad_x = _jax.lax.scan(body, init, (per_example, given["loss_target"]))
    with _jax.named_scope("update"):
        delta_w, new_m, new_v = {}, {}, {}
        for n in TWIN_WEIGHTS:
            delta_w[n], new_m[n], new_v[n] = _adamw(weights[n], grad_w[n], given["m_" + n], given["v_" + n])
    return (loss, grad_x, *[grad_w[n] for n in TWIN_WEIGHTS], *[delta_w[n] for n in TWIN_WEIGHTS],
            *[new_m[n] for n in TWIN_WEIGHTS], *[new_v[n] for n in TWIN_WEIGHTS])
```

```python
import functools

import numpy as np
import jax
import jax.numpy as jnp
from jax import lax
from jax.experimental import pallas as pl
from jax.experimental.pallas import tpu as pltpu

F32 = jnp.float32
BF16 = jnp.bfloat16
MESH = pl.DeviceIdType.MESH

D_MODEL = 1024
DEPTH = 4
GRID_W = 64
CTX_LEN = 256
GROUP_W = 256
HEAD_DIM = 64
N_Q_HEADS = 4
N_KV_HEADS = 2
ROPE_THETA = 10000.0
ATTN_SCALE = HEAD_DIM ** -0.5
SHORT_CONV_K = 3
CONFORMER_K = 31
CHUNK = 128
N_SPATIAL_GROUPS = 4
RMS_EPS = 1e-6
LN_EPS = 1e-5
ADAM_LR = 0.001
ADAM_B1 = 0.9
ADAM_B2 = 0.999
ADAM_EPS = 1e-08
ADAM_WD = 0.01
ADAM_STEP = 10

LANES = 128
HALO = 16
TM = 256
N_CTX_TILES = CTX_LEN // TM
W_C = 1024
W_R = 1792
W_Q = 512
PROJ_W = W_C + W_R + W_Q
N_CHIPS = 4
SHARD_IN = PROJ_W // N_CHIPS
SHARD_OUT = D_MODEL // N_CHIPS
SHARD_MOD = 3 * D_MODEL // N_CHIPS
VMEM_LIMIT = 56 * 1024 * 1024


def _pc(body, **kw):
    return pl.pallas_call(body, **kw)


def _cparams(**kw):
    return pltpu.CompilerParams(dimension_semantics=("arbitrary",), vmem_limit_bytes=VMEM_LIMIT, **kw)


def _full(shape):
    n = len(shape)
    return pl.BlockSpec(shape, lambda i: (0,) * n)


def _rows(width, tm=TM):
    return pl.BlockSpec((tm, width), lambda i: (i, 0))


def _heads(nh, width, tm=TM):
    return pl.BlockSpec((nh, tm, width), lambda i: (0, i, 0))


def _sigmoid(x):
    return jax.nn.sigmoid(x)


def _dot(a, b):
    return jnp.dot(a, b, preferred_element_type=F32)


def _dot_nt(a, b):
    return lax.dot_general(a, b, (((1,), (1,)), ((), ())), preferred_element_type=F32)


def _dot_tn(a, b):
    return lax.dot_general(a, b, (((0,), (0,)), ((), ())), preferred_element_type=F32)


def _lane(rows):
    return lax.broadcasted_iota(jnp.int32, (rows, LANES), 1)


def _rowsum(x):
    return jnp.sum(x, axis=1, keepdims=True)


def _colsum(x):
    return jnp.sum(x, axis=0, keepdims=True)


def _pair_sums(x, lo):
    s0 = _rowsum(jnp.where(lo, x, 0.0))
    s1 = _rowsum(jnp.where(lo, 0.0, x))
    return jnp.where(lo, s0, s1)


def _swap16(x, lo16):
    return jnp.where(lo16, pltpu.roll(x, LANES - 16, 1), pltpu.roll(x, 16, 1))


def _layer_norm_stats(x):
    mu = jnp.mean(x, axis=1, keepdims=True)
    xc = x - mu
    rs = lax.rsqrt(jnp.mean(xc * xc, axis=1, keepdims=True) + LN_EPS)
    return xc * rs, rs


def _layer_norm_bwd(dxn, xn, rs):
    return rs * (dxn - jnp.mean(dxn, axis=1, keepdims=True) - xn * jnp.mean(dxn * xn, axis=1, keepdims=True))


def _group_select(r, grp):
    out = jnp.where(grp == 0, r[0:CHUNK], 0.0)
    for g in range(1, N_SPATIAL_GROUPS):
        out = out + jnp.where(grp == g, r[g * CHUNK:(g + 1) * CHUNK], 0.0)
    return out


def _kv_chunk(s_lat):
    return 1024 if s_lat % 1024 == 0 else 256


def _all_gather_rows(x_shard):
    m_per, n = x_shard.shape

    def body(x_ref, out_ref, send_sems, recv_sems, local_sem):
        x, y, c = lax.axis_index("x"), lax.axis_index("y"), lax.axis_index("c")
        me, sibling = (x, y, c), (x, y, 1 - c)
        chips = [(1 - x, y), (x, 1 - y), (1 - x, 1 - y)]

        def rows(px, py, pc):
            return out_ref.at[pl.ds((4 * px + 2 * py + pc) * m_per, m_per), :]

        def copy(k, block, to, src=None):
            return pltpu.make_async_remote_copy(
                src_ref=rows(*block) if src is None else src, dst_ref=rows(*block),
                send_sem=send_sems.at[k], recv_sem=recv_sems.at[k], device_id=to, device_id_type=MESH)

        mine = pltpu.make_async_copy(x_ref, rows(*me), local_sem)
        mine.start()
        first = [copy(0, me, sibling, src=x_ref)]
        first += [copy(1 + j, me, (*chip, c), src=x_ref) for j, chip in enumerate(chips)]
        for cp in first:
            cp.start()
        passed = [copy(4 + j, (*chip, c), sibling) for j, chip in enumerate(chips)]
        for j, chip in enumerate(chips):
            copy(1 + j, (*chip, c), me).wait_recv()
            passed[j].start()
        copy(0, sibling, me).wait_recv()
        for j, chip in enumerate(chips):
            copy(4 + j, (*chip, 1 - c), me).wait_recv()
        for cp in first + passed:
            cp.wait_send()
        mine.wait()

    return _pc(
        body, name="all_gather_rows",
        out_shape=jax.ShapeDtypeStruct((8 * m_per, n), x_shard.dtype),
        in_specs=[pl.BlockSpec(memory_space=pltpu.VMEM)],
        out_specs=pl.BlockSpec(memory_space=pltpu.VMEM),
        scratch_shapes=[pltpu.SemaphoreType.DMA((7,)), pltpu.SemaphoreType.DMA((7,)), pltpu.SemaphoreType.DMA],
        compiler_params=pltpu.CompilerParams(vmem_limit_bytes=VMEM_LIMIT),
    )(x_shard)


def _gather_weights(wi, wo):
    half = DEPTH // 2

    def body(wi_ref, wo_ref, gi_ref, go_ref, send_sems, recv_sems, local_sems):
        x, y, c = lax.axis_index("x"), lax.axis_index("y"), lax.axis_index("c")
        kme = 2 * x + y
        sibling = (x, y, 1 - c)
        chips = [(1 - x, y), (x, 1 - y), (1 - x, 1 - y)]
        mine = pl.ds(c * half, half)
        theirs = pl.ds((1 - c) * half, half)
        sends, locals_ = [], []
        for a, (src, dst) in enumerate(((wi_ref, gi_ref), (wo_ref, go_ref))):
            loc = pltpu.make_async_copy(src, dst.at[kme], local_sems.at[a])
            loc.start()
            locals_.append(loc)
            for j, (px, py) in enumerate(chips):
                cp = pltpu.make_async_remote_copy(
                    src_ref=src.at[mine], dst_ref=dst.at[kme, mine],
                    send_sem=send_sems.at[6 * a + j], recv_sem=recv_sems.at[6 * a + j],
                    device_id=(px, py, c), device_id_type=MESH)
                cp.start()
                sends.append(cp)
        for a, (src, dst) in enumerate(((wi_ref, gi_ref), (wo_ref, go_ref))):
            for j, (px, py) in enumerate(chips):
                kk = 2 * px + py
                landed = dst.at[kk, mine]
                pltpu.make_async_remote_copy(
                    src_ref=landed, dst_ref=landed, send_sem=send_sems.at[6 * a + j], recv_sem=recv_sems.at[6 * a + j],
                    device_id=(px, py, c), device_id_type=MESH).wait_recv()
                fw = pltpu.make_async_remote_copy(
                    src_ref=landed, dst_ref=landed, send_sem=send_sems.at[6 * a + 3 + j], recv_sem=recv_sems.at[6 * a + 3 + j],
                    device_id=sibling, device_id_type=MESH)
                fw.start()
                sends.append(fw)
        for a, (src, dst) in enumerate(((wi_ref, gi_ref), (wo_ref, go_ref))):
            for j, (px, py) in enumerate(chips):
                kk = 2 * px + py
                other = dst.at[kk, theirs]
                pltpu.make_async_remote_copy(
                    src_ref=other, dst_ref=other, send_sem=send_sems.at[6 * a + 3 + j], recv_sem=recv_sems.at[6 * a + 3 + j],
                    device_id=sibling, device_id_type=MESH).wait_recv()
        for cp in sends:
            cp.wait_send()
        for loc in locals_:
            loc.wait()

    hbm = pl.BlockSpec(memory_space=pl.ANY)
    return _pc(
        body, name="gather_weights",
        out_shape=(jax.ShapeDtypeStruct((N_CHIPS,) + wi.shape, wi.dtype), jax.ShapeDtypeStruct((N_CHIPS,) + wo.shape, wo.dtype)),
        in_specs=[hbm, hbm], out_specs=(hbm, hbm),
        scratch_shapes=[pltpu.SemaphoreType.DMA((12,)), pltpu.SemaphoreType.DMA((12,)), pltpu.SemaphoreType.DMA((2,))],
    )(wi, wo)


def _scatter_slabs(gi, go):
    def body(gi_ref, go_ref, ri_ref, ro_ref, send_sems, recv_sems, local_sems):
        x, y, c = lax.axis_index("x"), lax.axis_index("y"), lax.axis_index("c")
        kme = 2 * x + y
        chips = [(1 - x, y), (x, 1 - y), (1 - x, 1 - y)]
        sends, locals_ = [], []
        for a, (src, dst) in enumerate(((gi_ref, ri_ref), (go_ref, ro_ref))):
            loc = pltpu.make_async_copy(src.at[kme], dst.at[kme], local_sems.at[a])
            loc.start()
            locals_.append(loc)
            for j, (px, py) in enumerate(chips):
                cp = pltpu.make_async_remote_copy(
                    src_ref=src.at[2 * px + py], dst_ref=dst.at[kme],
                    send_sem=send_sems.at[3 * a + j], recv_sem=recv_sems.at[3 * a + j],
                    device_id=(px, py, c), device_id_type=MESH)
                cp.start()
                sends.append(cp)
        for a, (src, dst) in enumerate(((gi_ref, ri_ref), (go_ref, ro_ref))):
            for j, (px, py) in enumerate(chips):
                landed = dst.at[2 * px + py]
                pltpu.make_async_remote_copy(
                    src_ref=landed, dst_ref=landed, send_sem=send_sems.at[3 * a + j], recv_sem=recv_sems.at[3 * a + j],
                    device_id=(px, py, c), device_id_type=MESH).wait_recv()
        for cp in sends:
            cp.wait_send()
        for loc in locals_:
            loc.wait()

    hbm = pl.BlockSpec(memory_space=pl.ANY)
    return _pc(
        body, name="scatter_slabs",
        out_shape=(jax.ShapeDtypeStruct(gi.shape, gi.dtype), jax.ShapeDtypeStruct(go.shape, go.dtype)),
        in_specs=[hbm, hbm], out_specs=(hbm, hbm),
        scratch_shapes=[pltpu.SemaphoreType.DMA((6,)), pltpu.SemaphoreType.DMA((6,)), pltpu.SemaphoreType.DMA((2,))],
    )(gi, go)


def _swap_with_sibling(a, b):
    def body(a_ref, b_ref, ra_ref, rb_ref, send_sems, recv_sems):
        x, y, c = lax.axis_index("x"), lax.axis_index("y"), lax.axis_index("c")
        copies = []
        for k, (src, dst) in enumerate(((a_ref, ra_ref), (b_ref, rb_ref))):
            cp = pltpu.make_async_remote_copy(
                src_ref=src, dst_ref=dst, send_sem=send_sems.at[k], recv_sem=recv_sems.at[k],
                device_id=(x, y, 1 - c), device_id_type=MESH)
            cp.start()
            copies.append(cp)
        for cp in copies:
            cp.wait()

    hbm = pl.BlockSpec(memory_space=pl.ANY)
    return _pc(
        body, name="swap_with_sibling",
        out_shape=(jax.ShapeDtypeStruct(a.shape, a.dtype), jax.ShapeDtypeStruct(b.shape, b.dtype)),
        in_specs=[hbm, hbm], out_specs=(hbm, hbm),
        scratch_shapes=[pltpu.SemaphoreType.DMA((2,)), pltpu.SemaphoreType.DMA((2,))],
    )(a, b)


def _mod_forward(c16, w_mod, b_mod_shard):
    def body(c_ref, w_ref, b_ref, s_ref, o_ref):
        cc = c_ref[...]
        s = cc * _sigmoid(cc)
        s_ref[...] = s
        o_ref[0] = jnp.dot(s, w_ref[0], preferred_element_type=F32, precision=lax.Precision.HIGHEST) + b_ref[0]

    return _pc(
        body, name="mod_forward", grid=(DEPTH,),
        out_shape=(jax.ShapeDtypeStruct((16, D_MODEL), F32), jax.ShapeDtypeStruct((DEPTH, 16, SHARD_MOD), F32)),
        in_specs=[_full((16, D_MODEL)),
                  pl.BlockSpec((1, D_MODEL, SHARD_MOD), lambda l: (l, 0, 0)),
                  pl.BlockSpec((1, 1, SHARD_MOD), lambda l: (l, 0, 0))],
        out_specs=(_full((16, D_MODEL)), pl.BlockSpec((1, 16, SHARD_MOD), lambda l: (l, 0, 0))),
        compiler_params=_cparams(),
    )(c16, w_mod, b_mod_shard)


def _mod_backward(s_t, g_rows, g_ctx, d_all, w_mod, c_ctx_col):
    def body(st_ref, g_ref, gc_ref, d_ref, w_ref, cc_ref, gw_ref, gb_ref, pc_ref):
        l = pl.program_id(0)
        gw_ref[0] = jnp.dot(st_ref[...], g_ref[0], preferred_element_type=F32, precision=lax.Precision.HIGHEST)
        gb_ref[0] = _colsum(d_ref[0])
        part = _rowsum(w_ref[0] * _colsum(gc_ref[0]))

        @pl.when(l == 0)
        def _():
            pc_ref[...] = jnp.zeros_like(pc_ref)

        pc_ref[...] += part

        @pl.when(l == DEPTH - 1)
        def _():
            cc = cc_ref[...]
            sg = _sigmoid(cc)
            pc_ref[...] = pc_ref[...] * (sg * (1.0 + cc * (1.0 - sg)))

    return _pc(
        body, name="mod_backward", grid=(DEPTH,),
        out_shape=(jax.ShapeDtypeStruct((DEPTH, D_MODEL, SHARD_MOD), F32),
                   jax.ShapeDtypeStruct((DEPTH, 1, 3 * D_MODEL), F32),
                   jax.ShapeDtypeStruct((D_MODEL, 1), F32)),
        in_specs=[_full((D_MODEL, LANES)),
                  pl.BlockSpec((1, LANES, SHARD_MOD), lambda l: (l, 0, 0)),
                  pl.BlockSpec((1, 8, SHARD_MOD), lambda l: (l, 0, 0)),
                  pl.BlockSpec((1, 16, 3 * D_MODEL), lambda l: (l, 0, 0)),
                  pl.BlockSpec((1, D_MODEL, SHARD_MOD), lambda l: (l, 0, 0)),
                  _full((D_MODEL, 1))],
        out_specs=(pl.BlockSpec((1, D_MODEL, SHARD_MOD), lambda l: (l, 0, 0)),
                   pl.BlockSpec((1, 1, 3 * D_MODEL), lambda l: (l, 0, 0)),
                   _full((D_MODEL, 1))),
        compiler_params=_cparams(),
    )(s_t, g_rows, g_ctx, d_all, w_mod, c_ctx_col)


def _head_norm(xb, lo):
    r = lax.rsqrt(_pair_sums(xb * xb, lo) * (1.0 / HEAD_DIM) + RMS_EPS)
    return xb * r, r


def _in_proj(xt, modv, g_pre, w_c, w_r, w_q, qk_gain, cos_t, sin_t):
    t = xt.shape[0]

    def body(x_ref, mod_ref, g_ref, wc_ref, wr_ref, wq_ref, gain_ref, cos_ref, sin_ref,
             h_ref, pc_ref, pr_ref, pq_ref, q_ref, k_ref, v_ref):
        is_ctx = pl.program_id(0) < N_CTX_TILES
        x = x_ref[...]
        r = lax.rsqrt(jnp.mean(x * x, axis=1, keepdims=True) + RMS_EPS)
        sh = jnp.where(is_ctx, mod_ref[0:1, :], mod_ref[3:4, :])
        sc = jnp.where(is_ctx, mod_ref[1:2, :], mod_ref[4:5, :])
        h = (x * r * g_ref[...]) * (1.0 + sc) + sh
        hb = h.astype(BF16)
        h_ref[...] = hb
        pc_ref[...] = _dot(hb, wc_ref[...])
        pr_ref[...] = _dot(hb, wr_ref[...])
        pq = _dot(hb, wq_ref[...])
        pq_ref[...] = pq
        lane = _lane(TM)
        lo = lane < HEAD_DIM
        lo16 = (lane & 31) < 16
        cos = cos_ref[...]
        sin = sin_ref[...]
        for b in range(3):
            xh, _ = _head_norm(pq[:, b * LANES:(b + 1) * LANES], lo)
            xg = xh * (gain_ref[0:1, :] if b < 2 else gain_ref[1:2, :])
            rot = xg * cos + _swap16(xg, lo16) * sin
            if b < 2:
                rot = rot * ATTN_SCALE
            dst = q_ref if b < 2 else k_ref
            base = 2 * b if b < 2 else 0
            dst[base] = jnp.where(lo, rot, 0.0).astype(BF16)
            dst[base + 1] = jnp.where(lo, pltpu.roll(rot, HEAD_DIM, 1), 0.0).astype(BF16)
        vb = pq[:, 3 * LANES:4 * LANES]
        one = jnp.where(lane == HEAD_DIM, 1.0, 0.0)
        v_ref[0] = jnp.where(lo, vb, one).astype(BF16)
        v_ref[1] = jnp.where(lo, pltpu.roll(vb, HEAD_DIM, 1), one).astype(BF16)

    return _pc(
        body, name="in_proj", grid=(t // TM,),
        out_shape=(jax.ShapeDtypeStruct((t, D_MODEL), BF16),
                   jax.ShapeDtypeStruct((t, W_C), F32), jax.ShapeDtypeStruct((t, W_R), F32), jax.ShapeDtypeStruct((t, W_Q), F32),
                   jax.ShapeDtypeStruct((N_Q_HEADS, t, LANES), BF16),
                   jax.ShapeDtypeStruct((N_KV_HEADS, t, LANES), BF16),
                   jax.ShapeDtypeStruct((N_KV_HEADS, t, LANES), BF16)),
        in_specs=[_rows(D_MODEL), _full((8, D_MODEL)), _full((1, D_MODEL)),
                  _full((D_MODEL, W_C)), _full((D_MODEL, W_R)), _full((D_MODEL, W_Q)),
                  _full((8, LANES)), _rows(LANES), _rows(LANES)],
        out_specs=(_rows(D_MODEL), _rows(W_C), _rows(W_R), _rows(W_Q),
                   _heads(N_Q_HEADS, LANES), _heads(N_KV_HEADS, LANES), _heads(N_KV_HEADS, LANES)),
        compiler_params=_cparams(),
    )(xt, modv, g_pre, w_c, w_r, w_q, qk_gain, cos_t, sin_t)


def _attention_fwd(q, k, v):
    t = q.shape[1]
    tkl = _kv_chunk(t - CTX_LEN)
    n_lat = (t - CTX_LEN) // tkl

    def body(q_ref, k_ref, v_ref, o_ref):
        i = pl.program_id(0)
        steps = jnp.where(i < N_CTX_TILES, 0, n_lat)
        lane = _lane(TM)
        for h in range(N_Q_HEADS):
            g = h // (N_Q_HEADS // N_KV_HEADS)
            qh = q_ref[h]

            def step(kc, vc, m, acc):
                s = _dot_nt(qh, kc)
                m_new = jnp.maximum(m, jnp.max(s, axis=1, keepdims=True))
                p = jnp.exp(s - m_new)
                return m_new, acc * jnp.exp(m - m_new) + _dot(p.astype(BF16), vc)

            m, acc = step(k_ref[g, 0:CTX_LEN, :], v_ref[g, 0:CTX_LEN, :],
                          jnp.full((TM, 1), -jnp.inf, F32), jnp.zeros((TM, LANES), F32))

            def loop(j, carry):
                st = pl.multiple_of(CTX_LEN + j * tkl, 256)
                return step(k_ref[g, pl.ds(st, tkl), :], v_ref[g, pl.ds(st, tkl), :], *carry)

            m, acc = lax.fori_loop(0, steps, loop, (m, acc))
            den = _rowsum(jnp.where(lane == HEAD_DIM, acc, 0.0))
            o_ref[h] = jnp.where(lane < HEAD_DIM, acc * (1.0 / den), jnp.where(lane == HEAD_DIM, m + jnp.log(den), 0.0))

    return _pc(
        body, name="attention_fwd", grid=(t // TM,),
        out_shape=jax.ShapeDtypeStruct((N_Q_HEADS, t, LANES), F32),
        in_specs=[_heads(N_Q_HEADS, LANES), _full((N_KV_HEADS, t, LANES)), _full((N_KV_HEADS, t, LANES))],
        out_specs=_heads(N_Q_HEADS, LANES),
        compiler_params=_cparams(),
    )(q, k, v)


def _halo_specs(width, t):
    last = t // HALO - 1
    per = TM // HALO
    prev = pl.BlockSpec((HALO, width), lambda i: (jnp.maximum(i * per - 1, 0), 0))
    nxt = pl.BlockSpec((HALO, width), lambda i: (jnp.minimum((i + 1) * per, last), 0))
    return prev, nxt


def _halo_valid(i, n_tiles):
    prev_ok = jnp.logical_and(i != 0, i != N_CTX_TILES)
    next_ok = jnp.logical_and(i != N_CTX_TILES - 1, i != n_tiles - 1)
    return jnp.where(prev_ok, 1.0, 0.0), jnp.where(next_ok, 1.0, 0.0)


def _conv_inputs(pc):
    u = pc[:, 0:GROUP_W] * pc[:, GROUP_W:2 * GROUP_W]
    z = pc[:, 2 * GROUP_W:3 * GROUP_W] * _sigmoid(pc[:, 3 * GROUP_W:4 * GROUP_W])
    return u, z


def _fill_ext(ext_ref, prev, mid, nxt):
    ext_ref[0:HALO, :] = prev
    ext_ref[HALO:HALO + TM, :] = mid
    ext_ref[HALO + TM:HALO + TM + HALO, :] = nxt


def _row_local_mixers(pr, ca, z2, oe, vecs, wss_ref, bsm, lane256):
    a_b, a_g, b_g = pr[:, 0:256], pr[:, 256:512], pr[:, 512:768]
    c_u, c_v, c_g, d_g = pr[:, 768:1024], pr[:, 1024:1280], pr[:, 1280:1536], pr[:, 1536:1792]
    zn, rs_b = _layer_norm_stats(z2)
    tb = zn * vecs[1:2, :] + vecs[2:3, :]
    vn_hat, rs_c = _layer_norm_stats(c_v)
    vn = vn_hat * vecs[3:4, :] + vecs[4:5, :]
    grp = jnp.right_shift(lane256, 6)
    sgs = []
    for ch in range(TM // CHUNK):
        r = _dot(wss_ref[...], vn[ch * CHUNK:(ch + 1) * CHUNK, :].astype(BF16))
        sgs.append(_group_select(r, grp[0:CHUNK]) + bsm)
    sg = jnp.concatenate(sgs, axis=0)
    lane = _lane(TM)
    lo = lane < HEAD_DIM
    att = jnp.concatenate([jnp.where(lo, oe[2 * b], pltpu.roll(oe[2 * b + 1], HEAD_DIM, 1)) for b in range(2)], axis=1)
    return dict(a_b=a_b, a_g=a_g, b_g=b_g, c_u=c_u, c_v=c_v, c_g=c_g, d_g=d_g, zn=zn, rs_b=rs_b, tb=tb,
                vn_hat=vn_hat, rs_c=rs_c, vn=vn, sg=sg, att=att, grp=grp, lo=lo, lane=lane)


def _mixer_concat(f, ca):
    ya = f["a_b"] * ca
    yb = f["tb"] * _sigmoid(f["tb"])
    yc = f["c_u"] * f["sg"]
    gates = [f[n] * _sigmoid(f[n]) for n in ("a_g", "b_g", "c_g", "d_g")]
    ys = (ya, yb, yc, f["att"])
    big = jnp.concatenate([yy * gg for yy, gg in zip(ys, gates)], axis=1).astype(BF16)
    return big, ys, gates


def _mix_out(pc, pr, oe, xt, modv, g_post, w_out, conv_a, conv_b, vecs, wss, bsm):
    t = xt.shape[0]
    n_tiles = t // TM
    prev_spec, next_spec = _halo_specs(W_C, t)

    def body(pc_ref, pp_ref, pn_ref, pr_ref, oe_ref, x_ref, mod_ref, gp_ref, wo_ref, cva_ref, cvb_ref, vec_ref, wss_ref, bsm_ref,
             xo_ref, y_ref, ca_ref, z2_ref, uext, zext):
        i = pl.program_id(0)
        is_ctx = i < N_CTX_TILES
        pv, nv = _halo_valid(i, n_tiles)
        u, z = _conv_inputs(pc_ref[...])
        up, zp = _conv_inputs(pp_ref[...])
        un, zn_ = _conv_inputs(pn_ref[...])
        _fill_ext(uext, up * pv, u, un * nv)
        _fill_ext(zext, zp * pv, z, zn_ * nv)
        ca = cva_ref[0:1, :] * uext[pl.ds(HALO - 1, TM), :]
        for kk in range(1, SHORT_CONV_K):
            ca = ca + cva_ref[kk:kk + 1, :] * uext[pl.ds(HALO - 1 + kk, TM), :]
        z2 = cvb_ref[0:1, :] * zext[pl.ds(HALO - CONFORMER_K // 2, TM), :]
        for kk in range(1, CONFORMER_K):
            z2 = z2 + cvb_ref[kk:kk + 1, :] * zext[pl.ds(HALO - CONFORMER_K // 2 + kk, TM), :]
        vecs = vec_ref[...]
        z2 = z2 + vecs[0:1, :]
        ca_ref[...] = ca
        z2_ref[...] = z2
        lane256 = lax.broadcasted_iota(jnp.int32, (TM, GROUP_W), 1)
        f = _row_local_mixers(pr_ref[...], ca, z2, oe_ref, vecs, wss_ref, bsm_ref[...], lane256)
        big, _, _ = _mixer_concat(f, ca)
        y = _dot(big, wo_ref[...])
        y_ref[...] = y
        ry = lax.rsqrt(jnp.mean(y * y, axis=1, keepdims=True) + RMS_EPS)
        gt = jnp.where(is_ctx, mod_ref[2:3, :], mod_ref[5:6, :])
        xo_ref[...] = x_ref[...] + gt * (y * ry * gp_ref[...])

    return _pc(
        body, name="mix_out", grid=(n_tiles,),
        out_shape=(jax.ShapeDtypeStruct((t, D_MODEL), F32), jax.ShapeDtypeStruct((t, D_MODEL), F32),
                   jax.ShapeDtypeStruct((t, GROUP_W), F32), jax.ShapeDtypeStruct((t, GROUP_W), F32)),
        in_specs=[_rows(W_C), prev_spec, next_spec, _rows(W_R), _heads(N_Q_HEADS, LANES), _rows(D_MODEL),
                  _full((8, D_MODEL)), _full((1, D_MODEL)), _full((D_MODEL, D_MODEL)),
                  _full((8, GROUP_W)), _full((32, GROUP_W)), _full((8, GROUP_W)),
                  _full((N_SPATIAL_GROUPS * CHUNK, CHUNK)), _full((CHUNK, GROUP_W))],
        out_specs=(_rows(D_MODEL), _rows(D_MODEL), _rows(GROUP_W), _rows(GROUP_W)),
        scratch_shapes=[pltpu.VMEM((TM + 2 * HALO, GROUP_W), F32), pltpu.VMEM((TM + 2 * HALO, GROUP_W), F32)],
        compiler_params=_cparams(),
    )(pc, pc, pc, pr, oe, xt, modv, g_post, w_out, conv_a, conv_b, vecs, wss, bsm)


def _loss_head(xt, target):
    t = xt.shape[0]

    def body(x_ref, t_ref, dx_ref, loss_ref):
        i = pl.program_id(0)

        @pl.when(i == 0)
        def _():
            loss_ref[...] = jnp.zeros_like(loss_ref)

        lat = jnp.where(i < N_CTX_TILES, 0.0, 1.0)
        err = (x_ref[...] - t_ref[...]) * lat
        dx_ref[...] = err * (1.0 / D_MODEL)
        loss_ref[...] += jnp.sum(err * err) * (0.5 / D_MODEL)

    return _pc(
        body, name="loss_head", grid=(t // TM,),
        out_shape=(jax.ShapeDtypeStruct((t, D_MODEL), F32), jax.ShapeDtypeStruct((8, LANES), F32)),
        in_specs=[_rows(D_MODEL), pl.BlockSpec((TM, D_MODEL), lambda i: (jnp.maximum(i - N_CTX_TILES, 0), 0))],
        out_specs=(_rows(D_MODEL), _full((8, LANES))),
        compiler_params=_cparams(),
    )(xt, target)


def _mix_out_bwd(dxo, y, pr, ca, z2, oe, modv, g_post, w_out, vecs, wss, wsts, bsm):
    t = y.shape[0]
    n_tiles = t // TM

    def body(dxo_ref, y_ref, pr_ref, ca_ref, z2_ref, oe_ref, mod_ref, gp_ref, wo_ref, vec_ref, wss_ref, wsts_ref, bsm_ref,
             dpr_ref, ga_ref, gb_ref, doe_ref, dwo_ref, pvec_ref, s256_ref, dws_ref, dbs_ref, dbsm):
        i = pl.program_id(0)
        is_ctx = i < N_CTX_TILES

        @pl.when(i == 0)
        def _():
            dwo_ref[...] = jnp.zeros_like(dwo_ref)
            pvec_ref[...] = jnp.zeros_like(pvec_ref)
            s256_ref[...] = jnp.zeros_like(s256_ref)
            dws_ref[...] = jnp.zeros_like(dws_ref)
            dbsm[...] = jnp.zeros_like(dbsm)

        dxo_ = dxo_ref[...]
        y_ = y_ref[...]
        ry = lax.rsqrt(jnp.mean(y_ * y_, axis=1, keepdims=True) + RMS_EPS)
        nh = y_ * ry
        gp = gp_ref[...]
        gt = jnp.where(is_ctx, mod_ref[2:3, :], mod_ref[5:6, :])
        dgt = _colsum(dxo_ * (nh * gp))
        pvec_ref[0:1, :] += jnp.where(is_ctx, dgt, 0.0)
        pvec_ref[1:2, :] += jnp.where(is_ctx, 0.0, dgt)
        dn = dxo_ * gt
        pvec_ref[2:3, :] += _colsum(dn * nh)
        dnh = dn * gp
        dy = ry * (dnh - nh * jnp.mean(dnh * nh, axis=1, keepdims=True))

        vecs = vec_ref[...]
        bsm_ = bsm_ref[...]
        ca_ = ca_ref[...]
        lane256 = lax.broadcasted_iota(jnp.int32, (TM, GROUP_W), 1)
        f = _row_local_mixers(pr_ref[...], ca_, z2_ref[...], oe_ref, vecs, wss_ref, bsm_, lane256)
        big, ys, gates = _mixer_concat(f, ca_)
        dyb = dy.astype(BF16)
        dwo_ref[...] += _dot_tn(big, dyb)
        dbig = _dot_nt(dyb, wo_ref[...])

        d_y, d_gate = [], []
        for n, (name, yy, gg) in enumerate(zip(("a_g", "b_g", "c_g", "d_g"), ys, gates)):
            dpart = dbig[:, n * GROUP_W:(n + 1) * GROUP_W]
            gx = f[name]
            sg_ = _sigmoid(gx)
            d_y.append(dpart * gg)
            d_gate.append(dpart * yy * (sg_ * (1.0 + gx * (1.0 - sg_))))
        dya, dyb_, dyc, datt = d_y

        d_ab = dya * ca_
        ga_ref[...] = dya * f["a_b"]
        tb = f["tb"]
        sb = _sigmoid(tb)
        dtb = dyb_ * (sb * (1.0 + tb * (1.0 - sb)))
        s256_ref[1:2, :] += _colsum(dtb * f["zn"])
        s256_ref[2:3, :] += _colsum(dtb)
        dz2 = _layer_norm_bwd(dtb * vecs[1:2, :], f["zn"], f["rs_b"])
        gb_ref[...] = dz2
        s256_ref[0:1, :] += _colsum(dz2)
        d_cu = dyc * f["sg"]
        dsg = dyc * f["c_u"]
        grp = f["grp"]
        dvn_parts = []
        for ch in range(TM // CHUNK):
            rows = slice(ch * CHUNK, (ch + 1) * CHUNK)
            dsg_c = dsg[rows, :]
            dbsm[...] += dsg_c
            vn_c = f["vn"][rows, :].astype(BF16)
            for g in range(N_SPATIAL_GROUPS):
                masked = jnp.where(grp[0:CHUNK] == g, dsg_c, 0.0).astype(BF16)
                dws_ref[g * CHUNK:(g + 1) * CHUNK, :] += _dot_nt(masked, vn_c)
            dvn_parts.append(_group_select(_dot(wsts_ref[...], dsg_c.astype(BF16)), grp[0:CHUNK]))
        dvn = jnp.concatenate(dvn_parts, axis=0)
        s256_ref[3:4, :] += _colsum(dvn * f["vn_hat"])
        s256_ref[4:5, :] += _colsum(dvn)
        d_cv = _layer_norm_bwd(dvn * vecs[3:4, :], f["vn_hat"], f["rs_c"])
        lane, lo = f["lane"], f["lo"]
        att = f["att"]
        for b in range(2):
            da = datt[:, b * LANES:(b + 1) * LANES]
            prod = da * att[:, b * LANES:(b + 1) * LANES]
            for hh in range(2):
                h = 2 * b + hh
                lse = _rowsum(jnp.where(lane == HEAD_DIM, oe_ref[h], 0.0))
                delta = _rowsum(jnp.where(lo, prod, 0.0) if hh == 0 else jnp.where(lo, 0.0, prod))
                dah = da if hh == 0 else pltpu.roll(da, HEAD_DIM, 1)
                doe_ref[h] = jnp.where(lo, dah, jnp.where(lane == HEAD_DIM, delta, jnp.where(lane == HEAD_DIM + 1, lse, 0.0)))

        dpr_ref[...] = jnp.concatenate([d_ab, d_gate[0], d_gate[1], d_cu, d_cv, d_gate[2], d_gate[3]], axis=1).astype(BF16)

        @pl.when(i == n_tiles - 1)
        def _():
            acc = dbsm[...]
            lane128 = _lane(CHUNK)
            out = jnp.zeros((CHUNK, LANES), F32)
            for g in range(N_SPATIAL_GROUPS):
                col = _rowsum(jnp.where(grp[0:CHUNK] == g, acc, 0.0))
                out = out + jnp.where(lane128 == g, col, 0.0)
            dbs_ref[...] = out

    return _pc(
        body, name="mix_out_bwd", grid=(n_tiles,),
        out_shape=(jax.ShapeDtypeStruct((t, W_R), BF16),
                   jax.ShapeDtypeStruct((t, GROUP_W), F32), jax.ShapeDtypeStruct((t, GROUP_W), F32),
                   jax.ShapeDtypeStruct((N_Q_HEADS, t, LANES), F32),
                   jax.ShapeDtypeStruct((D_MODEL, D_MODEL), F32),
                   jax.ShapeDtypeStruct((8, D_MODEL), F32),
                   jax.ShapeDtypeStruct((8, GROUP_W), F32),
                   jax.ShapeDtypeStruct((N_SPATIAL_GROUPS * CHUNK, CHUNK), F32),
                   jax.ShapeDtypeStruct((CHUNK, LANES), F32)),
        in_specs=[_rows(D_MODEL), _rows(D_MODEL), _rows(W_R), _rows(GROUP_W), _rows(GROUP_W), _heads(N_Q_HEADS, LANES),
                  _full((8, D_MODEL)), _full((1, D_MODEL)), _full((D_MODEL, D_MODEL)), _full((8, GROUP_W)),
                  _full((N_SPATIAL_GROUPS * CHUNK, CHUNK)), _full((N_SPATIAL_GROUPS * CHUNK, CHUNK)), _full((CHUNK, GROUP_W))],
        out_specs=(_rows(W_R), _rows(GROUP_W), _rows(GROUP_W), _heads(N_Q_HEADS, LANES),
                   _full((D_MODEL, D_MODEL)), _full((8, D_MODEL)), _full((8, GROUP_W)),
                   _full((N_SPATIAL_GROUPS * CHUNK, CHUNK)), _full((CHUNK, LANES))),
        scratch_shapes=[pltpu.VMEM((CHUNK, GROUP_W), F32)],
        compiler_params=_cparams(),
    )(dxo, y, pr, ca, z2, oe, modv, g_post, w_out, vecs, wss, wsts, bsm)


def _conv_bwd(pc, g_a, g_b, conv_a, conv_b):
    t = pc.shape[0]
    n_tiles = t // TM
    pc_prev, pc_next = _halo_specs(W_C, t)
    g_prev, g_next = _halo_specs(GROUP_W, t)

    def body(pc_ref, pp_ref, pn_ref, ga_ref, gap_ref, gan_ref, gb_ref, gbp_ref, gbn_ref, cva_ref, cvb_ref,
             dpc_ref, dca_ref, dcb_ref, uext, zext, gaext, gbext):
        i = pl.program_id(0)

        @pl.when(i == 0)
        def _():
            dca_ref[...] = jnp.zeros_like(dca_ref)
            dcb_ref[...] = jnp.zeros_like(dcb_ref)

        pv, nv = _halo_valid(i, n_tiles)
        pc_ = pc_ref[...]
        u, z = _conv_inputs(pc_)
        up, zp = _conv_inputs(pp_ref[...])
        un, zn_ = _conv_inputs(pn_ref[...])
        _fill_ext(uext, up * pv, u, un * nv)
        _fill_ext(zext, zp * pv, z, zn_ * nv)
        ga = ga_ref[...]
        gb = gb_ref[...]
        _fill_ext(gaext, gap_ref[...] * pv, ga, gan_ref[...] * nv)
        _fill_ext(gbext, gbp_ref[...] * pv, gb, gbn_ref[...] * nv)

        du = cva_ref[0:1, :] * gaext[pl.ds(HALO + 1, TM), :]
        dca_ref[0:1, :] += _colsum(ga * uext[pl.ds(HALO - 1, TM), :])
        for kk in range(1, SHORT_CONV_K):
            du = du + cva_ref[kk:kk + 1, :] * gaext[pl.ds(HALO + 1 - kk, TM), :]
            dca_ref[kk:kk + 1, :] += _colsum(ga * uext[pl.ds(HALO - 1 + kk, TM), :])
        half = CONFORMER_K // 2
        dz = cvb_ref[0:1, :] * gbext[pl.ds(HALO + half, TM), :]
        dcb_ref[0:1, :] += _colsum(gb * zext[pl.ds(HALO - half, TM), :])
        for kk in range(1, CONFORMER_K):
            dz = dz + cvb_ref[kk:kk + 1, :] * gbext[pl.ds(HALO + half - kk, TM), :]
            dcb_ref[kk:kk + 1, :] += _colsum(gb * zext[pl.ds(HALO - half + kk, TM), :])

        a_c, a_h = pc_[:, 0:GROUP_W], pc_[:, GROUP_W:2 * GROUP_W]
        glu_a, glu_g = pc_[:, 2 * GROUP_W:3 * GROUP_W], pc_[:, 3 * GROUP_W:4 * GROUP_W]
        sg = _sigmoid(glu_g)
        dpc_ref[...] = jnp.concatenate([du * a_h, du * a_c, dz * sg, dz * glu_a * sg * (1.0 - sg)], axis=1).astype(BF16)

    ext = pltpu.VMEM((TM + 2 * HALO, GROUP_W), F32)
    return _pc(
        body, name="conv_bwd", grid=(n_tiles,),
        out_shape=(jax.ShapeDtypeStruct((t, W_C), BF16), jax.ShapeDtypeStruct((8, GROUP_W), F32), jax.ShapeDtypeStruct((32, GROUP_W), F32)),
        in_specs=[_rows(W_C), pc_prev, pc_next, _rows(GROUP_W), g_prev, g_next, _rows(GROUP_W), g_prev, g_next,
                  _full((8, GROUP_W)), _full((32, GROUP_W))],
        out_specs=(_rows(W_C), _full((8, GROUP_W)), _full((32, GROUP_W))),
        scratch_shapes=[ext, ext, ext, ext],
        compiler_params=_cparams(),
    )(pc, pc, pc, g_a, g_a, g_a, g_b, g_b, g_b, conv_a, conv_b)


def _attention_bwd(q, k, v, doe):
    t = q.shape[1]
    tkl = _kv_chunk(t - CTX_LEN)
    n_lat = (t - CTX_LEN) // tkl
    n_tiles = t // TM

    def body(q_ref, do_ref, k_ref, v_ref, dq_ref, dk_hbm, dv_hbm, dk_acc, dv_acc):
        i = pl.program_id(0)

        @pl.when(i == 0)
        def _():
            dk_acc[...] = jnp.zeros_like(dk_acc)
            dv_acc[...] = jnp.zeros_like(dv_acc)

        steps = jnp.where(i < N_CTX_TILES, 0, n_lat)
        lane = _lane(TM)
        lo = lane < HEAD_DIM
        for h in range(N_Q_HEADS):
            g = h // (N_Q_HEADS // N_KV_HEADS)
            qh = q_ref[h]
            doh = do_ref[h]
            delta = _rowsum(jnp.where(lane == HEAD_DIM, doh, 0.0))
            lse = _rowsum(jnp.where(lane == HEAD_DIM + 1, doh, 0.0))
            dob = jnp.where(lo, doh, 0.0).astype(BF16)

            def step(st, size, dq):
                kc = k_ref[g, pl.ds(st, size), :]
                vc = v_ref[g, pl.ds(st, size), :]
                p = jnp.exp(_dot_nt(qh, kc) - lse)
                ds_ = (p * (_dot_nt(dob, vc) - delta)).astype(BF16)
                dk_acc[g, pl.ds(st, size), :] += _dot_tn(ds_, qh)
                dv_acc[g, pl.ds(st, size), :] += _dot_tn(p.astype(BF16), dob)
                return dq + _dot(ds_, kc)

            dq = step(0, CTX_LEN, jnp.zeros((TM, LANES), F32))
            dq = lax.fori_loop(0, steps, lambda j, acc: step(pl.multiple_of(CTX_LEN + j * tkl, 256), tkl, acc), dq)
            dq_ref[h] = dq

        @pl.when(i == n_tiles - 1)
        def _():
            pltpu.sync_copy(dk_acc, dk_hbm)
            pltpu.sync_copy(dv_acc, dv_hbm)

    kv_shape = jax.ShapeDtypeStruct((N_KV_HEADS, t, LANES), F32)
    return _pc(
        body, name="attention_bwd", grid=(n_tiles,),
        out_shape=(jax.ShapeDtypeStruct((N_Q_HEADS, t, LANES), F32), kv_shape, kv_shape),
        in_specs=[_heads(N_Q_HEADS, LANES), _heads(N_Q_HEADS, LANES),
                  _full((N_KV_HEADS, t, LANES)), _full((N_KV_HEADS, t, LANES))],
        out_specs=(_heads(N_Q_HEADS, LANES), pl.BlockSpec(memory_space=pl.ANY), pl.BlockSpec(memory_space=pl.ANY)),
        scratch_shapes=[pltpu.VMEM((N_KV_HEADS, t, LANES), F32), pltpu.VMEM((N_KV_HEADS, t, LANES), F32)],
        compiler_params=_cparams(),
    )(q, doe, k, v)


def _qkv_bwd(dq, dk, dv, pq, qk_gain, cos_t, sin_t):
    t = pq.shape[0]
    n_tiles = t // TM

    def body(dq_ref, dk_ref, dv_ref, pq_ref, gain_ref, cos_ref, sin_ref, dpq_ref, dgain_ref):
        i = pl.program_id(0)

        @pl.when(i == 0)
        def _():
            dgain_ref[...] = jnp.zeros_like(dgain_ref)

        lane = _lane(TM)
        lo = lane < HEAD_DIM
        lo16 = (lane & 31) < 16
        cos = cos_ref[...]
        sin = sin_ref[...]
        pq_ = pq_ref[...]
        outs = []
        for b in range(3):
            src = dq_ref if b < 2 else dk_ref
            base = 2 * b if b < 2 else 0
            drot = src[base] + pltpu.roll(src[base + 1], HEAD_DIM, 1)
            if b < 2:
                drot = drot * ATTN_SCALE
            dxg = drot * cos + _swap16(drot * sin, lo16)
            xh, r = _head_norm(pq_[:, b * LANES:(b + 1) * LANES], lo)
            row = 0 if b < 2 else 1
            dgain_ref[row:row + 1, :] += _colsum(dxg * xh)
            dxh = dxg * gain_ref[row:row + 1, :]
            outs.append(r * (dxh - xh * (_pair_sums(dxh * xh, lo) * (1.0 / HEAD_DIM))))
        outs.append(dv_ref[0] + pltpu.roll(dv_ref[1], HEAD_DIM, 1))
        dpq_ref[...] = jnp.concatenate(outs, axis=1).astype(BF16)

    return _pc(
        body, name="qkv_bwd", grid=(n_tiles,),
        out_shape=(jax.ShapeDtypeStruct((t, W_Q), BF16), jax.ShapeDtypeStruct((8, LANES), F32)),
        in_specs=[_heads(N_Q_HEADS, LANES), _heads(N_KV_HEADS, LANES), _heads(N_KV_HEADS, LANES), _rows(W_Q),
                  _full((8, LANES)), _rows(LANES), _rows(LANES)],
        out_specs=(_rows(W_Q), _full((8, LANES))),
        compiler_params=_cparams(),
    )(dq, dk, dv, pq, qk_gain, cos_t, sin_t)


def _in_proj_bwd(dpc, dpr, dpq, w_c, w_r, w_q, xt, dxo, modv, g_pre):
    t = xt.shape[0]

    def body(dpc_ref, dpr_ref, dpq_ref, wc_ref, wr_ref, wq_ref, x_ref, dxo_ref, mod_ref, g_ref, dx_ref, acc_ref):
        i = pl.program_id(0)
        is_ctx = i < N_CTX_TILES

        @pl.when(i == 0)
        def _():
            acc_ref[...] = jnp.zeros_like(acc_ref)

        dh = _dot_nt(dpc_ref[...], wc_ref[...]) + _dot_nt(dpr_ref[...], wr_ref[...]) + _dot_nt(dpq_ref[...], wq_ref[...])
        x = x_ref[...]
        r = lax.rsqrt(jnp.mean(x * x, axis=1, keepdims=True) + RMS_EPS)
        xn = x * r
        g = g_ref[...]
        sc = jnp.where(is_ctx, mod_ref[1:2, :], mod_ref[4:5, :])
        dsh = _colsum(dh)
        dsc = _colsum(dh * (xn * g))
        acc_ref[0:1, :] += jnp.where(is_ctx, dsh, 0.0)
        acc_ref[1:2, :] += jnp.where(is_ctx, dsc, 0.0)
        acc_ref[2:3, :] += jnp.where(is_ctx, 0.0, dsh)
        acc_ref[3:4, :] += jnp.where(is_ctx, 0.0, dsc)
        dxg = dh * (1.0 + sc)
        acc_ref[4:5, :] += _colsum(dxg * xn)
        dxn = dxg * g
        dx_ref[...] = r * (dxn - xn * jnp.mean(dxn * xn, axis=1, keepdims=True)) + dxo_ref[...]

    return _pc(
        body, name="in_proj_bwd", grid=(t // TM,),
        out_shape=(jax.ShapeDtypeStruct((t, D_MODEL), F32), jax.ShapeDtypeStruct((8, D_MODEL), F32)),
        in_specs=[_rows(W_C), _rows(W_R), _rows(W_Q), _full((D_MODEL, W_C)), _full((D_MODEL, W_R)), _full((D_MODEL, W_Q)),
                  _rows(D_MODEL), _rows(D_MODEL), _full((8, D_MODEL)), _full((1, D_MODEL))],
        out_specs=(_rows(D_MODEL), _full((8, D_MODEL))),
        compiler_params=_cparams(),
    )(dpc, dpr, dpq, w_c, w_r, w_q, xt, dxo, modv, g_pre)


def _in_proj_wgrad(h, dpc, dpr, dpq):
    t = h.shape[0]

    def body(h_ref, dpc_ref, dpr_ref, dpq_ref, gc_ref, gr_ref, gq_ref):
        @pl.when(pl.program_id(0) == 0)
        def _():
            gc_ref[...] = jnp.zeros_like(gc_ref)
            gr_ref[...] = jnp.zeros_like(gr_ref)
            gq_ref[...] = jnp.zeros_like(gq_ref)

        hb = h_ref[...]
        gc_ref[...] += _dot_tn(hb, dpc_ref[...])
        gr_ref[...] += _dot_tn(hb, dpr_ref[...])
        gq_ref[...] += _dot_tn(hb, dpq_ref[...])

    return _pc(
        body, name="in_proj_wgrad", grid=(t // TM,),
        out_shape=(jax.ShapeDtypeStruct((D_MODEL, W_C), F32), jax.ShapeDtypeStruct((D_MODEL, W_R), F32),
                   jax.ShapeDtypeStruct((D_MODEL, W_Q), F32)),
        in_specs=[_rows(D_MODEL), _rows(W_C), _rows(W_R), _rows(W_Q)],
        out_specs=(_full((D_MODEL, W_C)), _full((D_MODEL, W_R)), _full((D_MODEL, W_Q))),
        compiler_params=_cparams(),
    )(h, dpc, dpr, dpq)


def _sum_slabs(slabs, tile_rows):
    n, r, c = slabs.shape

    def body(s_ref, o_ref):
        acc = s_ref[0]
        for k in range(1, n):
            acc = acc + s_ref[k]
        o_ref[...] = acc

    return _pc(
        body, name="sum_slabs", grid=(r // tile_rows,),
        out_shape=jax.ShapeDtypeStruct((r, c), F32),
        in_specs=[pl.BlockSpec((n, tile_rows, c), lambda i: (0, i, 0))],
        out_specs=pl.BlockSpec((tile_rows, c), lambda i: (i, 0)),
        compiler_params=_cparams(),
    )(slabs)


def _adamw(grads, w, m, v, tile_rows):
    r, c = w.shape
    n_g = len(grads)

    def body(*refs):
        g = refs[0][...]
        for k in range(1, n_g):
            g = g + refs[k][...]
        w_ref, m_ref, v_ref, g_out, d_out, m_out, v_out = refs[n_g:]
        m_new = ADAM_B1 * m_ref[...] + (1.0 - ADAM_B1) * g
        v_new = ADAM_B2 * v_ref[...] + (1.0 - ADAM_B2) * (g * g)
        m_hat = m_new / (1.0 - ADAM_B1 ** ADAM_STEP)
        v_hat = v_new / (1.0 - ADAM_B2 ** ADAM_STEP)
        g_out[...] = g
        d_out[...] = -ADAM_LR * (m_hat / (jnp.sqrt(v_hat) + ADAM_EPS) + ADAM_WD * w_ref[...])
        m_out[...] = m_new
        v_out[...] = v_new

    spec = pl.BlockSpec((tile_rows, c), lambda i: (i, 0))
    shape = jax.ShapeDtypeStruct((r, c), F32)
    return _pc(
        body, name="adamw", grid=(r // tile_rows,),
        out_shape=(shape,) * 4, in_specs=[spec] * (n_g + 3), out_specs=(spec,) * 4,
        compiler_params=_cparams(),
    )(*grads, w, m, v)


def _rope_tables(s_lat):
    pos = jnp.arange(s_lat)
    pos_row = (pos // GRID_W).astype(F32)
    pos_col = (pos % GRID_W).astype(F32)
    axis_dim = HEAD_DIM // 2
    inv_freq = 1.0 / (ROPE_THETA ** (jnp.arange(0, axis_dim, 2, dtype=F32) / axis_dim))
    d = np.arange(LANES) % HEAD_DIM
    on_rows = (d // axis_dim) == 0
    freq = d % (axis_dim // 2)
    sign = np.where((d % axis_dim) < axis_dim // 2, -1.0, 1.0).astype(np.float32)
    ang = jnp.where(on_rows[None, :], pos_row[:, None], pos_col[:, None]) * inv_freq[freq][None, :]
    cos = jnp.concatenate([jnp.ones((CTX_LEN, LANES), F32), jnp.cos(ang)], axis=0)
    sin = jnp.concatenate([jnp.zeros((CTX_LEN, LANES), F32), jnp.sin(ang) * sign[None, :]], axis=0)
    return cos, sin


def _pad_rows(a, rows):
    return jnp.concatenate([a, jnp.zeros((rows - a.shape[0],) + a.shape[1:], a.dtype)], axis=0)


_SMALL = ("c_ctx", "b_mod", "g_pre", "g_post", "conv_a", "conv_b", "conv_b_bias", "conf_ln_g", "conf_ln_b",
          "sgu_ln_g", "sgu_ln_b", "w_s", "b_s", "q_gain", "k_gain")


def _pack(arrays):
    flat = jnp.concatenate([a.reshape(-1) for a in arrays])
    rows = -(-flat.shape[0] // (8 * LANES)) * 8
    return _pad_rows(flat.reshape(-1, 1), rows * LANES).reshape(rows, LANES)


def _unpack(packed, shapes):
    flat = packed.reshape(-1)
    out, off = [], 0
    for s in shapes:
        n = int(np.prod(s))
        out.append(flat[off:off + n].reshape(s))
        off += n
    return out


def kernel(x, c, ctx, c_ctx, w_mod, b_mod, g_pre, g_post, w_in, w_out, conv_a, conv_b, conv_b_bias, conf_ln_g, conf_ln_b, sgu_ln_g, sgu_ln_b, w_s, b_s, q_gain, k_gain, loss_target, m_c_ctx, m_w_mod, m_b_mod, m_g_pre, m_g_post, m_w_in, m_w_out, m_conv_a, m_conv_b, m_conv_b_bias, m_conf_ln_g, m_conf_ln_b, m_sgu_ln_g, m_sgu_ln_b, m_w_s, m_b_s, m_q_gain, m_k_gain, v_c_ctx, v_w_mod, v_b_mod, v_g_pre, v_g_post, v_w_in, v_w_out, v_conv_a, v_conv_b, v_conv_b_bias, v_conf_ln_g, v_conf_ln_b, v_sgu_ln_g, v_sgu_ln_b, v_w_s, v_b_s, v_q_gain, v_k_gain):
    weights = dict(c_ctx=c_ctx, w_mod=w_mod, b_mod=b_mod, g_pre=g_pre, g_post=g_post, w_in=w_in, w_out=w_out, conv_a=conv_a,
                   conv_b=conv_b, conv_b_bias=conv_b_bias, conf_ln_g=conf_ln_g, conf_ln_b=conf_ln_b, sgu_ln_g=sgu_ln_g,
                   sgu_ln_b=sgu_ln_b, w_s=w_s, b_s=b_s, q_gain=q_gain, k_gain=k_gain)
    m_in = dict(c_ctx=m_c_ctx, w_mod=m_w_mod, b_mod=m_b_mod, g_pre=m_g_pre, g_post=m_g_post, w_in=m_w_in, w_out=m_w_out,
                conv_a=m_conv_a, conv_b=m_conv_b, conv_b_bias=m_conv_b_bias, conf_ln_g=m_conf_ln_g, conf_ln_b=m_conf_ln_b,
                sgu_ln_g=m_sgu_ln_g, sgu_ln_b=m_sgu_ln_b, w_s=m_w_s, b_s=m_b_s, q_gain=m_q_gain, k_gain=m_k_gain)
    v_in = dict(c_ctx=v_c_ctx, w_mod=v_w_mod, b_mod=v_b_mod, g_pre=v_g_pre, g_post=v_g_post, w_in=v_w_in, w_out=v_w_out,
                conv_a=v_conv_a, conv_b=v_conv_b, conv_b_bias=v_conv_b_bias, conf_ln_g=v_conf_ln_g, conf_ln_b=v_conf_ln_b,
                sgu_ln_g=v_sgu_ln_g, sgu_ln_b=v_sgu_ln_b, w_s=v_w_s, b_s=v_b_s, q_gain=v_q_gain, k_gain=v_k_gain)
    order = ("c_ctx", "w_mod", "b_mod", "g_pre", "g_post", "w_in", "w_out", "conv_a", "conv_b", "conv_b_bias", "conf_ln_g",
             "conf_ln_b", "sgu_ln_g", "sgu_ln_b", "w_s", "b_s", "q_gain", "k_gain")

    s_lat = x.shape[1]
    ax, ay, ac = lax.axis_index("x"), lax.axis_index("y"), lax.axis_index("c")
    chip = 2 * ax + ay
    example = 4 * ax + 2 * ay + ac

    c_rows = _all_gather_rows(_pad_rows(c, 8))[::8]
    c16 = _pad_rows(jnp.concatenate([c_rows, c_ctx[None, :]], axis=0), 16)
    b_mod_shard = lax.dynamic_slice_in_dim(b_mod, chip * SHARD_MOD, SHARD_MOD, axis=1)[:, None, :]
    silu_c, mod_shard = _mod_forward(c16, w_mod, b_mod_shard)
    mod_all = _all_gather_rows(mod_shard.reshape(DEPTH * 16, SHARD_MOD)).reshape(8, DEPTH, 16, SHARD_MOD)
    mod_full = jnp.transpose(mod_all[::2], (1, 2, 0, 3)).reshape(DEPTH, 16, 3 * D_MODEL)
    mod_lat = lax.dynamic_index_in_dim(mod_full, example, axis=1, keepdims=False).reshape(DEPTH, 3, D_MODEL)
    mod_ctx = mod_full[:, 8].reshape(DEPTH, 3, D_MODEL)
    modv = jnp.concatenate([mod_ctx, mod_lat, jnp.zeros((DEPTH, 2, D_MODEL), F32)], axis=1)

    wi_all, wo_all = _gather_weights(w_in.astype(BF16), w_out.astype(BF16))
    wi_full = jnp.concatenate([wi_all[k] for k in range(N_CHIPS)], axis=-1)
    wo_full = jnp.concatenate([wo_all[k] for k in range(N_CHIPS)], axis=1)
    w_c = jnp.concatenate([wi_full[..., 256:768], wi_full[..., 1024:1536]], axis=-1)
    w_r = jnp.concatenate([wi_full[..., 0:256], wi_full[..., 768:1024], wi_full[..., 1536:2560], wi_full[..., 3072:3328]], axis=-1)
    w_q = wi_full[..., 2560:3072]

    cos_t, sin_t = _rope_tables(s_lat)
    conv_a_full = jnp.zeros((DEPTH, 8, GROUP_W), F32)
    conv_b_full = jnp.zeros((DEPTH, 32, GROUP_W), F32)
    conv_small = jnp.concatenate([conv_a.reshape(DEPTH * SHORT_CONV_K, -1), conv_b.reshape(DEPTH * CONFORMER_K, -1)], axis=0)
    n_cs = conv_small.shape[0]
    conv_rows = -(-n_cs // 8) * 8
    conv_all = _all_gather_rows(_pad_rows(conv_small, conv_rows)).reshape(8, conv_rows, -1)[::2]
    conv_all = jnp.transpose(conv_all, (1, 0, 2)).reshape(conv_rows, GROUP_W)
    conv_a_full = conv_a_full.at[:, :SHORT_CONV_K].set(conv_all[:DEPTH * SHORT_CONV_K].reshape(DEPTH, SHORT_CONV_K, GROUP_W))
    conv_b_full = conv_b_full.at[:, :CONFORMER_K].set(
        conv_all[DEPTH * SHORT_CONV_K:n_cs].reshape(DEPTH, CONFORMER_K, GROUP_W))

    vecs = jnp.stack([conv_b_bias, conf_ln_g, conf_ln_b, sgu_ln_g, sgu_ln_b] + [jnp.zeros_like(conv_b_bias)] * 3, axis=1)
    wss = w_s.reshape(DEPTH, N_SPATIAL_GROUPS * CHUNK, CHUNK).astype(BF16)
    wsts = jnp.swapaxes(w_s, 2, 3).reshape(DEPTH, N_SPATIAL_GROUPS * CHUNK, CHUNK).astype(BF16)
    bsm = jnp.repeat(jnp.swapaxes(b_s, 1, 2), HEAD_DIM, axis=2)
    qk_gain = jnp.concatenate([jnp.tile(q_gain, (1, 2))[:, None, :], jnp.tile(k_gain, (1, 2))[:, None, :],
                               jnp.zeros((DEPTH, 6, LANES), F32)], axis=1)

    xt = jnp.concatenate([ctx[0], x[0]], axis=0)
    saved = []
    for l in range(DEPTH):
        h, pc, pr, pq, q, k, v = _in_proj(xt, modv[l], g_pre[l][None, :], w_c[l], w_r[l], w_q[l], qk_gain[l], cos_t, sin_t)
        oe = _attention_fwd(q, k, v)
        x_new, y, ca, z2 = _mix_out(pc, pr, oe, xt, modv[l], g_post[l][None, :], wo_full[l], conv_a_full[l], conv_b_full[l],
                                    vecs[l], wss[l], bsm[l])
        saved.append(dict(x=xt, h=h, pc=pc, pr=pr, pq=pq, q=q, k=k, v=v, oe=oe, y=y, ca=ca, z2=z2))
        xt = x_new
    dxo, loss_acc = _loss_head(xt, loss_target[0])
    loss = lax.psum(loss_acc[0, 0], ("x", "y", "c"))

    g_small = {n: [None] * DEPTH for n in _SMALL}
    gw_c, gw_r, gw_q, gw_o, d_mod = [None] * DEPTH, [None] * DEPTH, [None] * DEPTH, [None] * DEPTH, [None] * DEPTH
    for l in reversed(range(DEPTH)):
        s = saved[l]
        dpr, g_a, g_b, doe, gw_o[l], pvec, s256, dws, dbs = _mix_out_bwd(
            dxo, s["y"], s["pr"], s["ca"], s["z2"], s["oe"], modv[l], g_post[l][None, :], wo_full[l], vecs[l], wss[l], wsts[l], bsm[l])
        dpc, dca, dcb = _conv_bwd(s["pc"], g_a, g_b, conv_a_full[l], conv_b_full[l])
        dq, dk, dv = _attention_bwd(s["q"], s["k"], s["v"], doe)
        dpq, dgain = _qkv_bwd(dq, dk, dv, s["pq"], qk_gain[l], cos_t, sin_t)
        dxo, acc = _in_proj_bwd(dpc, dpr, dpq, w_c[l], w_r[l], w_q[l], s["x"], dxo, modv[l], g_pre[l][None, :])
        gw_c[l], gw_r[l], gw_q[l] = _in_proj_wgrad(s["h"], dpc, dpr, dpq)
        d_mod[l] = jnp.stack([jnp.concatenate([acc[2], acc[3], pvec[1]]), jnp.concatenate([acc[0], acc[1], pvec[0]])])
        g_small["g_pre"][l] = acc[4]
        g_small["g_post"][l] = pvec[2]
        g_small["conv_a"][l] = dca[:SHORT_CONV_K]
        g_small["conv_b"][l] = dcb[:CONFORMER_K]
        g_small["conv_b_bias"][l] = s256[0]
        g_small["conf_ln_g"][l] = s256[1]
        g_small["conf_ln_b"][l] = s256[2]
        g_small["sgu_ln_g"][l] = s256[3]
        g_small["sgu_ln_b"][l] = s256[4]
        g_small["w_s"][l] = dws.reshape(N_SPATIAL_GROUPS, CHUNK, CHUNK)
        g_small["b_s"][l] = jnp.transpose(dbs[:, :N_SPATIAL_GROUPS])
        g_small["q_gain"][l] = dgain[0, :HEAD_DIM] + dgain[0, HEAD_DIM:]
        g_small["k_gain"][l] = dgain[1, :HEAD_DIM] + dgain[1, HEAD_DIM:]
    grad_x = dxo[CTX_LEN:][None]

    d_mod_all = _all_gather_rows(jnp.stack(d_mod).reshape(DEPTH * 2, 3 * D_MODEL)).reshape(8, DEPTH, 2, 3 * D_MODEL)
    d_lat = jnp.transpose(d_mod_all[:, :, 0], (1, 0, 2))
    d_ctx = jnp.transpose(d_mod_all[:, :, 1], (1, 0, 2))
    cols = lambda a: lax.dynamic_slice_in_dim(a.reshape(DEPTH, 8, N_CHIPS, SHARD_MOD), chip, 1, axis=2)[:, :, 0]
    silu_t = jnp.transpose(silu_c)
    s_t = jnp.concatenate([silu_t[:, 0:8], jnp.tile(silu_t[:, 8:9], (1, 8)), jnp.zeros((D_MODEL, LANES - 16), F32)], axis=1)
    g_rows = jnp.concatenate([cols(d_lat), cols(d_ctx), jnp.zeros((DEPTH, LANES - 16, SHARD_MOD), F32)], axis=1)
    g_w_mod, g_b_mod, c_ctx_part = _mod_backward(s_t, g_rows, cols(d_ctx), jnp.concatenate([d_lat, d_ctx], axis=1),
                                                 w_mod, c_ctx[:, None])

    for n in _SMALL:
        if n not in ("c_ctx", "b_mod"):
            g_small[n] = jnp.stack(g_small[n])
    small_parts = [0.5 * c_ctx_part[:, 0]] + [g_small[n] for n in _SMALL[2:]]
    packed = _pack(small_parts)
    gathered = _all_gather_rows(packed).reshape(8, packed.shape[0], LANES)
    small_sum = _sum_slabs(gathered, packed.shape[0])
    small_g = dict(zip(("c_ctx",) + _SMALL[2:], _unpack(small_sum, [p.shape for p in small_parts])))
    small_g["b_mod"] = g_b_mod[:, 0]
    ch64 = GROUP_W // N_CHIPS
    for n in ("conv_a", "conv_b"):
        small_g[n] = lax.dynamic_slice_in_dim(small_g[n], chip * ch64, ch64, axis=2)
    sw = _pack([weights[n] for n in _SMALL])
    sm = _pack([m_in[n] for n in _SMALL])
    sv = _pack([v_in[n] for n in _SMALL])
    sg = _pack([small_g[n] for n in _SMALL])
    shapes = [weights[n].shape for n in _SMALL]
    small_out = [dict(zip(_SMALL, _unpack(o, shapes))) for o in _adamw([sg], sw, sm, sv, sg.shape[0])]

    gw_c, gw_r, gw_q, gw_o = jnp.stack(gw_c), jnp.stack(gw_r), jnp.stack(gw_q), jnp.stack(gw_o)
    gw_in = jnp.concatenate([gw_r[..., 0:256], gw_c[..., 0:512], gw_r[..., 256:512], gw_c[..., 512:1024],
                             gw_r[..., 512:1536], gw_q, gw_r[..., 1536:1792]], axis=-1)
    slabs_in = jnp.transpose(gw_in.reshape(DEPTH, D_MODEL, N_CHIPS, SHARD_IN), (2, 0, 1, 3))
    slabs_out = jnp.transpose(gw_o.reshape(DEPTH, N_CHIPS, SHARD_OUT, D_MODEL), (1, 0, 2, 3))
    recv_in, recv_out = _scatter_slabs(slabs_in, slabs_out)
    sum_in = _sum_slabs(recv_in.reshape(N_CHIPS, DEPTH * D_MODEL, SHARD_IN), 512)
    sum_out = _sum_slabs(recv_out.reshape(N_CHIPS, DEPTH * SHARD_OUT, D_MODEL), 256)
    sib_in, sib_out = _swap_with_sibling(sum_in, sum_out)

    big = {}
    flat = lambda a: a.reshape(-1, a.shape[-1])
    for n, grads, rows in (("w_in", [sum_in, sib_in], 512), ("w_out", [sum_out, sib_out], 256), ("w_mod", [flat(g_w_mod)], 512)):
        outs = _adamw(grads, flat(weights[n]), flat(m_in[n]), flat(v_in[n]), rows)
        big[n] = [o.reshape(weights[n].shape) for o in outs]

    def leaf(n, j):
        return big[n][j] if n in big else small_out[j][n]

    return (loss, grad_x, *[leaf(n, 0) for n in order], *[leaf(n, 1) for n in order],
            *[leaf(n, 2) for n in order], *[leaf(n, 3) for n in order])
```

```python
import functools

import numpy as np
import jax
import jax.numpy as jnp
from jax import lax
from jax.experimental import pallas as pl
from jax.experimental.pallas import tpu as pltpu

F32 = jnp.float32
BF16 = jnp.bfloat16
MESH = pl.DeviceIdType.MESH

D_MODEL = 1024
DEPTH = 4
GRID_W = 64
CTX_LEN = 256
GROUP_W = 256
HEAD_DIM = 64
N_Q_HEADS = 4
N_KV_HEADS = 2
GQA = N_Q_HEADS // N_KV_HEADS
ROPE_THETA = 10000.0
ATTN_SCALE = HEAD_DIM ** -0.5
SHORT_CONV_K = 3
CONFORMER_K = 31
CHUNK = 128
N_SPATIAL_GROUPS = 4
RMS_EPS = 1e-6
LN_EPS = 1e-5
ADAM_LR = 0.001
ADAM_B1 = 0.9
ADAM_B2 = 0.999
ADAM_EPS = 1e-08
ADAM_WD = 0.01
ADAM_STEP = 10

LANES = 128
HALO = 16
TM = 256
N_CTX_TILES = CTX_LEN // TM
W_C = 1024
W_R = 1792
W_Q = 512
PROJ_W = W_C + W_R + W_Q
N_CHIPS = 4
SHARD_IN = PROJ_W // N_CHIPS
SHARD_OUT = D_MODEL // N_CHIPS
SHARD_MOD = 3 * D_MODEL // N_CHIPS
VMEM_LIMIT = 56 * 1024 * 1024


def _pc(body, **kw):
    return pl.pallas_call(body, **kw)


def _cparams(**kw):
    return pltpu.CompilerParams(dimension_semantics=("arbitrary",), vmem_limit_bytes=VMEM_LIMIT, **kw)


def _full(shape):
    n = len(shape)
    return pl.BlockSpec(shape, lambda i: (0,) * n)


def _rows(width, tm=TM):
    return pl.BlockSpec((tm, width), lambda i: (i, 0))


def _heads(nh, width, tm=TM):
    return pl.BlockSpec((nh, tm, width), lambda i: (0, i, 0))


def _sigmoid(x):
    return jax.nn.sigmoid(x)


def _dot(a, b):
    return jnp.dot(a, b, preferred_element_type=F32)


def _dot_nt(a, b):
    return lax.dot_general(a, b, (((1,), (1,)), ((), ())), preferred_element_type=F32)


def _dot_tn(a, b):
    return lax.dot_general(a, b, (((0,), (0,)), ((), ())), preferred_element_type=F32)


def _lane(rows):
    return lax.broadcasted_iota(jnp.int32, (rows, LANES), 1)


def _rowsum(x):
    return jnp.sum(x, axis=1, keepdims=True)


def _colsum(x):
    return jnp.sum(x, axis=0, keepdims=True)


def _pair_sums(x, lo):
    s0 = _rowsum(jnp.where(lo, x, 0.0))
    s1 = _rowsum(jnp.where(lo, 0.0, x))
    return jnp.where(lo, s0, s1)


def _swap16(x, lo16):
    return jnp.where(lo16, pltpu.roll(x, LANES - 16, 1), pltpu.roll(x, 16, 1))


def _layer_norm_stats(x):
    mu = jnp.mean(x, axis=1, keepdims=True)
    xc = x - mu
    rs = lax.rsqrt(jnp.mean(xc * xc, axis=1, keepdims=True) + LN_EPS)
    return xc * rs, rs


def _layer_norm_bwd(dxn, xn, rs):
    return rs * (dxn - jnp.mean(dxn, axis=1, keepdims=True) - xn * jnp.mean(dxn * xn, axis=1, keepdims=True))


def _group_select(r, grp):
    out = jnp.where(grp == 0, r[0:CHUNK], 0.0)
    for g in range(1, N_SPATIAL_GROUPS):
        out = out + jnp.where(grp == g, r[g * CHUNK:(g + 1) * CHUNK], 0.0)
    return out


def _kv_chunk(s_lat):
    return 1024 if s_lat % 1024 == 0 else 256


def _all_gather_rows(x_shard):
    m_per, n = x_shard.shape

    def body(x_ref, out_ref, send_sems, recv_sems, local_sem):
        x, y, c = lax.axis_index("x"), lax.axis_index("y"), lax.axis_index("c")
        me, sibling = (x, y, c), (x, y, 1 - c)
        chips = [(1 - x, y), (x, 1 - y), (1 - x, 1 - y)]

        def rows(px, py, pc):
            return out_ref.at[pl.ds((4 * px + 2 * py + pc) * m_per, m_per), :]

        def copy(k, block, to, src=None):
            return pltpu.make_async_remote_copy(
                src_ref=rows(*block) if src is None else src, dst_ref=rows(*block),
                send_sem=send_sems.at[k], recv_sem=recv_sems.at[k], device_id=to, device_id_type=MESH)

        mine = pltpu.make_async_copy(x_ref, rows(*me), local_sem)
        mine.start()
        first = [copy(0, me, sibling, src=x_ref)]
        first += [copy(1 + j, me, (*chip, c), src=x_ref) for j, chip in enumerate(chips)]
        for cp in first:
            cp.start()
        passed = [copy(4 + j, (*chip, c), sibling) for j, chip in enumerate(chips)]
        for j, chip in enumerate(chips):
            copy(1 + j, (*chip, c), me).wait_recv()
            passed[j].start()
        copy(0, sibling, me).wait_recv()
        for j, chip in enumerate(chips):
            copy(4 + j, (*chip, 1 - c), me).wait_recv()
        for cp in first + passed:
            cp.wait_send()
        mine.wait()

    return _pc(
        body, name="all_gather_rows",
        out_shape=jax.ShapeDtypeStruct((8 * m_per, n), x_shard.dtype),
        in_specs=[pl.BlockSpec(memory_space=pltpu.VMEM)],
        out_specs=pl.BlockSpec(memory_space=pltpu.VMEM),
        scratch_shapes=[pltpu.SemaphoreType.DMA((7,)), pltpu.SemaphoreType.DMA((7,)), pltpu.SemaphoreType.DMA],
        compiler_params=pltpu.CompilerParams(vmem_limit_bytes=VMEM_LIMIT),
    )(x_shard)


def _gather_weights(wi, wo):
    half = DEPTH // 2

    def body(wi_ref, wo_ref, gi_ref, go_ref, send_sems, recv_sems, local_sems):
        x, y, c = lax.axis_index("x"), lax.axis_index("y"), lax.axis_index("c")
        kme = 2 * x + y
        sibling = (x, y, 1 - c)
        chips = [(1 - x, y), (x, 1 - y), (1 - x, 1 - y)]
        mine = pl.ds(c * half, half)
        theirs = pl.ds((1 - c) * half, half)
        sends, locals_ = [], []
        for a, (src, dst) in enumerate(((wi_ref, gi_ref), (wo_ref, go_ref))):
            loc = pltpu.make_async_copy(src, dst.at[kme], local_sems.at[a])
            loc.start()
            locals_.append(loc)
            for j, (px, py) in enumerate(chips):
                cp = pltpu.make_async_remote_copy(
                    src_ref=src.at[mine], dst_ref=dst.at[kme, mine],
                    send_sem=send_sems.at[6 * a + j], recv_sem=recv_sems.at[6 * a + j],
                    device_id=(px, py, c), device_id_type=MESH)
                cp.start()
                sends.append(cp)
        for a, (src, dst) in enumerate(((wi_ref, gi_ref), (wo_ref, go_ref))):
            for j, (px, py) in enumerate(chips):
                kk = 2 * px + py
                landed = dst.at[kk, mine]
                pltpu.make_async_remote_copy(
                    src_ref=landed, dst_ref=landed, send_sem=send_sems.at[6 * a + j], recv_sem=recv_sems.at[6 * a + j],
                    device_id=(px, py, c), device_id_type=MESH).wait_recv()
                fw = pltpu.make_async_remote_copy(
                    src_ref=landed, dst_ref=landed, send_sem=send_sems.at[6 * a + 3 + j], recv_sem=recv_sems.at[6 * a + 3 + j],
                    device_id=sibling, device_id_type=MESH)
                fw.start()
                sends.append(fw)
        for a, (src, dst) in enumerate(((wi_ref, gi_ref), (wo_ref, go_ref))):
            for j, (px, py) in enumerate(chips):
                kk = 2 * px + py
                other = dst.at[kk, theirs]
                pltpu.make_async_remote_copy(
                    src_ref=other, dst_ref=other, send_sem=send_sems.at[6 * a + 3 + j], recv_sem=recv_sems.at[6 * a + 3 + j],
                    device_id=sibling, device_id_type=MESH).wait_recv()
        for cp in sends:
            cp.wait_send()
        for loc in locals_:
            loc.wait()

    hbm = pl.BlockSpec(memory_space=pl.ANY)
    return _pc(
        body, name="gather_weights",
        out_shape=(jax.ShapeDtypeStruct((N_CHIPS,) + wi.shape, wi.dtype), jax.ShapeDtypeStruct((N_CHIPS,) + wo.shape, wo.dtype)),
        in_specs=[hbm, hbm], out_specs=(hbm, hbm),
        scratch_shapes=[pltpu.SemaphoreType.DMA((12,)), pltpu.SemaphoreType.DMA((12,)), pltpu.SemaphoreType.DMA((2,))],
    )(wi, wo)


def _scatter_slabs(gi, go):
    def body(gi_ref, go_ref, ri_ref, ro_ref, send_sems, recv_sems, local_sems):
        x, y, c = lax.axis_index("x"), lax.axis_index("y"), lax.axis_index("c")
        kme = 2 * x + y
        chips = [(1 - x, y), (x, 1 - y), (1 - x, 1 - y)]
        sends, locals_ = [], []
        for a, (src, dst) in enumerate(((gi_ref, ri_ref), (go_ref, ro_ref))):
            loc = pltpu.make_async_copy(src.at[kme], dst.at[kme], local_sems.at[a])
            loc.start()
            locals_.append(loc)
            for j, (px, py) in enumerate(chips):
                cp = pltpu.make_async_remote_copy(
                    src_ref=src.at[2 * px + py], dst_ref=dst.at[kme],
                    send_sem=send_sems.at[3 * a + j], recv_sem=recv_sems.at[3 * a + j],
                    device_id=(px, py, c), device_id_type=MESH)
                cp.start()
                sends.append(cp)
        for a, (src, dst) in enumerate(((gi_ref, ri_ref), (go_ref, ro_ref))):
            for j, (px, py) in enumerate(chips):
                landed = dst.at[2 * px + py]
                pltpu.make_async_remote_copy(
                    src_ref=landed, dst_ref=landed, send_sem=send_sems.at[3 * a + j], recv_sem=recv_sems.at[3 * a + j],
                    device_id=(px, py, c), device_id_type=MESH).wait_recv()
        for cp in sends:
            cp.wait_send()
        for loc in locals_:
            loc.wait()

    hbm = pl.BlockSpec(memory_space=pl.ANY)
    return _pc(
        body, name="scatter_slabs",
        out_shape=(jax.ShapeDtypeStruct(gi.shape, gi.dtype), jax.ShapeDtypeStruct(go.shape, go.dtype)),
        in_specs=[hbm, hbm], out_specs=(hbm, hbm),
        scratch_shapes=[pltpu.SemaphoreType.DMA((6,)), pltpu.SemaphoreType.DMA((6,)), pltpu.SemaphoreType.DMA((2,))],
    )(gi, go)


def _swap_with_sibling(a, b):
    def body(a_ref, b_ref, ra_ref, rb_ref, send_sems, recv_sems):
        x, y, c = lax.axis_index("x"), lax.axis_index("y"), lax.axis_index("c")
        copies = []
        for k, (src, dst) in enumerate(((a_ref, ra_ref), (b_ref, rb_ref))):
            cp = pltpu.make_async_remote_copy(
                src_ref=src, dst_ref=dst, send_sem=send_sems.at[k], recv_sem=recv_sems.at[k],
                device_id=(x, y, 1 - c), device_id_type=MESH)
            cp.start()
            copies.append(cp)
        for cp in copies:
            cp.wait()

    hbm = pl.BlockSpec(memory_space=pl.ANY)
    return _pc(
        body, name="swap_with_sibling",
        out_shape=(jax.ShapeDtypeStruct(a.shape, a.dtype), jax.ShapeDtypeStruct(b.shape, b.dtype)),
        in_specs=[hbm, hbm], out_specs=(hbm, hbm),
        scratch_shapes=[pltpu.SemaphoreType.DMA((2,)), pltpu.SemaphoreType.DMA((2,))],
    )(a, b)


def _mod_forward(c16, w_mod, b_mod_shard):
    def body(c_ref, w_ref, b_ref, s_ref, o_ref):
        cc = c_ref[...]
        s = cc * _sigmoid(cc)
        s_ref[...] = s
        o_ref[0] = jnp.dot(s, w_ref[0], preferred_element_type=F32, precision=lax.Precision.HIGHEST) + b_ref[0]

    return _pc(
        body, name="mod_forward", grid=(DEPTH,),
        out_shape=(jax.ShapeDtypeStruct((16, D_MODEL), F32), jax.ShapeDtypeStruct((DEPTH, 16, SHARD_MOD), F32)),
        in_specs=[_full((16, D_MODEL)),
                  pl.BlockSpec((1, D_MODEL, SHARD_MOD), lambda l: (l, 0, 0)),
                  pl.BlockSpec((1, 1, SHARD_MOD), lambda l: (l, 0, 0))],
        out_specs=(_full((16, D_MODEL)), pl.BlockSpec((1, 16, SHARD_MOD), lambda l: (l, 0, 0))),
        compiler_params=_cparams(),
    )(c16, w_mod, b_mod_shard)


def _mod_backward(s_t, g_rows, g_ctx, d_all, w_mod, c_ctx_col):
    def body(st_ref, g_ref, gc_ref, d_ref, w_ref, cc_ref, gw_ref, gb_ref, pc_ref):
        l = pl.program_id(0)
        gw_ref[0] = jnp.dot(st_ref[...], g_ref[0], preferred_element_type=F32, precision=lax.Precision.HIGHEST)
        gb_ref[0] = _colsum(d_ref[0])
        part = _rowsum(w_ref[0] * _colsum(gc_ref[0]))

        @pl.when(l == 0)
        def _():
            pc_ref[...] = jnp.zeros_like(pc_ref)

        pc_ref[...] += part

        @pl.when(l == DEPTH - 1)
        def _():
            cc = cc_ref[...]
            sg = _sigmoid(cc)
            pc_ref[...] = pc_ref[...] * (sg * (1.0 + cc * (1.0 - sg)))

    return _pc(
        body, name="mod_backward", grid=(DEPTH,),
        out_shape=(jax.ShapeDtypeStruct((DEPTH, D_MODEL, SHARD_MOD), F32),
                   jax.ShapeDtypeStruct((DEPTH, 1, 3 * D_MODEL), F32),
                   jax.ShapeDtypeStruct((D_MODEL, 1), F32)),
        in_specs=[_full((D_MODEL, LANES)),
                  pl.BlockSpec((1, LANES, SHARD_MOD), lambda l: (l, 0, 0)),
                  pl.BlockSpec((1, 8, SHARD_MOD), lambda l: (l, 0, 0)),
                  pl.BlockSpec((1, 16, 3 * D_MODEL), lambda l: (l, 0, 0)),
                  pl.BlockSpec((1, D_MODEL, SHARD_MOD), lambda l: (l, 0, 0)),
                  _full((D_MODEL, 1))],
        out_specs=(pl.BlockSpec((1, D_MODEL, SHARD_MOD), lambda l: (l, 0, 0)),
                   pl.BlockSpec((1, 1, 3 * D_MODEL), lambda l: (l, 0, 0)),
                   _full((D_MODEL, 1))),
        compiler_params=_cparams(),
    )(s_t, g_rows, g_ctx, d_all, w_mod, c_ctx_col)


def _head_norm(xb, lo):
    r = lax.rsqrt(_pair_sums(xb * xb, lo) * (1.0 / HEAD_DIM) + RMS_EPS)
    return xb * r, r


def _in_proj(xt, modv, g_pre, w_c, w_r, w_q, qk_gain, cos_t, sin_t):
    t = xt.shape[0]

    def body(x_ref, mod_ref, g_ref, wc_ref, wr_ref, wq_ref, gain_ref, cos_ref, sin_ref,
             h_ref, pc_ref, pr_ref, pq_ref, q_ref, k_ref, v_ref):
        is_ctx = pl.program_id(0) < N_CTX_TILES
        x = x_ref[...]
        r = lax.rsqrt(jnp.mean(x * x, axis=1, keepdims=True) + RMS_EPS)
        sh = jnp.where(is_ctx, mod_ref[0:1, :], mod_ref[3:4, :])
        sc = jnp.where(is_ctx, mod_ref[1:2, :], mod_ref[4:5, :])
        h = (x * r * g_ref[...]) * (1.0 + sc) + sh
        hb = h.astype(BF16)
        h_ref[...] = hb
        pc_ref[...] = _dot(hb, wc_ref[...])
        pr_ref[...] = _dot(hb, wr_ref[...])
        pq = _dot(hb, wq_ref[...])
        pq_ref[...] = pq
        lane = _lane(TM)
        lo = lane < HEAD_DIM
        lo16 = (lane & 31) < 16
        cos = cos_ref[...]
        sin = sin_ref[...]
        for b in range(3):
            xh, _ = _head_norm(pq[:, b * LANES:(b + 1) * LANES], lo)
            xg = xh * (gain_ref[0:1, :] if b < 2 else gain_ref[1:2, :])
            rot = xg * cos + _swap16(xg, lo16) * sin
            if b < 2:
                rot = rot * ATTN_SCALE
            dst = q_ref if b < 2 else k_ref
            base = 2 * b if b < 2 else 0
            dst[base] = jnp.where(lo, rot, 0.0).astype(BF16)
            dst[base + 1] = jnp.where(lo, pltpu.roll(rot, HEAD_DIM, 1), 0.0).astype(BF16)
        vb = pq[:, 3 * LANES:4 * LANES]
        one = jnp.where(lane == HEAD_DIM, 1.0, 0.0)
        v_ref[0] = jnp.where(lo, vb, one).astype(BF16)
        v_ref[1] = jnp.where(lo, pltpu.roll(vb, HEAD_DIM, 1), one).astype(BF16)

    return _pc(
        body, name="in_proj", grid=(t // TM,),
        out_shape=(jax.ShapeDtypeStruct((t, D_MODEL), BF16),
                   jax.ShapeDtypeStruct((t, W_C), F32), jax.ShapeDtypeStruct((t, W_R), F32), jax.ShapeDtypeStruct((t, W_Q), F32),
                   jax.ShapeDtypeStruct((N_Q_HEADS, t, LANES), BF16),
                   jax.ShapeDtypeStruct((N_KV_HEADS, t, LANES), BF16),
                   jax.ShapeDtypeStruct((N_KV_HEADS, t, LANES), BF16)),
        in_specs=[_rows(D_MODEL), _full((8, D_MODEL)), _full((1, D_MODEL)),
                  _full((D_MODEL, W_C)), _full((D_MODEL, W_R)), _full((D_MODEL, W_Q)),
                  _full((8, LANES)), _rows(LANES), _rows(LANES)],
        out_specs=(_rows(D_MODEL), _rows(W_C), _rows(W_R), _rows(W_Q),
                   _heads(N_Q_HEADS, LANES), _heads(N_KV_HEADS, LANES), _heads(N_KV_HEADS, LANES)),
        compiler_params=_cparams(),
    )(xt, modv, g_pre, w_c, w_r, w_q, qk_gain, cos_t, sin_t)


def _attention_fwd(q, k, v):
    t = q.shape[1]
    tkl = _kv_chunk(t - CTX_LEN)
    n_lat = (t - CTX_LEN) // tkl

    def body(q_ref, k_ref, v_ref, o_ref):
        i = pl.program_id(0)
        steps = jnp.where(i < N_CTX_TILES, 0, n_lat)
        lane = _lane(GQA * TM)
        qs = [jnp.concatenate([q_ref[GQA * g + hh] for hh in range(GQA)], axis=0) for g in range(N_KV_HEADS)]

        def step(st, size, carry):
            out = []
            for g in range(N_KV_HEADS):
                m, acc = carry[g]
                s = _dot_nt(qs[g], k_ref[g, pl.ds(st, size), :])
                m_new = jnp.maximum(m, jnp.max(s, axis=1, keepdims=True))
                p = jnp.exp(s - m_new)
                out.append((m_new, acc * jnp.exp(m - m_new) + _dot(p.astype(BF16), v_ref[g, pl.ds(st, size), :])))
            return tuple(out)

        init = tuple((jnp.full((GQA * TM, 1), -jnp.inf, F32), jnp.zeros((GQA * TM, LANES), F32)) for _ in range(N_KV_HEADS))
        carry = step(0, CTX_LEN, init)
        carry = lax.fori_loop(0, steps, lambda j, cr: step(pl.multiple_of(CTX_LEN + j * tkl, 256), tkl, cr), carry)
        for g in range(N_KV_HEADS):
            m, acc = carry[g]
            den = _rowsum(jnp.where(lane == HEAD_DIM, acc, 0.0))
            out = jnp.where(lane < HEAD_DIM, acc * (1.0 / den), jnp.where(lane == HEAD_DIM, m + jnp.log(den), 0.0))
            for hh in range(GQA):
                o_ref[GQA * g + hh] = out[hh * TM:(hh + 1) * TM]

    return _pc(
        body, name="attention_fwd", grid=(t // TM,),
        out_shape=jax.ShapeDtypeStruct((N_Q_HEADS, t, LANES), F32),
        in_specs=[_heads(N_Q_HEADS, LANES), _full((N_KV_HEADS, t, LANES)), _full((N_KV_HEADS, t, LANES))],
        out_specs=_heads(N_Q_HEADS, LANES),
        compiler_params=_cparams(),
    )(q, k, v)


def _halo_specs(width, t):
    last = t // HALO - 1
    per = TM // HALO
    prev = pl.BlockSpec((HALO, width), lambda i: (jnp.maximum(i * per - 1, 0), 0))
    nxt = pl.BlockSpec((HALO, width), lambda i: (jnp.minimum((i + 1) * per, last), 0))
    return prev, nxt


def _halo_valid(i, n_tiles):
    prev_ok = jnp.logical_and(i != 0, i != N_CTX_TILES)
    next_ok = jnp.logical_and(i != N_CTX_TILES - 1, i != n_tiles - 1)
    return jnp.where(prev_ok, 1.0, 0.0), jnp.where(next_ok, 1.0, 0.0)


def _conv_inputs(pc):
    u = pc[:, 0:GROUP_W] * pc[:, GROUP_W:2 * GROUP_W]
    z = pc[:, 2 * GROUP_W:3 * GROUP_W] * _sigmoid(pc[:, 3 * GROUP_W:4 * GROUP_W])
    return u, z


def _fill_ext(ext_ref, prev, mid, nxt):
    ext_ref[0:HALO, :] = prev
    ext_ref[HALO:HALO + TM, :] = mid
    ext_ref[HALO + TM:HALO + TM + HALO, :] = nxt


def _row_local_mixers(pr, ca, z2, oe, vecs, wss_ref, bsm, lane256):
    a_b, a_g, b_g = pr[:, 0:256], pr[:, 256:512], pr[:, 512:768]
    c_u, c_v, c_g, d_g = pr[:, 768:1024], pr[:, 1024:1280], pr[:, 1280:1536], pr[:, 1536:1792]
    zn, rs_b = _layer_norm_stats(z2)
    tb = zn * vecs[1:2, :] + vecs[2:3, :]
    vn_hat, rs_c = _layer_norm_stats(c_v)
    vn = vn_hat * vecs[3:4, :] + vecs[4:5, :]
    grp = jnp.right_shift(lane256, 6)
    sgs = []
    for ch in range(TM // CHUNK):
        r = _dot(wss_ref[...], vn[ch * CHUNK:(ch + 1) * CHUNK, :].astype(BF16))
        sgs.append(_group_select(r, grp[0:CHUNK]) + bsm)
    sg = jnp.concatenate(sgs, axis=0)
    lane = _lane(TM)
    lo = lane < HEAD_DIM
    att = jnp.concatenate([jnp.where(lo, oe[2 * b], pltpu.roll(oe[2 * b + 1], HEAD_DIM, 1)) for b in range(2)], axis=1)
    return dict(a_b=a_b, a_g=a_g, b_g=b_g, c_u=c_u, c_v=c_v, c_g=c_g, d_g=d_g, zn=zn, rs_b=rs_b, tb=tb,
                vn_hat=vn_hat, rs_c=rs_c, vn=vn, sg=sg, att=att, grp=grp, lo=lo, lane=lane)


def _mixer_concat(f, ca):
    ya = f["a_b"] * ca
    yb = f["tb"] * _sigmoid(f["tb"])
    yc = f["c_u"] * f["sg"]
    gates = [f[n] * _sigmoid(f[n]) for n in ("a_g", "b_g", "c_g", "d_g")]
    ys = (ya, yb, yc, f["att"])
    big = jnp.concatenate([yy * gg for yy, gg in zip(ys, gates)], axis=1).astype(BF16)
    return big, ys, gates


def _mix_out(pc, pr, oe, xt, modv, g_post, w_out, conv_a, conv_b, vecs, wss, bsm):
    t = xt.shape[0]
    n_tiles = t // TM
    prev_spec, next_spec = _halo_specs(W_C, t)

    def body(pc_ref, pp_ref, pn_ref, pr_ref, oe_ref, x_ref, mod_ref, gp_ref, wo_ref, cva_ref, cvb_ref, vec_ref, wss_ref, bsm_ref,
             xo_ref, y_ref, ca_ref, z2_ref, uext, zext):
        i = pl.program_id(0)
        is_ctx = i < N_CTX_TILES
        pv, nv = _halo_valid(i, n_tiles)
        u, z = _conv_inputs(pc_ref[...])
        up, zp = _conv_inputs(pp_ref[...])
        un, zn_ = _conv_inputs(pn_ref[...])
        _fill_ext(uext, up * pv, u, un * nv)
        _fill_ext(zext, zp * pv, z, zn_ * nv)
        ca = cva_ref[0:1, :] * uext[pl.ds(HALO - 1, TM), :]
        for kk in range(1, SHORT_CONV_K):
            ca = ca + cva_ref[kk:kk + 1, :] * uext[pl.ds(HALO - 1 + kk, TM), :]
        z2 = cvb_ref[0:1, :] * zext[pl.ds(HALO - CONFORMER_K // 2, TM), :]
        for kk in range(1, CONFORMER_K):
            z2 = z2 + cvb_ref[kk:kk + 1, :] * zext[pl.ds(HALO - CONFORMER_K // 2 + kk, TM), :]
        vecs = vec_ref[...]
        z2 = z2 + vecs[0:1, :]
        ca_ref[...] = ca
        z2_ref[...] = z2
        lane256 = lax.broadcasted_iota(jnp.int32, (TM, GROUP_W), 1)
        f = _row_local_mixers(pr_ref[...], ca, z2, oe_ref, vecs, wss_ref, bsm_ref[...], lane256)
        big, _, _ = _mixer_concat(f, ca)
        y = _dot(big, wo_ref[...])
        y_ref[...] = y
        ry = lax.rsqrt(jnp.mean(y * y, axis=1, keepdims=True) + RMS_EPS)
        gt = jnp.where(is_ctx, mod_ref[2:3, :], mod_ref[5:6, :])
        xo_ref[...] = x_ref[...] + gt * (y * ry * gp_ref[...])

    return _pc(
        body, name="mix_out", grid=(n_tiles,),
        out_shape=(jax.ShapeDtypeStruct((t, D_MODEL), F32), jax.ShapeDtypeStruct((t, D_MODEL), F32),
                   jax.ShapeDtypeStruct((t, GROUP_W), F32), jax.ShapeDtypeStruct((t, GROUP_W), F32)),
        in_specs=[_rows(W_C), prev_spec, next_spec, _rows(W_R), _heads(N_Q_HEADS, LANES), _rows(D_MODEL),
                  _full((8, D_MODEL)), _full((1, D_MODEL)), _full((D_MODEL, D_MODEL)),
                  _full((8, GROUP_W)), _full((32, GROUP_W)), _full((8, GROUP_W)),
                  _full((N_SPATIAL_GROUPS * CHUNK, CHUNK)), _full((CHUNK, GROUP_W))],
        out_specs=(_rows(D_MODEL), _rows(D_MODEL), _rows(GROUP_W), _rows(GROUP_W)),
        scratch_shapes=[pltpu.VMEM((TM + 2 * HALO, GROUP_W), F32), pltpu.VMEM((TM + 2 * HALO, GROUP_W), F32)],
        compiler_params=_cparams(),
    )(pc, pc, pc, pr, oe, xt, modv, g_post, w_out, conv_a, conv_b, vecs, wss, bsm)


def _loss_head(xt, target):
    t = xt.shape[0]

    def body(x_ref, t_ref, dx_ref, loss_ref):
        i = pl.program_id(0)

        @pl.when(i == 0)
        def _():
            loss_ref[...] = jnp.zeros_like(loss_ref)

        lat = jnp.where(i < N_CTX_TILES, 0.0, 1.0)
        err = (x_ref[...] - t_ref[...]) * lat
        dx_ref[...] = err * (1.0 / D_MODEL)
        loss_ref[...] += jnp.sum(err * err) * (0.5 / D_MODEL)

    return _pc(
        body, name="loss_head", grid=(t // TM,),
        out_shape=(jax.ShapeDtypeStruct((t, D_MODEL), F32), jax.ShapeDtypeStruct((8, LANES), F32)),
        in_specs=[_rows(D_MODEL), pl.BlockSpec((TM, D_MODEL), lambda i: (jnp.maximum(i - N_CTX_TILES, 0), 0))],
        out_specs=(_rows(D_MODEL), _full((8, LANES))),
        compiler_params=_cparams(),
    )(xt, target)


def _mix_out_bwd(dxo, y, pr, ca, z2, oe, modv, g_post, w_out, vecs, wss, wsts, bsm):
    t = y.shape[0]
    n_tiles = t // TM

    def body(dxo_ref, y_ref, pr_ref, ca_ref, z2_ref, oe_ref, mod_ref, gp_ref, wo_ref, vec_ref, wss_ref, wsts_ref, bsm_ref,
             dpr_ref, ga_ref, gb_ref, doe_ref, dwo_ref, pvec_ref, s256_ref, dws_ref, dbs_ref, dbsm):
        i = pl.program_id(0)
        is_ctx = i < N_CTX_TILES

        @pl.when(i == 0)
        def _():
            dwo_ref[...] = jnp.zeros_like(dwo_ref)
            pvec_ref[...] = jnp.zeros_like(pvec_ref)
            s256_ref[...] = jnp.zeros_like(s256_ref)
            dws_ref[...] = jnp.zeros_like(dws_ref)
            dbsm[...] = jnp.zeros_like(dbsm)

        dxo_ = dxo_ref[...]
        y_ = y_ref[...]
        ry = lax.rsqrt(jnp.mean(y_ * y_, axis=1, keepdims=True) + RMS_EPS)
        nh = y_ * ry
        gp = gp_ref[...]
        gt = jnp.where(is_ctx, mod_ref[2:3, :], mod_ref[5:6, :])
        dgt = _colsum(dxo_ * (nh * gp))
        pvec_ref[0:1, :] += jnp.where(is_ctx, dgt, 0.0)
        pvec_ref[1:2, :] += jnp.where(is_ctx, 0.0, dgt)
        dn = dxo_ * gt
        pvec_ref[2:3, :] += _colsum(dn * nh)
        dnh = dn * gp
        dy = ry * (dnh - nh * jnp.mean(dnh * nh, axis=1, keepdims=True))

        vecs = vec_ref[...]
        bsm_ = bsm_ref[...]
        ca_ = ca_ref[...]
        lane256 = lax.broadcasted_iota(jnp.int32, (TM, GROUP_W), 1)
        f = _row_local_mixers(pr_ref[...], ca_, z2_ref[...], oe_ref, vecs, wss_ref, bsm_, lane256)
        big, ys, gates = _mixer_concat(f, ca_)
        dyb = dy.astype(BF16)
        dwo_ref[...] += _dot_tn(big, dyb)
        dbig = _dot_nt(dyb, wo_ref[...])

        d_y, d_gate = [], []
        for n, (name, yy, gg) in enumerate(zip(("a_g", "b_g", "c_g", "d_g"), ys, gates)):
            dpart = dbig[:, n * GROUP_W:(n + 1) * GROUP_W]
            gx = f[name]
            sg_ = _sigmoid(gx)
            d_y.append(dpart * gg)
            d_gate.append(dpart * yy * (sg_ * (1.0 + gx * (1.0 - sg_))))
        dya, dyb_, dyc, datt = d_y

        d_ab = dya * ca_
        ga_ref[...] = dya * f["a_b"]
        tb = f["tb"]
        sb = _sigmoid(tb)
        dtb = dyb_ * (sb * (1.0 + tb * (1.0 - sb)))
        s256_ref[1:2, :] += _colsum(dtb * f["zn"])
        s256_ref[2:3, :] += _colsum(dtb)
        dz2 = _layer_norm_bwd(dtb * vecs[1:2, :], f["zn"], f["rs_b"])
        gb_ref[...] = dz2
        s256_ref[0:1, :] += _colsum(dz2)
        d_cu = dyc * f["sg"]
        dsg = dyc * f["c_u"]
        grp = f["grp"]
        dvn_parts = []
        for ch in range(TM // CHUNK):
            rows = slice(ch * CHUNK, (ch + 1) * CHUNK)
            dsg_c = dsg[rows, :]
            dbsm[...] += dsg_c
            vn_c = f["vn"][rows, :].astype(BF16)
            for g in range(N_SPATIAL_GROUPS):
                masked = jnp.where(grp[0:CHUNK] == g, dsg_c, 0.0).astype(BF16)
                dws_ref[g * CHUNK:(g + 1) * CHUNK, :] += _dot_nt(masked, vn_c)
            dvn_parts.append(_group_select(_dot(wsts_ref[...], dsg_c.astype(BF16)), grp[0:CHUNK]))
        dvn = jnp.concatenate(dvn_parts, axis=0)
        s256_ref[3:4, :] += _colsum(dvn * f["vn_hat"])
        s256_ref[4:5, :] += _colsum(dvn)
        d_cv = _layer_norm_bwd(dvn * vecs[3:4, :], f["vn_hat"], f["rs_c"])
        lane, lo = f["lane"], f["lo"]
        att = f["att"]
        for b in range(2):
            da = datt[:, b * LANES:(b + 1) * LANES]
            prod = da * att[:, b * LANES:(b + 1) * LANES]
            for hh in range(2):
                h = 2 * b + hh
                lse = _rowsum(jnp.where(lane == HEAD_DIM, oe_ref[h], 0.0))
                delta = _rowsum(jnp.where(lo, prod, 0.0) if hh == 0 else jnp.where(lo, 0.0, prod))
                dah = da if hh == 0 else pltpu.roll(da, HEAD_DIM, 1)
                doe_ref[h] = jnp.where(lo, dah, jnp.where(lane == HEAD_DIM, delta, jnp.where(lane == HEAD_DIM + 1, lse, 0.0)))

        dpr_ref[...] = jnp.concatenate([d_ab, d_gate[0], d_gate[1], d_cu, d_cv, d_gate[2], d_gate[3]], axis=1).astype(BF16)

        @pl.when(i == n_tiles - 1)
        def _():
            acc = dbsm[...]
            lane128 = _lane(CHUNK)
            out = jnp.zeros((CHUNK, LANES), F32)
            for g in range(N_SPATIAL_GROUPS):
                col = _rowsum(jnp.where(grp[0:CHUNK] == g, acc, 0.0))
                out = out + jnp.where(lane128 == g, col, 0.0)
            dbs_ref[...] = out

    return _pc(
        body, name="mix_out_bwd", grid=(n_tiles,),
        out_shape=(jax.ShapeDtypeStruct((t, W_R), BF16),
                   jax.ShapeDtypeStruct((t, GROUP_W), F32), jax.ShapeDtypeStruct((t, GROUP_W), F32),
                   jax.ShapeDtypeStruct((N_Q_HEADS, t, LANES), F32),
                   jax.ShapeDtypeStruct((D_MODEL, D_MODEL), F32),
                   jax.ShapeDtypeStruct((8, D_MODEL), F32),
                   jax.ShapeDtypeStruct((8, GROUP_W), F32),
                   jax.ShapeDtypeStruct((N_SPATIAL_GROUPS * CHUNK, CHUNK), F32),
                   jax.ShapeDtypeStruct((CHUNK, LANES), F32)),
        in_specs=[_rows(D_MODEL), _rows(D_MODEL), _rows(W_R), _rows(GROUP_W), _rows(GROUP_W), _heads(N_Q_HEADS, LANES),
                  _full((8, D_MODEL)), _full((1, D_MODEL)), _full((D_MODEL, D_MODEL)), _full((8, GROUP_W)),
                  _full((N_SPATIAL_GROUPS * CHUNK, CHUNK)), _full((N_SPATIAL_GROUPS * CHUNK, CHUNK)), _full((CHUNK, GROUP_W))],
        out_specs=(_rows(W_R), _rows(GROUP_W), _rows(GROUP_W), _heads(N_Q_HEADS, LANES),
                   _full((D_MODEL, D_MODEL)), _full((8, D_MODEL)), _full((8, GROUP_W)),
                   _full((N_SPATIAL_GROUPS * CHUNK, CHUNK)), _full((CHUNK, LANES))),
        scratch_shapes=[pltpu.VMEM((CHUNK, GROUP_W), F32)],
        compiler_params=_cparams(),
    )(dxo, y, pr, ca, z2, oe, modv, g_post, w_out, vecs, wss, wsts, bsm)


def _conv_bwd(pc, g_a, g_b, conv_a, conv_b):
    t = pc.shape[0]
    n_tiles = t // TM
    pc_prev, pc_next = _halo_specs(W_C, t)
    g_prev, g_next = _halo_specs(GROUP_W, t)

    def body(pc_ref, pp_ref, pn_ref, ga_ref, gap_ref, gan_ref, gb_ref, gbp_ref, gbn_ref, cva_ref, cvb_ref,
             dpc_ref, dca_ref, dcb_ref, uext, zext, gaext, gbext):
        i = pl.program_id(0)

        @pl.when(i == 0)
        def _():
            dca_ref[...] = jnp.zeros_like(dca_ref)
            dcb_ref[...] = jnp.zeros_like(dcb_ref)

        pv, nv = _halo_valid(i, n_tiles)
        pc_ = pc_ref[...]
        u, z = _conv_inputs(pc_)
        up, zp = _conv_inputs(pp_ref[...])
        un, zn_ = _conv_inputs(pn_ref[...])
        _fill_ext(uext, up * pv, u, un * nv)
        _fill_ext(zext, zp * pv, z, zn_ * nv)
        ga = ga_ref[...]
        gb = gb_ref[...]
        _fill_ext(gaext, gap_ref[...] * pv, ga, gan_ref[...] * nv)
        _fill_ext(gbext, gbp_ref[...] * pv, gb, gbn_ref[...] * nv)

        du = cva_ref[0:1, :] * gaext[pl.ds(HALO + 1, TM), :]
        dca_ref[0:1, :] += _colsum(ga * uext[pl.ds(HALO - 1, TM), :])
        for kk in range(1, SHORT_CONV_K):
            du = du + cva_ref[kk:kk + 1, :] * gaext[pl.ds(HALO + 1 - kk, TM), :]
            dca_ref[kk:kk + 1, :] += _colsum(ga * uext[pl.ds(HALO - 1 + kk, TM), :])
        half = CONFORMER_K // 2
        dz = cvb_ref[0:1, :] * gbext[pl.ds(HALO + half, TM), :]
        dcb_ref[0:1, :] += _colsum(gb * zext[pl.ds(HALO - half, TM), :])
        for kk in range(1, CONFORMER_K):
            dz = dz + cvb_ref[kk:kk + 1, :] * gbext[pl.ds(HALO + half - kk, TM), :]
            dcb_ref[kk:kk + 1, :] += _colsum(gb * zext[pl.ds(HALO - half + kk, TM), :])

        a_c, a_h = pc_[:, 0:GROUP_W], pc_[:, GROUP_W:2 * GROUP_W]
        glu_a, glu_g = pc_[:, 2 * GROUP_W:3 * GROUP_W], pc_[:, 3 * GROUP_W:4 * GROUP_W]
        sg = _sigmoid(glu_g)
        dpc_ref[...] = jnp.concatenate([du * a_h, du * a_c, dz * sg, dz * glu_a * sg * (1.0 - sg)], axis=1).astype(BF16)

    ext = pltpu.VMEM((TM + 2 * HALO, GROUP_W), F32)
    return _pc(
        body, name="conv_bwd", grid=(n_tiles,),
        out_shape=(jax.ShapeDtypeStruct((t, W_C), BF16), jax.ShapeDtypeStruct((8, GROUP_W), F32), jax.ShapeDtypeStruct((32, GROUP_W), F32)),
        in_specs=[_rows(W_C), pc_prev, pc_next, _rows(GROUP_W), g_prev, g_next, _rows(GROUP_W), g_prev, g_next,
                  _full((8, GROUP_W)), _full((32, GROUP_W))],
        out_specs=(_rows(W_C), _full((8, GROUP_W)), _full((32, GROUP_W))),
        scratch_shapes=[ext, ext, ext, ext],
        compiler_params=_cparams(),
    )(pc, pc, pc, g_a, g_a, g_a, g_b, g_b, g_b, conv_a, conv_b)


def _attention_bwd(q, k, v, doe):
    t = q.shape[1]
    tkl = _kv_chunk(t - CTX_LEN)
    n_lat = (t - CTX_LEN) // tkl
    n_tiles = t // TM

    def body(q_ref, do_ref, k_ref, v_ref, dq_ref, dk_hbm, dv_hbm, dk_acc, dv_acc):
        i = pl.program_id(0)

        @pl.when(i == 0)
        def _():
            dk_acc[...] = jnp.zeros_like(dk_acc)
            dv_acc[...] = jnp.zeros_like(dv_acc)

        steps = jnp.where(i < N_CTX_TILES, 0, n_lat)
        lane = _lane(GQA * TM)
        lo = lane < HEAD_DIM
        qs, dos, deltas, lses = [], [], [], []
        for g in range(N_KV_HEADS):
            qs.append(jnp.concatenate([q_ref[GQA * g + hh] for hh in range(GQA)], axis=0))
            dog = jnp.concatenate([do_ref[GQA * g + hh] for hh in range(GQA)], axis=0)
            deltas.append(_rowsum(jnp.where(lane == HEAD_DIM, dog, 0.0)))
            lses.append(_rowsum(jnp.where(lane == HEAD_DIM + 1, dog, 0.0)))
            dos.append(jnp.where(lo, dog, 0.0).astype(BF16))

        def step(st, size, dqs):
            out = []
            for g in range(N_KV_HEADS):
                kc = k_ref[g, pl.ds(st, size), :]
                vc = v_ref[g, pl.ds(st, size), :]
                p = jnp.exp(_dot_nt(qs[g], kc) - lses[g])
                ds_ = (p * (_dot_nt(dos[g], vc) - deltas[g])).astype(BF16)
                dk_acc[g, pl.ds(st, size), :] += _dot_tn(ds_, qs[g])
                dv_acc[g, pl.ds(st, size), :] += _dot_tn(p.astype(BF16), dos[g])
                out.append(dqs[g] + _dot(ds_, kc))
            return tuple(out)

        dqs = step(0, CTX_LEN, tuple(jnp.zeros((GQA * TM, LANES), F32) for _ in range(N_KV_HEADS)))
        dqs = lax.fori_loop(0, steps, lambda j, acc: step(pl.multiple_of(CTX_LEN + j * tkl, 256), tkl, acc), dqs)
        for g in range(N_KV_HEADS):
            for hh in range(GQA):
                dq_ref[GQA * g + hh] = dqs[g][hh * TM:(hh + 1) * TM]

        @pl.when(i == n_tiles - 1)
        def _():
            pltpu.sync_copy(dk_acc, dk_hbm)
            pltpu.sync_copy(dv_acc, dv_hbm)

    kv_shape = jax.ShapeDtypeStruct((N_KV_HEADS, t, LANES), F32)
    return _pc(
        body, name="attention_bwd", grid=(n_tiles,),
        out_shape=(jax.ShapeDtypeStruct((N_Q_HEADS, t, LANES), F32), kv_shape, kv_shape),
        in_specs=[_heads(N_Q_HEADS, LANES), _heads(N_Q_HEADS, LANES),
                  _full((N_KV_HEADS, t, LANES)), _full((N_KV_HEADS, t, LANES))],
        out_specs=(_heads(N_Q_HEADS, LANES), pl.BlockSpec(memory_space=pl.ANY), pl.BlockSpec(memory_space=pl.ANY)),
        scratch_shapes=[pltpu.VMEM((N_KV_HEADS, t, LANES), F32), pltpu.VMEM((N_KV_HEADS, t, LANES), F32)],
        compiler_params=_cparams(),
    )(q, doe, k, v)


def _qkv_bwd(dq, dk, dv, pq, qk_gain, cos_t, sin_t):
    t = pq.shape[0]
    n_tiles = t // TM

    def body(dq_ref, dk_ref, dv_ref, pq_ref, gain_ref, cos_ref, sin_ref, dpq_ref, dgain_ref):
        i = pl.program_id(0)

        @pl.when(i == 0)
        def _():
            dgain_ref[...] = jnp.zeros_like(dgain_ref)

        lane = _lane(TM)
        lo = lane < HEAD_DIM
        lo16 = (lane & 31) < 16
        cos = cos_ref[...]
        sin = sin_ref[...]
        pq_ = pq_ref[...]
        outs = []
        for b in range(3):
            src = dq_ref if b < 2 else dk_ref
            base = 2 * b if b < 2 else 0
            drot = src[base] + pltpu.roll(src[base + 1], HEAD_DIM, 1)
            if b < 2:
                drot = drot * ATTN_SCALE
            dxg = drot * cos + _swap16(drot * sin, lo16)
            xh, r = _head_norm(pq_[:, b * LANES:(b + 1) * LANES], lo)
            row = 0 if b < 2 else 1
            dgain_ref[row:row + 1, :] += _colsum(dxg * xh)
            dxh = dxg * gain_ref[row:row + 1, :]
            outs.append(r * (dxh - xh * (_pair_sums(dxh * xh, lo) * (1.0 / HEAD_DIM))))
        outs.append(dv_ref[0] + pltpu.roll(dv_ref[1], HEAD_DIM, 1))
        dpq_ref[...] = jnp.concatenate(outs, axis=1).astype(BF16)

    return _pc(
        body, name="qkv_bwd", grid=(n_tiles,),
        out_shape=(jax.ShapeDtypeStruct((t, W_Q), BF16), jax.ShapeDtypeStruct((8, LANES), F32)),
        in_specs=[_heads(N_Q_HEADS, LANES), _heads(N_KV_HEADS, LANES), _heads(N_KV_HEADS, LANES), _rows(W_Q),
                  _full((8, LANES)), _rows(LANES), _rows(LANES)],
        out_specs=(_rows(W_Q), _full((8, LANES))),
        compiler_params=_cparams(),
    )(dq, dk, dv, pq, qk_gain, cos_t, sin_t)


def _in_proj_bwd(dpc, dpr, dpq, w_c, w_r, w_q, xt, dxo, modv, g_pre):
    t = xt.shape[0]

    def body(dpc_ref, dpr_ref, dpq_ref, wc_ref, wr_ref, wq_ref, x_ref, dxo_ref, mod_ref, g_ref, dx_ref, acc_ref):
        i = pl.program_id(0)
        is_ctx = i < N_CTX_TILES

        @pl.when(i == 0)
        def _():
            acc_ref[...] = jnp.zeros_like(acc_ref)

        dh = _dot_nt(dpc_ref[...], wc_ref[...]) + _dot_nt(dpr_ref[...], wr_ref[...]) + _dot_nt(dpq_ref[...], wq_ref[...])
        x = x_ref[...]
        r = lax.rsqrt(jnp.mean(x * x, axis=1, keepdims=True) + RMS_EPS)
        xn = x * r
        g = g_ref[...]
        sc = jnp.where(is_ctx, mod_ref[1:2, :], mod_ref[4:5, :])
        dsh = _colsum(dh)
        dsc = _colsum(dh * (xn * g))
        acc_ref[0:1, :] += jnp.where(is_ctx, dsh, 0.0)
        acc_ref[1:2, :] += jnp.where(is_ctx, dsc, 0.0)
        acc_ref[2:3, :] += jnp.where(is_ctx, 0.0, dsh)
        acc_ref[3:4, :] += jnp.where(is_ctx, 0.0, dsc)
        dxg = dh * (1.0 + sc)
        acc_ref[4:5, :] += _colsum(dxg * xn)
        dxn = dxg * g
        dx_ref[...] = r * (dxn - xn * jnp.mean(dxn * xn, axis=1, keepdims=True)) + dxo_ref[...]

    return _pc(
        body, name="in_proj_bwd", grid=(t // TM,),
        out_shape=(jax.ShapeDtypeStruct((t, D_MODEL), F32), jax.ShapeDtypeStruct((8, D_MODEL), F32)),
        in_specs=[_rows(W_C), _rows(W_R), _rows(W_Q), _full((D_MODEL, W_C)), _full((D_MODEL, W_R)), _full((D_MODEL, W_Q)),
                  _rows(D_MODEL), _rows(D_MODEL), _full((8, D_MODEL)), _full((1, D_MODEL))],
        out_specs=(_rows(D_MODEL), _full((8, D_MODEL))),
        compiler_params=_cparams(),
    )(dpc, dpr, dpq, w_c, w_r, w_q, xt, dxo, modv, g_pre)


def _in_proj_wgrad(h, dpc, dpr, dpq):
    t = h.shape[0]

    def body(h_ref, dpc_ref, dpr_ref, dpq_ref, gc_ref, gr_ref, gq_ref):
        @pl.when(pl.program_id(0) == 0)
        def _():
            gc_ref[...] = jnp.zeros_like(gc_ref)
            gr_ref[...] = jnp.zeros_like(gr_ref)
            gq_ref[...] = jnp.zeros_like(gq_ref)

        hb = h_ref[...]
        gc_ref[...] += _dot_tn(hb, dpc_ref[...])
        gr_ref[...] += _dot_tn(hb, dpr_ref[...])
        gq_ref[...] += _dot_tn(hb, dpq_ref[...])

    return _pc(
        body, name="in_proj_wgrad", grid=(t // TM,),
        out_shape=(jax.ShapeDtypeStruct((D_MODEL, W_C), F32), jax.ShapeDtypeStruct((D_MODEL, W_R), F32),
                   jax.ShapeDtypeStruct((D_MODEL, W_Q), F32)),
        in_specs=[_rows(D_MODEL), _rows(W_C), _rows(W_R), _rows(W_Q)],
        out_specs=(_full((D_MODEL, W_C)), _full((D_MODEL, W_R)), _full((D_MODEL, W_Q))),
        compiler_params=_cparams(),
    )(h, dpc, dpr, dpq)


def _sum_slabs(slabs, tile_rows):
    n, r, c = slabs.shape

    def body(s_ref, o_ref):
        acc = s_ref[0].astype(F32)
        for k in range(1, n):
            acc = acc + s_ref[k].astype(F32)
        o_ref[...] = acc

    return _pc(
        body, name="sum_slabs", grid=(r // tile_rows,),
        out_shape=jax.ShapeDtypeStruct((r, c), F32),
        in_specs=[pl.BlockSpec((n, tile_rows, c), lambda i: (0, i, 0))],
        out_specs=pl.BlockSpec((tile_rows, c), lambda i: (i, 0)),
        compiler_params=_cparams(),
    )(slabs)


def _adamw(grads, w, m, v, tile_rows):
    r, c = w.shape
    n_g = len(grads)

    def body(*refs):
        g = refs[0][...]
        for k in range(1, n_g):
            g = g + refs[k][...]
        w_ref, m_ref, v_ref, g_out, d_out, m_out, v_out = refs[n_g:]
        m_new = ADAM_B1 * m_ref[...] + (1.0 - ADAM_B1) * g
        v_new = ADAM_B2 * v_ref[...] + (1.0 - ADAM_B2) * (g * g)
        m_hat = m_new / (1.0 - ADAM_B1 ** ADAM_STEP)
        v_hat = v_new / (1.0 - ADAM_B2 ** ADAM_STEP)
        g_out[...] = g
        d_out[...] = -ADAM_LR * (m_hat / (jnp.sqrt(v_hat) + ADAM_EPS) + ADAM_WD * w_ref[...])
        m_out[...] = m_new
        v_out[...] = v_new

    spec = pl.BlockSpec((tile_rows, c), lambda i: (i, 0))
    shape = jax.ShapeDtypeStruct((r, c), F32)
    return _pc(
        body, name="adamw", grid=(r // tile_rows,),
        out_shape=(shape,) * 4, in_specs=[spec] * (n_g + 3), out_specs=(spec,) * 4,
        compiler_params=_cparams(),
    )(*grads, w, m, v)


def _rope_tables(s_lat):
    pos = jnp.arange(s_lat)
    pos_row = (pos // GRID_W).astype(F32)
    pos_col = (pos % GRID_W).astype(F32)
    axis_dim = HEAD_DIM // 2
    inv_freq = 1.0 / (ROPE_THETA ** (jnp.arange(0, axis_dim, 2, dtype=F32) / axis_dim))
    d = np.arange(LANES) % HEAD_DIM
    on_rows = (d // axis_dim) == 0
    freq = d % (axis_dim // 2)
    sign = np.where((d % axis_dim) < axis_dim // 2, -1.0, 1.0).astype(np.float32)
    ang = jnp.where(on_rows[None, :], pos_row[:, None], pos_col[:, None]) * inv_freq[freq][None, :]
    cos = jnp.concatenate([jnp.ones((CTX_LEN, LANES), F32), jnp.cos(ang)], axis=0)
    sin = jnp.concatenate([jnp.zeros((CTX_LEN, LANES), F32), jnp.sin(ang) * sign[None, :]], axis=0)
    return cos, sin


def _pad_rows(a, rows):
    return jnp.concatenate([a, jnp.zeros((rows - a.shape[0],) + a.shape[1:], a.dtype)], axis=0)


_SMALL = ("c_ctx", "b_mod", "g_pre", "g_post", "conv_a", "conv_b", "conv_b_bias", "conf_ln_g", "conf_ln_b",
          "sgu_ln_g", "sgu_ln_b", "w_s", "b_s", "q_gain", "k_gain")


def _pack(arrays):
    flat = jnp.concatenate([a.reshape(-1) for a in arrays])
    rows = -(-flat.shape[0] // (8 * LANES)) * 8
    return _pad_rows(flat.reshape(-1, 1), rows * LANES).reshape(rows, LANES)


def _unpack(packed, shapes):
    flat = packed.reshape(-1)
    out, off = [], 0
    for s in shapes:
        n = int(np.prod(s))
        out.append(flat[off:off + n].reshape(s))
        off += n
    return out


def kernel(x, c, ctx, c_ctx, w_mod, b_mod, g_pre, g_post, w_in, w_out, conv_a, conv_b, conv_b_bias, conf_ln_g, conf_ln_b, sgu_ln_g, sgu_ln_b, w_s, b_s, q_gain, k_gain, loss_target, m_c_ctx, m_w_mod, m_b_mod, m_g_pre, m_g_post, m_w_in, m_w_out, m_conv_a, m_conv_b, m_conv_b_bias, m_conf_ln_g, m_conf_ln_b, m_sgu_ln_g, m_sgu_ln_b, m_w_s, m_b_s, m_q_gain, m_k_gain, v_c_ctx, v_w_mod, v_b_mod, v_g_pre, v_g_post, v_w_in, v_w_out, v_conv_a, v_conv_b, v_conv_b_bias, v_conf_ln_g, v_conf_ln_b, v_sgu_ln_g, v_sgu_ln_b, v_w_s, v_b_s, v_q_gain, v_k_gain):
    weights = dict(c_ctx=c_ctx, w_mod=w_mod, b_mod=b_mod, g_pre=g_pre, g_post=g_post, w_in=w_in, w_out=w_out, conv_a=conv_a,
                   conv_b=conv_b, conv_b_bias=conv_b_bias, conf_ln_g=conf_ln_g, conf_ln_b=conf_ln_b, sgu_ln_g=sgu_ln_g,
                   sgu_ln_b=sgu_ln_b, w_s=w_s, b_s=b_s, q_gain=q_gain, k_gain=k_gain)
    m_in = dict(c_ctx=m_c_ctx, w_mod=m_w_mod, b_mod=m_b_mod, g_pre=m_g_pre, g_post=m_g_post, w_in=m_w_in, w_out=m_w_out,
                conv_a=m_conv_a, conv_b=m_conv_b, conv_b_bias=m_conv_b_bias, conf_ln_g=m_conf_ln_g, conf_ln_b=m_conf_ln_b,
                sgu_ln_g=m_sgu_ln_g, sgu_ln_b=m_sgu_ln_b, w_s=m_w_s, b_s=m_b_s, q_gain=m_q_gain, k_gain=m_k_gain)
    v_in = dict(c_ctx=v_c_ctx, w_mod=v_w_mod, b_mod=v_b_mod, g_pre=v_g_pre, g_post=v_g_post, w_in=v_w_in, w_out=v_w_out,
                conv_a=v_conv_a, conv_b=v_conv_b, conv_b_bias=v_conv_b_bias, conf_ln_g=v_conf_ln_g, conf_ln_b=v_conf_ln_b,
                sgu_ln_g=v_sgu_ln_g, sgu_ln_b=v_sgu_ln_b, w_s=v_w_s, b_s=v_b_s, q_gain=v_q_gain, k_gain=v_k_gain)
    order = ("c_ctx", "w_mod", "b_mod", "g_pre", "g_post", "w_in", "w_out", "conv_a", "conv_b", "conv_b_bias", "conf_ln_g",
             "conf_ln_b", "sgu_ln_g", "sgu_ln_b", "w_s", "b_s", "q_gain", "k_gain")

    s_lat = x.shape[1]
    ax, ay, ac = lax.axis_index("x"), lax.axis_index("y"), lax.axis_index("c")
    chip = 2 * ax + ay
    example = 4 * ax + 2 * ay + ac

    c_rows = _all_gather_rows(_pad_rows(c, 8))[::8]
    c16 = _pad_rows(jnp.concatenate([c_rows, c_ctx[None, :]], axis=0), 16)
    b_mod_shard = lax.dynamic_slice_in_dim(b_mod, chip * SHARD_MOD, SHARD_MOD, axis=1)[:, None, :]
    silu_c, mod_shard = _mod_forward(c16, w_mod, b_mod_shard)
    mod_all = _all_gather_rows(mod_shard.reshape(DEPTH * 16, SHARD_MOD)).reshape(8, DEPTH, 16, SHARD_MOD)
    mod_full = jnp.transpose(mod_all[::2], (1, 2, 0, 3)).reshape(DEPTH, 16, 3 * D_MODEL)
    mod_lat = lax.dynamic_index_in_dim(mod_full, example, axis=1, keepdims=False).reshape(DEPTH, 3, D_MODEL)
    mod_ctx = mod_full[:, 8].reshape(DEPTH, 3, D_MODEL)
    modv = jnp.concatenate([mod_ctx, mod_lat, jnp.zeros((DEPTH, 2, D_MODEL), F32)], axis=1)

    wi_all, wo_all = _gather_weights(w_in.astype(BF16), w_out.astype(BF16))
    wi_full = jnp.concatenate([wi_all[k] for k in range(N_CHIPS)], axis=-1)
    wo_full = jnp.concatenate([wo_all[k] for k in range(N_CHIPS)], axis=1)
    w_c = jnp.concatenate([wi_full[..., 256:768], wi_full[..., 1024:1536]], axis=-1)
    w_r = jnp.concatenate([wi_full[..., 0:256], wi_full[..., 768:1024], wi_full[..., 1536:2560], wi_full[..., 3072:3328]], axis=-1)
    w_q = wi_full[..., 2560:3072]

    cos_t, sin_t = _rope_tables(s_lat)
    conv_a_full = jnp.zeros((DEPTH, 8, GROUP_W), F32)
    conv_b_full = jnp.zeros((DEPTH, 32, GROUP_W), F32)
    conv_small = jnp.concatenate([conv_a.reshape(DEPTH * SHORT_CONV_K, -1), conv_b.reshape(DEPTH * CONFORMER_K, -1)], axis=0)
    n_cs = conv_small.shape[0]
    conv_rows = -(-n_cs // 8) * 8
    conv_all = _all_gather_rows(_pad_rows(conv_small, conv_rows)).reshape(8, conv_rows, -1)[::2]
    conv_all = jnp.transpose(conv_all, (1, 0, 2)).reshape(conv_rows, GROUP_W)
    conv_a_full = conv_a_full.at[:, :SHORT_CONV_K].set(conv_all[:DEPTH * SHORT_CONV_K].reshape(DEPTH, SHORT_CONV_K, GROUP_W))
    conv_b_full = conv_b_full.at[:, :CONFORMER_K].set(
        conv_all[DEPTH * SHORT_CONV_K:n_cs].reshape(DEPTH, CONFORMER_K, GROUP_W))

    vecs = jnp.stack([conv_b_bias, conf_ln_g, conf_ln_b, sgu_ln_g, sgu_ln_b] + [jnp.zeros_like(conv_b_bias)] * 3, axis=1)
    wss = w_s.reshape(DEPTH, N_SPATIAL_GROUPS * CHUNK, CHUNK).astype(BF16)
    wsts = jnp.swapaxes(w_s, 2, 3).reshape(DEPTH, N_SPATIAL_GROUPS * CHUNK, CHUNK).astype(BF16)
    bsm = jnp.repeat(jnp.swapaxes(b_s, 1, 2), HEAD_DIM, axis=2)
    qk_gain = jnp.concatenate([jnp.tile(q_gain, (1, 2))[:, None, :], jnp.tile(k_gain, (1, 2))[:, None, :],
                               jnp.zeros((DEPTH, 6, LANES), F32)], axis=1)

    xt = jnp.concatenate([ctx[0], x[0]], axis=0)
    saved = []
    for l in range(DEPTH):
        h, pc, pr, pq, q, k, v = _in_proj(xt, modv[l], g_pre[l][None, :], w_c[l], w_r[l], w_q[l], qk_gain[l], cos_t, sin_t)
        oe = _attention_fwd(q, k, v)
        x_new, y, ca, z2 = _mix_out(pc, pr, oe, xt, modv[l], g_post[l][None, :], wo_full[l], conv_a_full[l], conv_b_full[l],
                                    vecs[l], wss[l], bsm[l])
        saved.append(dict(x=xt, h=h, pc=pc, pr=pr, pq=pq, q=q, k=k, v=v, oe=oe, y=y, ca=ca, z2=z2))
        xt = x_new
    dxo, loss_acc = _loss_head(xt, loss_target[0])
    loss = lax.psum(loss_acc[0, 0], ("x", "y", "c"))

    g_small = {n: [None] * DEPTH for n in _SMALL}
    gw_c, gw_r, gw_q, gw_o, d_mod = [None] * DEPTH, [None] * DEPTH, [None] * DEPTH, [None] * DEPTH, [None] * DEPTH
    for l in reversed(range(DEPTH)):
        s = saved[l]
        dpr, g_a, g_b, doe, gw_o[l], pvec, s256, dws, dbs = _mix_out_bwd(
            dxo, s["y"], s["pr"], s["ca"], s["z2"], s["oe"], modv[l], g_post[l][None, :], wo_full[l], vecs[l], wss[l], wsts[l], bsm[l])
        dpc, dca, dcb = _conv_bwd(s["pc"], g_a, g_b, conv_a_full[l], conv_b_full[l])
        dq, dk, dv = _attention_bwd(s["q"], s["k"], s["v"], doe)
        dpq, dgain = _qkv_bwd(dq, dk, dv, s["pq"], qk_gain[l], cos_t, sin_t)
        dxo, acc = _in_proj_bwd(dpc, dpr, dpq, w_c[l], w_r[l], w_q[l], s["x"], dxo, modv[l], g_pre[l][None, :])
        gw_c[l], gw_r[l], gw_q[l] = _in_proj_wgrad(s["h"], dpc, dpr, dpq)
        d_mod[l] = jnp.stack([jnp.concatenate([acc[2], acc[3], pvec[1]]), jnp.concatenate([acc[0], acc[1], pvec[0]])])
        g_small["g_pre"][l] = acc[4]
        g_small["g_post"][l] = pvec[2]
        g_small["conv_a"][l] = dca[:SHORT_CONV_K]
        g_small["conv_b"][l] = dcb[:CONFORMER_K]
        g_small["conv_b_bias"][l] = s256[0]
        g_small["conf_ln_g"][l] = s256[1]
        g_small["conf_ln_b"][l] = s256[2]
        g_small["sgu_ln_g"][l] = s256[3]
        g_small["sgu_ln_b"][l] = s256[4]
        g_small["w_s"][l] = dws.reshape(N_SPATIAL_GROUPS, CHUNK, CHUNK)
        g_small["b_s"][l] = jnp.transpose(dbs[:, :N_SPATIAL_GROUPS])
        g_small["q_gain"][l] = dgain[0, :HEAD_DIM] + dgain[0, HEAD_DIM:]
        g_small["k_gain"][l] = dgain[1, :HEAD_DIM] + dgain[1, HEAD_DIM:]
    grad_x = dxo[CTX_LEN:][None]

    d_mod_all = _all_gather_rows(jnp.stack(d_mod).reshape(DEPTH * 2, 3 * D_MODEL)).reshape(8, DEPTH, 2, 3 * D_MODEL)
    d_lat = jnp.transpose(d_mod_all[:, :, 0], (1, 0, 2))
    d_ctx = jnp.transpose(d_mod_all[:, :, 1], (1, 0, 2))
    cols = lambda a: lax.dynamic_slice_in_dim(a.reshape(DEPTH, 8, N_CHIPS, SHARD_MOD), chip, 1, axis=2)[:, :, 0]
    silu_t = jnp.transpose(silu_c)
    s_t = jnp.concatenate([silu_t[:, 0:8], jnp.tile(silu_t[:, 8:9], (1, 8)), jnp.zeros((D_MODEL, LANES - 16), F32)], axis=1)
    g_rows = jnp.concatenate([cols(d_lat), cols(d_ctx), jnp.zeros((DEPTH, LANES - 16, SHARD_MOD), F32)], axis=1)
    g_w_mod, g_b_mod, c_ctx_part = _mod_backward(s_t, g_rows, cols(d_ctx), jnp.concatenate([d_lat, d_ctx], axis=1),
                                                 w_mod, c_ctx[:, None])

    for n in _SMALL:
        if n not in ("c_ctx", "b_mod"):
            g_small[n] = jnp.stack(g_small[n])
    small_parts = [0.5 * c_ctx_part[:, 0]] + [g_small[n] for n in _SMALL[2:]]
    packed = _pack(small_parts)
    gathered = _all_gather_rows(packed).reshape(8, packed.shape[0], LANES)
    small_sum = _sum_slabs(gathered, packed.shape[0])
    small_g = dict(zip(("c_ctx",) + _SMALL[2:], _unpack(small_sum, [p.shape for p in small_parts])))
    small_g["b_mod"] = g_b_mod[:, 0]
    ch64 = GROUP_W // N_CHIPS
    for n in ("conv_a", "conv_b"):
        small_g[n] = lax.dynamic_slice_in_dim(small_g[n], chip * ch64, ch64, axis=2)
    sw = _pack([weights[n] for n in _SMALL])
    sm = _pack([m_in[n] for n in _SMALL])
    sv = _pack([v_in[n] for n in _SMALL])
    sg = _pack([small_g[n] for n in _SMALL])
    shapes = [weights[n].shape for n in _SMALL]
    small_out = [dict(zip(_SMALL, _unpack(o, shapes))) for o in _adamw([sg], sw, sm, sv, sg.shape[0])]

    gw_c, gw_r, gw_q, gw_o = jnp.stack(gw_c), jnp.stack(gw_r), jnp.stack(gw_q), jnp.stack(gw_o)
    gw_in = jnp.concatenate([gw_r[..., 0:256], gw_c[..., 0:512], gw_r[..., 256:512], gw_c[..., 512:1024],
                             gw_r[..., 512:1536], gw_q, gw_r[..., 1536:1792]], axis=-1)
    slabs_in = jnp.transpose(gw_in.reshape(DEPTH, D_MODEL, N_CHIPS, SHARD_IN), (2, 0, 1, 3)).astype(BF16)
    slabs_out = jnp.transpose(gw_o.reshape(DEPTH, N_CHIPS, SHARD_OUT, D_MODEL), (1, 0, 2, 3)).astype(BF16)
    recv_in, recv_out = _scatter_slabs(slabs_in, slabs_out)
    sum_in = _sum_slabs(recv_in.reshape(N_CHIPS, DEPTH * D_MODEL, SHARD_IN), 512)
    sum_out = _sum_slabs(recv_out.reshape(N_CHIPS, DEPTH * SHARD_OUT, D_MODEL), 256)
    sib_in, sib_out = _swap_with_sibling(sum_in, sum_out)

    big = {}
    flat = lambda a: a.reshape(-1, a.shape[-1])
    for n, grads, rows in (("w_in", [sum_in, sib_in], 512), ("w_out", [sum_out, sib_out], 256), ("w_mod", [flat(g_w_mod)], 512)):
        outs = _adamw(grads, flat(weights[n]), flat(m_in[n]), flat(v_in[n]), rows)
        big[n] = [o.reshape(weights[n].shape) for o in outs]

    def leaf(n, j):
        return big[n][j] if n in big else small_out[j][n]

    return (loss, grad_x, *[leaf(n, 0) for n in order], *[leaf(n, 1) for n in order],
            *[leaf(n, 2) for n in order], *[leaf(n, 3) for n in order])
```

```python
import functools

import numpy as np
import jax
import jax.numpy as jnp
from jax import lax
from jax.experimental import pallas as pl
from jax.experimental.pallas import tpu as pltpu

F32 = jnp.float32
BF16 = jnp.bfloat16
MESH = pl.DeviceIdType.MESH

D_MODEL = 1024
DEPTH = 4
GRID_W = 64
CTX_LEN = 256
GROUP_W = 256
HEAD_DIM = 64
N_Q_HEADS = 4
N_KV_HEADS = 2
GQA = N_Q_HEADS // N_KV_HEADS
ROPE_THETA = 10000.0
ATTN_SCALE = HEAD_DIM ** -0.5
SHORT_CONV_K = 3
CONFORMER_K = 31
CHUNK = 128
N_SPATIAL_GROUPS = 4
RMS_EPS = 1e-6
LN_EPS = 1e-5
ADAM_LR = 0.001
ADAM_B1 = 0.9
ADAM_B2 = 0.999
ADAM_EPS = 1e-08
ADAM_WD = 0.01
ADAM_STEP = 10

LANES = 128
HALO = 16
TM = 256
N_CTX_TILES = CTX_LEN // TM
W_C = 1024
W_R = 1792
W_Q = 512
PROJ_W = W_C + W_R + W_Q
N_CHIPS = 4
SHARD_IN = PROJ_W // N_CHIPS
SHARD_OUT = D_MODEL // N_CHIPS
SHARD_MOD = 3 * D_MODEL // N_CHIPS
VMEM_LIMIT = 56 * 1024 * 1024


def _pc(body, **kw):
    return pl.pallas_call(body, **kw)


def _cparams(**kw):
    return pltpu.CompilerParams(dimension_semantics=("arbitrary",), vmem_limit_bytes=VMEM_LIMIT, **kw)


def _full(shape):
    n = len(shape)
    return pl.BlockSpec(shape, lambda i: (0,) * n)


def _rows(width, tm=TM):
    return pl.BlockSpec((tm, width), lambda i: (i, 0))


def _heads(nh, width, tm=TM):
    return pl.BlockSpec((nh, tm, width), lambda i: (0, i, 0))


def _sigmoid(x):
    return jax.nn.sigmoid(x)


def _dot(a, b):
    return jnp.dot(a, b, preferred_element_type=F32)


def _dot_nt(a, b):
    return lax.dot_general(a, b, (((1,), (1,)), ((), ())), preferred_element_type=F32)


def _dot_tn(a, b):
    return lax.dot_general(a, b, (((0,), (0,)), ((), ())), preferred_element_type=F32)


def _lane(rows):
    return lax.broadcasted_iota(jnp.int32, (rows, LANES), 1)


def _rowsum(x):
    return jnp.sum(x, axis=1, keepdims=True)


def _colsum(x):
    return jnp.sum(x, axis=0, keepdims=True)


def _pair_sums(x, lo):
    s0 = _rowsum(jnp.where(lo, x, 0.0))
    s1 = _rowsum(jnp.where(lo, 0.0, x))
    return jnp.where(lo, s0, s1)


def _swap16(x, lo16):
    return jnp.where(lo16, pltpu.roll(x, LANES - 16, 1), pltpu.roll(x, 16, 1))


def _layer_norm_stats(x):
    mu = jnp.mean(x, axis=1, keepdims=True)
    xc = x - mu
    rs = lax.rsqrt(jnp.mean(xc * xc, axis=1, keepdims=True) + LN_EPS)
    return xc * rs, rs


def _layer_norm_bwd(dxn, xn, rs):
    return rs * (dxn - jnp.mean(dxn, axis=1, keepdims=True) - xn * jnp.mean(dxn * xn, axis=1, keepdims=True))


def _group_select(r, grp):
    out = jnp.where(grp == 0, r[0:CHUNK], 0.0)
    for g in range(1, N_SPATIAL_GROUPS):
        out = out + jnp.where(grp == g, r[g * CHUNK:(g + 1) * CHUNK], 0.0)
    return out


def _kv_chunk(s_lat):
    return 1024 if s_lat % 1024 == 0 else 256


def _all_gather_rows(x_shard):
    m_per, n = x_shard.shape

    def body(x_ref, out_ref, send_sems, recv_sems, local_sem):
        x, y, c = lax.axis_index("x"), lax.axis_index("y"), lax.axis_index("c")
        me, sibling = (x, y, c), (x, y, 1 - c)
        chips = [(1 - x, y), (x, 1 - y), (1 - x, 1 - y)]

        def rows(px, py, pc):
            return out_ref.at[pl.ds((4 * px + 2 * py + pc) * m_per, m_per), :]

        def copy(k, block, to, src=None):
            return pltpu.make_async_remote_copy(
                src_ref=rows(*block) if src is None else src, dst_ref=rows(*block),
                send_sem=send_sems.at[k], recv_sem=recv_sems.at[k], device_id=to, device_id_type=MESH)

        mine = pltpu.make_async_copy(x_ref, rows(*me), local_sem)
        mine.start()
        first = [copy(0, me, sibling, src=x_ref)]
        first += [copy(1 + j, me, (*chip, c), src=x_ref) for j, chip in enumerate(chips)]
        for cp in first:
            cp.start()
        passed = [copy(4 + j, (*chip, c), sibling) for j, chip in enumerate(chips)]
        for j, chip in enumerate(chips):
            copy(1 + j, (*chip, c), me).wait_recv()
            passed[j].start()
        copy(0, sibling, me).wait_recv()
        for j, chip in enumerate(chips):
            copy(4 + j, (*chip, 1 - c), me).wait_recv()
        for cp in first + passed:
            cp.wait_send()
        mine.wait()

    return _pc(
        body, name="all_gather_rows",
        out_shape=jax.ShapeDtypeStruct((8 * m_per, n), x_shard.dtype),
        in_specs=[pl.BlockSpec(memory_space=pltpu.VMEM)],
        out_specs=pl.BlockSpec(memory_space=pltpu.VMEM),
        scratch_shapes=[pltpu.SemaphoreType.DMA((7,)), pltpu.SemaphoreType.DMA((7,)), pltpu.SemaphoreType.DMA],
        compiler_params=pltpu.CompilerParams(vmem_limit_bytes=VMEM_LIMIT),
    )(x_shard)


def _place():
    x, y, c = lax.axis_index("x"), lax.axis_index("y"), lax.axis_index("c")
    return x, y, c, [(1 - x, y), (x, 1 - y), (1 - x, 1 - y)]


def _remote(src, dst, send_sems, recv_sems, k, to):
    return pltpu.make_async_remote_copy(src_ref=src, dst_ref=dst, send_sem=send_sems.at[k], recv_sem=recv_sems.at[k],
                                        device_id=to, device_id_type=MESH)


GATHER_SEMS = 12
SCATTER_SEMS = 6


def _gather_phase(phase, pairs, send_sems, recv_sems, local_sems):
    x, y, c, chips = _place()
    kme = 2 * x + y
    sibling = (x, y, 1 - c)
    for a, (src, dst) in enumerate(pairs):
        half = src.shape[0] // 2
        mine = pl.ds(c * half, half)
        theirs = pl.ds((1 - c) * half, half)
        loc = pltpu.make_async_copy(src, dst.at[kme], local_sems.at[a])
        if phase == 0:
            loc.start()
        if phase == 2:
            loc.wait()
        for j, (px, py) in enumerate(chips):
            kk = 2 * px + py
            out = _remote(src.at[mine], dst.at[kme, mine], send_sems, recv_sems, 6 * a + j, (px, py, c))
            landed = dst.at[kk, mine]
            hand = _remote(landed, landed, send_sems, recv_sems, 6 * a + 3 + j, sibling)
            if phase == 0:
                out.start()
            if phase == 1:
                _remote(landed, landed, send_sems, recv_sems, 6 * a + j, (px, py, c)).wait_recv()
                hand.start()
            if phase == 2:
                other = dst.at[kk, theirs]
                _remote(other, other, send_sems, recv_sems, 6 * a + 3 + j, sibling).wait_recv()
                out.wait_send()
                hand.wait_send()


def _scatter_phase(phase, pairs, send_sems, recv_sems, local_sems):
    x, y, c, chips = _place()
    kme = 2 * x + y
    for a, (src, dst) in enumerate(pairs):
        loc = pltpu.make_async_copy(src.at[kme], dst.at[kme], local_sems.at[a])
        if phase == 0:
            loc.start()
        else:
            loc.wait()
        for j, (px, py) in enumerate(chips):
            kk = 2 * px + py
            out = _remote(src.at[kk], dst.at[kme], send_sems, recv_sems, 3 * a + j, (px, py, c))
            if phase == 0:
                out.start()
            else:
                landed = dst.at[kk]
                _remote(landed, landed, send_sems, recv_sems, 3 * a + j, (px, py, c)).wait_recv()
                out.wait_send()


def _comm_scratch(n):
    return [pltpu.SemaphoreType.DMA((n,)), pltpu.SemaphoreType.DMA((n,)), pltpu.SemaphoreType.DMA((2,))]


def _slots(a):
    return jax.ShapeDtypeStruct((N_CHIPS,) + a.shape, a.dtype)


def _gather_weights(wi, wo):
    def body(wi_ref, wo_ref, gi_ref, go_ref, send_sems, recv_sems, local_sems):
        for phase in range(3):
            _gather_phase(phase, ((wi_ref, gi_ref), (wo_ref, go_ref)), send_sems, recv_sems, local_sems)

    hbm = pl.BlockSpec(memory_space=pl.ANY)
    return _pc(
        body, name="gather_weights", out_shape=(_slots(wi), _slots(wo)),
        in_specs=[hbm, hbm], out_specs=(hbm, hbm), scratch_shapes=_comm_scratch(GATHER_SEMS),
    )(wi, wo)


def _scatter_slabs(gi, go):
    def body(gi_ref, go_ref, ri_ref, ro_ref, send_sems, recv_sems, local_sems):
        for phase in range(2):
            _scatter_phase(phase, ((gi_ref, ri_ref), (go_ref, ro_ref)), send_sems, recv_sems, local_sems)

    hbm = pl.BlockSpec(memory_space=pl.ANY)
    return _pc(
        body, name="scatter_slabs",
        out_shape=(jax.ShapeDtypeStruct(gi.shape, gi.dtype), jax.ShapeDtypeStruct(go.shape, go.dtype)),
        in_specs=[hbm, hbm], out_specs=(hbm, hbm), scratch_shapes=_comm_scratch(SCATTER_SEMS),
    )(gi, go)


def _swap_with_sibling(a, b):
    def body(a_ref, b_ref, ra_ref, rb_ref, send_sems, recv_sems):
        x, y, c = lax.axis_index("x"), lax.axis_index("y"), lax.axis_index("c")
        copies = []
        for k, (src, dst) in enumerate(((a_ref, ra_ref), (b_ref, rb_ref))):
            cp = pltpu.make_async_remote_copy(
                src_ref=src, dst_ref=dst, send_sem=send_sems.at[k], recv_sem=recv_sems.at[k],
                device_id=(x, y, 1 - c), device_id_type=MESH)
            cp.start()
            copies.append(cp)
        for cp in copies:
            cp.wait()

    hbm = pl.BlockSpec(memory_space=pl.ANY)
    return _pc(
        body, name="swap_with_sibling",
        out_shape=(jax.ShapeDtypeStruct(a.shape, a.dtype), jax.ShapeDtypeStruct(b.shape, b.dtype)),
        in_specs=[hbm, hbm], out_specs=(hbm, hbm),
        scratch_shapes=[pltpu.SemaphoreType.DMA((2,)), pltpu.SemaphoreType.DMA((2,))],
    )(a, b)


def _mod_forward(c16, w_mod, b_mod_shard):
    def body(c_ref, w_ref, b_ref, s_ref, o_ref):
        cc = c_ref[...]
        s = cc * _sigmoid(cc)
        s_ref[...] = s
        o_ref[0] = jnp.dot(s, w_ref[0], preferred_element_type=F32, precision=lax.Precision.HIGHEST) + b_ref[0]

    return _pc(
        body, name="mod_forward", grid=(DEPTH,),
        out_shape=(jax.ShapeDtypeStruct((16, D_MODEL), F32), jax.ShapeDtypeStruct((DEPTH, 16, SHARD_MOD), F32)),
        in_specs=[_full((16, D_MODEL)),
                  pl.BlockSpec((1, D_MODEL, SHARD_MOD), lambda l: (l, 0, 0)),
                  pl.BlockSpec((1, 1, SHARD_MOD), lambda l: (l, 0, 0))],
        out_specs=(_full((16, D_MODEL)), pl.BlockSpec((1, 16, SHARD_MOD), lambda l: (l, 0, 0))),
        compiler_params=_cparams(),
    )(c16, w_mod, b_mod_shard)


def _mod_backward(s_t, g_rows, g_ctx, d_all, w_mod, c_ctx_col):
    def body(st_ref, g_ref, gc_ref, d_ref, w_ref, cc_ref, gw_ref, gb_ref, pc_ref):
        l = pl.program_id(0)
        gw_ref[0] = jnp.dot(st_ref[...], g_ref[0], preferred_element_type=F32, precision=lax.Precision.HIGHEST)
        gb_ref[0] = _colsum(d_ref[0])
        part = _rowsum(w_ref[0] * _colsum(gc_ref[0]))

        @pl.when(l == 0)
        def _():
            pc_ref[...] = jnp.zeros_like(pc_ref)

        pc_ref[...] += part

        @pl.when(l == DEPTH - 1)
        def _():
            cc = cc_ref[...]
            sg = _sigmoid(cc)
            pc_ref[...] = pc_ref[...] * (sg * (1.0 + cc * (1.0 - sg)))

    return _pc(
        body, name="mod_backward", grid=(DEPTH,),
        out_shape=(jax.ShapeDtypeStruct((DEPTH, D_MODEL, SHARD_MOD), F32),
                   jax.ShapeDtypeStruct((DEPTH, 1, 3 * D_MODEL), F32),
                   jax.ShapeDtypeStruct((D_MODEL, 1), F32)),
        in_specs=[_full((D_MODEL, LANES)),
                  pl.BlockSpec((1, LANES, SHARD_MOD), lambda l: (l, 0, 0)),
                  pl.BlockSpec((1, 8, SHARD_MOD), lambda l: (l, 0, 0)),
                  pl.BlockSpec((1, 16, 3 * D_MODEL), lambda l: (l, 0, 0)),
                  pl.BlockSpec((1, D_MODEL, SHARD_MOD), lambda l: (l, 0, 0)),
                  _full((D_MODEL, 1))],
        out_specs=(pl.BlockSpec((1, D_MODEL, SHARD_MOD), lambda l: (l, 0, 0)),
                   pl.BlockSpec((1, 1, 3 * D_MODEL), lambda l: (l, 0, 0)),
                   _full((D_MODEL, 1))),
        compiler_params=_cparams(),
    )(s_t, g_rows, g_ctx, d_all, w_mod, c_ctx_col)


def _head_norm(xb, lo):
    r = lax.rsqrt(_pair_sums(xb * xb, lo) * (1.0 / HEAD_DIM) + RMS_EPS)
    return xb * r, r


def _in_proj(xt, modv, g_pre, w_c, w_r, w_q, qk_gain, cos_t, sin_t):
    t = xt.shape[0]

    def body(x_ref, mod_ref, g_ref, wc_ref, wr_ref, wq_ref, gain_ref, cos_ref, sin_ref,
             h_ref, pc_ref, pr_ref, pq_ref, q_ref, k_ref, v_ref):
        is_ctx = pl.program_id(0) < N_CTX_TILES
        x = x_ref[...]
        r = lax.rsqrt(jnp.mean(x * x, axis=1, keepdims=True) + RMS_EPS)
        sh = jnp.where(is_ctx, mod_ref[0:1, :], mod_ref[3:4, :])
        sc = jnp.where(is_ctx, mod_ref[1:2, :], mod_ref[4:5, :])
        h = (x * r * g_ref[...]) * (1.0 + sc) + sh
        hb = h.astype(BF16)
        h_ref[...] = hb
        pc_ref[...] = _dot(hb, wc_ref[...])
        pr_ref[...] = _dot(hb, wr_ref[...])
        pq = _dot(hb, wq_ref[...])
        pq_ref[...] = pq
        lane = _lane(TM)
        lo = lane < HEAD_DIM
        lo16 = (lane & 31) < 16
        cos = cos_ref[...]
        sin = sin_ref[...]
        for b in range(3):
            xh, _ = _head_norm(pq[:, b * LANES:(b + 1) * LANES], lo)
            xg = xh * (gain_ref[0:1, :] if b < 2 else gain_ref[1:2, :])
            rot = xg * cos + _swap16(xg, lo16) * sin
            if b < 2:
                rot = rot * ATTN_SCALE
            dst = q_ref if b < 2 else k_ref
            base = 2 * b if b < 2 else 0
            dst[base] = jnp.where(lo, rot, 0.0).astype(BF16)
            dst[base + 1] = jnp.where(lo, pltpu.roll(rot, HEAD_DIM, 1), 0.0).astype(BF16)
        vb = pq[:, 3 * LANES:4 * LANES]
        one = jnp.where(lane == HEAD_DIM, 1.0, 0.0)
        v_ref[0] = jnp.where(lo, vb, one).astype(BF16)
        v_ref[1] = jnp.where(lo, pltpu.roll(vb, HEAD_DIM, 1), one).astype(BF16)

    return _pc(
        body, name="in_proj", grid=(t // TM,),
        out_shape=(jax.ShapeDtypeStruct((t, D_MODEL), BF16),
                   jax.ShapeDtypeStruct((t, W_C), F32), jax.ShapeDtypeStruct((t, W_R), F32), jax.ShapeDtypeStruct((t, W_Q), F32),
                   jax.ShapeDtypeStruct((N_Q_HEADS, t, LANES), BF16),
                   jax.ShapeDtypeStruct((N_KV_HEADS, t, LANES), BF16),
                   jax.ShapeDtypeStruct((N_KV_HEADS, t, LANES), BF16)),
        in_specs=[_rows(D_MODEL), _full((8, D_MODEL)), _full((1, D_MODEL)),
                  _full((D_MODEL, W_C)), _full((D_MODEL, W_R)), _full((D_MODEL, W_Q)),
                  _full((8, LANES)), _rows(LANES), _rows(LANES)],
        out_specs=(_rows(D_MODEL), _rows(W_C), _rows(W_R), _rows(W_Q),
                   _heads(N_Q_HEADS, LANES), _heads(N_KV_HEADS, LANES), _heads(N_KV_HEADS, LANES)),
        compiler_params=_cparams(),
    )(xt, modv, g_pre, w_c, w_r, w_q, qk_gain, cos_t, sin_t)


def _attention_fwd(q, k, v, shards=None):
    t = q.shape[1]
    tkl = _kv_chunk(t - CTX_LEN)
    n_lat = (t - CTX_LEN) // tkl
    n_tiles = t // TM
    n_sh = 0 if shards is None else len(shards)

    def body(q_ref, k_ref, v_ref, *rest):
        i = pl.program_id(0)
        o_ref = rest[n_sh]
        if shards is not None:
            pairs = tuple(zip(rest[:n_sh], rest[n_sh + 1:2 * n_sh + 1]))
            for phase, at in enumerate((0, n_tiles // 2, n_tiles - 1)):
                @pl.when(i == at)
                def _(phase=phase):
                    _gather_phase(phase, pairs, *rest[2 * n_sh + 1:])
        steps = jnp.where(i < N_CTX_TILES, 0, n_lat)
        lane = _lane(GQA * TM)
        qs = [jnp.concatenate([q_ref[GQA * g + hh] for hh in range(GQA)], axis=0) for g in range(N_KV_HEADS)]

        def step(st, size, carry):
            out = []
            for g in range(N_KV_HEADS):
                m, acc = carry[g]
                s = _dot_nt(qs[g], k_ref[g, pl.ds(st, size), :])
                m_new = jnp.maximum(m, jnp.max(s, axis=1, keepdims=True))
                p = jnp.exp(s - m_new)
                out.append((m_new, acc * jnp.exp(m - m_new) + _dot(p.astype(BF16), v_ref[g, pl.ds(st, size), :])))
            return tuple(out)

        init = tuple((jnp.full((GQA * TM, 1), -jnp.inf, F32), jnp.zeros((GQA * TM, LANES), F32)) for _ in range(N_KV_HEADS))
        carry = step(0, CTX_LEN, init)
        carry = lax.fori_loop(0, steps, lambda j, cr: step(pl.multiple_of(CTX_LEN + j * tkl, 256), tkl, cr), carry)
        for g in range(N_KV_HEADS):
            m, acc = carry[g]
            den = _rowsum(jnp.where(lane == HEAD_DIM, acc, 0.0))
            out = jnp.where(lane < HEAD_DIM, acc * (1.0 / den), jnp.where(lane == HEAD_DIM, m + jnp.log(den), 0.0))
            for hh in range(GQA):
                o_ref[GQA * g + hh] = out[hh * TM:(hh + 1) * TM]

    hbm = pl.BlockSpec(memory_space=pl.ANY)
    extra = () if shards is None else tuple(shards)
    outs = _pc(
        body, name="attention_fwd" if shards is None else "attention_fwd_gather", grid=(n_tiles,),
        out_shape=(jax.ShapeDtypeStruct((N_Q_HEADS, t, LANES), F32),) + tuple(_slots(a) for a in extra),
        in_specs=[_heads(N_Q_HEADS, LANES), _full((N_KV_HEADS, t, LANES)), _full((N_KV_HEADS, t, LANES))] + [hbm] * n_sh,
        out_specs=(_heads(N_Q_HEADS, LANES),) + (hbm,) * n_sh,
        scratch_shapes=_comm_scratch(GATHER_SEMS) if shards is not None else [],
        compiler_params=_cparams(),
    )(q, k, v, *extra)
    return outs[0], tuple(outs[1:])


def _halo_specs(width, t):
    last = t // HALO - 1
    per = TM // HALO
    prev = pl.BlockSpec((HALO, width), lambda i: (jnp.maximum(i * per - 1, 0), 0))
    nxt = pl.BlockSpec((HALO, width), lambda i: (jnp.minimum((i + 1) * per, last), 0))
    return prev, nxt


def _halo_valid(i, n_tiles):
    prev_ok = jnp.logical_and(i != 0, i != N_CTX_TILES)
    next_ok = jnp.logical_and(i != N_CTX_TILES - 1, i != n_tiles - 1)
    return jnp.where(prev_ok, 1.0, 0.0), jnp.where(next_ok, 1.0, 0.0)


def _conv_inputs(pc):
    u = pc[:, 0:GROUP_W] * pc[:, GROUP_W:2 * GROUP_W]
    z = pc[:, 2 * GROUP_W:3 * GROUP_W] * _sigmoid(pc[:, 3 * GROUP_W:4 * GROUP_W])
    return u, z


def _fill_ext(ext_ref, prev, mid, nxt):
    ext_ref[0:HALO, :] = prev
    ext_ref[HALO:HALO + TM, :] = mid
    ext_ref[HALO + TM:HALO + TM + HALO, :] = nxt


def _row_local_mixers(pr, ca, z2, oe, vecs, wss_ref, bsm, lane256):
    a_b, a_g, b_g = pr[:, 0:256], pr[:, 256:512], pr[:, 512:768]
    c_u, c_v, c_g, d_g = pr[:, 768:1024], pr[:, 1024:1280], pr[:, 1280:1536], pr[:, 1536:1792]
    zn, rs_b = _layer_norm_stats(z2)
    tb = zn * vecs[1:2, :] + vecs[2:3, :]
    vn_hat, rs_c = _layer_norm_stats(c_v)
    vn = vn_hat * vecs[3:4, :] + vecs[4:5, :]
    grp = jnp.right_shift(lane256, 6)
    sgs = []
    for ch in range(TM // CHUNK):
        r = _dot(wss_ref[...], vn[ch * CHUNK:(ch + 1) * CHUNK, :].astype(BF16))
        sgs.append(_group_select(r, grp[0:CHUNK]) + bsm)
    sg = jnp.concatenate(sgs, axis=0)
    lane = _lane(TM)
    lo = lane < HEAD_DIM
    att = jnp.concatenate([jnp.where(lo, oe[2 * b], pltpu.roll(oe[2 * b + 1], HEAD_DIM, 1)) for b in range(2)], axis=1)
    return dict(a_b=a_b, a_g=a_g, b_g=b_g, c_u=c_u, c_v=c_v, c_g=c_g, d_g=d_g, zn=zn, rs_b=rs_b, tb=tb,
                vn_hat=vn_hat, rs_c=rs_c, vn=vn, sg=sg, att=att, grp=grp, lo=lo, lane=lane)


def _mixer_concat(f, ca):
    ya = f["a_b"] * ca
    yb = f["tb"] * _sigmoid(f["tb"])
    yc = f["c_u"] * f["sg"]
    gates = [f[n] * _sigmoid(f[n]) for n in ("a_g", "b_g", "c_g", "d_g")]
    ys = (ya, yb, yc, f["att"])
    big = jnp.concatenate([yy * gg for yy, gg in zip(ys, gates)], axis=1).astype(BF16)
    return big, ys, gates


def _mix_out(pc, pr, oe, xt, modv, g_post, w_out, conv_a, conv_b, vecs, wss, bsm):
    t = xt.shape[0]
    n_tiles = t // TM
    prev_spec, next_spec = _halo_specs(W_C, t)

    def body(pc_ref, pp_ref, pn_ref, pr_ref, oe_ref, x_ref, mod_ref, gp_ref, wo_ref, cva_ref, cvb_ref, vec_ref, wss_ref, bsm_ref,
             xo_ref, y_ref, ca_ref, z2_ref, uext, zext):
        i = pl.program_id(0)
        is_ctx = i < N_CTX_TILES
        pv, nv = _halo_valid(i, n_tiles)
        u, z = _conv_inputs(pc_ref[...])
        up, zp = _conv_inputs(pp_ref[...])
        un, zn_ = _conv_inputs(pn_ref[...])
        _fill_ext(uext, up * pv, u, un * nv)
        _fill_ext(zext, zp * pv, z, zn_ * nv)
        ca = cva_ref[0:1, :] * uext[pl.ds(HALO - 1, TM), :]
        for kk in range(1, SHORT_CONV_K):
            ca = ca + cva_ref[kk:kk + 1, :] * uext[pl.ds(HALO - 1 + kk, TM), :]
        z2 = cvb_ref[0:1, :] * zext[pl.ds(HALO - CONFORMER_K // 2, TM), :]
        for kk in range(1, CONFORMER_K):
            z2 = z2 + cvb_ref[kk:kk + 1, :] * zext[pl.ds(HALO - CONFORMER_K // 2 + kk, TM), :]
        vecs = vec_ref[...]
        z2 = z2 + vecs[0:1, :]
        ca_ref[...] = ca
        z2_ref[...] = z2
        lane256 = lax.broadcasted_iota(jnp.int32, (TM, GROUP_W), 1)
        f = _row_local_mixers(pr_ref[...], ca, z2, oe_ref, vecs, wss_ref, bsm_ref[...], lane256)
        big, _, _ = _mixer_concat(f, ca)
        y = _dot(big, wo_ref[...])
        y_ref[...] = y
        ry = lax.rsqrt(jnp.mean(y * y, axis=1, keepdims=True) + RMS_EPS)
        gt = jnp.where(is_ctx, mod_ref[2:3, :], mod_ref[5:6, :])
        xo_ref[...] = x_ref[...] + gt * (y * ry * gp_ref[...])

    return _pc(
        body, name="mix_out", grid=(n_tiles,),
        out_shape=(jax.ShapeDtypeStruct((t, D_MODEL), F32), jax.ShapeDtypeStruct((t, D_MODEL), F32),
                   jax.ShapeDtypeStruct((t, GROUP_W), F32), jax.ShapeDtypeStruct((t, GROUP_W), F32)),
        in_specs=[_rows(W_C), prev_spec, next_spec, _rows(W_R), _heads(N_Q_HEADS, LANES), _rows(D_MODEL),
                  _full((8, D_MODEL)), _full((1, D_MODEL)), _full((D_MODEL, D_MODEL)),
                  _full((8, GROUP_W)), _full((32, GROUP_W)), _full((8, GROUP_W)),
                  _full((N_SPATIAL_GROUPS * CHUNK, CHUNK)), _full((CHUNK, GROUP_W))],
        out_specs=(_rows(D_MODEL), _rows(D_MODEL), _rows(GROUP_W), _rows(GROUP_W)),
        scratch_shapes=[pltpu.VMEM((TM + 2 * HALO, GROUP_W), F32), pltpu.VMEM((TM + 2 * HALO, GROUP_W), F32)],
        compiler_params=_cparams(),
    )(pc, pc, pc, pr, oe, xt, modv, g_post, w_out, conv_a, conv_b, vecs, wss, bsm)


def _loss_head(xt, target):
    t = xt.shape[0]

    def body(x_ref, t_ref, dx_ref, loss_ref):
        i = pl.program_id(0)

        @pl.when(i == 0)
        def _():
            loss_ref[...] = jnp.zeros_like(loss_ref)

        lat = jnp.where(i < N_CTX_TILES, 0.0, 1.0)
        err = (x_ref[...] - t_ref[...]) * lat
        dx_ref[...] = err * (1.0 / D_MODEL)
        loss_ref[...] += jnp.sum(err * err) * (0.5 / D_MODEL)

    return _pc(
        body, name="loss_head", grid=(t // TM,),
        out_shape=(jax.ShapeDtypeStruct((t, D_MODEL), F32), jax.ShapeDtypeStruct((8, LANES), F32)),
        in_specs=[_rows(D_MODEL), pl.BlockSpec((TM, D_MODEL), lambda i: (jnp.maximum(i - N_CTX_TILES, 0), 0))],
        out_specs=(_rows(D_MODEL), _full((8, LANES))),
        compiler_params=_cparams(),
    )(xt, target)


def _mix_out_bwd(dxo, y, pr, ca, z2, oe, modv, g_post, w_out, vecs, wss, wsts, bsm):
    t = y.shape[0]
    n_tiles = t // TM

    def body(dxo_ref, y_ref, pr_ref, ca_ref, z2_ref, oe_ref, mod_ref, gp_ref, wo_ref, vec_ref, wss_ref, wsts_ref, bsm_ref,
             dpr_ref, ga_ref, gb_ref, doe_ref, dwo_ref, pvec_ref, s256_ref, dws_ref, dbs_ref, dbsm):
        i = pl.program_id(0)
        is_ctx = i < N_CTX_TILES

        @pl.when(i == 0)
        def _():
            dwo_ref[...] = jnp.zeros_like(dwo_ref)
            pvec_ref[...] = jnp.zeros_like(pvec_ref)
            s256_ref[...] = jnp.zeros_like(s256_ref)
            dws_ref[...] = jnp.zeros_like(dws_ref)
            dbsm[...] = jnp.zeros_like(dbsm)

        dxo_ = dxo_ref[...]
        y_ = y_ref[...]
        ry = lax.rsqrt(jnp.mean(y_ * y_, axis=1, keepdims=True) + RMS_EPS)
        nh = y_ * ry
        gp = gp_ref[...]
        gt = jnp.where(is_ctx, mod_ref[2:3, :], mod_ref[5:6, :])
        dgt = _colsum(dxo_ * (nh * gp))
        pvec_ref[0:1, :] += jnp.where(is_ctx, dgt, 0.0)
        pvec_ref[1:2, :] += jnp.where(is_ctx, 0.0, dgt)
        dn = dxo_ * gt
        pvec_ref[2:3, :] += _colsum(dn * nh)
        dnh = dn * gp
        dy = ry * (dnh - nh * jnp.mean(dnh * nh, axis=1, keepdims=True))

        vecs = vec_ref[...]
        bsm_ = bsm_ref[...]
        ca_ = ca_ref[...]
        lane256 = lax.broadcasted_iota(jnp.int32, (TM, GROUP_W), 1)
        f = _row_local_mixers(pr_ref[...], ca_, z2_ref[...], oe_ref, vecs, wss_ref, bsm_, lane256)
        big, ys, gates = _mixer_concat(f, ca_)
        dyb = dy.astype(BF16)
        dwo_ref[...] += _dot_tn(big, dyb)
        dbig = _dot_nt(dyb, wo_ref[...])

        d_y, d_gate = [], []
        for n, (name, yy, gg) in enumerate(zip(("a_g", "b_g", "c_g", "d_g"), ys, gates)):
            dpart = dbig[:, n * GROUP_W:(n + 1) * GROUP_W]
            gx = f[name]
            sg_ = _sigmoid(gx)
            d_y.append(dpart * gg)
            d_gate.append(dpart * yy * (sg_ * (1.0 + gx * (1.0 - sg_))))
        dya, dyb_, dyc, datt = d_y

        d_ab = dya * ca_
        ga_ref[...] = dya * f["a_b"]
        tb = f["tb"]
        sb = _sigmoid(tb)
        dtb = dyb_ * (sb * (1.0 + tb * (1.0 - sb)))
        s256_ref[1:2, :] += _colsum(dtb * f["zn"])
        s256_ref[2:3, :] += _colsum(dtb)
        dz2 = _layer_norm_bwd(dtb * vecs[1:2, :], f["zn"], f["rs_b"])
        gb_ref[...] = dz2
        s256_ref[0:1, :] += _colsum(dz2)
        d_cu = dyc * f["sg"]
        dsg = dyc * f["c_u"]
        grp = f["grp"]
        dvn_parts = []
        for ch in range(TM // CHUNK):
            rows = slice(ch * CHUNK, (ch + 1) * CHUNK)
            dsg_c = dsg[rows, :]
            dbsm[...] += dsg_c
            vn_c = f["vn"][rows, :].astype(BF16)
            for g in range(N_SPATIAL_GROUPS):
                masked = jnp.where(grp[0:CHUNK] == g, dsg_c, 0.0).astype(BF16)
                dws_ref[g * CHUNK:(g + 1) * CHUNK, :] += _dot_nt(masked, vn_c)
            dvn_parts.append(_group_select(_dot(wsts_ref[...], dsg_c.astype(BF16)), grp[0:CHUNK]))
        dvn = jnp.concatenate(dvn_parts, axis=0)
        s256_ref[3:4, :] += _colsum(dvn * f["vn_hat"])
        s256_ref[4:5, :] += _colsum(dvn)
        d_cv = _layer_norm_bwd(dvn * vecs[3:4, :], f["vn_hat"], f["rs_c"])
        lane, lo = f["lane"], f["lo"]
        att = f["att"]
        for b in range(2):
            da = datt[:, b * LANES:(b + 1) * LANES]
            prod = da * att[:, b * LANES:(b + 1) * LANES]
            for hh in range(2):
                h = 2 * b + hh
                lse = _rowsum(jnp.where(lane == HEAD_DIM, oe_ref[h], 0.0))
                delta = _rowsum(jnp.where(lo, prod, 0.0) if hh == 0 else jnp.where(lo, 0.0, prod))
                dah = da if hh == 0 else pltpu.roll(da, HEAD_DIM, 1)
                doe_ref[h] = jnp.where(lo, dah, jnp.where(lane == HEAD_DIM, delta, jnp.where(lane == HEAD_DIM + 1, lse, 0.0)))

        dpr_ref[...] = jnp.concatenate([d_ab, d_gate[0], d_gate[1], d_cu, d_cv, d_gate[2], d_gate[3]], axis=1).astype(BF16)

        @pl.when(i == n_tiles - 1)
        def _():
            acc = dbsm[...]
            lane128 = _lane(CHUNK)
            out = jnp.zeros((CHUNK, LANES), F32)
            for g in range(N_SPATIAL_GROUPS):
                col = _rowsum(jnp.where(grp[0:CHUNK] == g, acc, 0.0))
                out = out + jnp.where(lane128 == g, col, 0.0)
            dbs_ref[...] = out

    return _pc(
        body, name="mix_out_bwd", grid=(n_tiles,),
        out_shape=(jax.ShapeDtypeStruct((t, W_R), BF16),
                   jax.ShapeDtypeStruct((t, GROUP_W), F32), jax.ShapeDtypeStruct((t, GROUP_W), F32),
                   jax.ShapeDtypeStruct((N_Q_HEADS, t, LANES), F32),
                   jax.ShapeDtypeStruct((D_MODEL, D_MODEL), F32),
                   jax.ShapeDtypeStruct((8, D_MODEL), F32),
                   jax.ShapeDtypeStruct((8, GROUP_W), F32),
                   jax.ShapeDtypeStruct((N_SPATIAL_GROUPS * CHUNK, CHUNK), F32),
                   jax.ShapeDtypeStruct((CHUNK, LANES), F32)),
        in_specs=[_rows(D_MODEL), _rows(D_MODEL), _rows(W_R), _rows(GROUP_W), _rows(GROUP_W), _heads(N_Q_HEADS, LANES),
                  _full((8, D_MODEL)), _full((1, D_MODEL)), _full((D_MODEL, D_MODEL)), _full((8, GROUP_W)),
                  _full((N_SPATIAL_GROUPS * CHUNK, CHUNK)), _full((N_SPATIAL_GROUPS * CHUNK, CHUNK)), _full((CHUNK, GROUP_W))],
        out_specs=(_rows(W_R), _rows(GROUP_W), _rows(GROUP_W), _heads(N_Q_HEADS, LANES),
                   _full((D_MODEL, D_MODEL)), _full((8, D_MODEL)), _full((8, GROUP_W)),
                   _full((N_SPATIAL_GROUPS * CHUNK, CHUNK)), _full((CHUNK, LANES))),
        scratch_shapes=[pltpu.VMEM((CHUNK, GROUP_W), F32)],
        compiler_params=_cparams(),
    )(dxo, y, pr, ca, z2, oe, modv, g_post, w_out, vecs, wss, wsts, bsm)


def _conv_bwd(pc, g_a, g_b, conv_a, conv_b):
    t = pc.shape[0]
    n_tiles = t // TM
    pc_prev, pc_next = _halo_specs(W_C, t)
    g_prev, g_next = _halo_specs(GROUP_W, t)

    def body(pc_ref, pp_ref, pn_ref, ga_ref, gap_ref, gan_ref, gb_ref, gbp_ref, gbn_ref, cva_ref, cvb_ref,
             dpc_ref, dca_ref, dcb_ref, uext, zext, gaext, gbext):
        i = pl.program_id(0)

        @pl.when(i == 0)
        def _():
            dca_ref[...] = jnp.zeros_like(dca_ref)
            dcb_ref[...] = jnp.zeros_like(dcb_ref)

        pv, nv = _halo_valid(i, n_tiles)
        pc_ = pc_ref[...]
        u, z = _conv_inputs(pc_)
        up, zp = _conv_inputs(pp_ref[...])
        un, zn_ = _conv_inputs(pn_ref[...])
        _fill_ext(uext, up * pv, u, un * nv)
        _fill_ext(zext, zp * pv, z, zn_ * nv)
        ga = ga_ref[...]
        gb = gb_ref[...]
        _fill_ext(gaext, gap_ref[...] * pv, ga, gan_ref[...] * nv)
        _fill_ext(gbext, gbp_ref[...] * pv, gb, gbn_ref[...] * nv)

        du = cva_ref[0:1, :] * gaext[pl.ds(HALO + 1, TM), :]
        dca_ref[0:1, :] += _colsum(ga * uext[pl.ds(HALO - 1, TM), :])
        for kk in range(1, SHORT_CONV_K):
            du = du + cva_ref[kk:kk + 1, :] * gaext[pl.ds(HALO + 1 - kk, TM), :]
            dca_ref[kk:kk + 1, :] += _colsum(ga * uext[pl.ds(HALO - 1 + kk, TM), :])
        half = CONFORMER_K // 2
        dz = cvb_ref[0:1, :] * gbext[pl.ds(HALO + half, TM), :]
        dcb_ref[0:1, :] += _colsum(gb * zext[pl.ds(HALO - half, TM), :])
        for kk in range(1, CONFORMER_K):
            dz = dz + cvb_ref[kk:kk + 1, :] * gbext[pl.ds(HALO + half - kk, TM), :]
            dcb_ref[kk:kk + 1, :] += _colsum(gb * zext[pl.ds(HALO - half + kk, TM), :])

        a_c, a_h = pc_[:, 0:GROUP_W], pc_[:, GROUP_W:2 * GROUP_W]
        glu_a, glu_g = pc_[:, 2 * GROUP_W:3 * GROUP_W], pc_[:, 3 * GROUP_W:4 * GROUP_W]
        sg = _sigmoid(glu_g)
        dpc_ref[...] = jnp.concatenate([du * a_h, du * a_c, dz * sg, dz * glu_a * sg * (1.0 - sg)], axis=1).astype(BF16)

    ext = pltpu.VMEM((TM + 2 * HALO, GROUP_W), F32)
    return _pc(
        body, name="conv_bwd", grid=(n_tiles,),
        out_shape=(jax.ShapeDtypeStruct((t, W_C), BF16), jax.ShapeDtypeStruct((8, GROUP_W), F32), jax.ShapeDtypeStruct((32, GROUP_W), F32)),
        in_specs=[_rows(W_C), pc_prev, pc_next, _rows(GROUP_W), g_prev, g_next, _rows(GROUP_W), g_prev, g_next,
                  _full((8, GROUP_W)), _full((32, GROUP_W))],
        out_specs=(_rows(W_C), _full((8, GROUP_W)), _full((32, GROUP_W))),
        scratch_shapes=[ext, ext, ext, ext],
        compiler_params=_cparams(),
    )(pc, pc, pc, g_a, g_a, g_a, g_b, g_b, g_b, conv_a, conv_b)


def _attention_bwd(q, k, v, doe, slabs=None):
    t = q.shape[1]
    tkl = _kv_chunk(t - CTX_LEN)
    n_lat = (t - CTX_LEN) // tkl
    n_tiles = t // TM
    n_sl = 0 if slabs is None else len(slabs)

    def body(q_ref, do_ref, k_ref, v_ref, *rest):
        i = pl.program_id(0)
        dq_ref, dk_hbm, dv_hbm = rest[n_sl:n_sl + 3]
        dk_acc, dv_acc = rest[2 * n_sl + 3:2 * n_sl + 5]
        pairs = tuple(zip(rest[:n_sl], rest[n_sl + 3:2 * n_sl + 3]))
        sems = rest[2 * n_sl + 5:]

        @pl.when(i == 0)
        def _():
            dk_acc[...] = jnp.zeros_like(dk_acc)
            dv_acc[...] = jnp.zeros_like(dv_acc)
            if slabs is not None:
                _scatter_phase(0, pairs, *sems)

        steps = jnp.where(i < N_CTX_TILES, 0, n_lat)
        lane = _lane(GQA * TM)
        lo = lane < HEAD_DIM
        qs, dos, deltas, lses = [], [], [], []
        for g in range(N_KV_HEADS):
            qs.append(jnp.concatenate([q_ref[GQA * g + hh] for hh in range(GQA)], axis=0))
            dog = jnp.concatenate([do_ref[GQA * g + hh] for hh in range(GQA)], axis=0)
            deltas.append(_rowsum(jnp.where(lane == HEAD_DIM, dog, 0.0)))
            lses.append(_rowsum(jnp.where(lane == HEAD_DIM + 1, dog, 0.0)))
            dos.append(jnp.where(lo, dog, 0.0).astype(BF16))

        def step(st, size, dqs):
            out = []
            for g in range(N_KV_HEADS):
                kc = k_ref[g, pl.ds(st, size), :]
                vc = v_ref[g, pl.ds(st, size), :]
                p = jnp.exp(_dot_nt(qs[g], kc) - lses[g])
                ds_ = (p * (_dot_nt(dos[g], vc) - deltas[g])).astype(BF16)
                dk_acc[g, pl.ds(st, size), :] += _dot_tn(ds_, qs[g])
                dv_acc[g, pl.ds(st, size), :] += _dot_tn(p.astype(BF16), dos[g])
                out.append(dqs[g] + _dot(ds_, kc))
            return tuple(out)

        dqs = step(0, CTX_LEN, tuple(jnp.zeros((GQA * TM, LANES), F32) for _ in range(N_KV_HEADS)))
        dqs = lax.fori_loop(0, steps, lambda j, acc: step(pl.multiple_of(CTX_LEN + j * tkl, 256), tkl, acc), dqs)
        for g in range(N_KV_HEADS):
            for hh in range(GQA):
                dq_ref[GQA * g + hh] = dqs[g][hh * TM:(hh + 1) * TM]

        @pl.when(i == n_tiles - 1)
        def _():
            pltpu.sync_copy(dk_acc, dk_hbm)
            pltpu.sync_copy(dv_acc, dv_hbm)
            if slabs is not None:
                _scatter_phase(1, pairs, *sems)

    kv_shape = jax.ShapeDtypeStruct((N_KV_HEADS, t, LANES), F32)
    hbm = pl.BlockSpec(memory_space=pl.ANY)
    extra = () if slabs is None else tuple(slabs)
    outs = _pc(
        body, name="attention_bwd" if slabs is None else "attention_bwd_scatter", grid=(n_tiles,),
        out_shape=(jax.ShapeDtypeStruct((N_Q_HEADS, t, LANES), F32), kv_shape, kv_shape)
        + tuple(jax.ShapeDtypeStruct(a.shape, a.dtype) for a in extra),
        in_specs=[_heads(N_Q_HEADS, LANES), _heads(N_Q_HEADS, LANES),
                  _full((N_KV_HEADS, t, LANES)), _full((N_KV_HEADS, t, LANES))] + [hbm] * n_sl,
        out_specs=(_heads(N_Q_HEADS, LANES), hbm, hbm) + (hbm,) * n_sl,
        scratch_shapes=[pltpu.VMEM((N_KV_HEADS, t, LANES), F32), pltpu.VMEM((N_KV_HEADS, t, LANES), F32)]
        + (_comm_scratch(SCATTER_SEMS) if slabs is not None else []),
        compiler_params=_cparams(),
    )(q, doe, k, v, *extra)
    return outs[0], outs[1], outs[2], tuple(outs[3:])


def _qkv_bwd(dq, dk, dv, pq, qk_gain, cos_t, sin_t):
    t = pq.shape[0]
    n_tiles = t // TM

    def body(dq_ref, dk_ref, dv_ref, pq_ref, gain_ref, cos_ref, sin_ref, dpq_ref, dgain_ref):
        i = pl.program_id(0)

        @pl.when(i == 0)
        def _():
            dgain_ref[...] = jnp.zeros_like(dgain_ref)

        lane = _lane(TM)
        lo = lane < HEAD_DIM
        lo16 = (lane & 31) < 16
        cos = cos_ref[...]
        sin = sin_ref[...]
        pq_ = pq_ref[...]
        outs = []
        for b in range(3):
            src = dq_ref if b < 2 else dk_ref
            base = 2 * b if b < 2 else 0
            drot = src[base] + pltpu.roll(src[base + 1], HEAD_DIM, 1)
            if b < 2:
                drot = drot * ATTN_SCALE
            dxg = drot * cos + _swap16(drot * sin, lo16)
            xh, r = _head_norm(pq_[:, b * LANES:(b + 1) * LANES], lo)
            row = 0 if b < 2 else 1
            dgain_ref[row:row + 1, :] += _colsum(dxg * xh)
            dxh = dxg * gain_ref[row:row + 1, :]
            outs.append(r * (dxh - xh * (_pair_sums(dxh * xh, lo) * (1.0 / HEAD_DIM))))
        outs.append(dv_ref[0] + pltpu.roll(dv_ref[1], HEAD_DIM, 1))
        dpq_ref[...] = jnp.concatenate(outs, axis=1).astype(BF16)

    return _pc(
        body, name="qkv_bwd", grid=(n_tiles,),
        out_shape=(jax.ShapeDtypeStruct((t, W_Q), BF16), jax.ShapeDtypeStruct((8, LANES), F32)),
        in_specs=[_heads(N_Q_HEADS, LANES), _heads(N_KV_HEADS, LANES), _heads(N_KV_HEADS, LANES), _rows(W_Q),
                  _full((8, LANES)), _rows(LANES), _rows(LANES)],
        out_specs=(_rows(W_Q), _full((8, LANES))),
        compiler_params=_cparams(),
    )(dq, dk, dv, pq, qk_gain, cos_t, sin_t)


def _in_proj_bwd(dpc, dpr, dpq, w_c, w_r, w_q, xt, dxo, modv, g_pre):
    t = xt.shape[0]

    def body(dpc_ref, dpr_ref, dpq_ref, wc_ref, wr_ref, wq_ref, x_ref, dxo_ref, mod_ref, g_ref, dx_ref, acc_ref):
        i = pl.program_id(0)
        is_ctx = i < N_CTX_TILES

        @pl.when(i == 0)
        def _():
            acc_ref[...] = jnp.zeros_like(acc_ref)

        dh = _dot_nt(dpc_ref[...], wc_ref[...]) + _dot_nt(dpr_ref[...], wr_ref[...]) + _dot_nt(dpq_ref[...], wq_ref[...])
        x = x_ref[...]
        r = lax.rsqrt(jnp.mean(x * x, axis=1, keepdims=True) + RMS_EPS)
        xn = x * r
        g = g_ref[...]
        sc = jnp.where(is_ctx, mod_ref[1:2, :], mod_ref[4:5, :])
        dsh = _colsum(dh)
        dsc = _colsum(dh * (xn * g))
        acc_ref[0:1, :] += jnp.where(is_ctx, dsh, 0.0)
        acc_ref[1:2, :] += jnp.where(is_ctx, dsc, 0.0)
        acc_ref[2:3, :] += jnp.where(is_ctx, 0.0, dsh)
        acc_ref[3:4, :] += jnp.where(is_ctx, 0.0, dsc)
        dxg = dh * (1.0 + sc)
        acc_ref[4:5, :] += _colsum(dxg * xn)
        dxn = dxg * g
        dx_ref[...] = r * (dxn - xn * jnp.mean(dxn * xn, axis=1, keepdims=True)) + dxo_ref[...]

    return _pc(
        body, name="in_proj_bwd", grid=(t // TM,),
        out_shape=(jax.ShapeDtypeStruct((t, D_MODEL), F32), jax.ShapeDtypeStruct((8, D_MODEL), F32)),
        in_specs=[_rows(W_C), _rows(W_R), _rows(W_Q), _full((D_MODEL, W_C)), _full((D_MODEL, W_R)), _full((D_MODEL, W_Q)),
                  _rows(D_MODEL), _rows(D_MODEL), _full((8, D_MODEL)), _full((1, D_MODEL))],
        out_specs=(_rows(D_MODEL), _full((8, D_MODEL))),
        compiler_params=_cparams(),
    )(dpc, dpr, dpq, w_c, w_r, w_q, xt, dxo, modv, g_pre)


def _in_proj_wgrad(h, dpc, dpr, dpq):
    t = h.shape[0]

    def body(h_ref, dpc_ref, dpr_ref, dpq_ref, gc_ref, gr_ref, gq_ref):
        @pl.when(pl.program_id(0) == 0)
        def _():
            gc_ref[...] = jnp.zeros_like(gc_ref)
            gr_ref[...] = jnp.zeros_like(gr_ref)
            gq_ref[...] = jnp.zeros_like(gq_ref)

        hb = h_ref[...]
        gc_ref[...] += _dot_tn(hb, dpc_ref[...])
        gr_ref[...] += _dot_tn(hb, dpr_ref[...])
        gq_ref[...] += _dot_tn(hb, dpq_ref[...])

    return _pc(
        body, name="in_proj_wgrad", grid=(t // TM,),
        out_shape=(jax.ShapeDtypeStruct((D_MODEL, W_C), F32), jax.ShapeDtypeStruct((D_MODEL, W_R), F32),
                   jax.ShapeDtypeStruct((D_MODEL, W_Q), F32)),
        in_specs=[_rows(D_MODEL), _rows(W_C), _rows(W_R), _rows(W_Q)],
        out_specs=(_full((D_MODEL, W_C)), _full((D_MODEL, W_R)), _full((D_MODEL, W_Q))),
        compiler_params=_cparams(),
    )(h, dpc, dpr, dpq)


def _sum_slabs(slabs, tile_rows):
    n, r, c = slabs.shape

    def body(s_ref, o_ref):
        acc = s_ref[0].astype(F32)
        for k in range(1, n):
            acc = acc + s_ref[k].astype(F32)
        o_ref[...] = acc

    return _pc(
        body, name="sum_slabs", grid=(r // tile_rows,),
        out_shape=jax.ShapeDtypeStruct((r, c), F32),
        in_specs=[pl.BlockSpec((n, tile_rows, c), lambda i: (0, i, 0))],
        out_specs=pl.BlockSpec((tile_rows, c), lambda i: (i, 0)),
        compiler_params=_cparams(),
    )(slabs)


def _sum_layer_slabs(slabs, tile_rows):
    nl, n, r, c = slabs.shape
    per = r // tile_rows

    def body(s_ref, o_ref):
        acc = s_ref[0, 0].astype(F32)
        for k in range(1, n):
            acc = acc + s_ref[0, k].astype(F32)
        o_ref[...] = acc

    return _pc(
        body, name="sum_layer_slabs", grid=(nl * per,),
        out_shape=jax.ShapeDtypeStruct((nl * r, c), F32),
        in_specs=[pl.BlockSpec((1, n, tile_rows, c), lambda i: (i // per, 0, i % per, 0))],
        out_specs=pl.BlockSpec((tile_rows, c), lambda i: (i, 0)),
        compiler_params=_cparams(),
    )(slabs)


def _adamw(grads, w, m, v, tile_rows):
    r, c = w.shape
    n_g = len(grads)

    def body(*refs):
        g = refs[0][...]
        for k in range(1, n_g):
            g = g + refs[k][...]
        w_ref, m_ref, v_ref, g_out, d_out, m_out, v_out = refs[n_g:]
        m_new = ADAM_B1 * m_ref[...] + (1.0 - ADAM_B1) * g
        v_new = ADAM_B2 * v_ref[...] + (1.0 - ADAM_B2) * (g * g)
        m_hat = m_new / (1.0 - ADAM_B1 ** ADAM_STEP)
        v_hat = v_new / (1.0 - ADAM_B2 ** ADAM_STEP)
        g_out[...] = g
        d_out[...] = -ADAM_LR * (m_hat / (jnp.sqrt(v_hat) + ADAM_EPS) + ADAM_WD * w_ref[...])
        m_out[...] = m_new
        v_out[...] = v_new

    spec = pl.BlockSpec((tile_rows, c), lambda i: (i, 0))
    shape = jax.ShapeDtypeStruct((r, c), F32)
    return _pc(
        body, name="adamw", grid=(r // tile_rows,),
        out_shape=(shape,) * 4, in_specs=[spec] * (n_g + 3), out_specs=(spec,) * 4,
        compiler_params=_cparams(),
    )(*grads, w, m, v)


def _rope_tables(s_lat):
    pos = jnp.arange(s_lat)
    pos_row = (pos // GRID_W).astype(F32)
    pos_col = (pos % GRID_W).astype(F32)
    axis_dim = HEAD_DIM // 2
    inv_freq = 1.0 / (ROPE_THETA ** (jnp.arange(0, axis_dim, 2, dtype=F32) / axis_dim))
    d = np.arange(LANES) % HEAD_DIM
    on_rows = (d // axis_dim) == 0
    freq = d % (axis_dim // 2)
    sign = np.where((d % axis_dim) < axis_dim // 2, -1.0, 1.0).astype(np.float32)
    ang = jnp.where(on_rows[None, :], pos_row[:, None], pos_col[:, None]) * inv_freq[freq][None, :]
    cos = jnp.concatenate([jnp.ones((CTX_LEN, LANES), F32), jnp.cos(ang)], axis=0)
    sin = jnp.concatenate([jnp.zeros((CTX_LEN, LANES), F32), jnp.sin(ang) * sign[None, :]], axis=0)
    return cos, sin


def _pad_rows(a, rows):
    return jnp.concatenate([a, jnp.zeros((rows - a.shape[0],) + a.shape[1:], a.dtype)], axis=0)


_SMALL = ("c_ctx", "b_mod", "g_pre", "g_post", "conv_a", "conv_b", "conv_b_bias", "conf_ln_g", "conf_ln_b",
          "sgu_ln_g", "sgu_ln_b", "w_s", "b_s", "q_gain", "k_gain")


def _pack(arrays):
    flat = jnp.concatenate([a.reshape(-1) for a in arrays])
    rows = -(-flat.shape[0] // (8 * LANES)) * 8
    return _pad_rows(flat.reshape(-1, 1), rows * LANES).reshape(rows, LANES)


def _unpack(packed, shapes):
    flat = packed.reshape(-1)
    out, off = [], 0
    for s in shapes:
        n = int(np.prod(s))
        out.append(flat[off:off + n].reshape(s))
        off += n
    return out


def kernel(x, c, ctx, c_ctx, w_mod, b_mod, g_pre, g_post, w_in, w_out, conv_a, conv_b, conv_b_bias, conf_ln_g, conf_ln_b, sgu_ln_g, sgu_ln_b, w_s, b_s, q_gain, k_gain, loss_target, m_c_ctx, m_w_mod, m_b_mod, m_g_pre, m_g_post, m_w_in, m_w_out, m_conv_a, m_conv_b, m_conv_b_bias, m_conf_ln_g, m_conf_ln_b, m_sgu_ln_g, m_sgu_ln_b, m_w_s, m_b_s, m_q_gain, m_k_gain, v_c_ctx, v_w_mod, v_b_mod, v_g_pre, v_g_post, v_w_in, v_w_out, v_conv_a, v_conv_b, v_conv_b_bias, v_conf_ln_g, v_conf_ln_b, v_sgu_ln_g, v_sgu_ln_b, v_w_s, v_b_s, v_q_gain, v_k_gain):
    weights = dict(c_ctx=c_ctx, w_mod=w_mod, b_mod=b_mod, g_pre=g_pre, g_post=g_post, w_in=w_in, w_out=w_out, conv_a=conv_a,
                   conv_b=conv_b, conv_b_bias=conv_b_bias, conf_ln_g=conf_ln_g, conf_ln_b=conf_ln_b, sgu_ln_g=sgu_ln_g,
                   sgu_ln_b=sgu_ln_b, w_s=w_s, b_s=b_s, q_gain=q_gain, k_gain=k_gain)
    m_in = dict(c_ctx=m_c_ctx, w_mod=m_w_mod, b_mod=m_b_mod, g_pre=m_g_pre, g_post=m_g_post, w_in=m_w_in, w_out=m_w_out,
                conv_a=m_conv_a, conv_b=m_conv_b, conv_b_bias=m_conv_b_bias, conf_ln_g=m_conf_ln_g, conf_ln_b=m_conf_ln_b,
                sgu_ln_g=m_sgu_ln_g, sgu_ln_b=m_sgu_ln_b, w_s=m_w_s, b_s=m_b_s, q_gain=m_q_gain, k_gain=m_k_gain)
    v_in = dict(c_ctx=v_c_ctx, w_mod=v_w_mod, b_mod=v_b_mod, g_pre=v_g_pre, g_post=v_g_post, w_in=v_w_in, w_out=v_w_out,
                conv_a=v_conv_a, conv_b=v_conv_b, conv_b_bias=v_conv_b_bias, conf_ln_g=v_conf_ln_g, conf_ln_b=v_conf_ln_b,
                sgu_ln_g=v_sgu_ln_g, sgu_ln_b=v_sgu_ln_b, w_s=v_w_s, b_s=v_b_s, q_gain=v_q_gain, k_gain=v_k_gain)
    order = ("c_ctx", "w_mod", "b_mod", "g_pre", "g_post", "w_in", "w_out", "conv_a", "conv_b", "conv_b_bias", "conf_ln_g",
             "conf_ln_b", "sgu_ln_g", "sgu_ln_b", "w_s", "b_s", "q_gain", "k_gain")

    s_lat = x.shape[1]
    ax, ay, ac = lax.axis_index("x"), lax.axis_index("y"), lax.axis_index("c")
    chip = 2 * ax + ay
    example = 4 * ax + 2 * ay + ac

    c_rows = _all_gather_rows(_pad_rows(c, 8))[::8]
    c16 = _pad_rows(jnp.concatenate([c_rows, c_ctx[None, :]], axis=0), 16)
    b_mod_shard = lax.dynamic_slice_in_dim(b_mod, chip * SHARD_MOD, SHARD_MOD, axis=1)[:, None, :]
    silu_c, mod_shard = _mod_forward(c16, w_mod, b_mod_shard)
    mod_all = _all_gather_rows(mod_shard.reshape(DEPTH * 16, SHARD_MOD)).reshape(8, DEPTH, 16, SHARD_MOD)
    mod_full = jnp.transpose(mod_all[::2], (1, 2, 0, 3)).reshape(DEPTH, 16, 3 * D_MODEL)
    mod_lat = lax.dynamic_index_in_dim(mod_full, example, axis=1, keepdims=False).reshape(DEPTH, 3, D_MODEL)
    mod_ctx = mod_full[:, 8].reshape(DEPTH, 3, D_MODEL)
    modv = jnp.concatenate([mod_ctx, mod_lat, jnp.zeros((DEPTH, 2, D_MODEL), F32)], axis=1)

    wi_b, wo_b = w_in.astype(BF16), w_out.astype(BF16)

    def regroup(gathered):
        wi_all, wo_all = gathered
        wi_full = jnp.concatenate([wi_all[k] for k in range(N_CHIPS)], axis=-1)
        wo_l = jnp.concatenate([wo_all[k] for k in range(N_CHIPS)], axis=0)
        wc_l = jnp.concatenate([wi_full[:, 256:768], wi_full[:, 1024:1536]], axis=-1)
        wr_l = jnp.concatenate([wi_full[:, 0:256], wi_full[:, 768:1024], wi_full[:, 1536:2560], wi_full[:, 3072:3328]], axis=-1)
        return wc_l, wr_l, wi_full[:, 2560:3072], wo_l

    w_c, w_r, w_q, wo_full = [None] * DEPTH, [None] * DEPTH, [None] * DEPTH, [None] * DEPTH
    w_c[0], w_r[0], w_q[0], wo_full[0] = regroup(_gather_weights(wi_b[0], wo_b[0]))

    cos_t, sin_t = _rope_tables(s_lat)
    conv_a_full = jnp.zeros((DEPTH, 8, GROUP_W), F32)
    conv_b_full = jnp.zeros((DEPTH, 32, GROUP_W), F32)
    conv_small = jnp.concatenate([conv_a.reshape(DEPTH * SHORT_CONV_K, -1), conv_b.reshape(DEPTH * CONFORMER_K, -1)], axis=0)
    n_cs = conv_small.shape[0]
    conv_rows = -(-n_cs // 8) * 8
    conv_all = _all_gather_rows(_pad_rows(conv_small, conv_rows)).reshape(8, conv_rows, -1)[::2]
    conv_all = jnp.transpose(conv_all, (1, 0, 2)).reshape(conv_rows, GROUP_W)
    conv_a_full = conv_a_full.at[:, :SHORT_CONV_K].set(conv_all[:DEPTH * SHORT_CONV_K].reshape(DEPTH, SHORT_CONV_K, GROUP_W))
    conv_b_full = conv_b_full.at[:, :CONFORMER_K].set(
        conv_all[DEPTH * SHORT_CONV_K:n_cs].reshape(DEPTH, CONFORMER_K, GROUP_W))

    vecs = jnp.stack([conv_b_bias, conf_ln_g, conf_ln_b, sgu_ln_g, sgu_ln_b] + [jnp.zeros_like(conv_b_bias)] * 3, axis=1)
    wss = w_s.reshape(DEPTH, N_SPATIAL_GROUPS * CHUNK, CHUNK).astype(BF16)
    wsts = jnp.swapaxes(w_s, 2, 3).reshape(DEPTH, N_SPATIAL_GROUPS * CHUNK, CHUNK).astype(BF16)
    bsm = jnp.repeat(jnp.swapaxes(b_s, 1, 2), HEAD_DIM, axis=2)
    qk_gain = jnp.concatenate([jnp.tile(q_gain, (1, 2))[:, None, :], jnp.tile(k_gain, (1, 2))[:, None, :],
                               jnp.zeros((DEPTH, 6, LANES), F32)], axis=1)

    xt = jnp.concatenate([ctx[0], x[0]], axis=0)
    saved = []
    for l in range(DEPTH):
        h, pc, pr, pq, q, k, v = _in_proj(xt, modv[l], g_pre[l][None, :], w_c[l], w_r[l], w_q[l], qk_gain[l], cos_t, sin_t)
        if l + 1 < DEPTH:
            oe, gathered = _attention_fwd(q, k, v, (wi_b[l + 1], wo_b[l + 1]))
            w_c[l + 1], w_r[l + 1], w_q[l + 1], wo_full[l + 1] = regroup(gathered)
        else:
            oe, _ = _attention_fwd(q, k, v)
        x_new, y, ca, z2 = _mix_out(pc, pr, oe, xt, modv[l], g_post[l][None, :], wo_full[l], conv_a_full[l], conv_b_full[l],
                                    vecs[l], wss[l], bsm[l])
        saved.append(dict(x=xt, h=h, pc=pc, pr=pr, pq=pq, q=q, k=k, v=v, oe=oe, y=y, ca=ca, z2=z2))
        xt = x_new
    dxo, loss_acc = _loss_head(xt, loss_target[0])
    loss = lax.psum(loss_acc[0, 0], ("x", "y", "c"))

    g_small = {n: [None] * DEPTH for n in _SMALL}
    d_mod, landed = [None] * DEPTH, [None] * DEPTH
    slabs = None
    for l in reversed(range(DEPTH)):
        s = saved[l]
        dpr, g_a, g_b, doe, gw_o, pvec, s256, dws, dbs = _mix_out_bwd(
            dxo, s["y"], s["pr"], s["ca"], s["z2"], s["oe"], modv[l], g_post[l][None, :], wo_full[l], vecs[l], wss[l], wsts[l], bsm[l])
        dpc, dca, dcb = _conv_bwd(s["pc"], g_a, g_b, conv_a_full[l], conv_b_full[l])
        dq, dk, dv, got = _attention_bwd(s["q"], s["k"], s["v"], doe, slabs)
        if slabs is not None:
            landed[l + 1] = got
        dpq, dgain = _qkv_bwd(dq, dk, dv, s["pq"], qk_gain[l], cos_t, sin_t)
        dxo, acc = _in_proj_bwd(dpc, dpr, dpq, w_c[l], w_r[l], w_q[l], s["x"], dxo, modv[l], g_pre[l][None, :])
        gw_c, gw_r, gw_q = _in_proj_wgrad(s["h"], dpc, dpr, dpq)
        gw_in = jnp.concatenate([gw_r[:, 0:256], gw_c[:, 0:512], gw_r[:, 256:512], gw_c[:, 512:1024],
                                 gw_r[:, 512:1536], gw_q, gw_r[:, 1536:1792]], axis=-1)
        slabs = (jnp.transpose(gw_in.reshape(D_MODEL, N_CHIPS, SHARD_IN), (1, 0, 2)).astype(BF16),
                 gw_o.reshape(N_CHIPS, SHARD_OUT, D_MODEL).astype(BF16))
        d_mod[l] = jnp.stack([jnp.concatenate([acc[2], acc[3], pvec[1]]), jnp.concatenate([acc[0], acc[1], pvec[0]])])
        g_small["g_pre"][l] = acc[4]
        g_small["g_post"][l] = pvec[2]
        g_small["conv_a"][l] = dca[:SHORT_CONV_K]
        g_small["conv_b"][l] = dcb[:CONFORMER_K]
        g_small["conv_b_bias"][l] = s256[0]
        g_small["conf_ln_g"][l] = s256[1]
        g_small["conf_ln_b"][l] = s256[2]
        g_small["sgu_ln_g"][l] = s256[3]
        g_small["sgu_ln_b"][l] = s256[4]
        g_small["w_s"][l] = dws.reshape(N_SPATIAL_GROUPS, CHUNK, CHUNK)
        g_small["b_s"][l] = jnp.transpose(dbs[:, :N_SPATIAL_GROUPS])
        g_small["q_gain"][l] = dgain[0, :HEAD_DIM] + dgain[0, HEAD_DIM:]
        g_small["k_gain"][l] = dgain[1, :HEAD_DIM] + dgain[1, HEAD_DIM:]
    grad_x = dxo[CTX_LEN:][None]

    d_mod_all = _all_gather_rows(jnp.stack(d_mod).reshape(DEPTH * 2, 3 * D_MODEL)).reshape(8, DEPTH, 2, 3 * D_MODEL)
    d_lat = jnp.transpose(d_mod_all[:, :, 0], (1, 0, 2))
    d_ctx = jnp.transpose(d_mod_all[:, :, 1], (1, 0, 2))
    cols = lambda a: lax.dynamic_slice_in_dim(a.reshape(DEPTH, 8, N_CHIPS, SHARD_MOD), chip, 1, axis=2)[:, :, 0]
    silu_t = jnp.transpose(silu_c)
    s_t = jnp.concatenate([silu_t[:, 0:8], jnp.tile(silu_t[:, 8:9], (1, 8)), jnp.zeros((D_MODEL, LANES - 16), F32)], axis=1)
    g_rows = jnp.concatenate([cols(d_lat), cols(d_ctx), jnp.zeros((DEPTH, LANES - 16, SHARD_MOD), F32)], axis=1)
    g_w_mod, g_b_mod, c_ctx_part = _mod_backward(s_t, g_rows, cols(d_ctx), jnp.concatenate([d_lat, d_ctx], axis=1),
                                                 w_mod, c_ctx[:, None])

    for n in _SMALL:
        if n not in ("c_ctx", "b_mod"):
            g_small[n] = jnp.stack(g_small[n])
    small_parts = [0.5 * c_ctx_part[:, 0]] + [g_small[n] for n in _SMALL[2:]]
    packed = _pack(small_parts)
    gathered = _all_gather_rows(packed).reshape(8, packed.shape[0], LANES)
    small_sum = _sum_slabs(gathered, packed.shape[0])
    small_g = dict(zip(("c_ctx",) + _SMALL[2:], _unpack(small_sum, [p.shape for p in small_parts])))
    small_g["b_mod"] = g_b_mod[:, 0]
    ch64 = GROUP_W // N_CHIPS
    for n in ("conv_a", "conv_b"):
        small_g[n] = lax.dynamic_slice_in_dim(small_g[n], chip * ch64, ch64, axis=2)
    sw = _pack([weights[n] for n in _SMALL])
    sm = _pack([m_in[n] for n in _SMALL])
    sv = _pack([v_in[n] for n in _SMALL])
    sg = _pack([small_g[n] for n in _SMALL])
    shapes = [weights[n].shape for n in _SMALL]
    small_out = [dict(zip(_SMALL, _unpack(o, shapes))) for o in _adamw([sg], sw, sm, sv, sg.shape[0])]

    landed[0] = _scatter_slabs(*slabs)
    sum_in = _sum_layer_slabs(jnp.stack([landed[l][0] for l in range(DEPTH)]), 512)
    sum_out = _sum_layer_slabs(jnp.stack([landed[l][1] for l in range(DEPTH)]), 256)
    sib_in, sib_out = _swap_with_sibling(sum_in, sum_out)

    big = {}
    flat = lambda a: a.reshape(-1, a.shape[-1])
    for n, grads, rows in (("w_in", [sum_in, sib_in], 512), ("w_out", [sum_out, sib_out], 256), ("w_mod", [flat(g_w_mod)], 512)):
        outs = _adamw(grads, flat(weights[n]), flat(m_in[n]), flat(v_in[n]), rows)
        big[n] = [o.reshape(weights[n].shape) for o in outs]

    def leaf(n, j):
        return big[n][j] if n in big else small_out[j][n]

    return (loss, grad_x, *[leaf(n, 0) for n in order], *[leaf(n, 1) for n in order],
            *[leaf(n, 2) for n in order], *[leaf(n, 3) for n in order])
```

```python
import functools

import numpy as np
import jax
import jax.numpy as jnp
from jax import lax
from jax.experimental import pallas as pl
from jax.experimental.pallas import tpu as pltpu

F32 = jnp.float32
BF16 = jnp.bfloat16
MESH = pl.DeviceIdType.MESH

D_MODEL = 1024
DEPTH = 4
GRID_W = 64
CTX_LEN = 256
GROUP_W = 256
HEAD_DIM = 64
N_Q_HEADS = 4
N_KV_HEADS = 2
GQA = N_Q_HEADS // N_KV_HEADS
ROPE_THETA = 10000.0
ATTN_SCALE = HEAD_DIM ** -0.5
SHORT_CONV_K = 3
CONFORMER_K = 31
CHUNK = 128
N_SPATIAL_GROUPS = 4
RMS_EPS = 1e-6
LN_EPS = 1e-5
ADAM_LR = 0.001
ADAM_B1 = 0.9
ADAM_B2 = 0.999
ADAM_EPS = 1e-08
ADAM_WD = 0.01
ADAM_STEP = 10

LANES = 128
HALO = 16
TM = 256
N_CTX_TILES = CTX_LEN // TM
SUB = 3
TB = SUB * TM
W_C = 1024
W_R = 1792
W_Q = 512
PROJ_W = W_C + W_R + W_Q
N_CHIPS = 4
SHARD_IN = PROJ_W // N_CHIPS
SHARD_OUT = D_MODEL // N_CHIPS
SHARD_MOD = 3 * D_MODEL // N_CHIPS
VMEM_LIMIT = 56 * 1024 * 1024


def _pc(body, **kw):
    return pl.pallas_call(body, **kw)


def _cparams(**kw):
    return pltpu.CompilerParams(dimension_semantics=("arbitrary",), vmem_limit_bytes=VMEM_LIMIT, **kw)


def _full(shape):
    n = len(shape)
    return pl.BlockSpec(shape, lambda i: (0,) * n)


def _const(shape):
    n = len(shape)
    return pl.BlockSpec(shape, lambda i: (0,) * n, pipeline_mode=pl.Buffered(1))


def _rows(width, tm=TM):
    return pl.BlockSpec((tm, width), lambda i: (i, 0))


def _heads(nh, width, tm=TM):
    return pl.BlockSpec((nh, tm, width), lambda i: (0, i, 0))


def _sigmoid(x):
    return jax.nn.sigmoid(x)


def _dot(a, b):
    return jnp.dot(a, b, preferred_element_type=F32)


def _dot_nt(a, b):
    return lax.dot_general(a, b, (((1,), (1,)), ((), ())), preferred_element_type=F32)


def _dot_tn(a, b):
    return lax.dot_general(a, b, (((0,), (0,)), ((), ())), preferred_element_type=F32)


def _lane(rows):
    return lax.broadcasted_iota(jnp.int32, (rows, LANES), 1)


def _rowsum(x):
    return jnp.sum(x, axis=1, keepdims=True)


def _colsum(x):
    return jnp.sum(x, axis=0, keepdims=True)


def _pair_sums(x, lo):
    s0 = _rowsum(jnp.where(lo, x, 0.0))
    s1 = _rowsum(jnp.where(lo, 0.0, x))
    return jnp.where(lo, s0, s1)


def _swap16(x, lo16):
    return jnp.where(lo16, pltpu.roll(x, LANES - 16, 1), pltpu.roll(x, 16, 1))


def _layer_norm_stats(x):
    mu = jnp.mean(x, axis=1, keepdims=True)
    xc = x - mu
    rs = lax.rsqrt(jnp.mean(xc * xc, axis=1, keepdims=True) + LN_EPS)
    return xc * rs, rs


def _layer_norm_bwd(dxn, xn, rs):
    return rs * (dxn - jnp.mean(dxn, axis=1, keepdims=True) - xn * jnp.mean(dxn * xn, axis=1, keepdims=True))


def _group_select(r, grp):
    out = jnp.where(grp == 0, r[0:CHUNK], 0.0)
    for g in range(1, N_SPATIAL_GROUPS):
        out = out + jnp.where(grp == g, r[g * CHUNK:(g + 1) * CHUNK], 0.0)
    return out


def _kv_chunk(t):
    return 1024 if (t - CTX_LEN) % 1024 == 0 else 256


def _all_gather_rows(x_shard):
    m_per, n = x_shard.shape

    def body(x_ref, out_ref, send_sems, recv_sems, local_sem):
        x, y, c = lax.axis_index("x"), lax.axis_index("y"), lax.axis_index("c")
        me, sibling = (x, y, c), (x, y, 1 - c)
        chips = [(1 - x, y), (x, 1 - y), (1 - x, 1 - y)]

        def rows(px, py, pc):
            return out_ref.at[pl.ds((4 * px + 2 * py + pc) * m_per, m_per), :]

        def copy(k, block, to, src=None):
            return pltpu.make_async_remote_copy(
                src_ref=rows(*block) if src is None else src, dst_ref=rows(*block),
                send_sem=send_sems.at[k], recv_sem=recv_sems.at[k], device_id=to, device_id_type=MESH)

        mine = pltpu.make_async_copy(x_ref, rows(*me), local_sem)
        mine.start()
        first = [copy(0, me, sibling, src=x_ref)]
        first += [copy(1 + j, me, (*chip, c), src=x_ref) for j, chip in enumerate(chips)]
        for cp in first:
            cp.start()
        passed = [copy(4 + j, (*chip, c), sibling) for j, chip in enumerate(chips)]
        for j, chip in enumerate(chips):
            copy(1 + j, (*chip, c), me).wait_recv()
            passed[j].start()
        copy(0, sibling, me).wait_recv()
        for j, chip in enumerate(chips):
            copy(4 + j, (*chip, 1 - c), me).wait_recv()
        for cp in first + passed:
            cp.wait_send()
        mine.wait()

    return _pc(
        body, name="all_gather_rows",
        out_shape=jax.ShapeDtypeStruct((8 * m_per, n), x_shard.dtype),
        in_specs=[pl.BlockSpec(memory_space=pltpu.VMEM)],
        out_specs=pl.BlockSpec(memory_space=pltpu.VMEM),
        scratch_shapes=[pltpu.SemaphoreType.DMA((7,)), pltpu.SemaphoreType.DMA((7,)), pltpu.SemaphoreType.DMA],
        compiler_params=pltpu.CompilerParams(vmem_limit_bytes=VMEM_LIMIT),
    )(x_shard)


def _place():
    x, y, c = lax.axis_index("x"), lax.axis_index("y"), lax.axis_index("c")
    return x, y, c, [(1 - x, y), (x, 1 - y), (1 - x, 1 - y)]


def _remote(src, dst, send_sems, recv_sems, k, to):
    return pltpu.make_async_remote_copy(src_ref=src, dst_ref=dst, send_sem=send_sems.at[k], recv_sem=recv_sems.at[k],
                                        device_id=to, device_id_type=MESH)


GATHER_SEMS = 12
SCATTER_SEMS = 6


def _gather_phase(phase, pairs, send_sems, recv_sems, local_sems):
    x, y, c, chips = _place()
    kme = 2 * x + y
    sibling = (x, y, 1 - c)
    for a, (src, dst) in enumerate(pairs):
        half = src.shape[0] // 2
        mine = pl.ds(c * half, half)
        theirs = pl.ds((1 - c) * half, half)
        if phase == 0:
            pltpu.make_async_copy(src, dst.at[kme], local_sems.at[a]).start()
        if phase == 2:
            pltpu.make_async_copy(src, dst.at[kme], local_sems.at[a]).wait()
        for j, (px, py) in enumerate(chips):
            kk = 2 * px + py
            landed = dst.at[kk, mine]
            out = lambda: _remote(src.at[mine], dst.at[kme, mine], send_sems, recv_sems, 6 * a + j, (px, py, c))
            hand = lambda: _remote(landed, landed, send_sems, recv_sems, 6 * a + 3 + j, sibling)
            if phase == 0:
                out().start()
            if phase == 1:
                _remote(landed, landed, send_sems, recv_sems, 6 * a + j, (px, py, c)).wait_recv()
                hand().start()
            if phase == 2:
                other = dst.at[kk, theirs]
                _remote(other, other, send_sems, recv_sems, 6 * a + 3 + j, sibling).wait_recv()
                out().wait_send()
                hand().wait_send()


def _scatter_phase(phase, pairs, send_sems, recv_sems, local_sems):
    x, y, c, chips = _place()
    kme = 2 * x + y
    for a, (src, dst) in enumerate(pairs):
        loc = pltpu.make_async_copy(src.at[kme], dst.at[kme], local_sems.at[a])
        if phase == 0:
            loc.start()
        else:
            loc.wait()
        for j, (px, py) in enumerate(chips):
            kk = 2 * px + py
            out = _remote(src.at[kk], dst.at[kme], send_sems, recv_sems, 3 * a + j, (px, py, c))
            if phase == 0:
                out.start()
            else:
                landed = dst.at[kk]
                _remote(landed, landed, send_sems, recv_sems, 3 * a + j, (px, py, c)).wait_recv()
                out.wait_send()


def _comm_scratch(n):
    return [pltpu.SemaphoreType.DMA((n,)), pltpu.SemaphoreType.DMA((n,)), pltpu.SemaphoreType.DMA((2,))]


def _slots(a):
    return jax.ShapeDtypeStruct((N_CHIPS,) + a.shape, a.dtype)


def _gather_weights(wi, wo):
    def body(wi_ref, wo_ref, gi_ref, go_ref, send_sems, recv_sems, local_sems):
        for phase in range(3):
            _gather_phase(phase, ((wi_ref, gi_ref), (wo_ref, go_ref)), send_sems, recv_sems, local_sems)

    hbm = pl.BlockSpec(memory_space=pl.ANY)
    return _pc(
        body, name="gather_weights", out_shape=(_slots(wi), _slots(wo)),
        in_specs=[hbm, hbm], out_specs=(hbm, hbm), scratch_shapes=_comm_scratch(GATHER_SEMS),
    )(wi, wo)


def _scatter_slabs(gi, go):
    def body(gi_ref, go_ref, ri_ref, ro_ref, send_sems, recv_sems, local_sems):
        for phase in range(2):
            _scatter_phase(phase, ((gi_ref, ri_ref), (go_ref, ro_ref)), send_sems, recv_sems, local_sems)

    hbm = pl.BlockSpec(memory_space=pl.ANY)
    return _pc(
        body, name="scatter_slabs",
        out_shape=(jax.ShapeDtypeStruct(gi.shape, gi.dtype), jax.ShapeDtypeStruct(go.shape, go.dtype)),
        in_specs=[hbm, hbm], out_specs=(hbm, hbm), scratch_shapes=_comm_scratch(SCATTER_SEMS),
    )(gi, go)


def _swap_with_sibling(a, b):
    def body(a_ref, b_ref, ra_ref, rb_ref, send_sems, recv_sems):
        x, y, c = lax.axis_index("x"), lax.axis_index("y"), lax.axis_index("c")
        copies = []
        for k, (src, dst) in enumerate(((a_ref, ra_ref), (b_ref, rb_ref))):
            cp = pltpu.make_async_remote_copy(
                src_ref=src, dst_ref=dst, send_sem=send_sems.at[k], recv_sem=recv_sems.at[k],
                device_id=(x, y, 1 - c), device_id_type=MESH)
            cp.start()
            copies.append(cp)
        for cp in copies:
            cp.wait()

    hbm = pl.BlockSpec(memory_space=pl.ANY)
    return _pc(
        body, name="swap_with_sibling",
        out_shape=(jax.ShapeDtypeStruct(a.shape, a.dtype), jax.ShapeDtypeStruct(b.shape, b.dtype)),
        in_specs=[hbm, hbm], out_specs=(hbm, hbm),
        scratch_shapes=[pltpu.SemaphoreType.DMA((2,)), pltpu.SemaphoreType.DMA((2,))],
    )(a, b)


def _mod_forward(c16, w_mod, b_mod_shard):
    def body(c_ref, w_ref, b_ref, s_ref, o_ref):
        cc = c_ref[...]
        s = cc * _sigmoid(cc)
        s_ref[...] = s
        o_ref[0] = jnp.dot(s, w_ref[0], preferred_element_type=F32, precision=lax.Precision.HIGHEST) + b_ref[0]

    return _pc(
        body, name="mod_forward", grid=(DEPTH,),
        out_shape=(jax.ShapeDtypeStruct((16, D_MODEL), F32), jax.ShapeDtypeStruct((DEPTH, 16, SHARD_MOD), F32)),
        in_specs=[_full((16, D_MODEL)),
                  pl.BlockSpec((1, D_MODEL, SHARD_MOD), lambda l: (l, 0, 0)),
                  pl.BlockSpec((1, 1, SHARD_MOD), lambda l: (l, 0, 0))],
        out_specs=(_full((16, D_MODEL)), pl.BlockSpec((1, 16, SHARD_MOD), lambda l: (l, 0, 0))),
        compiler_params=_cparams(),
    )(c16, w_mod, b_mod_shard)


def _mod_backward(s_t, g_rows, g_ctx, d_all, w_mod, c_ctx_col):
    def body(st_ref, g_ref, gc_ref, d_ref, w_ref, cc_ref, gw_ref, gb_ref, pc_ref):
        l = pl.program_id(0)
        gw_ref[0] = jnp.dot(st_ref[...], g_ref[0], preferred_element_type=F32, precision=lax.Precision.HIGHEST)
        gb_ref[0] = _colsum(d_ref[0])
        part = _rowsum(w_ref[0] * _colsum(gc_ref[0]))

        @pl.when(l == 0)
        def _():
            pc_ref[...] = jnp.zeros_like(pc_ref)

        pc_ref[...] += part

        @pl.when(l == DEPTH - 1)
        def _():
            cc = cc_ref[...]
            sg = _sigmoid(cc)
            pc_ref[...] = pc_ref[...] * (sg * (1.0 + cc * (1.0 - sg)))

    return _pc(
        body, name="mod_backward", grid=(DEPTH,),
        out_shape=(jax.ShapeDtypeStruct((DEPTH, D_MODEL, SHARD_MOD), F32),
                   jax.ShapeDtypeStruct((DEPTH, 1, 3 * D_MODEL), F32),
                   jax.ShapeDtypeStruct((D_MODEL, 1), F32)),
        in_specs=[_full((D_MODEL, LANES)),
                  pl.BlockSpec((1, LANES, SHARD_MOD), lambda l: (l, 0, 0)),
                  pl.BlockSpec((1, 8, SHARD_MOD), lambda l: (l, 0, 0)),
                  pl.BlockSpec((1, 16, 3 * D_MODEL), lambda l: (l, 0, 0)),
                  pl.BlockSpec((1, D_MODEL, SHARD_MOD), lambda l: (l, 0, 0)),
                  _full((D_MODEL, 1))],
        out_specs=(pl.BlockSpec((1, D_MODEL, SHARD_MOD), lambda l: (l, 0, 0)),
                   pl.BlockSpec((1, 1, 3 * D_MODEL), lambda l: (l, 0, 0)),
                   _full((D_MODEL, 1))),
        compiler_params=_cparams(),
    )(s_t, g_rows, g_ctx, d_all, w_mod, c_ctx_col)


def _head_norm(xb, lo):
    r = lax.rsqrt(_pair_sums(xb * xb, lo) * (1.0 / HEAD_DIM) + RMS_EPS)
    return xb * r, r


def _in_proj(xt, modv, g_pre, w_c, w_r, w_q, qk_gain, cos_t, sin_t):
    t = xt.shape[0]

    def body(x_ref, mod_ref, g_ref, wc_ref, wr_ref, wq_ref, gain_ref, cos_ref, sin_ref,
             h_ref, pc_ref, pr_ref, pq_ref, q_ref, k_ref, v_ref):
        lane = _lane(TM)
        lo = lane < HEAD_DIM
        lo16 = (lane & 31) < 16
        one = jnp.where(lane == HEAD_DIM, 1.0, 0.0)
        for jj in range(SUB):
            rows = pl.ds(jj * TM, TM)
            is_ctx = pl.program_id(0) * SUB + jj < N_CTX_TILES
            x = x_ref[rows, :]
            r = lax.rsqrt(jnp.mean(x * x, axis=1, keepdims=True) + RMS_EPS)
            sh = jnp.where(is_ctx, mod_ref[0:1, :], mod_ref[3:4, :])
            sc = jnp.where(is_ctx, mod_ref[1:2, :], mod_ref[4:5, :])
            h = (x * r * g_ref[...]) * (1.0 + sc) + sh
            hb = h.astype(BF16)
            h_ref[rows, :] = hb
            pc_ref[rows, :] = _dot(hb, wc_ref[...])
            pr_ref[rows, :] = _dot(hb, wr_ref[...])
            pq = _dot(hb, wq_ref[...])
            pq_ref[rows, :] = pq
            cos = cos_ref[rows, :]
            sin = sin_ref[rows, :]
            for b in range(3):
                xh, _ = _head_norm(pq[:, b * LANES:(b + 1) * LANES], lo)
                xg = xh * (gain_ref[0:1, :] if b < 2 else gain_ref[1:2, :])
                rot = xg * cos + _swap16(xg, lo16) * sin
                if b < 2:
                    rot = rot * ATTN_SCALE
                dst = q_ref if b < 2 else k_ref
                base = 2 * b if b < 2 else 0
                dst[base, rows, :] = jnp.where(lo, rot, 0.0).astype(BF16)
                dst[base + 1, rows, :] = jnp.where(lo, pltpu.roll(rot, HEAD_DIM, 1), 0.0).astype(BF16)
            vb = pq[:, 3 * LANES:4 * LANES]
            v_ref[0, rows, :] = jnp.where(lo, vb, one).astype(BF16)
            v_ref[1, rows, :] = jnp.where(lo, pltpu.roll(vb, HEAD_DIM, 1), one).astype(BF16)

    return _pc(
        body, name="in_proj", grid=(t // TB,),
        out_shape=(jax.ShapeDtypeStruct((t, D_MODEL), BF16),
                   jax.ShapeDtypeStruct((t, W_C), F32), jax.ShapeDtypeStruct((t, W_R), F32), jax.ShapeDtypeStruct((t, W_Q), F32),
                   jax.ShapeDtypeStruct((N_Q_HEADS, t, LANES), BF16),
                   jax.ShapeDtypeStruct((N_KV_HEADS, t, LANES), BF16),
                   jax.ShapeDtypeStruct((N_KV_HEADS, t, LANES), BF16)),
        in_specs=[_rows(D_MODEL, TB), _const((8, D_MODEL)), _const((1, D_MODEL)),
                  _const((D_MODEL, W_C)), _const((D_MODEL, W_R)), _const((D_MODEL, W_Q)),
                  _const((8, LANES)), _rows(LANES, TB), _rows(LANES, TB)],
        out_specs=(_rows(D_MODEL, TB), _rows(W_C, TB), _rows(W_R, TB), _rows(W_Q, TB),
                   _heads(N_Q_HEADS, LANES, TB), _heads(N_KV_HEADS, LANES, TB), _heads(N_KV_HEADS, LANES, TB)),
        compiler_params=_cparams(),
    )(xt, modv, g_pre, w_c, w_r, w_q, qk_gain, cos_t, sin_t)


def _attention_fwd(q, k, v, shards=None):
    t = q.shape[1]
    tk = _kv_chunk(t)
    n_chunks = (t - CTX_LEN) // tk
    n_tiles = t // TM
    n_sh = 0 if shards is None else len(shards)

    def body(q_ref, k_ref, v_ref, *rest):
        i = pl.program_id(0)
        o_ref = rest[n_sh]
        if shards is not None:
            pairs = tuple(zip(rest[:n_sh], rest[n_sh + 1:2 * n_sh + 1]))
            for phase, at in enumerate((0, n_tiles // 2, n_tiles - 1)):
                @pl.when(i == at)
                def _(phase=phase):
                    _gather_phase(phase, pairs, *rest[2 * n_sh + 1:])
        lane = _lane(GQA * TM)
        qs = [jnp.concatenate([q_ref[GQA * g + hh] for hh in range(GQA)], axis=0) for g in range(N_KV_HEADS)]

        def step(st, size, carry):
            out = []
            for g in range(N_KV_HEADS):
                m, acc = carry[g]
                s = _dot_nt(qs[g], k_ref[g, pl.ds(st, size), :])
                m_new = jnp.maximum(m, jnp.max(s, axis=1, keepdims=True))
                p = jnp.exp(s - m_new)
                out.append((m_new, acc * jnp.exp(m - m_new) + _dot(p.astype(BF16), v_ref[g, pl.ds(st, size), :])))
            return tuple(out)

        init = tuple((jnp.full((GQA * TM, 1), -jnp.inf, F32), jnp.zeros((GQA * TM, LANES), F32)) for _ in range(N_KV_HEADS))

        def finish(carry):
            for g in range(N_KV_HEADS):
                m, acc = carry[g]
                den = _rowsum(jnp.where(lane == HEAD_DIM, acc, 0.0))
                out = jnp.where(lane < HEAD_DIM, acc * (1.0 / den), jnp.where(lane == HEAD_DIM, m + jnp.log(den), 0.0))
                for hh in range(GQA):
                    o_ref[GQA * g + hh] = out[hh * TM:(hh + 1) * TM]

        @pl.when(i < N_CTX_TILES)
        def _():
            finish(step(0, CTX_LEN, init))

        @pl.when(i >= N_CTX_TILES)
        def _():
            def two_steps(j, cr):
                st = pl.multiple_of(CTX_LEN + j * (2 * tk), 256)
                return step(st + tk, tk, step(st, tk, cr))

            carry = lax.fori_loop(0, n_chunks // 2, two_steps, step(0, CTX_LEN, init))
            if n_chunks % 2:
                carry = step(CTX_LEN + (n_chunks - 1) * tk, tk, carry)
            finish(carry)

    hbm = pl.BlockSpec(memory_space=pl.ANY)
    extra = () if shards is None else tuple(shards)
    outs = _pc(
        body, name="attention_fwd" if shards is None else "attention_fwd_gather", grid=(n_tiles,),
        out_shape=(jax.ShapeDtypeStruct((N_Q_HEADS, t, LANES), F32),) + tuple(_slots(a) for a in extra),
        in_specs=[_heads(N_Q_HEADS, LANES), _full((N_KV_HEADS, t, LANES)), _full((N_KV_HEADS, t, LANES))] + [hbm] * n_sh,
        out_specs=(_heads(N_Q_HEADS, LANES),) + (hbm,) * n_sh,
        scratch_shapes=_comm_scratch(GATHER_SEMS) if shards is not None else [],
        compiler_params=_cparams(),
    )(q, k, v, *extra)
    return outs[0], tuple(outs[1:])


def _halo_specs(width, t):
    last = t // HALO - 1
    per = TM // HALO
    prev = pl.BlockSpec((HALO, width), lambda i: (jnp.maximum(i * per - 1, 0), 0))
    nxt = pl.BlockSpec((HALO, width), lambda i: (jnp.minimum((i + 1) * per, last), 0))
    return prev, nxt


def _halo_valid(i, n_tiles):
    prev_ok = jnp.logical_and(i != 0, i != N_CTX_TILES)
    next_ok = jnp.logical_and(i != N_CTX_TILES - 1, i != n_tiles - 1)
    return jnp.where(prev_ok, 1.0, 0.0), jnp.where(next_ok, 1.0, 0.0)


def _conv_inputs(pc):
    u = pc[:, 0:GROUP_W] * pc[:, GROUP_W:2 * GROUP_W]
    z = pc[:, 2 * GROUP_W:3 * GROUP_W] * _sigmoid(pc[:, 3 * GROUP_W:4 * GROUP_W])
    return u, z


def _fill_ext(ext_ref, prev, mid, nxt):
    ext_ref[0:HALO, :] = prev
    ext_ref[HALO:HALO + TM, :] = mid
    ext_ref[HALO + TM:HALO + TM + HALO, :] = nxt


def _row_local_mixers(pr, ca, z2, oe, vecs, wss_ref, bsm, lane256):
    a_b, a_g, b_g = pr[:, 0:256], pr[:, 256:512], pr[:, 512:768]
    c_u, c_v, c_g, d_g = pr[:, 768:1024], pr[:, 1024:1280], pr[:, 1280:1536], pr[:, 1536:1792]
    zn, rs_b = _layer_norm_stats(z2)
    tb = zn * vecs[1:2, :] + vecs[2:3, :]
    vn_hat, rs_c = _layer_norm_stats(c_v)
    vn = vn_hat * vecs[3:4, :] + vecs[4:5, :]
    grp = jnp.right_shift(lane256, 6)
    sgs = []
    for ch in range(TM // CHUNK):
        r = _dot(wss_ref[...], vn[ch * CHUNK:(ch + 1) * CHUNK, :].astype(BF16))
        sgs.append(_group_select(r, grp[0:CHUNK]) + bsm)
    sg = jnp.concatenate(sgs, axis=0)
    lane = _lane(TM)
    lo = lane < HEAD_DIM
    att = jnp.concatenate([jnp.where(lo, oe[2 * b], pltpu.roll(oe[2 * b + 1], HEAD_DIM, 1)) for b in range(2)], axis=1)
    return dict(a_b=a_b, a_g=a_g, b_g=b_g, c_u=c_u, c_v=c_v, c_g=c_g, d_g=d_g, zn=zn, rs_b=rs_b, tb=tb,
                vn_hat=vn_hat, rs_c=rs_c, vn=vn, sg=sg, att=att, grp=grp, lo=lo, lane=lane)


def _mixer_concat(f, ca):
    ya = f["a_b"] * ca
    yb = f["tb"] * _sigmoid(f["tb"])
    yc = f["c_u"] * f["sg"]
    gates = [f[n] * _sigmoid(f[n]) for n in ("a_g", "b_g", "c_g", "d_g")]
    ys = (ya, yb, yc, f["att"])
    big = jnp.concatenate([yy * gg for yy, gg in zip(ys, gates)], axis=1).astype(BF16)
    return big, ys, gates


def _mix_out(pc, pr, oe, xt, modv, g_post, w_out, conv_a, conv_b, vecs, wss, bsm):
    t = xt.shape[0]
    n_tiles = t // TM
    prev_spec, next_spec = _halo_specs(W_C, t)

    def body(pc_ref, pp_ref, pn_ref, pr_ref, oe_ref, x_ref, mod_ref, gp_ref, wo_ref, cva_ref, cvb_ref, vec_ref, wss_ref, bsm_ref,
             xo_ref, y_ref, ca_ref, z2_ref, uext, zext):
        i = pl.program_id(0)
        is_ctx = i < N_CTX_TILES
        pv, nv = _halo_valid(i, n_tiles)
        u, z = _conv_inputs(pc_ref[...])
        up, zp = _conv_inputs(pp_ref[...])
        un, zn_ = _conv_inputs(pn_ref[...])
        _fill_ext(uext, up * pv, u, un * nv)
        _fill_ext(zext, zp * pv, z, zn_ * nv)
        ca = cva_ref[0:1, :] * uext[pl.ds(HALO - 1, TM), :]
        for kk in range(1, SHORT_CONV_K):
            ca = ca + cva_ref[kk:kk + 1, :] * uext[pl.ds(HALO - 1 + kk, TM), :]
        z2 = cvb_ref[0:1, :] * zext[pl.ds(HALO - CONFORMER_K // 2, TM), :]
        for kk in range(1, CONFORMER_K):
            z2 = z2 + cvb_ref[kk:kk + 1, :] * zext[pl.ds(HALO - CONFORMER_K // 2 + kk, TM), :]
        vecs = vec_ref[...]
        z2 = z2 + vecs[0:1, :]
        ca_ref[...] = ca
        z2_ref[...] = z2
        lane256 = lax.broadcasted_iota(jnp.int32, (TM, GROUP_W), 1)
        f = _row_local_mixers(pr_ref[...], ca, z2, oe_ref, vecs, wss_ref, bsm_ref[...], lane256)
        big, _, _ = _mixer_concat(f, ca)
        y = _dot(big, wo_ref[...])
        y_ref[...] = y
        ry = lax.rsqrt(jnp.mean(y * y, axis=1, keepdims=True) + RMS_EPS)
        gt = jnp.where(is_ctx, mod_ref[2:3, :], mod_ref[5:6, :])
        xo_ref[...] = x_ref[...] + gt * (y * ry * gp_ref[...])

    return _pc(
        body, name="mix_out", grid=(n_tiles,),
        out_shape=(jax.ShapeDtypeStruct((t, D_MODEL), F32), jax.ShapeDtypeStruct((t, D_MODEL), F32),
                   jax.ShapeDtypeStruct((t, GROUP_W), F32), jax.ShapeDtypeStruct((t, GROUP_W), F32)),
        in_specs=[_rows(W_C), prev_spec, next_spec, _rows(W_R), _heads(N_Q_HEADS, LANES), _rows(D_MODEL),
                  _full((8, D_MODEL)), _full((1, D_MODEL)), _full((D_MODEL, D_MODEL)),
                  _full((8, GROUP_W)), _full((32, GROUP_W)), _full((8, GROUP_W)),
                  _full((N_SPATIAL_GROUPS * CHUNK, CHUNK)), _full((CHUNK, GROUP_W))],
        out_specs=(_rows(D_MODEL), _rows(D_MODEL), _rows(GROUP_W), _rows(GROUP_W)),
        scratch_shapes=[pltpu.VMEM((TM + 2 * HALO, GROUP_W), F32), pltpu.VMEM((TM + 2 * HALO, GROUP_W), F32)],
        compiler_params=_cparams(),
    )(pc, pc, pc, pr, oe, xt, modv, g_post, w_out, conv_a, conv_b, vecs, wss, bsm)


def _loss_head(xt, target):
    t = xt.shape[0]

    def body(x_ref, t_ref, dx_ref, loss_ref):
        i = pl.program_id(0)

        @pl.when(i == 0)
        def _():
            loss_ref[...] = jnp.zeros_like(loss_ref)

        lat = jnp.where(i < N_CTX_TILES, 0.0, 1.0)
        err = (x_ref[...] - t_ref[...]) * lat
        dx_ref[...] = err * (1.0 / D_MODEL)
        loss_ref[...] += jnp.sum(err * err) * (0.5 / D_MODEL)

    return _pc(
        body, name="loss_head", grid=(t // TM,),
        out_shape=(jax.ShapeDtypeStruct((t, D_MODEL), F32), jax.ShapeDtypeStruct((8, LANES), F32)),
        in_specs=[_rows(D_MODEL), pl.BlockSpec((TM, D_MODEL), lambda i: (jnp.maximum(i - N_CTX_TILES, 0), 0))],
        out_specs=(_rows(D_MODEL), _full((8, LANES))),
        compiler_params=_cparams(),
    )(xt, target)


def _mix_out_bwd(dxo, y, pr, ca, z2, oe, modv, g_post, w_out, vecs, wss, wsts, bsm):
    t = y.shape[0]
    n_tiles = t // TM

    def body(dxo_ref, y_ref, pr_ref, ca_ref, z2_ref, oe_ref, mod_ref, gp_ref, wo_ref, vec_ref, wss_ref, wsts_ref, bsm_ref,
             dpr_ref, ga_ref, gb_ref, doe_ref, dwo_ref, pvec_ref, s256_ref, dws_ref, dbs_ref, dbsm):
        i = pl.program_id(0)
        is_ctx = i < N_CTX_TILES

        @pl.when(i == 0)
        def _():
            dwo_ref[...] = jnp.zeros_like(dwo_ref)
            pvec_ref[...] = jnp.zeros_like(pvec_ref)
            s256_ref[...] = jnp.zeros_like(s256_ref)
            dws_ref[...] = jnp.zeros_like(dws_ref)
            dbsm[...] = jnp.zeros_like(dbsm)

        dxo_ = dxo_ref[...]
        y_ = y_ref[...]
        ry = lax.rsqrt(jnp.mean(y_ * y_, axis=1, keepdims=True) + RMS_EPS)
        nh = y_ * ry
        gp = gp_ref[...]
        gt = jnp.where(is_ctx, mod_ref[2:3, :], mod_ref[5:6, :])
        dgt = _colsum(dxo_ * (nh * gp))
        pvec_ref[0:1, :] += jnp.where(is_ctx, dgt, 0.0)
        pvec_ref[1:2, :] += jnp.where(is_ctx, 0.0, dgt)
        dn = dxo_ * gt
        pvec_ref[2:3, :] += _colsum(dn * nh)
        dnh = dn * gp
        dy = ry * (dnh - nh * jnp.mean(dnh * nh, axis=1, keepdims=True))

        vecs = vec_ref[...]
        bsm_ = bsm_ref[...]
        ca_ = ca_ref[...]
        lane256 = lax.broadcasted_iota(jnp.int32, (TM, GROUP_W), 1)
        f = _row_local_mixers(pr_ref[...], ca_, z2_ref[...], oe_ref, vecs, wss_ref, bsm_, lane256)
        big, ys, gates = _mixer_concat(f, ca_)
        dyb = dy.astype(BF16)
        dwo_ref[...] += _dot_tn(big, dyb)
        dbig = _dot_nt(dyb, wo_ref[...])

        d_y, d_gate = [], []
        for n, (name, yy, gg) in enumerate(zip(("a_g", "b_g", "c_g", "d_g"), ys, gates)):
            dpart = dbig[:, n * GROUP_W:(n + 1) * GROUP_W]
            gx = f[name]
            sg_ = _sigmoid(gx)
            d_y.append(dpart * gg)
            d_gate.append(dpart * yy * (sg_ * (1.0 + gx * (1.0 - sg_))))
        dya, dyb_, dyc, datt = d_y

        d_ab = dya * ca_
        ga_ref[...] = dya * f["a_b"]
        tb = f["tb"]
        sb = _sigmoid(tb)
        dtb = dyb_ * (sb * (1.0 + tb * (1.0 - sb)))
        s256_ref[1:2, :] += _colsum(dtb * f["zn"])
        s256_ref[2:3, :] += _colsum(dtb)
        dz2 = _layer_norm_bwd(dtb * vecs[1:2, :], f["zn"], f["rs_b"])
        gb_ref[...] = dz2
        s256_ref[0:1, :] += _colsum(dz2)
        d_cu = dyc * f["sg"]
        dsg = dyc * f["c_u"]
        grp = f["grp"]
        dvn_parts = []
        for ch in range(TM // CHUNK):
            rows = slice(ch * CHUNK, (ch + 1) * CHUNK)
            dsg_c = dsg[rows, :]
            dbsm[...] += dsg_c
            vn_c = f["vn"][rows, :].astype(BF16)
            for g in range(N_SPATIAL_GROUPS):
                masked = jnp.where(grp[0:CHUNK] == g, dsg_c, 0.0).astype(BF16)
                dws_ref[g * CHUNK:(g + 1) * CHUNK, :] += _dot_nt(masked, vn_c)
            dvn_parts.append(_group_select(_dot(wsts_ref[...], dsg_c.astype(BF16)), grp[0:CHUNK]))
        dvn = jnp.concatenate(dvn_parts, axis=0)
        s256_ref[3:4, :] += _colsum(dvn * f["vn_hat"])
        s256_ref[4:5, :] += _colsum(dvn)
        d_cv = _layer_norm_bwd(dvn * vecs[3:4, :], f["vn_hat"], f["rs_c"])
        lane, lo = f["lane"], f["lo"]
        att = f["att"]
        for b in range(2):
            da = datt[:, b * LANES:(b + 1) * LANES]
            prod = da * att[:, b * LANES:(b + 1) * LANES]
            for hh in range(2):
                h = 2 * b + hh
                lse = _rowsum(jnp.where(lane == HEAD_DIM, oe_ref[h], 0.0))
                delta = _rowsum(jnp.where(lo, prod, 0.0) if hh == 0 else jnp.where(lo, 0.0, prod))
                dah = da if hh == 0 else pltpu.roll(da, HEAD_DIM, 1)
                doe_ref[h] = jnp.where(lo, dah, jnp.where(lane == HEAD_DIM, delta, jnp.where(lane == HEAD_DIM + 1, lse, 0.0)))

        dpr_ref[...] = jnp.concatenate([d_ab, d_gate[0], d_gate[1], d_cu, d_cv, d_gate[2], d_gate[3]], axis=1).astype(BF16)

        @pl.when(i == n_tiles - 1)
        def _():
            acc = dbsm[...]
            lane128 = _lane(CHUNK)
            out = jnp.zeros((CHUNK, LANES), F32)
            for g in range(N_SPATIAL_GROUPS):
                col = _rowsum(jnp.where(grp[0:CHUNK] == g, acc, 0.0))
                out = out + jnp.where(lane128 == g, col, 0.0)
            dbs_ref[...] = out

    return _pc(
        body, name="mix_out_bwd", grid=(n_tiles,),
        out_shape=(jax.ShapeDtypeStruct((t, W_R), BF16),
                   jax.ShapeDtypeStruct((t, GROUP_W), F32), jax.ShapeDtypeStruct((t, GROUP_W), F32),
                   jax.ShapeDtypeStruct((N_Q_HEADS, t, LANES), F32),
                   jax.ShapeDtypeStruct((D_MODEL, D_MODEL), F32),
                   jax.ShapeDtypeStruct((8, D_MODEL), F32),
                   jax.ShapeDtypeStruct((8, GROUP_W), F32),
                   jax.ShapeDtypeStruct((N_SPATIAL_GROUPS * CHUNK, CHUNK), F32),
                   jax.ShapeDtypeStruct((CHUNK, LANES), F32)),
        in_specs=[_rows(D_MODEL), _rows(D_MODEL), _rows(W_R), _rows(GROUP_W), _rows(GROUP_W), _heads(N_Q_HEADS, LANES),
                  _full((8, D_MODEL)), _full((1, D_MODEL)), _full((D_MODEL, D_MODEL)), _full((8, GROUP_W)),
                  _full((N_SPATIAL_GROUPS * CHUNK, CHUNK)), _full((N_SPATIAL_GROUPS * CHUNK, CHUNK)), _full((CHUNK, GROUP_W))],
        out_specs=(_rows(W_R), _rows(GROUP_W), _rows(GROUP_W), _heads(N_Q_HEADS, LANES),
                   _full((D_MODEL, D_MODEL)), _full((8, D_MODEL)), _full((8, GROUP_W)),
                   _full((N_SPATIAL_GROUPS * CHUNK, CHUNK)), _full((CHUNK, LANES))),
        scratch_shapes=[pltpu.VMEM((CHUNK, GROUP_W), F32)],
        compiler_params=_cparams(),
    )(dxo, y, pr, ca, z2, oe, modv, g_post, w_out, vecs, wss, wsts, bsm)


def _conv_bwd(pc, g_a, g_b, conv_a, conv_b):
    t = pc.shape[0]
    n_tiles = t // TM
    pc_prev, pc_next = _halo_specs(W_C, t)
    g_prev, g_next = _halo_specs(GROUP_W, t)

    def body(pc_ref, pp_ref, pn_ref, ga_ref, gap_ref, gan_ref, gb_ref, gbp_ref, gbn_ref, cva_ref, cvb_ref,
             dpc_ref, dca_ref, dcb_ref, uext, zext, gaext, gbext):
        i = pl.program_id(0)

        @pl.when(i == 0)
        def _():
            dca_ref[...] = jnp.zeros_like(dca_ref)
            dcb_ref[...] = jnp.zeros_like(dcb_ref)

        pv, nv = _halo_valid(i, n_tiles)
        pc_ = pc_ref[...]
        u, z = _conv_inputs(pc_)
        up, zp = _conv_inputs(pp_ref[...])
        un, zn_ = _conv_inputs(pn_ref[...])
        _fill_ext(uext, up * pv, u, un * nv)
        _fill_ext(zext, zp * pv, z, zn_ * nv)
        ga = ga_ref[...]
        gb = gb_ref[...]
        _fill_ext(gaext, gap_ref[...] * pv, ga, gan_ref[...] * nv)
        _fill_ext(gbext, gbp_ref[...] * pv, gb, gbn_ref[...] * nv)

        du = cva_ref[0:1, :] * gaext[pl.ds(HALO + 1, TM), :]
        dca_ref[0:1, :] += _colsum(ga * uext[pl.ds(HALO - 1, TM), :])
        for kk in range(1, SHORT_CONV_K):
            du = du + cva_ref[kk:kk + 1, :] * gaext[pl.ds(HALO + 1 - kk, TM), :]
            dca_ref[kk:kk + 1, :] += _colsum(ga * uext[pl.ds(HALO - 1 + kk, TM), :])
        half = CONFORMER_K // 2
        dz = cvb_ref[0:1, :] * gbext[pl.ds(HALO + half, TM), :]
        dcb_ref[0:1, :] += _colsum(gb * zext[pl.ds(HALO - half, TM), :])
        for kk in range(1, CONFORMER_K):
            dz = dz + cvb_ref[kk:kk + 1, :] * gbext[pl.ds(HALO + half - kk, TM), :]
            dcb_ref[kk:kk + 1, :] += _colsum(gb * zext[pl.ds(HALO - half + kk, TM), :])

        a_c, a_h = pc_[:, 0:GROUP_W], pc_[:, GROUP_W:2 * GROUP_W]
        glu_a, glu_g = pc_[:, 2 * GROUP_W:3 * GROUP_W], pc_[:, 3 * GROUP_W:4 * GROUP_W]
        sg = _sigmoid(glu_g)
        dpc_ref[...] = jnp.concatenate([du * a_h, du * a_c, dz * sg, dz * glu_a * sg * (1.0 - sg)], axis=1).astype(BF16)

    ext = pltpu.VMEM((TM + 2 * HALO, GROUP_W), F32)
    return _pc(
        body, name="conv_bwd", grid=(n_tiles,),
        out_shape=(jax.ShapeDtypeStruct((t, W_C), BF16), jax.ShapeDtypeStruct((8, GROUP_W), F32), jax.ShapeDtypeStruct((32, GROUP_W), F32)),
        in_specs=[_rows(W_C), pc_prev, pc_next, _rows(GROUP_W), g_prev, g_next, _rows(GROUP_W), g_prev, g_next,
                  _full((8, GROUP_W)), _full((32, GROUP_W))],
        out_specs=(_rows(W_C), _full((8, GROUP_W)), _full((32, GROUP_W))),
        scratch_shapes=[ext, ext, ext, ext],
        compiler_params=_cparams(),
    )(pc, pc, pc, g_a, g_a, g_a, g_b, g_b, g_b, conv_a, conv_b)


def _attention_bwd(q, k, v, doe, slabs=None):
    t = q.shape[1]
    tk = _kv_chunk(t)
    n_chunks = (t - CTX_LEN) // tk
    n_tiles = t // TM
    n_sl = 0 if slabs is None else len(slabs)

    def body(q_ref, do_ref, k_ref, v_ref, *rest):
        i = pl.program_id(0)
        dq_ref, dk_hbm, dv_hbm = rest[n_sl:n_sl + 3]
        dk_acc, dv_acc = rest[2 * n_sl + 3:2 * n_sl + 5]
        pairs = tuple(zip(rest[:n_sl], rest[n_sl + 3:2 * n_sl + 3]))
        sems = rest[2 * n_sl + 5:]

        @pl.when(i == 0)
        def _():
            dk_acc[...] = jnp.zeros_like(dk_acc)
            dv_acc[...] = jnp.zeros_like(dv_acc)
            if slabs is not None:
                _scatter_phase(0, pairs, *sems)

        lane = _lane(GQA * TM)
        lo = lane < HEAD_DIM
        qs, dos, deltas, lses = [], [], [], []
        for g in range(N_KV_HEADS):
            qs.append(jnp.concatenate([q_ref[GQA * g + hh] for hh in range(GQA)], axis=0))
            dog = jnp.concatenate([do_ref[GQA * g + hh] for hh in range(GQA)], axis=0)
            deltas.append(_rowsum(jnp.where(lane == HEAD_DIM, dog, 0.0)))
            lses.append(_rowsum(jnp.where(lane == HEAD_DIM + 1, dog, 0.0)))
            dos.append(jnp.where(lo, dog, 0.0).astype(BF16))

        def step(st, size, dqs):
            out = []
            for g in range(N_KV_HEADS):
                kc = k_ref[g, pl.ds(st, size), :]
                vc = v_ref[g, pl.ds(st, size), :]
                p = jnp.exp(_dot_nt(qs[g], kc) - lses[g])
                ds_ = (p * (_dot_nt(dos[g], vc) - deltas[g])).astype(BF16)
                dk_acc[g, pl.ds(st, size), :] += _dot_tn(ds_, qs[g])
                dv_acc[g, pl.ds(st, size), :] += _dot_tn(p.astype(BF16), dos[g])
                out.append(dqs[g] + _dot(ds_, kc))
            return tuple(out)

        zero = tuple(jnp.zeros((GQA * TM, LANES), F32) for _ in range(N_KV_HEADS))

        def finish(dqs):
            for g in range(N_KV_HEADS):
                for hh in range(GQA):
                    dq_ref[GQA * g + hh] = dqs[g][hh * TM:(hh + 1) * TM]

        @pl.when(i < N_CTX_TILES)
        def _():
            finish(step(0, CTX_LEN, zero))

        @pl.when(i >= N_CTX_TILES)
        def _():
            finish(lax.fori_loop(0, n_chunks, lambda j, acc: step(pl.multiple_of(CTX_LEN + j * tk, 256), tk, acc),
                                 step(0, CTX_LEN, zero)))

        @pl.when(i == n_tiles - 1)
        def _():
            pltpu.sync_copy(dk_acc, dk_hbm)
            pltpu.sync_copy(dv_acc, dv_hbm)
            if slabs is not None:
                _scatter_phase(1, pairs, *sems)

    kv_shape = jax.ShapeDtypeStruct((N_KV_HEADS, t, LANES), F32)
    hbm = pl.BlockSpec(memory_space=pl.ANY)
    extra = () if slabs is None else tuple(slabs)
    outs = _pc(
        body, name="attention_bwd" if slabs is None else "attention_bwd_scatter", grid=(n_tiles,),
        out_shape=(jax.ShapeDtypeStruct((N_Q_HEADS, t, LANES), F32), kv_shape, kv_shape)
        + tuple(jax.ShapeDtypeStruct(a.shape, a.dtype) for a in extra),
        in_specs=[_heads(N_Q_HEADS, LANES), _heads(N_Q_HEADS, LANES),
                  _full((N_KV_HEADS, t, LANES)), _full((N_KV_HEADS, t, LANES))] + [hbm] * n_sl,
        out_specs=(_heads(N_Q_HEADS, LANES), hbm, hbm) + (hbm,) * n_sl,
        scratch_shapes=[pltpu.VMEM((N_KV_HEADS, t, LANES), F32), pltpu.VMEM((N_KV_HEADS, t, LANES), F32)]
        + (_comm_scratch(SCATTER_SEMS) if slabs is not None else []),
        compiler_params=_cparams(),
    )(q, doe, k, v, *extra)
    return outs[0], outs[1], outs[2], tuple(outs[3:])


def _in_proj_bwd(dpc, dpr, dq, dk, dv, pq, qk_gain, cos_t, sin_t, w_c, w_r, w_q, xt, dxo, modv, g_pre):
    t = xt.shape[0]

    def body(dpc_ref, dpr_ref, dq_ref, dk_ref, dv_ref, pq_ref, gain_ref, cos_ref, sin_ref, wc_ref, wr_ref, wq_ref,
             x_ref, dxo_ref, mod_ref, g_ref, dx_ref, dpq_ref, acc_ref, dgain_ref):
        i = pl.program_id(0)

        @pl.when(i == 0)
        def _():
            acc_ref[...] = jnp.zeros_like(acc_ref)
            dgain_ref[...] = jnp.zeros_like(dgain_ref)

        lane = _lane(TM)
        lo = lane < HEAD_DIM
        lo16 = (lane & 31) < 16
        g = g_ref[...]
        for jj in range(SUB):
            rows = pl.ds(jj * TM, TM)
            is_ctx = i * SUB + jj < N_CTX_TILES
            cos = cos_ref[rows, :]
            sin = sin_ref[rows, :]
            outs = []
            for b in range(3):
                src = dq_ref if b < 2 else dk_ref
                base = 2 * b if b < 2 else 0
                drot = src[base, rows, :] + pltpu.roll(src[base + 1, rows, :], HEAD_DIM, 1)
                if b < 2:
                    drot = drot * ATTN_SCALE
                dxg = drot * cos + _swap16(drot * sin, lo16)
                xh, r = _head_norm(pq_ref[rows, b * LANES:(b + 1) * LANES], lo)
                row = 0 if b < 2 else 1
                dgain_ref[row:row + 1, :] += _colsum(dxg * xh)
                dxh = dxg * gain_ref[row:row + 1, :]
                outs.append(r * (dxh - xh * (_pair_sums(dxh * xh, lo) * (1.0 / HEAD_DIM))))
            outs.append(dv_ref[0, rows, :] + pltpu.roll(dv_ref[1, rows, :], HEAD_DIM, 1))
            dpq = jnp.concatenate(outs, axis=1).astype(BF16)
            dpq_ref[rows, :] = dpq

            dh = _dot_nt(dpc_ref[rows, :], wc_ref[...]) + _dot_nt(dpr_ref[rows, :], wr_ref[...]) + _dot_nt(dpq, wq_ref[...])
            x = x_ref[rows, :]
            r = lax.rsqrt(jnp.mean(x * x, axis=1, keepdims=True) + RMS_EPS)
            xn = x * r
            sc = jnp.where(is_ctx, mod_ref[1:2, :], mod_ref[4:5, :])
            dsh = _colsum(dh)
            dsc = _colsum(dh * (xn * g))
            acc_ref[0:1, :] += jnp.where(is_ctx, dsh, 0.0)
            acc_ref[1:2, :] += jnp.where(is_ctx, dsc, 0.0)
            acc_ref[2:3, :] += jnp.where(is_ctx, 0.0, dsh)
            acc_ref[3:4, :] += jnp.where(is_ctx, 0.0, dsc)
            dxg = dh * (1.0 + sc)
            acc_ref[4:5, :] += _colsum(dxg * xn)
            dxn = dxg * g
            dx_ref[rows, :] = r * (dxn - xn * jnp.mean(dxn * xn, axis=1, keepdims=True)) + dxo_ref[rows, :]

    return _pc(
        body, name="in_proj_bwd", grid=(t // TB,),
        out_shape=(jax.ShapeDtypeStruct((t, D_MODEL), F32), jax.ShapeDtypeStruct((t, W_Q), BF16),
                   jax.ShapeDtypeStruct((8, D_MODEL), F32), jax.ShapeDtypeStruct((8, LANES), F32)),
        in_specs=[_rows(W_C, TB), _rows(W_R, TB),
                  _heads(N_Q_HEADS, LANES, TB), _heads(N_KV_HEADS, LANES, TB), _heads(N_KV_HEADS, LANES, TB), _rows(W_Q, TB),
                  _const((8, LANES)), _rows(LANES, TB), _rows(LANES, TB),
                  _const((D_MODEL, W_C)), _const((D_MODEL, W_R)), _const((D_MODEL, W_Q)),
                  _rows(D_MODEL, TB), _rows(D_MODEL, TB), _const((8, D_MODEL)), _const((1, D_MODEL))],
        out_specs=(_rows(D_MODEL, TB), _rows(W_Q, TB), _full((8, D_MODEL)), _full((8, LANES))),
        compiler_params=_cparams(),
    )(dpc, dpr, dq, dk, dv, pq, qk_gain, cos_t, sin_t, w_c, w_r, w_q, xt, dxo, modv, g_pre)


def _in_proj_wgrad(h, dpc, dpr, dpq):
    t = h.shape[0]

    def body(h_ref, dpc_ref, dpr_ref, dpq_ref, gc_ref, gr_ref, gq_ref):
        @pl.when(pl.program_id(0) == 0)
        def _():
            gc_ref[...] = jnp.zeros_like(gc_ref)
            gr_ref[...] = jnp.zeros_like(gr_ref)
            gq_ref[...] = jnp.zeros_like(gq_ref)

        hb = h_ref[...]
        gc_ref[...] += _dot_tn(hb, dpc_ref[...])
        gr_ref[...] += _dot_tn(hb, dpr_ref[...])
        gq_ref[...] += _dot_tn(hb, dpq_ref[...])

    return _pc(
        body, name="in_proj_wgrad", grid=(t // TB,),
        out_shape=(jax.ShapeDtypeStruct((D_MODEL, W_C), F32), jax.ShapeDtypeStruct((D_MODEL, W_R), F32),
                   jax.ShapeDtypeStruct((D_MODEL, W_Q), F32)),
        in_specs=[_rows(D_MODEL, TB), _rows(W_C, TB), _rows(W_R, TB), _rows(W_Q, TB)],
        out_specs=(_full((D_MODEL, W_C)), _full((D_MODEL, W_R)), _full((D_MODEL, W_Q))),
        compiler_params=_cparams(),
    )(h, dpc, dpr, dpq)


def _sum_slabs(slabs, tile_rows):
    n, r, c = slabs.shape

    def body(s_ref, o_ref):
        acc = s_ref[0].astype(F32)
        for k in range(1, n):
            acc = acc + s_ref[k].astype(F32)
        o_ref[...] = acc

    return _pc(
        body, name="sum_slabs", grid=(r // tile_rows,),
        out_shape=jax.ShapeDtypeStruct((r, c), F32),
        in_specs=[pl.BlockSpec((n, tile_rows, c), lambda i: (0, i, 0))],
        out_specs=pl.BlockSpec((tile_rows, c), lambda i: (i, 0)),
        compiler_params=_cparams(),
    )(slabs)


def _sum_layer_slabs(slabs, tile_rows):
    nl, n, r, c = slabs.shape
    per = r // tile_rows

    def body(s_ref, o_ref):
        acc = s_ref[0, 0].astype(F32)
        for k in range(1, n):
            acc = acc + s_ref[0, k].astype(F32)
        o_ref[...] = acc

    return _pc(
        body, name="sum_layer_slabs", grid=(nl * per,),
        out_shape=jax.ShapeDtypeStruct((nl * r, c), F32),
        in_specs=[pl.BlockSpec((1, n, tile_rows, c), lambda i: (i // per, 0, i % per, 0))],
        out_specs=pl.BlockSpec((tile_rows, c), lambda i: (i, 0)),
        compiler_params=_cparams(),
    )(slabs)


def _adamw(grads, w, m, v, tile_rows):
    r, c = w.shape
    n_g = len(grads)

    def body(*refs):
        g = refs[0][...]
        for k in range(1, n_g):
            g = g + refs[k][...]
        w_ref, m_ref, v_ref, g_out, d_out, m_out, v_out = refs[n_g:]
        m_new = ADAM_B1 * m_ref[...] + (1.0 - ADAM_B1) * g
        v_new = ADAM_B2 * v_ref[...] + (1.0 - ADAM_B2) * (g * g)
        m_hat = m_new / (1.0 - ADAM_B1 ** ADAM_STEP)
        v_hat = v_new / (1.0 - ADAM_B2 ** ADAM_STEP)
        g_out[...] = g
        d_out[...] = -ADAM_LR * (m_hat / (jnp.sqrt(v_hat) + ADAM_EPS) + ADAM_WD * w_ref[...])
        m_out[...] = m_new
        v_out[...] = v_new

    spec = pl.BlockSpec((tile_rows, c), lambda i: (i, 0))
    shape = jax.ShapeDtypeStruct((r, c), F32)
    return _pc(
        body, name="adamw", grid=(r // tile_rows,),
        out_shape=(shape,) * 4, in_specs=[spec] * (n_g + 3), out_specs=(spec,) * 4,
        compiler_params=_cparams(),
    )(*grads, w, m, v)


def _rope_tables(s_lat):
    pos = jnp.arange(s_lat)
    pos_row = (pos // GRID_W).astype(F32)
    pos_col = (pos % GRID_W).astype(F32)
    axis_dim = HEAD_DIM // 2
    inv_freq = 1.0 / (ROPE_THETA ** (jnp.arange(0, axis_dim, 2, dtype=F32) / axis_dim))
    d = np.arange(LANES) % HEAD_DIM
    on_rows = (d // axis_dim) == 0
    freq = d % (axis_dim // 2)
    sign = np.where((d % axis_dim) < axis_dim // 2, -1.0, 1.0).astype(np.float32)
    ang = jnp.where(on_rows[None, :], pos_row[:, None], pos_col[:, None]) * inv_freq[freq][None, :]
    cos = jnp.concatenate([jnp.ones((CTX_LEN, LANES), F32), jnp.cos(ang)], axis=0)
    sin = jnp.concatenate([jnp.zeros((CTX_LEN, LANES), F32), jnp.sin(ang) * sign[None, :]], axis=0)
    return cos, sin


def _pad_rows(a, rows):
    return jnp.concatenate([a, jnp.zeros((rows - a.shape[0],) + a.shape[1:], a.dtype)], axis=0)


_SMALL = ("c_ctx", "b_mod", "g_pre", "g_post", "conv_a", "conv_b", "conv_b_bias", "conf_ln_g", "conf_ln_b",
          "sgu_ln_g", "sgu_ln_b", "w_s", "b_s", "q_gain", "k_gain")


def _pack(arrays):
    flat = jnp.concatenate([a.reshape(-1) for a in arrays])
    rows = -(-flat.shape[0] // (8 * LANES)) * 8
    return _pad_rows(flat.reshape(-1, 1), rows * LANES).reshape(rows, LANES)


def _unpack(packed, shapes):
    flat = packed.reshape(-1)
    out, off = [], 0
    for s in shapes:
        n = int(np.prod(s))
        out.append(flat[off:off + n].reshape(s))
        off += n
    return out


def kernel(x, c, ctx, c_ctx, w_mod, b_mod, g_pre, g_post, w_in, w_out, conv_a, conv_b, conv_b_bias, conf_ln_g, conf_ln_b, sgu_ln_g, sgu_ln_b, w_s, b_s, q_gain, k_gain, loss_target, m_c_ctx, m_w_mod, m_b_mod, m_g_pre, m_g_post, m_w_in, m_w_out, m_conv_a, m_conv_b, m_conv_b_bias, m_conf_ln_g, m_conf_ln_b, m_sgu_ln_g, m_sgu_ln_b, m_w_s, m_b_s, m_q_gain, m_k_gain, v_c_ctx, v_w_mod, v_b_mod, v_g_pre, v_g_post, v_w_in, v_w_out, v_conv_a, v_conv_b, v_conv_b_bias, v_conf_ln_g, v_conf_ln_b, v_sgu_ln_g, v_sgu_ln_b, v_w_s, v_b_s, v_q_gain, v_k_gain):
    weights = dict(c_ctx=c_ctx, w_mod=w_mod, b_mod=b_mod, g_pre=g_pre, g_post=g_post, w_in=w_in, w_out=w_out, conv_a=conv_a,
                   conv_b=conv_b, conv_b_bias=conv_b_bias, conf_ln_g=conf_ln_g, conf_ln_b=conf_ln_b, sgu_ln_g=sgu_ln_g,
                   sgu_ln_b=sgu_ln_b, w_s=w_s, b_s=b_s, q_gain=q_gain, k_gain=k_gain)
    m_in = dict(c_ctx=m_c_ctx, w_mod=m_w_mod, b_mod=m_b_mod, g_pre=m_g_pre, g_post=m_g_post, w_in=m_w_in, w_out=m_w_out,
                conv_a=m_conv_a, conv_b=m_conv_b, conv_b_bias=m_conv_b_bias, conf_ln_g=m_conf_ln_g, conf_ln_b=m_conf_ln_b,
                sgu_ln_g=m_sgu_ln_g, sgu_ln_b=m_sgu_ln_b, w_s=m_w_s, b_s=m_b_s, q_gain=m_q_gain, k_gain=m_k_gain)
    v_in = dict(c_ctx=v_c_ctx, w_mod=v_w_mod, b_mod=v_b_mod, g_pre=v_g_pre, g_post=v_g_post, w_in=v_w_in, w_out=v_w_out,
                conv_a=v_conv_a, conv_b=v_conv_b, conv_b_bias=v_conv_b_bias, conf_ln_g=v_conf_ln_g, conf_ln_b=v_conf_ln_b,
                sgu_ln_g=v_sgu_ln_g, sgu_ln_b=v_sgu_ln_b, w_s=v_w_s, b_s=v_b_s, q_gain=v_q_gain, k_gain=v_k_gain)
    order = ("c_ctx", "w_mod", "b_mod", "g_pre", "g_post", "w_in", "w_out", "conv_a", "conv_b", "conv_b_bias", "conf_ln_g",
             "conf_ln_b", "sgu_ln_g", "sgu_ln_b", "w_s", "b_s", "q_gain", "k_gain")

    s_lat = x.shape[1]
    ax, ay, ac = lax.axis_index("x"), lax.axis_index("y"), lax.axis_index("c")
    chip = 2 * ax + ay
    example = 4 * ax + 2 * ay + ac

    c_rows = _all_gather_rows(_pad_rows(c, 8))[::8]
    c16 = _pad_rows(jnp.concatenate([c_rows, c_ctx[None, :]], axis=0), 16)
    b_mod_shard = lax.dynamic_slice_in_dim(b_mod, chip * SHARD_MOD, SHARD_MOD, axis=1)[:, None, :]
    silu_c, mod_shard = _mod_forward(c16, w_mod, b_mod_shard)
    mod_all = _all_gather_rows(mod_shard.reshape(DEPTH * 16, SHARD_MOD)).reshape(8, DEPTH, 16, SHARD_MOD)
    mod_full = jnp.transpose(mod_all[::2], (1, 2, 0, 3)).reshape(DEPTH, 16, 3 * D_MODEL)
    mod_lat = lax.dynamic_index_in_dim(mod_full, example, axis=1, keepdims=False).reshape(DEPTH, 3, D_MODEL)
    mod_ctx = mod_full[:, 8].reshape(DEPTH, 3, D_MODEL)
    modv = jnp.concatenate([mod_ctx, mod_lat, jnp.zeros((DEPTH, 2, D_MODEL), F32)], axis=1)

    wi_b, wo_b = w_in.astype(BF16), w_out.astype(BF16)

    def regroup(gathered):
        wi_all, wo_all = gathered
        wi_full = jnp.concatenate([wi_all[k] for k in range(N_CHIPS)], axis=-1)
        wo_l = jnp.concatenate([wo_all[k] for k in range(N_CHIPS)], axis=0)
        wc_l = jnp.concatenate([wi_full[:, 256:768], wi_full[:, 1024:1536]], axis=-1)
        wr_l = jnp.concatenate([wi_full[:, 0:256], wi_full[:, 768:1024], wi_full[:, 1536:2560], wi_full[:, 3072:3328]], axis=-1)
        return wc_l, wr_l, wi_full[:, 2560:3072], wo_l

    w_c, w_r, w_q, wo_full = [None] * DEPTH, [None] * DEPTH, [None] * DEPTH, [None] * DEPTH
    w_c[0], w_r[0], w_q[0], wo_full[0] = regroup(_gather_weights(wi_b[0], wo_b[0]))

    cos_t, sin_t = _rope_tables(s_lat)
    conv_a_full = jnp.zeros((DEPTH, 8, GROUP_W), F32)
    conv_b_full = jnp.zeros((DEPTH, 32, GROUP_W), F32)
    conv_small = jnp.concatenate([conv_a.reshape(DEPTH * SHORT_CONV_K, -1), conv_b.reshape(DEPTH * CONFORMER_K, -1)], axis=0)
    n_cs = conv_small.shape[0]
    conv_rows = -(-n_cs // 8) * 8
    conv_all = _all_gather_rows(_pad_rows(conv_small, conv_rows)).reshape(8, conv_rows, -1)[::2]
    conv_all = jnp.transpose(conv_all, (1, 0, 2)).reshape(conv_rows, GROUP_W)
    conv_a_full = conv_a_full.at[:, :SHORT_CONV_K].set(conv_all[:DEPTH * SHORT_CONV_K].reshape(DEPTH, SHORT_CONV_K, GROUP_W))
    conv_b_full = conv_b_full.at[:, :CONFORMER_K].set(
        conv_all[DEPTH * SHORT_CONV_K:n_cs].reshape(DEPTH, CONFORMER_K, GROUP_W))

    vecs = jnp.stack([conv_b_bias, conf_ln_g, conf_ln_b, sgu_ln_g, sgu_ln_b] + [jnp.zeros_like(conv_b_bias)] * 3, axis=1)
    wss = w_s.reshape(DEPTH, N_SPATIAL_GROUPS * CHUNK, CHUNK).astype(BF16)
    wsts = jnp.swapaxes(w_s, 2, 3).reshape(DEPTH, N_SPATIAL_GROUPS * CHUNK, CHUNK).astype(BF16)
    bsm = jnp.repeat(jnp.swapaxes(b_s, 1, 2), HEAD_DIM, axis=2)
    qk_gain = jnp.concatenate([jnp.tile(q_gain, (1, 2))[:, None, :], jnp.tile(k_gain, (1, 2))[:, None, :],
                               jnp.zeros((DEPTH, 6, LANES), F32)], axis=1)

    xt = jnp.concatenate([ctx[0], x[0]], axis=0)
    saved = []
    for l in range(DEPTH):
        h, pc, pr, pq, q, k, v = _in_proj(xt, modv[l], g_pre[l][None, :], w_c[l], w_r[l], w_q[l], qk_gain[l], cos_t, sin_t)
        if l + 1 < DEPTH:
            oe, gathered = _attention_fwd(q, k, v, (wi_b[l + 1], wo_b[l + 1]))
            w_c[l + 1], w_r[l + 1], w_q[l + 1], wo_full[l + 1] = regroup(gathered)
        else:
            oe, _ = _attention_fwd(q, k, v)
        x_new, y, ca, z2 = _mix_out(pc, pr, oe, xt, modv[l], g_post[l][None, :], wo_full[l], conv_a_full[l], conv_b_full[l],
                                    vecs[l], wss[l], bsm[l])
        saved.append(dict(x=xt, h=h, pc=pc, pr=pr, pq=pq, q=q, k=k, v=v, oe=oe, y=y, ca=ca, z2=z2))
        xt = x_new
    dxo, loss_acc = _loss_head(xt, loss_target[0])
    loss = lax.psum(loss_acc[0, 0], ("x", "y", "c"))

    g_small = {n: [None] * DEPTH for n in _SMALL}
    d_mod, landed = [None] * DEPTH, [None] * DEPTH
    slabs = None
    for l in reversed(range(DEPTH)):
        s = saved[l]
        dpr, g_a, g_b, doe, gw_o, pvec, s256, dws, dbs = _mix_out_bwd(
            dxo, s["y"], s["pr"], s["ca"], s["z2"], s["oe"], modv[l], g_post[l][None, :], wo_full[l], vecs[l], wss[l], wsts[l], bsm[l])
        dpc, dca, dcb = _conv_bwd(s["pc"], g_a, g_b, conv_a_full[l], conv_b_full[l])
        dq, dk, dv, got = _attention_bwd(s["q"], s["k"], s["v"], doe, slabs)
        if slabs is not None:
            landed[l + 1] = got
        dxo, dpq, acc, dgain = _in_proj_bwd(dpc, dpr, dq, dk, dv, s["pq"], qk_gain[l], cos_t, sin_t, w_c[l], w_r[l], w_q[l],
                                            s["x"], dxo, modv[l], g_pre[l][None, :])
        gw_c, gw_r, gw_q = _in_proj_wgrad(s["h"], dpc, dpr, dpq)
        gw_in = jnp.concatenate([gw_r[:, 0:256], gw_c[:, 0:512], gw_r[:, 256:512], gw_c[:, 512:1024],
                                 gw_r[:, 512:1536], gw_q, gw_r[:, 1536:1792]], axis=-1)
        slabs = (jnp.transpose(gw_in.reshape(D_MODEL, N_CHIPS, SHARD_IN), (1, 0, 2)).astype(BF16),
                 gw_o.reshape(N_CHIPS, SHARD_OUT, D_MODEL).astype(BF16))
        d_mod[l] = jnp.stack([jnp.concatenate([acc[2], acc[3], pvec[1]]), jnp.concatenate([acc[0], acc[1], pvec[0]])])
        g_small["g_pre"][l] = acc[4]
        g_small["g_post"][l] = pvec[2]
        g_small["conv_a"][l] = dca[:SHORT_CONV_K]
        g_small["conv_b"][l] = dcb[:CONFORMER_K]
        g_small["conv_b_bias"][l] = s256[0]
        g_small["conf_ln_g"][l] = s256[1]
        g_small["conf_ln_b"][l] = s256[2]
        g_small["sgu_ln_g"][l] = s256[3]
        g_small["sgu_ln_b"][l] = s256[4]
        g_small["w_s"][l] = dws.reshape(N_SPATIAL_GROUPS, CHUNK, CHUNK)
        g_small["b_s"][l] = jnp.transpose(dbs[:, :N_SPATIAL_GROUPS])
        g_small["q_gain"][l] = dgain[0, :HEAD_DIM] + dgain[0, HEAD_DIM:]
        g_small["k_gain"][l] = dgain[1, :HEAD_DIM] + dgain[1, HEAD_DIM:]
    grad_x = dxo[CTX_LEN:][None]

    d_mod_all = _all_gather_rows(jnp.stack(d_mod).reshape(DEPTH * 2, 3 * D_MODEL)).reshape(8, DEPTH, 2, 3 * D_MODEL)
    d_lat = jnp.transpose(d_mod_all[:, :, 0], (1, 0, 2))
    d_ctx = jnp.transpose(d_mod_all[:, :, 1], (1, 0, 2))
    cols = lambda a: lax.dynamic_slice_in_dim(a.reshape(DEPTH, 8, N_CHIPS, SHARD_MOD), chip, 1, axis=2)[:, :, 0]
    silu_t = jnp.transpose(silu_c)
    s_t = jnp.concatenate([silu_t[:, 0:8], jnp.tile(silu_t[:, 8:9], (1, 8)), jnp.zeros((D_MODEL, LANES - 16), F32)], axis=1)
    g_rows = jnp.concatenate([cols(d_lat), cols(d_ctx), jnp.zeros((DEPTH, LANES - 16, SHARD_MOD), F32)], axis=1)
    g_w_mod, g_b_mod, c_ctx_part = _mod_backward(s_t, g_rows, cols(d_ctx), jnp.concatenate([d_lat, d_ctx], axis=1),
                                                 w_mod, c_ctx[:, None])

    for n in _SMALL:
        if n not in ("c_ctx", "b_mod"):
            g_small[n] = jnp.stack(g_small[n])
    small_parts = [0.5 * c_ctx_part[:, 0]] + [g_small[n] for n in _SMALL[2:]]
    packed = _pack(small_parts)
    gathered = _all_gather_rows(packed).reshape(8, packed.shape[0], LANES)
    small_sum = _sum_slabs(gathered, packed.shape[0])
    small_g = dict(zip(("c_ctx",) + _SMALL[2:], _unpack(small_sum, [p.shape for p in small_parts])))
    small_g["b_mod"] = g_b_mod[:, 0]
    ch64 = GROUP_W // N_CHIPS
    for n in ("conv_a", "conv_b"):
        small_g[n] = lax.dynamic_slice_in_dim(small_g[n], chip * ch64, ch64, axis=2)
    sw = _pack([weights[n] for n in _SMALL])
    sm = _pack([m_in[n] for n in _SMALL])
    sv = _pack([v_in[n] for n in _SMALL])
    sg = _pack([small_g[n] for n in _SMALL])
    shapes = [weights[n].shape for n in _SMALL]
    small_out = [dict(zip(_SMALL, _unpack(o, shapes))) for o in _adamw([sg], sw, sm, sv, sg.shape[0])]

    landed[0] = _scatter_slabs(*slabs)
    sum_in = _sum_layer_slabs(jnp.stack([landed[l][0] for l in range(DEPTH)]), 512)
    sum_out = _sum_layer_slabs(jnp.stack([landed[l][1] for l in range(DEPTH)]), 256)
    sib_in, sib_out = _swap_with_sibling(sum_in, sum_out)

    big = {}
    flat = lambda a: a.reshape(-1, a.shape[-1])
    for n, grads, rows in (("w_in", [sum_in, sib_in], 512), ("w_out", [sum_out, sib_out], 256), ("w_mod", [flat(g_w_mod)], 512)):
        outs = _adamw(grads, flat(weights[n]), flat(m_in[n]), flat(v_in[n]), rows)
        big[n] = [o.reshape(weights[n].shape) for o in outs]

    def leaf(n, j):
        return big[n][j] if n in big else small_out[j][n]

    return (loss, grad_x, *[leaf(n, 0) for n in order], *[leaf(n, 1) for n in order],
            *[leaf(n, 2) for n in order], *[leaf(n, 3) for n in order])
```

```python
import functools

import numpy as np
import jax
import jax.numpy as jnp
from jax import lax
from jax.experimental import pallas as pl
from jax.experimental.pallas import tpu as pltpu

F32 = jnp.float32
BF16 = jnp.bfloat16
MESH = pl.DeviceIdType.MESH

D_MODEL = 1024
DEPTH = 4
GRID_W = 64
CTX_LEN = 256
GROUP_W = 256
HEAD_DIM = 64
N_Q_HEADS = 4
N_KV_HEADS = 2
GQA = N_Q_HEADS // N_KV_HEADS
ROPE_THETA = 10000.0
ATTN_SCALE = HEAD_DIM ** -0.5
SHORT_CONV_K = 3
CONFORMER_K = 31
CHUNK = 128
N_SPATIAL_GROUPS = 4
RMS_EPS = 1e-6
LN_EPS = 1e-5
ADAM_LR = 0.001
ADAM_B1 = 0.9
ADAM_B2 = 0.999
ADAM_EPS = 1e-08
ADAM_WD = 0.01
ADAM_STEP = 10

LANES = 128
HALO = 16
CONV_ROWS = 64
TM = 256
N_CTX_TILES = CTX_LEN // TM
SUB = 3
TB = SUB * TM
W_C = 1024
W_R = 1792
W_Q = 512
PROJ_W = W_C + W_R + W_Q
N_CHIPS = 4
SHARD_IN = PROJ_W // N_CHIPS
SHARD_OUT = D_MODEL // N_CHIPS
SHARD_MOD = 3 * D_MODEL // N_CHIPS
VMEM_LIMIT = 56 * 1024 * 1024


def _pc(body, **kw):
    return pl.pallas_call(body, **kw)


def _cparams(**kw):
    return pltpu.CompilerParams(dimension_semantics=("arbitrary",), vmem_limit_bytes=VMEM_LIMIT, **kw)


def _full(shape):
    n = len(shape)
    return pl.BlockSpec(shape, lambda i: (0,) * n)


def _const(shape):
    n = len(shape)
    return pl.BlockSpec(shape, lambda i: (0,) * n, pipeline_mode=pl.Buffered(1))


def _rows(width, tm=TM):
    return pl.BlockSpec((tm, width), lambda i: (i, 0))


def _heads(nh, width, tm=TM):
    return pl.BlockSpec((nh, tm, width), lambda i: (0, i, 0))


def _sigmoid(x):
    return jax.nn.sigmoid(x)


def _dot(a, b):
    return jnp.dot(a, b, preferred_element_type=F32)


def _dot_nt(a, b):
    return lax.dot_general(a, b, (((1,), (1,)), ((), ())), preferred_element_type=F32)


def _dot_tn(a, b):
    return lax.dot_general(a, b, (((0,), (0,)), ((), ())), preferred_element_type=F32)


def _lane(rows):
    return lax.broadcasted_iota(jnp.int32, (rows, LANES), 1)


def _rowsum(x):
    return jnp.sum(x, axis=1, keepdims=True)


def _colsum(x):
    return jnp.sum(x, axis=0, keepdims=True)


def _pair_sums(x, lo):
    s0 = _rowsum(jnp.where(lo, x, 0.0))
    s1 = _rowsum(jnp.where(lo, 0.0, x))
    return jnp.where(lo, s0, s1)


def _swap16(x, lo16):
    return jnp.where(lo16, pltpu.roll(x, LANES - 16, 1), pltpu.roll(x, 16, 1))


def _layer_norm_stats(x):
    mu = jnp.mean(x, axis=1, keepdims=True)
    xc = x - mu
    rs = lax.rsqrt(jnp.mean(xc * xc, axis=1, keepdims=True) + LN_EPS)
    return xc * rs, rs


def _layer_norm_bwd(dxn, xn, rs):
    return rs * (dxn - jnp.mean(dxn, axis=1, keepdims=True) - xn * jnp.mean(dxn * xn, axis=1, keepdims=True))


def _group_select(r, grp):
    out = jnp.where(grp == 0, r[0:CHUNK], 0.0)
    for g in range(1, N_SPATIAL_GROUPS):
        out = out + jnp.where(grp == g, r[g * CHUNK:(g + 1) * CHUNK], 0.0)
    return out


def _kv_chunk(t):
    return 1024 if (t - CTX_LEN) % 1024 == 0 else 256


def _all_gather_rows(x_shard):
    m_per, n = x_shard.shape

    def body(x_ref, out_ref, send_sems, recv_sems, local_sem):
        x, y, c = lax.axis_index("x"), lax.axis_index("y"), lax.axis_index("c")
        me, sibling = (x, y, c), (x, y, 1 - c)
        chips = [(1 - x, y), (x, 1 - y), (1 - x, 1 - y)]

        def rows(px, py, pc):
            return out_ref.at[pl.ds((4 * px + 2 * py + pc) * m_per, m_per), :]

        def copy(k, block, to, src=None):
            return pltpu.make_async_remote_copy(
                src_ref=rows(*block) if src is None else src, dst_ref=rows(*block),
                send_sem=send_sems.at[k], recv_sem=recv_sems.at[k], device_id=to, device_id_type=MESH)

        mine = pltpu.make_async_copy(x_ref, rows(*me), local_sem)
        mine.start()
        first = [copy(0, me, sibling, src=x_ref)]
        first += [copy(1 + j, me, (*chip, c), src=x_ref) for j, chip in enumerate(chips)]
        for cp in first:
            cp.start()
        passed = [copy(4 + j, (*chip, c), sibling) for j, chip in enumerate(chips)]
        for j, chip in enumerate(chips):
            copy(1 + j, (*chip, c), me).wait_recv()
            passed[j].start()
        copy(0, sibling, me).wait_recv()
        for j, chip in enumerate(chips):
            copy(4 + j, (*chip, 1 - c), me).wait_recv()
        for cp in first + passed:
            cp.wait_send()
        mine.wait()

    return _pc(
        body, name="all_gather_rows",
        out_shape=jax.ShapeDtypeStruct((8 * m_per, n), x_shard.dtype),
        in_specs=[pl.BlockSpec(memory_space=pltpu.VMEM)],
        out_specs=pl.BlockSpec(memory_space=pltpu.VMEM),
        scratch_shapes=[pltpu.SemaphoreType.DMA((7,)), pltpu.SemaphoreType.DMA((7,)), pltpu.SemaphoreType.DMA],
        compiler_params=pltpu.CompilerParams(vmem_limit_bytes=VMEM_LIMIT),
    )(x_shard)


def _place():
    x, y, c = lax.axis_index("x"), lax.axis_index("y"), lax.axis_index("c")
    return x, y, c, [(1 - x, y), (x, 1 - y), (1 - x, 1 - y)]


def _remote(src, dst, send_sems, recv_sems, k, to):
    return pltpu.make_async_remote_copy(src_ref=src, dst_ref=dst, send_sem=send_sems.at[k], recv_sem=recv_sems.at[k],
                                        device_id=to, device_id_type=MESH)


GATHER_SEMS = 12
SCATTER_SEMS = 6


def _gather_phase(phase, pairs, send_sems, recv_sems, local_sems):
    x, y, c, chips = _place()
    kme = 2 * x + y
    sibling = (x, y, 1 - c)
    for a, (src, dst) in enumerate(pairs):
        half = src.shape[0] // 2
        mine = pl.ds(c * half, half)
        theirs = pl.ds((1 - c) * half, half)
        if phase == 0:
            pltpu.make_async_copy(src, dst.at[kme], local_sems.at[a]).start()
        if phase == 2:
            pltpu.make_async_copy(src, dst.at[kme], local_sems.at[a]).wait()
        for j, (px, py) in enumerate(chips):
            kk = 2 * px + py
            landed = dst.at[kk, mine]
            out = lambda: _remote(src.at[mine], dst.at[kme, mine], send_sems, recv_sems, 6 * a + j, (px, py, c))
            hand = lambda: _remote(landed, landed, send_sems, recv_sems, 6 * a + 3 + j, sibling)
            if phase == 0:
                out().start()
            if phase == 1:
                _remote(landed, landed, send_sems, recv_sems, 6 * a + j, (px, py, c)).wait_recv()
                hand().start()
            if phase == 2:
                other = dst.at[kk, theirs]
                _remote(other, other, send_sems, recv_sems, 6 * a + 3 + j, sibling).wait_recv()
                out().wait_send()
                hand().wait_send()


def _scatter_phase(phase, pairs, send_sems, recv_sems, local_sems):
    x, y, c, chips = _place()
    kme = 2 * x + y
    for a, (src, dst) in enumerate(pairs):
        loc = pltpu.make_async_copy(src.at[kme], dst.at[kme], local_sems.at[a])
        if phase == 0:
            loc.start()
        else:
            loc.wait()
        for j, (px, py) in enumerate(chips):
            kk = 2 * px + py
            out = _remote(src.at[kk], dst.at[kme], send_sems, recv_sems, 3 * a + j, (px, py, c))
            if phase == 0:
                out.start()
            else:
                landed = dst.at[kk]
                _remote(landed, landed, send_sems, recv_sems, 3 * a + j, (px, py, c)).wait_recv()
                out.wait_send()


def _comm_scratch(n):
    return [pltpu.SemaphoreType.DMA((n,)), pltpu.SemaphoreType.DMA((n,)), pltpu.SemaphoreType.DMA((2,))]


def _slots(a):
    return jax.ShapeDtypeStruct((N_CHIPS,) + a.shape, a.dtype)


def _gather_weights(wi, wo):
    def body(wi_ref, wo_ref, gi_ref, go_ref, send_sems, recv_sems, local_sems):
        for phase in range(3):
            _gather_phase(phase, ((wi_ref, gi_ref), (wo_ref, go_ref)), send_sems, recv_sems, local_sems)

    hbm = pl.BlockSpec(memory_space=pl.ANY)
    return _pc(
        body, name="gather_weights", out_shape=(_slots(wi), _slots(wo)),
        in_specs=[hbm, hbm], out_specs=(hbm, hbm), scratch_shapes=_comm_scratch(GATHER_SEMS),
    )(wi, wo)


def _scatter_slabs(gi, go):
    def body(gi_ref, go_ref, ri_ref, ro_ref, send_sems, recv_sems, local_sems):
        for phase in range(2):
            _scatter_phase(phase, ((gi_ref, ri_ref), (go_ref, ro_ref)), send_sems, recv_sems, local_sems)

    hbm = pl.BlockSpec(memory_space=pl.ANY)
    return _pc(
        body, name="scatter_slabs",
        out_shape=(jax.ShapeDtypeStruct(gi.shape, gi.dtype), jax.ShapeDtypeStruct(go.shape, go.dtype)),
        in_specs=[hbm, hbm], out_specs=(hbm, hbm), scratch_shapes=_comm_scratch(SCATTER_SEMS),
    )(gi, go)


def _swap_with_sibling(a, b):
    def body(a_ref, b_ref, ra_ref, rb_ref, send_sems, recv_sems):
        x, y, c = lax.axis_index("x"), lax.axis_index("y"), lax.axis_index("c")
        copies = []
        for k, (src, dst) in enumerate(((a_ref, ra_ref), (b_ref, rb_ref))):
            cp = pltpu.make_async_remote_copy(
                src_ref=src, dst_ref=dst, send_sem=send_sems.at[k], recv_sem=recv_sems.at[k],
                device_id=(x, y, 1 - c), device_id_type=MESH)
            cp.start()
            copies.append(cp)
        for cp in copies:
            cp.wait()

    hbm = pl.BlockSpec(memory_space=pl.ANY)
    return _pc(
        body, name="swap_with_sibling",
        out_shape=(jax.ShapeDtypeStruct(a.shape, a.dtype), jax.ShapeDtypeStruct(b.shape, b.dtype)),
        in_specs=[hbm, hbm], out_specs=(hbm, hbm),
        scratch_shapes=[pltpu.SemaphoreType.DMA((2,)), pltpu.SemaphoreType.DMA((2,))],
    )(a, b)


def _mod_forward(c16, w_mod, b_mod_shard):
    def body(c_ref, w_ref, b_ref, s_ref, o_ref):
        cc = c_ref[...]
        s = cc * _sigmoid(cc)
        s_ref[...] = s
        o_ref[0] = jnp.dot(s, w_ref[0], preferred_element_type=F32, precision=lax.Precision.HIGHEST) + b_ref[0]

    return _pc(
        body, name="mod_forward", grid=(DEPTH,),
        out_shape=(jax.ShapeDtypeStruct((16, D_MODEL), F32), jax.ShapeDtypeStruct((DEPTH, 16, SHARD_MOD), F32)),
        in_specs=[_full((16, D_MODEL)),
                  pl.BlockSpec((1, D_MODEL, SHARD_MOD), lambda l: (l, 0, 0)),
                  pl.BlockSpec((1, 1, SHARD_MOD), lambda l: (l, 0, 0))],
        out_specs=(_full((16, D_MODEL)), pl.BlockSpec((1, 16, SHARD_MOD), lambda l: (l, 0, 0))),
        compiler_params=_cparams(),
    )(c16, w_mod, b_mod_shard)


def _mod_backward(s_t, g_rows, g_ctx, d_all, w_mod, c_ctx_col):
    def body(st_ref, g_ref, gc_ref, d_ref, w_ref, cc_ref, gw_ref, gb_ref, pc_ref):
        l = pl.program_id(0)
        gw_ref[0] = jnp.dot(st_ref[...], g_ref[0], preferred_element_type=F32, precision=lax.Precision.HIGHEST)
        gb_ref[0] = _colsum(d_ref[0])
        part = _rowsum(w_ref[0] * _colsum(gc_ref[0]))

        @pl.when(l == 0)
        def _():
            pc_ref[...] = jnp.zeros_like(pc_ref)

        pc_ref[...] += part

        @pl.when(l == DEPTH - 1)
        def _():
            cc = cc_ref[...]
            sg = _sigmoid(cc)
            pc_ref[...] = pc_ref[...] * (sg * (1.0 + cc * (1.0 - sg)))

    return _pc(
        body, name="mod_backward", grid=(DEPTH,),
        out_shape=(jax.ShapeDtypeStruct((DEPTH, D_MODEL, SHARD_MOD), F32),
                   jax.ShapeDtypeStruct((DEPTH, 1, 3 * D_MODEL), F32),
                   jax.ShapeDtypeStruct((D_MODEL, 1), F32)),
        in_specs=[_full((D_MODEL, LANES)),
                  pl.BlockSpec((1, LANES, SHARD_MOD), lambda l: (l, 0, 0)),
                  pl.BlockSpec((1, 8, SHARD_MOD), lambda l: (l, 0, 0)),
                  pl.BlockSpec((1, 16, 3 * D_MODEL), lambda l: (l, 0, 0)),
                  pl.BlockSpec((1, D_MODEL, SHARD_MOD), lambda l: (l, 0, 0)),
                  _full((D_MODEL, 1))],
        out_specs=(pl.BlockSpec((1, D_MODEL, SHARD_MOD), lambda l: (l, 0, 0)),
                   pl.BlockSpec((1, 1, 3 * D_MODEL), lambda l: (l, 0, 0)),
                   _full((D_MODEL, 1))),
        compiler_params=_cparams(),
    )(s_t, g_rows, g_ctx, d_all, w_mod, c_ctx_col)


def _head_norm(xb, lo):
    r = lax.rsqrt(_pair_sums(xb * xb, lo) * (1.0 / HEAD_DIM) + RMS_EPS)
    return xb * r, r


def _in_proj(xt, modv, g_pre, w_c, w_r, w_q, qk_gain, cos_t, sin_t):
    t = xt.shape[0]

    def body(x_ref, mod_ref, g_ref, wc_ref, wr_ref, wq_ref, gain_ref, cos_ref, sin_ref,
             h_ref, pc_ref, pr_ref, pq_ref, q_ref, k_ref, v_ref):
        lane = _lane(TM)
        lo = lane < HEAD_DIM
        lo16 = (lane & 31) < 16
        one = jnp.where(lane == HEAD_DIM, 1.0, 0.0)
        for jj in range(SUB):
            rows = pl.ds(jj * TM, TM)
            is_ctx = pl.program_id(0) * SUB + jj < N_CTX_TILES
            x = x_ref[rows, :]
            r = lax.rsqrt(jnp.mean(x * x, axis=1, keepdims=True) + RMS_EPS)
            sh = jnp.where(is_ctx, mod_ref[0:1, :], mod_ref[3:4, :])
            sc = jnp.where(is_ctx, mod_ref[1:2, :], mod_ref[4:5, :])
            h = (x * r * g_ref[...]) * (1.0 + sc) + sh
            hb = h.astype(BF16)
            h_ref[rows, :] = hb
            pc_ref[rows, :] = _dot(hb, wc_ref[...])
            pr_ref[rows, :] = _dot(hb, wr_ref[...])
            pq = _dot(hb, wq_ref[...])
            pq_ref[rows, :] = pq
            cos = cos_ref[rows, :]
            sin = sin_ref[rows, :]
            for b in range(3):
                xh, _ = _head_norm(pq[:, b * LANES:(b + 1) * LANES], lo)
                xg = xh * (gain_ref[0:1, :] if b < 2 else gain_ref[1:2, :])
                rot = xg * cos + _swap16(xg, lo16) * sin
                if b < 2:
                    rot = rot * ATTN_SCALE
                dst = q_ref if b < 2 else k_ref
                base = 2 * b if b < 2 else 0
                dst[base, rows, :] = jnp.where(lo, rot, 0.0).astype(BF16)
                dst[base + 1, rows, :] = jnp.where(lo, pltpu.roll(rot, HEAD_DIM, 1), 0.0).astype(BF16)
            vb = pq[:, 3 * LANES:4 * LANES]
            v_ref[0, rows, :] = jnp.where(lo, vb, one).astype(BF16)
            v_ref[1, rows, :] = jnp.where(lo, pltpu.roll(vb, HEAD_DIM, 1), one).astype(BF16)

    return _pc(
        body, name="in_proj", grid=(t // TB,),
        out_shape=(jax.ShapeDtypeStruct((t, D_MODEL), BF16),
                   jax.ShapeDtypeStruct((t, W_C), F32), jax.ShapeDtypeStruct((t, W_R), F32), jax.ShapeDtypeStruct((t, W_Q), F32),
                   jax.ShapeDtypeStruct((N_Q_HEADS, t, LANES), BF16),
                   jax.ShapeDtypeStruct((N_KV_HEADS, t, LANES), BF16),
                   jax.ShapeDtypeStruct((N_KV_HEADS, t, LANES), BF16)),
        in_specs=[_rows(D_MODEL, TB), _const((8, D_MODEL)), _const((1, D_MODEL)),
                  _const((D_MODEL, W_C)), _const((D_MODEL, W_R)), _const((D_MODEL, W_Q)),
                  _const((8, LANES)), _rows(LANES, TB), _rows(LANES, TB)],
        out_specs=(_rows(D_MODEL, TB), _rows(W_C, TB), _rows(W_R, TB), _rows(W_Q, TB),
                   _heads(N_Q_HEADS, LANES, TB), _heads(N_KV_HEADS, LANES, TB), _heads(N_KV_HEADS, LANES, TB)),
        compiler_params=_cparams(),
    )(xt, modv, g_pre, w_c, w_r, w_q, qk_gain, cos_t, sin_t)


def _attention_fwd(q, k, v, shards=None):
    t = q.shape[1]
    tk = _kv_chunk(t)
    n_chunks = (t - CTX_LEN) // tk
    n_tiles = t // TM
    n_sh = 0 if shards is None else len(shards)

    def body(q_ref, k_ref, v_ref, *rest):
        i = pl.program_id(0)
        o_ref = rest[n_sh]
        if shards is not None:
            pairs = tuple(zip(rest[:n_sh], rest[n_sh + 1:2 * n_sh + 1]))
            for phase, at in enumerate((0, n_tiles // 2, n_tiles - 1)):
                @pl.when(i == at)
                def _(phase=phase):
                    _gather_phase(phase, pairs, *rest[2 * n_sh + 1:])
        lane = _lane(GQA * TM)
        qs = [jnp.concatenate([q_ref[GQA * g + hh] for hh in range(GQA)], axis=0) for g in range(N_KV_HEADS)]

        def step(st, size, carry):
            out = []
            for g in range(N_KV_HEADS):
                m, acc = carry[g]
                s = _dot_nt(qs[g], k_ref[g, pl.ds(st, size), :])
                m_new = jnp.maximum(m, jnp.max(s, axis=1, keepdims=True))
                p = jnp.exp(s - m_new)
                out.append((m_new, acc * jnp.exp(m - m_new) + _dot(p.astype(BF16), v_ref[g, pl.ds(st, size), :])))
            return tuple(out)

        init = tuple((jnp.full((GQA * TM, 1), -jnp.inf, F32), jnp.zeros((GQA * TM, LANES), F32)) for _ in range(N_KV_HEADS))

        def finish(carry):
            for g in range(N_KV_HEADS):
                m, acc = carry[g]
                den = _rowsum(jnp.where(lane == HEAD_DIM, acc, 0.0))
                out = jnp.where(lane < HEAD_DIM, acc * (1.0 / den), jnp.where(lane == HEAD_DIM, m + jnp.log(den), 0.0))
                for hh in range(GQA):
                    o_ref[GQA * g + hh] = out[hh * TM:(hh + 1) * TM]

        @pl.when(i < N_CTX_TILES)
        def _():
            finish(step(0, CTX_LEN, init))

        @pl.when(i >= N_CTX_TILES)
        def _():
            def two_steps(j, cr):
                st = pl.multiple_of(CTX_LEN + j * (2 * tk), 256)
                return step(st + tk, tk, step(st, tk, cr))

            carry = lax.fori_loop(0, n_chunks // 2, two_steps, step(0, CTX_LEN, init))
            if n_chunks % 2:
                carry = step(CTX_LEN + (n_chunks - 1) * tk, tk, carry)
            finish(carry)

    hbm = pl.BlockSpec(memory_space=pl.ANY)
    extra = () if shards is None else tuple(shards)
    outs = _pc(
        body, name="attention_fwd" if shards is None else "attention_fwd_gather", grid=(n_tiles,),
        out_shape=(jax.ShapeDtypeStruct((N_Q_HEADS, t, LANES), F32),) + tuple(_slots(a) for a in extra),
        in_specs=[_heads(N_Q_HEADS, LANES), _full((N_KV_HEADS, t, LANES)), _full((N_KV_HEADS, t, LANES))] + [hbm] * n_sh,
        out_specs=(_heads(N_Q_HEADS, LANES),) + (hbm,) * n_sh,
        scratch_shapes=_comm_scratch(GATHER_SEMS) if shards is not None else [],
        compiler_params=_cparams(),
    )(q, k, v, *extra)
    return outs[0], tuple(outs[1:])


def _halo_specs(width, t):
    last = t // HALO - 1
    per = TM // HALO
    prev = pl.BlockSpec((HALO, width), lambda i: (jnp.maximum(i * per - 1, 0), 0))
    nxt = pl.BlockSpec((HALO, width), lambda i: (jnp.minimum((i + 1) * per, last), 0))
    return prev, nxt


def _halo_valid(i, n_tiles):
    prev_ok = jnp.logical_and(i != 0, i != N_CTX_TILES)
    next_ok = jnp.logical_and(i != N_CTX_TILES - 1, i != n_tiles - 1)
    return jnp.where(prev_ok, 1.0, 0.0), jnp.where(next_ok, 1.0, 0.0)


def _conv_inputs(pc):
    u = pc[:, 0:GROUP_W] * pc[:, GROUP_W:2 * GROUP_W]
    z = pc[:, 2 * GROUP_W:3 * GROUP_W] * _sigmoid(pc[:, 3 * GROUP_W:4 * GROUP_W])
    return u, z


def _fill_ext(ext_ref, prev, mid, nxt):
    ext_ref[0:HALO, :] = prev
    ext_ref[HALO:HALO + TM, :] = mid
    ext_ref[HALO + TM:HALO + TM + HALO, :] = nxt


def _row_local_mixers(pr, ca, z2, oe, vecs, wss_ref, bsm, lane256):
    a_b, a_g, b_g = pr[:, 0:256], pr[:, 256:512], pr[:, 512:768]
    c_u, c_v, c_g, d_g = pr[:, 768:1024], pr[:, 1024:1280], pr[:, 1280:1536], pr[:, 1536:1792]
    zn, rs_b = _layer_norm_stats(z2)
    tb = zn * vecs[1:2, :] + vecs[2:3, :]
    vn_hat, rs_c = _layer_norm_stats(c_v)
    vn = vn_hat * vecs[3:4, :] + vecs[4:5, :]
    grp = jnp.right_shift(lane256, 6)
    sgs = []
    for ch in range(TM // CHUNK):
        r = _dot(wss_ref[...], vn[ch * CHUNK:(ch + 1) * CHUNK, :].astype(BF16))
        sgs.append(_group_select(r, grp[0:CHUNK]) + bsm)
    sg = jnp.concatenate(sgs, axis=0)
    lane = _lane(TM)
    lo = lane < HEAD_DIM
    att = jnp.concatenate([jnp.where(lo, oe[2 * b], pltpu.roll(oe[2 * b + 1], HEAD_DIM, 1)) for b in range(2)], axis=1)
    return dict(a_b=a_b, a_g=a_g, b_g=b_g, c_u=c_u, c_v=c_v, c_g=c_g, d_g=d_g, zn=zn, rs_b=rs_b, tb=tb,
                vn_hat=vn_hat, rs_c=rs_c, vn=vn, sg=sg, att=att, grp=grp, lo=lo, lane=lane)


def _mixer_concat(f, ca):
    ya = f["a_b"] * ca
    yb = f["tb"] * _sigmoid(f["tb"])
    yc = f["c_u"] * f["sg"]
    gates = [f[n] * _sigmoid(f[n]) for n in ("a_g", "b_g", "c_g", "d_g")]
    ys = (ya, yb, yc, f["att"])
    big = jnp.concatenate([yy * gg for yy, gg in zip(ys, gates)], axis=1).astype(BF16)
    return big, ys, gates


def _taps31(ext_ref, w_ref, flip):
    blocks = []
    for r0 in range(0, TM, CONV_ROWS):
        out = None
        for b in range(8):
            part = None
            for a in range(4):
                o = 8 * a + b
                if 1 <= o <= CONFORMER_K:
                    kk = CONFORMER_K - o if flip else o - 1
                    term = w_ref[kk:kk + 1, :] * ext_ref[pl.ds(r0 + 8 * a, CONV_ROWS + 8), :]
                    part = term if part is None else part + term
            part = part[b:b + CONV_ROWS]
            out = part if out is None else out + part
        blocks.append(out)
    return jnp.concatenate(blocks, axis=0)


def _mix_out(pc, pr, oe, xt, modv, g_post, w_out, conv_a, conv_b, vecs, wss, bsm, target=None):
    t = xt.shape[0]
    n_tiles = t // TM
    prev_spec, next_spec = _halo_specs(W_C, t)
    n_t = 0 if target is None else 1

    def body(pc_ref, pp_ref, pn_ref, pr_ref, oe_ref, x_ref, mod_ref, gp_ref, wo_ref, cva_ref, cvb_ref, vec_ref, wss_ref, bsm_ref,
             *rest):
        xo_ref, y_ref, ca_ref, z2_ref = rest[n_t:n_t + 4]
        uext, zext = rest[-2:]
        i = pl.program_id(0)
        is_ctx = i < N_CTX_TILES
        pv, nv = _halo_valid(i, n_tiles)
        u, z = _conv_inputs(pc_ref[...])
        up, zp = _conv_inputs(pp_ref[...])
        un, zn_ = _conv_inputs(pn_ref[...])
        _fill_ext(uext, up * pv, u, un * nv)
        _fill_ext(zext, zp * pv, z, zn_ * nv)
        ca = cva_ref[0:1, :] * uext[pl.ds(HALO - 1, TM), :]
        for kk in range(1, SHORT_CONV_K):
            ca = ca + cva_ref[kk:kk + 1, :] * uext[pl.ds(HALO - 1 + kk, TM), :]
        vecs = vec_ref[...]
        z2 = _taps31(zext, cvb_ref, False) + vecs[0:1, :]
        ca_ref[...] = ca
        z2_ref[...] = z2
        lane256 = lax.broadcasted_iota(jnp.int32, (TM, GROUP_W), 1)
        f = _row_local_mixers(pr_ref[...], ca, z2, oe_ref, vecs, wss_ref, bsm_ref[...], lane256)
        big, _, _ = _mixer_concat(f, ca)
        y = _dot(big, wo_ref[...])
        y_ref[...] = y
        ry = lax.rsqrt(jnp.mean(y * y, axis=1, keepdims=True) + RMS_EPS)
        gt = jnp.where(is_ctx, mod_ref[2:3, :], mod_ref[5:6, :])
        x_new = x_ref[...] + gt * (y * ry * gp_ref[...])
        if target is None:
            xo_ref[...] = x_new
        else:
            loss_ref = rest[n_t + 4]

            @pl.when(i == 0)
            def _():
                loss_ref[...] = jnp.zeros_like(loss_ref)

            err = (x_new - rest[0][...]) * jnp.where(is_ctx, 0.0, 1.0)
            xo_ref[...] = err * (1.0 / D_MODEL)
            loss_ref[...] += jnp.sum(err * err) * (0.5 / D_MODEL)

    rows_f32 = jax.ShapeDtypeStruct((t, D_MODEL), F32)
    group_f32 = jax.ShapeDtypeStruct((t, GROUP_W), F32)
    loss_shape, loss_spec, t_spec, t_arg = (), (), [], ()
    if target is not None:
        loss_shape, loss_spec = (jax.ShapeDtypeStruct((8, LANES), F32),), (_full((8, LANES)),)
        t_spec = [pl.BlockSpec((TM, D_MODEL), lambda i: (jnp.maximum(i - N_CTX_TILES, 0), 0))]
        t_arg = (target,)
    return _pc(
        body, name="mix_out" if target is None else "mix_out_loss", grid=(n_tiles,),
        out_shape=(rows_f32, rows_f32, group_f32, group_f32) + loss_shape,
        in_specs=[_rows(W_C), prev_spec, next_spec, _rows(W_R), _heads(N_Q_HEADS, LANES), _rows(D_MODEL),
                  _const((8, D_MODEL)), _const((1, D_MODEL)), _const((D_MODEL, D_MODEL)),
                  _const((8, GROUP_W)), _const((32, GROUP_W)), _const((8, GROUP_W)),
                  _const((N_SPATIAL_GROUPS * CHUNK, CHUNK)), _const((CHUNK, GROUP_W))] + t_spec,
        out_specs=(_rows(D_MODEL), _rows(D_MODEL), _rows(GROUP_W), _rows(GROUP_W)) + loss_spec,
        scratch_shapes=[pltpu.VMEM((TM + 2 * HALO, GROUP_W), F32), pltpu.VMEM((TM + 2 * HALO, GROUP_W), F32)],
        compiler_params=_cparams(),
    )(pc, pc, pc, pr, oe, xt, modv, g_post, w_out, conv_a, conv_b, vecs, wss, bsm, *t_arg)


def _mix_out_bwd(dxo, y, pr, ca, z2, oe, modv, g_post, w_out, vecs, wss, wsts, bsm):
    t = y.shape[0]
    n_tiles = t // TM

    def body(dxo_ref, y_ref, pr_ref, ca_ref, z2_ref, oe_ref, mod_ref, gp_ref, wo_ref, vec_ref, wss_ref, wsts_ref, bsm_ref,
             dpr_ref, ga_ref, gb_ref, doe_ref, dwo_ref, pvec_ref, s256_ref, dws_ref, dbs_ref, dbsm):
        i = pl.program_id(0)
        is_ctx = i < N_CTX_TILES

        @pl.when(i == 0)
        def _():
            dwo_ref[...] = jnp.zeros_like(dwo_ref)
            pvec_ref[...] = jnp.zeros_like(pvec_ref)
            s256_ref[...] = jnp.zeros_like(s256_ref)
            dws_ref[...] = jnp.zeros_like(dws_ref)
            dbsm[...] = jnp.zeros_like(dbsm)

        dxo_ = dxo_ref[...]
        y_ = y_ref[...]
        ry = lax.rsqrt(jnp.mean(y_ * y_, axis=1, keepdims=True) + RMS_EPS)
        nh = y_ * ry
        gp = gp_ref[...]
        gt = jnp.where(is_ctx, mod_ref[2:3, :], mod_ref[5:6, :])
        dgt = _colsum(dxo_ * (nh * gp))
        pvec_ref[0:1, :] += jnp.where(is_ctx, dgt, 0.0)
        pvec_ref[1:2, :] += jnp.where(is_ctx, 0.0, dgt)
        dn = dxo_ * gt
        pvec_ref[2:3, :] += _colsum(dn * nh)
        dnh = dn * gp
        dy = ry * (dnh - nh * jnp.mean(dnh * nh, axis=1, keepdims=True))

        vecs = vec_ref[...]
        bsm_ = bsm_ref[...]
        ca_ = ca_ref[...]
        lane256 = lax.broadcasted_iota(jnp.int32, (TM, GROUP_W), 1)
        f = _row_local_mixers(pr_ref[...], ca_, z2_ref[...], oe_ref, vecs, wss_ref, bsm_, lane256)
        big, ys, gates = _mixer_concat(f, ca_)
        dyb = dy.astype(BF16)
        dwo_ref[...] += _dot_tn(big, dyb)
        dbig = _dot_nt(dyb, wo_ref[...])

        d_y, d_gate = [], []
        for n, (name, yy, gg) in enumerate(zip(("a_g", "b_g", "c_g", "d_g"), ys, gates)):
            dpart = dbig[:, n * GROUP_W:(n + 1) * GROUP_W]
            gx = f[name]
            sg_ = _sigmoid(gx)
            d_y.append(dpart * gg)
            d_gate.append(dpart * yy * (sg_ * (1.0 + gx * (1.0 - sg_))))
        dya, dyb_, dyc, datt = d_y

        d_ab = dya * ca_
        ga_ref[...] = dya * f["a_b"]
        tb = f["tb"]
        sb = _sigmoid(tb)
        dtb = dyb_ * (sb * (1.0 + tb * (1.0 - sb)))
        s256_ref[1:2, :] += _colsum(dtb * f["zn"])
        s256_ref[2:3, :] += _colsum(dtb)
        dz2 = _layer_norm_bwd(dtb * vecs[1:2, :], f["zn"], f["rs_b"])
        gb_ref[...] = dz2
        s256_ref[0:1, :] += _colsum(dz2)
        d_cu = dyc * f["sg"]
        dsg = dyc * f["c_u"]
        grp = f["grp"]
        dvn_parts = []
        for ch in range(TM // CHUNK):
            rows = slice(ch * CHUNK, (ch + 1) * CHUNK)
            dsg_c = dsg[rows, :]
            dbsm[...] += dsg_c
            vn_c = f["vn"][rows, :].astype(BF16)
            for g in range(N_SPATIAL_GROUPS):
                masked = jnp.where(grp[0:CHUNK] == g, dsg_c, 0.0).astype(BF16)
                dws_ref[g * CHUNK:(g + 1) * CHUNK, :] += _dot_nt(masked, vn_c)
            dvn_parts.append(_group_select(_dot(wsts_ref[...], dsg_c.astype(BF16)), grp[0:CHUNK]))
        dvn = jnp.concatenate(dvn_parts, axis=0)
        s256_ref[3:4, :] += _colsum(dvn * f["vn_hat"])
        s256_ref[4:5, :] += _colsum(dvn)
        d_cv = _layer_norm_bwd(dvn * vecs[3:4, :], f["vn_hat"], f["rs_c"])
        lane, lo = f["lane"], f["lo"]
        att = f["att"]
        for b in range(2):
            da = datt[:, b * LANES:(b + 1) * LANES]
            prod = da * att[:, b * LANES:(b + 1) * LANES]
            for hh in range(2):
                h = 2 * b + hh
                lse = _rowsum(jnp.where(lane == HEAD_DIM, oe_ref[h], 0.0))
                delta = _rowsum(jnp.where(lo, prod, 0.0) if hh == 0 else jnp.where(lo, 0.0, prod))
                dah = da if hh == 0 else pltpu.roll(da, HEAD_DIM, 1)
                doe_ref[h] = jnp.where(lo, dah, jnp.where(lane == HEAD_DIM, delta, jnp.where(lane == HEAD_DIM + 1, lse, 0.0)))

        dpr_ref[...] = jnp.concatenate([d_ab, d_gate[0], d_gate[1], d_cu, d_cv, d_gate[2], d_gate[3]], axis=1).astype(BF16)

        @pl.when(i == n_tiles - 1)
        def _():
            acc = dbsm[...]
            lane128 = _lane(CHUNK)
            out = jnp.zeros((CHUNK, LANES), F32)
            for g in range(N_SPATIAL_GROUPS):
                col = _rowsum(jnp.where(grp[0:CHUNK] == g, acc, 0.0))
                out = out + jnp.where(lane128 == g, col, 0.0)
            dbs_ref[...] = out

    return _pc(
        body, name="mix_out_bwd", grid=(n_tiles,),
        out_shape=(jax.ShapeDtypeStruct((t, W_R), BF16),
                   jax.ShapeDtypeStruct((t, GROUP_W), F32), jax.ShapeDtypeStruct((t, GROUP_W), F32),
                   jax.ShapeDtypeStruct((N_Q_HEADS, t, LANES), F32),
                   jax.ShapeDtypeStruct((D_MODEL, D_MODEL), F32),
                   jax.ShapeDtypeStruct((8, D_MODEL), F32),
                   jax.ShapeDtypeStruct((8, GROUP_W), F32),
                   jax.ShapeDtypeStruct((N_SPATIAL_GROUPS * CHUNK, CHUNK), F32),
                   jax.ShapeDtypeStruct((CHUNK, LANES), F32)),
        in_specs=[_rows(D_MODEL), _rows(D_MODEL), _rows(W_R), _rows(GROUP_W), _rows(GROUP_W), _heads(N_Q_HEADS, LANES),
                  _full((8, D_MODEL)), _full((1, D_MODEL)), _full((D_MODEL, D_MODEL)), _full((8, GROUP_W)),
                  _full((N_SPATIAL_GROUPS * CHUNK, CHUNK)), _full((N_SPATIAL_GROUPS * CHUNK, CHUNK)), _full((CHUNK, GROUP_W))],
        out_specs=(_rows(W_R), _rows(GROUP_W), _rows(GROUP_W), _heads(N_Q_HEADS, LANES),
                   _full((D_MODEL, D_MODEL)), _full((8, D_MODEL)), _full((8, GROUP_W)),
                   _full((N_SPATIAL_GROUPS * CHUNK, CHUNK)), _full((CHUNK, LANES))),
        scratch_shapes=[pltpu.VMEM((CHUNK, GROUP_W), F32)],
        compiler_params=_cparams(),
    )(dxo, y, pr, ca, z2, oe, modv, g_post, w_out, vecs, wss, wsts, bsm)


def _conv_bwd(pc, g_a, g_b, conv_a, conv_b):
    t = pc.shape[0]
    n_tiles = t // TM
    pc_prev, pc_next = _halo_specs(W_C, t)
    g_prev, g_next = _halo_specs(GROUP_W, t)

    def body(pc_ref, pp_ref, pn_ref, ga_ref, gap_ref, gan_ref, gb_ref, gbp_ref, gbn_ref, cva_ref, cvb_ref,
             dpc_ref, dca_ref, dcb_ref, uext, zext, gaext, gbext):
        i = pl.program_id(0)

        @pl.when(i == 0)
        def _():
            dca_ref[...] = jnp.zeros_like(dca_ref)
            dcb_ref[...] = jnp.zeros_like(dcb_ref)

        pv, nv = _halo_valid(i, n_tiles)
        pc_ = pc_ref[...]
        u, z = _conv_inputs(pc_)
        up, zp = _conv_inputs(pp_ref[...])
        un, zn_ = _conv_inputs(pn_ref[...])
        _fill_ext(uext, up * pv, u, un * nv)
        _fill_ext(zext, zp * pv, z, zn_ * nv)
        ga = ga_ref[...]
        gb = gb_ref[...]
        _fill_ext(gaext, gap_ref[...] * pv, ga, gan_ref[...] * nv)
        _fill_ext(gbext, gbp_ref[...] * pv, gb, gbn_ref[...] * nv)

        du = cva_ref[0:1, :] * gaext[pl.ds(HALO + 1, TM), :]
        dca_ref[0:1, :] += _colsum(ga * uext[pl.ds(HALO - 1, TM), :])
        for kk in range(1, SHORT_CONV_K):
            du = du + cva_ref[kk:kk + 1, :] * gaext[pl.ds(HALO + 1 - kk, TM), :]
            dca_ref[kk:kk + 1, :] += _colsum(ga * uext[pl.ds(HALO - 1 + kk, TM), :])
        dz = _taps31(gbext, cvb_ref, True)
        for r0 in range(0, TM, CONV_ROWS):
            gb_rows = gb_ref[pl.ds(r0, CONV_ROWS), :]
            for b in range(8):
                zb = zext[pl.ds(r0 + b, CONV_ROWS + 24), :]
                for a in range(4):
                    kk = 8 * a + b - 1
                    if 0 <= kk < CONFORMER_K:
                        dcb_ref[kk:kk + 1, :] += _colsum(gb_rows * zb[8 * a:8 * a + CONV_ROWS])

        a_c, a_h = pc_[:, 0:GROUP_W], pc_[:, GROUP_W:2 * GROUP_W]
        glu_a, glu_g = pc_[:, 2 * GROUP_W:3 * GROUP_W], pc_[:, 3 * GROUP_W:4 * GROUP_W]
        sg = _sigmoid(glu_g)
        dpc_ref[...] = jnp.concatenate([du * a_h, du * a_c, dz * sg, dz * glu_a * sg * (1.0 - sg)], axis=1).astype(BF16)

    ext = pltpu.VMEM((TM + 2 * HALO, GROUP_W), F32)
    return _pc(
        body, name="conv_bwd", grid=(n_tiles,),
        out_shape=(jax.ShapeDtypeStruct((t, W_C), BF16), jax.ShapeDtypeStruct((8, GROUP_W), F32), jax.ShapeDtypeStruct((32, GROUP_W), F32)),
        in_specs=[_rows(W_C), pc_prev, pc_next, _rows(GROUP_W), g_prev, g_next, _rows(GROUP_W), g_prev, g_next,
                  _full((8, GROUP_W)), _full((32, GROUP_W))],
        out_specs=(_rows(W_C), _full((8, GROUP_W)), _full((32, GROUP_W))),
        scratch_shapes=[ext, ext, ext, ext],
        compiler_params=_cparams(),
    )(pc, pc, pc, g_a, g_a, g_a, g_b, g_b, g_b, conv_a, conv_b)


def _attention_bwd(q, k, v, doe, slabs=None):
    t = q.shape[1]
    tk = _kv_chunk(t)
    n_chunks = (t - CTX_LEN) // tk
    n_tiles = t // TM
    n_sl = 0 if slabs is None else len(slabs)

    def body(q_ref, do_ref, k_ref, v_ref, *rest):
        i = pl.program_id(0)
        dq_ref, dk_hbm, dv_hbm = rest[n_sl:n_sl + 3]
        dk_acc, dv_acc = rest[2 * n_sl + 3:2 * n_sl + 5]
        pairs = tuple(zip(rest[:n_sl], rest[n_sl + 3:2 * n_sl + 3]))
        sems = rest[2 * n_sl + 5:]

        @pl.when(i == 0)
        def _():
            dk_acc[...] = jnp.zeros_like(dk_acc)
            dv_acc[...] = jnp.zeros_like(dv_acc)
            if slabs is not None:
                _scatter_phase(0, pairs, *sems)

        lane = _lane(GQA * TM)
        lo = lane < HEAD_DIM
        qs, dos, deltas, lses = [], [], [], []
        for g in range(N_KV_HEADS):
            qs.append(jnp.concatenate([q_ref[GQA * g + hh] for hh in range(GQA)], axis=0))
            dog = jnp.concatenate([do_ref[GQA * g + hh] for hh in range(GQA)], axis=0)
            deltas.append(_rowsum(jnp.where(lane == HEAD_DIM, dog, 0.0)))
            lses.append(_rowsum(jnp.where(lane == HEAD_DIM + 1, dog, 0.0)))
            dos.append(jnp.where(lo, dog, 0.0).astype(BF16))

        def step(st, size, dqs):
            out = []
            for g in range(N_KV_HEADS):
                kc = k_ref[g, pl.ds(st, size), :]
                vc = v_ref[g, pl.ds(st, size), :]
                p = jnp.exp(_dot_nt(qs[g], kc) - lses[g])
                ds_ = (p * (_dot_nt(dos[g], vc) - deltas[g])).astype(BF16)
                dk_acc[g, pl.ds(st, size), :] += _dot_tn(ds_, qs[g])
                dv_acc[g, pl.ds(st, size), :] += _dot_tn(p.astype(BF16), dos[g])
                out.append(dqs[g] + _dot(ds_, kc))
            return tuple(out)

        zero = tuple(jnp.zeros((GQA * TM, LANES), F32) for _ in range(N_KV_HEADS))

        def finish(dqs):
            for g in range(N_KV_HEADS):
                for hh in range(GQA):
                    dq_ref[GQA * g + hh] = dqs[g][hh * TM:(hh + 1) * TM]

        @pl.when(i < N_CTX_TILES)
        def _():
            finish(step(0, CTX_LEN, zero))

        @pl.when(i >= N_CTX_TILES)
        def _():
            finish(lax.fori_loop(0, n_chunks, lambda j, acc: step(pl.multiple_of(CTX_LEN + j * tk, 256), tk, acc),
                                 step(0, CTX_LEN, zero)))

        @pl.when(i == n_tiles - 1)
        def _():
            pltpu.sync_copy(dk_acc, dk_hbm)
            pltpu.sync_copy(dv_acc, dv_hbm)
            if slabs is not None:
                _scatter_phase(1, pairs, *sems)

    kv_shape = jax.ShapeDtypeStruct((N_KV_HEADS, t, LANES), F32)
    hbm = pl.BlockSpec(memory_space=pl.ANY)
    extra = () if slabs is None else tuple(slabs)
    outs = _pc(
        body, name="attention_bwd" if slabs is None else "attention_bwd_scatter", grid=(n_tiles,),
        out_shape=(jax.ShapeDtypeStruct((N_Q_HEADS, t, LANES), F32), kv_shape, kv_shape)
        + tuple(jax.ShapeDtypeStruct(a.shape, a.dtype) for a in extra),
        in_specs=[_heads(N_Q_HEADS, LANES), _heads(N_Q_HEADS, LANES),
                  _full((N_KV_HEADS, t, LANES)), _full((N_KV_HEADS, t, LANES))] + [hbm] * n_sl,
        out_specs=(_heads(N_Q_HEADS, LANES), hbm, hbm) + (hbm,) * n_sl,
        scratch_shapes=[pltpu.VMEM((N_KV_HEADS, t, LANES), F32), pltpu.VMEM((N_KV_HEADS, t, LANES), F32)]
        + (_comm_scratch(SCATTER_SEMS) if slabs is not None else []),
        compiler_params=_cparams(),
    )(q, doe, k, v, *extra)
    return outs[0], outs[1], outs[2], tuple(outs[3:])


def _in_proj_bwd(dpc, dpr, dq, dk, dv, pq, qk_gain, cos_t, sin_t, w_c, w_r, w_q, xt, dxo, modv, g_pre):
    t = xt.shape[0]

    def body(dpc_ref, dpr_ref, dq_ref, dk_ref, dv_ref, pq_ref, gain_ref, cos_ref, sin_ref, wc_ref, wr_ref, wq_ref,
             x_ref, dxo_ref, mod_ref, g_ref, dx_ref, dpq_ref, acc_ref, dgain_ref):
        i = pl.program_id(0)

        @pl.when(i == 0)
        def _():
            acc_ref[...] = jnp.zeros_like(acc_ref)
            dgain_ref[...] = jnp.zeros_like(dgain_ref)

        lane = _lane(TM)
        lo = lane < HEAD_DIM
        lo16 = (lane & 31) < 16
        g = g_ref[...]
        for jj in range(SUB):
            rows = pl.ds(jj * TM, TM)
            is_ctx = i * SUB + jj < N_CTX_TILES
            cos = cos_ref[rows, :]
            sin = sin_ref[rows, :]
            outs = []
            for b in range(3):
                src = dq_ref if b < 2 else dk_ref
                base = 2 * b if b < 2 else 0
                drot = src[base, rows, :] + pltpu.roll(src[base + 1, rows, :], HEAD_DIM, 1)
                if b < 2:
                    drot = drot * ATTN_SCALE
                dxg = drot * cos + _swap16(drot * sin, lo16)
                xh, r = _head_norm(pq_ref[rows, b * LANES:(b + 1) * LANES], lo)
                row = 0 if b < 2 else 1
                dgain_ref[row:row + 1, :] += _colsum(dxg * xh)
                dxh = dxg * gain_ref[row:row + 1, :]
                outs.append(r * (dxh - xh * (_pair_sums(dxh * xh, lo) * (1.0 / HEAD_DIM))))
            outs.append(dv_ref[0, rows, :] + pltpu.roll(dv_ref[1, rows, :], HEAD_DIM, 1))
            dpq = jnp.concatenate(outs, axis=1).astype(BF16)
            dpq_ref[rows, :] = dpq

            dh = _dot_nt(dpc_ref[rows, :], wc_ref[...]) + _dot_nt(dpr_ref[rows, :], wr_ref[...]) + _dot_nt(dpq, wq_ref[...])
            x = x_ref[rows, :]
            r = lax.rsqrt(jnp.mean(x * x, axis=1, keepdims=True) + RMS_EPS)
            xn = x * r
            sc = jnp.where(is_ctx, mod_ref[1:2, :], mod_ref[4:5, :])
            dsh = _colsum(dh)
            dsc = _colsum(dh * (xn * g))
            acc_ref[0:1, :] += jnp.where(is_ctx, dsh, 0.0)
            acc_ref[1:2, :] += jnp.where(is_ctx, dsc, 0.0)
            acc_ref[2:3, :] += jnp.where(is_ctx, 0.0, dsh)
            acc_ref[3:4, :] += jnp.where(is_ctx, 0.0, dsc)
            dxg = dh * (1.0 + sc)
            acc_ref[4:5, :] += _colsum(dxg * xn)
            dxn = dxg * g
            dx_ref[rows, :] = r * (dxn - xn * jnp.mean(dxn * xn, axis=1, keepdims=True)) + dxo_ref[rows, :]

    return _pc(
        body, name="in_proj_bwd", grid=(t // TB,),
        out_shape=(jax.ShapeDtypeStruct((t, D_MODEL), F32), jax.ShapeDtypeStruct((t, W_Q), BF16),
                   jax.ShapeDtypeStruct((8, D_MODEL), F32), jax.ShapeDtypeStruct((8, LANES), F32)),
        in_specs=[_rows(W_C, TB), _rows(W_R, TB),
                  _heads(N_Q_HEADS, LANES, TB), _heads(N_KV_HEADS, LANES, TB), _heads(N_KV_HEADS, LANES, TB), _rows(W_Q, TB),
                  _const((8, LANES)), _rows(LANES, TB), _rows(LANES, TB),
                  _const((D_MODEL, W_C)), _const((D_MODEL, W_R)), _const((D_MODEL, W_Q)),
                  _rows(D_MODEL, TB), _rows(D_MODEL, TB), _const((8, D_MODEL)), _const((1, D_MODEL))],
        out_specs=(_rows(D_MODEL, TB), _rows(W_Q, TB), _full((8, D_MODEL)), _full((8, LANES))),
        compiler_params=_cparams(),
    )(dpc, dpr, dq, dk, dv, pq, qk_gain, cos_t, sin_t, w_c, w_r, w_q, xt, dxo, modv, g_pre)


def _in_proj_wgrad(h, dpc, dpr, dpq):
    t = h.shape[0]

    def body(h_ref, dpc_ref, dpr_ref, dpq_ref, gc_ref, gr_ref, gq_ref):
        @pl.when(pl.program_id(0) == 0)
        def _():
            gc_ref[...] = jnp.zeros_like(gc_ref)
            gr_ref[...] = jnp.zeros_like(gr_ref)
            gq_ref[...] = jnp.zeros_like(gq_ref)

        hb = h_ref[...]
        gc_ref[...] += _dot_tn(hb, dpc_ref[...])
        gr_ref[...] += _dot_tn(hb, dpr_ref[...])
        gq_ref[...] += _dot_tn(hb, dpq_ref[...])

    return _pc(
        body, name="in_proj_wgrad", grid=(t // TB,),
        out_shape=(jax.ShapeDtypeStruct((D_MODEL, W_C), F32), jax.ShapeDtypeStruct((D_MODEL, W_R), F32),
                   jax.ShapeDtypeStruct((D_MODEL, W_Q), F32)),
        in_specs=[_rows(D_MODEL, TB), _rows(W_C, TB), _rows(W_R, TB), _rows(W_Q, TB)],
        out_specs=(_full((D_MODEL, W_C)), _full((D_MODEL, W_R)), _full((D_MODEL, W_Q))),
        compiler_params=_cparams(),
    )(h, dpc, dpr, dpq)


def _sum_slabs(slabs, tile_rows):
    n, r, c = slabs.shape

    def body(s_ref, o_ref):
        acc = s_ref[0].astype(F32)
        for k in range(1, n):
            acc = acc + s_ref[k].astype(F32)
        o_ref[...] = acc

    return _pc(
        body, name="sum_slabs", grid=(r // tile_rows,),
        out_shape=jax.ShapeDtypeStruct((r, c), F32),
        in_specs=[pl.BlockSpec((n, tile_rows, c), lambda i: (0, i, 0))],
        out_specs=pl.BlockSpec((tile_rows, c), lambda i: (i, 0)),
        compiler_params=_cparams(),
    )(slabs)


def _sum_layer_slabs(layers, tile_rows):
    nl = len(layers)
    n, r, c = layers[0].shape
    per = r // tile_rows

    def body(*refs):
        o_ref = refs[nl]
        for l in range(nl):
            @pl.when(pl.program_id(0) // per == l)
            def _(l=l):
                acc = refs[l][0].astype(F32)
                for k in range(1, n):
                    acc = acc + refs[l][k].astype(F32)
                o_ref[...] = acc

    def spec(l):
        return pl.BlockSpec((n, tile_rows, c), lambda i: (0, jnp.clip(i - l * per, 0, per - 1), 0))

    return _pc(
        body, name="sum_layer_slabs", grid=(nl * per,),
        out_shape=jax.ShapeDtypeStruct((nl * r, c), F32),
        in_specs=[spec(l) for l in range(nl)],
        out_specs=pl.BlockSpec((tile_rows, c), lambda i: (i, 0)),
        compiler_params=_cparams(),
    )(*layers)


def _adamw(grads, w, m, v, tile_rows):
    r, c = w.shape
    n_g = len(grads)

    def body(*refs):
        g = refs[0][...]
        for k in range(1, n_g):
            g = g + refs[k][...]
        w_ref, m_ref, v_ref, g_out, d_out, m_out, v_out = refs[n_g:]
        m_new = ADAM_B1 * m_ref[...] + (1.0 - ADAM_B1) * g
        v_new = ADAM_B2 * v_ref[...] + (1.0 - ADAM_B2) * (g * g)
        m_hat = m_new / (1.0 - ADAM_B1 ** ADAM_STEP)
        v_hat = v_new / (1.0 - ADAM_B2 ** ADAM_STEP)
        g_out[...] = g
        d_out[...] = -ADAM_LR * (m_hat / (jnp.sqrt(v_hat) + ADAM_EPS) + ADAM_WD * w_ref[...])
        m_out[...] = m_new
        v_out[...] = v_new

    spec = pl.BlockSpec((tile_rows, c), lambda i: (i, 0))
    shape = jax.ShapeDtypeStruct((r, c), F32)
    return _pc(
        body, name="adamw", grid=(r // tile_rows,),
        out_shape=(shape,) * 4, in_specs=[spec] * (n_g + 3), out_specs=(spec,) * 4,
        compiler_params=_cparams(),
    )(*grads, w, m, v)


def _rope_tables(s_lat):
    pos = jnp.arange(s_lat)
    pos_row = (pos // GRID_W).astype(F32)
    pos_col = (pos % GRID_W).astype(F32)
    axis_dim = HEAD_DIM // 2
    inv_freq = 1.0 / (ROPE_THETA ** (jnp.arange(0, axis_dim, 2, dtype=F32) / axis_dim))
    d = np.arange(LANES) % HEAD_DIM
    on_rows = (d // axis_dim) == 0
    freq = d % (axis_dim // 2)
    sign = np.where((d % axis_dim) < axis_dim // 2, -1.0, 1.0).astype(np.float32)
    ang = jnp.where(on_rows[None, :], pos_row[:, None], pos_col[:, None]) * inv_freq[freq][None, :]
    cos = jnp.concatenate([jnp.ones((CTX_LEN, LANES), F32), jnp.cos(ang)], axis=0)
    sin = jnp.concatenate([jnp.zeros((CTX_LEN, LANES), F32), jnp.sin(ang) * sign[None, :]], axis=0)
    return cos, sin


def _pad_rows(a, rows):
    return jnp.concatenate([a, jnp.zeros((rows - a.shape[0],) + a.shape[1:], a.dtype)], axis=0)


_SMALL = ("c_ctx", "b_mod", "g_pre", "g_post", "conv_a", "conv_b", "conv_b_bias", "conf_ln_g", "conf_ln_b",
          "sgu_ln_g", "sgu_ln_b", "w_s", "b_s", "q_gain", "k_gain")


def _pack(arrays):
    flat = jnp.concatenate([a.reshape(-1) for a in arrays])
    rows = -(-flat.shape[0] // (8 * LANES)) * 8
    return _pad_rows(flat.reshape(-1, 1), rows * LANES).reshape(rows, LANES)


def _unpack(packed, shapes):
    flat = packed.reshape(-1)
    out, off = [], 0
    for s in shapes:
        n = int(np.prod(s))
        out.append(flat[off:off + n].reshape(s))
        off += n
    return out


def kernel(x, c, ctx, c_ctx, w_mod, b_mod, g_pre, g_post, w_in, w_out, conv_a, conv_b, conv_b_bias, conf_ln_g, conf_ln_b, sgu_ln_g, sgu_ln_b, w_s, b_s, q_gain, k_gain, loss_target, m_c_ctx, m_w_mod, m_b_mod, m_g_pre, m_g_post, m_w_in, m_w_out, m_conv_a, m_conv_b, m_conv_b_bias, m_conf_ln_g, m_conf_ln_b, m_sgu_ln_g, m_sgu_ln_b, m_w_s, m_b_s, m_q_gain, m_k_gain, v_c_ctx, v_w_mod, v_b_mod, v_g_pre, v_g_post, v_w_in, v_w_out, v_conv_a, v_conv_b, v_conv_b_bias, v_conf_ln_g, v_conf_ln_b, v_sgu_ln_g, v_sgu_ln_b, v_w_s, v_b_s, v_q_gain, v_k_gain):
    weights = dict(c_ctx=c_ctx, w_mod=w_mod, b_mod=b_mod, g_pre=g_pre, g_post=g_post, w_in=w_in, w_out=w_out, conv_a=conv_a,
                   conv_b=conv_b, conv_b_bias=conv_b_bias, conf_ln_g=conf_ln_g, conf_ln_b=conf_ln_b, sgu_ln_g=sgu_ln_g,
                   sgu_ln_b=sgu_ln_b, w_s=w_s, b_s=b_s, q_gain=q_gain, k_gain=k_gain)
    m_in = dict(c_ctx=m_c_ctx, w_mod=m_w_mod, b_mod=m_b_mod, g_pre=m_g_pre, g_post=m_g_post, w_in=m_w_in, w_out=m_w_out,
                conv_a=m_conv_a, conv_b=m_conv_b, conv_b_bias=m_conv_b_bias, conf_ln_g=m_conf_ln_g, conf_ln_b=m_conf_ln_b,
                sgu_ln_g=m_sgu_ln_g, sgu_ln_b=m_sgu_ln_b, w_s=m_w_s, b_s=m_b_s, q_gain=m_q_gain, k_gain=m_k_gain)
    v_in = dict(c_ctx=v_c_ctx, w_mod=v_w_mod, b_mod=v_b_mod, g_pre=v_g_pre, g_post=v_g_post, w_in=v_w_in, w_out=v_w_out,
                conv_a=v_conv_a, conv_b=v_conv_b, conv_b_bias=v_conv_b_bias, conf_ln_g=v_conf_ln_g, conf_ln_b=v_conf_ln_b,
                sgu_ln_g=v_sgu_ln_g, sgu_ln_b=v_sgu_ln_b, w_s=v_w_s, b_s=v_b_s, q_gain=v_q_gain, k_gain=v_k_gain)
    order = ("c_ctx", "w_mod", "b_mod", "g_pre", "g_post", "w_in", "w_out", "conv_a", "conv_b", "conv_b_bias", "conf_ln_g",
             "conf_ln_b", "sgu_ln_g", "sgu_ln_b", "w_s", "b_s", "q_gain", "k_gain")

    s_lat = x.shape[1]
    ax, ay, ac = lax.axis_index("x"), lax.axis_index("y"), lax.axis_index("c")
    chip = 2 * ax + ay
    example = 4 * ax + 2 * ay + ac

    c_rows = _all_gather_rows(_pad_rows(c, 8))[::8]
    c16 = _pad_rows(jnp.concatenate([c_rows, c_ctx[None, :]], axis=0), 16)
    b_mod_shard = lax.dynamic_slice_in_dim(b_mod, chip * SHARD_MOD, SHARD_MOD, axis=1)[:, None, :]
    silu_c, mod_shard = _mod_forward(c16, w_mod, b_mod_shard)
    mod_all = _all_gather_rows(mod_shard.reshape(DEPTH * 16, SHARD_MOD)).reshape(8, DEPTH, 16, SHARD_MOD)
    mod_full = jnp.transpose(mod_all[::2], (1, 2, 0, 3)).reshape(DEPTH, 16, 3 * D_MODEL)
    mod_lat = lax.dynamic_index_in_dim(mod_full, example, axis=1, keepdims=False).reshape(DEPTH, 3, D_MODEL)
    mod_ctx = mod_full[:, 8].reshape(DEPTH, 3, D_MODEL)
    modv = jnp.concatenate([mod_ctx, mod_lat, jnp.zeros((DEPTH, 2, D_MODEL), F32)], axis=1)

    wi_b, wo_b = w_in.astype(BF16), w_out.astype(BF16)

    def regroup(gathered):
        wi_all, wo_all = gathered
        wi_full = jnp.concatenate([wi_all[k] for k in range(N_CHIPS)], axis=-1)
        wo_l = jnp.concatenate([wo_all[k] for k in range(N_CHIPS)], axis=0)
        wc_l = jnp.concatenate([wi_full[:, 256:768], wi_full[:, 1024:1536]], axis=-1)
        wr_l = jnp.concatenate([wi_full[:, 0:256], wi_full[:, 768:1024], wi_full[:, 1536:2560], wi_full[:, 3072:3328]], axis=-1)
        return wc_l, wr_l, wi_full[:, 2560:3072], wo_l

    w_c, w_r, w_q, wo_full = [None] * DEPTH, [None] * DEPTH, [None] * DEPTH, [None] * DEPTH
    w_c[0], w_r[0], w_q[0], wo_full[0] = regroup(_gather_weights(wi_b[0], wo_b[0]))

    cos_t, sin_t = _rope_tables(s_lat)
    conv_a_full = jnp.zeros((DEPTH, 8, GROUP_W), F32)
    conv_b_full = jnp.zeros((DEPTH, 32, GROUP_W), F32)
    conv_small = jnp.concatenate([conv_a.reshape(DEPTH * SHORT_CONV_K, -1), conv_b.reshape(DEPTH * CONFORMER_K, -1)], axis=0)
    n_cs = conv_small.shape[0]
    conv_rows = -(-n_cs // 8) * 8
    conv_all = _all_gather_rows(_pad_rows(conv_small, conv_rows)).reshape(8, conv_rows, -1)[::2]
    conv_all = jnp.transpose(conv_all, (1, 0, 2)).reshape(conv_rows, GROUP_W)
    conv_a_full = conv_a_full.at[:, :SHORT_CONV_K].set(conv_all[:DEPTH * SHORT_CONV_K].reshape(DEPTH, SHORT_CONV_K, GROUP_W))
    conv_b_full = conv_b_full.at[:, :CONFORMER_K].set(
        conv_all[DEPTH * SHORT_CONV_K:n_cs].reshape(DEPTH, CONFORMER_K, GROUP_W))

    vecs = jnp.stack([conv_b_bias, conf_ln_g, conf_ln_b, sgu_ln_g, sgu_ln_b] + [jnp.zeros_like(conv_b_bias)] * 3, axis=1)
    wss = w_s.reshape(DEPTH, N_SPATIAL_GROUPS * CHUNK, CHUNK).astype(BF16)
    wsts = jnp.swapaxes(w_s, 2, 3).reshape(DEPTH, N_SPATIAL_GROUPS * CHUNK, CHUNK).astype(BF16)
    bsm = jnp.repeat(jnp.swapaxes(b_s, 1, 2), HEAD_DIM, axis=2)
    qk_gain = jnp.concatenate([jnp.tile(q_gain, (1, 2))[:, None, :], jnp.tile(k_gain, (1, 2))[:, None, :],
                               jnp.zeros((DEPTH, 6, LANES), F32)], axis=1)

    xt = jnp.concatenate([ctx[0], x[0]], axis=0)
    saved = []
    for l in range(DEPTH):
        h, pc, pr, pq, q, k, v = _in_proj(xt, modv[l], g_pre[l][None, :], w_c[l], w_r[l], w_q[l], qk_gain[l], cos_t, sin_t)
        if l + 1 < DEPTH:
            oe, gathered = _attention_fwd(q, k, v, (wi_b[l + 1], wo_b[l + 1]))
            w_c[l + 1], w_r[l + 1], w_q[l + 1], wo_full[l + 1] = regroup(gathered)
        else:
            oe, _ = _attention_fwd(q, k, v)
        mixed = _mix_out(pc, pr, oe, xt, modv[l], g_post[l][None, :], wo_full[l], conv_a_full[l], conv_b_full[l],
                         vecs[l], wss[l], bsm[l], loss_target[0] if l + 1 == DEPTH else None)
        x_new, y, ca, z2 = mixed[:4]
        saved.append(dict(x=xt, h=h, pc=pc, pr=pr, pq=pq, q=q, k=k, v=v, oe=oe, y=y, ca=ca, z2=z2))
        xt = x_new
    dxo = xt
    loss = lax.psum(mixed[4][0, 0], ("x", "y", "c"))

    g_small = {n: [None] * DEPTH for n in _SMALL}
    d_mod, landed = [None] * DEPTH, [None] * DEPTH
    slabs = None
    for l in reversed(range(DEPTH)):
        s = saved[l]
        dpr, g_a, g_b, doe, gw_o, pvec, s256, dws, dbs = _mix_out_bwd(
            dxo, s["y"], s["pr"], s["ca"], s["z2"], s["oe"], modv[l], g_post[l][None, :], wo_full[l], vecs[l], wss[l], wsts[l], bsm[l])
        dpc, dca, dcb = _conv_bwd(s["pc"], g_a, g_b, conv_a_full[l], conv_b_full[l])
        dq, dk, dv, got = _attention_bwd(s["q"], s["k"], s["v"], doe, slabs)
        if slabs is not None:
            landed[l + 1] = got
        dxo, dpq, acc, dgain = _in_proj_bwd(dpc, dpr, dq, dk, dv, s["pq"], qk_gain[l], cos_t, sin_t, w_c[l], w_r[l], w_q[l],
                                            s["x"], dxo, modv[l], g_pre[l][None, :])
        gw_c, gw_r, gw_q = _in_proj_wgrad(s["h"], dpc, dpr, dpq)
        gw_in = jnp.concatenate([gw_r[:, 0:256], gw_c[:, 0:512], gw_r[:, 256:512], gw_c[:, 512:1024],
                                 gw_r[:, 512:1536], gw_q, gw_r[:, 1536:1792]], axis=-1)
        slabs = (jnp.transpose(gw_in.reshape(D_MODEL, N_CHIPS, SHARD_IN), (1, 0, 2)).astype(BF16),
                 gw_o.reshape(N_CHIPS, SHARD_OUT, D_MODEL).astype(BF16))
        d_mod[l] = jnp.stack([jnp.concatenate([acc[2], acc[3], pvec[1]]), jnp.concatenate([acc[0], acc[1], pvec[0]])])
        g_small["g_pre"][l] = acc[4]
        g_small["g_post"][l] = pvec[2]
        g_small["conv_a"][l] = dca[:SHORT_CONV_K]
        g_small["conv_b"][l] = dcb[:CONFORMER_K]
        g_small["conv_b_bias"][l] = s256[0]
        g_small["conf_ln_g"][l] = s256[1]
        g_small["conf_ln_b"][l] = s256[2]
        g_small["sgu_ln_g"][l] = s256[3]
        g_small["sgu_ln_b"][l] = s256[4]
        g_small["w_s"][l] = dws.reshape(N_SPATIAL_GROUPS, CHUNK, CHUNK)
        g_small["b_s"][l] = jnp.transpose(dbs[:, :N_SPATIAL_GROUPS])
        g_small["q_gain"][l] = dgain[0, :HEAD_DIM] + dgain[0, HEAD_DIM:]
        g_small["k_gain"][l] = dgain[1, :HEAD_DIM] + dgain[1, HEAD_DIM:]
    grad_x = dxo[CTX_LEN:][None]

    d_mod_all = _all_gather_rows(jnp.stack(d_mod).reshape(DEPTH * 2, 3 * D_MODEL)).reshape(8, DEPTH, 2, 3 * D_MODEL)
    d_lat = jnp.transpose(d_mod_all[:, :, 0], (1, 0, 2))
    d_ctx = jnp.transpose(d_mod_all[:, :, 1], (1, 0, 2))
    cols = lambda a: lax.dynamic_slice_in_dim(a.reshape(DEPTH, 8, N_CHIPS, SHARD_MOD), chip, 1, axis=2)[:, :, 0]
    silu_t = jnp.transpose(silu_c)
    s_t = jnp.concatenate([silu_t[:, 0:8], jnp.tile(silu_t[:, 8:9], (1, 8)), jnp.zeros((D_MODEL, LANES - 16), F32)], axis=1)
    g_rows = jnp.concatenate([cols(d_lat), cols(d_ctx), jnp.zeros((DEPTH, LANES - 16, SHARD_MOD), F32)], axis=1)
    g_w_mod, g_b_mod, c_ctx_part = _mod_backward(s_t, g_rows, cols(d_ctx), jnp.concatenate([d_lat, d_ctx], axis=1),
                                                 w_mod, c_ctx[:, None])

    for n in _SMALL:
        if n not in ("c_ctx", "b_mod"):
            g_small[n] = jnp.stack(g_small[n])
    small_parts = [0.5 * c_ctx_part[:, 0]] + [g_small[n] for n in _SMALL[2:]]
    packed = _pack(small_parts)
    gathered = _all_gather_rows(packed).reshape(8, packed.shape[0], LANES)
    small_sum = _sum_slabs(gathered, packed.shape[0])
    small_g = dict(zip(("c_ctx",) + _SMALL[2:], _unpack(small_sum, [p.shape for p in small_parts])))
    small_g["b_mod"] = g_b_mod[:, 0]
    ch64 = GROUP_W // N_CHIPS
    for n in ("conv_a", "conv_b"):
        small_g[n] = lax.dynamic_slice_in_dim(small_g[n], chip * ch64, ch64, axis=2)
    sw = _pack([weights[n] for n in _SMALL])
    sm = _pack([m_in[n] for n in _SMALL])
    sv = _pack([v_in[n] for n in _SMALL])
    sg = _pack([small_g[n] for n in _SMALL])
    shapes = [weights[n].shape for n in _SMALL]
    small_out = [dict(zip(_SMALL, _unpack(o, shapes))) for o in _adamw([sg], sw, sm, sv, sg.shape[0])]

    landed[0] = _scatter_slabs(*slabs)
    sum_in = _sum_layer_slabs([landed[l][0] for l in range(DEPTH)], 512)
    sum_out = _sum_layer_slabs([landed[l][1] for l in range(DEPTH)], 256)
    sib_in, sib_out = _swap_with_sibling(sum_in, sum_out)

    big = {}
    flat = lambda a: a.reshape(-1, a.shape[-1])
    for n, grads, rows in (("w_in", [sum_in, sib_in], 512), ("w_out", [sum_out, sib_out], 256), ("w_mod", [flat(g_w_mod)], 512)):
        outs = _adamw(grads, flat(weights[n]), flat(m_in[n]), flat(v_in[n]), rows)
        big[n] = [o.reshape(weights[n].shape) for o in outs]

    def leaf(n, j):
        return big[n][j] if n in big else small_out[j][n]

    return (loss, grad_x, *[leaf(n, 0) for n in order], *[leaf(n, 1) for n in order],
            *[leaf(n, 2) for n in order], *[leaf(n, 3) for n in order])
```

```python
import functools

import numpy as np
import jax
import jax.numpy as jnp
from jax import lax
from jax.experimental import pallas as pl
from jax.experimental.pallas import tpu as pltpu

F32 = jnp.float32
BF16 = jnp.bfloat16
MESH = pl.DeviceIdType.MESH

D_MODEL = 1024
DEPTH = 4
GRID_W = 64
CTX_LEN = 256
GROUP_W = 256
HEAD_DIM = 64
N_Q_HEADS = 4
N_KV_HEADS = 2
GQA = N_Q_HEADS // N_KV_HEADS
ROPE_THETA = 10000.0
ATTN_SCALE = HEAD_DIM ** -0.5
SHORT_CONV_K = 3
CONFORMER_K = 31
CHUNK = 128
N_SPATIAL_GROUPS = 4
RMS_EPS = 1e-6
LN_EPS = 1e-5
ADAM_LR = 0.001
ADAM_B1 = 0.9
ADAM_B2 = 0.999
ADAM_EPS = 1e-08
ADAM_WD = 0.01
ADAM_STEP = 10

LANES = 128
HALO = 16
CONV_ROWS = 64
TM = 256
N_CTX_TILES = CTX_LEN // TM
SUB = 3
TB = SUB * TM
W_C = 1024
W_R = 1792
W_Q = 512
PROJ_W = W_C + W_R + W_Q
N_CHIPS = 4
SHARD_IN = PROJ_W // N_CHIPS
SHARD_OUT = D_MODEL // N_CHIPS
SHARD_MOD = 3 * D_MODEL // N_CHIPS
VMEM_LIMIT = 56 * 1024 * 1024


def _pc(body, **kw):
    return pl.pallas_call(body, **kw)


def _cparams(**kw):
    return pltpu.CompilerParams(dimension_semantics=("arbitrary",), vmem_limit_bytes=VMEM_LIMIT, **kw)


def _full(shape):
    n = len(shape)
    return pl.BlockSpec(shape, lambda i: (0,) * n)


def _const(shape):
    n = len(shape)
    return pl.BlockSpec(shape, lambda i: (0,) * n, pipeline_mode=pl.Buffered(1))


def _rows(width, tm=TM):
    return pl.BlockSpec((tm, width), lambda i: (i, 0))


def _heads(nh, width, tm=TM):
    return pl.BlockSpec((nh, tm, width), lambda i: (0, i, 0))


def _sigmoid(x):
    return jax.nn.sigmoid(x)


def _dot(a, b):
    return jnp.dot(a, b, preferred_element_type=F32)


def _dot_nt(a, b):
    return lax.dot_general(a, b, (((1,), (1,)), ((), ())), preferred_element_type=F32)


def _dot_tn(a, b):
    return lax.dot_general(a, b, (((0,), (0,)), ((), ())), preferred_element_type=F32)


def _lane(rows):
    return lax.broadcasted_iota(jnp.int32, (rows, LANES), 1)


def _rowsum(x):
    return jnp.sum(x, axis=1, keepdims=True)


def _colsum(x):
    return jnp.sum(x, axis=0, keepdims=True)


def _pair_sums(x, lo):
    s0 = _rowsum(jnp.where(lo, x, 0.0))
    s1 = _rowsum(jnp.where(lo, 0.0, x))
    return jnp.where(lo, s0, s1)


def _swap16(x, lo16):
    return jnp.where(lo16, pltpu.roll(x, LANES - 16, 1), pltpu.roll(x, 16, 1))


def _layer_norm_stats(x):
    mu = jnp.mean(x, axis=1, keepdims=True)
    xc = x - mu
    rs = lax.rsqrt(jnp.mean(xc * xc, axis=1, keepdims=True) + LN_EPS)
    return xc * rs, rs


def _layer_norm_bwd(dxn, xn, rs):
    return rs * (dxn - jnp.mean(dxn, axis=1, keepdims=True) - xn * jnp.mean(dxn * xn, axis=1, keepdims=True))


def _group_select(r, grp):
    out = jnp.where(grp == 0, r[0:CHUNK], 0.0)
    for g in range(1, N_SPATIAL_GROUPS):
        out = out + jnp.where(grp == g, r[g * CHUNK:(g + 1) * CHUNK], 0.0)
    return out


def _kv_chunk(t):
    return 1024 if (t - CTX_LEN) % 1024 == 0 else 256


def _all_gather_rows(x_shard):
    m_per, n = x_shard.shape

    def body(x_ref, out_ref, send_sems, recv_sems, local_sem):
        x, y, c = lax.axis_index("x"), lax.axis_index("y"), lax.axis_index("c")
        me, sibling = (x, y, c), (x, y, 1 - c)
        chips = [(1 - x, y), (x, 1 - y), (1 - x, 1 - y)]

        def rows(px, py, pc):
            return out_ref.at[pl.ds((4 * px + 2 * py + pc) * m_per, m_per), :]

        def copy(k, block, to, src=None):
            return pltpu.make_async_remote_copy(
                src_ref=rows(*block) if src is None else src, dst_ref=rows(*block),
                send_sem=send_sems.at[k], recv_sem=recv_sems.at[k], device_id=to, device_id_type=MESH)

        mine = pltpu.make_async_copy(x_ref, rows(*me), local_sem)
        mine.start()
        first = [copy(0, me, sibling, src=x_ref)]
        first += [copy(1 + j, me, (*chip, c), src=x_ref) for j, chip in enumerate(chips)]
        for cp in first:
            cp.start()
        passed = [copy(4 + j, (*chip, c), sibling) for j, chip in enumerate(chips)]
        for j, chip in enumerate(chips):
            copy(1 + j, (*chip, c), me).wait_recv()
            passed[j].start()
        copy(0, sibling, me).wait_recv()
        for j, chip in enumerate(chips):
            copy(4 + j, (*chip, 1 - c), me).wait_recv()
        for cp in first + passed:
            cp.wait_send()
        mine.wait()

    return _pc(
        body, name="all_gather_rows",
        out_shape=jax.ShapeDtypeStruct((8 * m_per, n), x_shard.dtype),
        in_specs=[pl.BlockSpec(memory_space=pltpu.VMEM)],
        out_specs=pl.BlockSpec(memory_space=pltpu.VMEM),
        scratch_shapes=[pltpu.SemaphoreType.DMA((7,)), pltpu.SemaphoreType.DMA((7,)), pltpu.SemaphoreType.DMA],
        compiler_params=pltpu.CompilerParams(vmem_limit_bytes=VMEM_LIMIT),
    )(x_shard)


def _place():
    x, y, c = lax.axis_index("x"), lax.axis_index("y"), lax.axis_index("c")
    return x, y, c, [(1 - x, y), (x, 1 - y), (1 - x, 1 - y)]


def _remote(src, dst, send_sems, recv_sems, k, to):
    return pltpu.make_async_remote_copy(src_ref=src, dst_ref=dst, send_sem=send_sems.at[k], recv_sem=recv_sems.at[k],
                                        device_id=to, device_id_type=MESH)


GATHER_SEMS = 12
SCATTER_SEMS = 6


def _gather_phase(phase, pairs, send_sems, recv_sems, local_sems):
    x, y, c, chips = _place()
    kme = 2 * x + y
    sibling = (x, y, 1 - c)
    for a, (src, dst) in enumerate(pairs):
        half = src.shape[0] // 2
        mine = pl.ds(c * half, half)
        theirs = pl.ds((1 - c) * half, half)
        if phase == 0:
            pltpu.make_async_copy(src, dst.at[kme], local_sems.at[a]).start()
        if phase == 2:
            pltpu.make_async_copy(src, dst.at[kme], local_sems.at[a]).wait()
        for j, (px, py) in enumerate(chips):
            kk = 2 * px + py
            landed = dst.at[kk, mine]
            out = lambda: _remote(src.at[mine], dst.at[kme, mine], send_sems, recv_sems, 6 * a + j, (px, py, c))
            hand = lambda: _remote(landed, landed, send_sems, recv_sems, 6 * a + 3 + j, sibling)
            if phase == 0:
                out().start()
            if phase == 1:
                _remote(landed, landed, send_sems, recv_sems, 6 * a + j, (px, py, c)).wait_recv()
                hand().start()
            if phase == 2:
                other = dst.at[kk, theirs]
                _remote(other, other, send_sems, recv_sems, 6 * a + 3 + j, sibling).wait_recv()
                out().wait_send()
                hand().wait_send()


def _scatter_phase(phase, pairs, send_sems, recv_sems, local_sems):
    x, y, c, chips = _place()
    kme = 2 * x + y
    for a, (src, dst) in enumerate(pairs):
        loc = pltpu.make_async_copy(src.at[kme], dst.at[kme], local_sems.at[a])
        if phase == 0:
            loc.start()
        else:
            loc.wait()
        for j, (px, py) in enumerate(chips):
            kk = 2 * px + py
            out = _remote(src.at[kk], dst.at[kme], send_sems, recv_sems, 3 * a + j, (px, py, c))
            if phase == 0:
                out.start()
            else:
                landed = dst.at[kk]
                _remote(landed, landed, send_sems, recv_sems, 3 * a + j, (px, py, c)).wait_recv()
                out.wait_send()


def _comm_scratch(n):
    return [pltpu.SemaphoreType.DMA((n,)), pltpu.SemaphoreType.DMA((n,)), pltpu.SemaphoreType.DMA((2,))]


def _slots(a):
    return jax.ShapeDtypeStruct((N_CHIPS,) + a.shape, a.dtype)


def _gather_weights(wi, wo):
    def body(wi_ref, wo_ref, gi_ref, go_ref, send_sems, recv_sems, local_sems):
        for phase in range(3):
            _gather_phase(phase, ((wi_ref, gi_ref), (wo_ref, go_ref)), send_sems, recv_sems, local_sems)

    hbm = pl.BlockSpec(memory_space=pl.ANY)
    return _pc(
        body, name="gather_weights", out_shape=(_slots(wi), _slots(wo)),
        in_specs=[hbm, hbm], out_specs=(hbm, hbm), scratch_shapes=_comm_scratch(GATHER_SEMS),
    )(wi, wo)


def _scatter_slabs(gi, go):
    def body(gi_ref, go_ref, ri_ref, ro_ref, send_sems, recv_sems, local_sems):
        for phase in range(2):
            _scatter_phase(phase, ((gi_ref, ri_ref), (go_ref, ro_ref)), send_sems, recv_sems, local_sems)

    hbm = pl.BlockSpec(memory_space=pl.ANY)
    return _pc(
        body, name="scatter_slabs",
        out_shape=(jax.ShapeDtypeStruct(gi.shape, gi.dtype), jax.ShapeDtypeStruct(go.shape, go.dtype)),
        in_specs=[hbm, hbm], out_specs=(hbm, hbm), scratch_shapes=_comm_scratch(SCATTER_SEMS),
    )(gi, go)


def _swap_with_sibling(a, b):
    def body(a_ref, b_ref, ra_ref, rb_ref, send_sems, recv_sems):
        x, y, c = lax.axis_index("x"), lax.axis_index("y"), lax.axis_index("c")
        copies = []
        for k, (src, dst) in enumerate(((a_ref, ra_ref), (b_ref, rb_ref))):
            cp = pltpu.make_async_remote_copy(
                src_ref=src, dst_ref=dst, send_sem=send_sems.at[k], recv_sem=recv_sems.at[k],
                device_id=(x, y, 1 - c), device_id_type=MESH)
            cp.start()
            copies.append(cp)
        for cp in copies:
            cp.wait()

    hbm = pl.BlockSpec(memory_space=pl.ANY)
    return _pc(
        body, name="swap_with_sibling",
        out_shape=(jax.ShapeDtypeStruct(a.shape, a.dtype), jax.ShapeDtypeStruct(b.shape, b.dtype)),
        in_specs=[hbm, hbm], out_specs=(hbm, hbm),
        scratch_shapes=[pltpu.SemaphoreType.DMA((2,)), pltpu.SemaphoreType.DMA((2,))],
    )(a, b)


def _mod_forward(c16, w_mod, b_mod_shard):
    def body(c_ref, w_ref, b_ref, s_ref, o_ref):
        cc = c_ref[...]
        s = cc * _sigmoid(cc)
        s_ref[...] = s
        o_ref[0] = jnp.dot(s, w_ref[0], preferred_element_type=F32, precision=lax.Precision.HIGHEST) + b_ref[0]

    return _pc(
        body, name="mod_forward", grid=(DEPTH,),
        out_shape=(jax.ShapeDtypeStruct((16, D_MODEL), F32), jax.ShapeDtypeStruct((DEPTH, 16, SHARD_MOD), F32)),
        in_specs=[_full((16, D_MODEL)),
                  pl.BlockSpec((1, D_MODEL, SHARD_MOD), lambda l: (l, 0, 0)),
                  pl.BlockSpec((1, 1, SHARD_MOD), lambda l: (l, 0, 0))],
        out_specs=(_full((16, D_MODEL)), pl.BlockSpec((1, 16, SHARD_MOD), lambda l: (l, 0, 0))),
        compiler_params=_cparams(),
    )(c16, w_mod, b_mod_shard)


def _mod_backward(s_t, g_rows, g_ctx, d_all, w_mod, c_ctx_col):
    def body(st_ref, g_ref, gc_ref, d_ref, w_ref, cc_ref, gw_ref, gb_ref, pc_ref):
        l = pl.program_id(0)
        gw_ref[0] = jnp.dot(st_ref[...], g_ref[0], preferred_element_type=F32, precision=lax.Precision.HIGHEST)
        gb_ref[0] = _colsum(d_ref[0])
        part = _rowsum(w_ref[0] * _colsum(gc_ref[0]))

        @pl.when(l == 0)
        def _():
            pc_ref[...] = jnp.zeros_like(pc_ref)

        pc_ref[...] += part

        @pl.when(l == DEPTH - 1)
        def _():
            cc = cc_ref[...]
            sg = _sigmoid(cc)
            pc_ref[...] = pc_ref[...] * (sg * (1.0 + cc * (1.0 - sg)))

    return _pc(
        body, name="mod_backward", grid=(DEPTH,),
        out_shape=(jax.ShapeDtypeStruct((DEPTH, D_MODEL, SHARD_MOD), F32),
                   jax.ShapeDtypeStruct((DEPTH, 1, 3 * D_MODEL), F32),
                   jax.ShapeDtypeStruct((D_MODEL, 1), F32)),
        in_specs=[_full((D_MODEL, LANES)),
                  pl.BlockSpec((1, LANES, SHARD_MOD), lambda l: (l, 0, 0)),
                  pl.BlockSpec((1, 8, SHARD_MOD), lambda l: (l, 0, 0)),
                  pl.BlockSpec((1, 16, 3 * D_MODEL), lambda l: (l, 0, 0)),
                  pl.BlockSpec((1, D_MODEL, SHARD_MOD), lambda l: (l, 0, 0)),
                  _full((D_MODEL, 1))],
        out_specs=(pl.BlockSpec((1, D_MODEL, SHARD_MOD), lambda l: (l, 0, 0)),
                   pl.BlockSpec((1, 1, 3 * D_MODEL), lambda l: (l, 0, 0)),
                   _full((D_MODEL, 1))),
        compiler_params=_cparams(),
    )(s_t, g_rows, g_ctx, d_all, w_mod, c_ctx_col)


def _head_norm(xb, lo):
    r = lax.rsqrt(_pair_sums(xb * xb, lo) * (1.0 / HEAD_DIM) + RMS_EPS)
    return xb * r, r


def _in_proj(xt, modv, g_pre, w_c, w_r, w_q, qk_gain, cos_t, sin_t):
    t = xt.shape[0]

    def body(x_ref, mod_ref, g_ref, wc_ref, wr_ref, wq_ref, gain_ref, cos_ref, sin_ref,
             h_ref, pc_ref, pr_ref, pq_ref, q_ref, k_ref, v_ref):
        lane = _lane(TM)
        lo = lane < HEAD_DIM
        lo16 = (lane & 31) < 16
        one = jnp.where(lane == HEAD_DIM, 1.0, 0.0)
        for jj in range(SUB):
            rows = pl.ds(jj * TM, TM)
            is_ctx = pl.program_id(0) * SUB + jj < N_CTX_TILES
            x = x_ref[rows, :]
            r = lax.rsqrt(jnp.mean(x * x, axis=1, keepdims=True) + RMS_EPS)
            sh = jnp.where(is_ctx, mod_ref[0:1, :], mod_ref[3:4, :])
            sc = jnp.where(is_ctx, mod_ref[1:2, :], mod_ref[4:5, :])
            h = (x * r * g_ref[...]) * (1.0 + sc) + sh
            hb = h.astype(BF16)
            h_ref[rows, :] = hb
            pc_ref[rows, :] = _dot(hb, wc_ref[...])
            pr_ref[rows, :] = _dot(hb, wr_ref[...])
            pq = _dot(hb, wq_ref[...])
            pq_ref[rows, :] = pq
            cos = cos_ref[rows, :]
            sin = sin_ref[rows, :]
            for b in range(3):
                xh, _ = _head_norm(pq[:, b * LANES:(b + 1) * LANES], lo)
                xg = xh * (gain_ref[0:1, :] if b < 2 else gain_ref[1:2, :])
                rot = xg * cos + _swap16(xg, lo16) * sin
                if b < 2:
                    rot = rot * ATTN_SCALE
                dst = q_ref if b < 2 else k_ref
                base = 2 * b if b < 2 else 0
                dst[base, rows, :] = jnp.where(lo, rot, 0.0).astype(BF16)
                dst[base + 1, rows, :] = jnp.where(lo, pltpu.roll(rot, HEAD_DIM, 1), 0.0).astype(BF16)
            vb = pq[:, 3 * LANES:4 * LANES]
            v_ref[0, rows, :] = jnp.where(lo, vb, one).astype(BF16)
            v_ref[1, rows, :] = jnp.where(lo, pltpu.roll(vb, HEAD_DIM, 1), one).astype(BF16)

    return _pc(
        body, name="in_proj", grid=(t // TB,),
        out_shape=(jax.ShapeDtypeStruct((t, D_MODEL), BF16),
                   jax.ShapeDtypeStruct((t, W_C), F32), jax.ShapeDtypeStruct((t, W_R), F32), jax.ShapeDtypeStruct((t, W_Q), F32),
                   jax.ShapeDtypeStruct((N_Q_HEADS, t, LANES), BF16),
                   jax.ShapeDtypeStruct((N_KV_HEADS, t, LANES), BF16),
                   jax.ShapeDtypeStruct((N_KV_HEADS, t, LANES), BF16)),
        in_specs=[_rows(D_MODEL, TB), _const((8, D_MODEL)), _const((1, D_MODEL)),
                  _const((D_MODEL, W_C)), _const((D_MODEL, W_R)), _const((D_MODEL, W_Q)),
                  _const((8, LANES)), _rows(LANES, TB), _rows(LANES, TB)],
        out_specs=(_rows(D_MODEL, TB), _rows(W_C, TB), _rows(W_R, TB), _rows(W_Q, TB),
                   _heads(N_Q_HEADS, LANES, TB), _heads(N_KV_HEADS, LANES, TB), _heads(N_KV_HEADS, LANES, TB)),
        compiler_params=_cparams(),
    )(xt, modv, g_pre, w_c, w_r, w_q, qk_gain, cos_t, sin_t)


def _attention_fwd(q, k, v, shards=None):
    t = q.shape[1]
    tk = _kv_chunk(t)
    n_chunks = (t - CTX_LEN) // tk
    n_tiles = t // TM
    n_sh = 0 if shards is None else len(shards)

    def body(q_ref, k_ref, v_ref, *rest):
        i = pl.program_id(0)
        o_ref = rest[n_sh]
        if shards is not None:
            pairs = tuple(zip(rest[:n_sh], rest[n_sh + 1:2 * n_sh + 1]))
            for phase, at in enumerate((0, n_tiles // 2, n_tiles - 1)):
                @pl.when(i == at)
                def _(phase=phase):
                    _gather_phase(phase, pairs, *rest[2 * n_sh + 1:])
        lane = _lane(GQA * TM)
        qs = [jnp.concatenate([q_ref[GQA * g + hh] for hh in range(GQA)], axis=0) for g in range(N_KV_HEADS)]

        def step(st, size, carry):
            out = []
            for g in range(N_KV_HEADS):
                m, acc = carry[g]
                s = _dot_nt(qs[g], k_ref[g, pl.ds(st, size), :])
                m_new = jnp.maximum(m, jnp.max(s, axis=1, keepdims=True))
                p = jnp.exp(s - m_new)
                out.append((m_new, acc * jnp.exp(m - m_new) + _dot(p.astype(BF16), v_ref[g, pl.ds(st, size), :])))
            return tuple(out)

        init = tuple((jnp.full((GQA * TM, 1), -jnp.inf, F32), jnp.zeros((GQA * TM, LANES), F32)) for _ in range(N_KV_HEADS))

        def finish(carry):
            for g in range(N_KV_HEADS):
                m, acc = carry[g]
                den = _rowsum(jnp.where(lane == HEAD_DIM, acc, 0.0))
                out = jnp.where(lane < HEAD_DIM, acc * (1.0 / den), jnp.where(lane == HEAD_DIM, m + jnp.log(den), 0.0))
                for hh in range(GQA):
                    o_ref[GQA * g + hh] = out[hh * TM:(hh + 1) * TM]

        @pl.when(i < N_CTX_TILES)
        def _():
            finish(step(0, CTX_LEN, init))

        @pl.when(i >= N_CTX_TILES)
        def _():
            per = 4 if n_chunks % 4 == 0 else 1

            def trip(j, cr):
                st = pl.multiple_of(CTX_LEN + j * (per * tk), 256)
                for u in range(per):
                    cr = step(st + u * tk, tk, cr)
                return cr

            finish(lax.fori_loop(0, n_chunks // per, trip, step(0, CTX_LEN, init)))

    hbm = pl.BlockSpec(memory_space=pl.ANY)
    extra = () if shards is None else tuple(shards)
    outs = _pc(
        body, name="attention_fwd" if shards is None else "attention_fwd_gather", grid=(n_tiles,),
        out_shape=(jax.ShapeDtypeStruct((N_Q_HEADS, t, LANES), F32),) + tuple(_slots(a) for a in extra),
        in_specs=[_heads(N_Q_HEADS, LANES), _full((N_KV_HEADS, t, LANES)), _full((N_KV_HEADS, t, LANES))] + [hbm] * n_sh,
        out_specs=(_heads(N_Q_HEADS, LANES),) + (hbm,) * n_sh,
        scratch_shapes=_comm_scratch(GATHER_SEMS) if shards is not None else [],
        compiler_params=_cparams(),
    )(q, k, v, *extra)
    return outs[0], tuple(outs[1:])


def _halo_specs(width, t, rows=TM):
    last = t // HALO - 1
    per = rows // HALO
    prev = pl.BlockSpec((HALO, width), lambda i: (jnp.maximum(i * per - 1, 0), 0))
    nxt = pl.BlockSpec((HALO, width), lambda i: (jnp.minimum((i + 1) * per, last), 0))
    return prev, nxt


def _halo_valid(i, n_tiles):
    prev_ok = jnp.logical_and(i != 0, i != N_CTX_TILES)
    next_ok = jnp.logical_and(i != N_CTX_TILES - 1, i != n_tiles - 1)
    return jnp.where(prev_ok, 1.0, 0.0), jnp.where(next_ok, 1.0, 0.0)


def _conv_inputs(pc):
    u = pc[:, 0:GROUP_W] * pc[:, GROUP_W:2 * GROUP_W]
    z = pc[:, 2 * GROUP_W:3 * GROUP_W] * _sigmoid(pc[:, 3 * GROUP_W:4 * GROUP_W])
    return u, z


def _fill_ext(ext_ref, prev, mid, nxt):
    ext_ref[0:HALO, :] = prev
    ext_ref[HALO:HALO + TM, :] = mid
    ext_ref[HALO + TM:HALO + TM + HALO, :] = nxt


def _row_local_mixers(pr, ca, z2, oe, vecs, wss_ref, bsm, lane256):
    a_b, a_g, b_g = pr[:, 0:256], pr[:, 256:512], pr[:, 512:768]
    c_u, c_v, c_g, d_g = pr[:, 768:1024], pr[:, 1024:1280], pr[:, 1280:1536], pr[:, 1536:1792]
    zn, rs_b = _layer_norm_stats(z2)
    tb = zn * vecs[1:2, :] + vecs[2:3, :]
    vn_hat, rs_c = _layer_norm_stats(c_v)
    vn = vn_hat * vecs[3:4, :] + vecs[4:5, :]
    grp = jnp.right_shift(lane256, 6)
    sgs = []
    for ch in range(TM // CHUNK):
        r = _dot(wss_ref[...], vn[ch * CHUNK:(ch + 1) * CHUNK, :].astype(BF16))
        sgs.append(_group_select(r, grp[0:CHUNK]) + bsm)
    sg = jnp.concatenate(sgs, axis=0)
    lane = _lane(TM)
    lo = lane < HEAD_DIM
    att = jnp.concatenate([jnp.where(lo, oe[2 * b], pltpu.roll(oe[2 * b + 1], HEAD_DIM, 1)) for b in range(2)], axis=1)
    return dict(a_b=a_b, a_g=a_g, b_g=b_g, c_u=c_u, c_v=c_v, c_g=c_g, d_g=d_g, zn=zn, rs_b=rs_b, tb=tb,
                vn_hat=vn_hat, rs_c=rs_c, vn=vn, sg=sg, att=att, grp=grp, lo=lo, lane=lane)


def _mixer_concat(f, ca):
    ya = f["a_b"] * ca
    yb = f["tb"] * _sigmoid(f["tb"])
    yc = f["c_u"] * f["sg"]
    gates = [f[n] * _sigmoid(f[n]) for n in ("a_g", "b_g", "c_g", "d_g")]
    ys = (ya, yb, yc, f["att"])
    big = jnp.concatenate([yy * gg for yy, gg in zip(ys, gates)], axis=1).astype(BF16)
    return big, ys, gates


def _taps31(ext_ref, w_ref, flip):
    blocks = []
    for r0 in range(0, TM, CONV_ROWS):
        out = None
        for b in range(8):
            part = None
            for a in range(4):
                o = 8 * a + b
                if 1 <= o <= CONFORMER_K:
                    kk = CONFORMER_K - o if flip else o - 1
                    term = w_ref[kk:kk + 1, :] * ext_ref[pl.ds(r0 + 8 * a, CONV_ROWS + 8), :]
                    part = term if part is None else part + term
            part = part[b:b + CONV_ROWS]
            out = part if out is None else out + part
        blocks.append(out)
    return jnp.concatenate(blocks, axis=0)


def _mix_out(pc, pr, oe, xt, modv, g_post, w_out, conv_a, conv_b, vecs, wss, bsm, target=None):
    t = xt.shape[0]
    n_tiles = t // TM
    prev_spec, next_spec = _halo_specs(W_C, t, TB)
    n_t = 0 if target is None else SUB

    def body(pc_ref, pp_ref, pn_ref, pr_ref, oe_ref, x_ref, mod_ref, gp_ref, wo_ref, cva_ref, cvb_ref, vec_ref, wss_ref, bsm_ref,
             *rest):
        xo_ref, y_ref, ca_ref, z2_ref = rest[n_t:n_t + 4]
        uext, zext = rest[-2:]
        i = pl.program_id(0)
        vecs = vec_ref[...]
        lane256 = lax.broadcasted_iota(jnp.int32, (TM, GROUP_W), 1)
        if target is not None:
            loss_ref = rest[n_t + 4]

            @pl.when(i == 0)
            def _():
                loss_ref[...] = jnp.zeros_like(loss_ref)

        for jj in range(SUB):
            rows = pl.ds(jj * TM, TM)
            tile = i * SUB + jj
            is_ctx = tile < N_CTX_TILES
            pv, nv = _halo_valid(tile, n_tiles)
            u, z = _conv_inputs(pc_ref[rows, :])
            up, zp = _conv_inputs(pp_ref[...] if jj == 0 else pc_ref[pl.ds(jj * TM - HALO, HALO), :])
            un, zn_ = _conv_inputs(pn_ref[...] if jj == SUB - 1 else pc_ref[pl.ds((jj + 1) * TM, HALO), :])
            ue, ze = uext.at[jj], zext.at[jj]
            _fill_ext(ue, up * pv, u, un * nv)
            _fill_ext(ze, zp * pv, z, zn_ * nv)
            ca = cva_ref[0:1, :] * ue[pl.ds(HALO - 1, TM), :]
            for kk in range(1, SHORT_CONV_K):
                ca = ca + cva_ref[kk:kk + 1, :] * ue[pl.ds(HALO - 1 + kk, TM), :]
            z2 = _taps31(ze, cvb_ref, False) + vecs[0:1, :]
            ca_ref[rows, :] = ca
            z2_ref[rows, :] = z2
            oes = [oe_ref[h, rows, :] for h in range(N_Q_HEADS)]
            f = _row_local_mixers(pr_ref[rows, :], ca, z2, oes, vecs, wss_ref, bsm_ref[...], lane256)
            big, _, _ = _mixer_concat(f, ca)
            y = _dot(big, wo_ref[...])
            y_ref[rows, :] = y
            ry = lax.rsqrt(jnp.mean(y * y, axis=1, keepdims=True) + RMS_EPS)
            gt = jnp.where(is_ctx, mod_ref[2:3, :], mod_ref[5:6, :])
            x_new = x_ref[rows, :] + gt * (y * ry * gp_ref[...])
            if target is None:
                xo_ref[rows, :] = x_new
            else:
                err = (x_new - rest[jj][...]) * jnp.where(is_ctx, 0.0, 1.0)
                xo_ref[rows, :] = err * (1.0 / D_MODEL)
                loss_ref[...] += jnp.sum(err * err) * (0.5 / D_MODEL)

    rows_f32 = jax.ShapeDtypeStruct((t, D_MODEL), F32)
    group_f32 = jax.ShapeDtypeStruct((t, GROUP_W), F32)
    loss_shape, loss_spec, t_spec, t_arg = (), (), [], ()
    if target is not None:
        loss_shape, loss_spec = (jax.ShapeDtypeStruct((8, LANES), F32),), (_full((8, LANES)),)
        t_spec = [pl.BlockSpec((TM, D_MODEL), lambda i, jj=jj: (jnp.maximum(i * SUB + jj - N_CTX_TILES, 0), 0))
                  for jj in range(SUB)]
        t_arg = (target,) * SUB
    return _pc(
        body, name="mix_out" if target is None else "mix_out_loss", grid=(t // TB,),
        out_shape=(rows_f32, rows_f32, group_f32, group_f32) + loss_shape,
        in_specs=[_rows(W_C, TB), prev_spec, next_spec, _rows(W_R, TB), _heads(N_Q_HEADS, LANES, TB), _rows(D_MODEL, TB),
                  _const((8, D_MODEL)), _const((1, D_MODEL)), _const((D_MODEL, D_MODEL)),
                  _const((8, GROUP_W)), _const((32, GROUP_W)), _const((8, GROUP_W)),
                  _const((N_SPATIAL_GROUPS * CHUNK, CHUNK)), _const((CHUNK, GROUP_W))] + t_spec,
        out_specs=(_rows(D_MODEL, TB), _rows(D_MODEL, TB), _rows(GROUP_W, TB), _rows(GROUP_W, TB)) + loss_spec,
        scratch_shapes=[pltpu.VMEM((SUB, TM + 2 * HALO, GROUP_W), F32), pltpu.VMEM((SUB, TM + 2 * HALO, GROUP_W), F32)],
        compiler_params=_cparams(),
    )(pc, pc, pc, pr, oe, xt, modv, g_post, w_out, conv_a, conv_b, vecs, wss, bsm, *t_arg)


def _mix_out_bwd(dxo, y, pr, ca, z2, oe, modv, g_post, w_out, vecs, wss, wsts, bsm):
    t = y.shape[0]
    n_tiles = t // TM

    def body(dxo_ref, y_ref, pr_ref, ca_ref, z2_ref, oe_ref, mod_ref, gp_ref, wo_ref, vec_ref, wss_ref, wsts_ref, bsm_ref,
             dpr_ref, ga_ref, gb_ref, doe_ref, dwo_ref, pvec_ref, s256_ref, dws_ref, dbs_ref, dbsm):
        i = pl.program_id(0)
        is_ctx = i < N_CTX_TILES

        @pl.when(i == 0)
        def _():
            dwo_ref[...] = jnp.zeros_like(dwo_ref)
            pvec_ref[...] = jnp.zeros_like(pvec_ref)
            s256_ref[...] = jnp.zeros_like(s256_ref)
            dws_ref[...] = jnp.zeros_like(dws_ref)
            dbsm[...] = jnp.zeros_like(dbsm)

        dxo_ = dxo_ref[...]
        y_ = y_ref[...]
        ry = lax.rsqrt(jnp.mean(y_ * y_, axis=1, keepdims=True) + RMS_EPS)
        nh = y_ * ry
        gp = gp_ref[...]
        gt = jnp.where(is_ctx, mod_ref[2:3, :], mod_ref[5:6, :])
        dgt = _colsum(dxo_ * (nh * gp))
        pvec_ref[0:1, :] += jnp.where(is_ctx, dgt, 0.0)
        pvec_ref[1:2, :] += jnp.where(is_ctx, 0.0, dgt)
        dn = dxo_ * gt
        pvec_ref[2:3, :] += _colsum(dn * nh)
        dnh = dn * gp
        dy = ry * (dnh - nh * jnp.mean(dnh * nh, axis=1, keepdims=True))

        vecs = vec_ref[...]
        bsm_ = bsm_ref[...]
        ca_ = ca_ref[...]
        lane256 = lax.broadcasted_iota(jnp.int32, (TM, GROUP_W), 1)
        f = _row_local_mixers(pr_ref[...], ca_, z2_ref[...], oe_ref, vecs, wss_ref, bsm_, lane256)
        big, ys, gates = _mixer_concat(f, ca_)
        dyb = dy.astype(BF16)
        dwo_ref[...] += _dot_tn(big, dyb)
        dbig = _dot_nt(dyb, wo_ref[...])

        d_y, d_gate = [], []
        for n, (name, yy, gg) in enumerate(zip(("a_g", "b_g", "c_g", "d_g"), ys, gates)):
            dpart = dbig[:, n * GROUP_W:(n + 1) * GROUP_W]
            gx = f[name]
            sg_ = _sigmoid(gx)
            d_y.append(dpart * gg)
            d_gate.append(dpart * yy * (sg_ * (1.0 + gx * (1.0 - sg_))))
        dya, dyb_, dyc, datt = d_y

        d_ab = dya * ca_
        ga_ref[...] = dya * f["a_b"]
        tb = f["tb"]
        sb = _sigmoid(tb)
        dtb = dyb_ * (sb * (1.0 + tb * (1.0 - sb)))
        s256_ref[1:2, :] += _colsum(dtb * f["zn"])
        s256_ref[2:3, :] += _colsum(dtb)
        dz2 = _layer_norm_bwd(dtb * vecs[1:2, :], f["zn"], f["rs_b"])
        gb_ref[...] = dz2
        s256_ref[0:1, :] += _colsum(dz2)
        d_cu = dyc * f["sg"]
        dsg = dyc * f["c_u"]
        grp = f["grp"]
        dvn_parts = []
        for ch in range(TM // CHUNK):
            rows = slice(ch * CHUNK, (ch + 1) * CHUNK)
            dsg_c = dsg[rows, :]
            dbsm[...] += dsg_c
            vn_c = f["vn"][rows, :].astype(BF16)
            for g in range(N_SPATIAL_GROUPS):
                masked = jnp.where(grp[0:CHUNK] == g, dsg_c, 0.0).astype(BF16)
                dws_ref[g * CHUNK:(g + 1) * CHUNK, :] += _dot_nt(masked, vn_c)
            dvn_parts.append(_group_select(_dot(wsts_ref[...], dsg_c.astype(BF16)), grp[0:CHUNK]))
        dvn = jnp.concatenate(dvn_parts, axis=0)
        s256_ref[3:4, :] += _colsum(dvn * f["vn_hat"])
        s256_ref[4:5, :] += _colsum(dvn)
        d_cv = _layer_norm_bwd(dvn * vecs[3:4, :], f["vn_hat"], f["rs_c"])
        lane, lo = f["lane"], f["lo"]
        att = f["att"]
        for b in range(2):
            da = datt[:, b * LANES:(b + 1) * LANES]
            prod = da * att[:, b * LANES:(b + 1) * LANES]
            for hh in range(2):
                h = 2 * b + hh
                lse = _rowsum(jnp.where(lane == HEAD_DIM, oe_ref[h], 0.0))
                delta = _rowsum(jnp.where(lo, prod, 0.0) if hh == 0 else jnp.where(lo, 0.0, prod))
                dah = da if hh == 0 else pltpu.roll(da, HEAD_DIM, 1)
                doe_ref[h] = jnp.where(lo, dah, jnp.where(lane == HEAD_DIM, delta, jnp.where(lane == HEAD_DIM + 1, lse, 0.0)))

        dpr_ref[...] = jnp.concatenate([d_ab, d_gate[0], d_gate[1], d_cu, d_cv, d_gate[2], d_gate[3]], axis=1).astype(BF16)

        @pl.when(i == n_tiles - 1)
        def _():
            acc = dbsm[...]
            lane128 = _lane(CHUNK)
            out = jnp.zeros((CHUNK, LANES), F32)
            for g in range(N_SPATIAL_GROUPS):
                col = _rowsum(jnp.where(grp[0:CHUNK] == g, acc, 0.0))
                out = out + jnp.where(lane128 == g, col, 0.0)
            dbs_ref[...] = out

    return _pc(
        body, name="mix_out_bwd", grid=(n_tiles,),
        out_shape=(jax.ShapeDtypeStruct((t, W_R), BF16),
                   jax.ShapeDtypeStruct((t, GROUP_W), F32), jax.ShapeDtypeStruct((t, GROUP_W), F32),
                   jax.ShapeDtypeStruct((N_Q_HEADS, t, LANES), F32),
                   jax.ShapeDtypeStruct((D_MODEL, D_MODEL), F32),
                   jax.ShapeDtypeStruct((8, D_MODEL), F32),
                   jax.ShapeDtypeStruct((8, GROUP_W), F32),
                   jax.ShapeDtypeStruct((N_SPATIAL_GROUPS * CHUNK, CHUNK), F32),
                   jax.ShapeDtypeStruct((CHUNK, LANES), F32)),
        in_specs=[_rows(D_MODEL), _rows(D_MODEL), _rows(W_R), _rows(GROUP_W), _rows(GROUP_W), _heads(N_Q_HEADS, LANES),
                  _full((8, D_MODEL)), _full((1, D_MODEL)), _full((D_MODEL, D_MODEL)), _full((8, GROUP_W)),
                  _full((N_SPATIAL_GROUPS * CHUNK, CHUNK)), _full((N_SPATIAL_GROUPS * CHUNK, CHUNK)), _full((CHUNK, GROUP_W))],
        out_specs=(_rows(W_R), _rows(GROUP_W), _rows(GROUP_W), _heads(N_Q_HEADS, LANES),
                   _full((D_MODEL, D_MODEL)), _full((8, D_MODEL)), _full((8, GROUP_W)),
                   _full((N_SPATIAL_GROUPS * CHUNK, CHUNK)), _full((CHUNK, LANES))),
        scratch_shapes=[pltpu.VMEM((CHUNK, GROUP_W), F32)],
        compiler_params=_cparams(),
    )(dxo, y, pr, ca, z2, oe, modv, g_post, w_out, vecs, wss, wsts, bsm)


def _conv_bwd(pc, g_a, g_b, conv_a, conv_b):
    t = pc.shape[0]
    n_tiles = t // TM
    pc_prev, pc_next = _halo_specs(W_C, t)
    g_prev, g_next = _halo_specs(GROUP_W, t)

    def body(pc_ref, pp_ref, pn_ref, ga_ref, gap_ref, gan_ref, gb_ref, gbp_ref, gbn_ref, cva_ref, cvb_ref,
             dpc_ref, dca_ref, dcb_ref, uext, zext, gaext, gbext):
        i = pl.program_id(0)

        @pl.when(i == 0)
        def _():
            dca_ref[...] = jnp.zeros_like(dca_ref)
            dcb_ref[...] = jnp.zeros_like(dcb_ref)

        pv, nv = _halo_valid(i, n_tiles)
        pc_ = pc_ref[...]
        u, z = _conv_inputs(pc_)
        up, zp = _conv_inputs(pp_ref[...])
        un, zn_ = _conv_inputs(pn_ref[...])
        _fill_ext(uext, up * pv, u, un * nv)
        _fill_ext(zext, zp * pv, z, zn_ * nv)
        ga = ga_ref[...]
        gb = gb_ref[...]
        _fill_ext(gaext, gap_ref[...] * pv, ga, gan_ref[...] * nv)
        _fill_ext(gbext, gbp_ref[...] * pv, gb, gbn_ref[...] * nv)

        du = cva_ref[0:1, :] * gaext[pl.ds(HALO + 1, TM), :]
        dca_ref[0:1, :] += _colsum(ga * uext[pl.ds(HALO - 1, TM), :])
        for kk in range(1, SHORT_CONV_K):
            du = du + cva_ref[kk:kk + 1, :] * gaext[pl.ds(HALO + 1 - kk, TM), :]
            dca_ref[kk:kk + 1, :] += _colsum(ga * uext[pl.ds(HALO - 1 + kk, TM), :])
        dz = _taps31(gbext, cvb_ref, True)
        for r0 in range(0, TM, CONV_ROWS):
            gb_rows = gb_ref[pl.ds(r0, CONV_ROWS), :]
            for b in range(8):
                zb = zext[pl.ds(r0 + b, CONV_ROWS + 24), :]
                for a in range(4):
                    kk = 8 * a + b - 1
                    if 0 <= kk < CONFORMER_K:
                        dcb_ref[kk:kk + 1, :] += _colsum(gb_rows * zb[8 * a:8 * a + CONV_ROWS])

        a_c, a_h = pc_[:, 0:GROUP_W], pc_[:, GROUP_W:2 * GROUP_W]
        glu_a, glu_g = pc_[:, 2 * GROUP_W:3 * GROUP_W], pc_[:, 3 * GROUP_W:4 * GROUP_W]
        sg = _sigmoid(glu_g)
        dpc_ref[...] = jnp.concatenate([du * a_h, du * a_c, dz * sg, dz * glu_a * sg * (1.0 - sg)], axis=1).astype(BF16)

    ext = pltpu.VMEM((TM + 2 * HALO, GROUP_W), F32)
    return _pc(
        body, name="conv_bwd", grid=(n_tiles,),
        out_shape=(jax.ShapeDtypeStruct((t, W_C), BF16), jax.ShapeDtypeStruct((8, GROUP_W), F32), jax.ShapeDtypeStruct((32, GROUP_W), F32)),
        in_specs=[_rows(W_C), pc_prev, pc_next, _rows(GROUP_W), g_prev, g_next, _rows(GROUP_W), g_prev, g_next,
                  _full((8, GROUP_W)), _full((32, GROUP_W))],
        out_specs=(_rows(W_C), _full((8, GROUP_W)), _full((32, GROUP_W))),
        scratch_shapes=[ext, ext, ext, ext],
        compiler_params=_cparams(),
    )(pc, pc, pc, g_a, g_a, g_a, g_b, g_b, g_b, conv_a, conv_b)


def _attention_bwd(q, k, v, doe, slabs=None):
    t = q.shape[1]
    tk = _kv_chunk(t)
    n_chunks = (t - CTX_LEN) // tk
    n_tiles = t // TM
    n_sl = 0 if slabs is None else len(slabs)

    def body(q_ref, do_ref, k_ref, v_ref, *rest):
        i = pl.program_id(0)
        dq_ref, dk_hbm, dv_hbm = rest[n_sl:n_sl + 3]
        dk_acc, dv_acc = rest[2 * n_sl + 3:2 * n_sl + 5]
        pairs = tuple(zip(rest[:n_sl], rest[n_sl + 3:2 * n_sl + 3]))
        sems = rest[2 * n_sl + 5:]

        @pl.when(i == 0)
        def _():
            dk_acc[...] = jnp.zeros_like(dk_acc)
            dv_acc[...] = jnp.zeros_like(dv_acc)
            if slabs is not None:
                _scatter_phase(0, pairs, *sems)

        lane = _lane(GQA * TM)
        lo = lane < HEAD_DIM
        qs, dos, deltas, lses = [], [], [], []
        for g in range(N_KV_HEADS):
            qs.append(jnp.concatenate([q_ref[GQA * g + hh] for hh in range(GQA)], axis=0))
            dog = jnp.concatenate([do_ref[GQA * g + hh] for hh in range(GQA)], axis=0)
            deltas.append(_rowsum(jnp.where(lane == HEAD_DIM, dog, 0.0)))
            lses.append(_rowsum(jnp.where(lane == HEAD_DIM + 1, dog, 0.0)))
            dos.append(jnp.where(lo, dog, 0.0).astype(BF16))

        def step(st, size, dqs):
            out = []
            for g in range(N_KV_HEADS):
                kc = k_ref[g, pl.ds(st, size), :]
                vc = v_ref[g, pl.ds(st, size), :]
                p = jnp.exp(_dot_nt(qs[g], kc) - lses[g])
                ds_ = (p * (_dot_nt(dos[g], vc) - deltas[g])).astype(BF16)
                dk_acc[g, pl.ds(st, size), :] += _dot_tn(ds_, qs[g])
                dv_acc[g, pl.ds(st, size), :] += _dot_tn(p.astype(BF16), dos[g])
                out.append(dqs[g] + _dot(ds_, kc))
            return tuple(out)

        zero = tuple(jnp.zeros((GQA * TM, LANES), F32) for _ in range(N_KV_HEADS))

        def finish(dqs):
            for g in range(N_KV_HEADS):
                for hh in range(GQA):
                    dq_ref[GQA * g + hh] = dqs[g][hh * TM:(hh + 1) * TM]

        @pl.when(i < N_CTX_TILES)
        def _():
            finish(step(0, CTX_LEN, zero))

        @pl.when(i >= N_CTX_TILES)
        def _():
            finish(lax.fori_loop(0, n_chunks, lambda j, acc: step(pl.multiple_of(CTX_LEN + j * tk, 256), tk, acc),
                                 step(0, CTX_LEN, zero)))

        @pl.when(i == n_tiles - 1)
        def _():
            pltpu.sync_copy(dk_acc, dk_hbm)
            pltpu.sync_copy(dv_acc, dv_hbm)
            if slabs is not None:
                _scatter_phase(1, pairs, *sems)

    kv_shape = jax.ShapeDtypeStruct((N_KV_HEADS, t, LANES), F32)
    hbm = pl.BlockSpec(memory_space=pl.ANY)
    extra = () if slabs is None else tuple(slabs)
    outs = _pc(
        body, name="attention_bwd" if slabs is None else "attention_bwd_scatter", grid=(n_tiles,),
        out_shape=(jax.ShapeDtypeStruct((N_Q_HEADS, t, LANES), F32), kv_shape, kv_shape)
        + tuple(jax.ShapeDtypeStruct(a.shape, a.dtype) for a in extra),
        in_specs=[_heads(N_Q_HEADS, LANES), _heads(N_Q_HEADS, LANES),
                  _full((N_KV_HEADS, t, LANES)), _full((N_KV_HEADS, t, LANES))] + [hbm] * n_sl,
        out_specs=(_heads(N_Q_HEADS, LANES), hbm, hbm) + (hbm,) * n_sl,
        scratch_shapes=[pltpu.VMEM((N_KV_HEADS, t, LANES), F32), pltpu.VMEM((N_KV_HEADS, t, LANES), F32)]
        + (_comm_scratch(SCATTER_SEMS) if slabs is not None else []),
        compiler_params=_cparams(),
    )(q, doe, k, v, *extra)
    return outs[0], outs[1], outs[2], tuple(outs[3:])


def _in_proj_bwd(dpc, dpr, dq, dk, dv, pq, qk_gain, cos_t, sin_t, w_c, w_r, w_q, xt, dxo, modv, g_pre):
    t = xt.shape[0]

    def body(dpc_ref, dpr_ref, dq_ref, dk_ref, dv_ref, pq_ref, gain_ref, cos_ref, sin_ref, wc_ref, wr_ref, wq_ref,
             x_ref, dxo_ref, mod_ref, g_ref, dx_ref, dpq_ref, acc_ref, dgain_ref):
        i = pl.program_id(0)

        @pl.when(i == 0)
        def _():
            acc_ref[...] = jnp.zeros_like(acc_ref)
            dgain_ref[...] = jnp.zeros_like(dgain_ref)

        lane = _lane(TM)
        lo = lane < HEAD_DIM
        lo16 = (lane & 31) < 16
        g = g_ref[...]
        for jj in range(SUB):
            rows = pl.ds(jj * TM, TM)
            is_ctx = i * SUB + jj < N_CTX_TILES
            cos = cos_ref[rows, :]
            sin = sin_ref[rows, :]
            outs = []
            for b in range(3):
                src = dq_ref if b < 2 else dk_ref
                base = 2 * b if b < 2 else 0
                drot = src[base, rows, :] + pltpu.roll(src[base + 1, rows, :], HEAD_DIM, 1)
                if b < 2:
                    drot = drot * ATTN_SCALE
                dxg = drot * cos + _swap16(drot * sin, lo16)
                xh, r = _head_norm(pq_ref[rows, b * LANES:(b + 1) * LANES], lo)
                row = 0 if b < 2 else 1
                dgain_ref[row:row + 1, :] += _colsum(dxg * xh)
                dxh = dxg * gain_ref[row:row + 1, :]
                outs.append(r * (dxh - xh * (_pair_sums(dxh * xh, lo) * (1.0 / HEAD_DIM))))
            outs.append(dv_ref[0, rows, :] + pltpu.roll(dv_ref[1, rows, :], HEAD_DIM, 1))
            dpq = jnp.concatenate(outs, axis=1).astype(BF16)
            dpq_ref[rows, :] = dpq

            dh = _dot_nt(dpc_ref[rows, :], wc_ref[...]) + _dot_nt(dpr_ref[rows, :], wr_ref[...]) + _dot_nt(dpq, wq_ref[...])
            x = x_ref[rows, :]
            r = lax.rsqrt(jnp.mean(x * x, axis=1, keepdims=True) + RMS_EPS)
            xn = x * r
            sc = jnp.where(is_ctx, mod_ref[1:2, :], mod_ref[4:5, :])
            dsh = _colsum(dh)
            dsc = _colsum(dh * (xn * g))
            acc_ref[0:1, :] += jnp.where(is_ctx, dsh, 0.0)
            acc_ref[1:2, :] += jnp.where(is_ctx, dsc, 0.0)
            acc_ref[2:3, :] += jnp.where(is_ctx, 0.0, dsh)
            acc_ref[3:4, :] += jnp.where(is_ctx, 0.0, dsc)
            dxg = dh * (1.0 + sc)
            acc_ref[4:5, :] += _colsum(dxg * xn)
            dxn = dxg * g
            dx_ref[rows, :] = r * (dxn - xn * jnp.mean(dxn * xn, axis=1, keepdims=True)) + dxo_ref[rows, :]

    return _pc(
        body, name="in_proj_bwd", grid=(t // TB,),
        out_shape=(jax.ShapeDtypeStruct((t, D_MODEL), F32), jax.ShapeDtypeStruct((t, W_Q), BF16),
                   jax.ShapeDtypeStruct((8, D_MODEL), F32), jax.ShapeDtypeStruct((8, LANES), F32)),
        in_specs=[_rows(W_C, TB), _rows(W_R, TB),
                  _heads(N_Q_HEADS, LANES, TB), _heads(N_KV_HEADS, LANES, TB), _heads(N_KV_HEADS, LANES, TB), _rows(W_Q, TB),
                  _const((8, LANES)), _rows(LANES, TB), _rows(LANES, TB),
                  _const((D_MODEL, W_C)), _const((D_MODEL, W_R)), _const((D_MODEL, W_Q)),
                  _rows(D_MODEL, TB), _rows(D_MODEL, TB), _const((8, D_MODEL)), _const((1, D_MODEL))],
        out_specs=(_rows(D_MODEL, TB), _rows(W_Q, TB), _full((8, D_MODEL)), _full((8, LANES))),
        compiler_params=_cparams(),
    )(dpc, dpr, dq, dk, dv, pq, qk_gain, cos_t, sin_t, w_c, w_r, w_q, xt, dxo, modv, g_pre)


def _in_proj_wgrad(h, dpc, dpr, dpq):
    t = h.shape[0]

    def body(h_ref, dpc_ref, dpr_ref, dpq_ref, gc_ref, gr_ref, gq_ref):
        @pl.when(pl.program_id(0) == 0)
        def _():
            gc_ref[...] = jnp.zeros_like(gc_ref)
            gr_ref[...] = jnp.zeros_like(gr_ref)
            gq_ref[...] = jnp.zeros_like(gq_ref)

        hb = h_ref[...]
        gc_ref[...] += _dot_tn(hb, dpc_ref[...])
        gr_ref[...] += _dot_tn(hb, dpr_ref[...])
        gq_ref[...] += _dot_tn(hb, dpq_ref[...])

    return _pc(
        body, name="in_proj_wgrad", grid=(t // TB,),
        out_shape=(jax.ShapeDtypeStruct((D_MODEL, W_C), F32), jax.ShapeDtypeStruct((D_MODEL, W_R), F32),
                   jax.ShapeDtypeStruct((D_MODEL, W_Q), F32)),
        in_specs=[_rows(D_MODEL, TB), _rows(W_C, TB), _rows(W_R, TB), _rows(W_Q, TB)],
        out_specs=(_full((D_MODEL, W_C)), _full((D_MODEL, W_R)), _full((D_MODEL, W_Q))),
        compiler_params=_cparams(),
    )(h, dpc, dpr, dpq)


def _sum_slabs(slabs, tile_rows):
    n, r, c = slabs.shape

    def body(s_ref, o_ref):
        acc = s_ref[0].astype(F32)
        for k in range(1, n):
            acc = acc + s_ref[k].astype(F32)
        o_ref[...] = acc

    return _pc(
        body, name="sum_slabs", grid=(r // tile_rows,),
        out_shape=jax.ShapeDtypeStruct((r, c), F32),
        in_specs=[pl.BlockSpec((n, tile_rows, c), lambda i: (0, i, 0))],
        out_specs=pl.BlockSpec((tile_rows, c), lambda i: (i, 0)),
        compiler_params=_cparams(),
    )(slabs)


def _sum_layer_slabs(layers, tile_rows):
    nl = len(layers)
    n, r, c = layers[0].shape
    per = r // tile_rows

    def body(*refs):
        o_ref = refs[nl]
        for l in range(nl):
            @pl.when(pl.program_id(0) // per == l)
            def _(l=l):
                acc = refs[l][0].astype(F32)
                for k in range(1, n):
                    acc = acc + refs[l][k].astype(F32)
                o_ref[...] = acc

    def spec(l):
        return pl.BlockSpec((n, tile_rows, c), lambda i: (0, jnp.clip(i - l * per, 0, per - 1), 0))

    return _pc(
        body, name="sum_layer_slabs", grid=(nl * per,),
        out_shape=jax.ShapeDtypeStruct((nl * r, c), F32),
        in_specs=[spec(l) for l in range(nl)],
        out_specs=pl.BlockSpec((tile_rows, c), lambda i: (i, 0)),
        compiler_params=_cparams(),
    )(*layers)


def _adamw(grads, w, m, v, tile_rows):
    r, c = w.shape
    n_g = len(grads)

    def body(*refs):
        g = refs[0][...]
        for k in range(1, n_g):
            g = g + refs[k][...]
        w_ref, m_ref, v_ref, g_out, d_out, m_out, v_out = refs[n_g:]
        m_new = ADAM_B1 * m_ref[...] + (1.0 - ADAM_B1) * g
        v_new = ADAM_B2 * v_ref[...] + (1.0 - ADAM_B2) * (g * g)
        m_hat = m_new / (1.0 - ADAM_B1 ** ADAM_STEP)
        v_hat = v_new / (1.0 - ADAM_B2 ** ADAM_STEP)
        g_out[...] = g
        d_out[...] = -ADAM_LR * (m_hat / (jnp.sqrt(v_hat) + ADAM_EPS) + ADAM_WD * w_ref[...])
        m_out[...] = m_new
        v_out[...] = v_new

    spec = pl.BlockSpec((tile_rows, c), lambda i: (i, 0))
    shape = jax.ShapeDtypeStruct((r, c), F32)
    return _pc(
        body, name="adamw", grid=(r // tile_rows,),
        out_shape=(shape,) * 4, in_specs=[spec] * (n_g + 3), out_specs=(spec,) * 4,
        compiler_params=_cparams(),
    )(*grads, w, m, v)


def _rope_tables(s_lat):
    n_rows = s_lat // GRID_W
    axis_dim = HEAD_DIM // 2
    inv_freq = 1.0 / (ROPE_THETA ** (jnp.arange(0, axis_dim, 2, dtype=F32) / axis_dim))
    d = np.arange(LANES) % HEAD_DIM
    on_rows = (d // axis_dim) == 0
    freq = d % (axis_dim // 2)
    sign = np.where((d % axis_dim) < axis_dim // 2, -1.0, 1.0).astype(np.float32)
    ang_r = jnp.arange(n_rows, dtype=F32)[:, None] * inv_freq[freq][None, :]
    ang_c = jnp.arange(GRID_W, dtype=F32)[:, None] * inv_freq[freq][None, :]

    def spread(fn):
        full = jnp.where(on_rows[None, None, :], fn(ang_r)[:, None, :], fn(ang_c)[None, :, :])
        return full.reshape(s_lat, LANES)

    cos = jnp.concatenate([jnp.ones((CTX_LEN, LANES), F32), spread(jnp.cos)], axis=0)
    sin = jnp.concatenate([jnp.zeros((CTX_LEN, LANES), F32), spread(jnp.sin) * sign[None, :]], axis=0)
    return cos, sin


def _pad_rows(a, rows):
    return jnp.concatenate([a, jnp.zeros((rows - a.shape[0],) + a.shape[1:], a.dtype)], axis=0)


_SMALL = ("c_ctx", "b_mod", "g_pre", "g_post", "conv_a", "conv_b", "conv_b_bias", "conf_ln_g", "conf_ln_b",
          "sgu_ln_g", "sgu_ln_b", "w_s", "b_s", "q_gain", "k_gain")


def _pack(arrays):
    flat = jnp.concatenate([a.reshape(-1) for a in arrays])
    rows = -(-flat.shape[0] // (8 * LANES)) * 8
    return _pad_rows(flat.reshape(-1, 1), rows * LANES).reshape(rows, LANES)


def _unpack(packed, shapes):
    flat = packed.reshape(-1)
    out, off = [], 0
    for s in shapes:
        n = int(np.prod(s))
        out.append(flat[off:off + n].reshape(s))
        off += n
    return out


def kernel(x, c, ctx, c_ctx, w_mod, b_mod, g_pre, g_post, w_in, w_out, conv_a, conv_b, conv_b_bias, conf_ln_g, conf_ln_b, sgu_ln_g, sgu_ln_b, w_s, b_s, q_gain, k_gain, loss_target, m_c_ctx, m_w_mod, m_b_mod, m_g_pre, m_g_post, m_w_in, m_w_out, m_conv_a, m_conv_b, m_conv_b_bias, m_conf_ln_g, m_conf_ln_b, m_sgu_ln_g, m_sgu_ln_b, m_w_s, m_b_s, m_q_gain, m_k_gain, v_c_ctx, v_w_mod, v_b_mod, v_g_pre, v_g_post, v_w_in, v_w_out, v_conv_a, v_conv_b, v_conv_b_bias, v_conf_ln_g, v_conf_ln_b, v_sgu_ln_g, v_sgu_ln_b, v_w_s, v_b_s, v_q_gain, v_k_gain):
    weights = dict(c_ctx=c_ctx, w_mod=w_mod, b_mod=b_mod, g_pre=g_pre, g_post=g_post, w_in=w_in, w_out=w_out, conv_a=conv_a,
                   conv_b=conv_b, conv_b_bias=conv_b_bias, conf_ln_g=conf_ln_g, conf_ln_b=conf_ln_b, sgu_ln_g=sgu_ln_g,
                   sgu_ln_b=sgu_ln_b, w_s=w_s, b_s=b_s, q_gain=q_gain, k_gain=k_gain)
    m_in = dict(c_ctx=m_c_ctx, w_mod=m_w_mod, b_mod=m_b_mod, g_pre=m_g_pre, g_post=m_g_post, w_in=m_w_in, w_out=m_w_out,
                conv_a=m_conv_a, conv_b=m_conv_b, conv_b_bias=m_conv_b_bias, conf_ln_g=m_conf_ln_g, conf_ln_b=m_conf_ln_b,
                sgu_ln_g=m_sgu_ln_g, sgu_ln_b=m_sgu_ln_b, w_s=m_w_s, b_s=m_b_s, q_gain=m_q_gain, k_gain=m_k_gain)
    v_in = dict(c_ctx=v_c_ctx, w_mod=v_w_mod, b_mod=v_b_mod, g_pre=v_g_pre, g_post=v_g_post, w_in=v_w_in, w_out=v_w_out,
                conv_a=v_conv_a, conv_b=v_conv_b, conv_b_bias=v_conv_b_bias, conf_ln_g=v_conf_ln_g, conf_ln_b=v_conf_ln_b,
                sgu_ln_g=v_sgu_ln_g, sgu_ln_b=v_sgu_ln_b, w_s=v_w_s, b_s=v_b_s, q_gain=v_q_gain, k_gain=v_k_gain)
    order = ("c_ctx", "w_mod", "b_mod", "g_pre", "g_post", "w_in", "w_out", "conv_a", "conv_b", "conv_b_bias", "conf_ln_g",
             "conf_ln_b", "sgu_ln_g", "sgu_ln_b", "w_s", "b_s", "q_gain", "k_gain")

    s_lat = x.shape[1]
    ax, ay, ac = lax.axis_index("x"), lax.axis_index("y"), lax.axis_index("c")
    chip = 2 * ax + ay
    example = 4 * ax + 2 * ay + ac

    c_rows = _all_gather_rows(_pad_rows(c, 8))[::8]
    c16 = _pad_rows(jnp.concatenate([c_rows, c_ctx[None, :]], axis=0), 16)
    b_mod_shard = lax.dynamic_slice_in_dim(b_mod, chip * SHARD_MOD, SHARD_MOD, axis=1)[:, None, :]
    silu_c, mod_shard = _mod_forward(c16, w_mod, b_mod_shard)
    mod_all = _all_gather_rows(mod_shard.reshape(DEPTH * 16, SHARD_MOD)).reshape(8, DEPTH, 16, SHARD_MOD)
    mod_full = jnp.transpose(mod_all[::2], (1, 2, 0, 3)).reshape(DEPTH, 16, 3 * D_MODEL)
    mod_lat = lax.dynamic_index_in_dim(mod_full, example, axis=1, keepdims=False).reshape(DEPTH, 3, D_MODEL)
    mod_ctx = mod_full[:, 8].reshape(DEPTH, 3, D_MODEL)
    modv = jnp.concatenate([mod_ctx, mod_lat, jnp.zeros((DEPTH, 2, D_MODEL), F32)], axis=1)

    wi_b, wo_b = w_in.astype(BF16), w_out.astype(BF16)

    def regroup(gathered):
        wi_all, wo_all = gathered
        wi_full = jnp.concatenate([wi_all[k] for k in range(N_CHIPS)], axis=-1)
        wo_l = jnp.concatenate([wo_all[k] for k in range(N_CHIPS)], axis=0)
        wc_l = jnp.concatenate([wi_full[:, 256:768], wi_full[:, 1024:1536]], axis=-1)
        wr_l = jnp.concatenate([wi_full[:, 0:256], wi_full[:, 768:1024], wi_full[:, 1536:2560], wi_full[:, 3072:3328]], axis=-1)
        return wc_l, wr_l, wi_full[:, 2560:3072], wo_l

    w_c, w_r, w_q, wo_full = [None] * DEPTH, [None] * DEPTH, [None] * DEPTH, [None] * DEPTH
    w_c[0], w_r[0], w_q[0], wo_full[0] = regroup(_gather_weights(wi_b[0], wo_b[0]))

    cos_t, sin_t = _rope_tables(s_lat)
    conv_a_full = jnp.zeros((DEPTH, 8, GROUP_W), F32)
    conv_b_full = jnp.zeros((DEPTH, 32, GROUP_W), F32)
    conv_small = jnp.concatenate([conv_a.reshape(DEPTH * SHORT_CONV_K, -1), conv_b.reshape(DEPTH * CONFORMER_K, -1)], axis=0)
    n_cs = conv_small.shape[0]
    conv_rows = -(-n_cs // 8) * 8
    conv_all = _all_gather_rows(_pad_rows(conv_small, conv_rows)).reshape(8, conv_rows, -1)[::2]
    conv_all = jnp.transpose(conv_all, (1, 0, 2)).reshape(conv_rows, GROUP_W)
    conv_a_full = conv_a_full.at[:, :SHORT_CONV_K].set(conv_all[:DEPTH * SHORT_CONV_K].reshape(DEPTH, SHORT_CONV_K, GROUP_W))
    conv_b_full = conv_b_full.at[:, :CONFORMER_K].set(
        conv_all[DEPTH * SHORT_CONV_K:n_cs].reshape(DEPTH, CONFORMER_K, GROUP_W))

    vecs = jnp.stack([conv_b_bias, conf_ln_g, conf_ln_b, sgu_ln_g, sgu_ln_b] + [jnp.zeros_like(conv_b_bias)] * 3, axis=1)
    wss = w_s.reshape(DEPTH, N_SPATIAL_GROUPS * CHUNK, CHUNK).astype(BF16)
    wsts = jnp.swapaxes(w_s, 2, 3).reshape(DEPTH, N_SPATIAL_GROUPS * CHUNK, CHUNK).astype(BF16)
    bsm = jnp.repeat(jnp.swapaxes(b_s, 1, 2), HEAD_DIM, axis=2)
    qk_gain = jnp.concatenate([jnp.tile(q_gain, (1, 2))[:, None, :], jnp.tile(k_gain, (1, 2))[:, None, :],
                               jnp.zeros((DEPTH, 6, LANES), F32)], axis=1)

    xt = jnp.concatenate([ctx[0], x[0]], axis=0)
    saved = []
    for l in range(DEPTH):
        h, pc, pr, pq, q, k, v = _in_proj(xt, modv[l], g_pre[l][None, :], w_c[l], w_r[l], w_q[l], qk_gain[l], cos_t, sin_t)
        if l + 1 < DEPTH:
            oe, gathered = _attention_fwd(q, k, v, (wi_b[l + 1], wo_b[l + 1]))
            w_c[l + 1], w_r[l + 1], w_q[l + 1], wo_full[l + 1] = regroup(gathered)
        else:
            oe, _ = _attention_fwd(q, k, v)
        mixed = _mix_out(pc, pr, oe, xt, modv[l], g_post[l][None, :], wo_full[l], conv_a_full[l], conv_b_full[l],
                         vecs[l], wss[l], bsm[l], loss_target[0] if l + 1 == DEPTH else None)
        x_new, y, ca, z2 = mixed[:4]
        saved.append(dict(x=xt, h=h, pc=pc, pr=pr, pq=pq, q=q, k=k, v=v, oe=oe, y=y, ca=ca, z2=z2))
        xt = x_new
    dxo = xt
    loss = lax.psum(mixed[4][0, 0], ("x", "y", "c"))

    g_small = {n: [None] * DEPTH for n in _SMALL}
    d_mod, landed = [None] * DEPTH, [None] * DEPTH
    slabs = None
    for l in reversed(range(DEPTH)):
        s = saved[l]
        dpr, g_a, g_b, doe, gw_o, pvec, s256, dws, dbs = _mix_out_bwd(
            dxo, s["y"], s["pr"], s["ca"], s["z2"], s["oe"], modv[l], g_post[l][None, :], wo_full[l], vecs[l], wss[l], wsts[l], bsm[l])
        dpc, dca, dcb = _conv_bwd(s["pc"], g_a, g_b, conv_a_full[l], conv_b_full[l])
        dq, dk, dv, got = _attention_bwd(s["q"], s["k"], s["v"], doe, slabs)
        if slabs is not None:
            landed[l + 1] = got
        dxo, dpq, acc, dgain = _in_proj_bwd(dpc, dpr, dq, dk, dv, s["pq"], qk_gain[l], cos_t, sin_t, w_c[l], w_r[l], w_q[l],
                                            s["x"], dxo, modv[l], g_pre[l][None, :])
        gw_c, gw_r, gw_q = _in_proj_wgrad(s["h"], dpc, dpr, dpq)
        gw_in = jnp.concatenate([gw_r[:, 0:256], gw_c[:, 0:512], gw_r[:, 256:512], gw_c[:, 512:1024],
                                 gw_r[:, 512:1536], gw_q, gw_r[:, 1536:1792]], axis=-1)
        slabs = (jnp.transpose(gw_in.reshape(D_MODEL, N_CHIPS, SHARD_IN), (1, 0, 2)).astype(BF16),
                 gw_o.reshape(N_CHIPS, SHARD_OUT, D_MODEL).astype(BF16))
        d_mod[l] = jnp.stack([jnp.concatenate([acc[2], acc[3], pvec[1]]), jnp.concatenate([acc[0], acc[1], pvec[0]])])
        g_small["g_pre"][l] = acc[4]
        g_small["g_post"][l] = pvec[2]
        g_small["conv_a"][l] = dca[:SHORT_CONV_K]
        g_small["conv_b"][l] = dcb[:CONFORMER_K]
        g_small["conv_b_bias"][l] = s256[0]
        g_small["conf_ln_g"][l] = s256[1]
        g_small["conf_ln_b"][l] = s256[2]
        g_small["sgu_ln_g"][l] = s256[3]
        g_small["sgu_ln_b"][l] = s256[4]
        g_small["w_s"][l] = dws.reshape(N_SPATIAL_GROUPS, CHUNK, CHUNK)
        g_small["b_s"][l] = jnp.transpose(dbs[:, :N_SPATIAL_GROUPS])
        g_small["q_gain"][l] = dgain[0, :HEAD_DIM] + dgain[0, HEAD_DIM:]
        g_small["k_gain"][l] = dgain[1, :HEAD_DIM] + dgain[1, HEAD_DIM:]
    grad_x = dxo[CTX_LEN:][None]

    d_mod_all = _all_gather_rows(jnp.stack(d_mod).reshape(DEPTH * 2, 3 * D_MODEL)).reshape(8, DEPTH, 2, 3 * D_MODEL)
    d_lat = jnp.transpose(d_mod_all[:, :, 0], (1, 0, 2))
    d_ctx = jnp.transpose(d_mod_all[:, :, 1], (1, 0, 2))
    cols = lambda a: lax.dynamic_slice_in_dim(a.reshape(DEPTH, 8, N_CHIPS, SHARD_MOD), chip, 1, axis=2)[:, :, 0]
    silu_t = jnp.transpose(silu_c)
    s_t = jnp.concatenate([silu_t[:, 0:8], jnp.tile(silu_t[:, 8:9], (1, 8)), jnp.zeros((D_MODEL, LANES - 16), F32)], axis=1)
    g_rows = jnp.concatenate([cols(d_lat), cols(d_ctx), jnp.zeros((DEPTH, LANES - 16, SHARD_MOD), F32)], axis=1)
    g_w_mod, g_b_mod, c_ctx_part = _mod_backward(s_t, g_rows, cols(d_ctx), jnp.concatenate([d_lat, d_ctx], axis=1),
                                                 w_mod, c_ctx[:, None])

    for n in _SMALL:
        if n not in ("c_ctx", "b_mod"):
            g_small[n] = jnp.stack(g_small[n])
    small_parts = [0.5 * c_ctx_part[:, 0]] + [g_small[n] for n in _SMALL[2:]]
    packed = _pack(small_parts)
    gathered = _all_gather_rows(packed).reshape(8, packed.shape[0], LANES)
    small_sum = _sum_slabs(gathered, packed.shape[0])
    small_g = dict(zip(("c_ctx",) + _SMALL[2:], _unpack(small_sum, [p.shape for p in small_parts])))
    small_g["b_mod"] = g_b_mod[:, 0]
    ch64 = GROUP_W // N_CHIPS
    for n in ("conv_a", "conv_b"):
        small_g[n] = lax.dynamic_slice_in_dim(small_g[n], chip * ch64, ch64, axis=2)
    sw = _pack([weights[n] for n in _SMALL])
    sm = _pack([m_in[n] for n in _SMALL])
    sv = _pack([v_in[n] for n in _SMALL])
    sg = _pack([small_g[n] for n in _SMALL])
    shapes = [weights[n].shape for n in _SMALL]
    small_out = [dict(zip(_SMALL, _unpack(o, shapes))) for o in _adamw([sg], sw, sm, sv, sg.shape[0])]

    landed[0] = _scatter_slabs(*slabs)
    sum_in = _sum_layer_slabs([landed[l][0] for l in range(DEPTH)], 512)
    sum_out = _sum_layer_slabs([landed[l][1] for l in range(DEPTH)], 256)
    sib_in, sib_out = _swap_with_sibling(sum_in, sum_out)

    big = {}
    flat = lambda a: a.reshape(-1, a.shape[-1])
    for n, grads, rows in (("w_in", [sum_in, sib_in], 512), ("w_out", [sum_out, sib_out], 256), ("w_mod", [flat(g_w_mod)], 512)):
        outs = _adamw(grads, flat(weights[n]), flat(m_in[n]), flat(v_in[n]), rows)
        big[n] = [o.reshape(weights[n].shape) for o in outs]

    def leaf(n, j):
        return big[n][j] if n in big else small_out[j][n]

    return (loss, grad_x, *[leaf(n, 0) for n in order], *[leaf(n, 1) for n in order],
            *[leaf(n, 2) for n in order], *[leaf(n, 3) for n in order])
```

```python
import functools

import numpy as np
import jax
import jax.numpy as jnp
from jax import lax
from jax.experimental import pallas as pl
from jax.experimental.pallas import tpu as pltpu

F32 = jnp.float32
BF16 = jnp.bfloat16
MESH = pl.DeviceIdType.MESH

D_MODEL = 1024
DEPTH = 4
GRID_W = 64
CTX_LEN = 256
GROUP_W = 256
HEAD_DIM = 64
N_Q_HEADS = 4
N_KV_HEADS = 2
GQA = N_Q_HEADS // N_KV_HEADS
ROPE_THETA = 10000.0
ATTN_SCALE = HEAD_DIM ** -0.5
SHORT_CONV_K = 3
CONFORMER_K = 31
CHUNK = 128
N_SPATIAL_GROUPS = 4
RMS_EPS = 1e-6
LN_EPS = 1e-5
ADAM_LR = 0.001
ADAM_B1 = 0.9
ADAM_B2 = 0.999
ADAM_EPS = 1e-08
ADAM_WD = 0.01
ADAM_STEP = 10

LANES = 128
HALO = 16
CONV_ROWS = 64
TM = 256
N_CTX_TILES = CTX_LEN // TM
SUB = 3
TB = SUB * TM
W_C = 1024
W_R = 1792
W_Q = 512
PROJ_W = W_C + W_R + W_Q
N_CHIPS = 4
SHARD_IN = PROJ_W // N_CHIPS
SHARD_OUT = D_MODEL // N_CHIPS
SHARD_MOD = 3 * D_MODEL // N_CHIPS
VMEM_LIMIT = 56 * 1024 * 1024


def _pc(body, **kw):
    return pl.pallas_call(body, **kw)


def _cparams(**kw):
    return pltpu.CompilerParams(dimension_semantics=("arbitrary",), vmem_limit_bytes=VMEM_LIMIT, **kw)


def _full(shape):
    n = len(shape)
    return pl.BlockSpec(shape, lambda i: (0,) * n)


def _const(shape):
    n = len(shape)
    return pl.BlockSpec(shape, lambda i: (0,) * n, pipeline_mode=pl.Buffered(1))


def _rows(width, tm=TM):
    return pl.BlockSpec((tm, width), lambda i: (i, 0))


def _heads(nh, width, tm=TM):
    return pl.BlockSpec((nh, tm, width), lambda i: (0, i, 0))


def _sigmoid(x):
    return jax.nn.sigmoid(x)


def _dot(a, b):
    return jnp.dot(a, b, preferred_element_type=F32)


def _dot_nt(a, b):
    return lax.dot_general(a, b, (((1,), (1,)), ((), ())), preferred_element_type=F32)


def _dot_tn(a, b):
    return lax.dot_general(a, b, (((0,), (0,)), ((), ())), preferred_element_type=F32)


def _lane(rows):
    return lax.broadcasted_iota(jnp.int32, (rows, LANES), 1)


def _rowsum(x):
    return jnp.sum(x, axis=1, keepdims=True)


def _colsum(x):
    return jnp.sum(x, axis=0, keepdims=True)


def _pair_sums(x, lo):
    s0 = _rowsum(jnp.where(lo, x, 0.0))
    s1 = _rowsum(jnp.where(lo, 0.0, x))
    return jnp.where(lo, s0, s1)


def _swap16(x, lo16):
    return jnp.where(lo16, pltpu.roll(x, LANES - 16, 1), pltpu.roll(x, 16, 1))


def _layer_norm_stats(x):
    mu = jnp.mean(x, axis=1, keepdims=True)
    xc = x - mu
    rs = lax.rsqrt(jnp.mean(xc * xc, axis=1, keepdims=True) + LN_EPS)
    return xc * rs, rs


def _layer_norm_bwd(dxn, xn, rs):
    return rs * (dxn - jnp.mean(dxn, axis=1, keepdims=True) - xn * jnp.mean(dxn * xn, axis=1, keepdims=True))


def _group_select(r, grp):
    out = jnp.where(grp == 0, r[0:CHUNK], 0.0)
    for g in range(1, N_SPATIAL_GROUPS):
        out = out + jnp.where(grp == g, r[g * CHUNK:(g + 1) * CHUNK], 0.0)
    return out


def _kv_chunk(t):
    return 1024 if (t - CTX_LEN) % 1024 == 0 else 256


def _all_gather_rows(x_shard):
    m_per, n = x_shard.shape

    def body(x_ref, out_ref, send_sems, recv_sems, local_sem):
        x, y, c = lax.axis_index("x"), lax.axis_index("y"), lax.axis_index("c")
        me, sibling = (x, y, c), (x, y, 1 - c)
        chips = [(1 - x, y), (x, 1 - y), (1 - x, 1 - y)]

        def rows(px, py, pc):
            return out_ref.at[pl.ds((4 * px + 2 * py + pc) * m_per, m_per), :]

        def copy(k, block, to, src=None):
            return pltpu.make_async_remote_copy(
                src_ref=rows(*block) if src is None else src, dst_ref=rows(*block),
                send_sem=send_sems.at[k], recv_sem=recv_sems.at[k], device_id=to, device_id_type=MESH)

        mine = pltpu.make_async_copy(x_ref, rows(*me), local_sem)
        mine.start()
        first = [copy(0, me, sibling, src=x_ref)]
        first += [copy(1 + j, me, (*chip, c), src=x_ref) for j, chip in enumerate(chips)]
        for cp in first:
            cp.start()
        passed = [copy(4 + j, (*chip, c), sibling) for j, chip in enumerate(chips)]
        for j, chip in enumerate(chips):
            copy(1 + j, (*chip, c), me).wait_recv()
            passed[j].start()
        copy(0, sibling, me).wait_recv()
        for j, chip in enumerate(chips):
            copy(4 + j, (*chip, 1 - c), me).wait_recv()
        for cp in first + passed:
            cp.wait_send()
        mine.wait()

    return _pc(
        body, name="all_gather_rows",
        out_shape=jax.ShapeDtypeStruct((8 * m_per, n), x_shard.dtype),
        in_specs=[pl.BlockSpec(memory_space=pltpu.VMEM)],
        out_specs=pl.BlockSpec(memory_space=pltpu.VMEM),
        scratch_shapes=[pltpu.SemaphoreType.DMA((7,)), pltpu.SemaphoreType.DMA((7,)), pltpu.SemaphoreType.DMA],
        compiler_params=pltpu.CompilerParams(vmem_limit_bytes=VMEM_LIMIT),
    )(x_shard)


def _place():
    x, y, c = lax.axis_index("x"), lax.axis_index("y"), lax.axis_index("c")
    return x, y, c, [(1 - x, y), (x, 1 - y), (1 - x, 1 - y)]


def _remote(src, dst, send_sems, recv_sems, k, to):
    return pltpu.make_async_remote_copy(src_ref=src, dst_ref=dst, send_sem=send_sems.at[k], recv_sem=recv_sems.at[k],
                                        device_id=to, device_id_type=MESH)


GATHER_SEMS = 6
SCATTER_SEMS = 3


def _gather_phase(phase, pairs, send_sems, recv_sems, local_sems):
    x, y, c, chips = _place()
    kme = 2 * x + y
    sibling = (x, y, 1 - c)
    for a, (src, dst) in enumerate(pairs):
        half = src.shape[0] // 2
        mine = pl.ds(c * half, half)
        theirs = pl.ds((1 - c) * half, half)
        if phase == 0:
            pltpu.make_async_copy(src, dst.at[kme], local_sems.at[a]).start()
        if phase == 2:
            pltpu.make_async_copy(src, dst.at[kme], local_sems.at[a]).wait()
        for j, (px, py) in enumerate(chips):
            kk = 2 * px + py
            landed = dst.at[kk, mine]
            out = lambda: _remote(src.at[mine], dst.at[kme, mine], send_sems, recv_sems, 6 * a + j, (px, py, c))
            hand = lambda: _remote(landed, landed, send_sems, recv_sems, 6 * a + 3 + j, sibling)
            if phase == 0:
                out().start()
            if phase == 1:
                _remote(landed, landed, send_sems, recv_sems, 6 * a + j, (px, py, c)).wait_recv()
                hand().start()
            if phase == 2:
                other = dst.at[kk, theirs]
                _remote(other, other, send_sems, recv_sems, 6 * a + 3 + j, sibling).wait_recv()
                out().wait_send()
                hand().wait_send()


def _scatter_phase(phase, pairs, send_sems, recv_sems, local_sems):
    x, y, c, chips = _place()
    kme = 2 * x + y
    for a, (src, dst) in enumerate(pairs):
        loc = pltpu.make_async_copy(src.at[kme], dst.at[kme], local_sems.at[a])
        if phase == 0:
            loc.start()
        else:
            loc.wait()
        for j, (px, py) in enumerate(chips):
            kk = 2 * px + py
            out = _remote(src.at[kk], dst.at[kme], send_sems, recv_sems, 3 * a + j, (px, py, c))
            if phase == 0:
                out.start()
            else:
                landed = dst.at[kk]
                _remote(landed, landed, send_sems, recv_sems, 3 * a + j, (px, py, c)).wait_recv()
                out.wait_send()


def _comm_scratch(per_array, n_arrays):
    n = per_array * n_arrays
    return [pltpu.SemaphoreType.DMA((n,)), pltpu.SemaphoreType.DMA((n,)), pltpu.SemaphoreType.DMA((n_arrays,))]


def _slots(a):
    return jax.ShapeDtypeStruct((N_CHIPS,) + a.shape, a.dtype)


def _gather_weights(shards):
    n = len(shards)

    def body(*refs):
        for phase in range(3):
            _gather_phase(phase, tuple(zip(refs[:n], refs[n:2 * n])), *refs[2 * n:])

    hbm = pl.BlockSpec(memory_space=pl.ANY)
    return _pc(
        body, name="gather_weights", out_shape=tuple(_slots(a) for a in shards),
        in_specs=[hbm] * n, out_specs=(hbm,) * n, scratch_shapes=_comm_scratch(GATHER_SEMS, n),
    )(*shards)


def _scatter_slabs(slabs):
    n = len(slabs)

    def body(*refs):
        for phase in range(2):
            _scatter_phase(phase, tuple(zip(refs[:n], refs[n:2 * n])), *refs[2 * n:])

    hbm = pl.BlockSpec(memory_space=pl.ANY)
    return _pc(
        body, name="scatter_slabs", out_shape=tuple(jax.ShapeDtypeStruct(a.shape, a.dtype) for a in slabs),
        in_specs=[hbm] * n, out_specs=(hbm,) * n, scratch_shapes=_comm_scratch(SCATTER_SEMS, n),
    )(*slabs)


def _swap_with_sibling(a, b):
    def body(a_ref, b_ref, ra_ref, rb_ref, send_sems, recv_sems):
        x, y, c = lax.axis_index("x"), lax.axis_index("y"), lax.axis_index("c")
        copies = []
        for k, (src, dst) in enumerate(((a_ref, ra_ref), (b_ref, rb_ref))):
            cp = pltpu.make_async_remote_copy(
                src_ref=src, dst_ref=dst, send_sem=send_sems.at[k], recv_sem=recv_sems.at[k],
                device_id=(x, y, 1 - c), device_id_type=MESH)
            cp.start()
            copies.append(cp)
        for cp in copies:
            cp.wait()

    hbm = pl.BlockSpec(memory_space=pl.ANY)
    return _pc(
        body, name="swap_with_sibling",
        out_shape=(jax.ShapeDtypeStruct(a.shape, a.dtype), jax.ShapeDtypeStruct(b.shape, b.dtype)),
        in_specs=[hbm, hbm], out_specs=(hbm, hbm),
        scratch_shapes=[pltpu.SemaphoreType.DMA((2,)), pltpu.SemaphoreType.DMA((2,))],
    )(a, b)


def _mod_forward(c16, w_mod, b_mod_shard):
    def body(c_ref, w_ref, b_ref, s_ref, o_ref):
        cc = c_ref[...]
        s = cc * _sigmoid(cc)
        s_ref[...] = s
        o_ref[0] = jnp.dot(s, w_ref[0], preferred_element_type=F32, precision=lax.Precision.HIGHEST) + b_ref[0]

    return _pc(
        body, name="mod_forward", grid=(DEPTH,),
        out_shape=(jax.ShapeDtypeStruct((16, D_MODEL), F32), jax.ShapeDtypeStruct((DEPTH, 16, SHARD_MOD), F32)),
        in_specs=[_full((16, D_MODEL)),
                  pl.BlockSpec((1, D_MODEL, SHARD_MOD), lambda l: (l, 0, 0)),
                  pl.BlockSpec((1, 1, SHARD_MOD), lambda l: (l, 0, 0))],
        out_specs=(_full((16, D_MODEL)), pl.BlockSpec((1, 16, SHARD_MOD), lambda l: (l, 0, 0))),
        compiler_params=_cparams(),
    )(c16, w_mod, b_mod_shard)


def _mod_backward(s_t, g_rows, g_ctx, d_all, w_mod, c_ctx_col):
    def body(st_ref, g_ref, gc_ref, d_ref, w_ref, cc_ref, gw_ref, gb_ref, pc_ref):
        l = pl.program_id(0)
        gw_ref[0] = jnp.dot(st_ref[...], g_ref[0], preferred_element_type=F32, precision=lax.Precision.HIGHEST)
        gb_ref[0] = _colsum(d_ref[0])
        part = _rowsum(w_ref[0] * _colsum(gc_ref[0]))

        @pl.when(l == 0)
        def _():
            pc_ref[...] = jnp.zeros_like(pc_ref)

        pc_ref[...] += part

        @pl.when(l == DEPTH - 1)
        def _():
            cc = cc_ref[...]
            sg = _sigmoid(cc)
            pc_ref[...] = pc_ref[...] * (sg * (1.0 + cc * (1.0 - sg)))

    return _pc(
        body, name="mod_backward", grid=(DEPTH,),
        out_shape=(jax.ShapeDtypeStruct((DEPTH, D_MODEL, SHARD_MOD), F32),
                   jax.ShapeDtypeStruct((DEPTH, 1, 3 * D_MODEL), F32),
                   jax.ShapeDtypeStruct((D_MODEL, 1), F32)),
        in_specs=[_full((D_MODEL, LANES)),
                  pl.BlockSpec((1, LANES, SHARD_MOD), lambda l: (l, 0, 0)),
                  pl.BlockSpec((1, 8, SHARD_MOD), lambda l: (l, 0, 0)),
                  pl.BlockSpec((1, 16, 3 * D_MODEL), lambda l: (l, 0, 0)),
                  pl.BlockSpec((1, D_MODEL, SHARD_MOD), lambda l: (l, 0, 0)),
                  _full((D_MODEL, 1))],
        out_specs=(pl.BlockSpec((1, D_MODEL, SHARD_MOD), lambda l: (l, 0, 0)),
                   pl.BlockSpec((1, 1, 3 * D_MODEL), lambda l: (l, 0, 0)),
                   _full((D_MODEL, 1))),
        compiler_params=_cparams(),
    )(s_t, g_rows, g_ctx, d_all, w_mod, c_ctx_col)


def _head_norm(xb, lo):
    r = lax.rsqrt(_pair_sums(xb * xb, lo) * (1.0 / HEAD_DIM) + RMS_EPS)
    return xb * r, r


def _in_proj(xt, modv, g_pre, w_c, w_r, w_q, qk_gain, cos_t, sin_t):
    t = xt.shape[0]

    def body(x_ref, mod_ref, g_ref, wc_ref, wr_ref, wq_ref, gain_ref, cos_ref, sin_ref,
             h_ref, pc_ref, pr_ref, pq_ref, q_ref, k_ref, v_ref):
        lane = _lane(TM)
        lo = lane < HEAD_DIM
        lo16 = (lane & 31) < 16
        one = jnp.where(lane == HEAD_DIM, 1.0, 0.0)
        for jj in range(SUB):
            rows = pl.ds(jj * TM, TM)
            is_ctx = pl.program_id(0) * SUB + jj < N_CTX_TILES
            x = x_ref[rows, :]
            r = lax.rsqrt(jnp.mean(x * x, axis=1, keepdims=True) + RMS_EPS)
            sh = jnp.where(is_ctx, mod_ref[0:1, :], mod_ref[3:4, :])
            sc = jnp.where(is_ctx, mod_ref[1:2, :], mod_ref[4:5, :])
            h = (x * r * g_ref[...]) * (1.0 + sc) + sh
            hb = h.astype(BF16)
            h_ref[rows, :] = hb
            pc_ref[rows, :] = _dot(hb, wc_ref[...])
            pr_ref[rows, :] = _dot(hb, wr_ref[...])
            pq = _dot(hb, wq_ref[...])
            pq_ref[rows, :] = pq
            cos = cos_ref[rows, :]
            sin = sin_ref[rows, :]
            for b in range(3):
                xh, _ = _head_norm(pq[:, b * LANES:(b + 1) * LANES], lo)
                xg = xh * (gain_ref[0:1, :] if b < 2 else gain_ref[1:2, :])
                rot = xg * cos + _swap16(xg, lo16) * sin
                if b < 2:
                    rot = rot * ATTN_SCALE
                dst = q_ref if b < 2 else k_ref
                base = 2 * b if b < 2 else 0
                dst[base, rows, :] = jnp.where(lo, rot, 0.0).astype(BF16)
                dst[base + 1, rows, :] = jnp.where(lo, pltpu.roll(rot, HEAD_DIM, 1), 0.0).astype(BF16)
            vb = pq[:, 3 * LANES:4 * LANES]
            v_ref[0, rows, :] = jnp.where(lo, vb, one).astype(BF16)
            v_ref[1, rows, :] = jnp.where(lo, pltpu.roll(vb, HEAD_DIM, 1), one).astype(BF16)

    return _pc(
        body, name="in_proj", grid=(t // TB,),
        out_shape=(jax.ShapeDtypeStruct((t, D_MODEL), BF16),
                   jax.ShapeDtypeStruct((t, W_C), F32), jax.ShapeDtypeStruct((t, W_R), F32), jax.ShapeDtypeStruct((t, W_Q), F32),
                   jax.ShapeDtypeStruct((N_Q_HEADS, t, LANES), BF16),
                   jax.ShapeDtypeStruct((N_KV_HEADS, t, LANES), BF16),
                   jax.ShapeDtypeStruct((N_KV_HEADS, t, LANES), BF16)),
        in_specs=[_rows(D_MODEL, TB), _const((8, D_MODEL)), _const((1, D_MODEL)),
                  _const((D_MODEL, W_C)), _const((D_MODEL, W_R)), _const((D_MODEL, W_Q)),
                  _const((8, LANES)), _rows(LANES, TB), _rows(LANES, TB)],
        out_specs=(_rows(D_MODEL, TB), _rows(W_C, TB), _rows(W_R, TB), _rows(W_Q, TB),
                   _heads(N_Q_HEADS, LANES, TB), _heads(N_KV_HEADS, LANES, TB), _heads(N_KV_HEADS, LANES, TB)),
        compiler_params=_cparams(),
    )(xt, modv, g_pre, w_c, w_r, w_q, qk_gain, cos_t, sin_t)


def _attention_fwd(q, k, v, shards=None):
    t = q.shape[1]
    tk = _kv_chunk(t)
    n_chunks = (t - CTX_LEN) // tk
    n_tiles = t // TM
    n_sh = 0 if shards is None else len(shards)

    def body(q_ref, k_ref, v_ref, *rest):
        i = pl.program_id(0)
        o_ref = rest[n_sh]
        if shards is not None:
            pairs = tuple(zip(rest[:n_sh], rest[n_sh + 1:2 * n_sh + 1]))
            for phase, at in enumerate((0, n_tiles // 2, n_tiles - 1)):
                @pl.when(i == at)
                def _(phase=phase):
                    _gather_phase(phase, pairs, *rest[2 * n_sh + 1:])
        lane = _lane(GQA * TM)
        qs = [jnp.concatenate([q_ref[GQA * g + hh] for hh in range(GQA)], axis=0) for g in range(N_KV_HEADS)]

        def step(st, size, carry):
            out = []
            for g in range(N_KV_HEADS):
                m, acc = carry[g]
                s = _dot_nt(qs[g], k_ref[g, pl.ds(st, size), :])
                m_new = jnp.maximum(m, jnp.max(s, axis=1, keepdims=True))
                p = jnp.exp(s - m_new)
                out.append((m_new, acc * jnp.exp(m - m_new) + _dot(p.astype(BF16), v_ref[g, pl.ds(st, size), :])))
            return tuple(out)

        init = tuple((jnp.full((GQA * TM, 1), -jnp.inf, F32), jnp.zeros((GQA * TM, LANES), F32)) for _ in range(N_KV_HEADS))

        def finish(carry):
            for g in range(N_KV_HEADS):
                m, acc = carry[g]
                den = _rowsum(jnp.where(lane == HEAD_DIM, acc, 0.0))
                out = jnp.where(lane < HEAD_DIM, acc * (1.0 / den), jnp.where(lane == HEAD_DIM, m + jnp.log(den), 0.0))
                for hh in range(GQA):
                    o_ref[GQA * g + hh] = out[hh * TM:(hh + 1) * TM]

        @pl.when(i < N_CTX_TILES)
        def _():
            finish(step(0, CTX_LEN, init))

        @pl.when(i >= N_CTX_TILES)
        def _():
            per = 4 if n_chunks % 4 == 0 else 1

            def trip(j, cr):
                st = pl.multiple_of(CTX_LEN + j * (per * tk), 256)
                for u in range(per):
                    cr = step(st + u * tk, tk, cr)
                return cr

            finish(lax.fori_loop(0, n_chunks // per, trip, step(0, CTX_LEN, init)))

    hbm = pl.BlockSpec(memory_space=pl.ANY)
    extra = () if shards is None else tuple(shards)
    outs = _pc(
        body, name="attention_fwd" if shards is None else "attention_fwd_gather", grid=(n_tiles,),
        out_shape=(jax.ShapeDtypeStruct((N_Q_HEADS, t, LANES), F32),) + tuple(_slots(a) for a in extra),
        in_specs=[_heads(N_Q_HEADS, LANES), _full((N_KV_HEADS, t, LANES)), _full((N_KV_HEADS, t, LANES))] + [hbm] * n_sh,
        out_specs=(_heads(N_Q_HEADS, LANES),) + (hbm,) * n_sh,
        scratch_shapes=_comm_scratch(GATHER_SEMS, n_sh) if shards is not None else [],
        compiler_params=_cparams(),
    )(q, k, v, *extra)
    return outs[0], tuple(outs[1:])


def _halo_specs(width, t, rows=TM):
    last = t // HALO - 1
    per = rows // HALO
    prev = pl.BlockSpec((HALO, width), lambda i: (jnp.maximum(i * per - 1, 0), 0))
    nxt = pl.BlockSpec((HALO, width), lambda i: (jnp.minimum((i + 1) * per, last), 0))
    return prev, nxt


def _halo_valid(i, n_tiles):
    prev_ok = jnp.logical_and(i != 0, i != N_CTX_TILES)
    next_ok = jnp.logical_and(i != N_CTX_TILES - 1, i != n_tiles - 1)
    return jnp.where(prev_ok, 1.0, 0.0), jnp.where(next_ok, 1.0, 0.0)


def _conv_inputs(pc):
    u = pc[:, 0:GROUP_W] * pc[:, GROUP_W:2 * GROUP_W]
    z = pc[:, 2 * GROUP_W:3 * GROUP_W] * _sigmoid(pc[:, 3 * GROUP_W:4 * GROUP_W])
    return u, z


def _fill_ext(ext_ref, prev, mid, nxt):
    ext_ref[0:HALO, :] = prev
    ext_ref[HALO:HALO + TM, :] = mid
    ext_ref[HALO + TM:HALO + TM + HALO, :] = nxt


def _row_local_mixers(pr, ca, z2, oe, vecs, wss_ref, bsm, lane256):
    a_b, a_g, b_g = pr[:, 0:256], pr[:, 256:512], pr[:, 512:768]
    c_u, c_v, c_g, d_g = pr[:, 768:1024], pr[:, 1024:1280], pr[:, 1280:1536], pr[:, 1536:1792]
    zn, rs_b = _layer_norm_stats(z2)
    tb = zn * vecs[1:2, :] + vecs[2:3, :]
    vn_hat, rs_c = _layer_norm_stats(c_v)
    vn = vn_hat * vecs[3:4, :] + vecs[4:5, :]
    grp = jnp.right_shift(lane256, 6)
    sgs = []
    for ch in range(TM // CHUNK):
        r = _dot(wss_ref[...], vn[ch * CHUNK:(ch + 1) * CHUNK, :].astype(BF16))
        sgs.append(_group_select(r, grp[0:CHUNK]) + bsm)
    sg = jnp.concatenate(sgs, axis=0)
    lane = _lane(TM)
    lo = lane < HEAD_DIM
    att = jnp.concatenate([jnp.where(lo, oe[2 * b], pltpu.roll(oe[2 * b + 1], HEAD_DIM, 1)) for b in range(2)], axis=1)
    return dict(a_b=a_b, a_g=a_g, b_g=b_g, c_u=c_u, c_v=c_v, c_g=c_g, d_g=d_g, zn=zn, rs_b=rs_b, tb=tb,
                vn_hat=vn_hat, rs_c=rs_c, vn=vn, sg=sg, att=att, grp=grp, lo=lo, lane=lane)


def _mixer_concat(f, ca):
    ya = f["a_b"] * ca
    yb = f["tb"] * _sigmoid(f["tb"])
    yc = f["c_u"] * f["sg"]
    gates = [f[n] * _sigmoid(f[n]) for n in ("a_g", "b_g", "c_g", "d_g")]
    ys = (ya, yb, yc, f["att"])
    big = jnp.concatenate([yy * gg for yy, gg in zip(ys, gates)], axis=1).astype(BF16)
    return big, ys, gates


def _taps31(ext_ref, w_ref, flip):
    blocks = []
    for r0 in range(0, TM, CONV_ROWS):
        out = None
        for b in range(8):
            part = None
            for a in range(4):
                o = 8 * a + b
                if 1 <= o <= CONFORMER_K:
                    kk = CONFORMER_K - o if flip else o - 1
                    term = w_ref[kk:kk + 1, :] * ext_ref[pl.ds(r0 + 8 * a, CONV_ROWS + 8), :]
                    part = term if part is None else part + term
            part = part[b:b + CONV_ROWS]
            out = part if out is None else out + part
        blocks.append(out)
    return jnp.concatenate(blocks, axis=0)


def _mix_out(pc, pr, oe, xt, modv, g_post, w_out, conv_a, conv_b, vecs, wss, bsm, target=None):
    t = xt.shape[0]
    n_tiles = t // TM
    prev_spec, next_spec = _halo_specs(W_C, t, TB)
    n_t = 0 if target is None else SUB

    def body(pc_ref, pp_ref, pn_ref, pr_ref, oe_ref, x_ref, mod_ref, gp_ref, wo_ref, cva_ref, cvb_ref, vec_ref, wss_ref, bsm_ref,
             *rest):
        xo_ref, y_ref, ca_ref, z2_ref = rest[n_t:n_t + 4]
        uext, zext = rest[-2:]
        i = pl.program_id(0)
        vecs = vec_ref[...]
        lane256 = lax.broadcasted_iota(jnp.int32, (TM, GROUP_W), 1)
        if target is not None:
            loss_ref = rest[n_t + 4]

            @pl.when(i == 0)
            def _():
                loss_ref[...] = jnp.zeros_like(loss_ref)

        for jj in range(SUB):
            rows = pl.ds(jj * TM, TM)
            tile = i * SUB + jj
            is_ctx = tile < N_CTX_TILES
            pv, nv = _halo_valid(tile, n_tiles)
            u, z = _conv_inputs(pc_ref[rows, :])
            up, zp = _conv_inputs(pp_ref[...] if jj == 0 else pc_ref[pl.ds(jj * TM - HALO, HALO), :])
            un, zn_ = _conv_inputs(pn_ref[...] if jj == SUB - 1 else pc_ref[pl.ds((jj + 1) * TM, HALO), :])
            ue, ze = uext.at[jj], zext.at[jj]
            _fill_ext(ue, up * pv, u, un * nv)
            _fill_ext(ze, zp * pv, z, zn_ * nv)
            ca = cva_ref[0:1, :] * ue[pl.ds(HALO - 1, TM), :]
            for kk in range(1, SHORT_CONV_K):
                ca = ca + cva_ref[kk:kk + 1, :] * ue[pl.ds(HALO - 1 + kk, TM), :]
            z2 = _taps31(ze, cvb_ref, False) + vecs[0:1, :]
            ca_ref[rows, :] = ca
            z2_ref[rows, :] = z2
            oes = [oe_ref[h, rows, :] for h in range(N_Q_HEADS)]
            f = _row_local_mixers(pr_ref[rows, :], ca, z2, oes, vecs, wss_ref, bsm_ref[...], lane256)
            big, _, _ = _mixer_concat(f, ca)
            y = _dot(big, wo_ref[...])
            y_ref[rows, :] = y
            ry = lax.rsqrt(jnp.mean(y * y, axis=1, keepdims=True) + RMS_EPS)
            gt = jnp.where(is_ctx, mod_ref[2:3, :], mod_ref[5:6, :])
            x_new = x_ref[rows, :] + gt * (y * ry * gp_ref[...])
            if target is None:
                xo_ref[rows, :] = x_new
            else:
                err = (x_new - rest[jj][...]) * jnp.where(is_ctx, 0.0, 1.0)
                xo_ref[rows, :] = err * (1.0 / D_MODEL)
                loss_ref[...] += jnp.sum(err * err) * (0.5 / D_MODEL)

    rows_f32 = jax.ShapeDtypeStruct((t, D_MODEL), F32)
    group_f32 = jax.ShapeDtypeStruct((t, GROUP_W), F32)
    loss_shape, loss_spec, t_spec, t_arg = (), (), [], ()
    if target is not None:
        loss_shape, loss_spec = (jax.ShapeDtypeStruct((8, LANES), F32),), (_full((8, LANES)),)
        t_spec = [pl.BlockSpec((TM, D_MODEL), lambda i, jj=jj: (jnp.maximum(i * SUB + jj - N_CTX_TILES, 0), 0))
                  for jj in range(SUB)]
        t_arg = (target,) * SUB
    return _pc(
        body, name="mix_out" if target is None else "mix_out_loss", grid=(t // TB,),
        out_shape=(rows_f32, rows_f32, group_f32, group_f32) + loss_shape,
        in_specs=[_rows(W_C, TB), prev_spec, next_spec, _rows(W_R, TB), _heads(N_Q_HEADS, LANES, TB), _rows(D_MODEL, TB),
                  _const((8, D_MODEL)), _const((1, D_MODEL)), _const((D_MODEL, D_MODEL)),
                  _const((8, GROUP_W)), _const((32, GROUP_W)), _const((8, GROUP_W)),
                  _const((N_SPATIAL_GROUPS * CHUNK, CHUNK)), _const((CHUNK, GROUP_W))] + t_spec,
        out_specs=(_rows(D_MODEL, TB), _rows(D_MODEL, TB), _rows(GROUP_W, TB), _rows(GROUP_W, TB)) + loss_spec,
        scratch_shapes=[pltpu.VMEM((SUB, TM + 2 * HALO, GROUP_W), F32), pltpu.VMEM((SUB, TM + 2 * HALO, GROUP_W), F32)],
        compiler_params=_cparams(),
    )(pc, pc, pc, pr, oe, xt, modv, g_post, w_out, conv_a, conv_b, vecs, wss, bsm, *t_arg)


def _mix_out_bwd(dxo, y, pr, ca, z2, oe, modv, g_post, w_out, vecs, wss, wsts, bsm):
    t = y.shape[0]
    n_tiles = t // TM

    def body(dxo_ref, y_ref, pr_ref, ca_ref, z2_ref, oe_ref, mod_ref, gp_ref, wo_ref, vec_ref, wss_ref, wsts_ref, bsm_ref,
             dpr_ref, ga_ref, gb_ref, doe_ref, dwo_ref, pvec_ref, s256_ref, dws_ref, dbs_ref, dbsm):
        i = pl.program_id(0)
        is_ctx = i < N_CTX_TILES

        @pl.when(i == 0)
        def _():
            dwo_ref[...] = jnp.zeros_like(dwo_ref)
            pvec_ref[...] = jnp.zeros_like(pvec_ref)
            s256_ref[...] = jnp.zeros_like(s256_ref)
            dws_ref[...] = jnp.zeros_like(dws_ref)
            dbsm[...] = jnp.zeros_like(dbsm)

        dxo_ = dxo_ref[...]
        y_ = y_ref[...]
        ry = lax.rsqrt(jnp.mean(y_ * y_, axis=1, keepdims=True) + RMS_EPS)
        nh = y_ * ry
        gp = gp_ref[...]
        gt = jnp.where(is_ctx, mod_ref[2:3, :], mod_ref[5:6, :])
        dgt = _colsum(dxo_ * (nh * gp))
        pvec_ref[0:1, :] += jnp.where(is_ctx, dgt, 0.0)
        pvec_ref[1:2, :] += jnp.where(is_ctx, 0.0, dgt)
        dn = dxo_ * gt
        pvec_ref[2:3, :] += _colsum(dn * nh)
        dnh = dn * gp
        dy = ry * (dnh - nh * jnp.mean(dnh * nh, axis=1, keepdims=True))

        vecs = vec_ref[...]
        bsm_ = bsm_ref[...]
        ca_ = ca_ref[...]
        lane256 = lax.broadcasted_iota(jnp.int32, (TM, GROUP_W), 1)
        f = _row_local_mixers(pr_ref[...], ca_, z2_ref[...], oe_ref, vecs, wss_ref, bsm_, lane256)
        big, ys, gates = _mixer_concat(f, ca_)
        dyb = dy.astype(BF16)
        dwo_ref[...] += _dot_tn(big, dyb)
        dbig = _dot_nt(dyb, wo_ref[...])

        d_y, d_gate = [], []
        for n, (name, yy, gg) in enumerate(zip(("a_g", "b_g", "c_g", "d_g"), ys, gates)):
            dpart = dbig[:, n * GROUP_W:(n + 1) * GROUP_W]
            gx = f[name]
            sg_ = _sigmoid(gx)
            d_y.append(dpart * gg)
            d_gate.append(dpart * yy * (sg_ * (1.0 + gx * (1.0 - sg_))))
        dya, dyb_, dyc, datt = d_y

        d_ab = dya * ca_
        ga_ref[...] = dya * f["a_b"]
        tb = f["tb"]
        sb = _sigmoid(tb)
        dtb = dyb_ * (sb * (1.0 + tb * (1.0 - sb)))
        s256_ref[1:2, :] += _colsum(dtb * f["zn"])
        s256_ref[2:3, :] += _colsum(dtb)
        dz2 = _layer_norm_bwd(dtb * vecs[1:2, :], f["zn"], f["rs_b"])
        gb_ref[...] = dz2
        s256_ref[0:1, :] += _colsum(dz2)
        d_cu = dyc * f["sg"]
        dsg = dyc * f["c_u"]
        grp = f["grp"]
        dvn_parts = []
        for ch in range(TM // CHUNK):
            rows = slice(ch * CHUNK, (ch + 1) * CHUNK)
            dsg_c = dsg[rows, :]
            dbsm[...] += dsg_c
            vn_c = f["vn"][rows, :].astype(BF16)
            for g in range(N_SPATIAL_GROUPS):
                masked = jnp.where(grp[0:CHUNK] == g, dsg_c, 0.0).astype(BF16)
                dws_ref[g * CHUNK:(g + 1) * CHUNK, :] += _dot_nt(masked, vn_c)
            dvn_parts.append(_group_select(_dot(wsts_ref[...], dsg_c.astype(BF16)), grp[0:CHUNK]))
        dvn = jnp.concatenate(dvn_parts, axis=0)
        s256_ref[3:4, :] += _colsum(dvn * f["vn_hat"])
        s256_ref[4:5, :] += _colsum(dvn)
        d_cv = _layer_norm_bwd(dvn * vecs[3:4, :], f["vn_hat"], f["rs_c"])
        lane, lo = f["lane"], f["lo"]
        att = f["att"]
        for b in range(2):
            da = datt[:, b * LANES:(b + 1) * LANES]
            prod = da * att[:, b * LANES:(b + 1) * LANES]
            for hh in range(2):
                h = 2 * b + hh
                lse = _rowsum(jnp.where(lane == HEAD_DIM, oe_ref[h], 0.0))
                delta = _rowsum(jnp.where(lo, prod, 0.0) if hh == 0 else jnp.where(lo, 0.0, prod))
                dah = da if hh == 0 else pltpu.roll(da, HEAD_DIM, 1)
                doe_ref[h] = jnp.where(lo, dah, jnp.where(lane == HEAD_DIM, delta, jnp.where(lane == HEAD_DIM + 1, lse, 0.0)))

        dpr_ref[...] = jnp.concatenate([d_ab, d_gate[0], d_gate[1], d_cu, d_cv, d_gate[2], d_gate[3]], axis=1).astype(BF16)

        @pl.when(i == n_tiles - 1)
        def _():
            acc = dbsm[...]
            lane128 = _lane(CHUNK)
            out = jnp.zeros((CHUNK, LANES), F32)
            for g in range(N_SPATIAL_GROUPS):
                col = _rowsum(jnp.where(grp[0:CHUNK] == g, acc, 0.0))
                out = out + jnp.where(lane128 == g, col, 0.0)
            dbs_ref[...] = out

    return _pc(
        body, name="mix_out_bwd", grid=(n_tiles,),
        out_shape=(jax.ShapeDtypeStruct((t, W_R), BF16),
                   jax.ShapeDtypeStruct((t, GROUP_W), F32), jax.ShapeDtypeStruct((t, GROUP_W), F32),
                   jax.ShapeDtypeStruct((N_Q_HEADS, t, LANES), F32),
                   jax.ShapeDtypeStruct((D_MODEL, D_MODEL), F32),
                   jax.ShapeDtypeStruct((8, D_MODEL), F32),
                   jax.ShapeDtypeStruct((8, GROUP_W), F32),
                   jax.ShapeDtypeStruct((N_SPATIAL_GROUPS * CHUNK, CHUNK), F32),
                   jax.ShapeDtypeStruct((CHUNK, LANES), F32)),
        in_specs=[_rows(D_MODEL), _rows(D_MODEL), _rows(W_R), _rows(GROUP_W), _rows(GROUP_W), _heads(N_Q_HEADS, LANES),
                  _full((8, D_MODEL)), _full((1, D_MODEL)), _full((D_MODEL, D_MODEL)), _full((8, GROUP_W)),
                  _full((N_SPATIAL_GROUPS * CHUNK, CHUNK)), _full((N_SPATIAL_GROUPS * CHUNK, CHUNK)), _full((CHUNK, GROUP_W))],
        out_specs=(_rows(W_R), _rows(GROUP_W), _rows(GROUP_W), _heads(N_Q_HEADS, LANES),
                   _full((D_MODEL, D_MODEL)), _full((8, D_MODEL)), _full((8, GROUP_W)),
                   _full((N_SPATIAL_GROUPS * CHUNK, CHUNK)), _full((CHUNK, LANES))),
        scratch_shapes=[pltpu.VMEM((CHUNK, GROUP_W), F32)],
        compiler_params=_cparams(),
    )(dxo, y, pr, ca, z2, oe, modv, g_post, w_out, vecs, wss, wsts, bsm)


def _conv_bwd(pc, g_a, g_b, conv_a, conv_b):
    t = pc.shape[0]
    n_tiles = t // TM
    pc_prev, pc_next = _halo_specs(W_C, t)
    g_prev, g_next = _halo_specs(GROUP_W, t)

    def body(pc_ref, pp_ref, pn_ref, ga_ref, gap_ref, gan_ref, gb_ref, gbp_ref, gbn_ref, cva_ref, cvb_ref,
             dpc_ref, dca_ref, dcb_ref, uext, zext, gaext, gbext):
        i = pl.program_id(0)

        @pl.when(i == 0)
        def _():
            dca_ref[...] = jnp.zeros_like(dca_ref)
            dcb_ref[...] = jnp.zeros_like(dcb_ref)

        pv, nv = _halo_valid(i, n_tiles)
        pc_ = pc_ref[...]
        u, z = _conv_inputs(pc_)
        up, zp = _conv_inputs(pp_ref[...])
        un, zn_ = _conv_inputs(pn_ref[...])
        _fill_ext(uext, up * pv, u, un * nv)
        _fill_ext(zext, zp * pv, z, zn_ * nv)
        ga = ga_ref[...]
        gb = gb_ref[...]
        _fill_ext(gaext, gap_ref[...] * pv, ga, gan_ref[...] * nv)
        _fill_ext(gbext, gbp_ref[...] * pv, gb, gbn_ref[...] * nv)

        du = cva_ref[0:1, :] * gaext[pl.ds(HALO + 1, TM), :]
        dca_ref[0:1, :] += _colsum(ga * uext[pl.ds(HALO - 1, TM), :])
        for kk in range(1, SHORT_CONV_K):
            du = du + cva_ref[kk:kk + 1, :] * gaext[pl.ds(HALO + 1 - kk, TM), :]
            dca_ref[kk:kk + 1, :] += _colsum(ga * uext[pl.ds(HALO - 1 + kk, TM), :])
        dz = _taps31(gbext, cvb_ref, True)
        for r0 in range(0, TM, CONV_ROWS):
            gb_rows = gb_ref[pl.ds(r0, CONV_ROWS), :]
            for b in range(8):
                zb = zext[pl.ds(r0 + b, CONV_ROWS + 24), :]
                for a in range(4):
                    kk = 8 * a + b - 1
                    if 0 <= kk < CONFORMER_K:
                        dcb_ref[kk:kk + 1, :] += _colsum(gb_rows * zb[8 * a:8 * a + CONV_ROWS])

        a_c, a_h = pc_[:, 0:GROUP_W], pc_[:, GROUP_W:2 * GROUP_W]
        glu_a, glu_g = pc_[:, 2 * GROUP_W:3 * GROUP_W], pc_[:, 3 * GROUP_W:4 * GROUP_W]
        sg = _sigmoid(glu_g)
        dpc_ref[...] = jnp.concatenate([du * a_h, du * a_c, dz * sg, dz * glu_a * sg * (1.0 - sg)], axis=1).astype(BF16)

    ext = pltpu.VMEM((TM + 2 * HALO, GROUP_W), F32)
    return _pc(
        body, name="conv_bwd", grid=(n_tiles,),
        out_shape=(jax.ShapeDtypeStruct((t, W_C), BF16), jax.ShapeDtypeStruct((8, GROUP_W), F32), jax.ShapeDtypeStruct((32, GROUP_W), F32)),
        in_specs=[_rows(W_C), pc_prev, pc_next, _rows(GROUP_W), g_prev, g_next, _rows(GROUP_W), g_prev, g_next,
                  _full((8, GROUP_W)), _full((32, GROUP_W))],
        out_specs=(_rows(W_C), _full((8, GROUP_W)), _full((32, GROUP_W))),
        scratch_shapes=[ext, ext, ext, ext],
        compiler_params=_cparams(),
    )(pc, pc, pc, g_a, g_a, g_a, g_b, g_b, g_b, conv_a, conv_b)


def _attention_bwd(q, k, v, doe, slabs=None):
    t = q.shape[1]
    tk = _kv_chunk(t)
    n_chunks = (t - CTX_LEN) // tk
    n_tiles = t // TM
    n_sl = 0 if slabs is None else len(slabs)

    def body(q_ref, do_ref, k_ref, v_ref, *rest):
        i = pl.program_id(0)
        dq_ref, dk_hbm, dv_hbm = rest[n_sl:n_sl + 3]
        dk_acc, dv_acc = rest[2 * n_sl + 3:2 * n_sl + 5]
        pairs = tuple(zip(rest[:n_sl], rest[n_sl + 3:2 * n_sl + 3]))
        sems = rest[2 * n_sl + 5:]

        @pl.when(i == 0)
        def _():
            dk_acc[...] = jnp.zeros_like(dk_acc)
            dv_acc[...] = jnp.zeros_like(dv_acc)
            if slabs is not None:
                _scatter_phase(0, pairs, *sems)

        lane = _lane(GQA * TM)
        lo = lane < HEAD_DIM
        qs, dos, deltas, lses = [], [], [], []
        for g in range(N_KV_HEADS):
            qs.append(jnp.concatenate([q_ref[GQA * g + hh] for hh in range(GQA)], axis=0))
            dog = jnp.concatenate([do_ref[GQA * g + hh] for hh in range(GQA)], axis=0)
            deltas.append(_rowsum(jnp.where(lane == HEAD_DIM, dog, 0.0)))
            lses.append(_rowsum(jnp.where(lane == HEAD_DIM + 1, dog, 0.0)))
            dos.append(jnp.where(lo, dog, 0.0).astype(BF16))

        def step(st, size, dqs):
            out = []
            for g in range(N_KV_HEADS):
                kc = k_ref[g, pl.ds(st, size), :]
                vc = v_ref[g, pl.ds(st, size), :]
                p = jnp.exp(_dot_nt(qs[g], kc) - lses[g])
                ds_ = (p * (_dot_nt(dos[g], vc) - deltas[g])).astype(BF16)
                dk_acc[g, pl.ds(st, size), :] += _dot_tn(ds_, qs[g])
                dv_acc[g, pl.ds(st, size), :] += _dot_tn(p.astype(BF16), dos[g])
                out.append(dqs[g] + _dot(ds_, kc))
            return tuple(out)

        zero = tuple(jnp.zeros((GQA * TM, LANES), F32) for _ in range(N_KV_HEADS))

        def finish(dqs):
            for g in range(N_KV_HEADS):
                for hh in range(GQA):
                    dq_ref[GQA * g + hh] = dqs[g][hh * TM:(hh + 1) * TM]

        @pl.when(i < N_CTX_TILES)
        def _():
            finish(step(0, CTX_LEN, zero))

        @pl.when(i >= N_CTX_TILES)
        def _():
            finish(lax.fori_loop(0, n_chunks, lambda j, acc: step(pl.multiple_of(CTX_LEN + j * tk, 256), tk, acc),
                                 step(0, CTX_LEN, zero)))

        @pl.when(i == n_tiles - 1)
        def _():
            pltpu.sync_copy(dk_acc, dk_hbm)
            pltpu.sync_copy(dv_acc, dv_hbm)
            if slabs is not None:
                _scatter_phase(1, pairs, *sems)

    kv_shape = jax.ShapeDtypeStruct((N_KV_HEADS, t, LANES), F32)
    hbm = pl.BlockSpec(memory_space=pl.ANY)
    extra = () if slabs is None else tuple(slabs)
    outs = _pc(
        body, name="attention_bwd" if slabs is None else "attention_bwd_scatter", grid=(n_tiles,),
        out_shape=(jax.ShapeDtypeStruct((N_Q_HEADS, t, LANES), F32), kv_shape, kv_shape)
        + tuple(jax.ShapeDtypeStruct(a.shape, a.dtype) for a in extra),
        in_specs=[_heads(N_Q_HEADS, LANES), _heads(N_Q_HEADS, LANES),
                  _full((N_KV_HEADS, t, LANES)), _full((N_KV_HEADS, t, LANES))] + [hbm] * n_sl,
        out_specs=(_heads(N_Q_HEADS, LANES), hbm, hbm) + (hbm,) * n_sl,
        scratch_shapes=[pltpu.VMEM((N_KV_HEADS, t, LANES), F32), pltpu.VMEM((N_KV_HEADS, t, LANES), F32)]
        + (_comm_scratch(SCATTER_SEMS, n_sl) if slabs is not None else []),
        compiler_params=_cparams(),
    )(q, doe, k, v, *extra)
    return outs[0], outs[1], outs[2], tuple(outs[3:])


def _in_proj_bwd(dpc, dpr, dq, dk, dv, pq, qk_gain, cos_t, sin_t, w_c, w_r, w_q, xt, dxo, modv, g_pre):
    t = xt.shape[0]

    def body(dpc_ref, dpr_ref, dq_ref, dk_ref, dv_ref, pq_ref, gain_ref, cos_ref, sin_ref, wc_ref, wr_ref, wq_ref,
             x_ref, dxo_ref, mod_ref, g_ref, dx_ref, dpq_ref, acc_ref, dgain_ref):
        i = pl.program_id(0)

        @pl.when(i == 0)
        def _():
            acc_ref[...] = jnp.zeros_like(acc_ref)
            dgain_ref[...] = jnp.zeros_like(dgain_ref)

        lane = _lane(TM)
        lo = lane < HEAD_DIM
        lo16 = (lane & 31) < 16
        g = g_ref[...]
        for jj in range(SUB):
            rows = pl.ds(jj * TM, TM)
            is_ctx = i * SUB + jj < N_CTX_TILES
            cos = cos_ref[rows, :]
            sin = sin_ref[rows, :]
            outs = []
            for b in range(3):
                src = dq_ref if b < 2 else dk_ref
                base = 2 * b if b < 2 else 0
                drot = src[base, rows, :] + pltpu.roll(src[base + 1, rows, :], HEAD_DIM, 1)
                if b < 2:
                    drot = drot * ATTN_SCALE
                dxg = drot * cos + _swap16(drot * sin, lo16)
                xh, r = _head_norm(pq_ref[rows, b * LANES:(b + 1) * LANES], lo)
                row = 0 if b < 2 else 1
                dgain_ref[row:row + 1, :] += _colsum(dxg * xh)
                dxh = dxg * gain_ref[row:row + 1, :]
                outs.append(r * (dxh - xh * (_pair_sums(dxh * xh, lo) * (1.0 / HEAD_DIM))))
            outs.append(dv_ref[0, rows, :] + pltpu.roll(dv_ref[1, rows, :], HEAD_DIM, 1))
            dpq = jnp.concatenate(outs, axis=1).astype(BF16)
            dpq_ref[rows, :] = dpq

            dh = _dot_nt(dpc_ref[rows, :], wc_ref[...]) + _dot_nt(dpr_ref[rows, :], wr_ref[...]) + _dot_nt(dpq, wq_ref[...])
            x = x_ref[rows, :]
            r = lax.rsqrt(jnp.mean(x * x, axis=1, keepdims=True) + RMS_EPS)
            xn = x * r
            sc = jnp.where(is_ctx, mod_ref[1:2, :], mod_ref[4:5, :])
            dsh = _colsum(dh)
            dsc = _colsum(dh * (xn * g))
            acc_ref[0:1, :] += jnp.where(is_ctx, dsh, 0.0)
            acc_ref[1:2, :] += jnp.where(is_ctx, dsc, 0.0)
            acc_ref[2:3, :] += jnp.where(is_ctx, 0.0, dsh)
            acc_ref[3:4, :] += jnp.where(is_ctx, 0.0, dsc)
            dxg = dh * (1.0 + sc)
            acc_ref[4:5, :] += _colsum(dxg * xn)
            dxn = dxg * g
            dx_ref[rows, :] = r * (dxn - xn * jnp.mean(dxn * xn, axis=1, keepdims=True)) + dxo_ref[rows, :]

    return _pc(
        body, name="in_proj_bwd", grid=(t // TB,),
        out_shape=(jax.ShapeDtypeStruct((t, D_MODEL), F32), jax.ShapeDtypeStruct((t, W_Q), BF16),
                   jax.ShapeDtypeStruct((8, D_MODEL), F32), jax.ShapeDtypeStruct((8, LANES), F32)),
        in_specs=[_rows(W_C, TB), _rows(W_R, TB),
                  _heads(N_Q_HEADS, LANES, TB), _heads(N_KV_HEADS, LANES, TB), _heads(N_KV_HEADS, LANES, TB), _rows(W_Q, TB),
                  _const((8, LANES)), _rows(LANES, TB), _rows(LANES, TB),
                  _const((D_MODEL, W_C)), _const((D_MODEL, W_R)), _const((D_MODEL, W_Q)),
                  _rows(D_MODEL, TB), _rows(D_MODEL, TB), _const((8, D_MODEL)), _const((1, D_MODEL))],
        out_specs=(_rows(D_MODEL, TB), _rows(W_Q, TB), _full((8, D_MODEL)), _full((8, LANES))),
        compiler_params=_cparams(),
    )(dpc, dpr, dq, dk, dv, pq, qk_gain, cos_t, sin_t, w_c, w_r, w_q, xt, dxo, modv, g_pre)


def _in_proj_wgrad(h, dpc, dpr, dpq):
    t = h.shape[0]

    def body(h_ref, dpc_ref, dpr_ref, dpq_ref, gc_ref, gr_ref, gq_ref):
        @pl.when(pl.program_id(0) == 0)
        def _():
            gc_ref[...] = jnp.zeros_like(gc_ref)
            gr_ref[...] = jnp.zeros_like(gr_ref)
            gq_ref[...] = jnp.zeros_like(gq_ref)

        hb = h_ref[...]
        gc_ref[...] += _dot_tn(hb, dpc_ref[...])
        gr_ref[...] += _dot_tn(hb, dpr_ref[...])
        gq_ref[...] += _dot_tn(hb, dpq_ref[...])

    return _pc(
        body, name="in_proj_wgrad", grid=(t // TB,),
        out_shape=(jax.ShapeDtypeStruct((D_MODEL, W_C), F32), jax.ShapeDtypeStruct((D_MODEL, W_R), F32),
                   jax.ShapeDtypeStruct((D_MODEL, W_Q), F32)),
        in_specs=[_rows(D_MODEL, TB), _rows(W_C, TB), _rows(W_R, TB), _rows(W_Q, TB)],
        out_specs=(_full((D_MODEL, W_C)), _full((D_MODEL, W_R)), _full((D_MODEL, W_Q))),
        compiler_params=_cparams(),
    )(h, dpc, dpr, dpq)


def _sum_slabs(slabs, tile_rows):
    n, r, c = slabs.shape

    def body(s_ref, o_ref):
        acc = s_ref[0].astype(F32)
        for k in range(1, n):
            acc = acc + s_ref[k].astype(F32)
        o_ref[...] = acc

    return _pc(
        body, name="sum_slabs", grid=(r // tile_rows,),
        out_shape=jax.ShapeDtypeStruct((r, c), F32),
        in_specs=[pl.BlockSpec((n, tile_rows, c), lambda i: (0, i, 0))],
        out_specs=pl.BlockSpec((tile_rows, c), lambda i: (i, 0)),
        compiler_params=_cparams(),
    )(slabs)


def _sum_layer_slabs(layers, tile_rows):
    nl = len(layers)
    n, r, c = layers[0].shape
    per = r // tile_rows

    def body(*refs):
        o_ref = refs[nl]
        for l in range(nl):
            @pl.when(pl.program_id(0) // per == l)
            def _(l=l):
                acc = refs[l][0].astype(F32)
                for k in range(1, n):
                    acc = acc + refs[l][k].astype(F32)
                o_ref[...] = acc

    def spec(l):
        return pl.BlockSpec((n, tile_rows, c), lambda i: (0, jnp.clip(i - l * per, 0, per - 1), 0))

    return _pc(
        body, name="sum_layer_slabs", grid=(nl * per,),
        out_shape=jax.ShapeDtypeStruct((nl * r, c), F32),
        in_specs=[spec(l) for l in range(nl)],
        out_specs=pl.BlockSpec((tile_rows, c), lambda i: (i, 0)),
        compiler_params=_cparams(),
    )(*layers)


def _adamw(grads, w, m, v, tile_rows):
    r, c = w.shape
    n_g = len(grads)

    def body(*refs):
        g = refs[0][...]
        for k in range(1, n_g):
            g = g + refs[k][...]
        w_ref, m_ref, v_ref, g_out, d_out, m_out, v_out = refs[n_g:]
        m_new = ADAM_B1 * m_ref[...] + (1.0 - ADAM_B1) * g
        v_new = ADAM_B2 * v_ref[...] + (1.0 - ADAM_B2) * (g * g)
        m_hat = m_new / (1.0 - ADAM_B1 ** ADAM_STEP)
        v_hat = v_new / (1.0 - ADAM_B2 ** ADAM_STEP)
        g_out[...] = g
        d_out[...] = -ADAM_LR * (m_hat / (jnp.sqrt(v_hat) + ADAM_EPS) + ADAM_WD * w_ref[...])
        m_out[...] = m_new
        v_out[...] = v_new

    spec = pl.BlockSpec((tile_rows, c), lambda i: (i, 0))
    shape = jax.ShapeDtypeStruct((r, c), F32)
    return _pc(
        body, name="adamw", grid=(r // tile_rows,),
        out_shape=(shape,) * 4, in_specs=[spec] * (n_g + 3), out_specs=(spec,) * 4,
        compiler_params=_cparams(),
    )(*grads, w, m, v)


def _rope_tables(s_lat):
    n_rows = s_lat // GRID_W
    axis_dim = HEAD_DIM // 2
    inv_freq = 1.0 / (ROPE_THETA ** (jnp.arange(0, axis_dim, 2, dtype=F32) / axis_dim))
    d = np.arange(LANES) % HEAD_DIM
    on_rows = (d // axis_dim) == 0
    freq = d % (axis_dim // 2)
    sign = np.where((d % axis_dim) < axis_dim // 2, -1.0, 1.0).astype(np.float32)
    ang_r = jnp.arange(n_rows, dtype=F32)[:, None] * inv_freq[freq][None, :]
    ang_c = jnp.arange(GRID_W, dtype=F32)[:, None] * inv_freq[freq][None, :]

    def spread(fn):
        full = jnp.where(on_rows[None, None, :], fn(ang_r)[:, None, :], fn(ang_c)[None, :, :])
        return full.reshape(s_lat, LANES)

    cos = jnp.concatenate([jnp.ones((CTX_LEN, LANES), F32), spread(jnp.cos)], axis=0)
    sin = jnp.concatenate([jnp.zeros((CTX_LEN, LANES), F32), spread(jnp.sin) * sign[None, :]], axis=0)
    return cos, sin


def _pad_rows(a, rows):
    return jnp.concatenate([a, jnp.zeros((rows - a.shape[0],) + a.shape[1:], a.dtype)], axis=0)


_SMALL = ("c_ctx", "b_mod", "g_pre", "g_post", "conv_a", "conv_b", "conv_b_bias", "conf_ln_g", "conf_ln_b",
          "sgu_ln_g", "sgu_ln_b", "w_s", "b_s", "q_gain", "k_gain")


def _pack(arrays):
    flat = jnp.concatenate([a.reshape(-1) for a in arrays])
    rows = -(-flat.shape[0] // (16 * LANES)) * 16
    return _pad_rows(flat.reshape(-1, 1), rows * LANES).reshape(rows, LANES)


def _unpack(packed, shapes):
    flat = packed.reshape(-1)
    out, off = [], 0
    for s in shapes:
        n = int(np.prod(s))
        out.append(flat[off:off + n].reshape(s))
        off += n
    return out


def kernel(x, c, ctx, c_ctx, w_mod, b_mod, g_pre, g_post, w_in, w_out, conv_a, conv_b, conv_b_bias, conf_ln_g, conf_ln_b, sgu_ln_g, sgu_ln_b, w_s, b_s, q_gain, k_gain, loss_target, m_c_ctx, m_w_mod, m_b_mod, m_g_pre, m_g_post, m_w_in, m_w_out, m_conv_a, m_conv_b, m_conv_b_bias, m_conf_ln_g, m_conf_ln_b, m_sgu_ln_g, m_sgu_ln_b, m_w_s, m_b_s, m_q_gain, m_k_gain, v_c_ctx, v_w_mod, v_b_mod, v_g_pre, v_g_post, v_w_in, v_w_out, v_conv_a, v_conv_b, v_conv_b_bias, v_conf_ln_g, v_conf_ln_b, v_sgu_ln_g, v_sgu_ln_b, v_w_s, v_b_s, v_q_gain, v_k_gain):
    weights = dict(c_ctx=c_ctx, w_mod=w_mod, b_mod=b_mod, g_pre=g_pre, g_post=g_post, w_in=w_in, w_out=w_out, conv_a=conv_a,
                   conv_b=conv_b, conv_b_bias=conv_b_bias, conf_ln_g=conf_ln_g, conf_ln_b=conf_ln_b, sgu_ln_g=sgu_ln_g,
                   sgu_ln_b=sgu_ln_b, w_s=w_s, b_s=b_s, q_gain=q_gain, k_gain=k_gain)
    m_in = dict(c_ctx=m_c_ctx, w_mod=m_w_mod, b_mod=m_b_mod, g_pre=m_g_pre, g_post=m_g_post, w_in=m_w_in, w_out=m_w_out,
                conv_a=m_conv_a, conv_b=m_conv_b, conv_b_bias=m_conv_b_bias, conf_ln_g=m_conf_ln_g, conf_ln_b=m_conf_ln_b,
                sgu_ln_g=m_sgu_ln_g, sgu_ln_b=m_sgu_ln_b, w_s=m_w_s, b_s=m_b_s, q_gain=m_q_gain, k_gain=m_k_gain)
    v_in = dict(c_ctx=v_c_ctx, w_mod=v_w_mod, b_mod=v_b_mod, g_pre=v_g_pre, g_post=v_g_post, w_in=v_w_in, w_out=v_w_out,
                conv_a=v_conv_a, conv_b=v_conv_b, conv_b_bias=v_conv_b_bias, conf_ln_g=v_conf_ln_g, conf_ln_b=v_conf_ln_b,
                sgu_ln_g=v_sgu_ln_g, sgu_ln_b=v_sgu_ln_b, w_s=v_w_s, b_s=v_b_s, q_gain=v_q_gain, k_gain=v_k_gain)
    order = ("c_ctx", "w_mod", "b_mod", "g_pre", "g_post", "w_in", "w_out", "conv_a", "conv_b", "conv_b_bias", "conf_ln_g",
             "conf_ln_b", "sgu_ln_g", "sgu_ln_b", "w_s", "b_s", "q_gain", "k_gain")

    s_lat = x.shape[1]
    ax, ay, ac = lax.axis_index("x"), lax.axis_index("y"), lax.axis_index("c")
    chip = 2 * ax + ay
    example = 4 * ax + 2 * ay + ac

    c_rows = _all_gather_rows(_pad_rows(c, 8))[::8]
    c16 = _pad_rows(jnp.concatenate([c_rows, c_ctx[None, :]], axis=0), 16)
    b_mod_shard = lax.dynamic_slice_in_dim(b_mod, chip * SHARD_MOD, SHARD_MOD, axis=1)[:, None, :]
    silu_c, mod_shard = _mod_forward(c16, w_mod, b_mod_shard)
    mod_all = _all_gather_rows(mod_shard.reshape(DEPTH * 16, SHARD_MOD)).reshape(8, DEPTH, 16, SHARD_MOD)
    mod_full = jnp.transpose(mod_all[::2], (1, 2, 0, 3)).reshape(DEPTH, 16, 3 * D_MODEL)
    mod_lat = lax.dynamic_index_in_dim(mod_full, example, axis=1, keepdims=False).reshape(DEPTH, 3, D_MODEL)
    mod_ctx = mod_full[:, 8].reshape(DEPTH, 3, D_MODEL)
    modv = jnp.concatenate([mod_ctx, mod_lat, jnp.zeros((DEPTH, 2, D_MODEL), F32)], axis=1)

    wi_b, wo_b = w_in.astype(BF16), w_out.astype(BF16)

    def regroup(wi_all):
        wi_full = jnp.concatenate([wi_all[k] for k in range(N_CHIPS)], axis=-1)
        wc_l = jnp.concatenate([wi_full[:, 256:768], wi_full[:, 1024:1536]], axis=-1)
        wr_l = jnp.concatenate([wi_full[:, 0:256], wi_full[:, 768:1024], wi_full[:, 1536:2560], wi_full[:, 3072:3328]], axis=-1)
        return wc_l, wr_l, wi_full[:, 2560:3072]

    w_c, w_r, w_q, wo_full = [None] * DEPTH, [None] * DEPTH, [None] * DEPTH, [None] * DEPTH
    w_c[0], w_r[0], w_q[0] = regroup(_gather_weights((wi_b[0],))[0])

    cos_t, sin_t = _rope_tables(s_lat)
    conv_a_full = jnp.zeros((DEPTH, 8, GROUP_W), F32)
    conv_b_full = jnp.zeros((DEPTH, 32, GROUP_W), F32)
    conv_small = jnp.concatenate([conv_a.reshape(DEPTH * SHORT_CONV_K, -1), conv_b.reshape(DEPTH * CONFORMER_K, -1)], axis=0)
    n_cs = conv_small.shape[0]
    conv_rows = -(-n_cs // 8) * 8
    conv_all = _all_gather_rows(_pad_rows(conv_small, conv_rows)).reshape(8, conv_rows, -1)[::2]
    conv_all = jnp.transpose(conv_all, (1, 0, 2)).reshape(conv_rows, GROUP_W)
    conv_a_full = conv_a_full.at[:, :SHORT_CONV_K].set(conv_all[:DEPTH * SHORT_CONV_K].reshape(DEPTH, SHORT_CONV_K, GROUP_W))
    conv_b_full = conv_b_full.at[:, :CONFORMER_K].set(
        conv_all[DEPTH * SHORT_CONV_K:n_cs].reshape(DEPTH, CONFORMER_K, GROUP_W))

    vecs = jnp.stack([conv_b_bias, conf_ln_g, conf_ln_b, sgu_ln_g, sgu_ln_b] + [jnp.zeros_like(conv_b_bias)] * 3, axis=1)
    wss = w_s.reshape(DEPTH, N_SPATIAL_GROUPS * CHUNK, CHUNK).astype(BF16)
    wsts = jnp.swapaxes(w_s, 2, 3).reshape(DEPTH, N_SPATIAL_GROUPS * CHUNK, CHUNK).astype(BF16)
    bsm = jnp.repeat(jnp.swapaxes(b_s, 1, 2), HEAD_DIM, axis=2)
    qk_gain = jnp.concatenate([jnp.tile(q_gain, (1, 2))[:, None, :], jnp.tile(k_gain, (1, 2))[:, None, :],
                               jnp.zeros((DEPTH, 6, LANES), F32)], axis=1)

    xt = jnp.concatenate([ctx[0], x[0]], axis=0)
    saved = []
    for l in range(DEPTH):
        h, pc, pr, pq, q, k, v = _in_proj(xt, modv[l], g_pre[l][None, :], w_c[l], w_r[l], w_q[l], qk_gain[l], cos_t, sin_t)
        oe, gathered = _attention_fwd(q, k, v, (wo_b[l],) + ((wi_b[l + 1],) if l + 1 < DEPTH else ()))
        wo_full[l] = jnp.concatenate([gathered[0][k] for k in range(N_CHIPS)], axis=0)
        if l + 1 < DEPTH:
            w_c[l + 1], w_r[l + 1], w_q[l + 1] = regroup(gathered[1])
        mixed = _mix_out(pc, pr, oe, xt, modv[l], g_post[l][None, :], wo_full[l], conv_a_full[l], conv_b_full[l],
                         vecs[l], wss[l], bsm[l], loss_target[0] if l + 1 == DEPTH else None)
        x_new, y, ca, z2 = mixed[:4]
        saved.append(dict(x=xt, h=h, pc=pc, pr=pr, pq=pq, q=q, k=k, v=v, oe=oe, y=y, ca=ca, z2=z2))
        xt = x_new
    dxo = xt
    loss = lax.psum(mixed[4][0, 0], ("x", "y", "c"))

    g_small = {n: [None] * DEPTH for n in _SMALL}
    d_mod, landed_in, landed_out = [None] * DEPTH, [None] * DEPTH, [None] * DEPTH
    slab_in = None
    for l in reversed(range(DEPTH)):
        s = saved[l]
        dpr, g_a, g_b, doe, gw_o, pvec, s256, dws, dbs = _mix_out_bwd(
            dxo, s["y"], s["pr"], s["ca"], s["z2"], s["oe"], modv[l], g_post[l][None, :], wo_full[l], vecs[l], wss[l], wsts[l], bsm[l])
        dpc, dca, dcb = _conv_bwd(s["pc"], g_a, g_b, conv_a_full[l], conv_b_full[l])
        slab_out = gw_o.reshape(N_CHIPS, SHARD_OUT, D_MODEL).astype(BF16)
        dq, dk, dv, got = _attention_bwd(s["q"], s["k"], s["v"], doe, (slab_out,) + (() if slab_in is None else (slab_in,)))
        landed_out[l] = got[0]
        if slab_in is not None:
            landed_in[l + 1] = got[1]
        dxo, dpq, acc, dgain = _in_proj_bwd(dpc, dpr, dq, dk, dv, s["pq"], qk_gain[l], cos_t, sin_t, w_c[l], w_r[l], w_q[l],
                                            s["x"], dxo, modv[l], g_pre[l][None, :])
        gw_c, gw_r, gw_q = _in_proj_wgrad(s["h"], dpc, dpr, dpq)
        gw_in = jnp.concatenate([gw_r[:, 0:256], gw_c[:, 0:512], gw_r[:, 256:512], gw_c[:, 512:1024],
                                 gw_r[:, 512:1536], gw_q, gw_r[:, 1536:1792]], axis=-1)
        slab_in = jnp.transpose(gw_in.reshape(D_MODEL, N_CHIPS, SHARD_IN), (1, 0, 2)).astype(BF16)
        d_mod[l] = jnp.stack([jnp.concatenate([acc[2], acc[3], pvec[1]]), jnp.concatenate([acc[0], acc[1], pvec[0]])])
        g_small["g_pre"][l] = acc[4]
        g_small["g_post"][l] = pvec[2]
        g_small["conv_a"][l] = dca[:SHORT_CONV_K]
        g_small["conv_b"][l] = dcb[:CONFORMER_K]
        g_small["conv_b_bias"][l] = s256[0]
        g_small["conf_ln_g"][l] = s256[1]
        g_small["conf_ln_b"][l] = s256[2]
        g_small["sgu_ln_g"][l] = s256[3]
        g_small["sgu_ln_b"][l] = s256[4]
        g_small["w_s"][l] = dws.reshape(N_SPATIAL_GROUPS, CHUNK, CHUNK)
        g_small["b_s"][l] = jnp.transpose(dbs[:, :N_SPATIAL_GROUPS])
        g_small["q_gain"][l] = dgain[0, :HEAD_DIM] + dgain[0, HEAD_DIM:]
        g_small["k_gain"][l] = dgain[1, :HEAD_DIM] + dgain[1, HEAD_DIM:]
    grad_x = dxo[CTX_LEN:][None]

    d_mod_all = _all_gather_rows(jnp.stack(d_mod).reshape(DEPTH * 2, 3 * D_MODEL)).reshape(8, DEPTH, 2, 3 * D_MODEL)
    d_lat = jnp.transpose(d_mod_all[:, :, 0], (1, 0, 2))
    d_ctx = jnp.transpose(d_mod_all[:, :, 1], (1, 0, 2))
    cols = lambda a: lax.dynamic_slice_in_dim(a.reshape(DEPTH, 8, N_CHIPS, SHARD_MOD), chip, 1, axis=2)[:, :, 0]
    silu_t = jnp.transpose(silu_c)
    s_t = jnp.concatenate([silu_t[:, 0:8], jnp.tile(silu_t[:, 8:9], (1, 8)), jnp.zeros((D_MODEL, LANES - 16), F32)], axis=1)
    g_rows = jnp.concatenate([cols(d_lat), cols(d_ctx), jnp.zeros((DEPTH, LANES - 16, SHARD_MOD), F32)], axis=1)
    g_w_mod, g_b_mod, c_ctx_part = _mod_backward(s_t, g_rows, cols(d_ctx), jnp.concatenate([d_lat, d_ctx], axis=1),
                                                 w_mod, c_ctx[:, None])

    for n in _SMALL:
        if n not in ("c_ctx", "b_mod"):
            g_small[n] = jnp.stack(g_small[n])
    small_parts = [0.5 * c_ctx_part[:, 0]] + [g_small[n] for n in _SMALL[2:]]
    packed = _pack(small_parts)
    gathered = _all_gather_rows(packed.astype(BF16)).reshape(8, packed.shape[0], LANES)
    small_sum = _sum_slabs(gathered, packed.shape[0])
    small_g = dict(zip(("c_ctx",) + _SMALL[2:], _unpack(small_sum, [p.shape for p in small_parts])))
    small_g["b_mod"] = g_b_mod[:, 0]
    ch64 = GROUP_W // N_CHIPS
    for n in ("conv_a", "conv_b"):
        small_g[n] = lax.dynamic_slice_in_dim(small_g[n], chip * ch64, ch64, axis=2)
    sw = _pack([weights[n] for n in _SMALL])
    sm = _pack([m_in[n] for n in _SMALL])
    sv = _pack([v_in[n] for n in _SMALL])
    sg = _pack([small_g[n] for n in _SMALL])
    shapes = [weights[n].shape for n in _SMALL]
    small_out = [dict(zip(_SMALL, _unpack(o, shapes))) for o in _adamw([sg], sw, sm, sv, sg.shape[0])]

    landed_in[0] = _scatter_slabs((slab_in,))[0]
    sum_in = _sum_layer_slabs(landed_in, 512)
    sum_out = _sum_layer_slabs(landed_out, 256)
    sib_in, sib_out = _swap_with_sibling(sum_in, sum_out)

    big = {}
    flat = lambda a: a.reshape(-1, a.shape[-1])
    for n, grads, rows in (("w_in", [sum_in, sib_in], 512), ("w_out", [sum_out, sib_out], 256), ("w_mod", [flat(g_w_mod)], 512)):
        outs = _adamw(grads, flat(weights[n]), flat(m_in[n]), flat(v_in[n]), rows)
        big[n] = [o.reshape(weights[n].shape) for o in outs]

    def leaf(n, j):
        return big[n][j] if n in big else small_out[j][n]

    return (loss, grad_x, *[leaf(n, 0) for n in order], *[leaf(n, 1) for n in order],
            *[leaf(n, 2) for n in order], *[leaf(n, 3) for n in order])
```

```python
import functools

import numpy as np
import jax
import jax.numpy as jnp
from jax import lax
from jax.experimental import pallas as pl
from jax.experimental.pallas import tpu as pltpu

F32 = jnp.float32
BF16 = jnp.bfloat16
MESH = pl.DeviceIdType.MESH

D_MODEL = 1024
DEPTH = 4
GRID_W = 64
CTX_LEN = 256
GROUP_W = 256
HEAD_DIM = 64
N_Q_HEADS = 4
N_KV_HEADS = 2
GQA = N_Q_HEADS // N_KV_HEADS
ROPE_THETA = 10000.0
ATTN_SCALE = HEAD_DIM ** -0.5
SHORT_CONV_K = 3
CONFORMER_K = 31
CHUNK = 128
N_SPATIAL_GROUPS = 4
RMS_EPS = 1e-6
LN_EPS = 1e-5
ADAM_LR = 0.001
ADAM_B1 = 0.9
ADAM_B2 = 0.999
ADAM_EPS = 1e-08
ADAM_WD = 0.01
ADAM_STEP = 10

LANES = 128
HALO = 16
CONV_ROWS = 64
TM = 256
N_CTX_TILES = CTX_LEN // TM
SUB = 3
TB = SUB * TM
W_C = 1024
W_R = 1792
W_Q = 512
PROJ_W = W_C + W_R + W_Q
N_CHIPS = 4
SHARD_IN = PROJ_W // N_CHIPS
SHARD_OUT = D_MODEL // N_CHIPS
SHARD_MOD = 3 * D_MODEL // N_CHIPS
VMEM_LIMIT = 56 * 1024 * 1024


def _pc(body, **kw):
    return pl.pallas_call(body, **kw)


def _cparams(**kw):
    return pltpu.CompilerParams(dimension_semantics=("arbitrary",), vmem_limit_bytes=VMEM_LIMIT, **kw)


def _full(shape):
    n = len(shape)
    return pl.BlockSpec(shape, lambda i: (0,) * n)


def _const(shape):
    n = len(shape)
    return pl.BlockSpec(shape, lambda i: (0,) * n, pipeline_mode=pl.Buffered(1))


def _rows(width, tm=TM):
    return pl.BlockSpec((tm, width), lambda i: (i, 0))


def _heads(nh, width, tm=TM):
    return pl.BlockSpec((nh, tm, width), lambda i: (0, i, 0))


def _sigmoid(x):
    return jax.nn.sigmoid(x)


def _dot(a, b):
    return jnp.dot(a, b, preferred_element_type=F32)


def _dot_nt(a, b):
    return lax.dot_general(a, b, (((1,), (1,)), ((), ())), preferred_element_type=F32)


def _dot_tn(a, b):
    return lax.dot_general(a, b, (((0,), (0,)), ((), ())), preferred_element_type=F32)


def _lane(rows):
    return lax.broadcasted_iota(jnp.int32, (rows, LANES), 1)


def _rowsum(x):
    return jnp.sum(x, axis=1, keepdims=True)


def _colsum(x):
    return jnp.sum(x, axis=0, keepdims=True)


def _pair_sums(x, lo):
    s0 = _rowsum(jnp.where(lo, x, 0.0))
    s1 = _rowsum(jnp.where(lo, 0.0, x))
    return jnp.where(lo, s0, s1)


def _swap16(x, lo16):
    return jnp.where(lo16, pltpu.roll(x, LANES - 16, 1), pltpu.roll(x, 16, 1))


def _layer_norm_stats(x):
    mu = jnp.mean(x, axis=1, keepdims=True)
    xc = x - mu
    rs = lax.rsqrt(jnp.mean(xc * xc, axis=1, keepdims=True) + LN_EPS)
    return xc * rs, rs


def _layer_norm_bwd(dxn, xn, rs):
    return rs * (dxn - jnp.mean(dxn, axis=1, keepdims=True) - xn * jnp.mean(dxn * xn, axis=1, keepdims=True))


def _group_select(r, grp):
    out = jnp.where(grp == 0, r[0:CHUNK], 0.0)
    for g in range(1, N_SPATIAL_GROUPS):
        out = out + jnp.where(grp == g, r[g * CHUNK:(g + 1) * CHUNK], 0.0)
    return out


def _kv_chunk(t):
    return 1024 if (t - CTX_LEN) % 1024 == 0 else 256


def _all_gather_rows(x_shard):
    m_per, n = x_shard.shape

    def body(x_ref, out_ref, send_sems, recv_sems, local_sem):
        x, y, c = lax.axis_index("x"), lax.axis_index("y"), lax.axis_index("c")
        me, sibling = (x, y, c), (x, y, 1 - c)
        chips = [(1 - x, y), (x, 1 - y), (1 - x, 1 - y)]

        def rows(px, py, pc):
            return out_ref.at[pl.ds((4 * px + 2 * py + pc) * m_per, m_per), :]

        def copy(k, block, to, src=None):
            return pltpu.make_async_remote_copy(
                src_ref=rows(*block) if src is None else src, dst_ref=rows(*block),
                send_sem=send_sems.at[k], recv_sem=recv_sems.at[k], device_id=to, device_id_type=MESH)

        mine = pltpu.make_async_copy(x_ref, rows(*me), local_sem)
        mine.start()
        first = [copy(0, me, sibling, src=x_ref)]
        first += [copy(1 + j, me, (*chip, c), src=x_ref) for j, chip in enumerate(chips)]
        for cp in first:
            cp.start()
        passed = [copy(4 + j, (*chip, c), sibling) for j, chip in enumerate(chips)]
        for j, chip in enumerate(chips):
            copy(1 + j, (*chip, c), me).wait_recv()
            passed[j].start()
        copy(0, sibling, me).wait_recv()
        for j, chip in enumerate(chips):
            copy(4 + j, (*chip, 1 - c), me).wait_recv()
        for cp in first + passed:
            cp.wait_send()
        mine.wait()

    return _pc(
        body, name="all_gather_rows",
        out_shape=jax.ShapeDtypeStruct((8 * m_per, n), x_shard.dtype),
        in_specs=[pl.BlockSpec(memory_space=pltpu.VMEM)],
        out_specs=pl.BlockSpec(memory_space=pltpu.VMEM),
        scratch_shapes=[pltpu.SemaphoreType.DMA((7,)), pltpu.SemaphoreType.DMA((7,)), pltpu.SemaphoreType.DMA],
        compiler_params=pltpu.CompilerParams(vmem_limit_bytes=VMEM_LIMIT),
    )(x_shard)


def _place():
    x, y, c = lax.axis_index("x"), lax.axis_index("y"), lax.axis_index("c")
    return x, y, c, [(1 - x, y), (x, 1 - y), (1 - x, 1 - y)]


def _remote(src, dst, send_sems, recv_sems, k, to):
    return pltpu.make_async_remote_copy(src_ref=src, dst_ref=dst, send_sem=send_sems.at[k], recv_sem=recv_sems.at[k],
                                        device_id=to, device_id_type=MESH)


GATHER_SEMS = 6
SCATTER_SEMS = 3


def _gather_phase(phase, pairs, send_sems, recv_sems, local_sems):
    x, y, c, chips = _place()
    kme = 2 * x + y
    sibling = (x, y, 1 - c)
    for a, (src, dst) in enumerate(pairs):
        half = src.shape[0] // 2
        mine = pl.ds(c * half, half)
        theirs = pl.ds((1 - c) * half, half)
        if phase == 0:
            pltpu.make_async_copy(src, dst.at[kme], local_sems.at[a]).start()
        if phase == 2:
            pltpu.make_async_copy(src, dst.at[kme], local_sems.at[a]).wait()
        for j, (px, py) in enumerate(chips):
            kk = 2 * px + py
            landed = dst.at[kk, mine]
            out = lambda: _remote(src.at[mine], dst.at[kme, mine], send_sems, recv_sems, 6 * a + j, (px, py, c))
            hand = lambda: _remote(landed, landed, send_sems, recv_sems, 6 * a + 3 + j, sibling)
            if phase == 0:
                out().start()
            if phase == 1:
                _remote(landed, landed, send_sems, recv_sems, 6 * a + j, (px, py, c)).wait_recv()
                hand().start()
            if phase == 2:
                other = dst.at[kk, theirs]
                _remote(other, other, send_sems, recv_sems, 6 * a + 3 + j, sibling).wait_recv()
                out().wait_send()
                hand().wait_send()


def _scatter_phase(phase, pairs, send_sems, recv_sems, local_sems):
    x, y, c, chips = _place()
    kme = 2 * x + y
    for a, (src, dst) in enumerate(pairs):
        loc = pltpu.make_async_copy(src.at[kme], dst.at[kme], local_sems.at[a])
        if phase == 0:
            loc.start()
        else:
            loc.wait()
        for j, (px, py) in enumerate(chips):
            kk = 2 * px + py
            out = _remote(src.at[kk], dst.at[kme], send_sems, recv_sems, 3 * a + j, (px, py, c))
            if phase == 0:
                out.start()
            else:
                landed = dst.at[kk]
                _remote(landed, landed, send_sems, recv_sems, 3 * a + j, (px, py, c)).wait_recv()
                out.wait_send()


def _comm_scratch(per_array, n_arrays):
    n = per_array * n_arrays
    return [pltpu.SemaphoreType.DMA((n,)), pltpu.SemaphoreType.DMA((n,)), pltpu.SemaphoreType.DMA((n_arrays,))]


def _slots(a):
    return jax.ShapeDtypeStruct((N_CHIPS,) + a.shape, a.dtype)


def _gather_weights(shards):
    n = len(shards)

    def body(*refs):
        for phase in range(3):
            _gather_phase(phase, tuple(zip(refs[:n], refs[n:2 * n])), *refs[2 * n:])

    hbm = pl.BlockSpec(memory_space=pl.ANY)
    return _pc(
        body, name="gather_weights", out_shape=tuple(_slots(a) for a in shards),
        in_specs=[hbm] * n, out_specs=(hbm,) * n, scratch_shapes=_comm_scratch(GATHER_SEMS, n),
    )(*shards)


def _scatter_slabs(slabs):
    n = len(slabs)

    def body(*refs):
        for phase in range(2):
            _scatter_phase(phase, tuple(zip(refs[:n], refs[n:2 * n])), *refs[2 * n:])

    hbm = pl.BlockSpec(memory_space=pl.ANY)
    return _pc(
        body, name="scatter_slabs", out_shape=tuple(jax.ShapeDtypeStruct(a.shape, a.dtype) for a in slabs),
        in_specs=[hbm] * n, out_specs=(hbm,) * n, scratch_shapes=_comm_scratch(SCATTER_SEMS, n),
    )(*slabs)


def _swap_with_sibling(a, b):
    def body(a_ref, b_ref, ra_ref, rb_ref, send_sems, recv_sems):
        x, y, c = lax.axis_index("x"), lax.axis_index("y"), lax.axis_index("c")
        copies = []
        for k, (src, dst) in enumerate(((a_ref, ra_ref), (b_ref, rb_ref))):
            cp = pltpu.make_async_remote_copy(
                src_ref=src, dst_ref=dst, send_sem=send_sems.at[k], recv_sem=recv_sems.at[k],
                device_id=(x, y, 1 - c), device_id_type=MESH)
            cp.start()
            copies.append(cp)
        for cp in copies:
            cp.wait()

    hbm = pl.BlockSpec(memory_space=pl.ANY)
    return _pc(
        body, name="swap_with_sibling",
        out_shape=(jax.ShapeDtypeStruct(a.shape, a.dtype), jax.ShapeDtypeStruct(b.shape, b.dtype)),
        in_specs=[hbm, hbm], out_specs=(hbm, hbm),
        scratch_shapes=[pltpu.SemaphoreType.DMA((2,)), pltpu.SemaphoreType.DMA((2,))],
    )(a, b)


def _mod_forward(c16, w_mod, b_mod_shard):
    def body(c_ref, w_ref, b_ref, s_ref, o_ref):
        cc = c_ref[...]
        s = cc * _sigmoid(cc)
        s_ref[...] = s
        o_ref[0] = jnp.dot(s, w_ref[0], preferred_element_type=F32, precision=lax.Precision.HIGHEST) + b_ref[0]

    return _pc(
        body, name="mod_forward", grid=(DEPTH,),
        out_shape=(jax.ShapeDtypeStruct((16, D_MODEL), F32), jax.ShapeDtypeStruct((DEPTH, 16, SHARD_MOD), F32)),
        in_specs=[_full((16, D_MODEL)),
                  pl.BlockSpec((1, D_MODEL, SHARD_MOD), lambda l: (l, 0, 0)),
                  pl.BlockSpec((1, 1, SHARD_MOD), lambda l: (l, 0, 0))],
        out_specs=(_full((16, D_MODEL)), pl.BlockSpec((1, 16, SHARD_MOD), lambda l: (l, 0, 0))),
        compiler_params=_cparams(),
    )(c16, w_mod, b_mod_shard)


def _mod_backward(s_t, g_rows, g_ctx, d_all, w_mod, c_ctx_col):
    def body(st_ref, g_ref, gc_ref, d_ref, w_ref, cc_ref, gw_ref, gb_ref, pc_ref):
        l = pl.program_id(0)
        gw_ref[0] = jnp.dot(st_ref[...], g_ref[0], preferred_element_type=F32, precision=lax.Precision.HIGHEST)
        gb_ref[0] = _colsum(d_ref[0])
        part = _rowsum(w_ref[0] * _colsum(gc_ref[0]))

        @pl.when(l == 0)
        def _():
            pc_ref[...] = jnp.zeros_like(pc_ref)

        pc_ref[...] += part

        @pl.when(l == DEPTH - 1)
        def _():
            cc = cc_ref[...]
            sg = _sigmoid(cc)
            pc_ref[...] = pc_ref[...] * (sg * (1.0 + cc * (1.0 - sg)))

    return _pc(
        body, name="mod_backward", grid=(DEPTH,),
        out_shape=(jax.ShapeDtypeStruct((DEPTH, D_MODEL, SHARD_MOD), F32),
                   jax.ShapeDtypeStruct((DEPTH, 1, 3 * D_MODEL), F32),
                   jax.ShapeDtypeStruct((D_MODEL, 1), F32)),
        in_specs=[_full((D_MODEL, LANES)),
                  pl.BlockSpec((1, LANES, SHARD_MOD), lambda l: (l, 0, 0)),
                  pl.BlockSpec((1, 8, SHARD_MOD), lambda l: (l, 0, 0)),
                  pl.BlockSpec((1, 16, 3 * D_MODEL), lambda l: (l, 0, 0)),
                  pl.BlockSpec((1, D_MODEL, SHARD_MOD), lambda l: (l, 0, 0)),
                  _full((D_MODEL, 1))],
        out_specs=(pl.BlockSpec((1, D_MODEL, SHARD_MOD), lambda l: (l, 0, 0)),
                   pl.BlockSpec((1, 1, 3 * D_MODEL), lambda l: (l, 0, 0)),
                   _full((D_MODEL, 1))),
        compiler_params=_cparams(),
    )(s_t, g_rows, g_ctx, d_all, w_mod, c_ctx_col)


def _head_norm(xb, lo):
    r = lax.rsqrt(_pair_sums(xb * xb, lo) * (1.0 / HEAD_DIM) + RMS_EPS)
    return xb * r, r


def _in_proj(xt, modv, g_pre, w_c, w_r, w_q, qk_gain, cos_t, sin_t):
    t = xt.shape[0]

    def body(x_ref, mod_ref, g_ref, wc_ref, wr_ref, wq_ref, gain_ref, cos_ref, sin_ref,
             h_ref, pc_ref, pr_ref, pq_ref, q_ref, k_ref, v_ref):
        lane = _lane(TM)
        lo = lane < HEAD_DIM
        lo16 = (lane & 31) < 16
        one = jnp.where(lane == HEAD_DIM, 1.0, 0.0)
        for jj in range(SUB):
            rows = pl.ds(jj * TM, TM)
            is_ctx = pl.program_id(0) * SUB + jj < N_CTX_TILES
            x = x_ref[rows, :]
            r = lax.rsqrt(jnp.mean(x * x, axis=1, keepdims=True) + RMS_EPS)
            sh = jnp.where(is_ctx, mod_ref[0:1, :], mod_ref[3:4, :])
            sc = jnp.where(is_ctx, mod_ref[1:2, :], mod_ref[4:5, :])
            h = (x * r * g_ref[...]) * (1.0 + sc) + sh
            hb = h.astype(BF16)
            h_ref[rows, :] = hb
            pc_ref[rows, :] = _dot(hb, wc_ref[...])
            pr_ref[rows, :] = _dot(hb, wr_ref[...])
            pq = _dot(hb, wq_ref[...])
            pq_ref[rows, :] = pq
            cos = cos_ref[rows, :]
            sin = sin_ref[rows, :]
            for b in range(3):
                xh, _ = _head_norm(pq[:, b * LANES:(b + 1) * LANES], lo)
                xg = xh * (gain_ref[0:1, :] if b < 2 else gain_ref[1:2, :])
                rot = xg * cos + _swap16(xg, lo16) * sin
                if b < 2:
                    rot = rot * ATTN_SCALE
                dst = q_ref if b < 2 else k_ref
                base = 2 * b if b < 2 else 0
                dst[base, rows, :] = jnp.where(lo, rot, 0.0).astype(BF16)
                dst[base + 1, rows, :] = jnp.where(lo, pltpu.roll(rot, HEAD_DIM, 1), 0.0).astype(BF16)
            vb = pq[:, 3 * LANES:4 * LANES]
            v_ref[0, rows, :] = jnp.where(lo, vb, one).astype(BF16)
            v_ref[1, rows, :] = jnp.where(lo, pltpu.roll(vb, HEAD_DIM, 1), one).astype(BF16)

    return _pc(
        body, name="in_proj", grid=(t // TB,),
        out_shape=(jax.ShapeDtypeStruct((t, D_MODEL), BF16),
                   jax.ShapeDtypeStruct((t, W_C), F32), jax.ShapeDtypeStruct((t, W_R), F32), jax.ShapeDtypeStruct((t, W_Q), F32),
                   jax.ShapeDtypeStruct((N_Q_HEADS, t, LANES), BF16),
                   jax.ShapeDtypeStruct((N_KV_HEADS, t, LANES), BF16),
                   jax.ShapeDtypeStruct((N_KV_HEADS, t, LANES), BF16)),
        in_specs=[_rows(D_MODEL, TB), _const((8, D_MODEL)), _const((1, D_MODEL)),
                  _const((D_MODEL, W_C)), _const((D_MODEL, W_R)), _const((D_MODEL, W_Q)),
                  _const((8, LANES)), _rows(LANES, TB), _rows(LANES, TB)],
        out_specs=(_rows(D_MODEL, TB), _rows(W_C, TB), _rows(W_R, TB), _rows(W_Q, TB),
                   _heads(N_Q_HEADS, LANES, TB), _heads(N_KV_HEADS, LANES, TB), _heads(N_KV_HEADS, LANES, TB)),
        compiler_params=_cparams(),
    )(xt, modv, g_pre, w_c, w_r, w_q, qk_gain, cos_t, sin_t)


def _attention_fwd(q, k, v, shards=None):
    t = q.shape[1]
    tk = _kv_chunk(t)
    n_chunks = (t - CTX_LEN) // tk
    n_tiles = t // TM
    n_sh = 0 if shards is None else len(shards)

    def body(q_ref, k_ref, v_ref, *rest):
        i = pl.program_id(0)
        o_ref = rest[n_sh]
        if shards is not None:
            pairs = tuple(zip(rest[:n_sh], rest[n_sh + 1:2 * n_sh + 1]))
            for phase, at in enumerate((0, n_tiles // 2, n_tiles - 1)):
                @pl.when(i == at)
                def _(phase=phase):
                    _gather_phase(phase, pairs, *rest[2 * n_sh + 1:])
        lane = _lane(GQA * TM)
        qs = [jnp.concatenate([q_ref[GQA * g + hh] for hh in range(GQA)], axis=0) for g in range(N_KV_HEADS)]

        def step(st, size, carry):
            out = []
            for g in range(N_KV_HEADS):
                m, acc = carry[g]
                s = _dot_nt(qs[g], k_ref[g, pl.ds(st, size), :])
                m_new = jnp.maximum(m, jnp.max(s, axis=1, keepdims=True))
                p = jnp.exp(s - m_new)
                out.append((m_new, acc * jnp.exp(m - m_new) + _dot(p.astype(BF16), v_ref[g, pl.ds(st, size), :])))
            return tuple(out)

        init = tuple((jnp.full((GQA * TM, 1), -jnp.inf, F32), jnp.zeros((GQA * TM, LANES), F32)) for _ in range(N_KV_HEADS))

        def finish(carry):
            for g in range(N_KV_HEADS):
                m, acc = carry[g]
                den = _rowsum(jnp.where(lane == HEAD_DIM, acc, 0.0))
                out = jnp.where(lane < HEAD_DIM, acc * (1.0 / den), jnp.where(lane == HEAD_DIM, m + jnp.log(den), 0.0))
                for hh in range(GQA):
                    o_ref[GQA * g + hh] = out[hh * TM:(hh + 1) * TM]

        @pl.when(i < N_CTX_TILES)
        def _():
            finish(step(0, CTX_LEN, init))

        @pl.when(i >= N_CTX_TILES)
        def _():
            per = 4 if n_chunks % 4 == 0 else 1

            def trip(j, cr):
                st = pl.multiple_of(CTX_LEN + j * (per * tk), 256)
                for u in range(per):
                    cr = step(st + u * tk, tk, cr)
                return cr

            finish(lax.fori_loop(0, n_chunks // per, trip, step(0, CTX_LEN, init)))

    hbm = pl.BlockSpec(memory_space=pl.ANY)
    extra = () if shards is None else tuple(shards)
    outs = _pc(
        body, name="attention_fwd" if shards is None else "attention_fwd_gather", grid=(n_tiles,),
        out_shape=(jax.ShapeDtypeStruct((N_Q_HEADS, t, LANES), F32),) + tuple(_slots(a) for a in extra),
        in_specs=[_heads(N_Q_HEADS, LANES), _full((N_KV_HEADS, t, LANES)), _full((N_KV_HEADS, t, LANES))] + [hbm] * n_sh,
        out_specs=(_heads(N_Q_HEADS, LANES),) + (hbm,) * n_sh,
        scratch_shapes=_comm_scratch(GATHER_SEMS, n_sh) if shards is not None else [],
        compiler_params=_cparams(),
    )(q, k, v, *extra)
    return outs[0], tuple(outs[1:])


def _halo_specs(width, t, rows=TM):
    last = t // HALO - 1
    per = rows // HALO
    prev = pl.BlockSpec((HALO, width), lambda i: (jnp.maximum(i * per - 1, 0), 0))
    nxt = pl.BlockSpec((HALO, width), lambda i: (jnp.minimum((i + 1) * per, last), 0))
    return prev, nxt


def _halo_valid(i, n_tiles):
    prev_ok = jnp.logical_and(i != 0, i != N_CTX_TILES)
    next_ok = jnp.logical_and(i != N_CTX_TILES - 1, i != n_tiles - 1)
    return jnp.where(prev_ok, 1.0, 0.0), jnp.where(next_ok, 1.0, 0.0)


def _conv_inputs(pc):
    u = pc[:, 0:GROUP_W] * pc[:, GROUP_W:2 * GROUP_W]
    z = pc[:, 2 * GROUP_W:3 * GROUP_W] * _sigmoid(pc[:, 3 * GROUP_W:4 * GROUP_W])
    return u, z


def _fill_ext(ext_ref, prev, mid, nxt):
    ext_ref[0:HALO, :] = prev
    ext_ref[HALO:HALO + TM, :] = mid
    ext_ref[HALO + TM:HALO + TM + HALO, :] = nxt


def _row_local_mixers(pr, ca, z2, oe, vecs, wss_ref, bsm, lane256):
    a_b, a_g, b_g = pr[:, 0:256], pr[:, 256:512], pr[:, 512:768]
    c_u, c_v, c_g, d_g = pr[:, 768:1024], pr[:, 1024:1280], pr[:, 1280:1536], pr[:, 1536:1792]
    zn, rs_b = _layer_norm_stats(z2)
    tb = zn * vecs[1:2, :] + vecs[2:3, :]
    vn_hat, rs_c = _layer_norm_stats(c_v)
    vn = vn_hat * vecs[3:4, :] + vecs[4:5, :]
    grp = jnp.right_shift(lane256, 6)
    sgs = []
    for ch in range(TM // CHUNK):
        r = _dot(wss_ref[...], vn[ch * CHUNK:(ch + 1) * CHUNK, :].astype(BF16))
        sgs.append(_group_select(r, grp[0:CHUNK]) + bsm)
    sg = jnp.concatenate(sgs, axis=0)
    lane = _lane(TM)
    lo = lane < HEAD_DIM
    att = jnp.concatenate([jnp.where(lo, oe[2 * b], pltpu.roll(oe[2 * b + 1], HEAD_DIM, 1)) for b in range(2)], axis=1)
    return dict(a_b=a_b, a_g=a_g, b_g=b_g, c_u=c_u, c_v=c_v, c_g=c_g, d_g=d_g, zn=zn, rs_b=rs_b, tb=tb,
                vn_hat=vn_hat, rs_c=rs_c, vn=vn, sg=sg, att=att, grp=grp, lo=lo, lane=lane)


def _mixer_concat(f, ca):
    ya = f["a_b"] * ca
    yb = f["tb"] * _sigmoid(f["tb"])
    yc = f["c_u"] * f["sg"]
    gates = [f[n] * _sigmoid(f[n]) for n in ("a_g", "b_g", "c_g", "d_g")]
    ys = (ya, yb, yc, f["att"])
    big = jnp.concatenate([yy * gg for yy, gg in zip(ys, gates)], axis=1).astype(BF16)
    return big, ys, gates


def _taps31(ext_ref, w_ref, flip):
    blocks = []
    for r0 in range(0, TM, CONV_ROWS):
        out = None
        for b in range(8):
            part = None
            for a in range(4):
                o = 8 * a + b
                if 1 <= o <= CONFORMER_K:
                    kk = CONFORMER_K - o if flip else o - 1
                    term = w_ref[kk:kk + 1, :] * ext_ref[pl.ds(r0 + 8 * a, CONV_ROWS + 8), :]
                    part = term if part is None else part + term
            part = part[b:b + CONV_ROWS]
            out = part if out is None else out + part
        blocks.append(out)
    return jnp.concatenate(blocks, axis=0)


def _mix_out(pc, pr, oe, xt, modv, g_post, w_out, conv_a, conv_b, vecs, wss, bsm, target=None):
    t = xt.shape[0]
    n_tiles = t // TM
    prev_spec, next_spec = _halo_specs(W_C, t, TB)
    n_t = 0 if target is None else SUB

    def body(pc_ref, pp_ref, pn_ref, pr_ref, oe_ref, x_ref, mod_ref, gp_ref, wo_ref, cva_ref, cvb_ref, vec_ref, wss_ref, bsm_ref,
             *rest):
        xo_ref, y_ref, ca_ref, z2_ref = rest[n_t:n_t + 4]
        uext, zext = rest[-2:]
        i = pl.program_id(0)
        vecs = vec_ref[...]
        lane256 = lax.broadcasted_iota(jnp.int32, (TM, GROUP_W), 1)
        if target is not None:
            loss_ref = rest[n_t + 4]

            @pl.when(i == 0)
            def _():
                loss_ref[...] = jnp.zeros_like(loss_ref)

        for jj in range(SUB):
            rows = pl.ds(jj * TM, TM)
            tile = i * SUB + jj
            is_ctx = tile < N_CTX_TILES
            pv, nv = _halo_valid(tile, n_tiles)
            u, z = _conv_inputs(pc_ref[rows, :])
            up, zp = _conv_inputs(pp_ref[...] if jj == 0 else pc_ref[pl.ds(jj * TM - HALO, HALO), :])
            un, zn_ = _conv_inputs(pn_ref[...] if jj == SUB - 1 else pc_ref[pl.ds((jj + 1) * TM, HALO), :])
            ue, ze = uext.at[jj], zext.at[jj]
            _fill_ext(ue, up * pv, u, un * nv)
            _fill_ext(ze, zp * pv, z, zn_ * nv)
            ca = cva_ref[0:1, :] * ue[pl.ds(HALO - 1, TM), :]
            for kk in range(1, SHORT_CONV_K):
                ca = ca + cva_ref[kk:kk + 1, :] * ue[pl.ds(HALO - 1 + kk, TM), :]
            z2 = _taps31(ze, cvb_ref, False) + vecs[0:1, :]
            ca_ref[rows, :] = ca
            z2_ref[rows, :] = z2
            oes = [oe_ref[h, rows, :] for h in range(N_Q_HEADS)]
            f = _row_local_mixers(pr_ref[rows, :], ca, z2, oes, vecs, wss_ref, bsm_ref[...], lane256)
            big, _, _ = _mixer_concat(f, ca)
            y = _dot(big, wo_ref[...])
            y_ref[rows, :] = y
            ry = lax.rsqrt(jnp.mean(y * y, axis=1, keepdims=True) + RMS_EPS)
            gt = jnp.where(is_ctx, mod_ref[2:3, :], mod_ref[5:6, :])
            x_new = x_ref[rows, :] + gt * (y * ry * gp_ref[...])
            if target is None:
                xo_ref[rows, :] = x_new
            else:
                err = (x_new - rest[jj][...]) * jnp.where(is_ctx, 0.0, 1.0)
                xo_ref[rows, :] = err * (1.0 / D_MODEL)
                loss_ref[...] += jnp.sum(err * err) * (0.5 / D_MODEL)

    rows_f32 = jax.ShapeDtypeStruct((t, D_MODEL), F32)
    group_f32 = jax.ShapeDtypeStruct((t, GROUP_W), F32)
    loss_shape, loss_spec, t_spec, t_arg = (), (), [], ()
    if target is not None:
        loss_shape, loss_spec = (jax.ShapeDtypeStruct((8, LANES), F32),), (_full((8, LANES)),)
        t_spec = [pl.BlockSpec((TM, D_MODEL), lambda i, jj=jj: (jnp.maximum(i * SUB + jj - N_CTX_TILES, 0), 0))
                  for jj in range(SUB)]
        t_arg = (target,) * SUB
    return _pc(
        body, name="mix_out" if target is None else "mix_out_loss", grid=(t // TB,),
        out_shape=(rows_f32, rows_f32, group_f32, group_f32) + loss_shape,
        in_specs=[_rows(W_C, TB), prev_spec, next_spec, _rows(W_R, TB), _heads(N_Q_HEADS, LANES, TB), _rows(D_MODEL, TB),
                  _const((8, D_MODEL)), _const((1, D_MODEL)), _const((D_MODEL, D_MODEL)),
                  _const((8, GROUP_W)), _const((32, GROUP_W)), _const((8, GROUP_W)),
                  _const((N_SPATIAL_GROUPS * CHUNK, CHUNK)), _const((CHUNK, GROUP_W))] + t_spec,
        out_specs=(_rows(D_MODEL, TB), _rows(D_MODEL, TB), _rows(GROUP_W, TB), _rows(GROUP_W, TB)) + loss_spec,
        scratch_shapes=[pltpu.VMEM((SUB, TM + 2 * HALO, GROUP_W), F32), pltpu.VMEM((SUB, TM + 2 * HALO, GROUP_W), F32)],
        compiler_params=_cparams(),
    )(pc, pc, pc, pr, oe, xt, modv, g_post, w_out, conv_a, conv_b, vecs, wss, bsm, *t_arg)


def _mix_out_bwd(dxo, y, pr, ca, z2, oe, modv, g_post, w_out, vecs, wss, wsts, bsm):
    t = y.shape[0]
    n_tiles = t // TM

    def body(dxo_ref, y_ref, pr_ref, ca_ref, z2_ref, oe_ref, mod_ref, gp_ref, wo_ref, vec_ref, wss_ref, wsts_ref, bsm_ref,
             dpr_ref, ga_ref, gb_ref, doe_ref, dwo_ref, pvec_ref, s256_ref, dws_ref, dbs_ref, dbsm):
        i = pl.program_id(0)
        is_ctx = i < N_CTX_TILES

        @pl.when(i == 0)
        def _():
            dwo_ref[...] = jnp.zeros_like(dwo_ref)
            pvec_ref[...] = jnp.zeros_like(pvec_ref)
            s256_ref[...] = jnp.zeros_like(s256_ref)
            dws_ref[...] = jnp.zeros_like(dws_ref)
            dbsm[...] = jnp.zeros_like(dbsm)

        dxo_ = dxo_ref[...]
        y_ = y_ref[...]
        ry = lax.rsqrt(jnp.mean(y_ * y_, axis=1, keepdims=True) + RMS_EPS)
        nh = y_ * ry
        gp = gp_ref[...]
        gt = jnp.where(is_ctx, mod_ref[2:3, :], mod_ref[5:6, :])
        dgt = _colsum(dxo_ * (nh * gp))
        pvec_ref[0:1, :] += jnp.where(is_ctx, dgt, 0.0)
        pvec_ref[1:2, :] += jnp.where(is_ctx, 0.0, dgt)
        dn = dxo_ * gt
        pvec_ref[2:3, :] += _colsum(dn * nh)
        dnh = dn * gp
        dy = ry * (dnh - nh * jnp.mean(dnh * nh, axis=1, keepdims=True))

        vecs = vec_ref[...]
        bsm_ = bsm_ref[...]
        ca_ = ca_ref[...]
        lane256 = lax.broadcasted_iota(jnp.int32, (TM, GROUP_W), 1)
        f = _row_local_mixers(pr_ref[...], ca_, z2_ref[...], oe_ref, vecs, wss_ref, bsm_, lane256)
        big, ys, gates = _mixer_concat(f, ca_)
        dyb = dy.astype(BF16)
        dwo_ref[...] += _dot_tn(big, dyb)
        dbig = _dot_nt(dyb, wo_ref[...])

        d_y, d_gate = [], []
        for n, (name, yy, gg) in enumerate(zip(("a_g", "b_g", "c_g", "d_g"), ys, gates)):
            dpart = dbig[:, n * GROUP_W:(n + 1) * GROUP_W]
            gx = f[name]
            sg_ = _sigmoid(gx)
            d_y.append(dpart * gg)
            d_gate.append(dpart * yy * (sg_ * (1.0 + gx * (1.0 - sg_))))
        dya, dyb_, dyc, datt = d_y

        d_ab = dya * ca_
        ga_ref[...] = dya * f["a_b"]
        tb = f["tb"]
        sb = _sigmoid(tb)
        dtb = dyb_ * (sb * (1.0 + tb * (1.0 - sb)))
        s256_ref[1:2, :] += _colsum(dtb * f["zn"])
        s256_ref[2:3, :] += _colsum(dtb)
        dz2 = _layer_norm_bwd(dtb * vecs[1:2, :], f["zn"], f["rs_b"])
        gb_ref[...] = dz2
        s256_ref[0:1, :] += _colsum(dz2)
        d_cu = dyc * f["sg"]
        dsg = dyc * f["c_u"]
        grp = f["grp"]
        dvn_parts = []
        for ch in range(TM // CHUNK):
            rows = slice(ch * CHUNK, (ch + 1) * CHUNK)
            dsg_c = dsg[rows, :]
            dbsm[...] += dsg_c
            vn_c = f["vn"][rows, :].astype(BF16)
            for g in range(N_SPATIAL_GROUPS):
                masked = jnp.where(grp[0:CHUNK] == g, dsg_c, 0.0).astype(BF16)
                dws_ref[g * CHUNK:(g + 1) * CHUNK, :] += _dot_nt(masked, vn_c)
            dvn_parts.append(_group_select(_dot(wsts_ref[...], dsg_c.astype(BF16)), grp[0:CHUNK]))
        dvn = jnp.concatenate(dvn_parts, axis=0)
        s256_ref[3:4, :] += _colsum(dvn * f["vn_hat"])
        s256_ref[4:5, :] += _colsum(dvn)
        d_cv = _layer_norm_bwd(dvn * vecs[3:4, :], f["vn_hat"], f["rs_c"])
        lane, lo = f["lane"], f["lo"]
        att = f["att"]
        for b in range(2):
            da = datt[:, b * LANES:(b + 1) * LANES]
            prod = da * att[:, b * LANES:(b + 1) * LANES]
            for hh in range(2):
                h = 2 * b + hh
                lse = _rowsum(jnp.where(lane == HEAD_DIM, oe_ref[h], 0.0))
                delta = _rowsum(jnp.where(lo, prod, 0.0) if hh == 0 else jnp.where(lo, 0.0, prod))
                dah = da if hh == 0 else pltpu.roll(da, HEAD_DIM, 1)
                doe_ref[h] = jnp.where(lo, dah, jnp.where(lane == HEAD_DIM, delta, jnp.where(lane == HEAD_DIM + 1, lse, 0.0)))

        dpr_ref[...] = jnp.concatenate([d_ab, d_gate[0], d_gate[1], d_cu, d_cv, d_gate[2], d_gate[3]], axis=1).astype(BF16)

        @pl.when(i == n_tiles - 1)
        def _():
            acc = dbsm[...]
            lane128 = _lane(CHUNK)
            out = jnp.zeros((CHUNK, LANES), F32)
            for g in range(N_SPATIAL_GROUPS):
                col = _rowsum(jnp.where(grp[0:CHUNK] == g, acc, 0.0))
                out = out + jnp.where(lane128 == g, col, 0.0)
            dbs_ref[...] = out

    return _pc(
        body, name="mix_out_bwd", grid=(n_tiles,),
        out_shape=(jax.ShapeDtypeStruct((t, W_R), BF16),
                   jax.ShapeDtypeStruct((t, GROUP_W), F32), jax.ShapeDtypeStruct((t, GROUP_W), F32),
                   jax.ShapeDtypeStruct((N_Q_HEADS, t, LANES), F32),
                   jax.ShapeDtypeStruct((D_MODEL, D_MODEL), F32),
                   jax.ShapeDtypeStruct((8, D_MODEL), F32),
                   jax.ShapeDtypeStruct((8, GROUP_W), F32),
                   jax.ShapeDtypeStruct((N_SPATIAL_GROUPS * CHUNK, CHUNK), F32),
                   jax.ShapeDtypeStruct((CHUNK, LANES), F32)),
        in_specs=[_rows(D_MODEL), _rows(D_MODEL), _rows(W_R), _rows(GROUP_W), _rows(GROUP_W), _heads(N_Q_HEADS, LANES),
                  _full((8, D_MODEL)), _full((1, D_MODEL)), _full((D_MODEL, D_MODEL)), _full((8, GROUP_W)),
                  _full((N_SPATIAL_GROUPS * CHUNK, CHUNK)), _full((N_SPATIAL_GROUPS * CHUNK, CHUNK)), _full((CHUNK, GROUP_W))],
        out_specs=(_rows(W_R), _rows(GROUP_W), _rows(GROUP_W), _heads(N_Q_HEADS, LANES),
                   _full((D_MODEL, D_MODEL)), _full((8, D_MODEL)), _full((8, GROUP_W)),
                   _full((N_SPATIAL_GROUPS * CHUNK, CHUNK)), _full((CHUNK, LANES))),
        scratch_shapes=[pltpu.VMEM((CHUNK, GROUP_W), F32)],
        compiler_params=_cparams(),
    )(dxo, y, pr, ca, z2, oe, modv, g_post, w_out, vecs, wss, wsts, bsm)


def _conv_bwd(pc, g_a, g_b, conv_a, conv_b):
    t = pc.shape[0]
    n_tiles = t // TM
    pc_prev, pc_next = _halo_specs(W_C, t, TB)
    g_prev, g_next = _halo_specs(GROUP_W, t, TB)

    def body(pc_ref, pp_ref, pn_ref, ga_ref, gap_ref, gan_ref, gb_ref, gbp_ref, gbn_ref, cva_ref, cvb_ref,
             dpc_ref, dca_ref, dcb_ref, uext, zext, gaext, gbext):
        i = pl.program_id(0)

        @pl.when(i == 0)
        def _():
            dca_ref[...] = jnp.zeros_like(dca_ref)
            dcb_ref[...] = jnp.zeros_like(dcb_ref)

        def halo(jj, tile_ref, prev_ref, next_ref):
            before = prev_ref[...] if jj == 0 else tile_ref[pl.ds(jj * TM - HALO, HALO), :]
            after = next_ref[...] if jj == SUB - 1 else tile_ref[pl.ds((jj + 1) * TM, HALO), :]
            return before, after

        for jj in range(SUB):
            rows = pl.ds(jj * TM, TM)
            pv, nv = _halo_valid(i * SUB + jj, n_tiles)
            pc_ = pc_ref[rows, :]
            u, z = _conv_inputs(pc_)
            pc_before, pc_after = halo(jj, pc_ref, pp_ref, pn_ref)
            up, zp = _conv_inputs(pc_before)
            un, zn_ = _conv_inputs(pc_after)
            ue, ze, gae, gbe = uext.at[jj], zext.at[jj], gaext.at[jj], gbext.at[jj]
            _fill_ext(ue, up * pv, u, un * nv)
            _fill_ext(ze, zp * pv, z, zn_ * nv)
            ga = ga_ref[rows, :]
            gb = gb_ref[rows, :]
            ga_before, ga_after = halo(jj, ga_ref, gap_ref, gan_ref)
            gb_before, gb_after = halo(jj, gb_ref, gbp_ref, gbn_ref)
            _fill_ext(gae, ga_before * pv, ga, ga_after * nv)
            _fill_ext(gbe, gb_before * pv, gb, gb_after * nv)

            du = cva_ref[0:1, :] * gae[pl.ds(HALO + 1, TM), :]
            dca_ref[0:1, :] += _colsum(ga * ue[pl.ds(HALO - 1, TM), :])
            for kk in range(1, SHORT_CONV_K):
                du = du + cva_ref[kk:kk + 1, :] * gae[pl.ds(HALO + 1 - kk, TM), :]
                dca_ref[kk:kk + 1, :] += _colsum(ga * ue[pl.ds(HALO - 1 + kk, TM), :])
            dz = _taps31(gbe, cvb_ref, True)
            for r0 in range(0, TM, CONV_ROWS):
                gb_rows = gb_ref[pl.ds(jj * TM + r0, CONV_ROWS), :]
                for b in range(8):
                    zb = ze[pl.ds(r0 + b, CONV_ROWS + 24), :]
                    for a in range(4):
                        kk = 8 * a + b - 1
                        if 0 <= kk < CONFORMER_K:
                            dcb_ref[kk:kk + 1, :] += _colsum(gb_rows * zb[8 * a:8 * a + CONV_ROWS])

            a_c, a_h = pc_[:, 0:GROUP_W], pc_[:, GROUP_W:2 * GROUP_W]
            glu_a, glu_g = pc_[:, 2 * GROUP_W:3 * GROUP_W], pc_[:, 3 * GROUP_W:4 * GROUP_W]
            sg = _sigmoid(glu_g)
            dpc_ref[rows, :] = jnp.concatenate([du * a_h, du * a_c, dz * sg, dz * glu_a * sg * (1.0 - sg)], axis=1).astype(BF16)

    ext = pltpu.VMEM((SUB, TM + 2 * HALO, GROUP_W), F32)
    return _pc(
        body, name="conv_bwd", grid=(t // TB,),
        out_shape=(jax.ShapeDtypeStruct((t, W_C), BF16), jax.ShapeDtypeStruct((8, GROUP_W), F32), jax.ShapeDtypeStruct((32, GROUP_W), F32)),
        in_specs=[_rows(W_C, TB), pc_prev, pc_next, _rows(GROUP_W, TB), g_prev, g_next, _rows(GROUP_W, TB), g_prev, g_next,
                  _const((8, GROUP_W)), _const((32, GROUP_W))],
        out_specs=(_rows(W_C, TB), _full((8, GROUP_W)), _full((32, GROUP_W))),
        scratch_shapes=[ext, ext, ext, ext],
        compiler_params=_cparams(),
    )(pc, pc, pc, g_a, g_a, g_a, g_b, g_b, g_b, conv_a, conv_b)


def _attention_bwd(q, k, v, doe, slabs=None):
    t = q.shape[1]
    tk = _kv_chunk(t)
    n_chunks = (t - CTX_LEN) // tk
    n_tiles = t // TM
    n_sl = 0 if slabs is None else len(slabs)

    def body(q_ref, do_ref, k_ref, v_ref, *rest):
        i = pl.program_id(0)
        dq_ref, dk_hbm, dv_hbm = rest[n_sl:n_sl + 3]
        dk_acc, dv_acc = rest[2 * n_sl + 3:2 * n_sl + 5]
        pairs = tuple(zip(rest[:n_sl], rest[n_sl + 3:2 * n_sl + 3]))
        sems = rest[2 * n_sl + 5:]

        @pl.when(i == 0)
        def _():
            dk_acc[...] = jnp.zeros_like(dk_acc)
            dv_acc[...] = jnp.zeros_like(dv_acc)
            if slabs is not None:
                _scatter_phase(0, pairs, *sems)

        lane = _lane(GQA * TM)
        lo = lane < HEAD_DIM
        qs, dos, deltas, lses = [], [], [], []
        for g in range(N_KV_HEADS):
            qs.append(jnp.concatenate([q_ref[GQA * g + hh] for hh in range(GQA)], axis=0))
            dog = jnp.concatenate([do_ref[GQA * g + hh] for hh in range(GQA)], axis=0)
            deltas.append(_rowsum(jnp.where(lane == HEAD_DIM, dog, 0.0)))
            lses.append(_rowsum(jnp.where(lane == HEAD_DIM + 1, dog, 0.0)))
            dos.append(jnp.where(lo, dog, 0.0).astype(BF16))

        def step(st, size, dqs):
            out = []
            for g in range(N_KV_HEADS):
                kc = k_ref[g, pl.ds(st, size), :]
                vc = v_ref[g, pl.ds(st, size), :]
                p = jnp.exp(_dot_nt(qs[g], kc) - lses[g])
                ds_ = (p * (_dot_nt(dos[g], vc) - deltas[g])).astype(BF16)
                dk_acc[g, pl.ds(st, size), :] += _dot_tn(ds_, qs[g])
                dv_acc[g, pl.ds(st, size), :] += _dot_tn(p.astype(BF16), dos[g])
                out.append(dqs[g] + _dot(ds_, kc))
            return tuple(out)

        zero = tuple(jnp.zeros((GQA * TM, LANES), F32) for _ in range(N_KV_HEADS))

        def finish(dqs):
            for g in range(N_KV_HEADS):
                for hh in range(GQA):
                    dq_ref[GQA * g + hh] = dqs[g][hh * TM:(hh + 1) * TM]

        @pl.when(i < N_CTX_TILES)
        def _():
            finish(step(0, CTX_LEN, zero))

        @pl.when(i >= N_CTX_TILES)
        def _():
            finish(lax.fori_loop(0, n_chunks, lambda j, acc: step(pl.multiple_of(CTX_LEN + j * tk, 256), tk, acc),
                                 step(0, CTX_LEN, zero)))

        @pl.when(i == n_tiles - 1)
        def _():
            pltpu.sync_copy(dk_acc, dk_hbm)
            pltpu.sync_copy(dv_acc, dv_hbm)
            if slabs is not None:
                _scatter_phase(1, pairs, *sems)

    kv_shape = jax.ShapeDtypeStruct((N_KV_HEADS, t, LANES), F32)
    hbm = pl.BlockSpec(memory_space=pl.ANY)
    extra = () if slabs is None else tuple(slabs)
    outs = _pc(
        body, name="attention_bwd" if slabs is None else "attention_bwd_scatter", grid=(n_tiles,),
        out_shape=(jax.ShapeDtypeStruct((N_Q_HEADS, t, LANES), F32), kv_shape, kv_shape)
        + tuple(jax.ShapeDtypeStruct(a.shape, a.dtype) for a in extra),
        in_specs=[_heads(N_Q_HEADS, LANES), _heads(N_Q_HEADS, LANES),
                  _full((N_KV_HEADS, t, LANES)), _full((N_KV_HEADS, t, LANES))] + [hbm] * n_sl,
        out_specs=(_heads(N_Q_HEADS, LANES), hbm, hbm) + (hbm,) * n_sl,
        scratch_shapes=[pltpu.VMEM((N_KV_HEADS, t, LANES), F32), pltpu.VMEM((N_KV_HEADS, t, LANES), F32)]
        + (_comm_scratch(SCATTER_SEMS, n_sl) if slabs is not None else []),
        compiler_params=_cparams(),
    )(q, doe, k, v, *extra)
    return outs[0], outs[1], outs[2], tuple(outs[3:])


def _in_proj_bwd(dpc, dpr, dq, dk, dv, pq, qk_gain, cos_t, sin_t, w_c, w_r, w_q, xt, dxo, modv, g_pre):
    t = xt.shape[0]

    def body(dpc_ref, dpr_ref, dq_ref, dk_ref, dv_ref, pq_ref, gain_ref, cos_ref, sin_ref, wc_ref, wr_ref, wq_ref,
             x_ref, dxo_ref, mod_ref, g_ref, dx_ref, dpq_ref, acc_ref, dgain_ref):
        i = pl.program_id(0)

        @pl.when(i == 0)
        def _():
            acc_ref[...] = jnp.zeros_like(acc_ref)
            dgain_ref[...] = jnp.zeros_like(dgain_ref)

        lane = _lane(TM)
        lo = lane < HEAD_DIM
        lo16 = (lane & 31) < 16
        g = g_ref[...]
        for jj in range(SUB):
            rows = pl.ds(jj * TM, TM)
            is_ctx = i * SUB + jj < N_CTX_TILES
            cos = cos_ref[rows, :]
            sin = sin_ref[rows, :]
            outs = []
            for b in range(3):
                src = dq_ref if b < 2 else dk_ref
                base = 2 * b if b < 2 else 0
                drot = src[base, rows, :] + pltpu.roll(src[base + 1, rows, :], HEAD_DIM, 1)
                if b < 2:
                    drot = drot * ATTN_SCALE
                dxg = drot * cos + _swap16(drot * sin, lo16)
                xh, r = _head_norm(pq_ref[rows, b * LANES:(b + 1) * LANES], lo)
                row = 0 if b < 2 else 1
                dgain_ref[row:row + 1, :] += _colsum(dxg * xh)
                dxh = dxg * gain_ref[row:row + 1, :]
                outs.append(r * (dxh - xh * (_pair_sums(dxh * xh, lo) * (1.0 / HEAD_DIM))))
            outs.append(dv_ref[0, rows, :] + pltpu.roll(dv_ref[1, rows, :], HEAD_DIM, 1))
            dpq = jnp.concatenate(outs, axis=1).astype(BF16)
            dpq_ref[rows, :] = dpq

            dh = _dot_nt(dpc_ref[rows, :], wc_ref[...]) + _dot_nt(dpr_ref[rows, :], wr_ref[...]) + _dot_nt(dpq, wq_ref[...])
            x = x_ref[rows, :]
            r = lax.rsqrt(jnp.mean(x * x, axis=1, keepdims=True) + RMS_EPS)
            xn = x * r
            sc = jnp.where(is_ctx, mod_ref[1:2, :], mod_ref[4:5, :])
            dsh = _colsum(dh)
            dsc = _colsum(dh * (xn * g))
            acc_ref[0:1, :] += jnp.where(is_ctx, dsh, 0.0)
            acc_ref[1:2, :] += jnp.where(is_ctx, dsc, 0.0)
            acc_ref[2:3, :] += jnp.where(is_ctx, 0.0, dsh)
            acc_ref[3:4, :] += jnp.where(is_ctx, 0.0, dsc)
            dxg = dh * (1.0 + sc)
            acc_ref[4:5, :] += _colsum(dxg * xn)
            dxn = dxg * g
            dx_ref[rows, :] = r * (dxn - xn * jnp.mean(dxn * xn, axis=1, keepdims=True)) + dxo_ref[rows, :]

    return _pc(
        body, name="in_proj_bwd", grid=(t // TB,),
        out_shape=(jax.ShapeDtypeStruct((t, D_MODEL), F32), jax.ShapeDtypeStruct((t, W_Q), BF16),
                   jax.ShapeDtypeStruct((8, D_MODEL), F32), jax.ShapeDtypeStruct((8, LANES), F32)),
        in_specs=[_rows(W_C, TB), _rows(W_R, TB),
                  _heads(N_Q_HEADS, LANES, TB), _heads(N_KV_HEADS, LANES, TB), _heads(N_KV_HEADS, LANES, TB), _rows(W_Q, TB),
                  _const((8, LANES)), _rows(LANES, TB), _rows(LANES, TB),
                  _const((D_MODEL, W_C)), _const((D_MODEL, W_R)), _const((D_MODEL, W_Q)),
                  _rows(D_MODEL, TB), _rows(D_MODEL, TB), _const((8, D_MODEL)), _const((1, D_MODEL))],
        out_specs=(_rows(D_MODEL, TB), _rows(W_Q, TB), _full((8, D_MODEL)), _full((8, LANES))),
        compiler_params=_cparams(),
    )(dpc, dpr, dq, dk, dv, pq, qk_gain, cos_t, sin_t, w_c, w_r, w_q, xt, dxo, modv, g_pre)


def _in_proj_wgrad(h, dpc, dpr, dpq):
    t = h.shape[0]

    def body(h_ref, dpc_ref, dpr_ref, dpq_ref, gc_ref, gr_ref, gq_ref):
        @pl.when(pl.program_id(0) == 0)
        def _():
            gc_ref[...] = jnp.zeros_like(gc_ref)
            gr_ref[...] = jnp.zeros_like(gr_ref)
            gq_ref[...] = jnp.zeros_like(gq_ref)

        hb = h_ref[...]
        gc_ref[...] += _dot_tn(hb, dpc_ref[...])
        gr_ref[...] += _dot_tn(hb, dpr_ref[...])
        gq_ref[...] += _dot_tn(hb, dpq_ref[...])

    return _pc(
        body, name="in_proj_wgrad", grid=(t // TB,),
        out_shape=(jax.ShapeDtypeStruct((D_MODEL, W_C), F32), jax.ShapeDtypeStruct((D_MODEL, W_R), F32),
                   jax.ShapeDtypeStruct((D_MODEL, W_Q), F32)),
        in_specs=[_rows(D_MODEL, TB), _rows(W_C, TB), _rows(W_R, TB), _rows(W_Q, TB)],
        out_specs=(_full((D_MODEL, W_C)), _full((D_MODEL, W_R)), _full((D_MODEL, W_Q))),
        compiler_params=_cparams(),
    )(h, dpc, dpr, dpq)


def _sum_slabs(slabs, tile_rows):
    n, r, c = slabs.shape

    def body(s_ref, o_ref):
        acc = s_ref[0].astype(F32)
        for k in range(1, n):
            acc = acc + s_ref[k].astype(F32)
        o_ref[...] = acc

    return _pc(
        body, name="sum_slabs", grid=(r // tile_rows,),
        out_shape=jax.ShapeDtypeStruct((r, c), F32),
        in_specs=[pl.BlockSpec((n, tile_rows, c), lambda i: (0, i, 0))],
        out_specs=pl.BlockSpec((tile_rows, c), lambda i: (i, 0)),
        compiler_params=_cparams(),
    )(slabs)


def _sum_layer_slabs(layers, tile_rows):
    nl = len(layers)
    n, r, c = layers[0].shape
    per = r // tile_rows

    def body(*refs):
        o_ref = refs[nl]
        for l in range(nl):
            @pl.when(pl.program_id(0) // per == l)
            def _(l=l):
                acc = refs[l][0].astype(F32)
                for k in range(1, n):
                    acc = acc + refs[l][k].astype(F32)
                o_ref[...] = acc

    def spec(l):
        return pl.BlockSpec((n, tile_rows, c), lambda i: (0, jnp.clip(i - l * per, 0, per - 1), 0))

    return _pc(
        body, name="sum_layer_slabs", grid=(nl * per,),
        out_shape=jax.ShapeDtypeStruct((nl * r, c), F32),
        in_specs=[spec(l) for l in range(nl)],
        out_specs=pl.BlockSpec((tile_rows, c), lambda i: (i, 0)),
        compiler_params=_cparams(),
    )(*layers)


def _adamw(grads, w, m, v, tile_rows):
    r, c = w.shape
    n_g = len(grads)

    def body(*refs):
        g = refs[0][...]
        for k in range(1, n_g):
            g = g + refs[k][...]
        w_ref, m_ref, v_ref, g_out, d_out, m_out, v_out = refs[n_g:]
        m_new = ADAM_B1 * m_ref[...] + (1.0 - ADAM_B1) * g
        v_new = ADAM_B2 * v_ref[...] + (1.0 - ADAM_B2) * (g * g)
        m_hat = m_new / (1.0 - ADAM_B1 ** ADAM_STEP)
        v_hat = v_new / (1.0 - ADAM_B2 ** ADAM_STEP)
        g_out[...] = g
        d_out[...] = -ADAM_LR * (m_hat / (jnp.sqrt(v_hat) + ADAM_EPS) + ADAM_WD * w_ref[...])
        m_out[...] = m_new
        v_out[...] = v_new

    spec = pl.BlockSpec((tile_rows, c), lambda i: (i, 0))
    shape = jax.ShapeDtypeStruct((r, c), F32)
    return _pc(
        body, name="adamw", grid=(r // tile_rows,),
        out_shape=(shape,) * 4, in_specs=[spec] * (n_g + 3), out_specs=(spec,) * 4,
        compiler_params=_cparams(),
    )(*grads, w, m, v)


def _rope_tables(s_lat):
    n_rows = s_lat // GRID_W
    axis_dim = HEAD_DIM // 2
    inv_freq = 1.0 / (ROPE_THETA ** (jnp.arange(0, axis_dim, 2, dtype=F32) / axis_dim))
    d = np.arange(LANES) % HEAD_DIM
    on_rows = (d // axis_dim) == 0
    freq = d % (axis_dim // 2)
    sign = np.where((d % axis_dim) < axis_dim // 2, -1.0, 1.0).astype(np.float32)
    ang_r = jnp.arange(n_rows, dtype=F32)[:, None] * inv_freq[freq][None, :]
    ang_c = jnp.arange(GRID_W, dtype=F32)[:, None] * inv_freq[freq][None, :]

    def spread(fn):
        full = jnp.where(on_rows[None, None, :], fn(ang_r)[:, None, :], fn(ang_c)[None, :, :])
        return full.reshape(s_lat, LANES)

    cos = jnp.concatenate([jnp.ones((CTX_LEN, LANES), F32), spread(jnp.cos)], axis=0)
    sin = jnp.concatenate([jnp.zeros((CTX_LEN, LANES), F32), spread(jnp.sin) * sign[None, :]], axis=0)
    return cos, sin


def _pad_rows(a, rows):
    return jnp.concatenate([a, jnp.zeros((rows - a.shape[0],) + a.shape[1:], a.dtype)], axis=0)


_SMALL = ("c_ctx", "b_mod", "g_pre", "g_post", "conv_a", "conv_b", "conv_b_bias", "conf_ln_g", "conf_ln_b",
          "sgu_ln_g", "sgu_ln_b", "w_s", "b_s", "q_gain", "k_gain")


def _pack(arrays):
    flat = jnp.concatenate([a.reshape(-1) for a in arrays])
    rows = -(-flat.shape[0] // (16 * LANES)) * 16
    return _pad_rows(flat.reshape(-1, 1), rows * LANES).reshape(rows, LANES)


def _unpack(packed, shapes):
    flat = packed.reshape(-1)
    out, off = [], 0
    for s in shapes:
        n = int(np.prod(s))
        out.append(flat[off:off + n].reshape(s))
        off += n
    return out


def kernel(x, c, ctx, c_ctx, w_mod, b_mod, g_pre, g_post, w_in, w_out, conv_a, conv_b, conv_b_bias, conf_ln_g, conf_ln_b, sgu_ln_g, sgu_ln_b, w_s, b_s, q_gain, k_gain, loss_target, m_c_ctx, m_w_mod, m_b_mod, m_g_pre, m_g_post, m_w_in, m_w_out, m_conv_a, m_conv_b, m_conv_b_bias, m_conf_ln_g, m_conf_ln_b, m_sgu_ln_g, m_sgu_ln_b, m_w_s, m_b_s, m_q_gain, m_k_gain, v_c_ctx, v_w_mod, v_b_mod, v_g_pre, v_g_post, v_w_in, v_w_out, v_conv_a, v_conv_b, v_conv_b_bias, v_conf_ln_g, v_conf_ln_b, v_sgu_ln_g, v_sgu_ln_b, v_w_s, v_b_s, v_q_gain, v_k_gain):
    weights = dict(c_ctx=c_ctx, w_mod=w_mod, b_mod=b_mod, g_pre=g_pre, g_post=g_post, w_in=w_in, w_out=w_out, conv_a=conv_a,
                   conv_b=conv_b, conv_b_bias=conv_b_bias, conf_ln_g=conf_ln_g, conf_ln_b=conf_ln_b, sgu_ln_g=sgu_ln_g,
                   sgu_ln_b=sgu_ln_b, w_s=w_s, b_s=b_s, q_gain=q_gain, k_gain=k_gain)
    m_in = dict(c_ctx=m_c_ctx, w_mod=m_w_mod, b_mod=m_b_mod, g_pre=m_g_pre, g_post=m_g_post, w_in=m_w_in, w_out=m_w_out,
                conv_a=m_conv_a, conv_b=m_conv_b, conv_b_bias=m_conv_b_bias, conf_ln_g=m_conf_ln_g, conf_ln_b=m_conf_ln_b,
                sgu_ln_g=m_sgu_ln_g, sgu_ln_b=m_sgu_ln_b, w_s=m_w_s, b_s=m_b_s, q_gain=m_q_gain, k_gain=m_k_gain)
    v_in = dict(c_ctx=v_c_ctx, w_mod=v_w_mod, b_mod=v_b_mod, g_pre=v_g_pre, g_post=v_g_post, w_in=v_w_in, w_out=v_w_out,
                conv_a=v_conv_a, conv_b=v_conv_b, conv_b_bias=v_conv_b_bias, conf_ln_g=v_conf_ln_g, conf_ln_b=v_conf_ln_b,
                sgu_ln_g=v_sgu_ln_g, sgu_ln_b=v_sgu_ln_b, w_s=v_w_s, b_s=v_b_s, q_gain=v_q_gain, k_gain=v_k_gain)
    order = ("c_ctx", "w_mod", "b_mod", "g_pre", "g_post", "w_in", "w_out", "conv_a", "conv_b", "conv_b_bias", "conf_ln_g",
             "conf_ln_b", "sgu_ln_g", "sgu_ln_b", "w_s", "b_s", "q_gain", "k_gain")

    s_lat = x.shape[1]
    ax, ay, ac = lax.axis_index("x"), lax.axis_index("y"), lax.axis_index("c")
    chip = 2 * ax + ay
    example = 4 * ax + 2 * ay + ac

    c_rows = _all_gather_rows(_pad_rows(c, 8))[::8]
    c16 = _pad_rows(jnp.concatenate([c_rows, c_ctx[None, :]], axis=0), 16)
    b_mod_shard = lax.dynamic_slice_in_dim(b_mod, chip * SHARD_MOD, SHARD_MOD, axis=1)[:, None, :]
    silu_c, mod_shard = _mod_forward(c16, w_mod, b_mod_shard)
    mod_all = _all_gather_rows(mod_shard.reshape(DEPTH * 16, SHARD_MOD)).reshape(8, DEPTH, 16, SHARD_MOD)
    mod_full = jnp.transpose(mod_all[::2], (1, 2, 0, 3)).reshape(DEPTH, 16, 3 * D_MODEL)
    mod_lat = lax.dynamic_index_in_dim(mod_full, example, axis=1, keepdims=False).reshape(DEPTH, 3, D_MODEL)
    mod_ctx = mod_full[:, 8].reshape(DEPTH, 3, D_MODEL)
    modv = jnp.concatenate([mod_ctx, mod_lat, jnp.zeros((DEPTH, 2, D_MODEL), F32)], axis=1)

    wi_b, wo_b = w_in.astype(BF16), w_out.astype(BF16)

    def regroup(wi_all):
        wi_full = jnp.concatenate([wi_all[k] for k in range(N_CHIPS)], axis=-1)
        wc_l = jnp.concatenate([wi_full[:, 256:768], wi_full[:, 1024:1536]], axis=-1)
        wr_l = jnp.concatenate([wi_full[:, 0:256], wi_full[:, 768:1024], wi_full[:, 1536:2560], wi_full[:, 3072:3328]], axis=-1)
        return wc_l, wr_l, wi_full[:, 2560:3072]

    w_c, w_r, w_q, wo_full = [None] * DEPTH, [None] * DEPTH, [None] * DEPTH, [None] * DEPTH
    w_c[0], w_r[0], w_q[0] = regroup(_gather_weights((wi_b[0],))[0])

    cos_t, sin_t = _rope_tables(s_lat)
    conv_a_full = jnp.zeros((DEPTH, 8, GROUP_W), F32)
    conv_b_full = jnp.zeros((DEPTH, 32, GROUP_W), F32)
    conv_small = jnp.concatenate([conv_a.reshape(DEPTH * SHORT_CONV_K, -1), conv_b.reshape(DEPTH * CONFORMER_K, -1)], axis=0)
    n_cs = conv_small.shape[0]
    conv_rows = -(-n_cs // 8) * 8
    conv_all = _all_gather_rows(_pad_rows(conv_small, conv_rows)).reshape(8, conv_rows, -1)[::2]
    conv_all = jnp.transpose(conv_all, (1, 0, 2)).reshape(conv_rows, GROUP_W)
    conv_a_full = conv_a_full.at[:, :SHORT_CONV_K].set(conv_all[:DEPTH * SHORT_CONV_K].reshape(DEPTH, SHORT_CONV_K, GROUP_W))
    conv_b_full = conv_b_full.at[:, :CONFORMER_K].set(
        conv_all[DEPTH * SHORT_CONV_K:n_cs].reshape(DEPTH, CONFORMER_K, GROUP_W))

    vecs = jnp.stack([conv_b_bias, conf_ln_g, conf_ln_b, sgu_ln_g, sgu_ln_b] + [jnp.zeros_like(conv_b_bias)] * 3, axis=1)
    wss = w_s.reshape(DEPTH, N_SPATIAL_GROUPS * CHUNK, CHUNK).astype(BF16)
    wsts = jnp.swapaxes(w_s, 2, 3).reshape(DEPTH, N_SPATIAL_GROUPS * CHUNK, CHUNK).astype(BF16)
    bsm = jnp.repeat(jnp.swapaxes(b_s, 1, 2), HEAD_DIM, axis=2)
    qk_gain = jnp.concatenate([jnp.tile(q_gain, (1, 2))[:, None, :], jnp.tile(k_gain, (1, 2))[:, None, :],
                               jnp.zeros((DEPTH, 6, LANES), F32)], axis=1)

    xt = jnp.concatenate([ctx[0], x[0]], axis=0)
    saved = []
    for l in range(DEPTH):
        h, pc, pr, pq, q, k, v = _in_proj(xt, modv[l], g_pre[l][None, :], w_c[l], w_r[l], w_q[l], qk_gain[l], cos_t, sin_t)
        oe, gathered = _attention_fwd(q, k, v, (wo_b[l],) + ((wi_b[l + 1],) if l + 1 < DEPTH else ()))
        wo_full[l] = jnp.concatenate([gathered[0][k] for k in range(N_CHIPS)], axis=0)
        if l + 1 < DEPTH:
            w_c[l + 1], w_r[l + 1], w_q[l + 1] = regroup(gathered[1])
        mixed = _mix_out(pc, pr, oe, xt, modv[l], g_post[l][None, :], wo_full[l], conv_a_full[l], conv_b_full[l],
                         vecs[l], wss[l], bsm[l], loss_target[0] if l + 1 == DEPTH else None)
        x_new, y, ca, z2 = mixed[:4]
        saved.append(dict(x=xt, h=h, pc=pc, pr=pr, pq=pq, q=q, k=k, v=v, oe=oe, y=y, ca=ca, z2=z2))
        xt = x_new
    dxo = xt
    loss = lax.psum(mixed[4][0, 0], ("x", "y", "c"))

    g_small = {n: [None] * DEPTH for n in _SMALL}
    d_mod, landed_in, landed_out = [None] * DEPTH, [None] * DEPTH, [None] * DEPTH
    slab_in = None
    for l in reversed(range(DEPTH)):
        s = saved[l]
        dpr, g_a, g_b, doe, gw_o, pvec, s256, dws, dbs = _mix_out_bwd(
            dxo, s["y"], s["pr"], s["ca"], s["z2"], s["oe"], modv[l], g_post[l][None, :], wo_full[l], vecs[l], wss[l], wsts[l], bsm[l])
        dpc, dca, dcb = _conv_bwd(s["pc"], g_a, g_b, conv_a_full[l], conv_b_full[l])
        slab_out = gw_o.reshape(N_CHIPS, SHARD_OUT, D_MODEL).astype(BF16)
        dq, dk, dv, got = _attention_bwd(s["q"], s["k"], s["v"], doe, (slab_out,) + (() if slab_in is None else (slab_in,)))
        landed_out[l] = got[0]
        if slab_in is not None:
            landed_in[l + 1] = got[1]
        dxo, dpq, acc, dgain = _in_proj_bwd(dpc, dpr, dq, dk, dv, s["pq"], qk_gain[l], cos_t, sin_t, w_c[l], w_r[l], w_q[l],
                                            s["x"], dxo, modv[l], g_pre[l][None, :])
        gw_c, gw_r, gw_q = _in_proj_wgrad(s["h"], dpc, dpr, dpq)
        gw_in = jnp.concatenate([gw_r[:, 0:256], gw_c[:, 0:512], gw_r[:, 256:512], gw_c[:, 512:1024],
                                 gw_r[:, 512:1536], gw_q, gw_r[:, 1536:1792]], axis=-1)
        slab_in = jnp.transpose(gw_in.reshape(D_MODEL, N_CHIPS, SHARD_IN), (1, 0, 2)).astype(BF16)
        d_mod[l] = jnp.stack([jnp.concatenate([acc[2], acc[3], pvec[1]]), jnp.concatenate([acc[0], acc[1], pvec[0]])])
        g_small["g_pre"][l] = acc[4]
        g_small["g_post"][l] = pvec[2]
        g_small["conv_a"][l] = dca[:SHORT_CONV_K]
        g_small["conv_b"][l] = dcb[:CONFORMER_K]
        g_small["conv_b_bias"][l] = s256[0]
        g_small["conf_ln_g"][l] = s256[1]
        g_small["conf_ln_b"][l] = s256[2]
        g_small["sgu_ln_g"][l] = s256[3]
        g_small["sgu_ln_b"][l] = s256[4]
        g_small["w_s"][l] = dws.reshape(N_SPATIAL_GROUPS, CHUNK, CHUNK)
        g_small["b_s"][l] = jnp.transpose(dbs[:, :N_SPATIAL_GROUPS])
        g_small["q_gain"][l] = dgain[0, :HEAD_DIM] + dgain[0, HEAD_DIM:]
        g_small["k_gain"][l] = dgain[1, :HEAD_DIM] + dgain[1, HEAD_DIM:]
    grad_x = dxo[CTX_LEN:][None]

    d_mod_all = _all_gather_rows(jnp.stack(d_mod).reshape(DEPTH * 2, 3 * D_MODEL)).reshape(8, DEPTH, 2, 3 * D_MODEL)
    d_lat = jnp.transpose(d_mod_all[:, :, 0], (1, 0, 2))
    d_ctx = jnp.transpose(d_mod_all[:, :, 1], (1, 0, 2))
    cols = lambda a: lax.dynamic_slice_in_dim(a.reshape(DEPTH, 8, N_CHIPS, SHARD_MOD), chip, 1, axis=2)[:, :, 0]
    silu_t = jnp.transpose(silu_c)
    s_t = jnp.concatenate([silu_t[:, 0:8], jnp.tile(silu_t[:, 8:9], (1, 8)), jnp.zeros((D_MODEL, LANES - 16), F32)], axis=1)
    g_rows = jnp.concatenate([cols(d_lat), cols(d_ctx), jnp.zeros((DEPTH, LANES - 16, SHARD_MOD), F32)], axis=1)
    g_w_mod, g_b_mod, c_ctx_part = _mod_backward(s_t, g_rows, cols(d_ctx), jnp.concatenate([d_lat, d_ctx], axis=1),
                                                 w_mod, c_ctx[:, None])

    for n in _SMALL:
        if n not in ("c_ctx", "b_mod"):
            g_small[n] = jnp.stack(g_small[n])
    small_parts = [0.5 * c_ctx_part[:, 0]] + [g_small[n] for n in _SMALL[2:]]
    packed = _pack(small_parts)
    gathered = _all_gather_rows(packed.astype(BF16)).reshape(8, packed.shape[0], LANES)
    small_sum = _sum_slabs(gathered, packed.shape[0])
    small_g = dict(zip(("c_ctx",) + _SMALL[2:], _unpack(small_sum, [p.shape for p in small_parts])))
    small_g["b_mod"] = g_b_mod[:, 0]
    ch64 = GROUP_W // N_CHIPS
    for n in ("conv_a", "conv_b"):
        small_g[n] = lax.dynamic_slice_in_dim(small_g[n], chip * ch64, ch64, axis=2)
    sw = _pack([weights[n] for n in _SMALL])
    sm = _pack([m_in[n] for n in _SMALL])
    sv = _pack([v_in[n] for n in _SMALL])
    sg = _pack([small_g[n] for n in _SMALL])
    shapes = [weights[n].shape for n in _SMALL]
    small_out = [dict(zip(_SMALL, _unpack(o, shapes))) for o in _adamw([sg], sw, sm, sv, sg.shape[0])]

    landed_in[0] = _scatter_slabs((slab_in,))[0]
    sum_in = _sum_layer_slabs(landed_in, 512)
    sum_out = _sum_layer_slabs(landed_out, 256)
    sib_in, sib_out = _swap_with_sibling(sum_in, sum_out)

    big = {}
    flat = lambda a: a.reshape(-1, a.shape[-1])
    for n, grads, rows in (("w_in", [sum_in, sib_in], 512), ("w_out", [sum_out, sib_out], 256), ("w_mod", [flat(g_w_mod)], 512)):
        outs = _adamw(grads, flat(weights[n]), flat(m_in[n]), flat(v_in[n]), rows)
        big[n] = [o.reshape(weights[n].shape) for o in outs]

    def leaf(n, j):
        return big[n][j] if n in big else small_out[j][n]

    return (loss, grad_x, *[leaf(n, 0) for n in order], *[leaf(n, 1) for n in order],
            *[leaf(n, 2) for n in order], *[leaf(n, 3) for n in order])
```

```python
import functools

import numpy as np
import jax
import jax.numpy as jnp
from jax import lax
from jax.experimental import pallas as pl
from jax.experimental.pallas import tpu as pltpu

F32 = jnp.float32
BF16 = jnp.bfloat16
MESH = pl.DeviceIdType.MESH

D_MODEL = 1024
DEPTH = 4
GRID_W = 64
CTX_LEN = 256
GROUP_W = 256
HEAD_DIM = 64
N_Q_HEADS = 4
N_KV_HEADS = 2
GQA = N_Q_HEADS // N_KV_HEADS
ROPE_THETA = 10000.0
ATTN_SCALE = HEAD_DIM ** -0.5
SHORT_CONV_K = 3
CONFORMER_K = 31
CHUNK = 128
N_SPATIAL_GROUPS = 4
RMS_EPS = 1e-6
LN_EPS = 1e-5
ADAM_LR = 0.001
ADAM_B1 = 0.9
ADAM_B2 = 0.999
ADAM_EPS = 1e-08
ADAM_WD = 0.01
ADAM_STEP = 10

LANES = 128
HALO = 16
CONV_ROWS = 64
TM = 256
N_CTX_TILES = CTX_LEN // TM
SUB = 3
TB = SUB * TM
W_C = 1024
W_R = 1792
W_Q = 512
PROJ_W = W_C + W_R + W_Q
N_CHIPS = 4
SHARD_IN = PROJ_W // N_CHIPS
SHARD_OUT = D_MODEL // N_CHIPS
SHARD_MOD = 3 * D_MODEL // N_CHIPS
VMEM_LIMIT = 56 * 1024 * 1024


def _pc(body, **kw):
    return pl.pallas_call(body, **kw)


def _cparams(**kw):
    return pltpu.CompilerParams(dimension_semantics=("arbitrary",), vmem_limit_bytes=VMEM_LIMIT, **kw)


def _full(shape):
    n = len(shape)
    return pl.BlockSpec(shape, lambda i: (0,) * n)


def _const(shape):
    n = len(shape)
    return pl.BlockSpec(shape, lambda i: (0,) * n, pipeline_mode=pl.Buffered(1))


def _rows(width, tm=TM):
    return pl.BlockSpec((tm, width), lambda i: (i, 0))


def _heads(nh, width, tm=TM):
    return pl.BlockSpec((nh, tm, width), lambda i: (0, i, 0))


def _sigmoid(x):
    return jax.nn.sigmoid(x)


def _dot(a, b):
    return jnp.dot(a, b, preferred_element_type=F32)


def _dot_nt(a, b):
    return lax.dot_general(a, b, (((1,), (1,)), ((), ())), preferred_element_type=F32)


def _dot_tn(a, b):
    return lax.dot_general(a, b, (((0,), (0,)), ((), ())), preferred_element_type=F32)


def _lane(rows):
    return lax.broadcasted_iota(jnp.int32, (rows, LANES), 1)


def _rowsum(x):
    return jnp.sum(x, axis=1, keepdims=True)


def _colsum(x):
    return jnp.sum(x, axis=0, keepdims=True)


def _pair_sums(x, lo):
    s0 = _rowsum(jnp.where(lo, x, 0.0))
    s1 = _rowsum(jnp.where(lo, 0.0, x))
    return jnp.where(lo, s0, s1)


def _swap16(x, lo16):
    return jnp.where(lo16, pltpu.roll(x, LANES - 16, 1), pltpu.roll(x, 16, 1))


def _layer_norm_stats(x):
    mu = jnp.mean(x, axis=1, keepdims=True)
    xc = x - mu
    rs = lax.rsqrt(jnp.mean(xc * xc, axis=1, keepdims=True) + LN_EPS)
    return xc * rs, rs


def _layer_norm_bwd(dxn, xn, rs):
    return rs * (dxn - jnp.mean(dxn, axis=1, keepdims=True) - xn * jnp.mean(dxn * xn, axis=1, keepdims=True))


def _group_select(r, grp):
    out = jnp.where(grp == 0, r[0:CHUNK], 0.0)
    for g in range(1, N_SPATIAL_GROUPS):
        out = out + jnp.where(grp == g, r[g * CHUNK:(g + 1) * CHUNK], 0.0)
    return out


def _kv_chunk(t):
    return 1024 if (t - CTX_LEN) % 1024 == 0 else 256


def _all_gather_rows(x_shard):
    m_per, n = x_shard.shape

    def body(x_ref, out_ref, send_sems, recv_sems, local_sem):
        x, y, c = lax.axis_index("x"), lax.axis_index("y"), lax.axis_index("c")
        me, sibling = (x, y, c), (x, y, 1 - c)
        chips = [(1 - x, y), (x, 1 - y), (1 - x, 1 - y)]

        def rows(px, py, pc):
            return out_ref.at[pl.ds((4 * px + 2 * py + pc) * m_per, m_per), :]

        def copy(k, block, to, src=None):
            return pltpu.make_async_remote_copy(
                src_ref=rows(*block) if src is None else src, dst_ref=rows(*block),
                send_sem=send_sems.at[k], recv_sem=recv_sems.at[k], device_id=to, device_id_type=MESH)

        mine = pltpu.make_async_copy(x_ref, rows(*me), local_sem)
        mine.start()
        first = [copy(0, me, sibling, src=x_ref)]
        first += [copy(1 + j, me, (*chip, c), src=x_ref) for j, chip in enumerate(chips)]
        for cp in first:
            cp.start()
        passed = [copy(4 + j, (*chip, c), sibling) for j, chip in enumerate(chips)]
        for j, chip in enumerate(chips):
            copy(1 + j, (*chip, c), me).wait_recv()
            passed[j].start()
        copy(0, sibling, me).wait_recv()
        for j, chip in enumerate(chips):
            copy(4 + j, (*chip, 1 - c), me).wait_recv()
        for cp in first + passed:
            cp.wait_send()
        mine.wait()

    return _pc(
        body, name="all_gather_rows",
        out_shape=jax.ShapeDtypeStruct((8 * m_per, n), x_shard.dtype),
        in_specs=[pl.BlockSpec(memory_space=pltpu.VMEM)],
        out_specs=pl.BlockSpec(memory_space=pltpu.VMEM),
        scratch_shapes=[pltpu.SemaphoreType.DMA((7,)), pltpu.SemaphoreType.DMA((7,)), pltpu.SemaphoreType.DMA],
        compiler_params=pltpu.CompilerParams(vmem_limit_bytes=VMEM_LIMIT),
    )(x_shard)


def _place():
    x, y, c = lax.axis_index("x"), lax.axis_index("y"), lax.axis_index("c")
    return x, y, c, [(1 - x, y), (x, 1 - y), (1 - x, 1 - y)]


def _remote(src, dst, send_sems, recv_sems, k, to):
    return pltpu.make_async_remote_copy(src_ref=src, dst_ref=dst, send_sem=send_sems.at[k], recv_sem=recv_sems.at[k],
                                        device_id=to, device_id_type=MESH)


GATHER_SEMS = 6
SCATTER_SEMS = 3


def _gather_phase(phase, pairs, send_sems, recv_sems, local_sems):
    x, y, c, chips = _place()
    kme = 2 * x + y
    sibling = (x, y, 1 - c)
    for a, (src, dst) in enumerate(pairs):
        half = src.shape[0] // 2
        mine = pl.ds(c * half, half)
        theirs = pl.ds((1 - c) * half, half)
        if phase == 0:
            pltpu.make_async_copy(src, dst.at[kme], local_sems.at[a]).start()
        if phase == 2:
            pltpu.make_async_copy(src, dst.at[kme], local_sems.at[a]).wait()
        for j, (px, py) in enumerate(chips):
            kk = 2 * px + py
            landed = dst.at[kk, mine]
            out = lambda: _remote(src.at[mine], dst.at[kme, mine], send_sems, recv_sems, 6 * a + j, (px, py, c))
            hand = lambda: _remote(landed, landed, send_sems, recv_sems, 6 * a + 3 + j, sibling)
            if phase == 0:
                out().start()
            if phase == 1:
                _remote(landed, landed, send_sems, recv_sems, 6 * a + j, (px, py, c)).wait_recv()
                hand().start()
            if phase == 2:
                other = dst.at[kk, theirs]
                _remote(other, other, send_sems, recv_sems, 6 * a + 3 + j, sibling).wait_recv()
                out().wait_send()
                hand().wait_send()


def _scatter_phase(phase, pairs, send_sems, recv_sems, local_sems):
    x, y, c, chips = _place()
    kme = 2 * x + y
    for a, (src, dst) in enumerate(pairs):
        loc = pltpu.make_async_copy(src.at[kme], dst.at[kme], local_sems.at[a])
        if phase == 0:
            loc.start()
        else:
            loc.wait()
        for j, (px, py) in enumerate(chips):
            kk = 2 * px + py
            out = _remote(src.at[kk], dst.at[kme], send_sems, recv_sems, 3 * a + j, (px, py, c))
            if phase == 0:
                out.start()
            else:
                landed = dst.at[kk]
                _remote(landed, landed, send_sems, recv_sems, 3 * a + j, (px, py, c)).wait_recv()
                out.wait_send()


def _comm_scratch(per_array, n_arrays):
    n = per_array * n_arrays
    return [pltpu.SemaphoreType.DMA((n,)), pltpu.SemaphoreType.DMA((n,)), pltpu.SemaphoreType.DMA((n_arrays,))]


def _slots(a):
    return jax.ShapeDtypeStruct((N_CHIPS,) + a.shape, a.dtype)


def _gather_weights(shards):
    n = len(shards)

    def body(*refs):
        for phase in range(3):
            _gather_phase(phase, tuple(zip(refs[:n], refs[n:2 * n])), *refs[2 * n:])

    hbm = pl.BlockSpec(memory_space=pl.ANY)
    return _pc(
        body, name="gather_weights", out_shape=tuple(_slots(a) for a in shards),
        in_specs=[hbm] * n, out_specs=(hbm,) * n, scratch_shapes=_comm_scratch(GATHER_SEMS, n),
    )(*shards)


def _scatter_slabs(slabs):
    n = len(slabs)

    def body(*refs):
        for phase in range(2):
            _scatter_phase(phase, tuple(zip(refs[:n], refs[n:2 * n])), *refs[2 * n:])

    hbm = pl.BlockSpec(memory_space=pl.ANY)
    return _pc(
        body, name="scatter_slabs", out_shape=tuple(jax.ShapeDtypeStruct(a.shape, a.dtype) for a in slabs),
        in_specs=[hbm] * n, out_specs=(hbm,) * n, scratch_shapes=_comm_scratch(SCATTER_SEMS, n),
    )(*slabs)


def _swap_with_sibling(a, b):
    def body(a_ref, b_ref, ra_ref, rb_ref, send_sems, recv_sems):
        x, y, c = lax.axis_index("x"), lax.axis_index("y"), lax.axis_index("c")
        copies = []
        for k, (src, dst) in enumerate(((a_ref, ra_ref), (b_ref, rb_ref))):
            cp = pltpu.make_async_remote_copy(
                src_ref=src, dst_ref=dst, send_sem=send_sems.at[k], recv_sem=recv_sems.at[k],
                device_id=(x, y, 1 - c), device_id_type=MESH)
            cp.start()
            copies.append(cp)
        for cp in copies:
            cp.wait()

    hbm = pl.BlockSpec(memory_space=pl.ANY)
    return _pc(
        body, name="swap_with_sibling",
        out_shape=(jax.ShapeDtypeStruct(a.shape, a.dtype), jax.ShapeDtypeStruct(b.shape, b.dtype)),
        in_specs=[hbm, hbm], out_specs=(hbm, hbm),
        scratch_shapes=[pltpu.SemaphoreType.DMA((2,)), pltpu.SemaphoreType.DMA((2,))],
    )(a, b)


def _mod_forward(c16, w_mod, b_mod_shard):
    def body(c_ref, w_ref, b_ref, s_ref, o_ref):
        cc = c_ref[...]
        s = cc * _sigmoid(cc)
        s_ref[...] = s
        o_ref[0] = jnp.dot(s, w_ref[0], preferred_element_type=F32, precision=lax.Precision.HIGHEST) + b_ref[0]

    return _pc(
        body, name="mod_forward", grid=(DEPTH,),
        out_shape=(jax.ShapeDtypeStruct((16, D_MODEL), F32), jax.ShapeDtypeStruct((DEPTH, 16, SHARD_MOD), F32)),
        in_specs=[_full((16, D_MODEL)),
                  pl.BlockSpec((1, D_MODEL, SHARD_MOD), lambda l: (l, 0, 0)),
                  pl.BlockSpec((1, 1, SHARD_MOD), lambda l: (l, 0, 0))],
        out_specs=(_full((16, D_MODEL)), pl.BlockSpec((1, 16, SHARD_MOD), lambda l: (l, 0, 0))),
        compiler_params=_cparams(),
    )(c16, w_mod, b_mod_shard)


def _mod_backward(s_t, g_rows, g_ctx, d_all, w_mod, c_ctx_col):
    def body(st_ref, g_ref, gc_ref, d_ref, w_ref, cc_ref, gw_ref, gb_ref, pc_ref):
        l = pl.program_id(0)
        gw_ref[0] = jnp.dot(st_ref[...], g_ref[0], preferred_element_type=F32, precision=lax.Precision.HIGHEST)
        gb_ref[0] = _colsum(d_ref[0])
        part = _rowsum(w_ref[0] * _colsum(gc_ref[0]))

        @pl.when(l == 0)
        def _():
            pc_ref[...] = jnp.zeros_like(pc_ref)

        pc_ref[...] += part

        @pl.when(l == DEPTH - 1)
        def _():
            cc = cc_ref[...]
            sg = _sigmoid(cc)
            pc_ref[...] = pc_ref[...] * (sg * (1.0 + cc * (1.0 - sg)))

    return _pc(
        body, name="mod_backward", grid=(DEPTH,),
        out_shape=(jax.ShapeDtypeStruct((DEPTH, D_MODEL, SHARD_MOD), F32),
                   jax.ShapeDtypeStruct((DEPTH, 1, 3 * D_MODEL), F32),
                   jax.ShapeDtypeStruct((D_MODEL, 1), F32)),
        in_specs=[_full((D_MODEL, LANES)),
                  pl.BlockSpec((1, LANES, SHARD_MOD), lambda l: (l, 0, 0)),
                  pl.BlockSpec((1, 8, SHARD_MOD), lambda l: (l, 0, 0)),
                  pl.BlockSpec((1, 16, 3 * D_MODEL), lambda l: (l, 0, 0)),
                  pl.BlockSpec((1, D_MODEL, SHARD_MOD), lambda l: (l, 0, 0)),
                  _full((D_MODEL, 1))],
        out_specs=(pl.BlockSpec((1, D_MODEL, SHARD_MOD), lambda l: (l, 0, 0)),
                   pl.BlockSpec((1, 1, 3 * D_MODEL), lambda l: (l, 0, 0)),
                   _full((D_MODEL, 1))),
        compiler_params=_cparams(),
    )(s_t, g_rows, g_ctx, d_all, w_mod, c_ctx_col)


def _head_norm(xb, lo):
    r = lax.rsqrt(_pair_sums(xb * xb, lo) * (1.0 / HEAD_DIM) + RMS_EPS)
    return xb * r, r


def _in_proj(xt, modv, g_pre, w_c, w_r, w_q, qk_gain, cos_t, sin_t):
    t = xt.shape[0]

    def body(x_ref, mod_ref, g_ref, wc_ref, wr_ref, wq_ref, gain_ref, cos_ref, sin_ref,
             h_ref, pc_ref, pr_ref, pq_ref, q_ref, k_ref, v_ref):
        lane = _lane(TM)
        lo = lane < HEAD_DIM
        lo16 = (lane & 31) < 16
        one = jnp.where(lane == HEAD_DIM, 1.0, 0.0)
        for jj in range(SUB):
            rows = pl.ds(jj * TM, TM)
            is_ctx = pl.program_id(0) * SUB + jj < N_CTX_TILES
            x = x_ref[rows, :]
            r = lax.rsqrt(jnp.mean(x * x, axis=1, keepdims=True) + RMS_EPS)
            sh = jnp.where(is_ctx, mod_ref[0:1, :], mod_ref[3:4, :])
            sc = jnp.where(is_ctx, mod_ref[1:2, :], mod_ref[4:5, :])
            h = (x * r * g_ref[...]) * (1.0 + sc) + sh
            hb = h.astype(BF16)
            h_ref[rows, :] = hb
            pc_ref[rows, :] = _dot(hb, wc_ref[...])
            pr_ref[rows, :] = _dot(hb, wr_ref[...])
            pq = _dot(hb, wq_ref[...])
            pq_ref[rows, :] = pq
            cos = cos_ref[rows, :]
            sin = sin_ref[rows, :]
            for b in range(3):
                xh, _ = _head_norm(pq[:, b * LANES:(b + 1) * LANES], lo)
                xg = xh * (gain_ref[0:1, :] if b < 2 else gain_ref[1:2, :])
                rot = xg * cos + _swap16(xg, lo16) * sin
                if b < 2:
                    rot = rot * ATTN_SCALE
                dst = q_ref if b < 2 else k_ref
                base = 2 * b if b < 2 else 0
                dst[base, rows, :] = jnp.where(lo, rot, 0.0).astype(BF16)
                dst[base + 1, rows, :] = jnp.where(lo, pltpu.roll(rot, HEAD_DIM, 1), 0.0).astype(BF16)
            vb = pq[:, 3 * LANES:4 * LANES]
            v_ref[0, rows, :] = jnp.where(lo, vb, one).astype(BF16)
            v_ref[1, rows, :] = jnp.where(lo, pltpu.roll(vb, HEAD_DIM, 1), one).astype(BF16)

    return _pc(
        body, name="in_proj", grid=(t // TB,),
        out_shape=(jax.ShapeDtypeStruct((t, D_MODEL), BF16),
                   jax.ShapeDtypeStruct((t, W_C), F32), jax.ShapeDtypeStruct((t, W_R), F32), jax.ShapeDtypeStruct((t, W_Q), F32),
                   jax.ShapeDtypeStruct((N_Q_HEADS, t, LANES), BF16),
                   jax.ShapeDtypeStruct((N_KV_HEADS, t, LANES), BF16),
                   jax.ShapeDtypeStruct((N_KV_HEADS, t, LANES), BF16)),
        in_specs=[_rows(D_MODEL, TB), _const((8, D_MODEL)), _const((1, D_MODEL)),
                  _const((D_MODEL, W_C)), _const((D_MODEL, W_R)), _const((D_MODEL, W_Q)),
                  _const((8, LANES)), _rows(LANES, TB), _rows(LANES, TB)],
        out_specs=(_rows(D_MODEL, TB), _rows(W_C, TB), _rows(W_R, TB), _rows(W_Q, TB),
                   _heads(N_Q_HEADS, LANES, TB), _heads(N_KV_HEADS, LANES, TB), _heads(N_KV_HEADS, LANES, TB)),
        compiler_params=_cparams(),
    )(xt, modv, g_pre, w_c, w_r, w_q, qk_gain, cos_t, sin_t)


def _attention_fwd(q, k, v, shards=None):
    t = q.shape[1]
    tk = _kv_chunk(t)
    n_chunks = (t - CTX_LEN) // tk
    n_tiles = t // TM
    n_sh = 0 if shards is None else len(shards)

    def body(q_ref, k_ref, v_ref, *rest):
        i = pl.program_id(0)
        o_ref = rest[n_sh]
        if shards is not None:
            pairs = tuple(zip(rest[:n_sh], rest[n_sh + 1:2 * n_sh + 1]))
            for phase, at in enumerate((0, n_tiles // 2, n_tiles - 1)):
                @pl.when(i == at)
                def _(phase=phase):
                    _gather_phase(phase, pairs, *rest[2 * n_sh + 1:])
        lane = _lane(GQA * TM)
        qs = [jnp.concatenate([q_ref[GQA * g + hh] for hh in range(GQA)], axis=0) for g in range(N_KV_HEADS)]

        def step(st, size, carry):
            out = []
            for g in range(N_KV_HEADS):
                m, acc = carry[g]
                s = _dot_nt(qs[g], k_ref[g, pl.ds(st, size), :])
                m_new = jnp.maximum(m, jnp.max(s, axis=1, keepdims=True))
                p = jnp.exp(s - m_new)
                out.append((m_new, acc * jnp.exp(m - m_new) + _dot(p.astype(BF16), v_ref[g, pl.ds(st, size), :])))
            return tuple(out)

        init = tuple((jnp.full((GQA * TM, 1), -jnp.inf, F32), jnp.zeros((GQA * TM, LANES), F32)) for _ in range(N_KV_HEADS))

        def finish(carry):
            for g in range(N_KV_HEADS):
                m, acc = carry[g]
                den = _rowsum(jnp.where(lane == HEAD_DIM, acc, 0.0))
                out = jnp.where(lane < HEAD_DIM, acc * (1.0 / den), jnp.where(lane == HEAD_DIM, m + jnp.log(den), 0.0))
                for hh in range(GQA):
                    o_ref[GQA * g + hh] = out[hh * TM:(hh + 1) * TM]

        @pl.when(i < N_CTX_TILES)
        def _():
            finish(step(0, CTX_LEN, init))

        @pl.when(i >= N_CTX_TILES)
        def _():
            per = 4 if n_chunks % 4 == 0 else 1

            def trip(j, cr):
                st = pl.multiple_of(CTX_LEN + j * (per * tk), 256)
                for u in range(per):
                    cr = step(st + u * tk, tk, cr)
                return cr

            finish(lax.fori_loop(0, n_chunks // per, trip, step(0, CTX_LEN, init)))

    hbm = pl.BlockSpec(memory_space=pl.ANY)
    extra = () if shards is None else tuple(shards)
    outs = _pc(
        body, name="attention_fwd" if shards is None else "attention_fwd_gather", grid=(n_tiles,),
        out_shape=(jax.ShapeDtypeStruct((N_Q_HEADS, t, LANES), F32),) + tuple(_slots(a) for a in extra),
        in_specs=[_heads(N_Q_HEADS, LANES), _full((N_KV_HEADS, t, LANES)), _full((N_KV_HEADS, t, LANES))] + [hbm] * n_sh,
        out_specs=(_heads(N_Q_HEADS, LANES),) + (hbm,) * n_sh,
        scratch_shapes=_comm_scratch(GATHER_SEMS, n_sh) if shards is not None else [],
        compiler_params=_cparams(),
    )(q, k, v, *extra)
    return outs[0], tuple(outs[1:])


def _halo_specs(width, t, rows=TM):
    last = t // HALO - 1
    per = rows // HALO
    prev = pl.BlockSpec((HALO, width), lambda i: (jnp.maximum(i * per - 1, 0), 0))
    nxt = pl.BlockSpec((HALO, width), lambda i: (jnp.minimum((i + 1) * per, last), 0))
    return prev, nxt


def _halo_valid(i, n_tiles):
    prev_ok = jnp.logical_and(i != 0, i != N_CTX_TILES)
    next_ok = jnp.logical_and(i != N_CTX_TILES - 1, i != n_tiles - 1)
    return jnp.where(prev_ok, 1.0, 0.0), jnp.where(next_ok, 1.0, 0.0)


def _conv_inputs(pc):
    u = pc[:, 0:GROUP_W] * pc[:, GROUP_W:2 * GROUP_W]
    z = pc[:, 2 * GROUP_W:3 * GROUP_W] * _sigmoid(pc[:, 3 * GROUP_W:4 * GROUP_W])
    return u, z


def _fill_ext(ext_ref, prev, mid, nxt):
    ext_ref[0:HALO, :] = prev
    ext_ref[HALO:HALO + TM, :] = mid
    ext_ref[HALO + TM:HALO + TM + HALO, :] = nxt


def _row_local_mixers(pr, ca, z2, oe, vecs, wss_ref, bsm, lane256):
    a_b, a_g, b_g = pr[:, 0:256], pr[:, 256:512], pr[:, 512:768]
    c_u, c_v, c_g, d_g = pr[:, 768:1024], pr[:, 1024:1280], pr[:, 1280:1536], pr[:, 1536:1792]
    zn, rs_b = _layer_norm_stats(z2)
    tb = zn * vecs[1:2, :] + vecs[2:3, :]
    vn_hat, rs_c = _layer_norm_stats(c_v)
    vn = vn_hat * vecs[3:4, :] + vecs[4:5, :]
    grp = jnp.right_shift(lane256, 6)
    sgs = []
    for ch in range(TM // CHUNK):
        r = _dot(wss_ref[...], vn[ch * CHUNK:(ch + 1) * CHUNK, :].astype(BF16))
        sgs.append(_group_select(r, grp[0:CHUNK]) + bsm)
    sg = jnp.concatenate(sgs, axis=0)
    lane = _lane(TM)
    lo = lane < HEAD_DIM
    att = jnp.concatenate([jnp.where(lo, oe[2 * b], pltpu.roll(oe[2 * b + 1], HEAD_DIM, 1)) for b in range(2)], axis=1)
    return dict(a_b=a_b, a_g=a_g, b_g=b_g, c_u=c_u, c_v=c_v, c_g=c_g, d_g=d_g, zn=zn, rs_b=rs_b, tb=tb,
                vn_hat=vn_hat, rs_c=rs_c, vn=vn, sg=sg, att=att, grp=grp, lo=lo, lane=lane)


def _mixer_concat(f, ca):
    ya = f["a_b"] * ca
    yb = f["tb"] * _sigmoid(f["tb"])
    yc = f["c_u"] * f["sg"]
    gates = [f[n] * _sigmoid(f[n]) for n in ("a_g", "b_g", "c_g", "d_g")]
    ys = (ya, yb, yc, f["att"])
    big = jnp.concatenate([yy * gg for yy, gg in zip(ys, gates)], axis=1).astype(BF16)
    return big, ys, gates


def _taps31(ext_ref, w_ref, flip):
    blocks = []
    for r0 in range(0, TM, CONV_ROWS):
        out = None
        for b in range(8):
            part = None
            for a in range(4):
                o = 8 * a + b
                if 1 <= o <= CONFORMER_K:
                    kk = CONFORMER_K - o if flip else o - 1
                    term = w_ref[kk:kk + 1, :] * ext_ref[pl.ds(r0 + 8 * a, CONV_ROWS + 8), :]
                    part = term if part is None else part + term
            part = part[b:b + CONV_ROWS]
            out = part if out is None else out + part
        blocks.append(out)
    return jnp.concatenate(blocks, axis=0)


def _mix_out(pc, pr, oe, xt, modv, g_post, w_out, conv_a, conv_b, vecs, wss, bsm, target=None):
    t = xt.shape[0]
    n_tiles = t // TM
    prev_spec, next_spec = _halo_specs(W_C, t, TB)
    n_t = 0 if target is None else SUB

    def body(pc_ref, pp_ref, pn_ref, pr_ref, oe_ref, x_ref, mod_ref, gp_ref, wo_ref, cva_ref, cvb_ref, vec_ref, wss_ref, bsm_ref,
             *rest):
        xo_ref, y_ref, ca_ref, z2_ref = rest[n_t:n_t + 4]
        uext, zext = rest[-2:]
        i = pl.program_id(0)
        vecs = vec_ref[...]
        lane256 = lax.broadcasted_iota(jnp.int32, (TM, GROUP_W), 1)
        if target is not None:
            loss_ref = rest[n_t + 4]

            @pl.when(i == 0)
            def _():
                loss_ref[...] = jnp.zeros_like(loss_ref)

        for jj in range(SUB):
            rows = pl.ds(jj * TM, TM)
            tile = i * SUB + jj
            is_ctx = tile < N_CTX_TILES
            pv, nv = _halo_valid(tile, n_tiles)
            u, z = _conv_inputs(pc_ref[rows, :])
            up, zp = _conv_inputs(pp_ref[...] if jj == 0 else pc_ref[pl.ds(jj * TM - HALO, HALO), :])
            un, zn_ = _conv_inputs(pn_ref[...] if jj == SUB - 1 else pc_ref[pl.ds((jj + 1) * TM, HALO), :])
            ue, ze = uext.at[jj], zext.at[jj]
            _fill_ext(ue, up * pv, u, un * nv)
            _fill_ext(ze, zp * pv, z, zn_ * nv)
            ca = cva_ref[0:1, :] * ue[pl.ds(HALO - 1, TM), :]
            for kk in range(1, SHORT_CONV_K):
                ca = ca + cva_ref[kk:kk + 1, :] * ue[pl.ds(HALO - 1 + kk, TM), :]
            z2 = _taps31(ze, cvb_ref, False) + vecs[0:1, :]
            ca_ref[rows, :] = ca
            z2_ref[rows, :] = z2
            oes = [oe_ref[h, rows, :] for h in range(N_Q_HEADS)]
            f = _row_local_mixers(pr_ref[rows, :], ca, z2, oes, vecs, wss_ref, bsm_ref[...], lane256)
            big, _, _ = _mixer_concat(f, ca)
            y = _dot(big, wo_ref[...])
            y_ref[rows, :] = y
            ry = lax.rsqrt(jnp.mean(y * y, axis=1, keepdims=True) + RMS_EPS)
            gt = jnp.where(is_ctx, mod_ref[2:3, :], mod_ref[5:6, :])
            x_new = x_ref[rows, :] + gt * (y * ry * gp_ref[...])
            if target is None:
                xo_ref[rows, :] = x_new
            else:
                err = (x_new - rest[jj][...]) * jnp.where(is_ctx, 0.0, 1.0)
                xo_ref[rows, :] = err * (1.0 / D_MODEL)
                loss_ref[...] += jnp.sum(err * err) * (0.5 / D_MODEL)

    rows_f32 = jax.ShapeDtypeStruct((t, D_MODEL), F32)
    group_f32 = jax.ShapeDtypeStruct((t, GROUP_W), F32)
    loss_shape, loss_spec, t_spec, t_arg = (), (), [], ()
    if target is not None:
        loss_shape, loss_spec = (jax.ShapeDtypeStruct((8, LANES), F32),), (_full((8, LANES)),)
        t_spec = [pl.BlockSpec((TM, D_MODEL), lambda i, jj=jj: (jnp.maximum(i * SUB + jj - N_CTX_TILES, 0), 0))
                  for jj in range(SUB)]
        t_arg = (target,) * SUB
    return _pc(
        body, name="mix_out" if target is None else "mix_out_loss", grid=(t // TB,),
        out_shape=(rows_f32, rows_f32, group_f32, group_f32) + loss_shape,
        in_specs=[_rows(W_C, TB), prev_spec, next_spec, _rows(W_R, TB), _heads(N_Q_HEADS, LANES, TB), _rows(D_MODEL, TB),
                  _const((8, D_MODEL)), _const((1, D_MODEL)), _const((D_MODEL, D_MODEL)),
                  _const((8, GROUP_W)), _const((32, GROUP_W)), _const((8, GROUP_W)),
                  _const((N_SPATIAL_GROUPS * CHUNK, CHUNK)), _const((CHUNK, GROUP_W))] + t_spec,
        out_specs=(_rows(D_MODEL, TB), _rows(D_MODEL, TB), _rows(GROUP_W, TB), _rows(GROUP_W, TB)) + loss_spec,
        scratch_shapes=[pltpu.VMEM((SUB, TM + 2 * HALO, GROUP_W), F32), pltpu.VMEM((SUB, TM + 2 * HALO, GROUP_W), F32)],
        compiler_params=_cparams(),
    )(pc, pc, pc, pr, oe, xt, modv, g_post, w_out, conv_a, conv_b, vecs, wss, bsm, *t_arg)


def _mix_out_bwd(dxo, y, pr, ca, z2, oe, modv, g_post, w_out, vecs, wss, wsts, bsm):
    t = y.shape[0]
    n_tiles = t // TM

    def body(dxo_ref, y_ref, pr_ref, ca_ref, z2_ref, oe_ref, mod_ref, gp_ref, wo_ref, vec_ref, wss_ref, wsts_ref, bsm_ref,
             dpr_ref, ga_ref, gb_ref, doe_ref, dwo_ref, pvec_ref, s256_ref, dws_ref, dbs_ref, dbsm):
        i = pl.program_id(0)
        is_ctx = i < N_CTX_TILES

        @pl.when(i == 0)
        def _():
            dwo_ref[...] = jnp.zeros_like(dwo_ref)
            pvec_ref[...] = jnp.zeros_like(pvec_ref)
            s256_ref[...] = jnp.zeros_like(s256_ref)
            dws_ref[...] = jnp.zeros_like(dws_ref)
            dbsm[...] = jnp.zeros_like(dbsm)

        dxo_ = dxo_ref[...]
        y_ = y_ref[...]
        ry = lax.rsqrt(jnp.mean(y_ * y_, axis=1, keepdims=True) + RMS_EPS)
        nh = y_ * ry
        gp = gp_ref[...]
        gt = jnp.where(is_ctx, mod_ref[2:3, :], mod_ref[5:6, :])
        dgt = _colsum(dxo_ * (nh * gp))
        pvec_ref[0:1, :] += jnp.where(is_ctx, dgt, 0.0)
        pvec_ref[1:2, :] += jnp.where(is_ctx, 0.0, dgt)
        dn = dxo_ * gt
        pvec_ref[2:3, :] += _colsum(dn * nh)
        dnh = dn * gp
        dy = ry * (dnh - nh * jnp.mean(dnh * nh, axis=1, keepdims=True))

        vecs = vec_ref[...]
        bsm_ = bsm_ref[...]
        ca_ = ca_ref[...]
        lane256 = lax.broadcasted_iota(jnp.int32, (TM, GROUP_W), 1)
        f = _row_local_mixers(pr_ref[...], ca_, z2_ref[...], oe_ref, vecs, wss_ref, bsm_, lane256)
        big, ys, gates = _mixer_concat(f, ca_)
        dyb = dy.astype(BF16)
        dwo_ref[...] += _dot_tn(big, dyb)
        dbig = _dot_nt(dyb, wo_ref[...])

        d_y, d_gate = [], []
        for n, (name, yy, gg) in enumerate(zip(("a_g", "b_g", "c_g", "d_g"), ys, gates)):
            dpart = dbig[:, n * GROUP_W:(n + 1) * GROUP_W]
            gx = f[name]
            sg_ = _sigmoid(gx)
            d_y.append(dpart * gg)
            d_gate.append(dpart * yy * (sg_ * (1.0 + gx * (1.0 - sg_))))
        dya, dyb_, dyc, datt = d_y

        d_ab = dya * ca_
        ga_ref[...] = dya * f["a_b"]
        tb = f["tb"]
        sb = _sigmoid(tb)
        dtb = dyb_ * (sb * (1.0 + tb * (1.0 - sb)))
        s256_ref[1:2, :] += _colsum(dtb * f["zn"])
        s256_ref[2:3, :] += _colsum(dtb)
        dz2 = _layer_norm_bwd(dtb * vecs[1:2, :], f["zn"], f["rs_b"])
        gb_ref[...] = dz2
        s256_ref[0:1, :] += _colsum(dz2)
        d_cu = dyc * f["sg"]
        dsg = dyc * f["c_u"]
        grp = f["grp"]
        dvn_parts = []
        for ch in range(TM // CHUNK):
            rows = slice(ch * CHUNK, (ch + 1) * CHUNK)
            dsg_c = dsg[rows, :]
            dbsm[...] += dsg_c
            vn_c = f["vn"][rows, :].astype(BF16)
            for g in range(N_SPATIAL_GROUPS):
                masked = jnp.where(grp[0:CHUNK] == g, dsg_c, 0.0).astype(BF16)
                dws_ref[g * CHUNK:(g + 1) * CHUNK, :] += _dot_nt(masked, vn_c)
            dvn_parts.append(_group_select(_dot(wsts_ref[...], dsg_c.astype(BF16)), grp[0:CHUNK]))
        dvn = jnp.concatenate(dvn_parts, axis=0)
        s256_ref[3:4, :] += _colsum(dvn * f["vn_hat"])
        s256_ref[4:5, :] += _colsum(dvn)
        d_cv = _layer_norm_bwd(dvn * vecs[3:4, :], f["vn_hat"], f["rs_c"])
        lane, lo = f["lane"], f["lo"]
        att = f["att"]
        for b in range(2):
            da = datt[:, b * LANES:(b + 1) * LANES]
            prod = da * att[:, b * LANES:(b + 1) * LANES]
            for hh in range(2):
                h = 2 * b + hh
                lse = _rowsum(jnp.where(lane == HEAD_DIM, oe_ref[h], 0.0))
                delta = _rowsum(jnp.where(lo, prod, 0.0) if hh == 0 else jnp.where(lo, 0.0, prod))
                dah = da if hh == 0 else pltpu.roll(da, HEAD_DIM, 1)
                doe_ref[h] = jnp.where(lo, dah, jnp.where(lane == HEAD_DIM, delta, jnp.where(lane == HEAD_DIM + 1, lse, 0.0)))

        dpr_ref[...] = jnp.concatenate([d_ab, d_gate[0], d_gate[1], d_cu, d_cv, d_gate[2], d_gate[3]], axis=1).astype(BF16)

        @pl.when(i == n_tiles - 1)
        def _():
            acc = dbsm[...]
            lane128 = _lane(CHUNK)
            out = jnp.zeros((CHUNK, LANES), F32)
            for g in range(N_SPATIAL_GROUPS):
                col = _rowsum(jnp.where(grp[0:CHUNK] == g, acc, 0.0))
                out = out + jnp.where(lane128 == g, col, 0.0)
            dbs_ref[...] = out

    return _pc(
        body, name="mix_out_bwd", grid=(n_tiles,),
        out_shape=(jax.ShapeDtypeStruct((t, W_R), BF16),
                   jax.ShapeDtypeStruct((t, GROUP_W), F32), jax.ShapeDtypeStruct((t, GROUP_W), F32),
                   jax.ShapeDtypeStruct((N_Q_HEADS, t, LANES), F32),
                   jax.ShapeDtypeStruct((D_MODEL, D_MODEL), F32),
                   jax.ShapeDtypeStruct((8, D_MODEL), F32),
                   jax.ShapeDtypeStruct((8, GROUP_W), F32),
                   jax.ShapeDtypeStruct((N_SPATIAL_GROUPS * CHUNK, CHUNK), F32),
                   jax.ShapeDtypeStruct((CHUNK, LANES), F32)),
        in_specs=[_rows(D_MODEL), _rows(D_MODEL), _rows(W_R), _rows(GROUP_W), _rows(GROUP_W), _heads(N_Q_HEADS, LANES),
                  _full((8, D_MODEL)), _full((1, D_MODEL)), _full((D_MODEL, D_MODEL)), _full((8, GROUP_W)),
                  _full((N_SPATIAL_GROUPS * CHUNK, CHUNK)), _full((N_SPATIAL_GROUPS * CHUNK, CHUNK)), _full((CHUNK, GROUP_W))],
        out_specs=(_rows(W_R), _rows(GROUP_W), _rows(GROUP_W), _heads(N_Q_HEADS, LANES),
                   _full((D_MODEL, D_MODEL)), _full((8, D_MODEL)), _full((8, GROUP_W)),
                   _full((N_SPATIAL_GROUPS * CHUNK, CHUNK)), _full((CHUNK, LANES))),
        scratch_shapes=[pltpu.VMEM((CHUNK, GROUP_W), F32)],
        compiler_params=_cparams(),
    )(dxo, y, pr, ca, z2, oe, modv, g_post, w_out, vecs, wss, wsts, bsm)


def _conv_bwd(pc, g_a, g_b, conv_a, conv_b):
    t = pc.shape[0]
    n_tiles = t // TM
    pc_prev, pc_next = _halo_specs(W_C, t, TB)
    g_prev, g_next = _halo_specs(GROUP_W, t, TB)

    def body(pc_ref, pp_ref, pn_ref, ga_ref, gap_ref, gan_ref, gb_ref, gbp_ref, gbn_ref, cva_ref, cvb_ref,
             dpc_ref, dca_ref, dcb_ref, uext, zext, gaext, gbext):
        i = pl.program_id(0)

        @pl.when(i == 0)
        def _():
            dca_ref[...] = jnp.zeros_like(dca_ref)
            dcb_ref[...] = jnp.zeros_like(dcb_ref)

        def halo(jj, tile_ref, prev_ref, next_ref):
            before = prev_ref[...] if jj == 0 else tile_ref[pl.ds(jj * TM - HALO, HALO), :]
            after = next_ref[...] if jj == SUB - 1 else tile_ref[pl.ds((jj + 1) * TM, HALO), :]
            return before, after

        for jj in range(SUB):
            rows = pl.ds(jj * TM, TM)
            pv, nv = _halo_valid(i * SUB + jj, n_tiles)
            pc_ = pc_ref[rows, :]
            u, z = _conv_inputs(pc_)
            pc_before, pc_after = halo(jj, pc_ref, pp_ref, pn_ref)
            up, zp = _conv_inputs(pc_before)
            un, zn_ = _conv_inputs(pc_after)
            ue, ze, gae, gbe = uext.at[jj], zext.at[jj], gaext.at[jj], gbext.at[jj]
            _fill_ext(ue, up * pv, u, un * nv)
            _fill_ext(ze, zp * pv, z, zn_ * nv)
            ga = ga_ref[rows, :]
            gb = gb_ref[rows, :]
            ga_before, ga_after = halo(jj, ga_ref, gap_ref, gan_ref)
            gb_before, gb_after = halo(jj, gb_ref, gbp_ref, gbn_ref)
            _fill_ext(gae, ga_before * pv, ga, ga_after * nv)
            _fill_ext(gbe, gb_before * pv, gb, gb_after * nv)

            du = cva_ref[0:1, :] * gae[pl.ds(HALO + 1, TM), :]
            dca_ref[0:1, :] += _colsum(ga * ue[pl.ds(HALO - 1, TM), :])
            for kk in range(1, SHORT_CONV_K):
                du = du + cva_ref[kk:kk + 1, :] * gae[pl.ds(HALO + 1 - kk, TM), :]
                dca_ref[kk:kk + 1, :] += _colsum(ga * ue[pl.ds(HALO - 1 + kk, TM), :])
            dz = _taps31(gbe, cvb_ref, True)
            for r0 in range(0, TM, CONV_ROWS):
                gb_rows = gb_ref[pl.ds(jj * TM + r0, CONV_ROWS), :]
                for b in range(8):
                    zb = ze[pl.ds(r0 + b, CONV_ROWS + 24), :]
                    for a in range(4):
                        kk = 8 * a + b - 1
                        if 0 <= kk < CONFORMER_K:
                            dcb_ref[kk:kk + 1, :] += _colsum(gb_rows * zb[8 * a:8 * a + CONV_ROWS])

            a_c, a_h = pc_[:, 0:GROUP_W], pc_[:, GROUP_W:2 * GROUP_W]
            glu_a, glu_g = pc_[:, 2 * GROUP_W:3 * GROUP_W], pc_[:, 3 * GROUP_W:4 * GROUP_W]
            sg = _sigmoid(glu_g)
            dpc_ref[rows, :] = jnp.concatenate([du * a_h, du * a_c, dz * sg, dz * glu_a * sg * (1.0 - sg)], axis=1).astype(BF16)

    ext = pltpu.VMEM((SUB, TM + 2 * HALO, GROUP_W), F32)
    return _pc(
        body, name="conv_bwd", grid=(t // TB,),
        out_shape=(jax.ShapeDtypeStruct((t, W_C), BF16), jax.ShapeDtypeStruct((8, GROUP_W), F32), jax.ShapeDtypeStruct((32, GROUP_W), F32)),
        in_specs=[_rows(W_C, TB), pc_prev, pc_next, _rows(GROUP_W, TB), g_prev, g_next, _rows(GROUP_W, TB), g_prev, g_next,
                  _const((8, GROUP_W)), _const((32, GROUP_W))],
        out_specs=(_rows(W_C, TB), _full((8, GROUP_W)), _full((32, GROUP_W))),
        scratch_shapes=[ext, ext, ext, ext],
        compiler_params=_cparams(),
    )(pc, pc, pc, g_a, g_a, g_a, g_b, g_b, g_b, conv_a, conv_b)


def _attention_bwd(q, k, v, doe, slabs, conv):
    t = q.shape[1]
    tk = _kv_chunk(t)
    n_chunks = (t - CTX_LEN) // tk
    n_tiles = t // TM
    n_sl = len(slabs)
    rpt = TM // n_chunks
    pc, g_a, g_b, conv_a, conv_b = conv

    def body(q_ref, do_ref, k_ref, v_ref, pc_ref, pp_ref, pn_ref, ga_ref, gap_ref, gan_ref, gb_ref, gbp_ref, gbn_ref,
             cva_ref, cvb_ref, *rest):
        i = pl.program_id(0)
        dq_ref, dk_hbm, dv_hbm, dpc_ref, dca_ref, dcb_ref = rest[n_sl:n_sl + 6]
        dk_acc, dv_acc, uext, zext, gaext, gbext, dz_s = rest[2 * n_sl + 6:2 * n_sl + 13]
        pairs = tuple(zip(rest[:n_sl], rest[n_sl + 6:2 * n_sl + 6]))
        sems = rest[2 * n_sl + 13:]

        @pl.when(i == 0)
        def _():
            dk_acc[...] = jnp.zeros_like(dk_acc)
            dv_acc[...] = jnp.zeros_like(dv_acc)
            dca_ref[...] = jnp.zeros_like(dca_ref)
            dcb_ref[...] = jnp.zeros_like(dcb_ref)
            _scatter_phase(0, pairs, *sems)

        pv, nv = _halo_valid(i, n_tiles)
        pc_ = pc_ref[...]
        u, z = _conv_inputs(pc_)
        up, zp = _conv_inputs(pp_ref[...])
        un, zn_ = _conv_inputs(pn_ref[...])
        _fill_ext(uext, up * pv, u, un * nv)
        _fill_ext(zext, zp * pv, z, zn_ * nv)
        _fill_ext(gaext, gap_ref[...] * pv, ga_ref[...], gan_ref[...] * nv)
        _fill_ext(gbext, gbp_ref[...] * pv, gb_ref[...], gbn_ref[...] * nv)

        def conv_rows(r0):
            out = None
            for b in range(8):
                part = None
                for a in range(4):
                    o = 8 * a + b
                    if 1 <= o <= CONFORMER_K:
                        term = cvb_ref[CONFORMER_K - o:CONFORMER_K - o + 1, :] * gbext[pl.ds(r0 + 8 * a, rpt + 8), :]
                        part = term if part is None else part + term
                part = part[b:b + rpt]
                out = part if out is None else out + part
            dz_s[pl.ds(r0, rpt), :] = out
            gb_rows = gb_ref[pl.ds(r0, rpt), :]
            window = zext[pl.ds(r0, rpt + 2 * HALO), :]
            for b in range(8):
                zb = window[b:b + rpt + 24]
                for a in range(4):
                    kk = 8 * a + b - 1
                    if 0 <= kk < CONFORMER_K:
                        dcb_ref[kk:kk + 1, :] += _colsum(gb_rows * zb[8 * a:8 * a + rpt])

        lane = _lane(GQA * TM)
        lo = lane < HEAD_DIM
        qs, dos, deltas, lses = [], [], [], []
        for g in range(N_KV_HEADS):
            qs.append(jnp.concatenate([q_ref[GQA * g + hh] for hh in range(GQA)], axis=0))
            dog = jnp.concatenate([do_ref[GQA * g + hh] for hh in range(GQA)], axis=0)
            deltas.append(_rowsum(jnp.where(lane == HEAD_DIM, dog, 0.0)))
            lses.append(_rowsum(jnp.where(lane == HEAD_DIM + 1, dog, 0.0)))
            dos.append(jnp.where(lo, dog, 0.0).astype(BF16))

        def step(st, size, dqs):
            out = []
            for g in range(N_KV_HEADS):
                kc = k_ref[g, pl.ds(st, size), :]
                vc = v_ref[g, pl.ds(st, size), :]
                p = jnp.exp(_dot_nt(qs[g], kc) - lses[g])
                ds_ = (p * (_dot_nt(dos[g], vc) - deltas[g])).astype(BF16)
                dk_acc[g, pl.ds(st, size), :] += _dot_tn(ds_, qs[g])
                dv_acc[g, pl.ds(st, size), :] += _dot_tn(p.astype(BF16), dos[g])
                out.append(dqs[g] + _dot(ds_, kc))
            return tuple(out)

        zero = tuple(jnp.zeros((GQA * TM, LANES), F32) for _ in range(N_KV_HEADS))

        def finish(dqs):
            for g in range(N_KV_HEADS):
                for hh in range(GQA):
                    dq_ref[GQA * g + hh] = dqs[g][hh * TM:(hh + 1) * TM]

        @pl.when(i < N_CTX_TILES)
        def _():
            finish(step(0, CTX_LEN, zero))
            for r in range(n_chunks):
                conv_rows(r * rpt)

        @pl.when(i >= N_CTX_TILES)
        def _():
            def trip(j, acc):
                conv_rows(pl.multiple_of(j * rpt, rpt))
                return step(pl.multiple_of(CTX_LEN + j * tk, 256), tk, acc)

            finish(lax.fori_loop(0, n_chunks, trip, step(0, CTX_LEN, zero)))

        ga = ga_ref[...]
        du = cva_ref[0:1, :] * gaext[pl.ds(HALO + 1, TM), :]
        dca_ref[0:1, :] += _colsum(ga * uext[pl.ds(HALO - 1, TM), :])
        for kk in range(1, SHORT_CONV_K):
            du = du + cva_ref[kk:kk + 1, :] * gaext[pl.ds(HALO + 1 - kk, TM), :]
            dca_ref[kk:kk + 1, :] += _colsum(ga * uext[pl.ds(HALO - 1 + kk, TM), :])
        dz = dz_s[...]
        a_c, a_h = pc_[:, 0:GROUP_W], pc_[:, GROUP_W:2 * GROUP_W]
        glu_a, glu_g = pc_[:, 2 * GROUP_W:3 * GROUP_W], pc_[:, 3 * GROUP_W:4 * GROUP_W]
        sg = _sigmoid(glu_g)
        dpc_ref[...] = jnp.concatenate([du * a_h, du * a_c, dz * sg, dz * glu_a * sg * (1.0 - sg)], axis=1).astype(BF16)

        @pl.when(i == n_tiles - 1)
        def _():
            pltpu.sync_copy(dk_acc, dk_hbm)
            pltpu.sync_copy(dv_acc, dv_hbm)
            _scatter_phase(1, pairs, *sems)

    kv_shape = jax.ShapeDtypeStruct((N_KV_HEADS, t, LANES), F32)
    hbm = pl.BlockSpec(memory_space=pl.ANY)
    pc_prev, pc_next = _halo_specs(W_C, t)
    g_prev, g_next = _halo_specs(GROUP_W, t)
    ext = pltpu.VMEM((TM + 2 * HALO, GROUP_W), F32)
    outs = _pc(
        body, name="attention_bwd_scatter", grid=(n_tiles,),
        out_shape=(jax.ShapeDtypeStruct((N_Q_HEADS, t, LANES), F32), kv_shape, kv_shape,
                   jax.ShapeDtypeStruct((t, W_C), BF16), jax.ShapeDtypeStruct((8, GROUP_W), F32),
                   jax.ShapeDtypeStruct((32, GROUP_W), F32)) + tuple(jax.ShapeDtypeStruct(a.shape, a.dtype) for a in slabs),
        in_specs=[_heads(N_Q_HEADS, LANES), _heads(N_Q_HEADS, LANES),
                  _const((N_KV_HEADS, t, LANES)), _const((N_KV_HEADS, t, LANES)),
                  _rows(W_C), pc_prev, pc_next, _rows(GROUP_W), g_prev, g_next, _rows(GROUP_W), g_prev, g_next,
                  _const((8, GROUP_W)), _const((32, GROUP_W))] + [hbm] * n_sl,
        out_specs=(_heads(N_Q_HEADS, LANES), hbm, hbm, _rows(W_C), _full((8, GROUP_W)), _full((32, GROUP_W))) + (hbm,) * n_sl,
        scratch_shapes=[pltpu.VMEM((N_KV_HEADS, t, LANES), F32), pltpu.VMEM((N_KV_HEADS, t, LANES), F32),
                        ext, ext, ext, ext, pltpu.VMEM((TM, GROUP_W), F32)] + _comm_scratch(SCATTER_SEMS, n_sl),
        compiler_params=_cparams(),
    )(q, doe, k, v, pc, pc, pc, g_a, g_a, g_a, g_b, g_b, g_b, conv_a, conv_b, *slabs)
    return outs[0], outs[1], outs[2], outs[3:6], tuple(outs[6:])


def _in_proj_bwd(dpc, dpr, dq, dk, dv, pq, qk_gain, cos_t, sin_t, w_c, w_r, w_q, xt, dxo, modv, g_pre):
    t = xt.shape[0]

    def body(dpc_ref, dpr_ref, dq_ref, dk_ref, dv_ref, pq_ref, gain_ref, cos_ref, sin_ref, wc_ref, wr_ref, wq_ref,
             x_ref, dxo_ref, mod_ref, g_ref, dx_ref, dpq_ref, acc_ref, dgain_ref):
        i = pl.program_id(0)

        @pl.when(i == 0)
        def _():
            acc_ref[...] = jnp.zeros_like(acc_ref)
            dgain_ref[...] = jnp.zeros_like(dgain_ref)

        lane = _lane(TM)
        lo = lane < HEAD_DIM
        lo16 = (lane & 31) < 16
        g = g_ref[...]
        for jj in range(SUB):
            rows = pl.ds(jj * TM, TM)
            is_ctx = i * SUB + jj < N_CTX_TILES
            cos = cos_ref[rows, :]
            sin = sin_ref[rows, :]
            outs = []
            for b in range(3):
                src = dq_ref if b < 2 else dk_ref
                base = 2 * b if b < 2 else 0
                drot = src[base, rows, :] + pltpu.roll(src[base + 1, rows, :], HEAD_DIM, 1)
                if b < 2:
                    drot = drot * ATTN_SCALE
                dxg = drot * cos + _swap16(drot * sin, lo16)
                xh, r = _head_norm(pq_ref[rows, b * LANES:(b + 1) * LANES], lo)
                row = 0 if b < 2 else 1
                dgain_ref[row:row + 1, :] += _colsum(dxg * xh)
                dxh = dxg * gain_ref[row:row + 1, :]
                outs.append(r * (dxh - xh * (_pair_sums(dxh * xh, lo) * (1.0 / HEAD_DIM))))
            outs.append(dv_ref[0, rows, :] + pltpu.roll(dv_ref[1, rows, :], HEAD_DIM, 1))
            dpq = jnp.concatenate(outs, axis=1).astype(BF16)
            dpq_ref[rows, :] = dpq

            dh = _dot_nt(dpc_ref[rows, :], wc_ref[...]) + _dot_nt(dpr_ref[rows, :], wr_ref[...]) + _dot_nt(dpq, wq_ref[...])
            x = x_ref[rows, :]
            r = lax.rsqrt(jnp.mean(x * x, axis=1, keepdims=True) + RMS_EPS)
            xn = x * r
            sc = jnp.where(is_ctx, mod_ref[1:2, :], mod_ref[4:5, :])
            dsh = _colsum(dh)
            dsc = _colsum(dh * (xn * g))
            acc_ref[0:1, :] += jnp.where(is_ctx, dsh, 0.0)
            acc_ref[1:2, :] += jnp.where(is_ctx, dsc, 0.0)
            acc_ref[2:3, :] += jnp.where(is_ctx, 0.0, dsh)
            acc_ref[3:4, :] += jnp.where(is_ctx, 0.0, dsc)
            dxg = dh * (1.0 + sc)
            acc_ref[4:5, :] += _colsum(dxg * xn)
            dxn = dxg * g
            dx_ref[rows, :] = r * (dxn - xn * jnp.mean(dxn * xn, axis=1, keepdims=True)) + dxo_ref[rows, :]

    return _pc(
        body, name="in_proj_bwd", grid=(t // TB,),
        out_shape=(jax.ShapeDtypeStruct((t, D_MODEL), F32), jax.ShapeDtypeStruct((t, W_Q), BF16),
                   jax.ShapeDtypeStruct((8, D_MODEL), F32), jax.ShapeDtypeStruct((8, LANES), F32)),
        in_specs=[_rows(W_C, TB), _rows(W_R, TB),
                  _heads(N_Q_HEADS, LANES, TB), _heads(N_KV_HEADS, LANES, TB), _heads(N_KV_HEADS, LANES, TB), _rows(W_Q, TB),
                  _const((8, LANES)), _rows(LANES, TB), _rows(LANES, TB),
                  _const((D_MODEL, W_C)), _const((D_MODEL, W_R)), _const((D_MODEL, W_Q)),
                  _rows(D_MODEL, TB), _rows(D_MODEL, TB), _const((8, D_MODEL)), _const((1, D_MODEL))],
        out_specs=(_rows(D_MODEL, TB), _rows(W_Q, TB), _full((8, D_MODEL)), _full((8, LANES))),
        compiler_params=_cparams(),
    )(dpc, dpr, dq, dk, dv, pq, qk_gain, cos_t, sin_t, w_c, w_r, w_q, xt, dxo, modv, g_pre)


def _in_proj_wgrad(h, dpc, dpr, dpq):
    t = h.shape[0]

    def body(h_ref, dpc_ref, dpr_ref, dpq_ref, gc_ref, gr_ref, gq_ref):
        @pl.when(pl.program_id(0) == 0)
        def _():
            gc_ref[...] = jnp.zeros_like(gc_ref)
            gr_ref[...] = jnp.zeros_like(gr_ref)
            gq_ref[...] = jnp.zeros_like(gq_ref)

        hb = h_ref[...]
        gc_ref[...] += _dot_tn(hb, dpc_ref[...])
        gr_ref[...] += _dot_tn(hb, dpr_ref[...])
        gq_ref[...] += _dot_tn(hb, dpq_ref[...])

    return _pc(
        body, name="in_proj_wgrad", grid=(t // TB,),
        out_shape=(jax.ShapeDtypeStruct((D_MODEL, W_C), F32), jax.ShapeDtypeStruct((D_MODEL, W_R), F32),
                   jax.ShapeDtypeStruct((D_MODEL, W_Q), F32)),
        in_specs=[_rows(D_MODEL, TB), _rows(W_C, TB), _rows(W_R, TB), _rows(W_Q, TB)],
        out_specs=(_full((D_MODEL, W_C)), _full((D_MODEL, W_R)), _full((D_MODEL, W_Q))),
        compiler_params=_cparams(),
    )(h, dpc, dpr, dpq)


def _sum_slabs(slabs, tile_rows):
    n, r, c = slabs.shape

    def body(s_ref, o_ref):
        acc = s_ref[0].astype(F32)
        for k in range(1, n):
            acc = acc + s_ref[k].astype(F32)
        o_ref[...] = acc

    return _pc(
        body, name="sum_slabs", grid=(r // tile_rows,),
        out_shape=jax.ShapeDtypeStruct((r, c), F32),
        in_specs=[pl.BlockSpec((n, tile_rows, c), lambda i: (0, i, 0))],
        out_specs=pl.BlockSpec((tile_rows, c), lambda i: (i, 0)),
        compiler_params=_cparams(),
    )(slabs)


def _sum_layer_slabs(layers, tile_rows):
    nl = len(layers)
    n, r, c = layers[0].shape
    per = r // tile_rows

    def body(*refs):
        o_ref = refs[nl]
        for l in range(nl):
            @pl.when(pl.program_id(0) // per == l)
            def _(l=l):
                acc = refs[l][0].astype(F32)
                for k in range(1, n):
                    acc = acc + refs[l][k].astype(F32)
                o_ref[...] = acc

    def spec(l):
        return pl.BlockSpec((n, tile_rows, c), lambda i: (0, jnp.clip(i - l * per, 0, per - 1), 0))

    return _pc(
        body, name="sum_layer_slabs", grid=(nl * per,),
        out_shape=jax.ShapeDtypeStruct((nl * r, c), F32),
        in_specs=[spec(l) for l in range(nl)],
        out_specs=pl.BlockSpec((tile_rows, c), lambda i: (i, 0)),
        compiler_params=_cparams(),
    )(*layers)


def _adamw(grads, w, m, v, tile_rows):
    r, c = w.shape
    n_g = len(grads)

    def body(*refs):
        g = refs[0][...]
        for k in range(1, n_g):
            g = g + refs[k][...]
        w_ref, m_ref, v_ref, g_out, d_out, m_out, v_out = refs[n_g:]
        m_new = ADAM_B1 * m_ref[...] + (1.0 - ADAM_B1) * g
        v_new = ADAM_B2 * v_ref[...] + (1.0 - ADAM_B2) * (g * g)
        m_hat = m_new / (1.0 - ADAM_B1 ** ADAM_STEP)
        v_hat = v_new / (1.0 - ADAM_B2 ** ADAM_STEP)
        g_out[...] = g
        d_out[...] = -ADAM_LR * (m_hat / (jnp.sqrt(v_hat) + ADAM_EPS) + ADAM_WD * w_ref[...])
        m_out[...] = m_new
        v_out[...] = v_new

    spec = pl.BlockSpec((tile_rows, c), lambda i: (i, 0))
    shape = jax.ShapeDtypeStruct((r, c), F32)
    return _pc(
        body, name="adamw", grid=(r // tile_rows,),
        out_shape=(shape,) * 4, in_specs=[spec] * (n_g + 3), out_specs=(spec,) * 4,
        compiler_params=_cparams(),
    )(*grads, w, m, v)


def _rope_tables(s_lat):
    n_rows = s_lat // GRID_W
    axis_dim = HEAD_DIM // 2
    inv_freq = 1.0 / (ROPE_THETA ** (jnp.arange(0, axis_dim, 2, dtype=F32) / axis_dim))
    d = np.arange(LANES) % HEAD_DIM
    on_rows = (d // axis_dim) == 0
    freq = d % (axis_dim // 2)
    sign = np.where((d % axis_dim) < axis_dim // 2, -1.0, 1.0).astype(np.float32)
    ang_r = jnp.arange(n_rows, dtype=F32)[:, None] * inv_freq[freq][None, :]
    ang_c = jnp.arange(GRID_W, dtype=F32)[:, None] * inv_freq[freq][None, :]

    def spread(fn):
        full = jnp.where(on_rows[None, None, :], fn(ang_r)[:, None, :], fn(ang_c)[None, :, :])
        return full.reshape(s_lat, LANES)

    cos = jnp.concatenate([jnp.ones((CTX_LEN, LANES), F32), spread(jnp.cos)], axis=0)
    sin = jnp.concatenate([jnp.zeros((CTX_LEN, LANES), F32), spread(jnp.sin) * sign[None, :]], axis=0)
    return cos, sin


def _pad_rows(a, rows):
    return jnp.concatenate([a, jnp.zeros((rows - a.shape[0],) + a.shape[1:], a.dtype)], axis=0)


_SMALL = ("c_ctx", "b_mod", "g_pre", "g_post", "conv_a", "conv_b", "conv_b_bias", "conf_ln_g", "conf_ln_b",
          "sgu_ln_g", "sgu_ln_b", "w_s", "b_s", "q_gain", "k_gain")


def _pack(arrays):
    flat = jnp.concatenate([a.reshape(-1) for a in arrays])
    rows = -(-flat.shape[0] // (16 * LANES)) * 16
    return _pad_rows(flat.reshape(-1, 1), rows * LANES).reshape(rows, LANES)


def _unpack(packed, shapes):
    flat = packed.reshape(-1)
    out, off = [], 0
    for s in shapes:
        n = int(np.prod(s))
        out.append(flat[off:off + n].reshape(s))
        off += n
    return out


def kernel(x, c, ctx, c_ctx, w_mod, b_mod, g_pre, g_post, w_in, w_out, conv_a, conv_b, conv_b_bias, conf_ln_g, conf_ln_b, sgu_ln_g, sgu_ln_b, w_s, b_s, q_gain, k_gain, loss_target, m_c_ctx, m_w_mod, m_b_mod, m_g_pre, m_g_post, m_w_in, m_w_out, m_conv_a, m_conv_b, m_conv_b_bias, m_conf_ln_g, m_conf_ln_b, m_sgu_ln_g, m_sgu_ln_b, m_w_s, m_b_s, m_q_gain, m_k_gain, v_c_ctx, v_w_mod, v_b_mod, v_g_pre, v_g_post, v_w_in, v_w_out, v_conv_a, v_conv_b, v_conv_b_bias, v_conf_ln_g, v_conf_ln_b, v_sgu_ln_g, v_sgu_ln_b, v_w_s, v_b_s, v_q_gain, v_k_gain):
    weights = dict(c_ctx=c_ctx, w_mod=w_mod, b_mod=b_mod, g_pre=g_pre, g_post=g_post, w_in=w_in, w_out=w_out, conv_a=conv_a,
                   conv_b=conv_b, conv_b_bias=conv_b_bias, conf_ln_g=conf_ln_g, conf_ln_b=conf_ln_b, sgu_ln_g=sgu_ln_g,
                   sgu_ln_b=sgu_ln_b, w_s=w_s, b_s=b_s, q_gain=q_gain, k_gain=k_gain)
    m_in = dict(c_ctx=m_c_ctx, w_mod=m_w_mod, b_mod=m_b_mod, g_pre=m_g_pre, g_post=m_g_post, w_in=m_w_in, w_out=m_w_out,
                conv_a=m_conv_a, conv_b=m_conv_b, conv_b_bias=m_conv_b_bias, conf_ln_g=m_conf_ln_g, conf_ln_b=m_conf_ln_b,
                sgu_ln_g=m_sgu_ln_g, sgu_ln_b=m_sgu_ln_b, w_s=m_w_s, b_s=m_b_s, q_gain=m_q_gain, k_gain=m_k_gain)
    v_in = dict(c_ctx=v_c_ctx, w_mod=v_w_mod, b_mod=v_b_mod, g_pre=v_g_pre, g_post=v_g_post, w_in=v_w_in, w_out=v_w_out,
                conv_a=v_conv_a, conv_b=v_conv_b, conv_b_bias=v_conv_b_bias, conf_ln_g=v_conf_ln_g, conf_ln_b=v_conf_ln_b,
                sgu_ln_g=v_sgu_ln_g, sgu_ln_b=v_sgu_ln_b, w_s=v_w_s, b_s=v_b_s, q_gain=v_q_gain, k_gain=v_k_gain)
    order = ("c_ctx", "w_mod", "b_mod", "g_pre", "g_post", "w_in", "w_out", "conv_a", "conv_b", "conv_b_bias", "conf_ln_g",
             "conf_ln_b", "sgu_ln_g", "sgu_ln_b", "w_s", "b_s", "q_gain", "k_gain")

    s_lat = x.shape[1]
    ax, ay, ac = lax.axis_index("x"), lax.axis_index("y"), lax.axis_index("c")
    chip = 2 * ax + ay
    example = 4 * ax + 2 * ay + ac

    c_rows = _all_gather_rows(_pad_rows(c, 8))[::8]
    c16 = _pad_rows(jnp.concatenate([c_rows, c_ctx[None, :]], axis=0), 16)
    b_mod_shard = lax.dynamic_slice_in_dim(b_mod, chip * SHARD_MOD, SHARD_MOD, axis=1)[:, None, :]
    silu_c, mod_shard = _mod_forward(c16, w_mod, b_mod_shard)
    mod_all = _all_gather_rows(mod_shard.reshape(DEPTH * 16, SHARD_MOD)).reshape(8, DEPTH, 16, SHARD_MOD)
    mod_full = jnp.transpose(mod_all[::2], (1, 2, 0, 3)).reshape(DEPTH, 16, 3 * D_MODEL)
    mod_lat = lax.dynamic_index_in_dim(mod_full, example, axis=1, keepdims=False).reshape(DEPTH, 3, D_MODEL)
    mod_ctx = mod_full[:, 8].reshape(DEPTH, 3, D_MODEL)
    modv = jnp.concatenate([mod_ctx, mod_lat, jnp.zeros((DEPTH, 2, D_MODEL), F32)], axis=1)

    wi_b, wo_b = w_in.astype(BF16), w_out.astype(BF16)

    def regroup(wi_all):
        wi_full = jnp.concatenate([wi_all[k] for k in range(N_CHIPS)], axis=-1)
        wc_l = jnp.concatenate([wi_full[:, 256:768], wi_full[:, 1024:1536]], axis=-1)
        wr_l = jnp.concatenate([wi_full[:, 0:256], wi_full[:, 768:1024], wi_full[:, 1536:2560], wi_full[:, 3072:3328]], axis=-1)
        return wc_l, wr_l, wi_full[:, 2560:3072]

    w_c, w_r, w_q, wo_full = [None] * DEPTH, [None] * DEPTH, [None] * DEPTH, [None] * DEPTH
    w_c[0], w_r[0], w_q[0] = regroup(_gather_weights((wi_b[0],))[0])

    cos_t, sin_t = _rope_tables(s_lat)
    conv_a_full = jnp.zeros((DEPTH, 8, GROUP_W), F32)
    conv_b_full = jnp.zeros((DEPTH, 32, GROUP_W), F32)
    conv_small = jnp.concatenate([conv_a.reshape(DEPTH * SHORT_CONV_K, -1), conv_b.reshape(DEPTH * CONFORMER_K, -1)], axis=0)
    n_cs = conv_small.shape[0]
    conv_rows = -(-n_cs // 8) * 8
    conv_all = _all_gather_rows(_pad_rows(conv_small, conv_rows)).reshape(8, conv_rows, -1)[::2]
    conv_all = jnp.transpose(conv_all, (1, 0, 2)).reshape(conv_rows, GROUP_W)
    conv_a_full = conv_a_full.at[:, :SHORT_CONV_K].set(conv_all[:DEPTH * SHORT_CONV_K].reshape(DEPTH, SHORT_CONV_K, GROUP_W))
    conv_b_full = conv_b_full.at[:, :CONFORMER_K].set(
        conv_all[DEPTH * SHORT_CONV_K:n_cs].reshape(DEPTH, CONFORMER_K, GROUP_W))

    vecs = jnp.stack([conv_b_bias, conf_ln_g, conf_ln_b, sgu_ln_g, sgu_ln_b] + [jnp.zeros_like(conv_b_bias)] * 3, axis=1)
    wss = w_s.reshape(DEPTH, N_SPATIAL_GROUPS * CHUNK, CHUNK).astype(BF16)
    wsts = jnp.swapaxes(w_s, 2, 3).reshape(DEPTH, N_SPATIAL_GROUPS * CHUNK, CHUNK).astype(BF16)
    bsm = jnp.repeat(jnp.swapaxes(b_s, 1, 2), HEAD_DIM, axis=2)
    qk_gain = jnp.concatenate([jnp.tile(q_gain, (1, 2))[:, None, :], jnp.tile(k_gain, (1, 2))[:, None, :],
                               jnp.zeros((DEPTH, 6, LANES), F32)], axis=1)

    xt = jnp.concatenate([ctx[0], x[0]], axis=0)
    saved = []
    for l in range(DEPTH):
        h, pc, pr, pq, q, k, v = _in_proj(xt, modv[l], g_pre[l][None, :], w_c[l], w_r[l], w_q[l], qk_gain[l], cos_t, sin_t)
        oe, gathered = _attention_fwd(q, k, v, (wo_b[l],) + ((wi_b[l + 1],) if l + 1 < DEPTH else ()))
        wo_full[l] = jnp.concatenate([gathered[0][k] for k in range(N_CHIPS)], axis=0)
        if l + 1 < DEPTH:
            w_c[l + 1], w_r[l + 1], w_q[l + 1] = regroup(gathered[1])
        mixed = _mix_out(pc, pr, oe, xt, modv[l], g_post[l][None, :], wo_full[l], conv_a_full[l], conv_b_full[l],
                         vecs[l], wss[l], bsm[l], loss_target[0] if l + 1 == DEPTH else None)
        x_new, y, ca, z2 = mixed[:4]
        saved.append(dict(x=xt, h=h, pc=pc, pr=pr, pq=pq, q=q, k=k, v=v, oe=oe, y=y, ca=ca, z2=z2))
        xt = x_new
    dxo = xt
    loss = lax.psum(mixed[4][0, 0], ("x", "y", "c"))

    g_small = {n: [None] * DEPTH for n in _SMALL}
    d_mod, landed_in, landed_out = [None] * DEPTH, [None] * DEPTH, [None] * DEPTH
    slab_in = None
    for l in reversed(range(DEPTH)):
        s = saved[l]
        dpr, g_a, g_b, doe, gw_o, pvec, s256, dws, dbs = _mix_out_bwd(
            dxo, s["y"], s["pr"], s["ca"], s["z2"], s["oe"], modv[l], g_post[l][None, :], wo_full[l], vecs[l], wss[l], wsts[l], bsm[l])
        slab_out = gw_o.reshape(N_CHIPS, SHARD_OUT, D_MODEL).astype(BF16)
        dq, dk, dv, (dpc, dca, dcb), got = _attention_bwd(
            s["q"], s["k"], s["v"], doe, (slab_out,) + (() if slab_in is None else (slab_in,)),
            (s["pc"], g_a, g_b, conv_a_full[l], conv_b_full[l]))
        landed_out[l] = got[0]
        if slab_in is not None:
            landed_in[l + 1] = got[1]
        dxo, dpq, acc, dgain = _in_proj_bwd(dpc, dpr, dq, dk, dv, s["pq"], qk_gain[l], cos_t, sin_t, w_c[l], w_r[l], w_q[l],
                                            s["x"], dxo, modv[l], g_pre[l][None, :])
        gw_c, gw_r, gw_q = _in_proj_wgrad(s["h"], dpc, dpr, dpq)
        gw_in = jnp.concatenate([gw_r[:, 0:256], gw_c[:, 0:512], gw_r[:, 256:512], gw_c[:, 512:1024],
                                 gw_r[:, 512:1536], gw_q, gw_r[:, 1536:1792]], axis=-1)
        slab_in = jnp.transpose(gw_in.reshape(D_MODEL, N_CHIPS, SHARD_IN), (1, 0, 2)).astype(BF16)
        d_mod[l] = jnp.stack([jnp.concatenate([acc[2], acc[3], pvec[1]]), jnp.concatenate([acc[0], acc[1], pvec[0]])])
        g_small["g_pre"][l] = acc[4]
        g_small["g_post"][l] = pvec[2]
        g_small["conv_a"][l] = dca[:SHORT_CONV_K]
        g_small["conv_b"][l] = dcb[:CONFORMER_K]
        g_small["conv_b_bias"][l] = s256[0]
        g_small["conf_ln_g"][l] = s256[1]
        g_small["conf_ln_b"][l] = s256[2]
        g_small["sgu_ln_g"][l] = s256[3]
        g_small["sgu_ln_b"][l] = s256[4]
        g_small["w_s"][l] = dws.reshape(N_SPATIAL_GROUPS, CHUNK, CHUNK)
        g_small["b_s"][l] = jnp.transpose(dbs[:, :N_SPATIAL_GROUPS])
        g_small["q_gain"][l] = dgain[0, :HEAD_DIM] + dgain[0, HEAD_DIM:]
        g_small["k_gain"][l] = dgain[1, :HEAD_DIM] + dgain[1, HEAD_DIM:]
    grad_x = dxo[CTX_LEN:][None]

    d_mod_all = _all_gather_rows(jnp.stack(d_mod).reshape(DEPTH * 2, 3 * D_MODEL)).reshape(8, DEPTH, 2, 3 * D_MODEL)
    d_lat = jnp.transpose(d_mod_all[:, :, 0], (1, 0, 2))
    d_ctx = jnp.transpose(d_mod_all[:, :, 1], (1, 0, 2))
    cols = lambda a: lax.dynamic_slice_in_dim(a.reshape(DEPTH, 8, N_CHIPS, SHARD_MOD), chip, 1, axis=2)[:, :, 0]
    silu_t = jnp.transpose(silu_c)
    s_t = jnp.concatenate([silu_t[:, 0:8], jnp.tile(silu_t[:, 8:9], (1, 8)), jnp.zeros((D_MODEL, LANES - 16), F32)], axis=1)
    g_rows = jnp.concatenate([cols(d_lat), cols(d_ctx), jnp.zeros((DEPTH, LANES - 16, SHARD_MOD), F32)], axis=1)
    g_w_mod, g_b_mod, c_ctx_part = _mod_backward(s_t, g_rows, cols(d_ctx), jnp.concatenate([d_lat, d_ctx], axis=1),
                                                 w_mod, c_ctx[:, None])

    for n in _SMALL:
        if n not in ("c_ctx", "b_mod"):
            g_small[n] = jnp.stack(g_small[n])
    small_parts = [0.5 * c_ctx_part[:, 0]] + [g_small[n] for n in _SMALL[2:]]
    packed = _pack(small_parts)
    gathered = _all_gather_rows(packed.astype(BF16)).reshape(8, packed.shape[0], LANES)
    small_sum = _sum_slabs(gathered, packed.shape[0])
    small_g = dict(zip(("c_ctx",) + _SMALL[2:], _unpack(small_sum, [p.shape for p in small_parts])))
    small_g["b_mod"] = g_b_mod[:, 0]
    ch64 = GROUP_W // N_CHIPS
    for n in ("conv_a", "conv_b"):
        small_g[n] = lax.dynamic_slice_in_dim(small_g[n], chip * ch64, ch64, axis=2)
    sw = _pack([weights[n] for n in _SMALL])
    sm = _pack([m_in[n] for n in _SMALL])
    sv = _pack([v_in[n] for n in _SMALL])
    sg = _pack([small_g[n] for n in _SMALL])
    shapes = [weights[n].shape for n in _SMALL]
    small_out = [dict(zip(_SMALL, _unpack(o, shapes))) for o in _adamw([sg], sw, sm, sv, sg.shape[0])]

    landed_in[0] = _scatter_slabs((slab_in,))[0]
    sum_in = _sum_layer_slabs(landed_in, 512)
    sum_out = _sum_layer_slabs(landed_out, 256)
    sib_in, sib_out = _swap_with_sibling(sum_in, sum_out)

    big = {}
    flat = lambda a: a.reshape(-1, a.shape[-1])
    for n, grads, rows in (("w_in", [sum_in, sib_in], 512), ("w_out", [sum_out, sib_out], 256), ("w_mod", [flat(g_w_mod)], 512)):
        outs = _adamw(grads, flat(weights[n]), flat(m_in[n]), flat(v_in[n]), rows)
        big[n] = [o.reshape(weights[n].shape) for o in outs]

    def leaf(n, j):
        return big[n][j] if n in big else small_out[j][n]

    return (loss, grad_x, *[leaf(n, 0) for n in order], *[leaf(n, 1) for n in order],
            *[leaf(n, 2) for n in order], *[leaf(n, 3) for n in order])
```

```python
import functools

import numpy as np
import jax
import jax.numpy as jnp
from jax import lax
from jax.experimental import pallas as pl
from jax.experimental.pallas import tpu as pltpu

F32 = jnp.float32
BF16 = jnp.bfloat16
MESH = pl.DeviceIdType.MESH

D_MODEL = 1024
DEPTH = 4
GRID_W = 64
CTX_LEN = 256
GROUP_W = 256
HEAD_DIM = 64
N_Q_HEADS = 4
N_KV_HEADS = 2
GQA = N_Q_HEADS // N_KV_HEADS
ROPE_THETA = 10000.0
ATTN_SCALE = HEAD_DIM ** -0.5
SHORT_CONV_K = 3
CONFORMER_K = 31
CHUNK = 128
N_SPATIAL_GROUPS = 4
RMS_EPS = 1e-6
LN_EPS = 1e-5
ADAM_LR = 0.001
ADAM_B1 = 0.9
ADAM_B2 = 0.999
ADAM_EPS = 1e-08
ADAM_WD = 0.01
ADAM_STEP = 10

LANES = 128
HALO = 16
CONV_ROWS = 64
TM = 256
N_CTX_TILES = CTX_LEN // TM
SUB = 3
TB = SUB * TM
W_C = 1024
W_R = 1792
W_Q = 512
PROJ_W = W_C + W_R + W_Q
N_CHIPS = 4
SHARD_IN = PROJ_W // N_CHIPS
SHARD_OUT = D_MODEL // N_CHIPS
SHARD_MOD = 3 * D_MODEL // N_CHIPS
VMEM_LIMIT = 56 * 1024 * 1024


def _pc(body, **kw):
    return pl.pallas_call(body, **kw)


def _cparams(**kw):
    return pltpu.CompilerParams(dimension_semantics=("arbitrary",), vmem_limit_bytes=VMEM_LIMIT, **kw)


def _full(shape):
    n = len(shape)
    return pl.BlockSpec(shape, lambda i: (0,) * n)


def _const(shape):
    n = len(shape)
    return pl.BlockSpec(shape, lambda i: (0,) * n, pipeline_mode=pl.Buffered(1))


def _rows(width, tm=TM):
    return pl.BlockSpec((tm, width), lambda i: (i, 0))


def _heads(nh, width, tm=TM):
    return pl.BlockSpec((nh, tm, width), lambda i: (0, i, 0))


def _sigmoid(x):
    return jax.nn.sigmoid(x)


def _dot(a, b):
    return jnp.dot(a, b, preferred_element_type=F32)


def _dot_nt(a, b):
    return lax.dot_general(a, b, (((1,), (1,)), ((), ())), preferred_element_type=F32)


def _dot_tn(a, b):
    return lax.dot_general(a, b, (((0,), (0,)), ((), ())), preferred_element_type=F32)


def _lane(rows):
    return lax.broadcasted_iota(jnp.int32, (rows, LANES), 1)


def _rowsum(x):
    return jnp.sum(x, axis=1, keepdims=True)


def _colsum(x):
    return jnp.sum(x, axis=0, keepdims=True)


def _pair_sums(x, lo):
    s0 = _rowsum(jnp.where(lo, x, 0.0))
    s1 = _rowsum(jnp.where(lo, 0.0, x))
    return jnp.where(lo, s0, s1)


def _swap16(x, lo16):
    return jnp.where(lo16, pltpu.roll(x, LANES - 16, 1), pltpu.roll(x, 16, 1))


def _layer_norm_stats(x):
    mu = jnp.mean(x, axis=1, keepdims=True)
    xc = x - mu
    rs = lax.rsqrt(jnp.mean(xc * xc, axis=1, keepdims=True) + LN_EPS)
    return xc * rs, rs


def _layer_norm_bwd(dxn, xn, rs):
    return rs * (dxn - jnp.mean(dxn, axis=1, keepdims=True) - xn * jnp.mean(dxn * xn, axis=1, keepdims=True))


def _group_select(r, grp):
    out = jnp.where(grp == 0, r[0:CHUNK], 0.0)
    for g in range(1, N_SPATIAL_GROUPS):
        out = out + jnp.where(grp == g, r[g * CHUNK:(g + 1) * CHUNK], 0.0)
    return out


def _kv_chunk(t):
    return 1024 if (t - CTX_LEN) % 1024 == 0 else 256


def _all_gather_rows(x_shard):
    m_per, n = x_shard.shape

    def body(x_ref, out_ref, send_sems, recv_sems, local_sem):
        x, y, c = lax.axis_index("x"), lax.axis_index("y"), lax.axis_index("c")
        me, sibling = (x, y, c), (x, y, 1 - c)
        chips = [(1 - x, y), (x, 1 - y), (1 - x, 1 - y)]

        def rows(px, py, pc):
            return out_ref.at[pl.ds((4 * px + 2 * py + pc) * m_per, m_per), :]

        def copy(k, block, to, src=None):
            return pltpu.make_async_remote_copy(
                src_ref=rows(*block) if src is None else src, dst_ref=rows(*block),
                send_sem=send_sems.at[k], recv_sem=recv_sems.at[k], device_id=to, device_id_type=MESH)

        mine = pltpu.make_async_copy(x_ref, rows(*me), local_sem)
        mine.start()
        first = [copy(0, me, sibling, src=x_ref)]
        first += [copy(1 + j, me, (*chip, c), src=x_ref) for j, chip in enumerate(chips)]
        for cp in first:
            cp.start()
        passed = [copy(4 + j, (*chip, c), sibling) for j, chip in enumerate(chips)]
        for j, chip in enumerate(chips):
            copy(1 + j, (*chip, c), me).wait_recv()
            passed[j].start()
        copy(0, sibling, me).wait_recv()
        for j, chip in enumerate(chips):
            copy(4 + j, (*chip, 1 - c), me).wait_recv()
        for cp in first + passed:
            cp.wait_send()
        mine.wait()

    return _pc(
        body, name="all_gather_rows",
        out_shape=jax.ShapeDtypeStruct((8 * m_per, n), x_shard.dtype),
        in_specs=[pl.BlockSpec(memory_space=pltpu.VMEM)],
        out_specs=pl.BlockSpec(memory_space=pltpu.VMEM),
        scratch_shapes=[pltpu.SemaphoreType.DMA((7,)), pltpu.SemaphoreType.DMA((7,)), pltpu.SemaphoreType.DMA],
        compiler_params=pltpu.CompilerParams(vmem_limit_bytes=VMEM_LIMIT),
    )(x_shard)


def _place():
    x, y, c = lax.axis_index("x"), lax.axis_index("y"), lax.axis_index("c")
    return x, y, c, [(1 - x, y), (x, 1 - y), (1 - x, 1 - y)]


def _remote(src, dst, send_sems, recv_sems, k, to):
    return pltpu.make_async_remote_copy(src_ref=src, dst_ref=dst, send_sem=send_sems.at[k], recv_sem=recv_sems.at[k],
                                        device_id=to, device_id_type=MESH)


GATHER_SEMS = 6
SCATTER_SEMS = 3


def _gather_phase(phase, pairs, send_sems, recv_sems, local_sems):
    x, y, c, chips = _place()
    kme = 2 * x + y
    sibling = (x, y, 1 - c)
    for a, (src, dst) in enumerate(pairs):
        half = src.shape[0] // 2
        mine = pl.ds(c * half, half)
        theirs = pl.ds((1 - c) * half, half)
        if phase == 0:
            pltpu.make_async_copy(src, dst.at[kme], local_sems.at[a]).start()
        if phase == 2:
            pltpu.make_async_copy(src, dst.at[kme], local_sems.at[a]).wait()
        for j, (px, py) in enumerate(chips):
            kk = 2 * px + py
            landed = dst.at[kk, mine]
            out = lambda: _remote(src.at[mine], dst.at[kme, mine], send_sems, recv_sems, 6 * a + j, (px, py, c))
            hand = lambda: _remote(landed, landed, send_sems, recv_sems, 6 * a + 3 + j, sibling)
            if phase == 0:
                out().start()
            if phase == 1:
                _remote(landed, landed, send_sems, recv_sems, 6 * a + j, (px, py, c)).wait_recv()
                hand().start()
            if phase == 2:
                other = dst.at[kk, theirs]
                _remote(other, other, send_sems, recv_sems, 6 * a + 3 + j, sibling).wait_recv()
                out().wait_send()
                hand().wait_send()


def _scatter_phase(phase, pairs, send_sems, recv_sems, local_sems):
    x, y, c, chips = _place()
    kme = 2 * x + y
    for a, (src, dst) in enumerate(pairs):
        loc = pltpu.make_async_copy(src.at[kme], dst.at[kme], local_sems.at[a])
        if phase == 0:
            loc.start()
        else:
            loc.wait()
        for j, (px, py) in enumerate(chips):
            kk = 2 * px + py
            out = _remote(src.at[kk], dst.at[kme], send_sems, recv_sems, 3 * a + j, (px, py, c))
            if phase == 0:
                out.start()
            else:
                landed = dst.at[kk]
                _remote(landed, landed, send_sems, recv_sems, 3 * a + j, (px, py, c)).wait_recv()
                out.wait_send()


def _comm_scratch(per_array, n_arrays):
    n = per_array * n_arrays
    return [pltpu.SemaphoreType.DMA((n,)), pltpu.SemaphoreType.DMA((n,)), pltpu.SemaphoreType.DMA((n_arrays,))]


def _slots(a):
    return jax.ShapeDtypeStruct((N_CHIPS,) + a.shape, a.dtype)


def _gather_weights(shards):
    n = len(shards)

    def body(*refs):
        for phase in range(3):
            _gather_phase(phase, tuple(zip(refs[:n], refs[n:2 * n])), *refs[2 * n:])

    hbm = pl.BlockSpec(memory_space=pl.ANY)
    return _pc(
        body, name="gather_weights", out_shape=tuple(_slots(a) for a in shards),
        in_specs=[hbm] * n, out_specs=(hbm,) * n, scratch_shapes=_comm_scratch(GATHER_SEMS, n),
    )(*shards)


def _scatter_slabs(slabs):
    n = len(slabs)

    def body(*refs):
        for phase in range(2):
            _scatter_phase(phase, tuple(zip(refs[:n], refs[n:2 * n])), *refs[2 * n:])

    hbm = pl.BlockSpec(memory_space=pl.ANY)
    return _pc(
        body, name="scatter_slabs", out_shape=tuple(jax.ShapeDtypeStruct(a.shape, a.dtype) for a in slabs),
        in_specs=[hbm] * n, out_specs=(hbm,) * n, scratch_shapes=_comm_scratch(SCATTER_SEMS, n),
    )(*slabs)


def _swap_with_sibling(a, b):
    def body(a_ref, b_ref, ra_ref, rb_ref, send_sems, recv_sems):
        x, y, c = lax.axis_index("x"), lax.axis_index("y"), lax.axis_index("c")
        copies = []
        for k, (src, dst) in enumerate(((a_ref, ra_ref), (b_ref, rb_ref))):
            cp = pltpu.make_async_remote_copy(
                src_ref=src, dst_ref=dst, send_sem=send_sems.at[k], recv_sem=recv_sems.at[k],
                device_id=(x, y, 1 - c), device_id_type=MESH)
            cp.start()
            copies.append(cp)
        for cp in copies:
            cp.wait()

    hbm = pl.BlockSpec(memory_space=pl.ANY)
    return _pc(
        body, name="swap_with_sibling",
        out_shape=(jax.ShapeDtypeStruct(a.shape, a.dtype), jax.ShapeDtypeStruct(b.shape, b.dtype)),
        in_specs=[hbm, hbm], out_specs=(hbm, hbm),
        scratch_shapes=[pltpu.SemaphoreType.DMA((2,)), pltpu.SemaphoreType.DMA((2,))],
    )(a, b)


def _mod_forward(c16, w_mod, b_mod_shard):
    def body(c_ref, w_ref, b_ref, s_ref, o_ref):
        cc = c_ref[...]
        s = cc * _sigmoid(cc)
        s_ref[...] = s
        o_ref[0] = jnp.dot(s, w_ref[0], preferred_element_type=F32, precision=lax.Precision.HIGHEST) + b_ref[0]

    return _pc(
        body, name="mod_forward", grid=(DEPTH,),
        out_shape=(jax.ShapeDtypeStruct((16, D_MODEL), F32), jax.ShapeDtypeStruct((DEPTH, 16, SHARD_MOD), F32)),
        in_specs=[_full((16, D_MODEL)),
                  pl.BlockSpec((1, D_MODEL, SHARD_MOD), lambda l: (l, 0, 0)),
                  pl.BlockSpec((1, 1, SHARD_MOD), lambda l: (l, 0, 0))],
        out_specs=(_full((16, D_MODEL)), pl.BlockSpec((1, 16, SHARD_MOD), lambda l: (l, 0, 0))),
        compiler_params=_cparams(),
    )(c16, w_mod, b_mod_shard)


def _mod_backward(s_t, g_rows, g_ctx, d_all, w_mod, c_ctx_col):
    def body(st_ref, g_ref, gc_ref, d_ref, w_ref, cc_ref, gw_ref, gb_ref, pc_ref):
        l = pl.program_id(0)
        gw_ref[0] = jnp.dot(st_ref[...], g_ref[0], preferred_element_type=F32, precision=lax.Precision.HIGHEST)
        gb_ref[0] = _colsum(d_ref[0])
        part = _rowsum(w_ref[0] * _colsum(gc_ref[0]))

        @pl.when(l == 0)
        def _():
            pc_ref[...] = jnp.zeros_like(pc_ref)

        pc_ref[...] += part

        @pl.when(l == DEPTH - 1)
        def _():
            cc = cc_ref[...]
            sg = _sigmoid(cc)
            pc_ref[...] = pc_ref[...] * (sg * (1.0 + cc * (1.0 - sg)))

    return _pc(
        body, name="mod_backward", grid=(DEPTH,),
        out_shape=(jax.ShapeDtypeStruct((DEPTH, D_MODEL, SHARD_MOD), F32),
                   jax.ShapeDtypeStruct((DEPTH, 1, 3 * D_MODEL), F32),
                   jax.ShapeDtypeStruct((D_MODEL, 1), F32)),
        in_specs=[_full((D_MODEL, LANES)),
                  pl.BlockSpec((1, LANES, SHARD_MOD), lambda l: (l, 0, 0)),
                  pl.BlockSpec((1, 8, SHARD_MOD), lambda l: (l, 0, 0)),
                  pl.BlockSpec((1, 16, 3 * D_MODEL), lambda l: (l, 0, 0)),
                  pl.BlockSpec((1, D_MODEL, SHARD_MOD), lambda l: (l, 0, 0)),
                  _full((D_MODEL, 1))],
        out_specs=(pl.BlockSpec((1, D_MODEL, SHARD_MOD), lambda l: (l, 0, 0)),
                   pl.BlockSpec((1, 1, 3 * D_MODEL), lambda l: (l, 0, 0)),
                   _full((D_MODEL, 1))),
        compiler_params=_cparams(),
    )(s_t, g_rows, g_ctx, d_all, w_mod, c_ctx_col)


def _head_norm(xb, lo):
    r = lax.rsqrt(_pair_sums(xb * xb, lo) * (1.0 / HEAD_DIM) + RMS_EPS)
    return xb * r, r


def _in_proj(xt, modv, g_pre, w_c, w_r, w_q, qk_gain, cos_t, sin_t):
    t = xt.shape[0]

    def body(x_ref, mod_ref, g_ref, wc_ref, wr_ref, wq_ref, gain_ref, cos_ref, sin_ref,
             h_ref, pc_ref, pr_ref, pq_ref, q_ref, k_ref, v_ref):
        lane = _lane(TM)
        lo = lane < HEAD_DIM
        lo16 = (lane & 31) < 16
        one = jnp.where(lane == HEAD_DIM, 1.0, 0.0)
        for jj in range(SUB):
            rows = pl.ds(jj * TM, TM)
            is_ctx = pl.program_id(0) * SUB + jj < N_CTX_TILES
            x = x_ref[rows, :]
            r = lax.rsqrt(jnp.mean(x * x, axis=1, keepdims=True) + RMS_EPS)
            sh = jnp.where(is_ctx, mod_ref[0:1, :], mod_ref[3:4, :])
            sc = jnp.where(is_ctx, mod_ref[1:2, :], mod_ref[4:5, :])
            h = (x * r * g_ref[...]) * (1.0 + sc) + sh
            hb = h.astype(BF16)
            h_ref[rows, :] = hb
            pc_ref[rows, :] = _dot(hb, wc_ref[...])
            pr_ref[rows, :] = _dot(hb, wr_ref[...])
            pq = _dot(hb, wq_ref[...])
            pq_ref[rows, :] = pq
            cos = cos_ref[rows, :]
            sin = sin_ref[rows, :]
            for b in range(3):
                xh, _ = _head_norm(pq[:, b * LANES:(b + 1) * LANES], lo)
                xg = xh * (gain_ref[0:1, :] if b < 2 else gain_ref[1:2, :])
                rot = xg * cos + _swap16(xg, lo16) * sin
                if b < 2:
                    rot = rot * ATTN_SCALE
                dst = q_ref if b < 2 else k_ref
                base = 2 * b if b < 2 else 0
                dst[base, rows, :] = jnp.where(lo, rot, 0.0).astype(BF16)
                dst[base + 1, rows, :] = jnp.where(lo, pltpu.roll(rot, HEAD_DIM, 1), 0.0).astype(BF16)
            vb = pq[:, 3 * LANES:4 * LANES]
            v_ref[0, rows, :] = jnp.where(lo, vb, one).astype(BF16)
            v_ref[1, rows, :] = jnp.where(lo, pltpu.roll(vb, HEAD_DIM, 1), one).astype(BF16)

    return _pc(
        body, name="in_proj", grid=(t // TB,),
        out_shape=(jax.ShapeDtypeStruct((t, D_MODEL), BF16),
                   jax.ShapeDtypeStruct((t, W_C), F32), jax.ShapeDtypeStruct((t, W_R), F32), jax.ShapeDtypeStruct((t, W_Q), F32),
                   jax.ShapeDtypeStruct((N_Q_HEADS, t, LANES), BF16),
                   jax.ShapeDtypeStruct((N_KV_HEADS, t, LANES), BF16),
                   jax.ShapeDtypeStruct((N_KV_HEADS, t, LANES), BF16)),
        in_specs=[_rows(D_MODEL, TB), _const((8, D_MODEL)), _const((1, D_MODEL)),
                  _const((D_MODEL, W_C)), _const((D_MODEL, W_R)), _const((D_MODEL, W_Q)),
                  _const((8, LANES)), _rows(LANES, TB), _rows(LANES, TB)],
        out_specs=(_rows(D_MODEL, TB), _rows(W_C, TB), _rows(W_R, TB), _rows(W_Q, TB),
                   _heads(N_Q_HEADS, LANES, TB), _heads(N_KV_HEADS, LANES, TB), _heads(N_KV_HEADS, LANES, TB)),
        compiler_params=_cparams(),
    )(xt, modv, g_pre, w_c, w_r, w_q, qk_gain, cos_t, sin_t)


def _attention_fwd(q, k, v, shards=None):
    t = q.shape[1]
    tk = _kv_chunk(t)
    n_chunks = (t - CTX_LEN) // tk
    n_tiles = t // TM
    n_sh = 0 if shards is None else len(shards)

    def body(q_ref, k_ref, v_ref, *rest):
        i = pl.program_id(0)
        o_ref = rest[n_sh]
        if shards is not None:
            pairs = tuple(zip(rest[:n_sh], rest[n_sh + 1:2 * n_sh + 1]))
            for phase, at in enumerate((0, n_tiles // 2, n_tiles - 1)):
                @pl.when(i == at)
                def _(phase=phase):
                    _gather_phase(phase, pairs, *rest[2 * n_sh + 1:])
        lane = _lane(GQA * TM)
        qs = [jnp.concatenate([q_ref[GQA * g + hh] for hh in range(GQA)], axis=0) for g in range(N_KV_HEADS)]

        def step(st, size, carry):
            out = []
            for g in range(N_KV_HEADS):
                m, acc = carry[g]
                s = _dot_nt(qs[g], k_ref[g, pl.ds(st, size), :])
                m_new = jnp.maximum(m, jnp.max(s, axis=1, keepdims=True))
                p = jnp.exp(s - m_new)
                out.append((m_new, acc * jnp.exp(m - m_new) + _dot(p.astype(BF16), v_ref[g, pl.ds(st, size), :])))
            return tuple(out)

        init = tuple((jnp.full((GQA * TM, 1), -jnp.inf, F32), jnp.zeros((GQA * TM, LANES), F32)) for _ in range(N_KV_HEADS))

        def finish(carry):
            for g in range(N_KV_HEADS):
                m, acc = carry[g]
                den = _rowsum(jnp.where(lane == HEAD_DIM, acc, 0.0))
                out = jnp.where(lane < HEAD_DIM, acc * (1.0 / den), jnp.where(lane == HEAD_DIM, m + jnp.log(den), 0.0))
                for hh in range(GQA):
                    o_ref[GQA * g + hh] = out[hh * TM:(hh + 1) * TM]

        @pl.when(i < N_CTX_TILES)
        def _():
            finish(step(0, CTX_LEN, init))

        @pl.when(i >= N_CTX_TILES)
        def _():
            per = 4 if n_chunks % 4 == 0 else 1

            def trip(j, cr):
                st = pl.multiple_of(CTX_LEN + j * (per * tk), 256)
                for u in range(per):
                    cr = step(st + u * tk, tk, cr)
                return cr

            finish(lax.fori_loop(0, n_chunks // per, trip, step(0, CTX_LEN, init)))

    hbm = pl.BlockSpec(memory_space=pl.ANY)
    extra = () if shards is None else tuple(shards)
    outs = _pc(
        body, name="attention_fwd" if shards is None else "attention_fwd_gather", grid=(n_tiles,),
        out_shape=(jax.ShapeDtypeStruct((N_Q_HEADS, t, LANES), F32),) + tuple(_slots(a) for a in extra),
        in_specs=[_heads(N_Q_HEADS, LANES), _full((N_KV_HEADS, t, LANES)), _full((N_KV_HEADS, t, LANES))] + [hbm] * n_sh,
        out_specs=(_heads(N_Q_HEADS, LANES),) + (hbm,) * n_sh,
        scratch_shapes=_comm_scratch(GATHER_SEMS, n_sh) if shards is not None else [],
        compiler_params=_cparams(),
    )(q, k, v, *extra)
    return outs[0], tuple(outs[1:])


def _halo_specs(width, t, rows=TM):
    last = t // HALO - 1
    per = rows // HALO
    prev = pl.BlockSpec((HALO, width), lambda i: (jnp.maximum(i * per - 1, 0), 0))
    nxt = pl.BlockSpec((HALO, width), lambda i: (jnp.minimum((i + 1) * per, last), 0))
    return prev, nxt


def _halo_valid(i, n_tiles):
    prev_ok = jnp.logical_and(i != 0, i != N_CTX_TILES)
    next_ok = jnp.logical_and(i != N_CTX_TILES - 1, i != n_tiles - 1)
    return jnp.where(prev_ok, 1.0, 0.0), jnp.where(next_ok, 1.0, 0.0)


def _conv_inputs(pc):
    u = pc[:, 0:GROUP_W] * pc[:, GROUP_W:2 * GROUP_W]
    z = pc[:, 2 * GROUP_W:3 * GROUP_W] * _sigmoid(pc[:, 3 * GROUP_W:4 * GROUP_W])
    return u, z


def _fill_ext(ext_ref, prev, mid, nxt):
    ext_ref[0:HALO, :] = prev
    ext_ref[HALO:HALO + TM, :] = mid
    ext_ref[HALO + TM:HALO + TM + HALO, :] = nxt


def _row_local_mixers(pr, ca, z2, oe, vecs, wss_ref, bsm, lane256):
    a_b, a_g, b_g = pr[:, 0:256], pr[:, 256:512], pr[:, 512:768]
    c_u, c_v, c_g, d_g = pr[:, 768:1024], pr[:, 1024:1280], pr[:, 1280:1536], pr[:, 1536:1792]
    zn, rs_b = _layer_norm_stats(z2)
    tb = zn * vecs[1:2, :] + vecs[2:3, :]
    vn_hat, rs_c = _layer_norm_stats(c_v)
    vn = vn_hat * vecs[3:4, :] + vecs[4:5, :]
    grp = jnp.right_shift(lane256, 6)
    sgs = []
    for ch in range(TM // CHUNK):
        r = _dot(wss_ref[...], vn[ch * CHUNK:(ch + 1) * CHUNK, :].astype(BF16))
        sgs.append(_group_select(r, grp[0:CHUNK]) + bsm)
    sg = jnp.concatenate(sgs, axis=0)
    lane = _lane(TM)
    lo = lane < HEAD_DIM
    att = jnp.concatenate([jnp.where(lo, oe[2 * b], pltpu.roll(oe[2 * b + 1], HEAD_DIM, 1)) for b in range(2)], axis=1)
    return dict(a_b=a_b, a_g=a_g, b_g=b_g, c_u=c_u, c_v=c_v, c_g=c_g, d_g=d_g, zn=zn, rs_b=rs_b, tb=tb,
                vn_hat=vn_hat, rs_c=rs_c, vn=vn, sg=sg, att=att, grp=grp, lo=lo, lane=lane)


def _mixer_concat(f, ca):
    ya = f["a_b"] * ca
    yb = f["tb"] * _sigmoid(f["tb"])
    yc = f["c_u"] * f["sg"]
    gates = [f[n] * _sigmoid(f[n]) for n in ("a_g", "b_g", "c_g", "d_g")]
    ys = (ya, yb, yc, f["att"])
    big = jnp.concatenate([yy * gg for yy, gg in zip(ys, gates)], axis=1).astype(BF16)
    return big, ys, gates


def _taps31(ext_ref, w_ref, flip):
    blocks = []
    for r0 in range(0, TM, CONV_ROWS):
        out = None
        for b in range(8):
            part = None
            for a in range(4):
                o = 8 * a + b
                if 1 <= o <= CONFORMER_K:
                    kk = CONFORMER_K - o if flip else o - 1
                    term = w_ref[kk:kk + 1, :] * ext_ref[pl.ds(r0 + 8 * a, CONV_ROWS + 8), :]
                    part = term if part is None else part + term
            part = part[b:b + CONV_ROWS]
            out = part if out is None else out + part
        blocks.append(out)
    return jnp.concatenate(blocks, axis=0)


def _mix_out(pc, pr, oe, xt, modv, g_post, w_out, conv_a, conv_b, vecs, wss, bsm, target=None):
    t = xt.shape[0]
    n_tiles = t // TM
    prev_spec, next_spec = _halo_specs(W_C, t, TB)
    n_t = 0 if target is None else SUB

    def body(pc_ref, pp_ref, pn_ref, pr_ref, oe_ref, x_ref, mod_ref, gp_ref, wo_ref, cva_ref, cvb_ref, vec_ref, wss_ref, bsm_ref,
             *rest):
        xo_ref, y_ref, ca_ref, z2_ref = rest[n_t:n_t + 4]
        uext, zext = rest[-2:]
        i = pl.program_id(0)
        vecs = vec_ref[...]
        lane256 = lax.broadcasted_iota(jnp.int32, (TM, GROUP_W), 1)
        if target is not None:
            loss_ref = rest[n_t + 4]

            @pl.when(i == 0)
            def _():
                loss_ref[...] = jnp.zeros_like(loss_ref)

        for jj in range(SUB):
            rows = pl.ds(jj * TM, TM)
            tile = i * SUB + jj
            is_ctx = tile < N_CTX_TILES
            pv, nv = _halo_valid(tile, n_tiles)
            u, z = _conv_inputs(pc_ref[rows, :])
            up, zp = _conv_inputs(pp_ref[...] if jj == 0 else pc_ref[pl.ds(jj * TM - HALO, HALO), :])
            un, zn_ = _conv_inputs(pn_ref[...] if jj == SUB - 1 else pc_ref[pl.ds((jj + 1) * TM, HALO), :])
            ue, ze = uext.at[jj], zext.at[jj]
            _fill_ext(ue, up * pv, u, un * nv)
            _fill_ext(ze, zp * pv, z, zn_ * nv)
            ca = cva_ref[0:1, :] * ue[pl.ds(HALO - 1, TM), :]
            for kk in range(1, SHORT_CONV_K):
                ca = ca + cva_ref[kk:kk + 1, :] * ue[pl.ds(HALO - 1 + kk, TM), :]
            z2 = _taps31(ze, cvb_ref, False) + vecs[0:1, :]
            ca_ref[rows, :] = ca
            z2_ref[rows, :] = z2
            oes = [oe_ref[h, rows, :] for h in range(N_Q_HEADS)]
            f = _row_local_mixers(pr_ref[rows, :], ca, z2, oes, vecs, wss_ref, bsm_ref[...], lane256)
            big, _, _ = _mixer_concat(f, ca)
            y = _dot(big, wo_ref[...])
            y_ref[rows, :] = y
            ry = lax.rsqrt(jnp.mean(y * y, axis=1, keepdims=True) + RMS_EPS)
            gt = jnp.where(is_ctx, mod_ref[2:3, :], mod_ref[5:6, :])
            x_new = x_ref[rows, :] + gt * (y * ry * gp_ref[...])
            if target is None:
                xo_ref[rows, :] = x_new
            else:
                err = (x_new - rest[jj][...]) * jnp.where(is_ctx, 0.0, 1.0)
                xo_ref[rows, :] = err * (1.0 / D_MODEL)
                loss_ref[...] += jnp.sum(err * err) * (0.5 / D_MODEL)

    rows_f32 = jax.ShapeDtypeStruct((t, D_MODEL), F32)
    group_f32 = jax.ShapeDtypeStruct((t, GROUP_W), F32)
    loss_shape, loss_spec, t_spec, t_arg = (), (), [], ()
    if target is not None:
        loss_shape, loss_spec = (jax.ShapeDtypeStruct((8, LANES), F32),), (_full((8, LANES)),)
        t_spec = [pl.BlockSpec((TM, D_MODEL), lambda i, jj=jj: (jnp.maximum(i * SUB + jj - N_CTX_TILES, 0), 0))
                  for jj in range(SUB)]
        t_arg = (target,) * SUB
    return _pc(
        body, name="mix_out" if target is None else "mix_out_loss", grid=(t // TB,),
        out_shape=(rows_f32, rows_f32, group_f32, group_f32) + loss_shape,
        in_specs=[_rows(W_C, TB), prev_spec, next_spec, _rows(W_R, TB), _heads(N_Q_HEADS, LANES, TB), _rows(D_MODEL, TB),
                  _const((8, D_MODEL)), _const((1, D_MODEL)), _const((D_MODEL, D_MODEL)),
                  _const((8, GROUP_W)), _const((32, GROUP_W)), _const((8, GROUP_W)),
                  _const((N_SPATIAL_GROUPS * CHUNK, CHUNK)), _const((CHUNK, GROUP_W))] + t_spec,
        out_specs=(_rows(D_MODEL, TB), _rows(D_MODEL, TB), _rows(GROUP_W, TB), _rows(GROUP_W, TB)) + loss_spec,
        scratch_shapes=[pltpu.VMEM((SUB, TM + 2 * HALO, GROUP_W), F32), pltpu.VMEM((SUB, TM + 2 * HALO, GROUP_W), F32)],
        compiler_params=_cparams(),
    )(pc, pc, pc, pr, oe, xt, modv, g_post, w_out, conv_a, conv_b, vecs, wss, bsm, *t_arg)


def _mix_out_bwd(dxo, y, pr, ca, z2, oe, modv, g_post, w_out, vecs, wss, wsts, bsm):
    t = y.shape[0]
    n_tiles = t // TM

    def body(dxo_ref, y_ref, pr_ref, ca_ref, z2_ref, oe_ref, mod_ref, gp_ref, wo_ref, vec_ref, wss_ref, wsts_ref, bsm_ref,
             dpr_ref, ga_ref, gb_ref, doe_ref, dwo_ref, pvec_ref, s256_ref, dws_ref, dbs_ref, dbsm):
        i = pl.program_id(0)
        is_ctx = i < N_CTX_TILES

        @pl.when(i == 0)
        def _():
            dwo_ref[...] = jnp.zeros_like(dwo_ref)
            pvec_ref[...] = jnp.zeros_like(pvec_ref)
            s256_ref[...] = jnp.zeros_like(s256_ref)
            dws_ref[...] = jnp.zeros_like(dws_ref)
            dbsm[...] = jnp.zeros_like(dbsm)

        dxo_ = dxo_ref[...]
        y_ = y_ref[...]
        ry = lax.rsqrt(jnp.mean(y_ * y_, axis=1, keepdims=True) + RMS_EPS)
        nh = y_ * ry
        gp = gp_ref[...]
        gt = jnp.where(is_ctx, mod_ref[2:3, :], mod_ref[5:6, :])
        dgt = _colsum(dxo_ * (nh * gp))
        pvec_ref[0:1, :] += jnp.where(is_ctx, dgt, 0.0)
        pvec_ref[1:2, :] += jnp.where(is_ctx, 0.0, dgt)
        dn = dxo_ * gt
        pvec_ref[2:3, :] += _colsum(dn * nh)
        dnh = dn * gp
        dy = ry * (dnh - nh * jnp.mean(dnh * nh, axis=1, keepdims=True))

        vecs = vec_ref[...]
        bsm_ = bsm_ref[...]
        ca_ = ca_ref[...]
        lane256 = lax.broadcasted_iota(jnp.int32, (TM, GROUP_W), 1)
        f = _row_local_mixers(pr_ref[...], ca_, z2_ref[...], oe_ref, vecs, wss_ref, bsm_, lane256)
        big, ys, gates = _mixer_concat(f, ca_)
        dyb = dy.astype(BF16)
        dwo_ref[...] += _dot_tn(big, dyb)
        dbig = _dot_nt(dyb, wo_ref[...])

        d_y, d_gate = [], []
        for n, (name, yy, gg) in enumerate(zip(("a_g", "b_g", "c_g", "d_g"), ys, gates)):
            dpart = dbig[:, n * GROUP_W:(n + 1) * GROUP_W]
            gx = f[name]
            sg_ = _sigmoid(gx)
            d_y.append(dpart * gg)
            d_gate.append(dpart * yy * (sg_ * (1.0 + gx * (1.0 - sg_))))
        dya, dyb_, dyc, datt = d_y

        d_ab = dya * ca_
        ga_ref[...] = dya * f["a_b"]
        tb = f["tb"]
        sb = _sigmoid(tb)
        dtb = dyb_ * (sb * (1.0 + tb * (1.0 - sb)))
        s256_ref[1:2, :] += _colsum(dtb * f["zn"])
        s256_ref[2:3, :] += _colsum(dtb)
        dz2 = _layer_norm_bwd(dtb * vecs[1:2, :], f["zn"], f["rs_b"])
        gb_ref[...] = dz2
        s256_ref[0:1, :] += _colsum(dz2)
        d_cu = dyc * f["sg"]
        dsg = dyc * f["c_u"]
        grp = f["grp"]
        dvn_parts = []
        for ch in range(TM // CHUNK):
            rows = slice(ch * CHUNK, (ch + 1) * CHUNK)
            dsg_c = dsg[rows, :]
            dbsm[...] += dsg_c
            vn_c = f["vn"][rows, :].astype(BF16)
            for g in range(N_SPATIAL_GROUPS):
                masked = jnp.where(grp[0:CHUNK] == g, dsg_c, 0.0).astype(BF16)
                dws_ref[g * CHUNK:(g + 1) * CHUNK, :] += _dot_nt(masked, vn_c)
            dvn_parts.append(_group_select(_dot(wsts_ref[...], dsg_c.astype(BF16)), grp[0:CHUNK]))
        dvn = jnp.concatenate(dvn_parts, axis=0)
        s256_ref[3:4, :] += _colsum(dvn * f["vn_hat"])
        s256_ref[4:5, :] += _colsum(dvn)
        d_cv = _layer_norm_bwd(dvn * vecs[3:4, :], f["vn_hat"], f["rs_c"])
        lane, lo = f["lane"], f["lo"]
        att = f["att"]
        for b in range(2):
            da = datt[:, b * LANES:(b + 1) * LANES]
            prod = da * att[:, b * LANES:(b + 1) * LANES]
            for hh in range(2):
                h = 2 * b + hh
                lse = _rowsum(jnp.where(lane == HEAD_DIM, oe_ref[h], 0.0))
                delta = _rowsum(jnp.where(lo, prod, 0.0) if hh == 0 else jnp.where(lo, 0.0, prod))
                dah = da if hh == 0 else pltpu.roll(da, HEAD_DIM, 1)
                doe_ref[h] = jnp.where(lo, dah, jnp.where(lane == HEAD_DIM, delta, jnp.where(lane == HEAD_DIM + 1, lse, 0.0)))

        dpr_ref[...] = jnp.concatenate([d_ab, d_gate[0], d_gate[1], d_cu, d_cv, d_gate[2], d_gate[3]], axis=1).astype(BF16)

        @pl.when(i == n_tiles - 1)
        def _():
            acc = dbsm[...]
            lane128 = _lane(CHUNK)
            out = jnp.zeros((CHUNK, LANES), F32)
            for g in range(N_SPATIAL_GROUPS):
                col = _rowsum(jnp.where(grp[0:CHUNK] == g, acc, 0.0))
                out = out + jnp.where(lane128 == g, col, 0.0)
            dbs_ref[...] = out

    return _pc(
        body, name="mix_out_bwd", grid=(n_tiles,),
        out_shape=(jax.ShapeDtypeStruct((t, W_R), BF16),
                   jax.ShapeDtypeStruct((t, GROUP_W), F32), jax.ShapeDtypeStruct((t, GROUP_W), F32),
                   jax.ShapeDtypeStruct((N_Q_HEADS, t, LANES), F32),
                   jax.ShapeDtypeStruct((D_MODEL, D_MODEL), F32),
                   jax.ShapeDtypeStruct((8, D_MODEL), F32),
                   jax.ShapeDtypeStruct((8, GROUP_W), F32),
                   jax.ShapeDtypeStruct((N_SPATIAL_GROUPS * CHUNK, CHUNK), F32),
                   jax.ShapeDtypeStruct((CHUNK, LANES), F32)),
        in_specs=[_rows(D_MODEL), _rows(D_MODEL), _rows(W_R), _rows(GROUP_W), _rows(GROUP_W), _heads(N_Q_HEADS, LANES),
                  _full((8, D_MODEL)), _full((1, D_MODEL)), _full((D_MODEL, D_MODEL)), _full((8, GROUP_W)),
                  _full((N_SPATIAL_GROUPS * CHUNK, CHUNK)), _full((N_SPATIAL_GROUPS * CHUNK, CHUNK)), _full((CHUNK, GROUP_W))],
        out_specs=(_rows(W_R), _rows(GROUP_W), _rows(GROUP_W), _heads(N_Q_HEADS, LANES),
                   _full((D_MODEL, D_MODEL)), _full((8, D_MODEL)), _full((8, GROUP_W)),
                   _full((N_SPATIAL_GROUPS * CHUNK, CHUNK)), _full((CHUNK, LANES))),
        scratch_shapes=[pltpu.VMEM((CHUNK, GROUP_W), F32)],
        compiler_params=_cparams(),
    )(dxo, y, pr, ca, z2, oe, modv, g_post, w_out, vecs, wss, wsts, bsm)


def _attention_bwd(q, k, v, doe, slabs, conv):
    t = q.shape[1]
    tk = _kv_chunk(t)
    n_chunks = (t - CTX_LEN) // tk
    n_tiles = t // TM
    n_sl = len(slabs)
    rpt = TM // n_chunks
    pc, g_a, g_b, conv_a, conv_b = conv

    def body(q_ref, do_ref, k_ref, v_ref, pc_ref, pp_ref, pn_ref, ga_ref, gap_ref, gan_ref, gb_ref, gbp_ref, gbn_ref,
             cva_ref, cvb_ref, *rest):
        i = pl.program_id(0)
        dq_ref, dk_hbm, dv_hbm, dpc_ref, dca_ref, dcb_ref = rest[n_sl:n_sl + 6]
        dk_acc, dv_acc, uext, zext, gaext, gbext = rest[2 * n_sl + 6:2 * n_sl + 12]
        pairs = tuple(zip(rest[:n_sl], rest[n_sl + 6:2 * n_sl + 6]))
        sems = rest[2 * n_sl + 12:]

        @pl.when(i == 0)
        def _():
            dk_acc[...] = jnp.zeros_like(dk_acc)
            dv_acc[...] = jnp.zeros_like(dv_acc)
            dca_ref[...] = jnp.zeros_like(dca_ref)
            dcb_ref[...] = jnp.zeros_like(dcb_ref)
            _scatter_phase(0, pairs, *sems)

        pv, nv = _halo_valid(i, n_tiles)
        pc_ = pc_ref[...]
        u, z = _conv_inputs(pc_)
        up, zp = _conv_inputs(pp_ref[...])
        un, zn_ = _conv_inputs(pn_ref[...])
        _fill_ext(uext, up * pv, u, un * nv)
        _fill_ext(zext, zp * pv, z, zn_ * nv)
        _fill_ext(gaext, gap_ref[...] * pv, ga_ref[...], gan_ref[...] * nv)
        _fill_ext(gbext, gbp_ref[...] * pv, gb_ref[...], gbn_ref[...] * nv)

        def conv_rows(r0):
            dz = None
            for b in range(8):
                part = None
                for a in range(4):
                    o = 8 * a + b
                    if 1 <= o <= CONFORMER_K:
                        term = cvb_ref[CONFORMER_K - o:CONFORMER_K - o + 1, :] * gbext[pl.ds(r0 + 8 * a, rpt + 8), :]
                        part = term if part is None else part + term
                part = part[b:b + rpt]
                dz = part if dz is None else dz + part
            ga_rows = ga_ref[pl.ds(r0, rpt), :]
            ga_win = gaext[pl.ds(r0 + HALO // 2, rpt + HALO), :]
            u_win = uext[pl.ds(r0 + HALO // 2, rpt + HALO), :]
            du = None
            for kk in range(SHORT_CONV_K):
                term = cva_ref[kk:kk + 1, :] * ga_win[HALO // 2 + 1 - kk:HALO // 2 + 1 - kk + rpt]
                du = term if du is None else du + term
                dca_ref[kk:kk + 1, :] += _colsum(ga_rows * u_win[HALO // 2 - 1 + kk:HALO // 2 - 1 + kk + rpt])
            pc_rows = pc_ref[pl.ds(r0, rpt), :]
            a_c, a_h = pc_rows[:, 0:GROUP_W], pc_rows[:, GROUP_W:2 * GROUP_W]
            glu_a, glu_g = pc_rows[:, 2 * GROUP_W:3 * GROUP_W], pc_rows[:, 3 * GROUP_W:4 * GROUP_W]
            sg = _sigmoid(glu_g)
            dpc_ref[pl.ds(r0, rpt), :] = jnp.concatenate(
                [du * a_h, du * a_c, dz * sg, dz * glu_a * sg * (1.0 - sg)], axis=1).astype(BF16)
            gb_rows = gb_ref[pl.ds(r0, rpt), :]
            window = zext[pl.ds(r0, rpt + 2 * HALO), :]
            for b in range(8):
                zb = window[b:b + rpt + 24]
                for a in range(4):
                    kk = 8 * a + b - 1
                    if 0 <= kk < CONFORMER_K:
                        dcb_ref[kk:kk + 1, :] += _colsum(gb_rows * zb[8 * a:8 * a + rpt])

        lane = _lane(GQA * TM)
        lo = lane < HEAD_DIM
        qs, dos, deltas, lses = [], [], [], []
        for g in range(N_KV_HEADS):
            qs.append(jnp.concatenate([q_ref[GQA * g + hh] for hh in range(GQA)], axis=0))
            dog = jnp.concatenate([do_ref[GQA * g + hh] for hh in range(GQA)], axis=0)
            deltas.append(_rowsum(jnp.where(lane == HEAD_DIM, dog, 0.0)))
            lses.append(_rowsum(jnp.where(lane == HEAD_DIM + 1, dog, 0.0)))
            dos.append(jnp.where(lo, dog, 0.0).astype(BF16))

        def step(st, size, dqs):
            out = []
            for g in range(N_KV_HEADS):
                kc = k_ref[g, pl.ds(st, size), :]
                vc = v_ref[g, pl.ds(st, size), :]
                p = jnp.exp(_dot_nt(qs[g], kc) - lses[g])
                ds_ = (p * (_dot_nt(dos[g], vc) - deltas[g])).astype(BF16)
                dk_acc[g, pl.ds(st, size), :] += _dot_tn(ds_, qs[g])
                dv_acc[g, pl.ds(st, size), :] += _dot_tn(p.astype(BF16), dos[g])
                out.append(dqs[g] + _dot(ds_, kc))
            return tuple(out)

        zero = tuple(jnp.zeros((GQA * TM, LANES), F32) for _ in range(N_KV_HEADS))

        def finish(dqs):
            for g in range(N_KV_HEADS):
                for hh in range(GQA):
                    dq_ref[GQA * g + hh] = dqs[g][hh * TM:(hh + 1) * TM]

        @pl.when(i < N_CTX_TILES)
        def _():
            finish(step(0, CTX_LEN, zero))
            for r in range(n_chunks):
                conv_rows(r * rpt)

        @pl.when(i >= N_CTX_TILES)
        def _():
            def trip(j, acc):
                conv_rows(pl.multiple_of(j * rpt, rpt))
                return step(pl.multiple_of(CTX_LEN + j * tk, 256), tk, acc)

            finish(lax.fori_loop(0, n_chunks, trip, step(0, CTX_LEN, zero)))

        @pl.when(i == n_tiles - 1)
        def _():
            pltpu.sync_copy(dk_acc, dk_hbm)
            pltpu.sync_copy(dv_acc, dv_hbm)
            _scatter_phase(1, pairs, *sems)

    kv_shape = jax.ShapeDtypeStruct((N_KV_HEADS, t, LANES), F32)
    hbm = pl.BlockSpec(memory_space=pl.ANY)
    pc_prev, pc_next = _halo_specs(W_C, t)
    g_prev, g_next = _halo_specs(GROUP_W, t)
    ext = pltpu.VMEM((TM + 2 * HALO, GROUP_W), F32)
    outs = _pc(
        body, name="attention_bwd_scatter", grid=(n_tiles,),
        out_shape=(jax.ShapeDtypeStruct((N_Q_HEADS, t, LANES), F32), kv_shape, kv_shape,
                   jax.ShapeDtypeStruct((t, W_C), BF16), jax.ShapeDtypeStruct((8, GROUP_W), F32),
                   jax.ShapeDtypeStruct((32, GROUP_W), F32)) + tuple(jax.ShapeDtypeStruct(a.shape, a.dtype) for a in slabs),
        in_specs=[_heads(N_Q_HEADS, LANES), _heads(N_Q_HEADS, LANES),
                  _const((N_KV_HEADS, t, LANES)), _const((N_KV_HEADS, t, LANES)),
                  _rows(W_C), pc_prev, pc_next, _rows(GROUP_W), g_prev, g_next, _rows(GROUP_W), g_prev, g_next,
                  _const((8, GROUP_W)), _const((32, GROUP_W))] + [hbm] * n_sl,
        out_specs=(_heads(N_Q_HEADS, LANES), hbm, hbm, _rows(W_C), _full((8, GROUP_W)), _full((32, GROUP_W))) + (hbm,) * n_sl,
        scratch_shapes=[pltpu.VMEM((N_KV_HEADS, t, LANES), F32), pltpu.VMEM((N_KV_HEADS, t, LANES), F32),
                        ext, ext, ext, ext] + _comm_scratch(SCATTER_SEMS, n_sl),
        compiler_params=_cparams(),
    )(q, doe, k, v, pc, pc, pc, g_a, g_a, g_a, g_b, g_b, g_b, conv_a, conv_b, *slabs)
    return outs[0], outs[1], outs[2], outs[3:6], tuple(outs[6:])


def _in_proj_bwd(dpc, dpr, dq, dk, dv, pq, qk_gain, cos_t, sin_t, w_c, w_r, w_q, xt, dxo, modv, g_pre):
    t = xt.shape[0]

    def body(dpc_ref, dpr_ref, dq_ref, dk_ref, dv_ref, pq_ref, gain_ref, cos_ref, sin_ref, wc_ref, wr_ref, wq_ref,
             x_ref, dxo_ref, mod_ref, g_ref, dx_ref, dpq_ref, acc_ref, dgain_ref):
        i = pl.program_id(0)

        @pl.when(i == 0)
        def _():
            acc_ref[...] = jnp.zeros_like(acc_ref)
            dgain_ref[...] = jnp.zeros_like(dgain_ref)

        lane = _lane(TM)
        lo = lane < HEAD_DIM
        lo16 = (lane & 31) < 16
        g = g_ref[...]
        for jj in range(SUB):
            rows = pl.ds(jj * TM, TM)
            is_ctx = i * SUB + jj < N_CTX_TILES
            cos = cos_ref[rows, :]
            sin = sin_ref[rows, :]
            outs = []
            for b in range(3):
                src = dq_ref if b < 2 else dk_ref
                base = 2 * b if b < 2 else 0
                drot = src[base, rows, :] + pltpu.roll(src[base + 1, rows, :], HEAD_DIM, 1)
                if b < 2:
                    drot = drot * ATTN_SCALE
                dxg = drot * cos + _swap16(drot * sin, lo16)
                xh, r = _head_norm(pq_ref[rows, b * LANES:(b + 1) * LANES], lo)
                row = 0 if b < 2 else 1
                dgain_ref[row:row + 1, :] += _colsum(dxg * xh)
                dxh = dxg * gain_ref[row:row + 1, :]
                outs.append(r * (dxh - xh * (_pair_sums(dxh * xh, lo) * (1.0 / HEAD_DIM))))
            outs.append(dv_ref[0, rows, :] + pltpu.roll(dv_ref[1, rows, :], HEAD_DIM, 1))
            dpq = jnp.concatenate(outs, axis=1).astype(BF16)
            dpq_ref[rows, :] = dpq

            dh = _dot_nt(dpc_ref[rows, :], wc_ref[...]) + _dot_nt(dpr_ref[rows, :], wr_ref[...]) + _dot_nt(dpq, wq_ref[...])
            x = x_ref[rows, :]
            r = lax.rsqrt(jnp.mean(x * x, axis=1, keepdims=True) + RMS_EPS)
            xn = x * r
            sc = jnp.where(is_ctx, mod_ref[1:2, :], mod_ref[4:5, :])
            dsh = _colsum(dh)
            dsc = _colsum(dh * (xn * g))
            acc_ref[0:1, :] += jnp.where(is_ctx, dsh, 0.0)
            acc_ref[1:2, :] += jnp.where(is_ctx, dsc, 0.0)
            acc_ref[2:3, :] += jnp.where(is_ctx, 0.0, dsh)
            acc_ref[3:4, :] += jnp.where(is_ctx, 0.0, dsc)
            dxg = dh * (1.0 + sc)
            acc_ref[4:5, :] += _colsum(dxg * xn)
            dxn = dxg * g
            dx_ref[rows, :] = r * (dxn - xn * jnp.mean(dxn * xn, axis=1, keepdims=True)) + dxo_ref[rows, :]

    return _pc(
        body, name="in_proj_bwd", grid=(t // TB,),
        out_shape=(jax.ShapeDtypeStruct((t, D_MODEL), F32), jax.ShapeDtypeStruct((t, W_Q), BF16),
                   jax.ShapeDtypeStruct((8, D_MODEL), F32), jax.ShapeDtypeStruct((8, LANES), F32)),
        in_specs=[_rows(W_C, TB), _rows(W_R, TB),
                  _heads(N_Q_HEADS, LANES, TB), _heads(N_KV_HEADS, LANES, TB), _heads(N_KV_HEADS, LANES, TB), _rows(W_Q, TB),
                  _const((8, LANES)), _rows(LANES, TB), _rows(LANES, TB),
                  _const((D_MODEL, W_C)), _const((D_MODEL, W_R)), _const((D_MODEL, W_Q)),
                  _rows(D_MODEL, TB), _rows(D_MODEL, TB), _const((8, D_MODEL)), _const((1, D_MODEL))],
        out_specs=(_rows(D_MODEL, TB), _rows(W_Q, TB), _full((8, D_MODEL)), _full((8, LANES))),
        compiler_params=_cparams(),
    )(dpc, dpr, dq, dk, dv, pq, qk_gain, cos_t, sin_t, w_c, w_r, w_q, xt, dxo, modv, g_pre)


def _in_proj_wgrad(h, dpc, dpr, dpq):
    t = h.shape[0]

    def body(h_ref, dpc_ref, dpr_ref, dpq_ref, gc_ref, gr_ref, gq_ref):
        @pl.when(pl.program_id(0) == 0)
        def _():
            gc_ref[...] = jnp.zeros_like(gc_ref)
            gr_ref[...] = jnp.zeros_like(gr_ref)
            gq_ref[...] = jnp.zeros_like(gq_ref)

        hb = h_ref[...]
        gc_ref[...] += _dot_tn(hb, dpc_ref[...])
        gr_ref[...] += _dot_tn(hb, dpr_ref[...])
        gq_ref[...] += _dot_tn(hb, dpq_ref[...])

    return _pc(
        body, name="in_proj_wgrad", grid=(t // TB,),
        out_shape=(jax.ShapeDtypeStruct((D_MODEL, W_C), F32), jax.ShapeDtypeStruct((D_MODEL, W_R), F32),
                   jax.ShapeDtypeStruct((D_MODEL, W_Q), F32)),
        in_specs=[_rows(D_MODEL, TB), _rows(W_C, TB), _rows(W_R, TB), _rows(W_Q, TB)],
        out_specs=(_full((D_MODEL, W_C)), _full((D_MODEL, W_R)), _full((D_MODEL, W_Q))),
        compiler_params=_cparams(),
    )(h, dpc, dpr, dpq)


def _sum_slabs(slabs, tile_rows):
    n, r, c = slabs.shape

    def body(s_ref, o_ref):
        acc = s_ref[0].astype(F32)
        for k in range(1, n):
            acc = acc + s_ref[k].astype(F32)
        o_ref[...] = acc

    return _pc(
        body, name="sum_slabs", grid=(r // tile_rows,),
        out_shape=jax.ShapeDtypeStruct((r, c), F32),
        in_specs=[pl.BlockSpec((n, tile_rows, c), lambda i: (0, i, 0))],
        out_specs=pl.BlockSpec((tile_rows, c), lambda i: (i, 0)),
        compiler_params=_cparams(),
    )(slabs)


def _sum_layer_slabs(layers, tile_rows):
    nl = len(layers)
    n, r, c = layers[0].shape
    per = r // tile_rows

    def body(*refs):
        o_ref = refs[nl]
        for l in range(nl):
            @pl.when(pl.program_id(0) // per == l)
            def _(l=l):
                acc = refs[l][0].astype(F32)
                for k in range(1, n):
                    acc = acc + refs[l][k].astype(F32)
                o_ref[...] = acc

    def spec(l):
        return pl.BlockSpec((n, tile_rows, c), lambda i: (0, jnp.clip(i - l * per, 0, per - 1), 0))

    return _pc(
        body, name="sum_layer_slabs", grid=(nl * per,),
        out_shape=jax.ShapeDtypeStruct((nl * r, c), F32),
        in_specs=[spec(l) for l in range(nl)],
        out_specs=pl.BlockSpec((tile_rows, c), lambda i: (i, 0)),
        compiler_params=_cparams(),
    )(*layers)


def _adamw(grads, w, m, v, tile_rows):
    r, c = w.shape
    n_g = len(grads)

    def body(*refs):
        g = refs[0][...]
        for k in range(1, n_g):
            g = g + refs[k][...]
        w_ref, m_ref, v_ref, g_out, d_out, m_out, v_out = refs[n_g:]
        m_new = ADAM_B1 * m_ref[...] + (1.0 - ADAM_B1) * g
        v_new = ADAM_B2 * v_ref[...] + (1.0 - ADAM_B2) * (g * g)
        m_hat = m_new / (1.0 - ADAM_B1 ** ADAM_STEP)
        v_hat = v_new / (1.0 - ADAM_B2 ** ADAM_STEP)
        g_out[...] = g
        d_out[...] = -ADAM_LR * (m_hat / (jnp.sqrt(v_hat) + ADAM_EPS) + ADAM_WD * w_ref[...])
        m_out[...] = m_new
        v_out[...] = v_new

    spec = pl.BlockSpec((tile_rows, c), lambda i: (i, 0))
    shape = jax.ShapeDtypeStruct((r, c), F32)
    return _pc(
        body, name="adamw", grid=(r // tile_rows,),
        out_shape=(shape,) * 4, in_specs=[spec] * (n_g + 3), out_specs=(spec,) * 4,
        compiler_params=_cparams(),
    )(*grads, w, m, v)


def _rope_tables(s_lat):
    n_rows = s_lat // GRID_W
    axis_dim = HEAD_DIM // 2
    inv_freq = 1.0 / (ROPE_THETA ** (jnp.arange(0, axis_dim, 2, dtype=F32) / axis_dim))
    d = np.arange(LANES) % HEAD_DIM
    on_rows = (d // axis_dim) == 0
    freq = d % (axis_dim // 2)
    sign = np.where((d % axis_dim) < axis_dim // 2, -1.0, 1.0).astype(np.float32)
    ang_r = jnp.arange(n_rows, dtype=F32)[:, None] * inv_freq[freq][None, :]
    ang_c = jnp.arange(GRID_W, dtype=F32)[:, None] * inv_freq[freq][None, :]

    def spread(fn):
        full = jnp.where(on_rows[None, None, :], fn(ang_r)[:, None, :], fn(ang_c)[None, :, :])
        return full.reshape(s_lat, LANES)

    cos = jnp.concatenate([jnp.ones((CTX_LEN, LANES), F32), spread(jnp.cos)], axis=0)
    sin = jnp.concatenate([jnp.zeros((CTX_LEN, LANES), F32), spread(jnp.sin) * sign[None, :]], axis=0)
    return cos, sin


def _pad_rows(a, rows):
    return jnp.concatenate([a, jnp.zeros((rows - a.shape[0],) + a.shape[1:], a.dtype)], axis=0)


_SMALL = ("c_ctx", "b_mod", "g_pre", "g_post", "conv_a", "conv_b", "conv_b_bias", "conf_ln_g", "conf_ln_b",
          "sgu_ln_g", "sgu_ln_b", "w_s", "b_s", "q_gain", "k_gain")


def _pack(arrays):
    flat = jnp.concatenate([a.reshape(-1) for a in arrays])
    rows = -(-flat.shape[0] // (16 * LANES)) * 16
    return _pad_rows(flat.reshape(-1, 1), rows * LANES).reshape(rows, LANES)


def _unpack(packed, shapes):
    flat = packed.reshape(-1)
    out, off = [], 0
    for s in shapes:
        n = int(np.prod(s))
        out.append(flat[off:off + n].reshape(s))
        off += n
    return out


def kernel(x, c, ctx, c_ctx, w_mod, b_mod, g_pre, g_post, w_in, w_out, conv_a, conv_b, conv_b_bias, conf_ln_g, conf_ln_b, sgu_ln_g, sgu_ln_b, w_s, b_s, q_gain, k_gain, loss_target, m_c_ctx, m_w_mod, m_b_mod, m_g_pre, m_g_post, m_w_in, m_w_out, m_conv_a, m_conv_b, m_conv_b_bias, m_conf_ln_g, m_conf_ln_b, m_sgu_ln_g, m_sgu_ln_b, m_w_s, m_b_s, m_q_gain, m_k_gain, v_c_ctx, v_w_mod, v_b_mod, v_g_pre, v_g_post, v_w_in, v_w_out, v_conv_a, v_conv_b, v_conv_b_bias, v_conf_ln_g, v_conf_ln_b, v_sgu_ln_g, v_sgu_ln_b, v_w_s, v_b_s, v_q_gain, v_k_gain):
    weights = dict(c_ctx=c_ctx, w_mod=w_mod, b_mod=b_mod, g_pre=g_pre, g_post=g_post, w_in=w_in, w_out=w_out, conv_a=conv_a,
                   conv_b=conv_b, conv_b_bias=conv_b_bias, conf_ln_g=conf_ln_g, conf_ln_b=conf_ln_b, sgu_ln_g=sgu_ln_g,
                   sgu_ln_b=sgu_ln_b, w_s=w_s, b_s=b_s, q_gain=q_gain, k_gain=k_gain)
    m_in = dict(c_ctx=m_c_ctx, w_mod=m_w_mod, b_mod=m_b_mod, g_pre=m_g_pre, g_post=m_g_post, w_in=m_w_in, w_out=m_w_out,
                conv_a=m_conv_a, conv_b=m_conv_b, conv_b_bias=m_conv_b_bias, conf_ln_g=m_conf_ln_g, conf_ln_b=m_conf_ln_b,
                sgu_ln_g=m_sgu_ln_g, sgu_ln_b=m_sgu_ln_b, w_s=m_w_s, b_s=m_b_s, q_gain=m_q_gain, k_gain=m_k_gain)
    v_in = dict(c_ctx=v_c_ctx, w_mod=v_w_mod, b_mod=v_b_mod, g_pre=v_g_pre, g_post=v_g_post, w_in=v_w_in, w_out=v_w_out,
                conv_a=v_conv_a, conv_b=v_conv_b, conv_b_bias=v_conv_b_bias, conf_ln_g=v_conf_ln_g, conf_ln_b=v_conf_ln_b,
                sgu_ln_g=v_sgu_ln_g, sgu_ln_b=v_sgu_ln_b, w_s=v_w_s, b_s=v_b_s, q_gain=v_q_gain, k_gain=v_k_gain)
    order = ("c_ctx", "w_mod", "b_mod", "g_pre", "g_post", "w_in", "w_out", "conv_a", "conv_b", "conv_b_bias", "conf_ln_g",
             "conf_ln_b", "sgu_ln_g", "sgu_ln_b", "w_s", "b_s", "q_gain", "k_gain")

    s_lat = x.shape[1]
    ax, ay, ac = lax.axis_index("x"), lax.axis_index("y"), lax.axis_index("c")
    chip = 2 * ax + ay
    example = 4 * ax + 2 * ay + ac

    c_rows = _all_gather_rows(_pad_rows(c, 8))[::8]
    c16 = _pad_rows(jnp.concatenate([c_rows, c_ctx[None, :]], axis=0), 16)
    b_mod_shard = lax.dynamic_slice_in_dim(b_mod, chip * SHARD_MOD, SHARD_MOD, axis=1)[:, None, :]
    silu_c, mod_shard = _mod_forward(c16, w_mod, b_mod_shard)
    mod_all = _all_gather_rows(mod_shard.reshape(DEPTH * 16, SHARD_MOD)).reshape(8, DEPTH, 16, SHARD_MOD)
    mod_full = jnp.transpose(mod_all[::2], (1, 2, 0, 3)).reshape(DEPTH, 16, 3 * D_MODEL)
    mod_lat = lax.dynamic_index_in_dim(mod_full, example, axis=1, keepdims=False).reshape(DEPTH, 3, D_MODEL)
    mod_ctx = mod_full[:, 8].reshape(DEPTH, 3, D_MODEL)
    modv = jnp.concatenate([mod_ctx, mod_lat, jnp.zeros((DEPTH, 2, D_MODEL), F32)], axis=1)

    wi_b, wo_b = w_in.astype(BF16), w_out.astype(BF16)

    def regroup(wi_all):
        wi_full = jnp.concatenate([wi_all[k] for k in range(N_CHIPS)], axis=-1)
        wc_l = jnp.concatenate([wi_full[:, 256:768], wi_full[:, 1024:1536]], axis=-1)
        wr_l = jnp.concatenate([wi_full[:, 0:256], wi_full[:, 768:1024], wi_full[:, 1536:2560], wi_full[:, 3072:3328]], axis=-1)
        return wc_l, wr_l, wi_full[:, 2560:3072]

    w_c, w_r, w_q, wo_full = [None] * DEPTH, [None] * DEPTH, [None] * DEPTH, [None] * DEPTH
    w_c[0], w_r[0], w_q[0] = regroup(_gather_weights((wi_b[0],))[0])

    cos_t, sin_t = _rope_tables(s_lat)
    conv_a_full = jnp.zeros((DEPTH, 8, GROUP_W), F32)
    conv_b_full = jnp.zeros((DEPTH, 32, GROUP_W), F32)
    conv_small = jnp.concatenate([conv_a.reshape(DEPTH * SHORT_CONV_K, -1), conv_b.reshape(DEPTH * CONFORMER_K, -1)], axis=0)
    n_cs = conv_small.shape[0]
    conv_rows = -(-n_cs // 8) * 8
    conv_all = _all_gather_rows(_pad_rows(conv_small, conv_rows)).reshape(8, conv_rows, -1)[::2]
    conv_all = jnp.transpose(conv_all, (1, 0, 2)).reshape(conv_rows, GROUP_W)
    conv_a_full = conv_a_full.at[:, :SHORT_CONV_K].set(conv_all[:DEPTH * SHORT_CONV_K].reshape(DEPTH, SHORT_CONV_K, GROUP_W))
    conv_b_full = conv_b_full.at[:, :CONFORMER_K].set(
        conv_all[DEPTH * SHORT_CONV_K:n_cs].reshape(DEPTH, CONFORMER_K, GROUP_W))

    vecs = jnp.stack([conv_b_bias, conf_ln_g, conf_ln_b, sgu_ln_g, sgu_ln_b] + [jnp.zeros_like(conv_b_bias)] * 3, axis=1)
    wss = w_s.reshape(DEPTH, N_SPATIAL_GROUPS * CHUNK, CHUNK).astype(BF16)
    wsts = jnp.swapaxes(w_s, 2, 3).reshape(DEPTH, N_SPATIAL_GROUPS * CHUNK, CHUNK).astype(BF16)
    bsm = jnp.repeat(jnp.swapaxes(b_s, 1, 2), HEAD_DIM, axis=2)
    qk_gain = jnp.concatenate([jnp.tile(q_gain, (1, 2))[:, None, :], jnp.tile(k_gain, (1, 2))[:, None, :],
                               jnp.zeros((DEPTH, 6, LANES), F32)], axis=1)

    xt = jnp.concatenate([ctx[0], x[0]], axis=0)
    saved = []
    for l in range(DEPTH):
        h, pc, pr, pq, q, k, v = _in_proj(xt, modv[l], g_pre[l][None, :], w_c[l], w_r[l], w_q[l], qk_gain[l], cos_t, sin_t)
        oe, gathered = _attention_fwd(q, k, v, (wo_b[l],) + ((wi_b[l + 1],) if l + 1 < DEPTH else ()))
        wo_full[l] = jnp.concatenate([gathered[0][k] for k in range(N_CHIPS)], axis=0)
        if l + 1 < DEPTH:
            w_c[l + 1], w_r[l + 1], w_q[l + 1] = regroup(gathered[1])
        mixed = _mix_out(pc, pr, oe, xt, modv[l], g_post[l][None, :], wo_full[l], conv_a_full[l], conv_b_full[l],
                         vecs[l], wss[l], bsm[l], loss_target[0] if l + 1 == DEPTH else None)
        x_new, y, ca, z2 = mixed[:4]
        saved.append(dict(x=xt, h=h, pc=pc, pr=pr, pq=pq, q=q, k=k, v=v, oe=oe, y=y, ca=ca, z2=z2))
        xt = x_new
    dxo = xt
    loss = lax.psum(mixed[4][0, 0], ("x", "y", "c"))

    g_small = {n: [None] * DEPTH for n in _SMALL}
    d_mod, landed_in, landed_out = [None] * DEPTH, [None] * DEPTH, [None] * DEPTH
    slab_in = None
    for l in reversed(range(DEPTH)):
        s = saved[l]
        dpr, g_a, g_b, doe, gw_o, pvec, s256, dws, dbs = _mix_out_bwd(
            dxo, s["y"], s["pr"], s["ca"], s["z2"], s["oe"], modv[l], g_post[l][None, :], wo_full[l], vecs[l], wss[l], wsts[l], bsm[l])
        slab_out = gw_o.reshape(N_CHIPS, SHARD_OUT, D_MODEL).astype(BF16)
        dq, dk, dv, (dpc, dca, dcb), got = _attention_bwd(
            s["q"], s["k"], s["v"], doe, (slab_out,) + (() if slab_in is None else (slab_in,)),
            (s["pc"], g_a, g_b, conv_a_full[l], conv_b_full[l]))
        landed_out[l] = got[0]
        if slab_in is not None:
            landed_in[l + 1] = got[1]
        dxo, dpq, acc, dgain = _in_proj_bwd(dpc, dpr, dq, dk, dv, s["pq"], qk_gain[l], cos_t, sin_t, w_c[l], w_r[l], w_q[l],
                                            s["x"], dxo, modv[l], g_pre[l][None, :])
        gw_c, gw_r, gw_q = _in_proj_wgrad(s["h"], dpc, dpr, dpq)
        gw_in = jnp.concatenate([gw_r[:, 0:256], gw_c[:, 0:512], gw_r[:, 256:512], gw_c[:, 512:1024],
                                 gw_r[:, 512:1536], gw_q, gw_r[:, 1536:1792]], axis=-1)
        slab_in = jnp.transpose(gw_in.reshape(D_MODEL, N_CHIPS, SHARD_IN), (1, 0, 2)).astype(BF16)
        d_mod[l] = jnp.stack([jnp.concatenate([acc[2], acc[3], pvec[1]]), jnp.concatenate([acc[0], acc[1], pvec[0]])])
        g_small["g_pre"][l] = acc[4]
        g_small["g_post"][l] = pvec[2]
        g_small["conv_a"][l] = dca[:SHORT_CONV_K]
        g_small["conv_b"][l] = dcb[:CONFORMER_K]
        g_small["conv_b_bias"][l] = s256[0]
        g_small["conf_ln_g"][l] = s256[1]
        g_small["conf_ln_b"][l] = s256[2]
        g_small["sgu_ln_g"][l] = s256[3]
        g_small["sgu_ln_b"][l] = s256[4]
        g_small["w_s"][l] = dws.reshape(N_SPATIAL_GROUPS, CHUNK, CHUNK)
        g_small["b_s"][l] = jnp.transpose(dbs[:, :N_SPATIAL_GROUPS])
        g_small["q_gain"][l] = dgain[0, :HEAD_DIM] + dgain[0, HEAD_DIM:]
        g_small["k_gain"][l] = dgain[1, :HEAD_DIM] + dgain[1, HEAD_DIM:]
    grad_x = dxo[CTX_LEN:][None]

    d_mod_all = _all_gather_rows(jnp.stack(d_mod).reshape(DEPTH * 2, 3 * D_MODEL)).reshape(8, DEPTH, 2, 3 * D_MODEL)
    d_lat = jnp.transpose(d_mod_all[:, :, 0], (1, 0, 2))
    d_ctx = jnp.transpose(d_mod_all[:, :, 1], (1, 0, 2))
    cols = lambda a: lax.dynamic_slice_in_dim(a.reshape(DEPTH, 8, N_CHIPS, SHARD_MOD), chip, 1, axis=2)[:, :, 0]
    silu_t = jnp.transpose(silu_c)
    s_t = jnp.concatenate([silu_t[:, 0:8], jnp.tile(silu_t[:, 8:9], (1, 8)), jnp.zeros((D_MODEL, LANES - 16), F32)], axis=1)
    g_rows = jnp.concatenate([cols(d_lat), cols(d_ctx), jnp.zeros((DEPTH, LANES - 16, SHARD_MOD), F32)], axis=1)
    g_w_mod, g_b_mod, c_ctx_part = _mod_backward(s_t, g_rows, cols(d_ctx), jnp.concatenate([d_lat, d_ctx], axis=1),
                                                 w_mod, c_ctx[:, None])

    for n in _SMALL:
        if n not in ("c_ctx", "b_mod"):
            g_small[n] = jnp.stack(g_small[n])
    small_parts = [0.5 * c_ctx_part[:, 0]] + [g_small[n] for n in _SMALL[2:]]
    packed = _pack(small_parts)
    gathered = _all_gather_rows(packed.astype(BF16)).reshape(8, packed.shape[0], LANES)
    small_sum = _sum_slabs(gathered, packed.shape[0])
    small_g = dict(zip(("c_ctx",) + _SMALL[2:], _unpack(small_sum, [p.shape for p in small_parts])))
    small_g["b_mod"] = g_b_mod[:, 0]
    ch64 = GROUP_W // N_CHIPS
    for n in ("conv_a", "conv_b"):
        small_g[n] = lax.dynamic_slice_in_dim(small_g[n], chip * ch64, ch64, axis=2)
    sw = _pack([weights[n] for n in _SMALL])
    sm = _pack([m_in[n] for n in _SMALL])
    sv = _pack([v_in[n] for n in _SMALL])
    sg = _pack([small_g[n] for n in _SMALL])
    shapes = [weights[n].shape for n in _SMALL]
    small_out = [dict(zip(_SMALL, _unpack(o, shapes))) for o in _adamw([sg], sw, sm, sv, sg.shape[0])]

    landed_in[0] = _scatter_slabs((slab_in,))[0]
    sum_in = _sum_layer_slabs(landed_in, 512)
    sum_out = _sum_layer_slabs(landed_out, 256)
    sib_in, sib_out = _swap_with_sibling(sum_in, sum_out)

    big = {}
    flat = lambda a: a.reshape(-1, a.shape[-1])
    for n, grads, rows in (("w_in", [sum_in, sib_in], 512), ("w_out", [sum_out, sib_out], 256), ("w_mod", [flat(g_w_mod)], 512)):
        outs = _adamw(grads, flat(weights[n]), flat(m_in[n]), flat(v_in[n]), rows)
        big[n] = [o.reshape(weights[n].shape) for o in outs]

    def leaf(n, j):
        return big[n][j] if n in big else small_out[j][n]

    return (loss, grad_x, *[leaf(n, 0) for n in order], *[leaf(n, 1) for n in order],
            *[leaf(n, 2) for n in order], *[leaf(n, 3) for n in order])
```

```python
import functools

import numpy as np
import jax
import jax.numpy as jnp
from jax import lax
from jax.experimental import pallas as pl
from jax.experimental.pallas import tpu as pltpu

F32 = jnp.float32
BF16 = jnp.bfloat16
MESH = pl.DeviceIdType.MESH

D_MODEL = 1024
DEPTH = 4
GRID_W = 64
CTX_LEN = 256
GROUP_W = 256
HEAD_DIM = 64
N_Q_HEADS = 4
N_KV_HEADS = 2
GQA = N_Q_HEADS // N_KV_HEADS
ROPE_THETA = 10000.0
ATTN_SCALE = HEAD_DIM ** -0.5
SHORT_CONV_K = 3
CONFORMER_K = 31
CHUNK = 128
N_SPATIAL_GROUPS = 4
RMS_EPS = 1e-6
LN_EPS = 1e-5
ADAM_LR = 0.001
ADAM_B1 = 0.9
ADAM_B2 = 0.999
ADAM_EPS = 1e-08
ADAM_WD = 0.01
ADAM_STEP = 10

LANES = 128
HALO = 16
CONV_ROWS = 64
TM = 256
N_CTX_TILES = CTX_LEN // TM
SUB = 3
TB = SUB * TM
W_C = 1024
W_R = 1792
W_Q = 512
PROJ_W = W_C + W_R + W_Q
N_CHIPS = 4
SHARD_IN = PROJ_W // N_CHIPS
SHARD_OUT = D_MODEL // N_CHIPS
SHARD_MOD = 3 * D_MODEL // N_CHIPS
VMEM_LIMIT = 56 * 1024 * 1024
VMEM_LIMIT_WIDE = 60 * 1024 * 1024


def _pc(body, **kw):
    return pl.pallas_call(body, **kw)


def _cparams(**kw):
    return pltpu.CompilerParams(dimension_semantics=("arbitrary",), vmem_limit_bytes=VMEM_LIMIT, **kw)


def _full(shape):
    n = len(shape)
    return pl.BlockSpec(shape, lambda i: (0,) * n)


def _const(shape):
    n = len(shape)
    return pl.BlockSpec(shape, lambda i: (0,) * n, pipeline_mode=pl.Buffered(1))


def _rows(width, tm=TM):
    return pl.BlockSpec((tm, width), lambda i: (i, 0))


def _heads(nh, width, tm=TM):
    return pl.BlockSpec((nh, tm, width), lambda i: (0, i, 0))


def _sigmoid(x):
    return jax.nn.sigmoid(x)


def _dot(a, b):
    return jnp.dot(a, b, preferred_element_type=F32)


def _dot_nt(a, b):
    return lax.dot_general(a, b, (((1,), (1,)), ((), ())), preferred_element_type=F32)


def _dot_tn(a, b):
    return lax.dot_general(a, b, (((0,), (0,)), ((), ())), preferred_element_type=F32)


def _lane(rows):
    return lax.broadcasted_iota(jnp.int32, (rows, LANES), 1)


def _rowsum(x):
    return jnp.sum(x, axis=1, keepdims=True)


def _colsum(x):
    return jnp.sum(x, axis=0, keepdims=True)


def _pair_sums(x, lo):
    s0 = _rowsum(jnp.where(lo, x, 0.0))
    s1 = _rowsum(jnp.where(lo, 0.0, x))
    return jnp.where(lo, s0, s1)


def _swap16(x, lo16):
    return jnp.where(lo16, pltpu.roll(x, LANES - 16, 1), pltpu.roll(x, 16, 1))


def _layer_norm_stats(x):
    mu = jnp.mean(x, axis=1, keepdims=True)
    xc = x - mu
    rs = lax.rsqrt(jnp.mean(xc * xc, axis=1, keepdims=True) + LN_EPS)
    return xc * rs, rs


def _layer_norm_bwd(dxn, xn, rs):
    return rs * (dxn - jnp.mean(dxn, axis=1, keepdims=True) - xn * jnp.mean(dxn * xn, axis=1, keepdims=True))


def _group_select(r, grp):
    out = jnp.where(grp == 0, r[0:CHUNK], 0.0)
    for g in range(1, N_SPATIAL_GROUPS):
        out = out + jnp.where(grp == g, r[g * CHUNK:(g + 1) * CHUNK], 0.0)
    return out


def _kv_chunk(t):
    return 1024 if (t - CTX_LEN) % 1024 == 0 else 256


def _all_gather_rows(x_shard):
    m_per, n = x_shard.shape

    def body(x_ref, out_ref, send_sems, recv_sems, local_sem):
        x, y, c = lax.axis_index("x"), lax.axis_index("y"), lax.axis_index("c")
        me, sibling = (x, y, c), (x, y, 1 - c)
        chips = [(1 - x, y), (x, 1 - y), (1 - x, 1 - y)]

        def rows(px, py, pc):
            return out_ref.at[pl.ds((4 * px + 2 * py + pc) * m_per, m_per), :]

        def copy(k, block, to, src=None):
            return pltpu.make_async_remote_copy(
                src_ref=rows(*block) if src is None else src, dst_ref=rows(*block),
                send_sem=send_sems.at[k], recv_sem=recv_sems.at[k], device_id=to, device_id_type=MESH)

        mine = pltpu.make_async_copy(x_ref, rows(*me), local_sem)
        mine.start()
        first = [copy(0, me, sibling, src=x_ref)]
        first += [copy(1 + j, me, (*chip, c), src=x_ref) for j, chip in enumerate(chips)]
        for cp in first:
            cp.start()
        passed = [copy(4 + j, (*chip, c), sibling) for j, chip in enumerate(chips)]
        for j, chip in enumerate(chips):
            copy(1 + j, (*chip, c), me).wait_recv()
            passed[j].start()
        copy(0, sibling, me).wait_recv()
        for j, chip in enumerate(chips):
            copy(4 + j, (*chip, 1 - c), me).wait_recv()
        for cp in first + passed:
            cp.wait_send()
        mine.wait()

    return _pc(
        body, name="all_gather_rows",
        out_shape=jax.ShapeDtypeStruct((8 * m_per, n), x_shard.dtype),
        in_specs=[pl.BlockSpec(memory_space=pltpu.VMEM)],
        out_specs=pl.BlockSpec(memory_space=pltpu.VMEM),
        scratch_shapes=[pltpu.SemaphoreType.DMA((7,)), pltpu.SemaphoreType.DMA((7,)), pltpu.SemaphoreType.DMA],
        compiler_params=pltpu.CompilerParams(vmem_limit_bytes=VMEM_LIMIT),
    )(x_shard)


def _place():
    x, y, c = lax.axis_index("x"), lax.axis_index("y"), lax.axis_index("c")
    return x, y, c, [(1 - x, y), (x, 1 - y), (1 - x, 1 - y)]


def _remote(src, dst, send_sems, recv_sems, k, to):
    return pltpu.make_async_remote_copy(src_ref=src, dst_ref=dst, send_sem=send_sems.at[k], recv_sem=recv_sems.at[k],
                                        device_id=to, device_id_type=MESH)


GATHER_SEMS = 6
SCATTER_SEMS = 3


def _gather_phase(phase, pairs, send_sems, recv_sems, local_sems):
    x, y, c, chips = _place()
    kme = 2 * x + y
    sibling = (x, y, 1 - c)
    for a, (src, dst) in enumerate(pairs):
        half = src.shape[0] // 2
        mine = pl.ds(c * half, half)
        theirs = pl.ds((1 - c) * half, half)
        if phase == 0:
            pltpu.make_async_copy(src, dst.at[kme], local_sems.at[a]).start()
        if phase == 2:
            pltpu.make_async_copy(src, dst.at[kme], local_sems.at[a]).wait()
        for j, (px, py) in enumerate(chips):
            kk = 2 * px + py
            landed = dst.at[kk, mine]
            out = lambda: _remote(src.at[mine], dst.at[kme, mine], send_sems, recv_sems, 6 * a + j, (px, py, c))
            hand = lambda: _remote(landed, landed, send_sems, recv_sems, 6 * a + 3 + j, sibling)
            if phase == 0:
                out().start()
            if phase == 1:
                _remote(landed, landed, send_sems, recv_sems, 6 * a + j, (px, py, c)).wait_recv()
                hand().start()
            if phase == 2:
                other = dst.at[kk, theirs]
                _remote(other, other, send_sems, recv_sems, 6 * a + 3 + j, sibling).wait_recv()
                out().wait_send()
                hand().wait_send()


def _scatter_phase(phase, pairs, send_sems, recv_sems, local_sems):
    x, y, c, chips = _place()
    kme = 2 * x + y
    for a, (src, dst) in enumerate(pairs):
        loc = pltpu.make_async_copy(src.at[kme], dst.at[kme], local_sems.at[a])
        if phase == 0:
            loc.start()
        else:
            loc.wait()
        for j, (px, py) in enumerate(chips):
            kk = 2 * px + py
            out = _remote(src.at[kk], dst.at[kme], send_sems, recv_sems, 3 * a + j, (px, py, c))
            if phase == 0:
                out.start()
            else:
                landed = dst.at[kk]
                _remote(landed, landed, send_sems, recv_sems, 3 * a + j, (px, py, c)).wait_recv()
                out.wait_send()


def _comm_scratch(per_array, n_arrays):
    n = per_array * n_arrays
    return [pltpu.SemaphoreType.DMA((n,)), pltpu.SemaphoreType.DMA((n,)), pltpu.SemaphoreType.DMA((n_arrays,))]


def _slots(a):
    return jax.ShapeDtypeStruct((N_CHIPS,) + a.shape, a.dtype)


def _gather_weights(shards):
    n = len(shards)

    def body(*refs):
        for phase in range(3):
            _gather_phase(phase, tuple(zip(refs[:n], refs[n:2 * n])), *refs[2 * n:])

    hbm = pl.BlockSpec(memory_space=pl.ANY)
    return _pc(
        body, name="gather_weights", out_shape=tuple(_slots(a) for a in shards),
        in_specs=[hbm] * n, out_specs=(hbm,) * n, scratch_shapes=_comm_scratch(GATHER_SEMS, n),
    )(*shards)


def _scatter_slabs(slabs):
    n = len(slabs)

    def body(*refs):
        for phase in range(2):
            _scatter_phase(phase, tuple(zip(refs[:n], refs[n:2 * n])), *refs[2 * n:])

    hbm = pl.BlockSpec(memory_space=pl.ANY)
    return _pc(
        body, name="scatter_slabs", out_shape=tuple(jax.ShapeDtypeStruct(a.shape, a.dtype) for a in slabs),
        in_specs=[hbm] * n, out_specs=(hbm,) * n, scratch_shapes=_comm_scratch(SCATTER_SEMS, n),
    )(*slabs)


def _swap_with_sibling(a, b):
    def body(a_ref, b_ref, ra_ref, rb_ref, send_sems, recv_sems):
        x, y, c = lax.axis_index("x"), lax.axis_index("y"), lax.axis_index("c")
        copies = []
        for k, (src, dst) in enumerate(((a_ref, ra_ref), (b_ref, rb_ref))):
            cp = pltpu.make_async_remote_copy(
                src_ref=src, dst_ref=dst, send_sem=send_sems.at[k], recv_sem=recv_sems.at[k],
                device_id=(x, y, 1 - c), device_id_type=MESH)
            cp.start()
            copies.append(cp)
        for cp in copies:
            cp.wait()

    hbm = pl.BlockSpec(memory_space=pl.ANY)
    return _pc(
        body, name="swap_with_sibling",
        out_shape=(jax.ShapeDtypeStruct(a.shape, a.dtype), jax.ShapeDtypeStruct(b.shape, b.dtype)),
        in_specs=[hbm, hbm], out_specs=(hbm, hbm),
        scratch_shapes=[pltpu.SemaphoreType.DMA((2,)), pltpu.SemaphoreType.DMA((2,))],
    )(a, b)


def _mod_forward(c16, w_mod, b_mod_shard):
    def body(c_ref, w_ref, b_ref, s_ref, o_ref):
        cc = c_ref[...]
        s = cc * _sigmoid(cc)
        s_ref[...] = s
        o_ref[0] = jnp.dot(s, w_ref[0], preferred_element_type=F32, precision=lax.Precision.HIGHEST) + b_ref[0]

    return _pc(
        body, name="mod_forward", grid=(DEPTH,),
        out_shape=(jax.ShapeDtypeStruct((16, D_MODEL), F32), jax.ShapeDtypeStruct((DEPTH, 16, SHARD_MOD), F32)),
        in_specs=[_full((16, D_MODEL)),
                  pl.BlockSpec((1, D_MODEL, SHARD_MOD), lambda l: (l, 0, 0)),
                  pl.BlockSpec((1, 1, SHARD_MOD), lambda l: (l, 0, 0))],
        out_specs=(_full((16, D_MODEL)), pl.BlockSpec((1, 16, SHARD_MOD), lambda l: (l, 0, 0))),
        compiler_params=_cparams(),
    )(c16, w_mod, b_mod_shard)


def _mod_backward(s_t, g_rows, g_ctx, d_all, w_mod, c_ctx_col):
    def body(st_ref, g_ref, gc_ref, d_ref, w_ref, cc_ref, gw_ref, gb_ref, pc_ref):
        l = pl.program_id(0)
        gw_ref[0] = jnp.dot(st_ref[...], g_ref[0], preferred_element_type=F32, precision=lax.Precision.HIGHEST)
        gb_ref[0] = _colsum(d_ref[0])
        part = _rowsum(w_ref[0] * _colsum(gc_ref[0]))

        @pl.when(l == 0)
        def _():
            pc_ref[...] = jnp.zeros_like(pc_ref)

        pc_ref[...] += part

        @pl.when(l == DEPTH - 1)
        def _():
            cc = cc_ref[...]
            sg = _sigmoid(cc)
            pc_ref[...] = pc_ref[...] * (sg * (1.0 + cc * (1.0 - sg)))

    return _pc(
        body, name="mod_backward", grid=(DEPTH,),
        out_shape=(jax.ShapeDtypeStruct((DEPTH, D_MODEL, SHARD_MOD), F32),
                   jax.ShapeDtypeStruct((DEPTH, 1, 3 * D_MODEL), F32),
                   jax.ShapeDtypeStruct((D_MODEL, 1), F32)),
        in_specs=[_full((D_MODEL, LANES)),
                  pl.BlockSpec((1, LANES, SHARD_MOD), lambda l: (l, 0, 0)),
                  pl.BlockSpec((1, 8, SHARD_MOD), lambda l: (l, 0, 0)),
                  pl.BlockSpec((1, 16, 3 * D_MODEL), lambda l: (l, 0, 0)),
                  pl.BlockSpec((1, D_MODEL, SHARD_MOD), lambda l: (l, 0, 0)),
                  _full((D_MODEL, 1))],
        out_specs=(pl.BlockSpec((1, D_MODEL, SHARD_MOD), lambda l: (l, 0, 0)),
                   pl.BlockSpec((1, 1, 3 * D_MODEL), lambda l: (l, 0, 0)),
                   _full((D_MODEL, 1))),
        compiler_params=_cparams(),
    )(s_t, g_rows, g_ctx, d_all, w_mod, c_ctx_col)


def _head_norm(xb, lo):
    r = lax.rsqrt(_pair_sums(xb * xb, lo) * (1.0 / HEAD_DIM) + RMS_EPS)
    return xb * r, r


def _in_proj(xt, modv, g_pre, w_c, w_r, w_q, qk_gain, cos_t, sin_t):
    t = xt.shape[0]

    def body(x_ref, mod_ref, g_ref, wc_ref, wr_ref, wq_ref, gain_ref, cos_ref, sin_ref,
             h_ref, pc_ref, pr_ref, pq_ref, q_ref, k_ref, v_ref):
        lane = _lane(TM)
        lo = lane < HEAD_DIM
        lo16 = (lane & 31) < 16
        one = jnp.where(lane == HEAD_DIM, 1.0, 0.0)
        for jj in range(SUB):
            rows = pl.ds(jj * TM, TM)
            is_ctx = pl.program_id(0) * SUB + jj < N_CTX_TILES
            x = x_ref[rows, :]
            r = lax.rsqrt(jnp.mean(x * x, axis=1, keepdims=True) + RMS_EPS)
            sh = jnp.where(is_ctx, mod_ref[0:1, :], mod_ref[3:4, :])
            sc = jnp.where(is_ctx, mod_ref[1:2, :], mod_ref[4:5, :])
            h = (x * r * g_ref[...]) * (1.0 + sc) + sh
            hb = h.astype(BF16)
            h_ref[rows, :] = hb
            pc_ref[rows, :] = _dot(hb, wc_ref[...])
            pr_ref[rows, :] = _dot(hb, wr_ref[...])
            pq = _dot(hb, wq_ref[...])
            pq_ref[rows, :] = pq
            cos = cos_ref[rows, :]
            sin = sin_ref[rows, :]
            for b in range(3):
                xh, _ = _head_norm(pq[:, b * LANES:(b + 1) * LANES], lo)
                xg = xh * (gain_ref[0:1, :] if b < 2 else gain_ref[1:2, :])
                rot = xg * cos + _swap16(xg, lo16) * sin
                if b < 2:
                    rot = rot * ATTN_SCALE
                dst = q_ref if b < 2 else k_ref
                base = 2 * b if b < 2 else 0
                dst[base, rows, :] = jnp.where(lo, rot, 0.0).astype(BF16)
                dst[base + 1, rows, :] = jnp.where(lo, pltpu.roll(rot, HEAD_DIM, 1), 0.0).astype(BF16)
            vb = pq[:, 3 * LANES:4 * LANES]
            v_ref[0, rows, :] = jnp.where(lo, vb, one).astype(BF16)
            v_ref[1, rows, :] = jnp.where(lo, pltpu.roll(vb, HEAD_DIM, 1), one).astype(BF16)

    return _pc(
        body, name="in_proj", grid=(t // TB,),
        out_shape=(jax.ShapeDtypeStruct((t, D_MODEL), BF16),
                   jax.ShapeDtypeStruct((t, W_C), F32), jax.ShapeDtypeStruct((t, W_R), F32), jax.ShapeDtypeStruct((t, W_Q), F32),
                   jax.ShapeDtypeStruct((N_Q_HEADS, t, LANES), BF16),
                   jax.ShapeDtypeStruct((N_KV_HEADS, t, LANES), BF16),
                   jax.ShapeDtypeStruct((N_KV_HEADS, t, LANES), BF16)),
        in_specs=[_rows(D_MODEL, TB), _const((8, D_MODEL)), _const((1, D_MODEL)),
                  _const((D_MODEL, W_C)), _const((D_MODEL, W_R)), _const((D_MODEL, W_Q)),
                  _const((8, LANES)), _rows(LANES, TB), _rows(LANES, TB)],
        out_specs=(_rows(D_MODEL, TB), _rows(W_C, TB), _rows(W_R, TB), _rows(W_Q, TB),
                   _heads(N_Q_HEADS, LANES, TB), _heads(N_KV_HEADS, LANES, TB), _heads(N_KV_HEADS, LANES, TB)),
        compiler_params=_cparams(),
    )(xt, modv, g_pre, w_c, w_r, w_q, qk_gain, cos_t, sin_t)


def _attention_fwd(q, k, v, shards=None):
    t = q.shape[1]
    tk = _kv_chunk(t)
    n_chunks = (t - CTX_LEN) // tk
    n_tiles = t // TM
    n_sh = 0 if shards is None else len(shards)

    def body(q_ref, k_ref, v_ref, *rest):
        i = pl.program_id(0)
        o_ref = rest[n_sh]
        if shards is not None:
            pairs = tuple(zip(rest[:n_sh], rest[n_sh + 1:2 * n_sh + 1]))
            for phase, at in enumerate((0, n_tiles // 2, n_tiles - 1)):
                @pl.when(i == at)
                def _(phase=phase):
                    _gather_phase(phase, pairs, *rest[2 * n_sh + 1:])
        lane = _lane(GQA * TM)
        qs = [jnp.concatenate([q_ref[GQA * g + hh] for hh in range(GQA)], axis=0) for g in range(N_KV_HEADS)]

        def step(st, size, carry):
            out = []
            for g in range(N_KV_HEADS):
                m, acc = carry[g]
                s = _dot_nt(qs[g], k_ref[g, pl.ds(st, size), :])
                m_new = jnp.maximum(m, jnp.max(s, axis=1, keepdims=True))
                p = jnp.exp(s - m_new)
                out.append((m_new, acc * jnp.exp(m - m_new) + _dot(p.astype(BF16), v_ref[g, pl.ds(st, size), :])))
            return tuple(out)

        init = tuple((jnp.full((GQA * TM, 1), -jnp.inf, F32), jnp.zeros((GQA * TM, LANES), F32)) for _ in range(N_KV_HEADS))

        def finish(carry):
            for g in range(N_KV_HEADS):
                m, acc = carry[g]
                den = _rowsum(jnp.where(lane == HEAD_DIM, acc, 0.0))
                out = jnp.where(lane < HEAD_DIM, acc * (1.0 / den), jnp.where(lane == HEAD_DIM, m + jnp.log(den), 0.0))
                for hh in range(GQA):
                    o_ref[GQA * g + hh] = out[hh * TM:(hh + 1) * TM]

        @pl.when(i < N_CTX_TILES)
        def _():
            finish(step(0, CTX_LEN, init))

        @pl.when(i >= N_CTX_TILES)
        def _():
            per = 4 if n_chunks % 4 == 0 else 1

            def trip(j, cr):
                st = pl.multiple_of(CTX_LEN + j * (per * tk), 256)
                for u in range(per):
                    cr = step(st + u * tk, tk, cr)
                return cr

            finish(lax.fori_loop(0, n_chunks // per, trip, step(0, CTX_LEN, init)))

    hbm = pl.BlockSpec(memory_space=pl.ANY)
    extra = () if shards is None else tuple(shards)
    outs = _pc(
        body, name="attention_fwd" if shards is None else "attention_fwd_gather", grid=(n_tiles,),
        out_shape=(jax.ShapeDtypeStruct((N_Q_HEADS, t, LANES), F32),) + tuple(_slots(a) for a in extra),
        in_specs=[_heads(N_Q_HEADS, LANES), _full((N_KV_HEADS, t, LANES)), _full((N_KV_HEADS, t, LANES))] + [hbm] * n_sh,
        out_specs=(_heads(N_Q_HEADS, LANES),) + (hbm,) * n_sh,
        scratch_shapes=_comm_scratch(GATHER_SEMS, n_sh) if shards is not None else [],
        compiler_params=_cparams(),
    )(q, k, v, *extra)
    return outs[0], tuple(outs[1:])


def _halo_specs(width, t, rows=TM):
    last = t // HALO - 1
    per = rows // HALO
    prev = pl.BlockSpec((HALO, width), lambda i: (jnp.maximum(i * per - 1, 0), 0))
    nxt = pl.BlockSpec((HALO, width), lambda i: (jnp.minimum((i + 1) * per, last), 0))
    return prev, nxt


def _halo_valid(i, n_tiles):
    prev_ok = jnp.logical_and(i != 0, i != N_CTX_TILES)
    next_ok = jnp.logical_and(i != N_CTX_TILES - 1, i != n_tiles - 1)
    return jnp.where(prev_ok, 1.0, 0.0), jnp.where(next_ok, 1.0, 0.0)


def _conv_inputs(pc):
    u = pc[:, 0:GROUP_W] * pc[:, GROUP_W:2 * GROUP_W]
    z = pc[:, 2 * GROUP_W:3 * GROUP_W] * _sigmoid(pc[:, 3 * GROUP_W:4 * GROUP_W])
    return u, z


def _fill_ext(ext_ref, prev, mid, nxt):
    ext_ref[0:HALO, :] = prev
    ext_ref[HALO:HALO + TM, :] = mid
    ext_ref[HALO + TM:HALO + TM + HALO, :] = nxt


def _row_local_mixers(pr, ca, z2, oe, vecs, wss_ref, bsm, lane256):
    a_b, a_g, b_g = pr[:, 0:256], pr[:, 256:512], pr[:, 512:768]
    c_u, c_v, c_g, d_g = pr[:, 768:1024], pr[:, 1024:1280], pr[:, 1280:1536], pr[:, 1536:1792]
    zn, rs_b = _layer_norm_stats(z2)
    tb = zn * vecs[1:2, :] + vecs[2:3, :]
    vn_hat, rs_c = _layer_norm_stats(c_v)
    vn = vn_hat * vecs[3:4, :] + vecs[4:5, :]
    grp = jnp.right_shift(lane256, 6)
    sgs = []
    for ch in range(TM // CHUNK):
        r = _dot(wss_ref[...], vn[ch * CHUNK:(ch + 1) * CHUNK, :].astype(BF16))
        sgs.append(_group_select(r, grp[0:CHUNK]) + bsm)
    sg = jnp.concatenate(sgs, axis=0)
    lane = _lane(TM)
    lo = lane < HEAD_DIM
    att = jnp.concatenate([jnp.where(lo, oe[2 * b], pltpu.roll(oe[2 * b + 1], HEAD_DIM, 1)) for b in range(2)], axis=1)
    return dict(a_b=a_b, a_g=a_g, b_g=b_g, c_u=c_u, c_v=c_v, c_g=c_g, d_g=d_g, zn=zn, rs_b=rs_b, tb=tb,
                vn_hat=vn_hat, rs_c=rs_c, vn=vn, sg=sg, att=att, grp=grp, lo=lo, lane=lane)


def _mixer_concat(f, ca):
    ya = f["a_b"] * ca
    yb = f["tb"] * _sigmoid(f["tb"])
    yc = f["c_u"] * f["sg"]
    gates = [f[n] * _sigmoid(f[n]) for n in ("a_g", "b_g", "c_g", "d_g")]
    ys = (ya, yb, yc, f["att"])
    big = jnp.concatenate([yy * gg for yy, gg in zip(ys, gates)], axis=1).astype(BF16)
    return big, ys, gates


def _taps31(ext_ref, w_ref, flip):
    blocks = []
    for r0 in range(0, TM, CONV_ROWS):
        out = None
        for b in range(8):
            part = None
            for a in range(4):
                o = 8 * a + b
                if 1 <= o <= CONFORMER_K:
                    kk = CONFORMER_K - o if flip else o - 1
                    term = w_ref[kk:kk + 1, :] * ext_ref[pl.ds(r0 + 8 * a, CONV_ROWS + 8), :]
                    part = term if part is None else part + term
            part = part[b:b + CONV_ROWS]
            out = part if out is None else out + part
        blocks.append(out)
    return jnp.concatenate(blocks, axis=0)


def _mix_out(pc, pr, oe, xt, modv, g_post, w_out, conv_a, conv_b, vecs, wss, bsm, target=None):
    t = xt.shape[0]
    n_tiles = t // TM
    prev_spec, next_spec = _halo_specs(W_C, t, TB)
    n_t = 0 if target is None else SUB

    def body(pc_ref, pp_ref, pn_ref, pr_ref, oe_ref, x_ref, mod_ref, gp_ref, wo_ref, cva_ref, cvb_ref, vec_ref, wss_ref, bsm_ref,
             *rest):
        xo_ref, y_ref, ca_ref, z2_ref = rest[n_t:n_t + 4]
        uext, zext = rest[-2:]
        i = pl.program_id(0)
        vecs = vec_ref[...]
        lane256 = lax.broadcasted_iota(jnp.int32, (TM, GROUP_W), 1)
        if target is not None:
            loss_ref = rest[n_t + 4]

            @pl.when(i == 0)
            def _():
                loss_ref[...] = jnp.zeros_like(loss_ref)

        for jj in range(SUB):
            rows = pl.ds(jj * TM, TM)
            tile = i * SUB + jj
            is_ctx = tile < N_CTX_TILES
            pv, nv = _halo_valid(tile, n_tiles)
            u, z = _conv_inputs(pc_ref[rows, :])
            up, zp = _conv_inputs(pp_ref[...] if jj == 0 else pc_ref[pl.ds(jj * TM - HALO, HALO), :])
            un, zn_ = _conv_inputs(pn_ref[...] if jj == SUB - 1 else pc_ref[pl.ds((jj + 1) * TM, HALO), :])
            ue, ze = uext.at[jj], zext.at[jj]
            _fill_ext(ue, up * pv, u, un * nv)
            _fill_ext(ze, zp * pv, z, zn_ * nv)
            ca = cva_ref[0:1, :] * ue[pl.ds(HALO - 1, TM), :]
            for kk in range(1, SHORT_CONV_K):
                ca = ca + cva_ref[kk:kk + 1, :] * ue[pl.ds(HALO - 1 + kk, TM), :]
            z2 = _taps31(ze, cvb_ref, False) + vecs[0:1, :]
            ca_ref[rows, :] = ca
            z2_ref[rows, :] = z2
            oes = [oe_ref[h, rows, :] for h in range(N_Q_HEADS)]
            f = _row_local_mixers(pr_ref[rows, :], ca, z2, oes, vecs, wss_ref, bsm_ref[...], lane256)
            big, _, _ = _mixer_concat(f, ca)
            y = _dot(big, wo_ref[...])
            y_ref[rows, :] = y
            ry = lax.rsqrt(jnp.mean(y * y, axis=1, keepdims=True) + RMS_EPS)
            gt = jnp.where(is_ctx, mod_ref[2:3, :], mod_ref[5:6, :])
            x_new = x_ref[rows, :] + gt * (y * ry * gp_ref[...])
            if target is None:
                xo_ref[rows, :] = x_new
            else:
                err = (x_new - rest[jj][...]) * jnp.where(is_ctx, 0.0, 1.0)
                xo_ref[rows, :] = err * (1.0 / D_MODEL)
                loss_ref[...] += jnp.sum(err * err) * (0.5 / D_MODEL)

    rows_f32 = jax.ShapeDtypeStruct((t, D_MODEL), F32)
    group_f32 = jax.ShapeDtypeStruct((t, GROUP_W), F32)
    loss_shape, loss_spec, t_spec, t_arg = (), (), [], ()
    if target is not None:
        loss_shape, loss_spec = (jax.ShapeDtypeStruct((8, LANES), F32),), (_full((8, LANES)),)
        t_spec = [pl.BlockSpec((TM, D_MODEL), lambda i, jj=jj: (jnp.maximum(i * SUB + jj - N_CTX_TILES, 0), 0))
                  for jj in range(SUB)]
        t_arg = (target,) * SUB
    return _pc(
        body, name="mix_out" if target is None else "mix_out_loss", grid=(t // TB,),
        out_shape=(rows_f32, rows_f32, group_f32, group_f32) + loss_shape,
        in_specs=[_rows(W_C, TB), prev_spec, next_spec, _rows(W_R, TB), _heads(N_Q_HEADS, LANES, TB), _rows(D_MODEL, TB),
                  _const((8, D_MODEL)), _const((1, D_MODEL)), _const((D_MODEL, D_MODEL)),
                  _const((8, GROUP_W)), _const((32, GROUP_W)), _const((8, GROUP_W)),
                  _const((N_SPATIAL_GROUPS * CHUNK, CHUNK)), _const((CHUNK, GROUP_W))] + t_spec,
        out_specs=(_rows(D_MODEL, TB), _rows(D_MODEL, TB), _rows(GROUP_W, TB), _rows(GROUP_W, TB)) + loss_spec,
        scratch_shapes=[pltpu.VMEM((SUB, TM + 2 * HALO, GROUP_W), F32), pltpu.VMEM((SUB, TM + 2 * HALO, GROUP_W), F32)],
        compiler_params=_cparams(),
    )(pc, pc, pc, pr, oe, xt, modv, g_post, w_out, conv_a, conv_b, vecs, wss, bsm, *t_arg)


def _mix_out_bwd(dxo, y, pr, ca, z2, oe, modv, g_post, w_out, vecs, wss, wsts, bsm):
    t = y.shape[0]
    n_tiles = t // TM

    def body(dxo_ref, y_ref, pr_ref, ca_ref, z2_ref, oe_ref, mod_ref, gp_ref, wo_ref, vec_ref, wss_ref, wsts_ref, bsm_ref,
             dpr_ref, ga_ref, gb_ref, doe_ref, dwo_hbm, pvec_ref, s256_ref, dws_ref, dbs_ref, dbsm, dwo_ref):
        i = pl.program_id(0)

        @pl.when(i == 0)
        def _():
            dwo_ref[...] = jnp.zeros_like(dwo_ref)
            pvec_ref[...] = jnp.zeros_like(pvec_ref)
            s256_ref[...] = jnp.zeros_like(s256_ref)
            dws_ref[...] = jnp.zeros_like(dws_ref)
            dbsm[...] = jnp.zeros_like(dbsm)

        gp = gp_ref[...]
        vecs = vec_ref[...]
        bsm_ = bsm_ref[...]
        lane256 = lax.broadcasted_iota(jnp.int32, (TM, GROUP_W), 1)
        grp = jnp.right_shift(lane256, 6)

        for jj in range(SUB):
            tile_rows = pl.ds(jj * TM, TM)
            is_ctx = i * SUB + jj < N_CTX_TILES
            dxo_ = dxo_ref[tile_rows, :]
            y_ = y_ref[tile_rows, :]
            ry = lax.rsqrt(jnp.mean(y_ * y_, axis=1, keepdims=True) + RMS_EPS)
            nh = y_ * ry
            gt = jnp.where(is_ctx, mod_ref[2:3, :], mod_ref[5:6, :])
            dgt = _colsum(dxo_ * (nh * gp))
            pvec_ref[0:1, :] += jnp.where(is_ctx, dgt, 0.0)
            pvec_ref[1:2, :] += jnp.where(is_ctx, 0.0, dgt)
            dn = dxo_ * gt
            pvec_ref[2:3, :] += _colsum(dn * nh)
            dnh = dn * gp
            dy = ry * (dnh - nh * jnp.mean(dnh * nh, axis=1, keepdims=True))

            ca_ = ca_ref[tile_rows, :]
            oes = [oe_ref[h, tile_rows, :] for h in range(N_Q_HEADS)]
            f = _row_local_mixers(pr_ref[tile_rows, :], ca_, z2_ref[tile_rows, :], oes, vecs, wss_ref, bsm_, lane256)
            big, ys, gates = _mixer_concat(f, ca_)
            dyb = dy.astype(BF16)
            dwo_ref[...] += _dot_tn(big, dyb)
            dbig = _dot_nt(dyb, wo_ref[...])

            d_y, d_gate = [], []
            for n, (name, yy, gg) in enumerate(zip(("a_g", "b_g", "c_g", "d_g"), ys, gates)):
                dpart = dbig[:, n * GROUP_W:(n + 1) * GROUP_W]
                gx = f[name]
                sg_ = _sigmoid(gx)
                d_y.append(dpart * gg)
                d_gate.append(dpart * yy * (sg_ * (1.0 + gx * (1.0 - sg_))))
            dya, dyb_, dyc, datt = d_y

            d_ab = dya * ca_
            ga_ref[tile_rows, :] = dya * f["a_b"]
            tb = f["tb"]
            sb = _sigmoid(tb)
            dtb = dyb_ * (sb * (1.0 + tb * (1.0 - sb)))
            s256_ref[1:2, :] += _colsum(dtb * f["zn"])
            s256_ref[2:3, :] += _colsum(dtb)
            dz2 = _layer_norm_bwd(dtb * vecs[1:2, :], f["zn"], f["rs_b"])
            gb_ref[tile_rows, :] = dz2
            s256_ref[0:1, :] += _colsum(dz2)
            d_cu = dyc * f["sg"]
            dsg = dyc * f["c_u"]
            dvn_parts = []
            for ch in range(TM // CHUNK):
                rows = slice(ch * CHUNK, (ch + 1) * CHUNK)
                dsg_c = dsg[rows, :]
                dbsm[...] += dsg_c
                vn_c = f["vn"][rows, :].astype(BF16)
                for g in range(N_SPATIAL_GROUPS):
                    masked = jnp.where(grp[0:CHUNK] == g, dsg_c, 0.0).astype(BF16)
                    dws_ref[g * CHUNK:(g + 1) * CHUNK, :] += _dot_nt(masked, vn_c)
                dvn_parts.append(_group_select(_dot(wsts_ref[...], dsg_c.astype(BF16)), grp[0:CHUNK]))
            dvn = jnp.concatenate(dvn_parts, axis=0)
            s256_ref[3:4, :] += _colsum(dvn * f["vn_hat"])
            s256_ref[4:5, :] += _colsum(dvn)
            d_cv = _layer_norm_bwd(dvn * vecs[3:4, :], f["vn_hat"], f["rs_c"])
            lane, lo = f["lane"], f["lo"]
            att = f["att"]
            for b in range(2):
                da = datt[:, b * LANES:(b + 1) * LANES]
                prod = da * att[:, b * LANES:(b + 1) * LANES]
                for hh in range(2):
                    h = 2 * b + hh
                    lse = _rowsum(jnp.where(lane == HEAD_DIM, oes[h], 0.0))
                    delta = _rowsum(jnp.where(lo, prod, 0.0) if hh == 0 else jnp.where(lo, 0.0, prod))
                    dah = da if hh == 0 else pltpu.roll(da, HEAD_DIM, 1)
                    doe_ref[h, tile_rows, :] = jnp.where(
                        lo, dah, jnp.where(lane == HEAD_DIM, delta, jnp.where(lane == HEAD_DIM + 1, lse, 0.0)))

            dpr_ref[tile_rows, :] = jnp.concatenate(
                [d_ab, d_gate[0], d_gate[1], d_cu, d_cv, d_gate[2], d_gate[3]], axis=1).astype(BF16)

        @pl.when(i == t // TB - 1)
        def _():
            acc = dbsm[...]
            lane128 = _lane(CHUNK)
            out = jnp.zeros((CHUNK, LANES), F32)
            for g in range(N_SPATIAL_GROUPS):
                col = _rowsum(jnp.where(grp[0:CHUNK] == g, acc, 0.0))
                out = out + jnp.where(lane128 == g, col, 0.0)
            dbs_ref[...] = out
            pltpu.sync_copy(dwo_ref, dwo_hbm)

    return _pc(
        body, name="mix_out_bwd", grid=(t // TB,),
        out_shape=(jax.ShapeDtypeStruct((t, W_R), BF16),
                   jax.ShapeDtypeStruct((t, GROUP_W), F32), jax.ShapeDtypeStruct((t, GROUP_W), F32),
                   jax.ShapeDtypeStruct((N_Q_HEADS, t, LANES), F32),
                   jax.ShapeDtypeStruct((D_MODEL, D_MODEL), F32),
                   jax.ShapeDtypeStruct((8, D_MODEL), F32),
                   jax.ShapeDtypeStruct((8, GROUP_W), F32),
                   jax.ShapeDtypeStruct((N_SPATIAL_GROUPS * CHUNK, CHUNK), F32),
                   jax.ShapeDtypeStruct((CHUNK, LANES), F32)),
        in_specs=[_rows(D_MODEL, TB), _rows(D_MODEL, TB), _rows(W_R, TB), _rows(GROUP_W, TB), _rows(GROUP_W, TB),
                  _heads(N_Q_HEADS, LANES, TB),
                  _const((8, D_MODEL)), _const((1, D_MODEL)), _const((D_MODEL, D_MODEL)), _const((8, GROUP_W)),
                  _const((N_SPATIAL_GROUPS * CHUNK, CHUNK)), _const((N_SPATIAL_GROUPS * CHUNK, CHUNK)), _const((CHUNK, GROUP_W))],
        out_specs=(_rows(W_R, TB), _rows(GROUP_W, TB), _rows(GROUP_W, TB), _heads(N_Q_HEADS, LANES, TB),
                   pl.BlockSpec(memory_space=pl.ANY), _full((8, D_MODEL)), _full((8, GROUP_W)),
                   _full((N_SPATIAL_GROUPS * CHUNK, CHUNK)), _full((CHUNK, LANES))),
        scratch_shapes=[pltpu.VMEM((CHUNK, GROUP_W), F32), pltpu.VMEM((D_MODEL, D_MODEL), F32)],
        compiler_params=pltpu.CompilerParams(dimension_semantics=("arbitrary",), vmem_limit_bytes=VMEM_LIMIT_WIDE),
    )(dxo, y, pr, ca, z2, oe, modv, g_post, w_out, vecs, wss, wsts, bsm)


def _conv_bwd(pc, g_a, g_b, conv_a, conv_b):
    t = pc.shape[0]
    n_tiles = t // TM
    pc_prev, pc_next = _halo_specs(W_C, t, TB)
    g_prev, g_next = _halo_specs(GROUP_W, t, TB)

    def body(pc_ref, pp_ref, pn_ref, ga_ref, gap_ref, gan_ref, gb_ref, gbp_ref, gbn_ref, cva_ref, cvb_ref,
             dpc_ref, dca_ref, dcb_ref, uext, zext, gaext, gbext):
        i = pl.program_id(0)

        @pl.when(i == 0)
        def _():
            dca_ref[...] = jnp.zeros_like(dca_ref)
            dcb_ref[...] = jnp.zeros_like(dcb_ref)

        def halo(jj, tile_ref, prev_ref, next_ref):
            before = prev_ref[...] if jj == 0 else tile_ref[pl.ds(jj * TM - HALO, HALO), :]
            after = next_ref[...] if jj == SUB - 1 else tile_ref[pl.ds((jj + 1) * TM, HALO), :]
            return before, after

        for jj in range(SUB):
            rows = pl.ds(jj * TM, TM)
            pv, nv = _halo_valid(i * SUB + jj, n_tiles)
            pc_ = pc_ref[rows, :]
            u, z = _conv_inputs(pc_)
            pc_before, pc_after = halo(jj, pc_ref, pp_ref, pn_ref)
            up, zp = _conv_inputs(pc_before)
            un, zn_ = _conv_inputs(pc_after)
            ue, ze, gae, gbe = uext.at[jj], zext.at[jj], gaext.at[jj], gbext.at[jj]
            _fill_ext(ue, up * pv, u, un * nv)
            _fill_ext(ze, zp * pv, z, zn_ * nv)
            ga = ga_ref[rows, :]
            gb = gb_ref[rows, :]
            ga_before, ga_after = halo(jj, ga_ref, gap_ref, gan_ref)
            gb_before, gb_after = halo(jj, gb_ref, gbp_ref, gbn_ref)
            _fill_ext(gae, ga_before * pv, ga, ga_after * nv)
            _fill_ext(gbe, gb_before * pv, gb, gb_after * nv)

            du = cva_ref[0:1, :] * gae[pl.ds(HALO + 1, TM), :]
            dca_ref[0:1, :] += _colsum(ga * ue[pl.ds(HALO - 1, TM), :])
            for kk in range(1, SHORT_CONV_K):
                du = du + cva_ref[kk:kk + 1, :] * gae[pl.ds(HALO + 1 - kk, TM), :]
                dca_ref[kk:kk + 1, :] += _colsum(ga * ue[pl.ds(HALO - 1 + kk, TM), :])
            dz = _taps31(gbe, cvb_ref, True)
            for r0 in range(0, TM, CONV_ROWS):
                gb_rows = gb_ref[pl.ds(jj * TM + r0, CONV_ROWS), :]
                for b in range(8):
                    zb = ze[pl.ds(r0 + b, CONV_ROWS + 24), :]
                    for a in range(4):
                        kk = 8 * a + b - 1
                        if 0 <= kk < CONFORMER_K:
                            dcb_ref[kk:kk + 1, :] += _colsum(gb_rows * zb[8 * a:8 * a + CONV_ROWS])

            a_c, a_h = pc_[:, 0:GROUP_W], pc_[:, GROUP_W:2 * GROUP_W]
            glu_a, glu_g = pc_[:, 2 * GROUP_W:3 * GROUP_W], pc_[:, 3 * GROUP_W:4 * GROUP_W]
            sg = _sigmoid(glu_g)
            dpc_ref[rows, :] = jnp.concatenate([du * a_h, du * a_c, dz * sg, dz * glu_a * sg * (1.0 - sg)], axis=1).astype(BF16)

    ext = pltpu.VMEM((SUB, TM + 2 * HALO, GROUP_W), F32)
    return _pc(
        body, name="conv_bwd", grid=(t // TB,),
        out_shape=(jax.ShapeDtypeStruct((t, W_C), BF16), jax.ShapeDtypeStruct((8, GROUP_W), F32), jax.ShapeDtypeStruct((32, GROUP_W), F32)),
        in_specs=[_rows(W_C, TB), pc_prev, pc_next, _rows(GROUP_W, TB), g_prev, g_next, _rows(GROUP_W, TB), g_prev, g_next,
                  _const((8, GROUP_W)), _const((32, GROUP_W))],
        out_specs=(_rows(W_C, TB), _full((8, GROUP_W)), _full((32, GROUP_W))),
        scratch_shapes=[ext, ext, ext, ext],
        compiler_params=_cparams(),
    )(pc, pc, pc, g_a, g_a, g_a, g_b, g_b, g_b, conv_a, conv_b)


def _attention_bwd(q, k, v, doe, slabs, conv):
    t = q.shape[1]
    tk = _kv_chunk(t)
    n_chunks = (t - CTX_LEN) // tk
    n_tiles = t // TM
    n_sl = len(slabs)
    rpt = TM // n_chunks
    pc, g_a, g_b, conv_a, conv_b = conv

    def body(q_ref, do_ref, k_ref, v_ref, pc_ref, pp_ref, pn_ref, ga_ref, gap_ref, gan_ref, gb_ref, gbp_ref, gbn_ref,
             cva_ref, cvb_ref, *rest):
        i = pl.program_id(0)
        dq_ref, dk_hbm, dv_hbm, dpc_ref, dca_ref, dcb_ref = rest[n_sl:n_sl + 6]
        dk_acc, dv_acc, uext, zext, gaext, gbext, dz_s = rest[2 * n_sl + 6:2 * n_sl + 13]
        pairs = tuple(zip(rest[:n_sl], rest[n_sl + 6:2 * n_sl + 6]))
        sems = rest[2 * n_sl + 13:]

        @pl.when(i == 0)
        def _():
            dk_acc[...] = jnp.zeros_like(dk_acc)
            dv_acc[...] = jnp.zeros_like(dv_acc)
            dca_ref[...] = jnp.zeros_like(dca_ref)
            dcb_ref[...] = jnp.zeros_like(dcb_ref)
            _scatter_phase(0, pairs, *sems)

        pv, nv = _halo_valid(i, n_tiles)
        pc_ = pc_ref[...]
        u, z = _conv_inputs(pc_)
        up, zp = _conv_inputs(pp_ref[...])
        un, zn_ = _conv_inputs(pn_ref[...])
        _fill_ext(uext, up * pv, u, un * nv)
        _fill_ext(zext, zp * pv, z, zn_ * nv)
        _fill_ext(gaext, gap_ref[...] * pv, ga_ref[...], gan_ref[...] * nv)
        _fill_ext(gbext, gbp_ref[...] * pv, gb_ref[...], gbn_ref[...] * nv)

        def conv_rows(r0):
            out = None
            for b in range(8):
                part = None
                for a in range(4):
                    o = 8 * a + b
                    if 1 <= o <= CONFORMER_K:
                        term = cvb_ref[CONFORMER_K - o:CONFORMER_K - o + 1, :] * gbext[pl.ds(r0 + 8 * a, rpt + 8), :]
                        part = term if part is None else part + term
                part = part[b:b + rpt]
                out = part if out is None else out + part
            dz_s[pl.ds(r0, rpt), :] = out
            gb_rows = gb_ref[pl.ds(r0, rpt), :]
            window = zext[pl.ds(r0, rpt + 2 * HALO), :]
            for b in range(8):
                zb = window[b:b + rpt + 24]
                for a in range(4):
                    kk = 8 * a + b - 1
                    if 0 <= kk < CONFORMER_K:
                        dcb_ref[kk:kk + 1, :] += _colsum(gb_rows * zb[8 * a:8 * a + rpt])

        lane = _lane(GQA * TM)
        lo = lane < HEAD_DIM
        qs, dos, deltas, lses = [], [], [], []
        for g in range(N_KV_HEADS):
            qs.append(jnp.concatenate([q_ref[GQA * g + hh] for hh in range(GQA)], axis=0))
            dog = jnp.concatenate([do_ref[GQA * g + hh] for hh in range(GQA)], axis=0)
            deltas.append(_rowsum(jnp.where(lane == HEAD_DIM, dog, 0.0)))
            lses.append(_rowsum(jnp.where(lane == HEAD_DIM + 1, dog, 0.0)))
            dos.append(jnp.where(lo, dog, 0.0).astype(BF16))

        def step(st, size, dqs):
            out = []
            for g in range(N_KV_HEADS):
                kc = k_ref[g, pl.ds(st, size), :]
                vc = v_ref[g, pl.ds(st, size), :]
                p = jnp.exp(_dot_nt(qs[g], kc) - lses[g])
                ds_ = (p * (_dot_nt(dos[g], vc) - deltas[g])).astype(BF16)
                dk_acc[g, pl.ds(st, size), :] += _dot_tn(ds_, qs[g])
                dv_acc[g, pl.ds(st, size), :] += _dot_tn(p.astype(BF16), dos[g])
                out.append(dqs[g] + _dot(ds_, kc))
            return tuple(out)

        zero = tuple(jnp.zeros((GQA * TM, LANES), F32) for _ in range(N_KV_HEADS))

        def finish(dqs):
            for g in range(N_KV_HEADS):
                for hh in range(GQA):
                    dq_ref[GQA * g + hh] = dqs[g][hh * TM:(hh + 1) * TM]

        @pl.when(i < N_CTX_TILES)
        def _():
            finish(step(0, CTX_LEN, zero))
            for r in range(n_chunks):
                conv_rows(r * rpt)

        @pl.when(i >= N_CTX_TILES)
        def _():
            def trip(j, acc):
                conv_rows(pl.multiple_of(j * rpt, rpt))
                return step(pl.multiple_of(CTX_LEN + j * tk, 256), tk, acc)

            finish(lax.fori_loop(0, n_chunks, trip, step(0, CTX_LEN, zero)))

        ga = ga_ref[...]
        du = cva_ref[0:1, :] * gaext[pl.ds(HALO + 1, TM), :]
        dca_ref[0:1, :] += _colsum(ga * uext[pl.ds(HALO - 1, TM), :])
        for kk in range(1, SHORT_CONV_K):
            du = du + cva_ref[kk:kk + 1, :] * gaext[pl.ds(HALO + 1 - kk, TM), :]
            dca_ref[kk:kk + 1, :] += _colsum(ga * uext[pl.ds(HALO - 1 + kk, TM), :])
        dz = dz_s[...]
        a_c, a_h = pc_[:, 0:GROUP_W], pc_[:, GROUP_W:2 * GROUP_W]
        glu_a, glu_g = pc_[:, 2 * GROUP_W:3 * GROUP_W], pc_[:, 3 * GROUP_W:4 * GROUP_W]
        sg = _sigmoid(glu_g)
        dpc_ref[...] = jnp.concatenate([du * a_h, du * a_c, dz * sg, dz * glu_a * sg * (1.0 - sg)], axis=1).astype(BF16)

        @pl.when(i == n_tiles - 1)
        def _():
            pltpu.sync_copy(dk_acc, dk_hbm)
            pltpu.sync_copy(dv_acc, dv_hbm)
            _scatter_phase(1, pairs, *sems)

    kv_shape = jax.ShapeDtypeStruct((N_KV_HEADS, t, LANES), F32)
    hbm = pl.BlockSpec(memory_space=pl.ANY)
    pc_prev, pc_next = _halo_specs(W_C, t)
    g_prev, g_next = _halo_specs(GROUP_W, t)
    ext = pltpu.VMEM((TM + 2 * HALO, GROUP_W), F32)
    outs = _pc(
        body, name="attention_bwd_scatter", grid=(n_tiles,),
        out_shape=(jax.ShapeDtypeStruct((N_Q_HEADS, t, LANES), F32), kv_shape, kv_shape,
                   jax.ShapeDtypeStruct((t, W_C), BF16), jax.ShapeDtypeStruct((8, GROUP_W), F32),
                   jax.ShapeDtypeStruct((32, GROUP_W), F32)) + tuple(jax.ShapeDtypeStruct(a.shape, a.dtype) for a in slabs),
        in_specs=[_heads(N_Q_HEADS, LANES), _heads(N_Q_HEADS, LANES),
                  _const((N_KV_HEADS, t, LANES)), _const((N_KV_HEADS, t, LANES)),
                  _rows(W_C), pc_prev, pc_next, _rows(GROUP_W), g_prev, g_next, _rows(GROUP_W), g_prev, g_next,
                  _const((8, GROUP_W)), _const((32, GROUP_W))] + [hbm] * n_sl,
        out_specs=(_heads(N_Q_HEADS, LANES), hbm, hbm, _rows(W_C), _full((8, GROUP_W)), _full((32, GROUP_W))) + (hbm,) * n_sl,
        scratch_shapes=[pltpu.VMEM((N_KV_HEADS, t, LANES), F32), pltpu.VMEM((N_KV_HEADS, t, LANES), F32),
                        ext, ext, ext, ext, pltpu.VMEM((TM, GROUP_W), F32)] + _comm_scratch(SCATTER_SEMS, n_sl),
        compiler_params=_cparams(),
    )(q, doe, k, v, pc, pc, pc, g_a, g_a, g_a, g_b, g_b, g_b, conv_a, conv_b, *slabs)
    return outs[0], outs[1], outs[2], outs[3:6], tuple(outs[6:])


def _in_proj_bwd(dpc, dpr, dq, dk, dv, pq, qk_gain, cos_t, sin_t, w_c, w_r, w_q, xt, dxo, modv, g_pre):
    t = xt.shape[0]

    def body(dpc_ref, dpr_ref, dq_ref, dk_ref, dv_ref, pq_ref, gain_ref, cos_ref, sin_ref, wc_ref, wr_ref, wq_ref,
             x_ref, dxo_ref, mod_ref, g_ref, dx_ref, dpq_ref, acc_ref, dgain_ref):
        i = pl.program_id(0)

        @pl.when(i == 0)
        def _():
            acc_ref[...] = jnp.zeros_like(acc_ref)
            dgain_ref[...] = jnp.zeros_like(dgain_ref)

        lane = _lane(TM)
        lo = lane < HEAD_DIM
        lo16 = (lane & 31) < 16
        g = g_ref[...]
        for jj in range(SUB):
            rows = pl.ds(jj * TM, TM)
            is_ctx = i * SUB + jj < N_CTX_TILES
            cos = cos_ref[rows, :]
            sin = sin_ref[rows, :]
            outs = []
            for b in range(3):
                src = dq_ref if b < 2 else dk_ref
                base = 2 * b if b < 2 else 0
                drot = src[base, rows, :] + pltpu.roll(src[base + 1, rows, :], HEAD_DIM, 1)
                if b < 2:
                    drot = drot * ATTN_SCALE
                dxg = drot * cos + _swap16(drot * sin, lo16)
                xh, r = _head_norm(pq_ref[rows, b * LANES:(b + 1) * LANES], lo)
                row = 0 if b < 2 else 1
                dgain_ref[row:row + 1, :] += _colsum(dxg * xh)
                dxh = dxg * gain_ref[row:row + 1, :]
                outs.append(r * (dxh - xh * (_pair_sums(dxh * xh, lo) * (1.0 / HEAD_DIM))))
            outs.append(dv_ref[0, rows, :] + pltpu.roll(dv_ref[1, rows, :], HEAD_DIM, 1))
            dpq = jnp.concatenate(outs, axis=1).astype(BF16)
            dpq_ref[rows, :] = dpq

            dh = _dot_nt(dpc_ref[rows, :], wc_ref[...]) + _dot_nt(dpr_ref[rows, :], wr_ref[...]) + _dot_nt(dpq, wq_ref[...])
            x = x_ref[rows, :]
            r = lax.rsqrt(jnp.mean(x * x, axis=1, keepdims=True) + RMS_EPS)
            xn = x * r
            sc = jnp.where(is_ctx, mod_ref[1:2, :], mod_ref[4:5, :])
            dsh = _colsum(dh)
            dsc = _colsum(dh * (xn * g))
            acc_ref[0:1, :] += jnp.where(is_ctx, dsh, 0.0)
            acc_ref[1:2, :] += jnp.where(is_ctx, dsc, 0.0)
            acc_ref[2:3, :] += jnp.where(is_ctx, 0.0, dsh)
            acc_ref[3:4, :] += jnp.where(is_ctx, 0.0, dsc)
            dxg = dh * (1.0 + sc)
            acc_ref[4:5, :] += _colsum(dxg * xn)
            dxn = dxg * g
            dx_ref[rows, :] = r * (dxn - xn * jnp.mean(dxn * xn, axis=1, keepdims=True)) + dxo_ref[rows, :]

    return _pc(
        body, name="in_proj_bwd", grid=(t // TB,),
        out_shape=(jax.ShapeDtypeStruct((t, D_MODEL), F32), jax.ShapeDtypeStruct((t, W_Q), BF16),
                   jax.ShapeDtypeStruct((8, D_MODEL), F32), jax.ShapeDtypeStruct((8, LANES), F32)),
        in_specs=[_rows(W_C, TB), _rows(W_R, TB),
                  _heads(N_Q_HEADS, LANES, TB), _heads(N_KV_HEADS, LANES, TB), _heads(N_KV_HEADS, LANES, TB), _rows(W_Q, TB),
                  _const((8, LANES)), _rows(LANES, TB), _rows(LANES, TB),
                  _const((D_MODEL, W_C)), _const((D_MODEL, W_R)), _const((D_MODEL, W_Q)),
                  _rows(D_MODEL, TB), _rows(D_MODEL, TB), _const((8, D_MODEL)), _const((1, D_MODEL))],
        out_specs=(_rows(D_MODEL, TB), _rows(W_Q, TB), _full((8, D_MODEL)), _full((8, LANES))),
        compiler_params=_cparams(),
    )(dpc, dpr, dq, dk, dv, pq, qk_gain, cos_t, sin_t, w_c, w_r, w_q, xt, dxo, modv, g_pre)


def _in_proj_wgrad(h, dpc, dpr, dpq):
    t = h.shape[0]

    def body(h_ref, dpc_ref, dpr_ref, dpq_ref, gc_ref, gr_ref, gq_ref):
        @pl.when(pl.program_id(0) == 0)
        def _():
            gc_ref[...] = jnp.zeros_like(gc_ref)
            gr_ref[...] = jnp.zeros_like(gr_ref)
            gq_ref[...] = jnp.zeros_like(gq_ref)

        hb = h_ref[...]
        gc_ref[...] += _dot_tn(hb, dpc_ref[...])
        gr_ref[...] += _dot_tn(hb, dpr_ref[...])
        gq_ref[...] += _dot_tn(hb, dpq_ref[...])

    return _pc(
        body, name="in_proj_wgrad", grid=(t // TB,),
        out_shape=(jax.ShapeDtypeStruct((D_MODEL, W_C), F32), jax.ShapeDtypeStruct((D_MODEL, W_R), F32),
                   jax.ShapeDtypeStruct((D_MODEL, W_Q), F32)),
        in_specs=[_rows(D_MODEL, TB), _rows(W_C, TB), _rows(W_R, TB), _rows(W_Q, TB)],
        out_specs=(_full((D_MODEL, W_C)), _full((D_MODEL, W_R)), _full((D_MODEL, W_Q))),
        compiler_params=_cparams(),
    )(h, dpc, dpr, dpq)


def _sum_slabs(slabs, tile_rows):
    n, r, c = slabs.shape

    def body(s_ref, o_ref):
        acc = s_ref[0].astype(F32)
        for k in range(1, n):
            acc = acc + s_ref[k].astype(F32)
        o_ref[...] = acc

    return _pc(
        body, name="sum_slabs", grid=(r // tile_rows,),
        out_shape=jax.ShapeDtypeStruct((r, c), F32),
        in_specs=[pl.BlockSpec((n, tile_rows, c), lambda i: (0, i, 0))],
        out_specs=pl.BlockSpec((tile_rows, c), lambda i: (i, 0)),
        compiler_params=_cparams(),
    )(slabs)


def _sum_layer_slabs(layers, tile_rows):
    nl = len(layers)
    n, r, c = layers[0].shape
    per = r // tile_rows

    def body(*refs):
        o_ref = refs[nl]
        for l in range(nl):
            @pl.when(pl.program_id(0) // per == l)
            def _(l=l):
                acc = refs[l][0].astype(F32)
                for k in range(1, n):
                    acc = acc + refs[l][k].astype(F32)
                o_ref[...] = acc

    def spec(l):
        return pl.BlockSpec((n, tile_rows, c), lambda i: (0, jnp.clip(i - l * per, 0, per - 1), 0))

    return _pc(
        body, name="sum_layer_slabs", grid=(nl * per,),
        out_shape=jax.ShapeDtypeStruct((nl * r, c), F32),
        in_specs=[spec(l) for l in range(nl)],
        out_specs=pl.BlockSpec((tile_rows, c), lambda i: (i, 0)),
        compiler_params=_cparams(),
    )(*layers)


def _adamw(grads, w, m, v, tile_rows):
    r, c = w.shape
    n_g = len(grads)

    def body(*refs):
        g = refs[0][...]
        for k in range(1, n_g):
            g = g + refs[k][...]
        w_ref, m_ref, v_ref, g_out, d_out, m_out, v_out = refs[n_g:]
        m_new = ADAM_B1 * m_ref[...] + (1.0 - ADAM_B1) * g
        v_new = ADAM_B2 * v_ref[...] + (1.0 - ADAM_B2) * (g * g)
        m_hat = m_new / (1.0 - ADAM_B1 ** ADAM_STEP)
        v_hat = v_new / (1.0 - ADAM_B2 ** ADAM_STEP)
        g_out[...] = g
        d_out[...] = -ADAM_LR * (m_hat / (jnp.sqrt(v_hat) + ADAM_EPS) + ADAM_WD * w_ref[...])
        m_out[...] = m_new
        v_out[...] = v_new

    spec = pl.BlockSpec((tile_rows, c), lambda i: (i, 0))
    shape = jax.ShapeDtypeStruct((r, c), F32)
    return _pc(
        body, name="adamw", grid=(r // tile_rows,),
        out_shape=(shape,) * 4, in_specs=[spec] * (n_g + 3), out_specs=(spec,) * 4,
        compiler_params=_cparams(),
    )(*grads, w, m, v)


def _rope_tables(s_lat):
    n_rows = s_lat // GRID_W
    axis_dim = HEAD_DIM // 2
    inv_freq = 1.0 / (ROPE_THETA ** (jnp.arange(0, axis_dim, 2, dtype=F32) / axis_dim))
    d = np.arange(LANES) % HEAD_DIM
    on_rows = (d // axis_dim) == 0
    freq = d % (axis_dim // 2)
    sign = np.where((d % axis_dim) < axis_dim // 2, -1.0, 1.0).astype(np.float32)
    ang_r = jnp.arange(n_rows, dtype=F32)[:, None] * inv_freq[freq][None, :]
    ang_c = jnp.arange(GRID_W, dtype=F32)[:, None] * inv_freq[freq][None, :]

    def spread(fn):
        full = jnp.where(on_rows[None, None, :], fn(ang_r)[:, None, :], fn(ang_c)[None, :, :])
        return full.reshape(s_lat, LANES)

    cos = jnp.concatenate([jnp.ones((CTX_LEN, LANES), F32), spread(jnp.cos)], axis=0)
    sin = jnp.concatenate([jnp.zeros((CTX_LEN, LANES), F32), spread(jnp.sin) * sign[None, :]], axis=0)
    return cos, sin


def _pad_rows(a, rows):
    return jnp.concatenate([a, jnp.zeros((rows - a.shape[0],) + a.shape[1:], a.dtype)], axis=0)


_SMALL = ("c_ctx", "b_mod", "g_pre", "g_post", "conv_a", "conv_b", "conv_b_bias", "conf_ln_g", "conf_ln_b",
          "sgu_ln_g", "sgu_ln_b", "w_s", "b_s", "q_gain", "k_gain")


def _pack(arrays):
    flat = jnp.concatenate([a.reshape(-1) for a in arrays])
    rows = -(-flat.shape[0] // (16 * LANES)) * 16
    return _pad_rows(flat.reshape(-1, 1), rows * LANES).reshape(rows, LANES)


def _unpack(packed, shapes):
    flat = packed.reshape(-1)
    out, off = [], 0
    for s in shapes:
        n = int(np.prod(s))
        out.append(flat[off:off + n].reshape(s))
        off += n
    return out


def kernel(x, c, ctx, c_ctx, w_mod, b_mod, g_pre, g_post, w_in, w_out, conv_a, conv_b, conv_b_bias, conf_ln_g, conf_ln_b, sgu_ln_g, sgu_ln_b, w_s, b_s, q_gain, k_gain, loss_target, m_c_ctx, m_w_mod, m_b_mod, m_g_pre, m_g_post, m_w_in, m_w_out, m_conv_a, m_conv_b, m_conv_b_bias, m_conf_ln_g, m_conf_ln_b, m_sgu_ln_g, m_sgu_ln_b, m_w_s, m_b_s, m_q_gain, m_k_gain, v_c_ctx, v_w_mod, v_b_mod, v_g_pre, v_g_post, v_w_in, v_w_out, v_conv_a, v_conv_b, v_conv_b_bias, v_conf_ln_g, v_conf_ln_b, v_sgu_ln_g, v_sgu_ln_b, v_w_s, v_b_s, v_q_gain, v_k_gain):
    weights = dict(c_ctx=c_ctx, w_mod=w_mod, b_mod=b_mod, g_pre=g_pre, g_post=g_post, w_in=w_in, w_out=w_out, conv_a=conv_a,
                   conv_b=conv_b, conv_b_bias=conv_b_bias, conf_ln_g=conf_ln_g, conf_ln_b=conf_ln_b, sgu_ln_g=sgu_ln_g,
                   sgu_ln_b=sgu_ln_b, w_s=w_s, b_s=b_s, q_gain=q_gain, k_gain=k_gain)
    m_in = dict(c_ctx=m_c_ctx, w_mod=m_w_mod, b_mod=m_b_mod, g_pre=m_g_pre, g_post=m_g_post, w_in=m_w_in, w_out=m_w_out,
                conv_a=m_conv_a, conv_b=m_conv_b, conv_b_bias=m_conv_b_bias, conf_ln_g=m_conf_ln_g, conf_ln_b=m_conf_ln_b,
                sgu_ln_g=m_sgu_ln_g, sgu_ln_b=m_sgu_ln_b, w_s=m_w_s, b_s=m_b_s, q_gain=m_q_gain, k_gain=m_k_gain)
    v_in = dict(c_ctx=v_c_ctx, w_mod=v_w_mod, b_mod=v_b_mod, g_pre=v_g_pre, g_post=v_g_post, w_in=v_w_in, w_out=v_w_out,
                conv_a=v_conv_a, conv_b=v_conv_b, conv_b_bias=v_conv_b_bias, conf_ln_g=v_conf_ln_g, conf_ln_b=v_conf_ln_b,
                sgu_ln_g=v_sgu_ln_g, sgu_ln_b=v_sgu_ln_b, w_s=v_w_s, b_s=v_b_s, q_gain=v_q_gain, k_gain=v_k_gain)
    order = ("c_ctx", "w_mod", "b_mod", "g_pre", "g_post", "w_in", "w_out", "conv_a", "conv_b", "conv_b_bias", "conf_ln_g",
             "conf_ln_b", "sgu_ln_g", "sgu_ln_b", "w_s", "b_s", "q_gain", "k_gain")

    s_lat = x.shape[1]
    ax, ay, ac = lax.axis_index("x"), lax.axis_index("y"), lax.axis_index("c")
    chip = 2 * ax + ay
    example = 4 * ax + 2 * ay + ac

    c_rows = _all_gather_rows(_pad_rows(c, 8))[::8]
    c16 = _pad_rows(jnp.concatenate([c_rows, c_ctx[None, :]], axis=0), 16)
    b_mod_shard = lax.dynamic_slice_in_dim(b_mod, chip * SHARD_MOD, SHARD_MOD, axis=1)[:, None, :]
    silu_c, mod_shard = _mod_forward(c16, w_mod, b_mod_shard)
    mod_all = _all_gather_rows(mod_shard.reshape(DEPTH * 16, SHARD_MOD)).reshape(8, DEPTH, 16, SHARD_MOD)
    mod_full = jnp.transpose(mod_all[::2], (1, 2, 0, 3)).reshape(DEPTH, 16, 3 * D_MODEL)
    mod_lat = lax.dynamic_index_in_dim(mod_full, example, axis=1, keepdims=False).reshape(DEPTH, 3, D_MODEL)
    mod_ctx = mod_full[:, 8].reshape(DEPTH, 3, D_MODEL)
    modv = jnp.concatenate([mod_ctx, mod_lat, jnp.zeros((DEPTH, 2, D_MODEL), F32)], axis=1)

    wi_b, wo_b = w_in.astype(BF16), w_out.astype(BF16)

    def regroup(wi_all):
        wi_full = jnp.concatenate([wi_all[k] for k in range(N_CHIPS)], axis=-1)
        wc_l = jnp.concatenate([wi_full[:, 256:768], wi_full[:, 1024:1536]], axis=-1)
        wr_l = jnp.concatenate([wi_full[:, 0:256], wi_full[:, 768:1024], wi_full[:, 1536:2560], wi_full[:, 3072:3328]], axis=-1)
        return wc_l, wr_l, wi_full[:, 2560:3072]

    w_c, w_r, w_q, wo_full = [None] * DEPTH, [None] * DEPTH, [None] * DEPTH, [None] * DEPTH
    w_c[0], w_r[0], w_q[0] = regroup(_gather_weights((wi_b[0],))[0])

    cos_t, sin_t = _rope_tables(s_lat)
    conv_a_full = jnp.zeros((DEPTH, 8, GROUP_W), F32)
    conv_b_full = jnp.zeros((DEPTH, 32, GROUP_W), F32)
    conv_small = jnp.concatenate([conv_a.reshape(DEPTH * SHORT_CONV_K, -1), conv_b.reshape(DEPTH * CONFORMER_K, -1)], axis=0)
    n_cs = conv_small.shape[0]
    conv_rows = -(-n_cs // 8) * 8
    conv_all = _all_gather_rows(_pad_rows(conv_small, conv_rows)).reshape(8, conv_rows, -1)[::2]
    conv_all = jnp.transpose(conv_all, (1, 0, 2)).reshape(conv_rows, GROUP_W)
    conv_a_full = conv_a_full.at[:, :SHORT_CONV_K].set(conv_all[:DEPTH * SHORT_CONV_K].reshape(DEPTH, SHORT_CONV_K, GROUP_W))
    conv_b_full = conv_b_full.at[:, :CONFORMER_K].set(
        conv_all[DEPTH * SHORT_CONV_K:n_cs].reshape(DEPTH, CONFORMER_K, GROUP_W))

    vecs = jnp.stack([conv_b_bias, conf_ln_g, conf_ln_b, sgu_ln_g, sgu_ln_b] + [jnp.zeros_like(conv_b_bias)] * 3, axis=1)
    wss = w_s.reshape(DEPTH, N_SPATIAL_GROUPS * CHUNK, CHUNK).astype(BF16)
    wsts = jnp.swapaxes(w_s, 2, 3).reshape(DEPTH, N_SPATIAL_GROUPS * CHUNK, CHUNK).astype(BF16)
    bsm = jnp.repeat(jnp.swapaxes(b_s, 1, 2), HEAD_DIM, axis=2)
    qk_gain = jnp.concatenate([jnp.tile(q_gain, (1, 2))[:, None, :], jnp.tile(k_gain, (1, 2))[:, None, :],
                               jnp.zeros((DEPTH, 6, LANES), F32)], axis=1)

    xt = jnp.concatenate([ctx[0], x[0]], axis=0)
    saved = []
    for l in range(DEPTH):
        h, pc, pr, pq, q, k, v = _in_proj(xt, modv[l], g_pre[l][None, :], w_c[l], w_r[l], w_q[l], qk_gain[l], cos_t, sin_t)
        oe, gathered = _attention_fwd(q, k, v, (wo_b[l],) + ((wi_b[l + 1],) if l + 1 < DEPTH else ()))
        wo_full[l] = jnp.concatenate([gathered[0][k] for k in range(N_CHIPS)], axis=0)
        if l + 1 < DEPTH:
            w_c[l + 1], w_r[l + 1], w_q[l + 1] = regroup(gathered[1])
        mixed = _mix_out(pc, pr, oe, xt, modv[l], g_post[l][None, :], wo_full[l], conv_a_full[l], conv_b_full[l],
                         vecs[l], wss[l], bsm[l], loss_target[0] if l + 1 == DEPTH else None)
        x_new, y, ca, z2 = mixed[:4]
        saved.append(dict(x=xt, h=h, pc=pc, pr=pr, pq=pq, q=q, k=k, v=v, oe=oe, y=y, ca=ca, z2=z2))
        xt = x_new
    dxo = xt
    loss = lax.psum(mixed[4][0, 0], ("x", "y", "c"))

    g_small = {n: [None] * DEPTH for n in _SMALL}
    d_mod, landed_in, landed_out = [None] * DEPTH, [None] * DEPTH, [None] * DEPTH
    slab_in = None
    for l in reversed(range(DEPTH)):
        s = saved[l]
        dpr, g_a, g_b, doe, gw_o, pvec, s256, dws, dbs = _mix_out_bwd(
            dxo, s["y"], s["pr"], s["ca"], s["z2"], s["oe"], modv[l], g_post[l][None, :], wo_full[l], vecs[l], wss[l], wsts[l], bsm[l])
        slab_out = gw_o.reshape(N_CHIPS, SHARD_OUT, D_MODEL).astype(BF16)
        dq, dk, dv, (dpc, dca, dcb), got = _attention_bwd(
            s["q"], s["k"], s["v"], doe, (slab_out,) + (() if slab_in is None else (slab_in,)),
            (s["pc"], g_a, g_b, conv_a_full[l], conv_b_full[l]))
        landed_out[l] = got[0]
        if slab_in is not None:
            landed_in[l + 1] = got[1]
        dxo, dpq, acc, dgain = _in_proj_bwd(dpc, dpr, dq, dk, dv, s["pq"], qk_gain[l], cos_t, sin_t, w_c[l], w_r[l], w_q[l],
                                            s["x"], dxo, modv[l], g_pre[l][None, :])
        gw_c, gw_r, gw_q = _in_proj_wgrad(s["h"], dpc, dpr, dpq)
        gw_in = jnp.concatenate([gw_r[:, 0:256], gw_c[:, 0:512], gw_r[:, 256:512], gw_c[:, 512:1024],
                                 gw_r[:, 512:1536], gw_q, gw_r[:, 1536:1792]], axis=-1)
        slab_in = jnp.transpose(gw_in.reshape(D_MODEL, N_CHIPS, SHARD_IN), (1, 0, 2)).astype(BF16)
        d_mod[l] = jnp.stack([jnp.concatenate([acc[2], acc[3], pvec[1]]), jnp.concatenate([acc[0], acc[1], pvec[0]])])
        g_small["g_pre"][l] = acc[4]
        g_small["g_post"][l] = pvec[2]
        g_small["conv_a"][l] = dca[:SHORT_CONV_K]
        g_small["conv_b"][l] = dcb[:CONFORMER_K]
        g_small["conv_b_bias"][l] = s256[0]
        g_small["conf_ln_g"][l] = s256[1]
        g_small["conf_ln_b"][l] = s256[2]
        g_small["sgu_ln_g"][l] = s256[3]
        g_small["sgu_ln_b"][l] = s256[4]
        g_small["w_s"][l] = dws.reshape(N_SPATIAL_GROUPS, CHUNK, CHUNK)
        g_small["b_s"][l] = jnp.transpose(dbs[:, :N_SPATIAL_GROUPS])
        g_small["q_gain"][l] = dgain[0, :HEAD_DIM] + dgain[0, HEAD_DIM:]
        g_small["k_gain"][l] = dgain[1, :HEAD_DIM] + dgain[1, HEAD_DIM:]
    grad_x = dxo[CTX_LEN:][None]

    d_mod_all = _all_gather_rows(jnp.stack(d_mod).reshape(DEPTH * 2, 3 * D_MODEL)).reshape(8, DEPTH, 2, 3 * D_MODEL)
    d_lat = jnp.transpose(d_mod_all[:, :, 0], (1, 0, 2))
    d_ctx = jnp.transpose(d_mod_all[:, :, 1], (1, 0, 2))
    cols = lambda a: lax.dynamic_slice_in_dim(a.reshape(DEPTH, 8, N_CHIPS, SHARD_MOD), chip, 1, axis=2)[:, :, 0]
    silu_t = jnp.transpose(silu_c)
    s_t = jnp.concatenate([silu_t[:, 0:8], jnp.tile(silu_t[:, 8:9], (1, 8)), jnp.zeros((D_MODEL, LANES - 16), F32)], axis=1)
    g_rows = jnp.concatenate([cols(d_lat), cols(d_ctx), jnp.zeros((DEPTH, LANES - 16, SHARD_MOD), F32)], axis=1)
    g_w_mod, g_b_mod, c_ctx_part = _mod_backward(s_t, g_rows, cols(d_ctx), jnp.concatenate([d_lat, d_ctx], axis=1),
                                                 w_mod, c_ctx[:, None])

    for n in _SMALL:
        if n not in ("c_ctx", "b_mod"):
            g_small[n] = jnp.stack(g_small[n])
    small_parts = [0.5 * c_ctx_part[:, 0]] + [g_small[n] for n in _SMALL[2:]]
    packed = _pack(small_parts)
    gathered = _all_gather_rows(packed.astype(BF16)).reshape(8, packed.shape[0], LANES)
    small_sum = _sum_slabs(gathered, packed.shape[0])
    small_g = dict(zip(("c_ctx",) + _SMALL[2:], _unpack(small_sum, [p.shape for p in small_parts])))
    small_g["b_mod"] = g_b_mod[:, 0]
    ch64 = GROUP_W // N_CHIPS
    for n in ("conv_a", "conv_b"):
        small_g[n] = lax.dynamic_slice_in_dim(small_g[n], chip * ch64, ch64, axis=2)
    sw = _pack([weights[n] for n in _SMALL])
    sm = _pack([m_in[n] for n in _SMALL])
    sv = _pack([v_in[n] for n in _SMALL])
    sg = _pack([small_g[n] for n in _SMALL])
    shapes = [weights[n].shape for n in _SMALL]
    small_out = [dict(zip(_SMALL, _unpack(o, shapes))) for o in _adamw([sg], sw, sm, sv, sg.shape[0])]

    landed_in[0] = _scatter_slabs((slab_in,))[0]
    sum_in = _sum_layer_slabs(landed_in, 512)
    sum_out = _sum_layer_slabs(landed_out, 256)
    sib_in, sib_out = _swap_with_sibling(sum_in, sum_out)

    big = {}
    flat = lambda a: a.reshape(-1, a.shape[-1])
    for n, grads, rows in (("w_in", [sum_in, sib_in], 512), ("w_out", [sum_out, sib_out], 256), ("w_mod", [flat(g_w_mod)], 512)):
        outs = _adamw(grads, flat(weights[n]), flat(m_in[n]), flat(v_in[n]), rows)
        big[n] = [o.reshape(weights[n].shape) for o in outs]

    def leaf(n, j):
        return big[n][j] if n in big else small_out[j][n]

    return (loss, grad_x, *[leaf(n, 0) for n in order], *[leaf(n, 1) for n in order],
            *[leaf(n, 2) for n in order], *[leaf(n, 3) for n in order])
```

```python
import functools

import numpy as np
import jax
import jax.numpy as jnp
from jax import lax
from jax.experimental import pallas as pl
from jax.experimental.pallas import tpu as pltpu

F32 = jnp.float32
BF16 = jnp.bfloat16
MESH = pl.DeviceIdType.MESH

D_MODEL = 1024
DEPTH = 4
GRID_W = 64
CTX_LEN = 256
GROUP_W = 256
HEAD_DIM = 64
N_Q_HEADS = 4
N_KV_HEADS = 2
GQA = N_Q_HEADS // N_KV_HEADS
ROPE_THETA = 10000.0
ATTN_SCALE = HEAD_DIM ** -0.5
SHORT_CONV_K = 3
CONFORMER_K = 31
CHUNK = 128
N_SPATIAL_GROUPS = 4
RMS_EPS = 1e-6
LN_EPS = 1e-5
ADAM_LR = 0.001
ADAM_B1 = 0.9
ADAM_B2 = 0.999
ADAM_EPS = 1e-08
ADAM_WD = 0.01
ADAM_STEP = 10

LANES = 128
HALO = 16
CONV_ROWS = 64
TM = 256
N_CTX_TILES = CTX_LEN // TM
SUB = 3
TB = SUB * TM
W_C = 1024
W_R = 1792
W_Q = 512
PROJ_W = W_C + W_R + W_Q
N_CHIPS = 4
SHARD_IN = PROJ_W // N_CHIPS
SHARD_OUT = D_MODEL // N_CHIPS
SHARD_MOD = 3 * D_MODEL // N_CHIPS
VMEM_LIMIT = 56 * 1024 * 1024
VMEM_LIMIT_WIDE = 60 * 1024 * 1024


def _pc(body, **kw):
    return pl.pallas_call(body, **kw)


def _cparams(**kw):
    return pltpu.CompilerParams(dimension_semantics=("arbitrary",), vmem_limit_bytes=VMEM_LIMIT, **kw)


def _full(shape):
    n = len(shape)
    return pl.BlockSpec(shape, lambda i: (0,) * n)


def _const(shape):
    n = len(shape)
    return pl.BlockSpec(shape, lambda i: (0,) * n, pipeline_mode=pl.Buffered(1))


def _rows(width, tm=TM):
    return pl.BlockSpec((tm, width), lambda i: (i, 0))


def _heads(nh, width, tm=TM):
    return pl.BlockSpec((nh, tm, width), lambda i: (0, i, 0))


def _sigmoid(x):
    return jax.nn.sigmoid(x)


def _dot(a, b):
    return jnp.dot(a, b, preferred_element_type=F32)


def _dot_nt(a, b):
    return lax.dot_general(a, b, (((1,), (1,)), ((), ())), preferred_element_type=F32)


def _dot_tn(a, b):
    return lax.dot_general(a, b, (((0,), (0,)), ((), ())), preferred_element_type=F32)


def _lane(rows):
    return lax.broadcasted_iota(jnp.int32, (rows, LANES), 1)


def _rowsum(x):
    return jnp.sum(x, axis=1, keepdims=True)


def _colsum(x):
    return jnp.sum(x, axis=0, keepdims=True)


def _pair_sums(x, lo):
    s0 = _rowsum(jnp.where(lo, x, 0.0))
    s1 = _rowsum(jnp.where(lo, 0.0, x))
    return jnp.where(lo, s0, s1)


def _swap16(x, lo16):
    return jnp.where(lo16, pltpu.roll(x, LANES - 16, 1), pltpu.roll(x, 16, 1))


def _layer_norm_stats(x):
    mu = jnp.mean(x, axis=1, keepdims=True)
    xc = x - mu
    rs = lax.rsqrt(jnp.mean(xc * xc, axis=1, keepdims=True) + LN_EPS)
    return xc * rs, rs


def _layer_norm_bwd(dxn, xn, rs):
    return rs * (dxn - jnp.mean(dxn, axis=1, keepdims=True) - xn * jnp.mean(dxn * xn, axis=1, keepdims=True))


def _group_select(r, grp):
    out = jnp.where(grp == 0, r[0:CHUNK], 0.0)
    for g in range(1, N_SPATIAL_GROUPS):
        out = out + jnp.where(grp == g, r[g * CHUNK:(g + 1) * CHUNK], 0.0)
    return out


def _kv_chunk(t):
    return 1024 if (t - CTX_LEN) % 1024 == 0 else 256


def _all_gather_rows(x_shard):
    m_per, n = x_shard.shape

    def body(x_ref, out_ref, send_sems, recv_sems, local_sem):
        x, y, c = lax.axis_index("x"), lax.axis_index("y"), lax.axis_index("c")
        me, sibling = (x, y, c), (x, y, 1 - c)
        chips = [(1 - x, y), (x, 1 - y), (1 - x, 1 - y)]

        def rows(px, py, pc):
            return out_ref.at[pl.ds((4 * px + 2 * py + pc) * m_per, m_per), :]

        def copy(k, block, to, src=None):
            return pltpu.make_async_remote_copy(
                src_ref=rows(*block) if src is None else src, dst_ref=rows(*block),
                send_sem=send_sems.at[k], recv_sem=recv_sems.at[k], device_id=to, device_id_type=MESH)

        mine = pltpu.make_async_copy(x_ref, rows(*me), local_sem)
        mine.start()
        first = [copy(0, me, sibling, src=x_ref)]
        first += [copy(1 + j, me, (*chip, c), src=x_ref) for j, chip in enumerate(chips)]
        for cp in first:
            cp.start()
        passed = [copy(4 + j, (*chip, c), sibling) for j, chip in enumerate(chips)]
        for j, chip in enumerate(chips):
            copy(1 + j, (*chip, c), me).wait_recv()
            passed[j].start()
        copy(0, sibling, me).wait_recv()
        for j, chip in enumerate(chips):
            copy(4 + j, (*chip, 1 - c), me).wait_recv()
        for cp in first + passed:
            cp.wait_send()
        mine.wait()

    return _pc(
        body, name="all_gather_rows",
        out_shape=jax.ShapeDtypeStruct((8 * m_per, n), x_shard.dtype),
        in_specs=[pl.BlockSpec(memory_space=pltpu.VMEM)],
        out_specs=pl.BlockSpec(memory_space=pltpu.VMEM),
        scratch_shapes=[pltpu.SemaphoreType.DMA((7,)), pltpu.SemaphoreType.DMA((7,)), pltpu.SemaphoreType.DMA],
        compiler_params=pltpu.CompilerParams(vmem_limit_bytes=VMEM_LIMIT),
    )(x_shard)


def _place():
    x, y, c = lax.axis_index("x"), lax.axis_index("y"), lax.axis_index("c")
    return x, y, c, [(1 - x, y), (x, 1 - y), (1 - x, 1 - y)]


def _remote(src, dst, send_sems, recv_sems, k, to):
    return pltpu.make_async_remote_copy(src_ref=src, dst_ref=dst, send_sem=send_sems.at[k], recv_sem=recv_sems.at[k],
                                        device_id=to, device_id_type=MESH)


GATHER_SEMS = 6
SCATTER_SEMS = 3


def _gather_phase(phase, pairs, send_sems, recv_sems, local_sems):
    x, y, c, chips = _place()
    kme = 2 * x + y
    sibling = (x, y, 1 - c)
    for a, (src, dst) in enumerate(pairs):
        half = src.shape[0] // 2
        mine = pl.ds(c * half, half)
        theirs = pl.ds((1 - c) * half, half)
        if phase == 0:
            pltpu.make_async_copy(src, dst.at[kme], local_sems.at[a]).start()
        if phase == 2:
            pltpu.make_async_copy(src, dst.at[kme], local_sems.at[a]).wait()
        for j, (px, py) in enumerate(chips):
            kk = 2 * px + py
            landed = dst.at[kk, mine]
            out = lambda: _remote(src.at[mine], dst.at[kme, mine], send_sems, recv_sems, 6 * a + j, (px, py, c))
            hand = lambda: _remote(landed, landed, send_sems, recv_sems, 6 * a + 3 + j, sibling)
            if phase == 0:
                out().start()
            if phase == 1:
                _remote(landed, landed, send_sems, recv_sems, 6 * a + j, (px, py, c)).wait_recv()
                hand().start()
            if phase == 2:
                other = dst.at[kk, theirs]
                _remote(other, other, send_sems, recv_sems, 6 * a + 3 + j, sibling).wait_recv()
                out().wait_send()
                hand().wait_send()


def _scatter_phase(phase, pairs, send_sems, recv_sems, local_sems):
    x, y, c, chips = _place()
    kme = 2 * x + y
    for a, (src, dst) in enumerate(pairs):
        loc = pltpu.make_async_copy(src.at[kme], dst.at[kme], local_sems.at[a])
        if phase == 0:
            loc.start()
        else:
            loc.wait()
        for j, (px, py) in enumerate(chips):
            kk = 2 * px + py
            out = _remote(src.at[kk], dst.at[kme], send_sems, recv_sems, 3 * a + j, (px, py, c))
            if phase == 0:
                out.start()
            else:
                landed = dst.at[kk]
                _remote(landed, landed, send_sems, recv_sems, 3 * a + j, (px, py, c)).wait_recv()
                out.wait_send()


def _comm_scratch(per_array, n_arrays):
    n = per_array * n_arrays
    return [pltpu.SemaphoreType.DMA((n,)), pltpu.SemaphoreType.DMA((n,)), pltpu.SemaphoreType.DMA((n_arrays,))]


def _slots(a):
    return jax.ShapeDtypeStruct((N_CHIPS,) + a.shape, a.dtype)


def _gather_weights(shards):
    n = len(shards)

    def body(*refs):
        for phase in range(3):
            _gather_phase(phase, tuple(zip(refs[:n], refs[n:2 * n])), *refs[2 * n:])

    hbm = pl.BlockSpec(memory_space=pl.ANY)
    return _pc(
        body, name="gather_weights", out_shape=tuple(_slots(a) for a in shards),
        in_specs=[hbm] * n, out_specs=(hbm,) * n, scratch_shapes=_comm_scratch(GATHER_SEMS, n),
    )(*shards)


def _scatter_slabs(slabs):
    n = len(slabs)

    def body(*refs):
        for phase in range(2):
            _scatter_phase(phase, tuple(zip(refs[:n], refs[n:2 * n])), *refs[2 * n:])

    hbm = pl.BlockSpec(memory_space=pl.ANY)
    return _pc(
        body, name="scatter_slabs", out_shape=tuple(jax.ShapeDtypeStruct(a.shape, a.dtype) for a in slabs),
        in_specs=[hbm] * n, out_specs=(hbm,) * n, scratch_shapes=_comm_scratch(SCATTER_SEMS, n),
    )(*slabs)


def _swap_with_sibling(a, b):
    def body(a_ref, b_ref, ra_ref, rb_ref, send_sems, recv_sems):
        x, y, c = lax.axis_index("x"), lax.axis_index("y"), lax.axis_index("c")
        copies = []
        for k, (src, dst) in enumerate(((a_ref, ra_ref), (b_ref, rb_ref))):
            cp = pltpu.make_async_remote_copy(
                src_ref=src, dst_ref=dst, send_sem=send_sems.at[k], recv_sem=recv_sems.at[k],
                device_id=(x, y, 1 - c), device_id_type=MESH)
            cp.start()
            copies.append(cp)
        for cp in copies:
            cp.wait()

    hbm = pl.BlockSpec(memory_space=pl.ANY)
    return _pc(
        body, name="swap_with_sibling",
        out_shape=(jax.ShapeDtypeStruct(a.shape, a.dtype), jax.ShapeDtypeStruct(b.shape, b.dtype)),
        in_specs=[hbm, hbm], out_specs=(hbm, hbm),
        scratch_shapes=[pltpu.SemaphoreType.DMA((2,)), pltpu.SemaphoreType.DMA((2,))],
    )(a, b)


def _mod_forward(c16, w_mod, b_mod_shard):
    def body(c_ref, w_ref, b_ref, s_ref, o_ref):
        cc = c_ref[...]
        s = cc * _sigmoid(cc)
        s_ref[...] = s
        o_ref[0] = jnp.dot(s, w_ref[0], preferred_element_type=F32, precision=lax.Precision.HIGHEST) + b_ref[0]

    return _pc(
        body, name="mod_forward", grid=(DEPTH,),
        out_shape=(jax.ShapeDtypeStruct((16, D_MODEL), F32), jax.ShapeDtypeStruct((DEPTH, 16, SHARD_MOD), F32)),
        in_specs=[_full((16, D_MODEL)),
                  pl.BlockSpec((1, D_MODEL, SHARD_MOD), lambda l: (l, 0, 0)),
                  pl.BlockSpec((1, 1, SHARD_MOD), lambda l: (l, 0, 0))],
        out_specs=(_full((16, D_MODEL)), pl.BlockSpec((1, 16, SHARD_MOD), lambda l: (l, 0, 0))),
        compiler_params=_cparams(),
    )(c16, w_mod, b_mod_shard)


def _mod_backward(s_t, g_rows, g_ctx, d_all, w_mod, c_ctx_col):
    def body(st_ref, g_ref, gc_ref, d_ref, w_ref, cc_ref, gw_ref, gb_ref, pc_ref):
        l = pl.program_id(0)
        gw_ref[0] = jnp.dot(st_ref[...], g_ref[0], preferred_element_type=F32, precision=lax.Precision.HIGHEST)
        gb_ref[0] = _colsum(d_ref[0])
        part = _rowsum(w_ref[0] * _colsum(gc_ref[0]))

        @pl.when(l == 0)
        def _():
            pc_ref[...] = jnp.zeros_like(pc_ref)

        pc_ref[...] += part

        @pl.when(l == DEPTH - 1)
        def _():
            cc = cc_ref[...]
            sg = _sigmoid(cc)
            pc_ref[...] = pc_ref[...] * (sg * (1.0 + cc * (1.0 - sg)))

    return _pc(
        body, name="mod_backward", grid=(DEPTH,),
        out_shape=(jax.ShapeDtypeStruct((DEPTH, D_MODEL, SHARD_MOD), F32),
                   jax.ShapeDtypeStruct((DEPTH, 1, 3 * D_MODEL), F32),
                   jax.ShapeDtypeStruct((D_MODEL, 1), F32)),
        in_specs=[_full((D_MODEL, LANES)),
                  pl.BlockSpec((1, LANES, SHARD_MOD), lambda l: (l, 0, 0)),
                  pl.BlockSpec((1, 8, SHARD_MOD), lambda l: (l, 0, 0)),
                  pl.BlockSpec((1, 16, 3 * D_MODEL), lambda l: (l, 0, 0)),
                  pl.BlockSpec((1, D_MODEL, SHARD_MOD), lambda l: (l, 0, 0)),
                  _full((D_MODEL, 1))],
        out_specs=(pl.BlockSpec((1, D_MODEL, SHARD_MOD), lambda l: (l, 0, 0)),
                   pl.BlockSpec((1, 1, 3 * D_MODEL), lambda l: (l, 0, 0)),
                   _full((D_MODEL, 1))),
        compiler_params=_cparams(),
    )(s_t, g_rows, g_ctx, d_all, w_mod, c_ctx_col)


def _head_norm(xb, lo):
    r = lax.rsqrt(_pair_sums(xb * xb, lo) * (1.0 / HEAD_DIM) + RMS_EPS)
    return xb * r, r


def _in_proj(xt, modv, g_pre, w_c, w_r, w_q, qk_gain, cos_t, sin_t):
    t = xt.shape[0]

    def body(x_ref, mod_ref, g_ref, wc_ref, wr_ref, wq_ref, gain_ref, cos_ref, sin_ref,
             h_ref, pc_ref, pr_ref, pq_ref, q_ref, k_ref, v_ref):
        lane = _lane(TM)
        lo = lane < HEAD_DIM
        lo16 = (lane & 31) < 16
        one = jnp.where(lane == HEAD_DIM, 1.0, 0.0)
        for jj in range(SUB):
            rows = pl.ds(jj * TM, TM)
            is_ctx = pl.program_id(0) * SUB + jj < N_CTX_TILES
            x = x_ref[rows, :]
            r = lax.rsqrt(jnp.mean(x * x, axis=1, keepdims=True) + RMS_EPS)
            sh = jnp.where(is_ctx, mod_ref[0:1, :], mod_ref[3:4, :])
            sc = jnp.where(is_ctx, mod_ref[1:2, :], mod_ref[4:5, :])
            h = (x * r * g_ref[...]) * (1.0 + sc) + sh
            hb = h.astype(BF16)
            h_ref[rows, :] = hb
            pc_ref[rows, :] = _dot(hb, wc_ref[...])
            pr_ref[rows, :] = _dot(hb, wr_ref[...])
            pq = _dot(hb, wq_ref[...])
            pq_ref[rows, :] = pq
            cos = cos_ref[rows, :]
            sin = sin_ref[rows, :]
            for b in range(3):
                xh, _ = _head_norm(pq[:, b * LANES:(b + 1) * LANES], lo)
                xg = xh * (gain_ref[0:1, :] if b < 2 else gain_ref[1:2, :])
                rot = xg * cos + _swap16(xg, lo16) * sin
                if b < 2:
                    rot = rot * ATTN_SCALE
                dst = q_ref if b < 2 else k_ref
                base = 2 * b if b < 2 else 0
                dst[base, rows, :] = jnp.where(lo, rot, 0.0).astype(BF16)
                dst[base + 1, rows, :] = jnp.where(lo, pltpu.roll(rot, HEAD_DIM, 1), 0.0).astype(BF16)
            vb = pq[:, 3 * LANES:4 * LANES]
            v_ref[0, rows, :] = jnp.where(lo, vb, one).astype(BF16)
            v_ref[1, rows, :] = jnp.where(lo, pltpu.roll(vb, HEAD_DIM, 1), one).astype(BF16)

    return _pc(
        body, name="in_proj", grid=(t // TB,),
        out_shape=(jax.ShapeDtypeStruct((t, D_MODEL), BF16),
                   jax.ShapeDtypeStruct((t, W_C), F32), jax.ShapeDtypeStruct((t, W_R), F32), jax.ShapeDtypeStruct((t, W_Q), F32),
                   jax.ShapeDtypeStruct((N_Q_HEADS, t, LANES), BF16),
                   jax.ShapeDtypeStruct((N_KV_HEADS, t, LANES), BF16),
                   jax.ShapeDtypeStruct((N_KV_HEADS, t, LANES), BF16)),
        in_specs=[_rows(D_MODEL, TB), _const((8, D_MODEL)), _const((1, D_MODEL)),
                  _const((D_MODEL, W_C)), _const((D_MODEL, W_R)), _const((D_MODEL, W_Q)),
                  _const((8, LANES)), _rows(LANES, TB), _rows(LANES, TB)],
        out_specs=(_rows(D_MODEL, TB), _rows(W_C, TB), _rows(W_R, TB), _rows(W_Q, TB),
                   _heads(N_Q_HEADS, LANES, TB), _heads(N_KV_HEADS, LANES, TB), _heads(N_KV_HEADS, LANES, TB)),
        compiler_params=_cparams(),
    )(xt, modv, g_pre, w_c, w_r, w_q, qk_gain, cos_t, sin_t)


def _attention_fwd(q, k, v, shards=None):
    t = q.shape[1]
    tk = _kv_chunk(t)
    n_chunks = (t - CTX_LEN) // tk
    n_tiles = t // TM
    n_sh = 0 if shards is None else len(shards)

    def body(q_ref, k_ref, v_ref, *rest):
        i = pl.program_id(0)
        o_ref = rest[n_sh]
        if shards is not None:
            pairs = tuple(zip(rest[:n_sh], rest[n_sh + 1:2 * n_sh + 1]))
            for phase, at in enumerate((0, n_tiles // 2, n_tiles - 1)):
                @pl.when(i == at)
                def _(phase=phase):
                    _gather_phase(phase, pairs, *rest[2 * n_sh + 1:])
        lane = _lane(GQA * TM)
        qs = [jnp.concatenate([q_ref[GQA * g + hh] for hh in range(GQA)], axis=0) for g in range(N_KV_HEADS)]

        def step(st, size, carry):
            out = []
            for g in range(N_KV_HEADS):
                m, acc = carry[g]
                s = _dot_nt(qs[g], k_ref[g, pl.ds(st, size), :])
                m_new = jnp.maximum(m, jnp.max(s, axis=1, keepdims=True))
                p = jnp.exp(s - m_new)
                out.append((m_new, acc * jnp.exp(m - m_new) + _dot(p.astype(BF16), v_ref[g, pl.ds(st, size), :])))
            return tuple(out)

        init = tuple((jnp.full((GQA * TM, 1), -jnp.inf, F32), jnp.zeros((GQA * TM, LANES), F32)) for _ in range(N_KV_HEADS))

        def finish(carry):
            for g in range(N_KV_HEADS):
                m, acc = carry[g]
                den = _rowsum(jnp.where(lane == HEAD_DIM, acc, 0.0))
                out = jnp.where(lane < HEAD_DIM, acc * (1.0 / den), jnp.where(lane == HEAD_DIM, m + jnp.log(den), 0.0))
                for hh in range(GQA):
                    o_ref[GQA * g + hh] = out[hh * TM:(hh + 1) * TM]

        @pl.when(i < N_CTX_TILES)
        def _():
            finish(step(0, CTX_LEN, init))

        @pl.when(i >= N_CTX_TILES)
        def _():
            per = 8 if n_chunks % 8 == 0 else 1

            def trip(j, cr):
                st = pl.multiple_of(CTX_LEN + j * (per * tk), 256)
                for u in range(per):
                    cr = step(st + u * tk, tk, cr)
                return cr

            finish(lax.fori_loop(0, n_chunks // per, trip, step(0, CTX_LEN, init)))

    hbm = pl.BlockSpec(memory_space=pl.ANY)
    extra = () if shards is None else tuple(shards)
    outs = _pc(
        body, name="attention_fwd" if shards is None else "attention_fwd_gather", grid=(n_tiles,),
        out_shape=(jax.ShapeDtypeStruct((N_Q_HEADS, t, LANES), F32),) + tuple(_slots(a) for a in extra),
        in_specs=[_heads(N_Q_HEADS, LANES), _full((N_KV_HEADS, t, LANES)), _full((N_KV_HEADS, t, LANES))] + [hbm] * n_sh,
        out_specs=(_heads(N_Q_HEADS, LANES),) + (hbm,) * n_sh,
        scratch_shapes=_comm_scratch(GATHER_SEMS, n_sh) if shards is not None else [],
        compiler_params=_cparams(),
    )(q, k, v, *extra)
    return outs[0], tuple(outs[1:])


def _halo_specs(width, t, rows=TM):
    last = t // HALO - 1
    per = rows // HALO
    prev = pl.BlockSpec((HALO, width), lambda i: (jnp.maximum(i * per - 1, 0), 0))
    nxt = pl.BlockSpec((HALO, width), lambda i: (jnp.minimum((i + 1) * per, last), 0))
    return prev, nxt


def _halo_valid(i, n_tiles):
    prev_ok = jnp.logical_and(i != 0, i != N_CTX_TILES)
    next_ok = jnp.logical_and(i != N_CTX_TILES - 1, i != n_tiles - 1)
    return jnp.where(prev_ok, 1.0, 0.0), jnp.where(next_ok, 1.0, 0.0)


def _conv_inputs(pc):
    u = pc[:, 0:GROUP_W] * pc[:, GROUP_W:2 * GROUP_W]
    z = pc[:, 2 * GROUP_W:3 * GROUP_W] * _sigmoid(pc[:, 3 * GROUP_W:4 * GROUP_W])
    return u, z


def _fill_ext(ext_ref, prev, mid, nxt):
    ext_ref[0:HALO, :] = prev
    ext_ref[HALO:HALO + TM, :] = mid
    ext_ref[HALO + TM:HALO + TM + HALO, :] = nxt


def _row_local_mixers(pr, ca, z2, oe, vecs, wss_ref, bsm, lane256):
    a_b, a_g, b_g = pr[:, 0:256], pr[:, 256:512], pr[:, 512:768]
    c_u, c_v, c_g, d_g = pr[:, 768:1024], pr[:, 1024:1280], pr[:, 1280:1536], pr[:, 1536:1792]
    zn, rs_b = _layer_norm_stats(z2)
    tb = zn * vecs[1:2, :] + vecs[2:3, :]
    vn_hat, rs_c = _layer_norm_stats(c_v)
    vn = vn_hat * vecs[3:4, :] + vecs[4:5, :]
    grp = jnp.right_shift(lane256, 6)
    sgs = []
    for ch in range(TM // CHUNK):
        r = _dot(wss_ref[...], vn[ch * CHUNK:(ch + 1) * CHUNK, :].astype(BF16))
        sgs.append(_group_select(r, grp[0:CHUNK]) + bsm)
    sg = jnp.concatenate(sgs, axis=0)
    lane = _lane(TM)
    lo = lane < HEAD_DIM
    att = jnp.concatenate([jnp.where(lo, oe[2 * b], pltpu.roll(oe[2 * b + 1], HEAD_DIM, 1)) for b in range(2)], axis=1)
    return dict(a_b=a_b, a_g=a_g, b_g=b_g, c_u=c_u, c_v=c_v, c_g=c_g, d_g=d_g, zn=zn, rs_b=rs_b, tb=tb,
                vn_hat=vn_hat, rs_c=rs_c, vn=vn, sg=sg, att=att, grp=grp, lo=lo, lane=lane)


def _mixer_concat(f, ca):
    ya = f["a_b"] * ca
    yb = f["tb"] * _sigmoid(f["tb"])
    yc = f["c_u"] * f["sg"]
    gates = [f[n] * _sigmoid(f[n]) for n in ("a_g", "b_g", "c_g", "d_g")]
    ys = (ya, yb, yc, f["att"])
    big = jnp.concatenate([yy * gg for yy, gg in zip(ys, gates)], axis=1).astype(BF16)
    return big, ys, gates


def _taps31(ext_ref, w_ref, flip):
    blocks = []
    for r0 in range(0, TM, CONV_ROWS):
        out = None
        for b in range(8):
            part = None
            for a in range(4):
                o = 8 * a + b
                if 1 <= o <= CONFORMER_K:
                    kk = CONFORMER_K - o if flip else o - 1
                    term = w_ref[kk:kk + 1, :] * ext_ref[pl.ds(r0 + 8 * a, CONV_ROWS + 8), :]
                    part = term if part is None else part + term
            part = part[b:b + CONV_ROWS]
            out = part if out is None else out + part
        blocks.append(out)
    return jnp.concatenate(blocks, axis=0)


def _mix_out(pc, pr, oe, xt, modv, g_post, w_out, conv_a, conv_b, vecs, wss, bsm, target=None):
    t = xt.shape[0]
    n_tiles = t // TM
    prev_spec, next_spec = _halo_specs(W_C, t, TB)
    n_t = 0 if target is None else SUB

    def body(pc_ref, pp_ref, pn_ref, pr_ref, oe_ref, x_ref, mod_ref, gp_ref, wo_ref, cva_ref, cvb_ref, vec_ref, wss_ref, bsm_ref,
             *rest):
        xo_ref, y_ref, ca_ref, z2_ref = rest[n_t:n_t + 4]
        uext, zext = rest[-2:]
        i = pl.program_id(0)
        vecs = vec_ref[...]
        lane256 = lax.broadcasted_iota(jnp.int32, (TM, GROUP_W), 1)
        if target is not None:
            loss_ref = rest[n_t + 4]

            @pl.when(i == 0)
            def _():
                loss_ref[...] = jnp.zeros_like(loss_ref)

        for jj in range(SUB):
            rows = pl.ds(jj * TM, TM)
            tile = i * SUB + jj
            is_ctx = tile < N_CTX_TILES
            pv, nv = _halo_valid(tile, n_tiles)
            u, z = _conv_inputs(pc_ref[rows, :])
            up, zp = _conv_inputs(pp_ref[...] if jj == 0 else pc_ref[pl.ds(jj * TM - HALO, HALO), :])
            un, zn_ = _conv_inputs(pn_ref[...] if jj == SUB - 1 else pc_ref[pl.ds((jj + 1) * TM, HALO), :])
            ue, ze = uext.at[jj], zext.at[jj]
            _fill_ext(ue, up * pv, u, un * nv)
            _fill_ext(ze, zp * pv, z, zn_ * nv)
            ca = cva_ref[0:1, :] * ue[pl.ds(HALO - 1, TM), :]
            for kk in range(1, SHORT_CONV_K):
                ca = ca + cva_ref[kk:kk + 1, :] * ue[pl.ds(HALO - 1 + kk, TM), :]
            z2 = _taps31(ze, cvb_ref, False) + vecs[0:1, :]
            ca_ref[rows, :] = ca
            z2_ref[rows, :] = z2
            oes = [oe_ref[h, rows, :] for h in range(N_Q_HEADS)]
            f = _row_local_mixers(pr_ref[rows, :], ca, z2, oes, vecs, wss_ref, bsm_ref[...], lane256)
            big, _, _ = _mixer_concat(f, ca)
            y = _dot(big, wo_ref[...])
            y_ref[rows, :] = y
            ry = lax.rsqrt(jnp.mean(y * y, axis=1, keepdims=True) + RMS_EPS)
            gt = jnp.where(is_ctx, mod_ref[2:3, :], mod_ref[5:6, :])
            x_new = x_ref[rows, :] + gt * (y * ry * gp_ref[...])
            if target is None:
                xo_ref[rows, :] = x_new
            else:
                err = (x_new - rest[jj][...]) * jnp.where(is_ctx, 0.0, 1.0)
                xo_ref[rows, :] = err * (1.0 / D_MODEL)
                loss_ref[...] += jnp.sum(err * err) * (0.5 / D_MODEL)

    rows_f32 = jax.ShapeDtypeStruct((t, D_MODEL), F32)
    group_f32 = jax.ShapeDtypeStruct((t, GROUP_W), F32)
    loss_shape, loss_spec, t_spec, t_arg = (), (), [], ()
    if target is not None:
        loss_shape, loss_spec = (jax.ShapeDtypeStruct((8, LANES), F32),), (_full((8, LANES)),)
        t_spec = [pl.BlockSpec((TM, D_MODEL), lambda i, jj=jj: (jnp.maximum(i * SUB + jj - N_CTX_TILES, 0), 0))
                  for jj in range(SUB)]
        t_arg = (target,) * SUB
    return _pc(
        body, name="mix_out" if target is None else "mix_out_loss", grid=(t // TB,),
        out_shape=(rows_f32, rows_f32, group_f32, group_f32) + loss_shape,
        in_specs=[_rows(W_C, TB), prev_spec, next_spec, _rows(W_R, TB), _heads(N_Q_HEADS, LANES, TB), _rows(D_MODEL, TB),
                  _const((8, D_MODEL)), _const((1, D_MODEL)), _const((D_MODEL, D_MODEL)),
                  _const((8, GROUP_W)), _const((32, GROUP_W)), _const((8, GROUP_W)),
                  _const((N_SPATIAL_GROUPS * CHUNK, CHUNK)), _const((CHUNK, GROUP_W))] + t_spec,
        out_specs=(_rows(D_MODEL, TB), _rows(D_MODEL, TB), _rows(GROUP_W, TB), _rows(GROUP_W, TB)) + loss_spec,
        scratch_shapes=[pltpu.VMEM((SUB, TM + 2 * HALO, GROUP_W), F32), pltpu.VMEM((SUB, TM + 2 * HALO, GROUP_W), F32)],
        compiler_params=_cparams(),
    )(pc, pc, pc, pr, oe, xt, modv, g_post, w_out, conv_a, conv_b, vecs, wss, bsm, *t_arg)


def _mix_out_bwd(dxo, y, pr, ca, z2, oe, modv, g_post, w_out, vecs, wss, wsts, bsm):
    t = y.shape[0]
    n_tiles = t // TM

    def body(dxo_ref, y_ref, pr_ref, ca_ref, z2_ref, oe_ref, mod_ref, gp_ref, wo_ref, vec_ref, wss_ref, wsts_ref, bsm_ref,
             dpr_ref, ga_ref, gb_ref, doe_ref, dwo_hbm, pvec_ref, s256_ref, dws_ref, dbs_ref, dbsm, dwo_ref):
        i = pl.program_id(0)

        @pl.when(i == 0)
        def _():
            dwo_ref[...] = jnp.zeros_like(dwo_ref)
            pvec_ref[...] = jnp.zeros_like(pvec_ref)
            s256_ref[...] = jnp.zeros_like(s256_ref)
            dws_ref[...] = jnp.zeros_like(dws_ref)
            dbsm[...] = jnp.zeros_like(dbsm)

        gp = gp_ref[...]
        vecs = vec_ref[...]
        bsm_ = bsm_ref[...]
        lane256 = lax.broadcasted_iota(jnp.int32, (TM, GROUP_W), 1)
        grp = jnp.right_shift(lane256, 6)

        for jj in range(SUB):
            tile_rows = pl.ds(jj * TM, TM)
            is_ctx = i * SUB + jj < N_CTX_TILES
            dxo_ = dxo_ref[tile_rows, :]
            y_ = y_ref[tile_rows, :]
            ry = lax.rsqrt(jnp.mean(y_ * y_, axis=1, keepdims=True) + RMS_EPS)
            nh = y_ * ry
            gt = jnp.where(is_ctx, mod_ref[2:3, :], mod_ref[5:6, :])
            dgt = _colsum(dxo_ * (nh * gp))
            pvec_ref[0:1, :] += jnp.where(is_ctx, dgt, 0.0)
            pvec_ref[1:2, :] += jnp.where(is_ctx, 0.0, dgt)
            dn = dxo_ * gt
            pvec_ref[2:3, :] += _colsum(dn * nh)
            dnh = dn * gp
            dy = ry * (dnh - nh * jnp.mean(dnh * nh, axis=1, keepdims=True))

            ca_ = ca_ref[tile_rows, :]
            oes = [oe_ref[h, tile_rows, :] for h in range(N_Q_HEADS)]
            f = _row_local_mixers(pr_ref[tile_rows, :], ca_, z2_ref[tile_rows, :], oes, vecs, wss_ref, bsm_, lane256)
            big, ys, gates = _mixer_concat(f, ca_)
            dyb = dy.astype(BF16)
            dwo_ref[...] += _dot_tn(big, dyb)
            dbig = _dot_nt(dyb, wo_ref[...])

            d_y, d_gate = [], []
            for n, (name, yy, gg) in enumerate(zip(("a_g", "b_g", "c_g", "d_g"), ys, gates)):
                dpart = dbig[:, n * GROUP_W:(n + 1) * GROUP_W]
                gx = f[name]
                sg_ = _sigmoid(gx)
                d_y.append(dpart * gg)
                d_gate.append(dpart * yy * (sg_ * (1.0 + gx * (1.0 - sg_))))
            dya, dyb_, dyc, datt = d_y

            d_ab = dya * ca_
            ga_ref[tile_rows, :] = dya * f["a_b"]
            tb = f["tb"]
            sb = _sigmoid(tb)
            dtb = dyb_ * (sb * (1.0 + tb * (1.0 - sb)))
            s256_ref[1:2, :] += _colsum(dtb * f["zn"])
            s256_ref[2:3, :] += _colsum(dtb)
            dz2 = _layer_norm_bwd(dtb * vecs[1:2, :], f["zn"], f["rs_b"])
            gb_ref[tile_rows, :] = dz2
            s256_ref[0:1, :] += _colsum(dz2)
            d_cu = dyc * f["sg"]
            dsg = dyc * f["c_u"]
            dvn_parts = []
            for ch in range(TM // CHUNK):
                rows = slice(ch * CHUNK, (ch + 1) * CHUNK)
                dsg_c = dsg[rows, :]
                dbsm[...] += dsg_c
                vn_c = f["vn"][rows, :].astype(BF16)
                for g in range(N_SPATIAL_GROUPS):
                    masked = jnp.where(grp[0:CHUNK] == g, dsg_c, 0.0).astype(BF16)
                    dws_ref[g * CHUNK:(g + 1) * CHUNK, :] += _dot_nt(masked, vn_c)
                dvn_parts.append(_group_select(_dot(wsts_ref[...], dsg_c.astype(BF16)), grp[0:CHUNK]))
            dvn = jnp.concatenate(dvn_parts, axis=0)
            s256_ref[3:4, :] += _colsum(dvn * f["vn_hat"])
            s256_ref[4:5, :] += _colsum(dvn)
            d_cv = _layer_norm_bwd(dvn * vecs[3:4, :], f["vn_hat"], f["rs_c"])
            lane, lo = f["lane"], f["lo"]
            att = f["att"]
            for b in range(2):
                da = datt[:, b * LANES:(b + 1) * LANES]
                prod = da * att[:, b * LANES:(b + 1) * LANES]
                for hh in range(2):
                    h = 2 * b + hh
                    lse = _rowsum(jnp.where(lane == HEAD_DIM, oes[h], 0.0))
                    delta = _rowsum(jnp.where(lo, prod, 0.0) if hh == 0 else jnp.where(lo, 0.0, prod))
                    dah = da if hh == 0 else pltpu.roll(da, HEAD_DIM, 1)
                    doe_ref[h, tile_rows, :] = jnp.where(
                        lo, dah, jnp.where(lane == HEAD_DIM, delta, jnp.where(lane == HEAD_DIM + 1, lse, 0.0)))

            dpr_ref[tile_rows, :] = jnp.concatenate(
                [d_ab, d_gate[0], d_gate[1], d_cu, d_cv, d_gate[2], d_gate[3]], axis=1).astype(BF16)

        @pl.when(i == t // TB - 1)
        def _():
            acc = dbsm[...]
            lane128 = _lane(CHUNK)
            out = jnp.zeros((CHUNK, LANES), F32)
            for g in range(N_SPATIAL_GROUPS):
                col = _rowsum(jnp.where(grp[0:CHUNK] == g, acc, 0.0))
                out = out + jnp.where(lane128 == g, col, 0.0)
            dbs_ref[...] = out
            pltpu.sync_copy(dwo_ref, dwo_hbm)

    return _pc(
        body, name="mix_out_bwd", grid=(t // TB,),
        out_shape=(jax.ShapeDtypeStruct((t, W_R), BF16),
                   jax.ShapeDtypeStruct((t, GROUP_W), F32), jax.ShapeDtypeStruct((t, GROUP_W), F32),
                   jax.ShapeDtypeStruct((N_Q_HEADS, t, LANES), F32),
                   jax.ShapeDtypeStruct((D_MODEL, D_MODEL), F32),
                   jax.ShapeDtypeStruct((8, D_MODEL), F32),
                   jax.ShapeDtypeStruct((8, GROUP_W), F32),
                   jax.ShapeDtypeStruct((N_SPATIAL_GROUPS * CHUNK, CHUNK), F32),
                   jax.ShapeDtypeStruct((CHUNK, LANES), F32)),
        in_specs=[_rows(D_MODEL, TB), _rows(D_MODEL, TB), _rows(W_R, TB), _rows(GROUP_W, TB), _rows(GROUP_W, TB),
                  _heads(N_Q_HEADS, LANES, TB),
                  _const((8, D_MODEL)), _const((1, D_MODEL)), _const((D_MODEL, D_MODEL)), _const((8, GROUP_W)),
                  _const((N_SPATIAL_GROUPS * CHUNK, CHUNK)), _const((N_SPATIAL_GROUPS * CHUNK, CHUNK)), _const((CHUNK, GROUP_W))],
        out_specs=(_rows(W_R, TB), _rows(GROUP_W, TB), _rows(GROUP_W, TB), _heads(N_Q_HEADS, LANES, TB),
                   pl.BlockSpec(memory_space=pl.ANY), _full((8, D_MODEL)), _full((8, GROUP_W)),
                   _full((N_SPATIAL_GROUPS * CHUNK, CHUNK)), _full((CHUNK, LANES))),
        scratch_shapes=[pltpu.VMEM((CHUNK, GROUP_W), F32), pltpu.VMEM((D_MODEL, D_MODEL), F32)],
        compiler_params=pltpu.CompilerParams(dimension_semantics=("arbitrary",), vmem_limit_bytes=VMEM_LIMIT_WIDE),
    )(dxo, y, pr, ca, z2, oe, modv, g_post, w_out, vecs, wss, wsts, bsm)


def _conv_bwd(pc, g_a, g_b, conv_a, conv_b):
    t = pc.shape[0]
    n_tiles = t // TM
    pc_prev, pc_next = _halo_specs(W_C, t, TB)
    g_prev, g_next = _halo_specs(GROUP_W, t, TB)

    def body(pc_ref, pp_ref, pn_ref, ga_ref, gap_ref, gan_ref, gb_ref, gbp_ref, gbn_ref, cva_ref, cvb_ref,
             dpc_ref, dca_ref, dcb_ref, uext, zext, gaext, gbext):
        i = pl.program_id(0)

        @pl.when(i == 0)
        def _():
            dca_ref[...] = jnp.zeros_like(dca_ref)
            dcb_ref[...] = jnp.zeros_like(dcb_ref)

        def halo(jj, tile_ref, prev_ref, next_ref):
            before = prev_ref[...] if jj == 0 else tile_ref[pl.ds(jj * TM - HALO, HALO), :]
            after = next_ref[...] if jj == SUB - 1 else tile_ref[pl.ds((jj + 1) * TM, HALO), :]
            return before, after

        for jj in range(SUB):
            rows = pl.ds(jj * TM, TM)
            pv, nv = _halo_valid(i * SUB + jj, n_tiles)
            pc_ = pc_ref[rows, :]
            u, z = _conv_inputs(pc_)
            pc_before, pc_after = halo(jj, pc_ref, pp_ref, pn_ref)
            up, zp = _conv_inputs(pc_before)
            un, zn_ = _conv_inputs(pc_after)
            ue, ze, gae, gbe = uext.at[jj], zext.at[jj], gaext.at[jj], gbext.at[jj]
            _fill_ext(ue, up * pv, u, un * nv)
            _fill_ext(ze, zp * pv, z, zn_ * nv)
            ga = ga_ref[rows, :]
            gb = gb_ref[rows, :]
            ga_before, ga_after = halo(jj, ga_ref, gap_ref, gan_ref)
            gb_before, gb_after = halo(jj, gb_ref, gbp_ref, gbn_ref)
            _fill_ext(gae, ga_before * pv, ga, ga_after * nv)
            _fill_ext(gbe, gb_before * pv, gb, gb_after * nv)

            du = cva_ref[0:1, :] * gae[pl.ds(HALO + 1, TM), :]
            dca_ref[0:1, :] += _colsum(ga * ue[pl.ds(HALO - 1, TM), :])
            for kk in range(1, SHORT_CONV_K):
                du = du + cva_ref[kk:kk + 1, :] * gae[pl.ds(HALO + 1 - kk, TM), :]
                dca_ref[kk:kk + 1, :] += _colsum(ga * ue[pl.ds(HALO - 1 + kk, TM), :])
            dz = _taps31(gbe, cvb_ref, True)
            for r0 in range(0, TM, CONV_ROWS):
                gb_rows = gb_ref[pl.ds(jj * TM + r0, CONV_ROWS), :]
                for b in range(8):
                    zb = ze[pl.ds(r0 + b, CONV_ROWS + 24), :]
                    for a in range(4):
                        kk = 8 * a + b - 1
                        if 0 <= kk < CONFORMER_K:
                            dcb_ref[kk:kk + 1, :] += _colsum(gb_rows * zb[8 * a:8 * a + CONV_ROWS])

            a_c, a_h = pc_[:, 0:GROUP_W], pc_[:, GROUP_W:2 * GROUP_W]
            glu_a, glu_g = pc_[:, 2 * GROUP_W:3 * GROUP_W], pc_[:, 3 * GROUP_W:4 * GROUP_W]
            sg = _sigmoid(glu_g)
            dpc_ref[rows, :] = jnp.concatenate([du * a_h, du * a_c, dz * sg, dz * glu_a * sg * (1.0 - sg)], axis=1).astype(BF16)

    ext = pltpu.VMEM((SUB, TM + 2 * HALO, GROUP_W), F32)
    return _pc(
        body, name="conv_bwd", grid=(t // TB,),
        out_shape=(jax.ShapeDtypeStruct((t, W_C), BF16), jax.ShapeDtypeStruct((8, GROUP_W), F32), jax.ShapeDtypeStruct((32, GROUP_W), F32)),
        in_specs=[_rows(W_C, TB), pc_prev, pc_next, _rows(GROUP_W, TB), g_prev, g_next, _rows(GROUP_W, TB), g_prev, g_next,
                  _const((8, GROUP_W)), _const((32, GROUP_W))],
        out_specs=(_rows(W_C, TB), _full((8, GROUP_W)), _full((32, GROUP_W))),
        scratch_shapes=[ext, ext, ext, ext],
        compiler_params=_cparams(),
    )(pc, pc, pc, g_a, g_a, g_a, g_b, g_b, g_b, conv_a, conv_b)


def _attention_bwd(q, k, v, doe, slabs, conv):
    t = q.shape[1]
    tk = _kv_chunk(t)
    n_chunks = (t - CTX_LEN) // tk
    n_tiles = t // TM
    n_sl = len(slabs)
    rpt = TM // n_chunks
    pc, g_a, g_b, conv_a, conv_b = conv

    def body(q_ref, do_ref, k_ref, v_ref, pc_ref, pp_ref, pn_ref, ga_ref, gap_ref, gan_ref, gb_ref, gbp_ref, gbn_ref,
             cva_ref, cvb_ref, *rest):
        i = pl.program_id(0)
        dq_ref, dk_hbm, dv_hbm, dpc_ref, dca_ref, dcb_ref = rest[n_sl:n_sl + 6]
        dk_acc, dv_acc, uext, zext, gaext, gbext, dz_s = rest[2 * n_sl + 6:2 * n_sl + 13]
        pairs = tuple(zip(rest[:n_sl], rest[n_sl + 6:2 * n_sl + 6]))
        sems = rest[2 * n_sl + 13:]

        @pl.when(i == 0)
        def _():
            dk_acc[...] = jnp.zeros_like(dk_acc)
            dv_acc[...] = jnp.zeros_like(dv_acc)
            dca_ref[...] = jnp.zeros_like(dca_ref)
            dcb_ref[...] = jnp.zeros_like(dcb_ref)
            _scatter_phase(0, pairs, *sems)

        pv, nv = _halo_valid(i, n_tiles)
        pc_ = pc_ref[...]
        u, z = _conv_inputs(pc_)
        up, zp = _conv_inputs(pp_ref[...])
        un, zn_ = _conv_inputs(pn_ref[...])
        _fill_ext(uext, up * pv, u, un * nv)
        _fill_ext(zext, zp * pv, z, zn_ * nv)
        _fill_ext(gaext, gap_ref[...] * pv, ga_ref[...], gan_ref[...] * nv)
        _fill_ext(gbext, gbp_ref[...] * pv, gb_ref[...], gbn_ref[...] * nv)

        def conv_rows(r0):
            out = None
            for b in range(8):
                part = None
                for a in range(4):
                    o = 8 * a + b
                    if 1 <= o <= CONFORMER_K:
                        term = cvb_ref[CONFORMER_K - o:CONFORMER_K - o + 1, :] * gbext[pl.ds(r0 + 8 * a, rpt + 8), :]
                        part = term if part is None else part + term
                part = part[b:b + rpt]
                out = part if out is None else out + part
            dz_s[pl.ds(r0, rpt), :] = out
            gb_rows = gb_ref[pl.ds(r0, rpt), :]
            window = zext[pl.ds(r0, rpt + 2 * HALO), :]
            for b in range(8):
                zb = window[b:b + rpt + 24]
                for a in range(4):
                    kk = 8 * a + b - 1
                    if 0 <= kk < CONFORMER_K:
                        dcb_ref[kk:kk + 1, :] += _colsum(gb_rows * zb[8 * a:8 * a + rpt])

        lane = _lane(GQA * TM)
        lo = lane < HEAD_DIM
        qs, dos, deltas, lses = [], [], [], []
        for g in range(N_KV_HEADS):
            qs.append(jnp.concatenate([q_ref[GQA * g + hh] for hh in range(GQA)], axis=0))
            dog = jnp.concatenate([do_ref[GQA * g + hh] for hh in range(GQA)], axis=0)
            deltas.append(_rowsum(jnp.where(lane == HEAD_DIM, dog, 0.0)))
            lses.append(_rowsum(jnp.where(lane == HEAD_DIM + 1, dog, 0.0)))
            dos.append(jnp.where(lo, dog, 0.0).astype(BF16))

        def step(st, size, dqs):
            out = []
            for g in range(N_KV_HEADS):
                kc = k_ref[g, pl.ds(st, size), :]
                vc = v_ref[g, pl.ds(st, size), :]
                p = jnp.exp(_dot_nt(qs[g], kc) - lses[g])
                ds_ = (p * (_dot_nt(dos[g], vc) - deltas[g])).astype(BF16)
                dk_acc[g, pl.ds(st, size), :] += _dot_tn(ds_, qs[g])
                dv_acc[g, pl.ds(st, size), :] += _dot_tn(p.astype(BF16), dos[g])
                out.append(dqs[g] + _dot(ds_, kc))
            return tuple(out)

        zero = tuple(jnp.zeros((GQA * TM, LANES), F32) for _ in range(N_KV_HEADS))

        def finish(dqs):
            for g in range(N_KV_HEADS):
                for hh in range(GQA):
                    dq_ref[GQA * g + hh] = dqs[g][hh * TM:(hh + 1) * TM]

        @pl.when(i < N_CTX_TILES)
        def _():
            finish(step(0, CTX_LEN, zero))
            for r in range(n_chunks):
                conv_rows(r * rpt)

        @pl.when(i >= N_CTX_TILES)
        def _():
            def trip(j, acc):
                conv_rows(pl.multiple_of(j * rpt, rpt))
                return step(pl.multiple_of(CTX_LEN + j * tk, 256), tk, acc)

            finish(lax.fori_loop(0, n_chunks, trip, step(0, CTX_LEN, zero)))

        ga = ga_ref[...]
        du = cva_ref[0:1, :] * gaext[pl.ds(HALO + 1, TM), :]
        dca_ref[0:1, :] += _colsum(ga * uext[pl.ds(HALO - 1, TM), :])
        for kk in range(1, SHORT_CONV_K):
            du = du + cva_ref[kk:kk + 1, :] * gaext[pl.ds(HALO + 1 - kk, TM), :]
            dca_ref[kk:kk + 1, :] += _colsum(ga * uext[pl.ds(HALO - 1 + kk, TM), :])
        dz = dz_s[...]
        a_c, a_h = pc_[:, 0:GROUP_W], pc_[:, GROUP_W:2 * GROUP_W]
        glu_a, glu_g = pc_[:, 2 * GROUP_W:3 * GROUP_W], pc_[:, 3 * GROUP_W:4 * GROUP_W]
        sg = _sigmoid(glu_g)
        dpc_ref[...] = jnp.concatenate([du * a_h, du * a_c, dz * sg, dz * glu_a * sg * (1.0 - sg)], axis=1).astype(BF16)

        @pl.when(i == n_tiles - 1)
        def _():
            pltpu.sync_copy(dk_acc, dk_hbm)
            pltpu.sync_copy(dv_acc, dv_hbm)
            _scatter_phase(1, pairs, *sems)

    kv_shape = jax.ShapeDtypeStruct((N_KV_HEADS, t, LANES), F32)
    hbm = pl.BlockSpec(memory_space=pl.ANY)
    pc_prev, pc_next = _halo_specs(W_C, t)
    g_prev, g_next = _halo_specs(GROUP_W, t)
    ext = pltpu.VMEM((TM + 2 * HALO, GROUP_W), F32)
    outs = _pc(
        body, name="attention_bwd_scatter", grid=(n_tiles,),
        out_shape=(jax.ShapeDtypeStruct((N_Q_HEADS, t, LANES), F32), kv_shape, kv_shape,
                   jax.ShapeDtypeStruct((t, W_C), BF16), jax.ShapeDtypeStruct((8, GROUP_W), F32),
                   jax.ShapeDtypeStruct((32, GROUP_W), F32)) + tuple(jax.ShapeDtypeStruct(a.shape, a.dtype) for a in slabs),
        in_specs=[_heads(N_Q_HEADS, LANES), _heads(N_Q_HEADS, LANES),
                  _const((N_KV_HEADS, t, LANES)), _const((N_KV_HEADS, t, LANES)),
                  _rows(W_C), pc_prev, pc_next, _rows(GROUP_W), g_prev, g_next, _rows(GROUP_W), g_prev, g_next,
                  _const((8, GROUP_W)), _const((32, GROUP_W))] + [hbm] * n_sl,
        out_specs=(_heads(N_Q_HEADS, LANES), hbm, hbm, _rows(W_C), _full((8, GROUP_W)), _full((32, GROUP_W))) + (hbm,) * n_sl,
        scratch_shapes=[pltpu.VMEM((N_KV_HEADS, t, LANES), F32), pltpu.VMEM((N_KV_HEADS, t, LANES), F32),
                        ext, ext, ext, ext, pltpu.VMEM((TM, GROUP_W), F32)] + _comm_scratch(SCATTER_SEMS, n_sl),
        compiler_params=_cparams(),
    )(q, doe, k, v, pc, pc, pc, g_a, g_a, g_a, g_b, g_b, g_b, conv_a, conv_b, *slabs)
    return outs[0], outs[1], outs[2], outs[3:6], tuple(outs[6:])


def _in_proj_bwd(dpc, dpr, dq, dk, dv, pq, qk_gain, cos_t, sin_t, w_c, w_r, w_q, xt, dxo, modv, g_pre):
    t = xt.shape[0]

    def body(dpc_ref, dpr_ref, dq_ref, dk_ref, dv_ref, pq_ref, gain_ref, cos_ref, sin_ref, wc_ref, wr_ref, wq_ref,
             x_ref, dxo_ref, mod_ref, g_ref, dx_ref, dpq_ref, acc_ref, dgain_ref):
        i = pl.program_id(0)

        @pl.when(i == 0)
        def _():
            acc_ref[...] = jnp.zeros_like(acc_ref)
            dgain_ref[...] = jnp.zeros_like(dgain_ref)

        lane = _lane(TM)
        lo = lane < HEAD_DIM
        lo16 = (lane & 31) < 16
        g = g_ref[...]
        for jj in range(SUB):
            rows = pl.ds(jj * TM, TM)
            is_ctx = i * SUB + jj < N_CTX_TILES
            cos = cos_ref[rows, :]
            sin = sin_ref[rows, :]
            outs = []
            for b in range(3):
                src = dq_ref if b < 2 else dk_ref
                base = 2 * b if b < 2 else 0
                drot = src[base, rows, :] + pltpu.roll(src[base + 1, rows, :], HEAD_DIM, 1)
                if b < 2:
                    drot = drot * ATTN_SCALE
                dxg = drot * cos + _swap16(drot * sin, lo16)
                xh, r = _head_norm(pq_ref[rows, b * LANES:(b + 1) * LANES], lo)
                row = 0 if b < 2 else 1
                dgain_ref[row:row + 1, :] += _colsum(dxg * xh)
                dxh = dxg * gain_ref[row:row + 1, :]
                outs.append(r * (dxh - xh * (_pair_sums(dxh * xh, lo) * (1.0 / HEAD_DIM))))
            outs.append(dv_ref[0, rows, :] + pltpu.roll(dv_ref[1, rows, :], HEAD_DIM, 1))
            dpq = jnp.concatenate(outs, axis=1).astype(BF16)
            dpq_ref[rows, :] = dpq

            dh = _dot_nt(dpc_ref[rows, :], wc_ref[...]) + _dot_nt(dpr_ref[rows, :], wr_ref[...]) + _dot_nt(dpq, wq_ref[...])
            x = x_ref[rows, :]
            r = lax.rsqrt(jnp.mean(x * x, axis=1, keepdims=True) + RMS_EPS)
            xn = x * r
            sc = jnp.where(is_ctx, mod_ref[1:2, :], mod_ref[4:5, :])
            dsh = _colsum(dh)
            dsc = _colsum(dh * (xn * g))
            acc_ref[0:1, :] += jnp.where(is_ctx, dsh, 0.0)
            acc_ref[1:2, :] += jnp.where(is_ctx, dsc, 0.0)
            acc_ref[2:3, :] += jnp.where(is_ctx, 0.0, dsh)
            acc_ref[3:4, :] += jnp.where(is_ctx, 0.0, dsc)
            dxg = dh * (1.0 + sc)
            acc_ref[4:5, :] += _colsum(dxg * xn)
            dxn = dxg * g
            dx_ref[rows, :] = r * (dxn - xn * jnp.mean(dxn * xn, axis=1, keepdims=True)) + dxo_ref[rows, :]

    return _pc(
        body, name="in_proj_bwd", grid=(t // TB,),
        out_shape=(jax.ShapeDtypeStruct((t, D_MODEL), F32), jax.ShapeDtypeStruct((t, W_Q), BF16),
                   jax.ShapeDtypeStruct((8, D_MODEL), F32), jax.ShapeDtypeStruct((8, LANES), F32)),
        in_specs=[_rows(W_C, TB), _rows(W_R, TB),
                  _heads(N_Q_HEADS, LANES, TB), _heads(N_KV_HEADS, LANES, TB), _heads(N_KV_HEADS, LANES, TB), _rows(W_Q, TB),
                  _const((8, LANES)), _rows(LANES, TB), _rows(LANES, TB),
                  _const((D_MODEL, W_C)), _const((D_MODEL, W_R)), _const((D_MODEL, W_Q)),
                  _rows(D_MODEL, TB), _rows(D_MODEL, TB), _const((8, D_MODEL)), _const((1, D_MODEL))],
        out_specs=(_rows(D_MODEL, TB), _rows(W_Q, TB), _full((8, D_MODEL)), _full((8, LANES))),
        compiler_params=_cparams(),
    )(dpc, dpr, dq, dk, dv, pq, qk_gain, cos_t, sin_t, w_c, w_r, w_q, xt, dxo, modv, g_pre)


def _in_proj_wgrad(h, dpc, dpr, dpq):
    t = h.shape[0]

    def body(h_ref, dpc_ref, dpr_ref, dpq_ref, gc_ref, gr_ref, gq_ref):
        @pl.when(pl.program_id(0) == 0)
        def _():
            gc_ref[...] = jnp.zeros_like(gc_ref)
            gr_ref[...] = jnp.zeros_like(gr_ref)
            gq_ref[...] = jnp.zeros_like(gq_ref)

        hb = h_ref[...]
        gc_ref[...] += _dot_tn(hb, dpc_ref[...])
        gr_ref[...] += _dot_tn(hb, dpr_ref[...])
        gq_ref[...] += _dot_tn(hb, dpq_ref[...])

    return _pc(
        body, name="in_proj_wgrad", grid=(t // TB,),
        out_shape=(jax.ShapeDtypeStruct((D_MODEL, W_C), F32), jax.ShapeDtypeStruct((D_MODEL, W_R), F32),
                   jax.ShapeDtypeStruct((D_MODEL, W_Q), F32)),
        in_specs=[_rows(D_MODEL, TB), _rows(W_C, TB), _rows(W_R, TB), _rows(W_Q, TB)],
        out_specs=(_full((D_MODEL, W_C)), _full((D_MODEL, W_R)), _full((D_MODEL, W_Q))),
        compiler_params=_cparams(),
    )(h, dpc, dpr, dpq)


def _sum_slabs(slabs, tile_rows):
    n, r, c = slabs.shape

    def body(s_ref, o_ref):
        acc = s_ref[0].astype(F32)
        for k in range(1, n):
            acc = acc + s_ref[k].astype(F32)
        o_ref[...] = acc

    return _pc(
        body, name="sum_slabs", grid=(r // tile_rows,),
        out_shape=jax.ShapeDtypeStruct((r, c), F32),
        in_specs=[pl.BlockSpec((n, tile_rows, c), lambda i: (0, i, 0))],
        out_specs=pl.BlockSpec((tile_rows, c), lambda i: (i, 0)),
        compiler_params=_cparams(),
    )(slabs)


def _sum_layer_slabs(layers, tile_rows):
    nl = len(layers)
    n, r, c = layers[0].shape
    per = r // tile_rows

    def body(*refs):
        o_ref = refs[nl]
        for l in range(nl):
            @pl.when(pl.program_id(0) // per == l)
            def _(l=l):
                acc = refs[l][0].astype(F32)
                for k in range(1, n):
                    acc = acc + refs[l][k].astype(F32)
                o_ref[...] = acc

    def spec(l):
        return pl.BlockSpec((n, tile_rows, c), lambda i: (0, jnp.clip(i - l * per, 0, per - 1), 0))

    return _pc(
        body, name="sum_layer_slabs", grid=(nl * per,),
        out_shape=jax.ShapeDtypeStruct((nl * r, c), F32),
        in_specs=[spec(l) for l in range(nl)],
        out_specs=pl.BlockSpec((tile_rows, c), lambda i: (i, 0)),
        compiler_params=_cparams(),
    )(*layers)


def _adamw(grads, w, m, v, tile_rows):
    r, c = w.shape
    n_g = len(grads)

    def body(*refs):
        g = refs[0][...]
        for k in range(1, n_g):
            g = g + refs[k][...]
        w_ref, m_ref, v_ref, g_out, d_out, m_out, v_out = refs[n_g:]
        m_new = ADAM_B1 * m_ref[...] + (1.0 - ADAM_B1) * g
        v_new = ADAM_B2 * v_ref[...] + (1.0 - ADAM_B2) * (g * g)
        m_hat = m_new / (1.0 - ADAM_B1 ** ADAM_STEP)
        v_hat = v_new / (1.0 - ADAM_B2 ** ADAM_STEP)
        g_out[...] = g
        d_out[...] = -ADAM_LR * (m_hat / (jnp.sqrt(v_hat) + ADAM_EPS) + ADAM_WD * w_ref[...])
        m_out[...] = m_new
        v_out[...] = v_new

    spec = pl.BlockSpec((tile_rows, c), lambda i: (i, 0))
    shape = jax.ShapeDtypeStruct((r, c), F32)
    return _pc(
        body, name="adamw", grid=(r // tile_rows,),
        out_shape=(shape,) * 4, in_specs=[spec] * (n_g + 3), out_specs=(spec,) * 4,
        compiler_params=_cparams(),
    )(*grads, w, m, v)


def _rope_tables(s_lat):
    n_rows = s_lat // GRID_W
    axis_dim = HEAD_DIM // 2
    inv_freq = 1.0 / (ROPE_THETA ** (jnp.arange(0, axis_dim, 2, dtype=F32) / axis_dim))
    d = np.arange(LANES) % HEAD_DIM
    on_rows = (d // axis_dim) == 0
    freq = d % (axis_dim // 2)
    sign = np.where((d % axis_dim) < axis_dim // 2, -1.0, 1.0).astype(np.float32)
    ang_r = jnp.arange(n_rows, dtype=F32)[:, None] * inv_freq[freq][None, :]
    ang_c = jnp.arange(GRID_W, dtype=F32)[:, None] * inv_freq[freq][None, :]

    def spread(fn):
        full = jnp.where(on_rows[None, None, :], fn(ang_r)[:, None, :], fn(ang_c)[None, :, :])
        return full.reshape(s_lat, LANES)

    cos = jnp.concatenate([jnp.ones((CTX_LEN, LANES), F32), spread(jnp.cos)], axis=0)
    sin = jnp.concatenate([jnp.zeros((CTX_LEN, LANES), F32), spread(jnp.sin) * sign[None, :]], axis=0)
    return cos, sin


def _pad_rows(a, rows):
    return jnp.concatenate([a, jnp.zeros((rows - a.shape[0],) + a.shape[1:], a.dtype)], axis=0)


_SMALL = ("c_ctx", "b_mod", "g_pre", "g_post", "conv_a", "conv_b", "conv_b_bias", "conf_ln_g", "conf_ln_b",
          "sgu_ln_g", "sgu_ln_b", "w_s", "b_s", "q_gain", "k_gain")


def _pack(arrays):
    flat = jnp.concatenate([a.reshape(-1) for a in arrays])
    rows = -(-flat.shape[0] // (16 * LANES)) * 16
    return _pad_rows(flat.reshape(-1, 1), rows * LANES).reshape(rows, LANES)


def _unpack(packed, shapes):
    flat = packed.reshape(-1)
    out, off = [], 0
    for s in shapes:
        n = int(np.prod(s))
        out.append(flat[off:off + n].reshape(s))
        off += n
    return out


def kernel(x, c, ctx, c_ctx, w_mod, b_mod, g_pre, g_post, w_in, w_out, conv_a, conv_b, conv_b_bias, conf_ln_g, conf_ln_b, sgu_ln_g, sgu_ln_b, w_s, b_s, q_gain, k_gain, loss_target, m_c_ctx, m_w_mod, m_b_mod, m_g_pre, m_g_post, m_w_in, m_w_out, m_conv_a, m_conv_b, m_conv_b_bias, m_conf_ln_g, m_conf_ln_b, m_sgu_ln_g, m_sgu_ln_b, m_w_s, m_b_s, m_q_gain, m_k_gain, v_c_ctx, v_w_mod, v_b_mod, v_g_pre, v_g_post, v_w_in, v_w_out, v_conv_a, v_conv_b, v_conv_b_bias, v_conf_ln_g, v_conf_ln_b, v_sgu_ln_g, v_sgu_ln_b, v_w_s, v_b_s, v_q_gain, v_k_gain):
    weights = dict(c_ctx=c_ctx, w_mod=w_mod, b_mod=b_mod, g_pre=g_pre, g_post=g_post, w_in=w_in, w_out=w_out, conv_a=conv_a,
                   conv_b=conv_b, conv_b_bias=conv_b_bias, conf_ln_g=conf_ln_g, conf_ln_b=conf_ln_b, sgu_ln_g=sgu_ln_g,
                   sgu_ln_b=sgu_ln_b, w_s=w_s, b_s=b_s, q_gain=q_gain, k_gain=k_gain)
    m_in = dict(c_ctx=m_c_ctx, w_mod=m_w_mod, b_mod=m_b_mod, g_pre=m_g_pre, g_post=m_g_post, w_in=m_w_in, w_out=m_w_out,
                conv_a=m_conv_a, conv_b=m_conv_b, conv_b_bias=m_conv_b_bias, conf_ln_g=m_conf_ln_g, conf_ln_b=m_conf_ln_b,
                sgu_ln_g=m_sgu_ln_g, sgu_ln_b=m_sgu_ln_b, w_s=m_w_s, b_s=m_b_s, q_gain=m_q_gain, k_gain=m_k_gain)
    v_in = dict(c_ctx=v_c_ctx, w_mod=v_w_mod, b_mod=v_b_mod, g_pre=v_g_pre, g_post=v_g_post, w_in=v_w_in, w_out=v_w_out,
                conv_a=v_conv_a, conv_b=v_conv_b, conv_b_bias=v_conv_b_bias, conf_ln_g=v_conf_ln_g, conf_ln_b=v_conf_ln_b,
                sgu_ln_g=v_sgu_ln_g, sgu_ln_b=v_sgu_ln_b, w_s=v_w_s, b_s=v_b_s, q_gain=v_q_gain, k_gain=v_k_gain)
    order = ("c_ctx", "w_mod", "b_mod", "g_pre", "g_post", "w_in", "w_out", "conv_a", "conv_b", "conv_b_bias", "conf_ln_g",
             "conf_ln_b", "sgu_ln_g", "sgu_ln_b", "w_s", "b_s", "q_gain", "k_gain")

    s_lat = x.shape[1]
    ax, ay, ac = lax.axis_index("x"), lax.axis_index("y"), lax.axis_index("c")
    chip = 2 * ax + ay
    example = 4 * ax + 2 * ay + ac

    c_rows = _all_gather_rows(_pad_rows(c, 8))[::8]
    c16 = _pad_rows(jnp.concatenate([c_rows, c_ctx[None, :]], axis=0), 16)
    b_mod_shard = lax.dynamic_slice_in_dim(b_mod, chip * SHARD_MOD, SHARD_MOD, axis=1)[:, None, :]
    silu_c, mod_shard = _mod_forward(c16, w_mod, b_mod_shard)
    mod_all = _all_gather_rows(mod_shard.reshape(DEPTH * 16, SHARD_MOD)).reshape(8, DEPTH, 16, SHARD_MOD)
    mod_full = jnp.transpose(mod_all[::2], (1, 2, 0, 3)).reshape(DEPTH, 16, 3 * D_MODEL)
    mod_lat = lax.dynamic_index_in_dim(mod_full, example, axis=1, keepdims=False).reshape(DEPTH, 3, D_MODEL)
    mod_ctx = mod_full[:, 8].reshape(DEPTH, 3, D_MODEL)
    modv = jnp.concatenate([mod_ctx, mod_lat, jnp.zeros((DEPTH, 2, D_MODEL), F32)], axis=1)

    wi_b, wo_b = w_in.astype(BF16), w_out.astype(BF16)

    def regroup(wi_all):
        wi_full = jnp.concatenate([wi_all[k] for k in range(N_CHIPS)], axis=-1)
        wc_l = jnp.concatenate([wi_full[:, 256:768], wi_full[:, 1024:1536]], axis=-1)
        wr_l = jnp.concatenate([wi_full[:, 0:256], wi_full[:, 768:1024], wi_full[:, 1536:2560], wi_full[:, 3072:3328]], axis=-1)
        return wc_l, wr_l, wi_full[:, 2560:3072]

    w_c, w_r, w_q, wo_full = [None] * DEPTH, [None] * DEPTH, [None] * DEPTH, [None] * DEPTH
    w_c[0], w_r[0], w_q[0] = regroup(_gather_weights((wi_b[0],))[0])

    cos_t, sin_t = _rope_tables(s_lat)
    conv_a_full = jnp.zeros((DEPTH, 8, GROUP_W), F32)
    conv_b_full = jnp.zeros((DEPTH, 32, GROUP_W), F32)
    conv_small = jnp.concatenate([conv_a.reshape(DEPTH * SHORT_CONV_K, -1), conv_b.reshape(DEPTH * CONFORMER_K, -1)], axis=0)
    n_cs = conv_small.shape[0]
    conv_rows = -(-n_cs // 8) * 8
    conv_all = _all_gather_rows(_pad_rows(conv_small, conv_rows)).reshape(8, conv_rows, -1)[::2]
    conv_all = jnp.transpose(conv_all, (1, 0, 2)).reshape(conv_rows, GROUP_W)
    conv_a_full = conv_a_full.at[:, :SHORT_CONV_K].set(conv_all[:DEPTH * SHORT_CONV_K].reshape(DEPTH, SHORT_CONV_K, GROUP_W))
    conv_b_full = conv_b_full.at[:, :CONFORMER_K].set(
        conv_all[DEPTH * SHORT_CONV_K:n_cs].reshape(DEPTH, CONFORMER_K, GROUP_W))

    vecs = jnp.stack([conv_b_bias, conf_ln_g, conf_ln_b, sgu_ln_g, sgu_ln_b] + [jnp.zeros_like(conv_b_bias)] * 3, axis=1)
    wss = w_s.reshape(DEPTH, N_SPATIAL_GROUPS * CHUNK, CHUNK).astype(BF16)
    wsts = jnp.swapaxes(w_s, 2, 3).reshape(DEPTH, N_SPATIAL_GROUPS * CHUNK, CHUNK).astype(BF16)
    bsm = jnp.repeat(jnp.swapaxes(b_s, 1, 2), HEAD_DIM, axis=2)
    qk_gain = jnp.concatenate([jnp.tile(q_gain, (1, 2))[:, None, :], jnp.tile(k_gain, (1, 2))[:, None, :],
                               jnp.zeros((DEPTH, 6, LANES), F32)], axis=1)

    xt = jnp.concatenate([ctx[0], x[0]], axis=0)
    saved = []
    for l in range(DEPTH):
        h, pc, pr, pq, q, k, v = _in_proj(xt, modv[l], g_pre[l][None, :], w_c[l], w_r[l], w_q[l], qk_gain[l], cos_t, sin_t)
        oe, gathered = _attention_fwd(q, k, v, (wo_b[l],) + ((wi_b[l + 1],) if l + 1 < DEPTH else ()))
        wo_full[l] = jnp.concatenate([gathered[0][k] for k in range(N_CHIPS)], axis=0)
        if l + 1 < DEPTH:
            w_c[l + 1], w_r[l + 1], w_q[l + 1] = regroup(gathered[1])
        mixed = _mix_out(pc, pr, oe, xt, modv[l], g_post[l][None, :], wo_full[l], conv_a_full[l], conv_b_full[l],
                         vecs[l], wss[l], bsm[l], loss_target[0] if l + 1 == DEPTH else None)
        x_new, y, ca, z2 = mixed[:4]
        saved.append(dict(x=xt, h=h, pc=pc, pr=pr, pq=pq, q=q, k=k, v=v, oe=oe, y=y, ca=ca, z2=z2))
        xt = x_new
    dxo = xt
    loss = lax.psum(mixed[4][0, 0], ("x", "y", "c"))

    g_small = {n: [None] * DEPTH for n in _SMALL}
    d_mod, landed_in, landed_out = [None] * DEPTH, [None] * DEPTH, [None] * DEPTH
    slab_in = None
    for l in reversed(range(DEPTH)):
        s = saved[l]
        dpr, g_a, g_b, doe, gw_o, pvec, s256, dws, dbs = _mix_out_bwd(
            dxo, s["y"], s["pr"], s["ca"], s["z2"], s["oe"], modv[l], g_post[l][None, :], wo_full[l], vecs[l], wss[l], wsts[l], bsm[l])
        slab_out = gw_o.reshape(N_CHIPS, SHARD_OUT, D_MODEL).astype(BF16)
        dq, dk, dv, (dpc, dca, dcb), got = _attention_bwd(
            s["q"], s["k"], s["v"], doe, (slab_out,) + (() if slab_in is None else (slab_in,)),
            (s["pc"], g_a, g_b, conv_a_full[l], conv_b_full[l]))
        landed_out[l] = got[0]
        if slab_in is not None:
            landed_in[l + 1] = got[1]
        dxo, dpq, acc, dgain = _in_proj_bwd(dpc, dpr, dq, dk, dv, s["pq"], qk_gain[l], cos_t, sin_t, w_c[l], w_r[l], w_q[l],
                                            s["x"], dxo, modv[l], g_pre[l][None, :])
        gw_c, gw_r, gw_q = _in_proj_wgrad(s["h"], dpc, dpr, dpq)
        gw_in = jnp.concatenate([gw_r[:, 0:256], gw_c[:, 0:512], gw_r[:, 256:512], gw_c[:, 512:1024],
                                 gw_r[:, 512:1536], gw_q, gw_r[:, 1536:1792]], axis=-1)
        slab_in = jnp.transpose(gw_in.reshape(D_MODEL, N_CHIPS, SHARD_IN), (1, 0, 2)).astype(BF16)
        d_mod[l] = jnp.stack([jnp.concatenate([acc[2], acc[3], pvec[1]]), jnp.concatenate([acc[0], acc[1], pvec[0]])])
        g_small["g_pre"][l] = acc[4]
        g_small["g_post"][l] = pvec[2]
        g_small["conv_a"][l] = dca[:SHORT_CONV_K]
        g_small["conv_b"][l] = dcb[:CONFORMER_K]
        g_small["conv_b_bias"][l] = s256[0]
        g_small["conf_ln_g"][l] = s256[1]
        g_small["conf_ln_b"][l] = s256[2]
        g_small["sgu_ln_g"][l] = s256[3]
        g_small["sgu_ln_b"][l] = s256[4]
        g_small["w_s"][l] = dws.reshape(N_SPATIAL_GROUPS, CHUNK, CHUNK)
        g_small["b_s"][l] = jnp.transpose(dbs[:, :N_SPATIAL_GROUPS])
        g_small["q_gain"][l] = dgain[0, :HEAD_DIM] + dgain[0, HEAD_DIM:]
        g_small["k_gain"][l] = dgain[1, :HEAD_DIM] + dgain[1, HEAD_DIM:]
    grad_x = dxo[CTX_LEN:][None]

    d_mod_all = _all_gather_rows(jnp.stack(d_mod).reshape(DEPTH * 2, 3 * D_MODEL)).reshape(8, DEPTH, 2, 3 * D_MODEL)
    d_lat = jnp.transpose(d_mod_all[:, :, 0], (1, 0, 2))
    d_ctx = jnp.transpose(d_mod_all[:, :, 1], (1, 0, 2))
    cols = lambda a: lax.dynamic_slice_in_dim(a.reshape(DEPTH, 8, N_CHIPS, SHARD_MOD), chip, 1, axis=2)[:, :, 0]
    silu_t = jnp.transpose(silu_c)
    s_t = jnp.concatenate([silu_t[:, 0:8], jnp.tile(silu_t[:, 8:9], (1, 8)), jnp.zeros((D_MODEL, LANES - 16), F32)], axis=1)
    g_rows = jnp.concatenate([cols(d_lat), cols(d_ctx), jnp.zeros((DEPTH, LANES - 16, SHARD_MOD), F32)], axis=1)
    g_w_mod, g_b_mod, c_ctx_part = _mod_backward(s_t, g_rows, cols(d_ctx), jnp.concatenate([d_lat, d_ctx], axis=1),
                                                 w_mod, c_ctx[:, None])

    for n in _SMALL:
        if n not in ("c_ctx", "b_mod"):
            g_small[n] = jnp.stack(g_small[n])
    small_parts = [0.5 * c_ctx_part[:, 0]] + [g_small[n] for n in _SMALL[2:]]
    packed = _pack(small_parts)
    gathered = _all_gather_rows(packed.astype(BF16)).reshape(8, packed.shape[0], LANES)
    small_sum = _sum_slabs(gathered, packed.shape[0])
    small_g = dict(zip(("c_ctx",) + _SMALL[2:], _unpack(small_sum, [p.shape for p in small_parts])))
    small_g["b_mod"] = g_b_mod[:, 0]
    ch64 = GROUP_W // N_CHIPS
    for n in ("conv_a", "conv_b"):
        small_g[n] = lax.dynamic_slice_in_dim(small_g[n], chip * ch64, ch64, axis=2)
    sw = _pack([weights[n] for n in _SMALL])
    sm = _pack([m_in[n] for n in _SMALL])
    sv = _pack([v_in[n] for n in _SMALL])
    sg = _pack([small_g[n] for n in _SMALL])
    shapes = [weights[n].shape for n in _SMALL]
    small_out = [dict(zip(_SMALL, _unpack(o, shapes))) for o in _adamw([sg], sw, sm, sv, sg.shape[0])]

    landed_in[0] = _scatter_slabs((slab_in,))[0]
    sum_in = _sum_layer_slabs(landed_in, 512)
    sum_out = _sum_layer_slabs(landed_out, 256)
    sib_in, sib_out = _swap_with_sibling(sum_in, sum_out)

    big = {}
    flat = lambda a: a.reshape(-1, a.shape[-1])
    for n, grads, rows in (("w_in", [sum_in, sib_in], 512), ("w_out", [sum_out, sib_out], 256), ("w_mod", [flat(g_w_mod)], 512)):
        outs = _adamw(grads, flat(weights[n]), flat(m_in[n]), flat(v_in[n]), rows)
        big[n] = [o.reshape(weights[n].shape) for o in outs]

    def leaf(n, j):
        return big[n][j] if n in big else small_out[j][n]

    return (loss, grad_x, *[leaf(n, 0) for n in order], *[leaf(n, 1) for n in order],
            *[leaf(n, 2) for n in order], *[leaf(n, 3) for n in order])
```

```python
import functools

import numpy as np
import jax
import jax.numpy as jnp
from jax import lax
from jax.experimental import pallas as pl
from jax.experimental.pallas import tpu as pltpu

F32 = jnp.float32
BF16 = jnp.bfloat16
MESH = pl.DeviceIdType.MESH

D_MODEL = 1024
DEPTH = 4
GRID_W = 64
CTX_LEN = 256
GROUP_W = 256
HEAD_DIM = 64
N_Q_HEADS = 4
N_KV_HEADS = 2
GQA = N_Q_HEADS // N_KV_HEADS
ROPE_THETA = 10000.0
ATTN_SCALE = HEAD_DIM ** -0.5
SHORT_CONV_K = 3
CONFORMER_K = 31
CHUNK = 128
N_SPATIAL_GROUPS = 4
RMS_EPS = 1e-6
LN_EPS = 1e-5
ADAM_LR = 0.001
ADAM_B1 = 0.9
ADAM_B2 = 0.999
ADAM_EPS = 1e-08
ADAM_WD = 0.01
ADAM_STEP = 10

LANES = 128
HALO = 16
CONV_ROWS = 64
TM = 256
N_CTX_TILES = CTX_LEN // TM
SUB = 3
TB = SUB * TM
W_C = 1024
W_R = 1792
W_Q = 512
PROJ_W = W_C + W_R + W_Q
N_CHIPS = 4
SHARD_IN = PROJ_W // N_CHIPS
SHARD_OUT = D_MODEL // N_CHIPS
SHARD_MOD = 3 * D_MODEL // N_CHIPS
VMEM_LIMIT = 56 * 1024 * 1024
VMEM_LIMIT_WIDE = 60 * 1024 * 1024


def _pc(body, **kw):
    return pl.pallas_call(body, **kw)


def _cparams(**kw):
    return pltpu.CompilerParams(dimension_semantics=("arbitrary",), vmem_limit_bytes=VMEM_LIMIT, **kw)


def _full(shape):
    n = len(shape)
    return pl.BlockSpec(shape, lambda i: (0,) * n)


def _const(shape):
    n = len(shape)
    return pl.BlockSpec(shape, lambda i: (0,) * n, pipeline_mode=pl.Buffered(1))


def _rows(width, tm=TM):
    return pl.BlockSpec((tm, width), lambda i: (i, 0))


def _heads(nh, width, tm=TM):
    return pl.BlockSpec((nh, tm, width), lambda i: (0, i, 0))


def _sigmoid(x):
    return jax.nn.sigmoid(x)


def _dot(a, b):
    return jnp.dot(a, b, preferred_element_type=F32)


def _dot_nt(a, b):
    return lax.dot_general(a, b, (((1,), (1,)), ((), ())), preferred_element_type=F32)


def _dot_tn(a, b):
    return lax.dot_general(a, b, (((0,), (0,)), ((), ())), preferred_element_type=F32)


def _lane(rows):
    return lax.broadcasted_iota(jnp.int32, (rows, LANES), 1)


def _rowsum(x):
    return jnp.sum(x, axis=1, keepdims=True)


def _colsum(x):
    return jnp.sum(x, axis=0, keepdims=True)


def _pair_sums(x, lo):
    s0 = _rowsum(jnp.where(lo, x, 0.0))
    s1 = _rowsum(jnp.where(lo, 0.0, x))
    return jnp.where(lo, s0, s1)


def _swap16(x, lo16):
    return jnp.where(lo16, pltpu.roll(x, LANES - 16, 1), pltpu.roll(x, 16, 1))


def _layer_norm_stats(x):
    mu = jnp.mean(x, axis=1, keepdims=True)
    xc = x - mu
    rs = lax.rsqrt(jnp.mean(xc * xc, axis=1, keepdims=True) + LN_EPS)
    return xc * rs, rs


def _layer_norm_bwd(dxn, xn, rs):
    return rs * (dxn - jnp.mean(dxn, axis=1, keepdims=True) - xn * jnp.mean(dxn * xn, axis=1, keepdims=True))


def _group_select(r, grp):
    out = jnp.where(grp == 0, r[0:CHUNK], 0.0)
    for g in range(1, N_SPATIAL_GROUPS):
        out = out + jnp.where(grp == g, r[g * CHUNK:(g + 1) * CHUNK], 0.0)
    return out


def _kv_chunk(t):
    return 1024 if (t - CTX_LEN) % 1024 == 0 else 256


def _all_gather_rows(x_shard):
    m_per, n = x_shard.shape

    def body(x_ref, out_ref, send_sems, recv_sems, local_sem):
        x, y, c = lax.axis_index("x"), lax.axis_index("y"), lax.axis_index("c")
        me, sibling = (x, y, c), (x, y, 1 - c)
        chips = [(1 - x, y), (x, 1 - y), (1 - x, 1 - y)]

        def rows(px, py, pc):
            return out_ref.at[pl.ds((4 * px + 2 * py + pc) * m_per, m_per), :]

        def copy(k, block, to, src=None):
            return pltpu.make_async_remote_copy(
                src_ref=rows(*block) if src is None else src, dst_ref=rows(*block),
                send_sem=send_sems.at[k], recv_sem=recv_sems.at[k], device_id=to, device_id_type=MESH)

        mine = pltpu.make_async_copy(x_ref, rows(*me), local_sem)
        mine.start()
        first = [copy(0, me, sibling, src=x_ref)]
        first += [copy(1 + j, me, (*chip, c), src=x_ref) for j, chip in enumerate(chips)]
        for cp in first:
            cp.start()
        passed = [copy(4 + j, (*chip, c), sibling) for j, chip in enumerate(chips)]
        for j, chip in enumerate(chips):
            copy(1 + j, (*chip, c), me).wait_recv()
            passed[j].start()
        copy(0, sibling, me).wait_recv()
        for j, chip in enumerate(chips):
            copy(4 + j, (*chip, 1 - c), me).wait_recv()
        for cp in first + passed:
            cp.wait_send()
        mine.wait()

    return _pc(
        body, name="all_gather_rows",
        out_shape=jax.ShapeDtypeStruct((8 * m_per, n), x_shard.dtype),
        in_specs=[pl.BlockSpec(memory_space=pltpu.VMEM)],
        out_specs=pl.BlockSpec(memory_space=pltpu.VMEM),
        scratch_shapes=[pltpu.SemaphoreType.DMA((7,)), pltpu.SemaphoreType.DMA((7,)), pltpu.SemaphoreType.DMA],
        compiler_params=pltpu.CompilerParams(vmem_limit_bytes=VMEM_LIMIT),
    )(x_shard)


def _place():
    x, y, c = lax.axis_index("x"), lax.axis_index("y"), lax.axis_index("c")
    return x, y, c, [(1 - x, y), (x, 1 - y), (1 - x, 1 - y)]


def _remote(src, dst, send_sems, recv_sems, k, to):
    return pltpu.make_async_remote_copy(src_ref=src, dst_ref=dst, send_sem=send_sems.at[k], recv_sem=recv_sems.at[k],
                                        device_id=to, device_id_type=MESH)


GATHER_SEMS = 6
SCATTER_SEMS = 3


def _gather_phase(phase, pairs, send_sems, recv_sems, local_sems):
    x, y, c, chips = _place()
    kme = 2 * x + y
    sibling = (x, y, 1 - c)
    for a, (src, dst) in enumerate(pairs):
        half = src.shape[0] // 2
        mine = pl.ds(c * half, half)
        theirs = pl.ds((1 - c) * half, half)
        if phase == 0:
            pltpu.make_async_copy(src, dst.at[kme], local_sems.at[a]).start()
        if phase == 2:
            pltpu.make_async_copy(src, dst.at[kme], local_sems.at[a]).wait()
        for j, (px, py) in enumerate(chips):
            kk = 2 * px + py
            landed = dst.at[kk, mine]
            out = lambda: _remote(src.at[mine], dst.at[kme, mine], send_sems, recv_sems, 6 * a + j, (px, py, c))
            hand = lambda: _remote(landed, landed, send_sems, recv_sems, 6 * a + 3 + j, sibling)
            if phase == 0:
                out().start()
            if phase == 1:
                _remote(landed, landed, send_sems, recv_sems, 6 * a + j, (px, py, c)).wait_recv()
                hand().start()
            if phase == 2:
                other = dst.at[kk, theirs]
                _remote(other, other, send_sems, recv_sems, 6 * a + 3 + j, sibling).wait_recv()
                out().wait_send()
                hand().wait_send()


def _scatter_phase(phase, pairs, send_sems, recv_sems, local_sems):
    x, y, c, chips = _place()
    kme = 2 * x + y
    for a, (src, dst) in enumerate(pairs):
        loc = pltpu.make_async_copy(src.at[kme], dst.at[kme], local_sems.at[a])
        if phase == 0:
            loc.start()
        else:
            loc.wait()
        for j, (px, py) in enumerate(chips):
            kk = 2 * px + py
            out = _remote(src.at[kk], dst.at[kme], send_sems, recv_sems, 3 * a + j, (px, py, c))
            if phase == 0:
                out.start()
            else:
                landed = dst.at[kk]
                _remote(landed, landed, send_sems, recv_sems, 3 * a + j, (px, py, c)).wait_recv()
                out.wait_send()


def _comm_scratch(per_array, n_arrays):
    n = per_array * n_arrays
    return [pltpu.SemaphoreType.DMA((n,)), pltpu.SemaphoreType.DMA((n,)), pltpu.SemaphoreType.DMA((n_arrays,))]


def _slots(a):
    return jax.ShapeDtypeStruct((N_CHIPS,) + a.shape, a.dtype)


def _gather_weights(shards):
    n = len(shards)

    def body(*refs):
        for phase in range(3):
            _gather_phase(phase, tuple(zip(refs[:n], refs[n:2 * n])), *refs[2 * n:])

    hbm = pl.BlockSpec(memory_space=pl.ANY)
    return _pc(
        body, name="gather_weights", out_shape=tuple(_slots(a) for a in shards),
        in_specs=[hbm] * n, out_specs=(hbm,) * n, scratch_shapes=_comm_scratch(GATHER_SEMS, n),
    )(*shards)


def _scatter_slabs(slabs):
    n = len(slabs)

    def body(*refs):
        for phase in range(2):
            _scatter_phase(phase, tuple(zip(refs[:n], refs[n:2 * n])), *refs[2 * n:])

    hbm = pl.BlockSpec(memory_space=pl.ANY)
    return _pc(
        body, name="scatter_slabs", out_shape=tuple(jax.ShapeDtypeStruct(a.shape, a.dtype) for a in slabs),
        in_specs=[hbm] * n, out_specs=(hbm,) * n, scratch_shapes=_comm_scratch(SCATTER_SEMS, n),
    )(*slabs)


def _swap_with_sibling(a, b):
    def body(a_ref, b_ref, ra_ref, rb_ref, send_sems, recv_sems):
        x, y, c = lax.axis_index("x"), lax.axis_index("y"), lax.axis_index("c")
        copies = []
        for k, (src, dst) in enumerate(((a_ref, ra_ref), (b_ref, rb_ref))):
            cp = pltpu.make_async_remote_copy(
                src_ref=src, dst_ref=dst, send_sem=send_sems.at[k], recv_sem=recv_sems.at[k],
                device_id=(x, y, 1 - c), device_id_type=MESH)
            cp.start()
            copies.append(cp)
        for cp in copies:
            cp.wait()

    hbm = pl.BlockSpec(memory_space=pl.ANY)
    return _pc(
        body, name="swap_with_sibling",
        out_shape=(jax.ShapeDtypeStruct(a.shape, a.dtype), jax.ShapeDtypeStruct(b.shape, b.dtype)),
        in_specs=[hbm, hbm], out_specs=(hbm, hbm),
        scratch_shapes=[pltpu.SemaphoreType.DMA((2,)), pltpu.SemaphoreType.DMA((2,))],
    )(a, b)


def _mod_forward(c16, w_mod, b_mod_shard):
    def body(c_ref, w_ref, b_ref, s_ref, o_ref):
        cc = c_ref[...]
        s = cc * _sigmoid(cc)
        s_ref[...] = s
        o_ref[0] = jnp.dot(s, w_ref[0], preferred_element_type=F32, precision=lax.Precision.HIGHEST) + b_ref[0]

    return _pc(
        body, name="mod_forward", grid=(DEPTH,),
        out_shape=(jax.ShapeDtypeStruct((16, D_MODEL), F32), jax.ShapeDtypeStruct((DEPTH, 16, SHARD_MOD), F32)),
        in_specs=[_full((16, D_MODEL)),
                  pl.BlockSpec((1, D_MODEL, SHARD_MOD), lambda l: (l, 0, 0)),
                  pl.BlockSpec((1, 1, SHARD_MOD), lambda l: (l, 0, 0))],
        out_specs=(_full((16, D_MODEL)), pl.BlockSpec((1, 16, SHARD_MOD), lambda l: (l, 0, 0))),
        compiler_params=_cparams(),
    )(c16, w_mod, b_mod_shard)


def _mod_backward(s_t, g_rows, g_ctx, d_all, w_mod, c_ctx_col):
    def body(st_ref, g_ref, gc_ref, d_ref, w_ref, cc_ref, gw_ref, gb_ref, pc_ref):
        l = pl.program_id(0)
        gw_ref[0] = jnp.dot(st_ref[...], g_ref[0], preferred_element_type=F32, precision=lax.Precision.HIGHEST)
        gb_ref[0] = _colsum(d_ref[0])
        part = _rowsum(w_ref[0] * _colsum(gc_ref[0]))

        @pl.when(l == 0)
        def _():
            pc_ref[...] = jnp.zeros_like(pc_ref)

        pc_ref[...] += part

        @pl.when(l == DEPTH - 1)
        def _():
            cc = cc_ref[...]
            sg = _sigmoid(cc)
            pc_ref[...] = pc_ref[...] * (sg * (1.0 + cc * (1.0 - sg)))

    return _pc(
        body, name="mod_backward", grid=(DEPTH,),
        out_shape=(jax.ShapeDtypeStruct((DEPTH, D_MODEL, SHARD_MOD), F32),
                   jax.ShapeDtypeStruct((DEPTH, 1, 3 * D_MODEL), F32),
                   jax.ShapeDtypeStruct((D_MODEL, 1), F32)),
        in_specs=[_full((D_MODEL, LANES)),
                  pl.BlockSpec((1, LANES, SHARD_MOD), lambda l: (l, 0, 0)),
                  pl.BlockSpec((1, 8, SHARD_MOD), lambda l: (l, 0, 0)),
                  pl.BlockSpec((1, 16, 3 * D_MODEL), lambda l: (l, 0, 0)),
                  pl.BlockSpec((1, D_MODEL, SHARD_MOD), lambda l: (l, 0, 0)),
                  _full((D_MODEL, 1))],
        out_specs=(pl.BlockSpec((1, D_MODEL, SHARD_MOD), lambda l: (l, 0, 0)),
                   pl.BlockSpec((1, 1, 3 * D_MODEL), lambda l: (l, 0, 0)),
                   _full((D_MODEL, 1))),
        compiler_params=_cparams(),
    )(s_t, g_rows, g_ctx, d_all, w_mod, c_ctx_col)


def _head_norm(xb, lo):
    r = lax.rsqrt(_pair_sums(xb * xb, lo) * (1.0 / HEAD_DIM) + RMS_EPS)
    return xb * r, r


def _in_proj(xt, modv, g_pre, w_c, w_r, w_q, qk_gain, cos_t, sin_t):
    t = xt.shape[0]

    def body(x_ref, mod_ref, g_ref, wc_ref, wr_ref, wq_ref, gain_ref, cos_ref, sin_ref,
             h_ref, pc_ref, pr_ref, pq_ref, q_ref, k_ref, v_ref):
        lane = _lane(TM)
        lo = lane < HEAD_DIM
        lo16 = (lane & 31) < 16
        one = jnp.where(lane == HEAD_DIM, 1.0, 0.0)
        for jj in range(SUB):
            rows = pl.ds(jj * TM, TM)
            is_ctx = pl.program_id(0) * SUB + jj < N_CTX_TILES
            x = x_ref[rows, :]
            r = lax.rsqrt(jnp.mean(x * x, axis=1, keepdims=True) + RMS_EPS)
            sh = jnp.where(is_ctx, mod_ref[0:1, :], mod_ref[3:4, :])
            sc = jnp.where(is_ctx, mod_ref[1:2, :], mod_ref[4:5, :])
            h = (x * r * g_ref[...]) * (1.0 + sc) + sh
            hb = h.astype(BF16)
            h_ref[rows, :] = hb
            pc_ref[rows, :] = _dot(hb, wc_ref[...])
            pr_ref[rows, :] = _dot(hb, wr_ref[...])
            pq = _dot(hb, wq_ref[...])
            pq_ref[rows, :] = pq
            cos = cos_ref[rows, :]
            sin = sin_ref[rows, :]
            for b in range(3):
                xh, _ = _head_norm(pq[:, b * LANES:(b + 1) * LANES], lo)
                xg = xh * (gain_ref[0:1, :] if b < 2 else gain_ref[1:2, :])
                rot = xg * cos + _swap16(xg, lo16) * sin
                if b < 2:
                    rot = rot * ATTN_SCALE
                dst = q_ref if b < 2 else k_ref
                base = 2 * b if b < 2 else 0
                dst[base, rows, :] = jnp.where(lo, rot, 0.0).astype(BF16)
                dst[base + 1, rows, :] = jnp.where(lo, pltpu.roll(rot, HEAD_DIM, 1), 0.0).astype(BF16)
            vb = pq[:, 3 * LANES:4 * LANES]
            v_ref[0, rows, :] = jnp.where(lo, vb, one).astype(BF16)
            v_ref[1, rows, :] = jnp.where(lo, pltpu.roll(vb, HEAD_DIM, 1), one).astype(BF16)

    return _pc(
        body, name="in_proj", grid=(t // TB,),
        out_shape=(jax.ShapeDtypeStruct((t, D_MODEL), BF16),
                   jax.ShapeDtypeStruct((t, W_C), F32), jax.ShapeDtypeStruct((t, W_R), F32), jax.ShapeDtypeStruct((t, W_Q), F32),
                   jax.ShapeDtypeStruct((N_Q_HEADS, t, LANES), BF16),
                   jax.ShapeDtypeStruct((N_KV_HEADS, t, LANES), BF16),
                   jax.ShapeDtypeStruct((N_KV_HEADS, t, LANES), BF16)),
        in_specs=[_rows(D_MODEL, TB), _const((8, D_MODEL)), _const((1, D_MODEL)),
                  _const((D_MODEL, W_C)), _const((D_MODEL, W_R)), _const((D_MODEL, W_Q)),
                  _const((8, LANES)), _rows(LANES, TB), _rows(LANES, TB)],
        out_specs=(_rows(D_MODEL, TB), _rows(W_C, TB), _rows(W_R, TB), _rows(W_Q, TB),
                   _heads(N_Q_HEADS, LANES, TB), _heads(N_KV_HEADS, LANES, TB), _heads(N_KV_HEADS, LANES, TB)),
        compiler_params=_cparams(),
    )(xt, modv, g_pre, w_c, w_r, w_q, qk_gain, cos_t, sin_t)


def _attention_fwd(q, k, v, shards=None):
    t = q.shape[1]
    tk = _kv_chunk(t)
    n_chunks = (t - CTX_LEN) // tk
    n_tiles = t // TM
    n_sh = 0 if shards is None else len(shards)

    def body(q_ref, k_ref, v_ref, *rest):
        i = pl.program_id(0)
        o_ref = rest[n_sh]
        if shards is not None:
            pairs = tuple(zip(rest[:n_sh], rest[n_sh + 1:2 * n_sh + 1]))
            for phase, at in enumerate((0, n_tiles // 2, n_tiles - 1)):
                @pl.when(i == at)
                def _(phase=phase):
                    _gather_phase(phase, pairs, *rest[2 * n_sh + 1:])
        lane = _lane(GQA * TM)
        qs = [jnp.concatenate([q_ref[GQA * g + hh] for hh in range(GQA)], axis=0) for g in range(N_KV_HEADS)]

        def step(st, size, carry):
            out = []
            for g in range(N_KV_HEADS):
                m, acc = carry[g]
                s = _dot_nt(qs[g], k_ref[g, pl.ds(st, size), :])
                m_new = jnp.maximum(m, jnp.max(s, axis=1, keepdims=True))
                p = jnp.exp(s - m_new)
                out.append((m_new, acc * jnp.exp(m - m_new) + _dot(p.astype(BF16), v_ref[g, pl.ds(st, size), :])))
            return tuple(out)

        init = tuple((jnp.full((GQA * TM, 1), -jnp.inf, F32), jnp.zeros((GQA * TM, LANES), F32)) for _ in range(N_KV_HEADS))

        def finish(carry):
            for g in range(N_KV_HEADS):
                m, acc = carry[g]
                den = _rowsum(jnp.where(lane == HEAD_DIM, acc, 0.0))
                out = jnp.where(lane < HEAD_DIM, acc * (1.0 / den), jnp.where(lane == HEAD_DIM, m + jnp.log(den), 0.0))
                for hh in range(GQA):
                    o_ref[GQA * g + hh] = out[hh * TM:(hh + 1) * TM]

        @pl.when(i < N_CTX_TILES)
        def _():
            finish(step(0, CTX_LEN, init))

        @pl.when(i >= N_CTX_TILES)
        def _():
            per = 8 if n_chunks % 8 == 0 else 1

            def trip(j, cr):
                st = pl.multiple_of(CTX_LEN + j * (per * tk), 256)
                for u in range(per):
                    cr = step(st + u * tk, tk, cr)
                return cr

            finish(lax.fori_loop(0, n_chunks // per, trip, step(0, CTX_LEN, init)))

    hbm = pl.BlockSpec(memory_space=pl.ANY)
    extra = () if shards is None else tuple(shards)
    outs = _pc(
        body, name="attention_fwd" if shards is None else "attention_fwd_gather", grid=(n_tiles,),
        out_shape=(jax.ShapeDtypeStruct((N_Q_HEADS, t, LANES), F32),) + tuple(_slots(a) for a in extra),
        in_specs=[_heads(N_Q_HEADS, LANES), _full((N_KV_HEADS, t, LANES)), _full((N_KV_HEADS, t, LANES))] + [hbm] * n_sh,
        out_specs=(_heads(N_Q_HEADS, LANES),) + (hbm,) * n_sh,
        scratch_shapes=_comm_scratch(GATHER_SEMS, n_sh) if shards is not None else [],
        compiler_params=_cparams(),
    )(q, k, v, *extra)
    return outs[0], tuple(outs[1:])


def _halo_specs(width, t, rows=TM):
    last = t // HALO - 1
    per = rows // HALO
    prev = pl.BlockSpec((HALO, width), lambda i: (jnp.maximum(i * per - 1, 0), 0))
    nxt = pl.BlockSpec((HALO, width), lambda i: (jnp.minimum((i + 1) * per, last), 0))
    return prev, nxt


def _halo_valid(i, n_tiles):
    prev_ok = jnp.logical_and(i != 0, i != N_CTX_TILES)
    next_ok = jnp.logical_and(i != N_CTX_TILES - 1, i != n_tiles - 1)
    return jnp.where(prev_ok, 1.0, 0.0), jnp.where(next_ok, 1.0, 0.0)


def _conv_inputs(pc):
    u = pc[:, 0:GROUP_W] * pc[:, GROUP_W:2 * GROUP_W]
    z = pc[:, 2 * GROUP_W:3 * GROUP_W] * _sigmoid(pc[:, 3 * GROUP_W:4 * GROUP_W])
    return u, z


def _fill_ext(ext_ref, prev, mid, nxt):
    ext_ref[0:HALO, :] = prev
    ext_ref[HALO:HALO + TM, :] = mid
    ext_ref[HALO + TM:HALO + TM + HALO, :] = nxt


def _row_local_mixers(pr, ca, z2, oe, vecs, wss_ref, bsm, lane256):
    a_b, a_g, b_g = pr[:, 0:256], pr[:, 256:512], pr[:, 512:768]
    c_u, c_v, c_g, d_g = pr[:, 768:1024], pr[:, 1024:1280], pr[:, 1280:1536], pr[:, 1536:1792]
    zn, rs_b = _layer_norm_stats(z2)
    tb = zn * vecs[1:2, :] + vecs[2:3, :]
    vn_hat, rs_c = _layer_norm_stats(c_v)
    vn = vn_hat * vecs[3:4, :] + vecs[4:5, :]
    grp = jnp.right_shift(lane256, 6)
    sgs = []
    for ch in range(TM // CHUNK):
        r = _dot(wss_ref[...], vn[ch * CHUNK:(ch + 1) * CHUNK, :].astype(BF16))
        sgs.append(_group_select(r, grp[0:CHUNK]) + bsm)
    sg = jnp.concatenate(sgs, axis=0)
    lane = _lane(TM)
    lo = lane < HEAD_DIM
    att = jnp.concatenate([jnp.where(lo, oe[2 * b], pltpu.roll(oe[2 * b + 1], HEAD_DIM, 1)) for b in range(2)], axis=1)
    return dict(a_b=a_b, a_g=a_g, b_g=b_g, c_u=c_u, c_v=c_v, c_g=c_g, d_g=d_g, zn=zn, rs_b=rs_b, tb=tb,
                vn_hat=vn_hat, rs_c=rs_c, vn=vn, sg=sg, att=att, grp=grp, lo=lo, lane=lane)


def _mixer_concat(f, ca):
    ya = f["a_b"] * ca
    yb = f["tb"] * _sigmoid(f["tb"])
    yc = f["c_u"] * f["sg"]
    gates = [f[n] * _sigmoid(f[n]) for n in ("a_g", "b_g", "c_g", "d_g")]
    ys = (ya, yb, yc, f["att"])
    big = jnp.concatenate([yy * gg for yy, gg in zip(ys, gates)], axis=1).astype(BF16)
    return big, ys, gates


def _taps31(ext_ref, w_ref, flip):
    blocks = []
    for r0 in range(0, TM, CONV_ROWS):
        out = None
        for b in range(8):
            part = None
            for a in range(4):
                o = 8 * a + b
                if 1 <= o <= CONFORMER_K:
                    kk = CONFORMER_K - o if flip else o - 1
                    term = w_ref[kk:kk + 1, :] * ext_ref[pl.ds(r0 + 8 * a, CONV_ROWS + 8), :]
                    part = term if part is None else part + term
            part = part[b:b + CONV_ROWS]
            out = part if out is None else out + part
        blocks.append(out)
    return jnp.concatenate(blocks, axis=0)


def _mix_out(pc, pr, oe, xt, modv, g_post, w_out, conv_a, conv_b, vecs, wss, bsm, target=None):
    t = xt.shape[0]
    n_tiles = t // TM
    prev_spec, next_spec = _halo_specs(W_C, t, TB)
    n_t = 0 if target is None else SUB

    def body(pc_ref, pp_ref, pn_ref, pr_ref, oe_ref, x_ref, mod_ref, gp_ref, wo_ref, cva_ref, cvb_ref, vec_ref, wss_ref, bsm_ref,
             *rest):
        xo_ref, y_ref, ca_ref, z2_ref = rest[n_t:n_t + 4]
        uext, zext = rest[-2:]
        i = pl.program_id(0)
        vecs = vec_ref[...]
        lane256 = lax.broadcasted_iota(jnp.int32, (TM, GROUP_W), 1)
        if target is not None:
            loss_ref = rest[n_t + 4]

            @pl.when(i == 0)
            def _():
                loss_ref[...] = jnp.zeros_like(loss_ref)

        for jj in range(SUB):
            rows = pl.ds(jj * TM, TM)
            tile = i * SUB + jj
            is_ctx = tile < N_CTX_TILES
            pv, nv = _halo_valid(tile, n_tiles)
            u, z = _conv_inputs(pc_ref[rows, :])
            up, zp = _conv_inputs(pp_ref[...] if jj == 0 else pc_ref[pl.ds(jj * TM - HALO, HALO), :])
            un, zn_ = _conv_inputs(pn_ref[...] if jj == SUB - 1 else pc_ref[pl.ds((jj + 1) * TM, HALO), :])
            ue, ze = uext.at[jj], zext.at[jj]
            _fill_ext(ue, up * pv, u, un * nv)
            _fill_ext(ze, zp * pv, z, zn_ * nv)
            ca = cva_ref[0:1, :] * ue[pl.ds(HALO - 1, TM), :]
            for kk in range(1, SHORT_CONV_K):
                ca = ca + cva_ref[kk:kk + 1, :] * ue[pl.ds(HALO - 1 + kk, TM), :]
            z2 = _taps31(ze, cvb_ref, False) + vecs[0:1, :]
            ca_ref[rows, :] = ca
            z2_ref[rows, :] = z2
            oes = [oe_ref[h, rows, :] for h in range(N_Q_HEADS)]
            f = _row_local_mixers(pr_ref[rows, :], ca, z2, oes, vecs, wss_ref, bsm_ref[...], lane256)
            big, _, _ = _mixer_concat(f, ca)
            y = _dot(big, wo_ref[...])
            y_ref[rows, :] = y
            ry = lax.rsqrt(jnp.mean(y * y, axis=1, keepdims=True) + RMS_EPS)
            gt = jnp.where(is_ctx, mod_ref[2:3, :], mod_ref[5:6, :])
            x_new = x_ref[rows, :] + gt * (y * ry * gp_ref[...])
            if target is None:
                xo_ref[rows, :] = x_new
            else:
                err = (x_new - rest[jj][...]) * jnp.where(is_ctx, 0.0, 1.0)
                xo_ref[rows, :] = err * (1.0 / D_MODEL)
                loss_ref[...] += jnp.sum(err * err) * (0.5 / D_MODEL)

    rows_f32 = jax.ShapeDtypeStruct((t, D_MODEL), F32)
    group_f32 = jax.ShapeDtypeStruct((t, GROUP_W), F32)
    loss_shape, loss_spec, t_spec, t_arg = (), (), [], ()
    if target is not None:
        loss_shape, loss_spec = (jax.ShapeDtypeStruct((8, LANES), F32),), (_full((8, LANES)),)
        t_spec = [pl.BlockSpec((TM, D_MODEL), lambda i, jj=jj: (jnp.maximum(i * SUB + jj - N_CTX_TILES, 0), 0))
                  for jj in range(SUB)]
        t_arg = (target,) * SUB
    return _pc(
        body, name="mix_out" if target is None else "mix_out_loss", grid=(t // TB,),
        out_shape=(rows_f32, rows_f32, group_f32, group_f32) + loss_shape,
        in_specs=[_rows(W_C, TB), prev_spec, next_spec, _rows(W_R, TB), _heads(N_Q_HEADS, LANES, TB), _rows(D_MODEL, TB),
                  _const((8, D_MODEL)), _const((1, D_MODEL)), _const((D_MODEL, D_MODEL)),
                  _const((8, GROUP_W)), _const((32, GROUP_W)), _const((8, GROUP_W)),
                  _const((N_SPATIAL_GROUPS * CHUNK, CHUNK)), _const((CHUNK, GROUP_W))] + t_spec,
        out_specs=(_rows(D_MODEL, TB), _rows(D_MODEL, TB), _rows(GROUP_W, TB), _rows(GROUP_W, TB)) + loss_spec,
        scratch_shapes=[pltpu.VMEM((SUB, TM + 2 * HALO, GROUP_W), F32), pltpu.VMEM((SUB, TM + 2 * HALO, GROUP_W), F32)],
        compiler_params=_cparams(),
    )(pc, pc, pc, pr, oe, xt, modv, g_post, w_out, conv_a, conv_b, vecs, wss, bsm, *t_arg)


def _mix_out_bwd(dxo, y, pr, ca, z2, oe, modv, g_post, w_out, vecs, wss, wsts, bsm):
    t = y.shape[0]
    n_tiles = t // TM

    def body(dxo_ref, y_ref, pr_ref, ca_ref, z2_ref, oe_ref, mod_ref, gp_ref, wo_ref, vec_ref, wss_ref, wsts_ref, bsm_ref,
             dpr_ref, ga_ref, gb_ref, doe_ref, dwo_hbm, pvec_ref, s256_ref, dws_ref, dbs_ref, dbsm, dwo_ref):
        i = pl.program_id(0)

        @pl.when(i == 0)
        def _():
            dwo_ref[...] = jnp.zeros_like(dwo_ref)
            pvec_ref[...] = jnp.zeros_like(pvec_ref)
            s256_ref[...] = jnp.zeros_like(s256_ref)
            dws_ref[...] = jnp.zeros_like(dws_ref)
            dbsm[...] = jnp.zeros_like(dbsm)

        gp = gp_ref[...]
        vecs = vec_ref[...]
        bsm_ = bsm_ref[...]
        lane256 = lax.broadcasted_iota(jnp.int32, (TM, GROUP_W), 1)
        grp = jnp.right_shift(lane256, 6)

        for jj in range(SUB):
            tile_rows = pl.ds(jj * TM, TM)
            is_ctx = i * SUB + jj < N_CTX_TILES
            dxo_ = dxo_ref[tile_rows, :]
            y_ = y_ref[tile_rows, :]
            ry = lax.rsqrt(jnp.mean(y_ * y_, axis=1, keepdims=True) + RMS_EPS)
            nh = y_ * ry
            gt = jnp.where(is_ctx, mod_ref[2:3, :], mod_ref[5:6, :])
            dgt = _colsum(dxo_ * (nh * gp))
            pvec_ref[0:1, :] += jnp.where(is_ctx, dgt, 0.0)
            pvec_ref[1:2, :] += jnp.where(is_ctx, 0.0, dgt)
            dn = dxo_ * gt
            pvec_ref[2:3, :] += _colsum(dn * nh)
            dnh = dn * gp
            dy = ry * (dnh - nh * jnp.mean(dnh * nh, axis=1, keepdims=True))

            ca_ = ca_ref[tile_rows, :]
            oes = [oe_ref[h, tile_rows, :] for h in range(N_Q_HEADS)]
            f = _row_local_mixers(pr_ref[tile_rows, :], ca_, z2_ref[tile_rows, :], oes, vecs, wss_ref, bsm_, lane256)
            big, ys, gates = _mixer_concat(f, ca_)
            dyb = dy.astype(BF16)
            dwo_ref[...] += _dot_tn(big, dyb)
            dbig = _dot_nt(dyb, wo_ref[...])

            d_y, d_gate = [], []
            for n, (name, yy, gg) in enumerate(zip(("a_g", "b_g", "c_g", "d_g"), ys, gates)):
                dpart = dbig[:, n * GROUP_W:(n + 1) * GROUP_W]
                gx = f[name]
                sg_ = _sigmoid(gx)
                d_y.append(dpart * gg)
                d_gate.append(dpart * yy * (sg_ * (1.0 + gx * (1.0 - sg_))))
            dya, dyb_, dyc, datt = d_y

            d_ab = dya * ca_
            ga_ref[tile_rows, :] = dya * f["a_b"]
            tb = f["tb"]
            sb = _sigmoid(tb)
            dtb = dyb_ * (sb * (1.0 + tb * (1.0 - sb)))
            s256_ref[1:2, :] += _colsum(dtb * f["zn"])
            s256_ref[2:3, :] += _colsum(dtb)
            dz2 = _layer_norm_bwd(dtb * vecs[1:2, :], f["zn"], f["rs_b"])
            gb_ref[tile_rows, :] = dz2
            s256_ref[0:1, :] += _colsum(dz2)
            d_cu = dyc * f["sg"]
            dsg = dyc * f["c_u"]
            dvn_parts = []
            for ch in range(TM // CHUNK):
                rows = slice(ch * CHUNK, (ch + 1) * CHUNK)
                dsg_c = dsg[rows, :]
                dbsm[...] += dsg_c
                vn_c = f["vn"][rows, :].astype(BF16)
                for g in range(N_SPATIAL_GROUPS):
                    masked = jnp.where(grp[0:CHUNK] == g, dsg_c, 0.0).astype(BF16)
                    dws_ref[g * CHUNK:(g + 1) * CHUNK, :] += _dot_nt(masked, vn_c)
                dvn_parts.append(_group_select(_dot(wsts_ref[...], dsg_c.astype(BF16)), grp[0:CHUNK]))
            dvn = jnp.concatenate(dvn_parts, axis=0)
            s256_ref[3:4, :] += _colsum(dvn * f["vn_hat"])
            s256_ref[4:5, :] += _colsum(dvn)
            d_cv = _layer_norm_bwd(dvn * vecs[3:4, :], f["vn_hat"], f["rs_c"])
            lane, lo = f["lane"], f["lo"]
            att = f["att"]
            for b in range(2):
                da = datt[:, b * LANES:(b + 1) * LANES]
                prod = da * att[:, b * LANES:(b + 1) * LANES]
                for hh in range(2):
                    h = 2 * b + hh
                    lse = _rowsum(jnp.where(lane == HEAD_DIM, oes[h], 0.0))
                    delta = _rowsum(jnp.where(lo, prod, 0.0) if hh == 0 else jnp.where(lo, 0.0, prod))
                    dah = da if hh == 0 else pltpu.roll(da, HEAD_DIM, 1)
                    doe_ref[h, tile_rows, :] = jnp.where(
                        lo, dah, jnp.where(lane == HEAD_DIM, delta, jnp.where(lane == HEAD_DIM + 1, lse, 0.0)))

            dpr_ref[tile_rows, :] = jnp.concatenate(
                [d_ab, d_gate[0], d_gate[1], d_cu, d_cv, d_gate[2], d_gate[3]], axis=1).astype(BF16)

        @pl.when(i == t // TB - 1)
        def _():
            acc = dbsm[...]
            lane128 = _lane(CHUNK)
            out = jnp.zeros((CHUNK, LANES), F32)
            for g in range(N_SPATIAL_GROUPS):
                col = _rowsum(jnp.where(grp[0:CHUNK] == g, acc, 0.0))
                out = out + jnp.where(lane128 == g, col, 0.0)
            dbs_ref[...] = out
            pltpu.sync_copy(dwo_ref, dwo_hbm)

    return _pc(
        body, name="mix_out_bwd", grid=(t // TB,),
        out_shape=(jax.ShapeDtypeStruct((t, W_R), BF16),
                   jax.ShapeDtypeStruct((t, GROUP_W), F32), jax.ShapeDtypeStruct((t, GROUP_W), F32),
                   jax.ShapeDtypeStruct((N_Q_HEADS, t, LANES), F32),
                   jax.ShapeDtypeStruct((D_MODEL, D_MODEL), F32),
                   jax.ShapeDtypeStruct((8, D_MODEL), F32),
                   jax.ShapeDtypeStruct((8, GROUP_W), F32),
                   jax.ShapeDtypeStruct((N_SPATIAL_GROUPS * CHUNK, CHUNK), F32),
                   jax.ShapeDtypeStruct((CHUNK, LANES), F32)),
        in_specs=[_rows(D_MODEL, TB), _rows(D_MODEL, TB), _rows(W_R, TB), _rows(GROUP_W, TB), _rows(GROUP_W, TB),
                  _heads(N_Q_HEADS, LANES, TB),
                  _const((8, D_MODEL)), _const((1, D_MODEL)), _const((D_MODEL, D_MODEL)), _const((8, GROUP_W)),
                  _const((N_SPATIAL_GROUPS * CHUNK, CHUNK)), _const((N_SPATIAL_GROUPS * CHUNK, CHUNK)), _const((CHUNK, GROUP_W))],
        out_specs=(_rows(W_R, TB), _rows(GROUP_W, TB), _rows(GROUP_W, TB), _heads(N_Q_HEADS, LANES, TB),
                   pl.BlockSpec(memory_space=pl.ANY), _full((8, D_MODEL)), _full((8, GROUP_W)),
                   _full((N_SPATIAL_GROUPS * CHUNK, CHUNK)), _full((CHUNK, LANES))),
        scratch_shapes=[pltpu.VMEM((CHUNK, GROUP_W), F32), pltpu.VMEM((D_MODEL, D_MODEL), F32)],
        compiler_params=pltpu.CompilerParams(dimension_semantics=("arbitrary",), vmem_limit_bytes=VMEM_LIMIT_WIDE),
    )(dxo, y, pr, ca, z2, oe, modv, g_post, w_out, vecs, wss, wsts, bsm)


def _conv_bwd(pc, g_a, g_b, conv_a, conv_b):
    t = pc.shape[0]
    n_tiles = t // TM
    pc_prev, pc_next = _halo_specs(W_C, t, TB)
    g_prev, g_next = _halo_specs(GROUP_W, t, TB)

    def body(pc_ref, pp_ref, pn_ref, ga_ref, gap_ref, gan_ref, gb_ref, gbp_ref, gbn_ref, cva_ref, cvb_ref,
             dpc_ref, dca_ref, dcb_ref, uext, zext, gaext, gbext):
        i = pl.program_id(0)

        @pl.when(i == 0)
        def _():
            dca_ref[...] = jnp.zeros_like(dca_ref)
            dcb_ref[...] = jnp.zeros_like(dcb_ref)

        def halo(jj, tile_ref, prev_ref, next_ref):
            before = prev_ref[...] if jj == 0 else tile_ref[pl.ds(jj * TM - HALO, HALO), :]
            after = next_ref[...] if jj == SUB - 1 else tile_ref[pl.ds((jj + 1) * TM, HALO), :]
            return before, after

        for jj in range(SUB):
            rows = pl.ds(jj * TM, TM)
            pv, nv = _halo_valid(i * SUB + jj, n_tiles)
            pc_ = pc_ref[rows, :]
            u, z = _conv_inputs(pc_)
            pc_before, pc_after = halo(jj, pc_ref, pp_ref, pn_ref)
            up, zp = _conv_inputs(pc_before)
            un, zn_ = _conv_inputs(pc_after)
            ue, ze, gae, gbe = uext.at[jj], zext.at[jj], gaext.at[jj], gbext.at[jj]
            _fill_ext(ue, up * pv, u, un * nv)
            _fill_ext(ze, zp * pv, z, zn_ * nv)
            ga = ga_ref[rows, :]
            gb = gb_ref[rows, :]
            ga_before, ga_after = halo(jj, ga_ref, gap_ref, gan_ref)
            gb_before, gb_after = halo(jj, gb_ref, gbp_ref, gbn_ref)
            _fill_ext(gae, ga_before * pv, ga, ga_after * nv)
            _fill_ext(gbe, gb_before * pv, gb, gb_after * nv)

            du = cva_ref[0:1, :] * gae[pl.ds(HALO + 1, TM), :]
            dca_ref[0:1, :] += _colsum(ga * ue[pl.ds(HALO - 1, TM), :])
            for kk in range(1, SHORT_CONV_K):
                du = du + cva_ref[kk:kk + 1, :] * gae[pl.ds(HALO + 1 - kk, TM), :]
                dca_ref[kk:kk + 1, :] += _colsum(ga * ue[pl.ds(HALO - 1 + kk, TM), :])
            dz = _taps31(gbe, cvb_ref, True)
            for r0 in range(0, TM, CONV_ROWS):
                gb_rows = gb_ref[pl.ds(jj * TM + r0, CONV_ROWS), :]
                for b in range(8):
                    zb = ze[pl.ds(r0 + b, CONV_ROWS + 24), :]
                    for a in range(4):
                        kk = 8 * a + b - 1
                        if 0 <= kk < CONFORMER_K:
                            dcb_ref[kk:kk + 1, :] += _colsum(gb_rows * zb[8 * a:8 * a + CONV_ROWS])

            a_c, a_h = pc_[:, 0:GROUP_W], pc_[:, GROUP_W:2 * GROUP_W]
            glu_a, glu_g = pc_[:, 2 * GROUP_W:3 * GROUP_W], pc_[:, 3 * GROUP_W:4 * GROUP_W]
            sg = _sigmoid(glu_g)
            dpc_ref[rows, :] = jnp.concatenate([du * a_h, du * a_c, dz * sg, dz * glu_a * sg * (1.0 - sg)], axis=1).astype(BF16)

    ext = pltpu.VMEM((SUB, TM + 2 * HALO, GROUP_W), F32)
    return _pc(
        body, name="conv_bwd", grid=(t // TB,),
        out_shape=(jax.ShapeDtypeStruct((t, W_C), BF16), jax.ShapeDtypeStruct((8, GROUP_W), F32), jax.ShapeDtypeStruct((32, GROUP_W), F32)),
        in_specs=[_rows(W_C, TB), pc_prev, pc_next, _rows(GROUP_W, TB), g_prev, g_next, _rows(GROUP_W, TB), g_prev, g_next,
                  _const((8, GROUP_W)), _const((32, GROUP_W))],
        out_specs=(_rows(W_C, TB), _full((8, GROUP_W)), _full((32, GROUP_W))),
        scratch_shapes=[ext, ext, ext, ext],
        compiler_params=_cparams(),
    )(pc, pc, pc, g_a, g_a, g_a, g_b, g_b, g_b, conv_a, conv_b)


def _attention_bwd(q, k, v, doe, slabs, conv):
    t = q.shape[1]
    tk = _kv_chunk(t)
    n_chunks = (t - CTX_LEN) // tk
    n_tiles = t // TM
    n_sl = len(slabs)
    rpt = TM // n_chunks
    pc, g_a, g_b, conv_a, conv_b = conv

    def body(q_ref, do_ref, k_ref, v_ref, pc_ref, pp_ref, pn_ref, ga_ref, gap_ref, gan_ref, gb_ref, gbp_ref, gbn_ref,
             cva_ref, cvb_ref, *rest):
        i = pl.program_id(0)
        dq_ref, dk_hbm, dv_hbm, dpc_ref, dca_ref, dcb_ref = rest[n_sl:n_sl + 6]
        dk_acc, dv_acc, uext, zext, gaext, gbext, dz_s = rest[2 * n_sl + 6:2 * n_sl + 13]
        pairs = tuple(zip(rest[:n_sl], rest[n_sl + 6:2 * n_sl + 6]))
        sems = rest[2 * n_sl + 13:]

        @pl.when(i == 0)
        def _():
            dk_acc[...] = jnp.zeros_like(dk_acc)
            dv_acc[...] = jnp.zeros_like(dv_acc)
            dca_ref[...] = jnp.zeros_like(dca_ref)
            dcb_ref[...] = jnp.zeros_like(dcb_ref)
            _scatter_phase(0, pairs, *sems)

        pv, nv = _halo_valid(i, n_tiles)
        pc_ = pc_ref[...]
        u, z = _conv_inputs(pc_)
        up, zp = _conv_inputs(pp_ref[...])
        un, zn_ = _conv_inputs(pn_ref[...])
        _fill_ext(uext, up * pv, u, un * nv)
        _fill_ext(zext, zp * pv, z, zn_ * nv)
        _fill_ext(gaext, gap_ref[...] * pv, ga_ref[...], gan_ref[...] * nv)
        _fill_ext(gbext, gbp_ref[...] * pv, gb_ref[...], gbn_ref[...] * nv)

        def conv_rows(r0):
            out = None
            for b in range(8):
                part = None
                for a in range(4):
                    o = 8 * a + b
                    if 1 <= o <= CONFORMER_K:
                        term = cvb_ref[CONFORMER_K - o:CONFORMER_K - o + 1, :] * gbext[pl.ds(r0 + 8 * a, rpt + 8), :]
                        part = term if part is None else part + term
                part = part[b:b + rpt]
                out = part if out is None else out + part
            dz_s[pl.ds(r0, rpt), :] = out
            gb_rows = gb_ref[pl.ds(r0, rpt), :]
            window = zext[pl.ds(r0, rpt + 2 * HALO), :]
            for b in range(8):
                zb = window[b:b + rpt + 24]
                for a in range(4):
                    kk = 8 * a + b - 1
                    if 0 <= kk < CONFORMER_K:
                        dcb_ref[kk:kk + 1, :] += _colsum(gb_rows * zb[8 * a:8 * a + rpt])

        lane = _lane(GQA * TM)
        lo = lane < HEAD_DIM
        qs, dos, deltas, lses = [], [], [], []
        for g in range(N_KV_HEADS):
            qs.append(jnp.concatenate([q_ref[GQA * g + hh] for hh in range(GQA)], axis=0))
            dog = jnp.concatenate([do_ref[GQA * g + hh] for hh in range(GQA)], axis=0)
            deltas.append(_rowsum(jnp.where(lane == HEAD_DIM, dog, 0.0)))
            lses.append(_rowsum(jnp.where(lane == HEAD_DIM + 1, dog, 0.0)))
            dos.append(jnp.where(lo, dog, 0.0).astype(BF16))

        def step(st, size, dqs):
            out = []
            for g in range(N_KV_HEADS):
                kc = k_ref[g, pl.ds(st, size), :]
                vc = v_ref[g, pl.ds(st, size), :]
                p = jnp.exp(_dot_nt(qs[g], kc) - lses[g])
                ds_ = (p * (_dot_nt(dos[g], vc) - deltas[g])).astype(BF16)
                dk_acc[g, pl.ds(st, size), :] += _dot_tn(ds_, qs[g])
                dv_acc[g, pl.ds(st, size), :] += _dot_tn(p.astype(BF16), dos[g])
                out.append(dqs[g] + _dot(ds_, kc))
            return tuple(out)

        zero = tuple(jnp.zeros((GQA * TM, LANES), F32) for _ in range(N_KV_HEADS))

        def finish(dqs):
            for g in range(N_KV_HEADS):
                for hh in range(GQA):
                    dq_ref[GQA * g + hh] = dqs[g][hh * TM:(hh + 1) * TM]

        @pl.when(i < N_CTX_TILES)
        def _():
            finish(step(0, CTX_LEN, zero))
            for r in range(n_chunks):
                conv_rows(r * rpt)

        @pl.when(i >= N_CTX_TILES)
        def _():
            def trip(j, acc):
                conv_rows(pl.multiple_of(j * rpt, rpt))
                return step(pl.multiple_of(CTX_LEN + j * tk, 256), tk, acc)

            finish(lax.fori_loop(0, n_chunks, trip, step(0, CTX_LEN, zero)))

        ga = ga_ref[...]
        du = cva_ref[0:1, :] * gaext[pl.ds(HALO + 1, TM), :]
        dca_ref[0:1, :] += _colsum(ga * uext[pl.ds(HALO - 1, TM), :])
        for kk in range(1, SHORT_CONV_K):
            du = du + cva_ref[kk:kk + 1, :] * gaext[pl.ds(HALO + 1 - kk, TM), :]
            dca_ref[kk:kk + 1, :] += _colsum(ga * uext[pl.ds(HALO - 1 + kk, TM), :])
        dz = dz_s[...]
        a_c, a_h = pc_[:, 0:GROUP_W], pc_[:, GROUP_W:2 * GROUP_W]
        glu_a, glu_g = pc_[:, 2 * GROUP_W:3 * GROUP_W], pc_[:, 3 * GROUP_W:4 * GROUP_W]
        sg = _sigmoid(glu_g)
        dpc_ref[...] = jnp.concatenate([du * a_h, du * a_c, dz * sg, dz * glu_a * sg * (1.0 - sg)], axis=1).astype(BF16)

        @pl.when(i == n_tiles - 1)
        def _():
            pltpu.sync_copy(dk_acc, dk_hbm)
            pltpu.sync_copy(dv_acc, dv_hbm)
            _scatter_phase(1, pairs, *sems)

    kv_shape = jax.ShapeDtypeStruct((N_KV_HEADS, t, LANES), F32)
    hbm = pl.BlockSpec(memory_space=pl.ANY)
    pc_prev, pc_next = _halo_specs(W_C, t)
    g_prev, g_next = _halo_specs(GROUP_W, t)
    ext = pltpu.VMEM((TM + 2 * HALO, GROUP_W), F32)
    outs = _pc(
        body, name="attention_bwd_scatter", grid=(n_tiles,),
        out_shape=(jax.ShapeDtypeStruct((N_Q_HEADS, t, LANES), F32), kv_shape, kv_shape,
                   jax.ShapeDtypeStruct((t, W_C), BF16), jax.ShapeDtypeStruct((8, GROUP_W), F32),
                   jax.ShapeDtypeStruct((32, GROUP_W), F32)) + tuple(jax.ShapeDtypeStruct(a.shape, a.dtype) for a in slabs),
        in_specs=[_heads(N_Q_HEADS, LANES), _heads(N_Q_HEADS, LANES),
                  _const((N_KV_HEADS, t, LANES)), _const((N_KV_HEADS, t, LANES)),
                  _rows(W_C), pc_prev, pc_next, _rows(GROUP_W), g_prev, g_next, _rows(GROUP_W), g_prev, g_next,
                  _const((8, GROUP_W)), _const((32, GROUP_W))] + [hbm] * n_sl,
        out_specs=(_heads(N_Q_HEADS, LANES), hbm, hbm, _rows(W_C), _full((8, GROUP_W)), _full((32, GROUP_W))) + (hbm,) * n_sl,
        scratch_shapes=[pltpu.VMEM((N_KV_HEADS, t, LANES), F32), pltpu.VMEM((N_KV_HEADS, t, LANES), F32),
                        ext, ext, ext, ext, pltpu.VMEM((TM, GROUP_W), F32)] + _comm_scratch(SCATTER_SEMS, n_sl),
        compiler_params=_cparams(),
    )(q, doe, k, v, pc, pc, pc, g_a, g_a, g_a, g_b, g_b, g_b, conv_a, conv_b, *slabs)
    return outs[0], outs[1], outs[2], outs[3:6], tuple(outs[6:])


def _in_proj_bwd(dpc, dpr, dq, dk, dv, pq, qk_gain, cos_t, sin_t, w_c, w_r, w_q, xt, dxo, modv, g_pre):
    t = xt.shape[0]

    def body(dpc_ref, dpr_ref, dq_ref, dk_ref, dv_ref, pq_ref, gain_ref, cos_ref, sin_ref, wc_ref, wr_ref, wq_ref,
             x_ref, dxo_ref, mod_ref, g_ref, dx_ref, dpq_ref, acc_ref, dgain_ref):
        i = pl.program_id(0)

        @pl.when(i == 0)
        def _():
            acc_ref[...] = jnp.zeros_like(acc_ref)
            dgain_ref[...] = jnp.zeros_like(dgain_ref)

        lane = _lane(TM)
        lo = lane < HEAD_DIM
        lo16 = (lane & 31) < 16
        g = g_ref[...]
        for jj in range(SUB):
            rows = pl.ds(jj * TM, TM)
            is_ctx = i * SUB + jj < N_CTX_TILES
            cos = cos_ref[rows, :]
            sin = sin_ref[rows, :]
            outs = []
            for b in range(3):
                src = dq_ref if b < 2 else dk_ref
                base = 2 * b if b < 2 else 0
                drot = src[base, rows, :] + pltpu.roll(src[base + 1, rows, :], HEAD_DIM, 1)
                if b < 2:
                    drot = drot * ATTN_SCALE
                dxg = drot * cos + _swap16(drot * sin, lo16)
                xh, r = _head_norm(pq_ref[rows, b * LANES:(b + 1) * LANES], lo)
                row = 0 if b < 2 else 1
                dgain_ref[row:row + 1, :] += _colsum(dxg * xh)
                dxh = dxg * gain_ref[row:row + 1, :]
                outs.append(r * (dxh - xh * (_pair_sums(dxh * xh, lo) * (1.0 / HEAD_DIM))))
            outs.append(dv_ref[0, rows, :] + pltpu.roll(dv_ref[1, rows, :], HEAD_DIM, 1))
            dpq = jnp.concatenate(outs, axis=1).astype(BF16)
            dpq_ref[rows, :] = dpq

            dh = _dot_nt(dpc_ref[rows, :], wc_ref[...]) + _dot_nt(dpr_ref[rows, :], wr_ref[...]) + _dot_nt(dpq, wq_ref[...])
            x = x_ref[rows, :]
            r = lax.rsqrt(jnp.mean(x * x, axis=1, keepdims=True) + RMS_EPS)
            xn = x * r
            sc = jnp.where(is_ctx, mod_ref[1:2, :], mod_ref[4:5, :])
            dsh = _colsum(dh)
            dsc = _colsum(dh * (xn * g))
            acc_ref[0:1, :] += jnp.where(is_ctx, dsh, 0.0)
            acc_ref[1:2, :] += jnp.where(is_ctx, dsc, 0.0)
            acc_ref[2:3, :] += jnp.where(is_ctx, 0.0, dsh)
            acc_ref[3:4, :] += jnp.where(is_ctx, 0.0, dsc)
            dxg = dh * (1.0 + sc)
            acc_ref[4:5, :] += _colsum(dxg * xn)
            dxn = dxg * g
            dx_ref[rows, :] = r * (dxn - xn * jnp.mean(dxn * xn, axis=1, keepdims=True)) + dxo_ref[rows, :]

    return _pc(
        body, name="in_proj_bwd", grid=(t // TB,),
        out_shape=(jax.ShapeDtypeStruct((t, D_MODEL), F32), jax.ShapeDtypeStruct((t, W_Q), BF16),
                   jax.ShapeDtypeStruct((8, D_MODEL), F32), jax.ShapeDtypeStruct((8, LANES), F32)),
        in_specs=[_rows(W_C, TB), _rows(W_R, TB),
                  _heads(N_Q_HEADS, LANES, TB), _heads(N_KV_HEADS, LANES, TB), _heads(N_KV_HEADS, LANES, TB), _rows(W_Q, TB),
                  _const((8, LANES)), _rows(LANES, TB), _rows(LANES, TB),
                  _const((D_MODEL, W_C)), _const((D_MODEL, W_R)), _const((D_MODEL, W_Q)),
                  _rows(D_MODEL, TB), _rows(D_MODEL, TB), _const((8, D_MODEL)), _const((1, D_MODEL))],
        out_specs=(_rows(D_MODEL, TB), _rows(W_Q, TB), _full((8, D_MODEL)), _full((8, LANES))),
        compiler_params=_cparams(),
    )(dpc, dpr, dq, dk, dv, pq, qk_gain, cos_t, sin_t, w_c, w_r, w_q, xt, dxo, modv, g_pre)


def _in_proj_wgrad(h, dpc, dpr, dpq):
    t = h.shape[0]

    def body(h_ref, dpc_ref, dpr_ref, dpq_ref, gc_ref, gr_ref, gq_ref):
        @pl.when(pl.program_id(0) == 0)
        def _():
            gc_ref[...] = jnp.zeros_like(gc_ref)
            gr_ref[...] = jnp.zeros_like(gr_ref)
            gq_ref[...] = jnp.zeros_like(gq_ref)

        hb = h_ref[...]
        gc_ref[...] += _dot_tn(hb, dpc_ref[...])
        gr_ref[...] += _dot_tn(hb, dpr_ref[...])
        gq_ref[...] += _dot_tn(hb, dpq_ref[...])

    return _pc(
        body, name="in_proj_wgrad", grid=(t // TB,),
        out_shape=(jax.ShapeDtypeStruct((D_MODEL, W_C), F32), jax.ShapeDtypeStruct((D_MODEL, W_R), F32),
                   jax.ShapeDtypeStruct((D_MODEL, W_Q), F32)),
        in_specs=[_rows(D_MODEL, TB), _rows(W_C, TB), _rows(W_R, TB), _rows(W_Q, TB)],
        out_specs=(_full((D_MODEL, W_C)), _full((D_MODEL, W_R)), _full((D_MODEL, W_Q))),
        compiler_params=_cparams(),
    )(h, dpc, dpr, dpq)


def _sum_slabs(slabs, tile_rows):
    n, r, c = slabs.shape

    def body(s_ref, o_ref):
        acc = s_ref[0].astype(F32)
        for k in range(1, n):
            acc = acc + s_ref[k].astype(F32)
        o_ref[...] = acc

    return _pc(
        body, name="sum_slabs", grid=(r // tile_rows,),
        out_shape=jax.ShapeDtypeStruct((r, c), F32),
        in_specs=[pl.BlockSpec((n, tile_rows, c), lambda i: (0, i, 0))],
        out_specs=pl.BlockSpec((tile_rows, c), lambda i: (i, 0)),
        compiler_params=_cparams(),
    )(slabs)


def _sum_layer_slabs(layers, tile_rows):
    nl = len(layers)
    n, r, c = layers[0].shape
    per = r // tile_rows

    def body(*refs):
        o_ref = refs[nl]
        for l in range(nl):
            @pl.when(pl.program_id(0) // per == l)
            def _(l=l):
                acc = refs[l][0].astype(F32)
                for k in range(1, n):
                    acc = acc + refs[l][k].astype(F32)
                o_ref[...] = acc

    def spec(l):
        return pl.BlockSpec((n, tile_rows, c), lambda i: (0, jnp.clip(i - l * per, 0, per - 1), 0))

    return _pc(
        body, name="sum_layer_slabs", grid=(nl * per,),
        out_shape=jax.ShapeDtypeStruct((nl * r, c), F32),
        in_specs=[spec(l) for l in range(nl)],
        out_specs=pl.BlockSpec((tile_rows, c), lambda i: (i, 0)),
        compiler_params=_cparams(),
    )(*layers)


def _adamw(grads, w, m, v, tile_rows):
    r, c = w.shape
    n_g = len(grads)

    def body(*refs):
        g = refs[0][...]
        for k in range(1, n_g):
            g = g + refs[k][...]
        w_ref, m_ref, v_ref, g_out, d_out, m_out, v_out = refs[n_g:]
        m_new = ADAM_B1 * m_ref[...] + (1.0 - ADAM_B1) * g
        v_new = ADAM_B2 * v_ref[...] + (1.0 - ADAM_B2) * (g * g)
        m_hat = m_new / (1.0 - ADAM_B1 ** ADAM_STEP)
        v_hat = v_new / (1.0 - ADAM_B2 ** ADAM_STEP)
        g_out[...] = g
        d_out[...] = -ADAM_LR * (m_hat / (jnp.sqrt(v_hat) + ADAM_EPS) + ADAM_WD * w_ref[...])
        m_out[...] = m_new
        v_out[...] = v_new

    spec = pl.BlockSpec((tile_rows, c), lambda i: (i, 0))
    shape = jax.ShapeDtypeStruct((r, c), F32)
    return _pc(
        body, name="adamw", grid=(r // tile_rows,),
        out_shape=(shape,) * 4, in_specs=[spec] * (n_g + 3), out_specs=(spec,) * 4,
        compiler_params=_cparams(),
    )(*grads, w, m, v)


def _rope_tables(s_lat):
    n_rows = s_lat // GRID_W
    axis_dim = HEAD_DIM // 2
    inv_freq = 1.0 / (ROPE_THETA ** (jnp.arange(0, axis_dim, 2, dtype=F32) / axis_dim))
    d = np.arange(LANES) % HEAD_DIM
    on_rows = (d // axis_dim) == 0
    freq = d % (axis_dim // 2)
    sign = np.where((d % axis_dim) < axis_dim // 2, -1.0, 1.0).astype(np.float32)
    ang_r = jnp.arange(n_rows, dtype=F32)[:, None] * inv_freq[freq][None, :]
    ang_c = jnp.arange(GRID_W, dtype=F32)[:, None] * inv_freq[freq][None, :]

    def spread(fn):
        full = jnp.where(on_rows[None, None, :], fn(ang_r)[:, None, :], fn(ang_c)[None, :, :])
        return full.reshape(s_lat, LANES)

    cos = jnp.concatenate([jnp.ones((CTX_LEN, LANES), F32), spread(jnp.cos)], axis=0)
    sin = jnp.concatenate([jnp.zeros((CTX_LEN, LANES), F32), spread(jnp.sin) * sign[None, :]], axis=0)
    return cos, sin


def _pad_rows(a, rows):
    return jnp.concatenate([a, jnp.zeros((rows - a.shape[0],) + a.shape[1:], a.dtype)], axis=0)


_SMALL = ("c_ctx", "b_mod", "g_pre", "g_post", "conv_a", "conv_b", "conv_b_bias", "conf_ln_g", "conf_ln_b",
          "sgu_ln_g", "sgu_ln_b", "w_s", "b_s", "q_gain", "k_gain")


def _pack(arrays):
    flat = jnp.concatenate([a.reshape(-1) for a in arrays])
    rows = -(-flat.shape[0] // (16 * LANES)) * 16
    return _pad_rows(flat.reshape(-1, 1), rows * LANES).reshape(rows, LANES)


def _unpack(packed, shapes):
    flat = packed.reshape(-1)
    out, off = [], 0
    for s in shapes:
        n = int(np.prod(s))
        out.append(flat[off:off + n].reshape(s))
        off += n
    return out


def kernel(x, c, ctx, c_ctx, w_mod, b_mod, g_pre, g_post, w_in, w_out, conv_a, conv_b, conv_b_bias, conf_ln_g, conf_ln_b, sgu_ln_g, sgu_ln_b, w_s, b_s, q_gain, k_gain, loss_target, m_c_ctx, m_w_mod, m_b_mod, m_g_pre, m_g_post, m_w_in, m_w_out, m_conv_a, m_conv_b, m_conv_b_bias, m_conf_ln_g, m_conf_ln_b, m_sgu_ln_g, m_sgu_ln_b, m_w_s, m_b_s, m_q_gain, m_k_gain, v_c_ctx, v_w_mod, v_b_mod, v_g_pre, v_g_post, v_w_in, v_w_out, v_conv_a, v_conv_b, v_conv_b_bias, v_conf_ln_g, v_conf_ln_b, v_sgu_ln_g, v_sgu_ln_b, v_w_s, v_b_s, v_q_gain, v_k_gain):
    weights = dict(c_ctx=c_ctx, w_mod=w_mod, b_mod=b_mod, g_pre=g_pre, g_post=g_post, w_in=w_in, w_out=w_out, conv_a=conv_a,
                   conv_b=conv_b, conv_b_bias=conv_b_bias, conf_ln_g=conf_ln_g, conf_ln_b=conf_ln_b, sgu_ln_g=sgu_ln_g,
                   sgu_ln_b=sgu_ln_b, w_s=w_s, b_s=b_s, q_gain=q_gain, k_gain=k_gain)
    m_in = dict(c_ctx=m_c_ctx, w_mod=m_w_mod, b_mod=m_b_mod, g_pre=m_g_pre, g_post=m_g_post, w_in=m_w_in, w_out=m_w_out,
                conv_a=m_conv_a, conv_b=m_conv_b, conv_b_bias=m_conv_b_bias, conf_ln_g=m_conf_ln_g, conf_ln_b=m_conf_ln_b,
                sgu_ln_g=m_sgu_ln_g, sgu_ln_b=m_sgu_ln_b, w_s=m_w_s, b_s=m_b_s, q_gain=m_q_gain, k_gain=m_k_gain)
    v_in = dict(c_ctx=v_c_ctx, w_mod=v_w_mod, b_mod=v_b_mod, g_pre=v_g_pre, g_post=v_g_post, w_in=v_w_in, w_out=v_w_out,
                conv_a=v_conv_a, conv_b=v_conv_b, conv_b_bias=v_conv_b_bias, conf_ln_g=v_conf_ln_g, conf_ln_b=v_conf_ln_b,
                sgu_ln_g=v_sgu_ln_g, sgu_ln_b=v_sgu_ln_b, w_s=v_w_s, b_s=v_b_s, q_gain=v_q_gain, k_gain=v_k_gain)
    order = ("c_ctx", "w_mod", "b_mod", "g_pre", "g_post", "w_in", "w_out", "conv_a", "conv_b", "conv_b_bias", "conf_ln_g",
             "conf_ln_b", "sgu_ln_g", "sgu_ln_b", "w_s", "b_s", "q_gain", "k_gain")

    s_lat = x.shape[1]
    ax, ay, ac = lax.axis_index("x"), lax.axis_index("y"), lax.axis_index("c")
    chip = 2 * ax + ay
    example = 4 * ax + 2 * ay + ac

    conv_small = jnp.concatenate([conv_a.reshape(DEPTH * SHORT_CONV_K, -1), conv_b.reshape(DEPTH * CONFORMER_K, -1)], axis=0)
    n_cs, ch_shard = conv_small.shape
    head = _pack([c, conv_small])
    head_all = _all_gather_rows(head).reshape(8, -1)
    c_rows = head_all[:, :D_MODEL]
    c16 = _pad_rows(jnp.concatenate([c_rows, c_ctx[None, :]], axis=0), 16)
    b_mod_shard = lax.dynamic_slice_in_dim(b_mod, chip * SHARD_MOD, SHARD_MOD, axis=1)[:, None, :]
    silu_c, mod_shard = _mod_forward(c16, w_mod, b_mod_shard)
    mod_all = _all_gather_rows(mod_shard.reshape(DEPTH * 16, SHARD_MOD)).reshape(8, DEPTH, 16, SHARD_MOD)
    mod_full = jnp.transpose(mod_all[::2], (1, 2, 0, 3)).reshape(DEPTH, 16, 3 * D_MODEL)
    mod_lat = lax.dynamic_index_in_dim(mod_full, example, axis=1, keepdims=False).reshape(DEPTH, 3, D_MODEL)
    mod_ctx = mod_full[:, 8].reshape(DEPTH, 3, D_MODEL)
    modv = jnp.concatenate([mod_ctx, mod_lat, jnp.zeros((DEPTH, 2, D_MODEL), F32)], axis=1)

    wi_b, wo_b = w_in.astype(BF16), w_out.astype(BF16)

    def regroup(wi_all):
        wi_full = jnp.concatenate([wi_all[k] for k in range(N_CHIPS)], axis=-1)
        wc_l = jnp.concatenate([wi_full[:, 256:768], wi_full[:, 1024:1536]], axis=-1)
        wr_l = jnp.concatenate([wi_full[:, 0:256], wi_full[:, 768:1024], wi_full[:, 1536:2560], wi_full[:, 3072:3328]], axis=-1)
        return wc_l, wr_l, wi_full[:, 2560:3072]

    w_c, w_r, w_q, wo_full = [None] * DEPTH, [None] * DEPTH, [None] * DEPTH, [None] * DEPTH
    w_c[0], w_r[0], w_q[0] = regroup(_gather_weights((wi_b[0],))[0])

    cos_t, sin_t = _rope_tables(s_lat)
    conv_a_full = jnp.zeros((DEPTH, 8, GROUP_W), F32)
    conv_b_full = jnp.zeros((DEPTH, 32, GROUP_W), F32)
    conv_all = head_all[::2, D_MODEL:D_MODEL + n_cs * ch_shard].reshape(N_CHIPS, n_cs, ch_shard)
    conv_all = jnp.transpose(conv_all, (1, 0, 2)).reshape(n_cs, GROUP_W)
    conv_a_full = conv_a_full.at[:, :SHORT_CONV_K].set(conv_all[:DEPTH * SHORT_CONV_K].reshape(DEPTH, SHORT_CONV_K, GROUP_W))
    conv_b_full = conv_b_full.at[:, :CONFORMER_K].set(
        conv_all[DEPTH * SHORT_CONV_K:n_cs].reshape(DEPTH, CONFORMER_K, GROUP_W))

    vecs = jnp.stack([conv_b_bias, conf_ln_g, conf_ln_b, sgu_ln_g, sgu_ln_b] + [jnp.zeros_like(conv_b_bias)] * 3, axis=1)
    wss = w_s.reshape(DEPTH, N_SPATIAL_GROUPS * CHUNK, CHUNK).astype(BF16)
    wsts = jnp.swapaxes(w_s, 2, 3).reshape(DEPTH, N_SPATIAL_GROUPS * CHUNK, CHUNK).astype(BF16)
    bsm = jnp.repeat(jnp.swapaxes(b_s, 1, 2), HEAD_DIM, axis=2)
    qk_gain = jnp.concatenate([jnp.tile(q_gain, (1, 2))[:, None, :], jnp.tile(k_gain, (1, 2))[:, None, :],
                               jnp.zeros((DEPTH, 6, LANES), F32)], axis=1)

    xt = jnp.concatenate([ctx[0], x[0]], axis=0)
    saved = []
    for l in range(DEPTH):
        h, pc, pr, pq, q, k, v = _in_proj(xt, modv[l], g_pre[l][None, :], w_c[l], w_r[l], w_q[l], qk_gain[l], cos_t, sin_t)
        oe, gathered = _attention_fwd(q, k, v, (wo_b[l],) + ((wi_b[l + 1],) if l + 1 < DEPTH else ()))
        wo_full[l] = jnp.concatenate([gathered[0][k] for k in range(N_CHIPS)], axis=0)
        if l + 1 < DEPTH:
            w_c[l + 1], w_r[l + 1], w_q[l + 1] = regroup(gathered[1])
        mixed = _mix_out(pc, pr, oe, xt, modv[l], g_post[l][None, :], wo_full[l], conv_a_full[l], conv_b_full[l],
                         vecs[l], wss[l], bsm[l], loss_target[0] if l + 1 == DEPTH else None)
        x_new, y, ca, z2 = mixed[:4]
        saved.append(dict(x=xt, h=h, pc=pc, pr=pr, pq=pq, q=q, k=k, v=v, oe=oe, y=y, ca=ca, z2=z2))
        xt = x_new
    dxo = xt
    loss = lax.psum(mixed[4][0, 0], ("x", "y", "c"))

    g_small = {n: [None] * DEPTH for n in _SMALL}
    d_mod, landed_in, landed_out = [None] * DEPTH, [None] * DEPTH, [None] * DEPTH
    slab_in = None
    for l in reversed(range(DEPTH)):
        s = saved[l]
        dpr, g_a, g_b, doe, gw_o, pvec, s256, dws, dbs = _mix_out_bwd(
            dxo, s["y"], s["pr"], s["ca"], s["z2"], s["oe"], modv[l], g_post[l][None, :], wo_full[l], vecs[l], wss[l], wsts[l], bsm[l])
        slab_out = gw_o.reshape(N_CHIPS, SHARD_OUT, D_MODEL).astype(BF16)
        dq, dk, dv, (dpc, dca, dcb), got = _attention_bwd(
            s["q"], s["k"], s["v"], doe, (slab_out,) + (() if slab_in is None else (slab_in,)),
            (s["pc"], g_a, g_b, conv_a_full[l], conv_b_full[l]))
        landed_out[l] = got[0]
        if slab_in is not None:
            landed_in[l + 1] = got[1]
        dxo, dpq, acc, dgain = _in_proj_bwd(dpc, dpr, dq, dk, dv, s["pq"], qk_gain[l], cos_t, sin_t, w_c[l], w_r[l], w_q[l],
                                            s["x"], dxo, modv[l], g_pre[l][None, :])
        gw_c, gw_r, gw_q = _in_proj_wgrad(s["h"], dpc, dpr, dpq)
        gw_in = jnp.concatenate([gw_r[:, 0:256], gw_c[:, 0:512], gw_r[:, 256:512], gw_c[:, 512:1024],
                                 gw_r[:, 512:1536], gw_q, gw_r[:, 1536:1792]], axis=-1)
        slab_in = jnp.transpose(gw_in.reshape(D_MODEL, N_CHIPS, SHARD_IN), (1, 0, 2)).astype(BF16)
        d_mod[l] = jnp.stack([jnp.concatenate([acc[2], acc[3], pvec[1]]), jnp.concatenate([acc[0], acc[1], pvec[0]])])
        g_small["g_pre"][l] = acc[4]
        g_small["g_post"][l] = pvec[2]
        g_small["conv_a"][l] = dca[:SHORT_CONV_K]
        g_small["conv_b"][l] = dcb[:CONFORMER_K]
        g_small["conv_b_bias"][l] = s256[0]
        g_small["conf_ln_g"][l] = s256[1]
        g_small["conf_ln_b"][l] = s256[2]
        g_small["sgu_ln_g"][l] = s256[3]
        g_small["sgu_ln_b"][l] = s256[4]
        g_small["w_s"][l] = dws.reshape(N_SPATIAL_GROUPS, CHUNK, CHUNK)
        g_small["b_s"][l] = jnp.transpose(dbs[:, :N_SPATIAL_GROUPS])
        g_small["q_gain"][l] = dgain[0, :HEAD_DIM] + dgain[0, HEAD_DIM:]
        g_small["k_gain"][l] = dgain[1, :HEAD_DIM] + dgain[1, HEAD_DIM:]
    grad_x = dxo[CTX_LEN:][None]

    d_mod_all = _all_gather_rows(jnp.stack(d_mod).reshape(DEPTH * 2, 3 * D_MODEL)).reshape(8, DEPTH, 2, 3 * D_MODEL)
    d_lat = jnp.transpose(d_mod_all[:, :, 0], (1, 0, 2))
    d_ctx = jnp.transpose(d_mod_all[:, :, 1], (1, 0, 2))
    cols = lambda a: lax.dynamic_slice_in_dim(a.reshape(DEPTH, 8, N_CHIPS, SHARD_MOD), chip, 1, axis=2)[:, :, 0]
    silu_t = jnp.transpose(silu_c)
    s_t = jnp.concatenate([silu_t[:, 0:8], jnp.tile(silu_t[:, 8:9], (1, 8)), jnp.zeros((D_MODEL, LANES - 16), F32)], axis=1)
    g_rows = jnp.concatenate([cols(d_lat), cols(d_ctx), jnp.zeros((DEPTH, LANES - 16, SHARD_MOD), F32)], axis=1)
    g_w_mod, g_b_mod, c_ctx_part = _mod_backward(s_t, g_rows, cols(d_ctx), jnp.concatenate([d_lat, d_ctx], axis=1),
                                                 w_mod, c_ctx[:, None])

    for n in _SMALL:
        if n not in ("c_ctx", "b_mod"):
            g_small[n] = jnp.stack(g_small[n])
    small_parts = [0.5 * c_ctx_part[:, 0]] + [g_small[n] for n in _SMALL[2:]]
    packed = _pack(small_parts)
    gathered = _all_gather_rows(packed.astype(BF16)).reshape(8, packed.shape[0], LANES)
    small_sum = _sum_slabs(gathered, packed.shape[0])
    small_g = dict(zip(("c_ctx",) + _SMALL[2:], _unpack(small_sum, [p.shape for p in small_parts])))
    small_g["b_mod"] = g_b_mod[:, 0]
    ch64 = GROUP_W // N_CHIPS
    for n in ("conv_a", "conv_b"):
        small_g[n] = lax.dynamic_slice_in_dim(small_g[n], chip * ch64, ch64, axis=2)
    sw = _pack([weights[n] for n in _SMALL])
    sm = _pack([m_in[n] for n in _SMALL])
    sv = _pack([v_in[n] for n in _SMALL])
    sg = _pack([small_g[n] for n in _SMALL])
    shapes = [weights[n].shape for n in _SMALL]
    small_out = [dict(zip(_SMALL, _unpack(o, shapes))) for o in _adamw([sg], sw, sm, sv, sg.shape[0])]

    landed_in[0] = _scatter_slabs((slab_in,))[0]
    sum_in = _sum_layer_slabs(landed_in, 512)
    sum_out = _sum_layer_slabs(landed_out, 256)
    sib_in, sib_out = _swap_with_sibling(sum_in, sum_out)

    big = {}
    flat = lambda a: a.reshape(-1, a.shape[-1])
    for n, grads, rows in (("w_in", [sum_in, sib_in], 512), ("w_out", [sum_out, sib_out], 256), ("w_mod", [flat(g_w_mod)], 512)):
        outs = _adamw(grads, flat(weights[n]), flat(m_in[n]), flat(v_in[n]), rows)
        big[n] = [o.reshape(weights[n].shape) for o in outs]

    def leaf(n, j):
        return big[n][j] if n in big else small_out[j][n]

    return (loss, grad_x, *[leaf(n, 0) for n in order], *[leaf(n, 1) for n in order],
            *[leaf(n, 2) for n in order], *[leaf(n, 3) for n in order])
```
